```python
import jax, jax.numpy as jnp
from jax import lax
import numpy as np

D_MODEL = 2048
BATCH = 4
SEQ = 2048
DEPTH = 1
DEC_BATCH = 8
DEC_SEQ = 4
PAST_LEN = 16384
PAGE_SIZE = 128

SSD_HEADS = 16
SSD_HEADDIM = 64
SSD_INNER = SSD_HEADS * SSD_HEADDIM
SSD_GROUPS = 2
SSD_STATE = 128
CONV_W = 4
CONV_DIM = SSD_INNER + 2 * SSD_GROUPS * SSD_STATE
SSD_CHUNK = 128
ATT_HEADS = 16
ATT_KV_HEADS = 4
HEAD_DIM = 64
ATT_GQ = ATT_HEADS // ATT_KV_HEADS
ATT_INNER = ATT_HEADS * HEAD_DIM
ROT_DIM = HEAD_DIM // 4
ROPE_THETA = 500000.0
DILATED_BRANCHES = ((128, 1), (512, 4), (2048, 16))
W_MAX = 2048
Q_BLOCK = 128
D_MIX = SSD_INNER + ATT_INNER
EPS = 1e-6
IN_SPLITS = (SSD_INNER,
             SSD_INNER + CONV_DIM,
             SSD_INNER + CONV_DIM + SSD_HEADS,
             SSD_INNER + CONV_DIM + SSD_HEADS + ATT_INNER,
             SSD_INNER + CONV_DIM + SSD_HEADS + ATT_INNER + ATT_KV_HEADS * HEAD_DIM,
             SSD_INNER + CONV_DIM + SSD_HEADS + ATT_INNER + 2 * ATT_KV_HEADS * HEAD_DIM)
IN_COLS = SSD_INNER + CONV_DIM + SSD_HEADS + ATT_INNER + 2 * ATT_KV_HEADS * HEAD_DIM + ATT_INNER

kernel_name = "hymba_ssd_dilated_swa_step"


def rmsnorm(x, w):
    xf = x.astype(jnp.float32)
    y = xf * lax.rsqrt(jnp.mean(xf * xf, axis=-1, keepdims=True) + EPS)
    return y * w.astype(jnp.float32)


def rope_partial(x, pos):
    half = ROT_DIM // 2
    inv = ROPE_THETA ** (-jnp.arange(0, ROT_DIM, 2, dtype=jnp.float32) / ROT_DIM)
    ang = pos.astype(jnp.float32)[:, None] * inv[None, :]
    cos = jnp.cos(ang)[None, :, None, :]
    sin = jnp.sin(ang)[None, :, None, :]
    x1 = x[..., :half]
    x2 = x[..., half:ROT_DIM]
    return jnp.concatenate([x1 * cos - x2 * sin, x2 * cos + x1 * sin, x[..., ROT_DIM:]], axis=-1)


def causal_conv(xbc, conv_prev, conv_w, conv_b):
    T = xbc.shape[1]
    xp = jnp.concatenate([conv_prev.astype(xbc.dtype), xbc], axis=1)
    y = sum(conv_w[k] * xp[:, k:k + T] for k in range(CONV_W)) + conv_b
    return jax.nn.silu(y), xp[:, T:]


def ssd_chunked(x, dt, a, bm, cm, h0, chunk):
    Bsz, T, H, P = x.shape
    G, N = bm.shape[2], bm.shape[3]
    R = H // G
    nc = T // chunk
    x = x.reshape(Bsz, nc, chunk, G, R, P)
    dt = dt.reshape(Bsz, nc, chunk, G, R)
    bm = bm.reshape(Bsz, nc, chunk, G, N)
    cm = cm.reshape(Bsz, nc, chunk, G, N)
    cum = jnp.cumsum(dt * a.reshape(G, R), axis=2)
    causal = jnp.tril(jnp.ones((chunk, chunk), dtype=bool))[:, :, None, None]
    seg = cum[:, :, :, None] - cum[:, :, None, :]
    decay = jnp.exp(jnp.where(causal, seg, -jnp.inf))
    cb = jnp.einsum('bclgn,bcsgn->bclsg', cm, bm)
    scores = cb[..., None] * decay * dt[:, :, None]
    y_diag = jnp.einsum('bclsgr,bcsgrp->bclgrp', scores, x)
    decay_end = jnp.exp(cum[:, :, -1:] - cum)
    st = jnp.einsum('bcsgn,bcsgrp->bcgrpn', bm, x * (decay_end * dt)[..., None])
    chunk_decay = jnp.exp(cum[:, :, -1])

    def step(h, inp):
        s_c, d_c = inp
        return h * d_c[..., None, None] + s_c, h

    h_final, h_prev = lax.scan(step, h0.astype(jnp.float32).reshape(Bsz, G, R, P, N),
                               (jnp.moveaxis(st, 1, 0), jnp.moveaxis(chunk_decay, 1, 0)))
    h_prev = jnp.moveaxis(h_prev, 0, 1)
    y_off = jnp.einsum('bclgn,bcgrpn->bclgrp', cm, h_prev) * jnp.exp(cum)[..., None]
    y = (y_diag + y_off).reshape(Bsz, T, H, P)
    return y, h_final.reshape(Bsz, H, P, N)


def dilated_window_attention(q, k_ext, v_ext, q_idx, lo):
    nums, dens, maxs = [], [], []
    for window, dil in DILATED_BRANCHES:
        offs = jnp.arange(window // dil + 1, dtype=jnp.int32) * dil
        idx = q_idx[:, None] - offs[None, :]
        valid = idx >= lo
        idx = jnp.maximum(idx, 0)
        kg = jnp.take(k_ext, idx, axis=1)
        vg = jnp.take(v_ext, idx, axis=1)
        s = jnp.einsum('bqhgd,bqjhd->bqhgj', q, kg).astype(jnp.float32)
        s = jnp.where(valid[None, :, None, None, :], s, -jnp.inf)
        m = jnp.max(s, axis=-1)
        p = jnp.exp(s - m[..., None])
        maxs.append(m)
        dens.append(jnp.sum(p, axis=-1))
        nums.append(jnp.einsum('bqhgj,bqjhd->bqhgd', p, vg.astype(jnp.float32)))
    mx = jnp.max(jnp.stack(maxs), axis=0)
    ws = [jnp.exp(m - mx) for m in maxs]
    num = sum(n * w[..., None] for n, w in zip(nums, ws))
    den = sum(d * w for d, w in zip(dens, ws))
    return num / den[..., None]


def prompt_attention(q, k, v):
    B, T = q.shape[0], q.shape[1]
    k_pad = jnp.pad(k, ((0, 0), (W_MAX, 0), (0, 0), (0, 0)))
    v_pad = jnp.pad(v, ((0, 0), (W_MAX, 0), (0, 0), (0, 0)))
    nb = T // Q_BLOCK
    q_blocks = jnp.moveaxis(q.reshape(B, nb, Q_BLOCK, ATT_KV_HEADS, ATT_GQ, HEAD_DIM), 1, 0)
    starts = jnp.arange(nb, dtype=jnp.int32) * Q_BLOCK

    def one_block(args):
        q_blk, start = args
        q_idx = W_MAX + start + jnp.arange(Q_BLOCK, dtype=jnp.int32)
        return dilated_window_attention(q_blk, k_pad, v_pad, q_idx, W_MAX)

    o = lax.map(one_block, (q_blocks, starts))
    o = jnp.moveaxis(o, 0, 1).reshape(B, T, ATT_KV_HEADS, ATT_GQ, HEAD_DIM)
    keep = min(W_MAX, T)
    return o, k[:, T - keep:], v[:, T - keep:]


def sample_attention(q, k, v, cache_k, cache_v):
    win = cache_k.shape[1]
    Ts = q.shape[1]
    k_ext = jnp.concatenate([cache_k.astype(k.dtype), k], axis=1)
    v_ext = jnp.concatenate([cache_v.astype(v.dtype), v], axis=1)
    q_idx = win + jnp.arange(Ts, dtype=jnp.int32)
    o = dilated_window_attention(q, k_ext, v_ext, q_idx, 0)
    return o, k_ext[:, Ts:], v_ext[:, Ts:]


def hybrid_layer(x, pos, conv_prev, h0, ssd_chunk, attend, norm_w, w_in, conv_w, conv_b,
                 dt_bias, a_log, d_skip, ssd_norm_w, q_norm_w, k_norm_w, w_out):
    B, T, _ = x.shape
    hn = rmsnorm(x, norm_w)
    proj = hn @ w_in.astype(jnp.float32)
    z, xbc, dt_raw, q, k, v, gate = jnp.split(proj, IN_SPLITS, axis=-1)
    xbc_c, conv_new = causal_conv(xbc, conv_prev, conv_w, conv_b)
    xs, bm, cm = jnp.split(xbc_c, [SSD_INNER, SSD_INNER + SSD_GROUPS * SSD_STATE], axis=-1)
    dt = jax.nn.softplus(dt_raw + dt_bias)
    a = -jnp.exp(a_log.astype(jnp.float32))
    xh = xs.reshape(B, T, SSD_HEADS, SSD_HEADDIM)
    y_ssd, h_new = ssd_chunked(xh, dt, a, bm.reshape(B, T, SSD_GROUPS, SSD_STATE),
                               cm.reshape(B, T, SSD_GROUPS, SSD_STATE), h0, ssd_chunk)
    y_ssd = (y_ssd + d_skip[:, None] * xh).reshape(B, T, SSD_INNER) * jax.nn.silu(z)
    y_ssd = rmsnorm(y_ssd.reshape(B, T, SSD_GROUPS, SSD_INNER // SSD_GROUPS),
                    ssd_norm_w.reshape(SSD_GROUPS, SSD_INNER // SSD_GROUPS)).reshape(B, T, SSD_INNER)
    q = rope_partial(rmsnorm(q.reshape(B, T, ATT_HEADS, HEAD_DIM), q_norm_w), pos) * (HEAD_DIM ** -0.5)
    k = rope_partial(rmsnorm(k.reshape(B, T, ATT_KV_HEADS, HEAD_DIM), k_norm_w), pos)
    v = v.reshape(B, T, ATT_KV_HEADS, HEAD_DIM)
    o, k_state, v_state = attend(q.reshape(B, T, ATT_KV_HEADS, ATT_GQ, HEAD_DIM), k, v)
    y_att = o.reshape(B, T, ATT_INNER) * jax.nn.silu(gate)
    out = jnp.concatenate([y_ssd, y_att], axis=-1) @ w_out.astype(jnp.float32)
    return (x + out).astype(x.dtype), conv_new, h_new, k_state, v_state


def setup_inputs(seed: int = 0) -> dict:
    key = jax.random.key(seed)
    ks = jax.random.split(key, 20)
    win = min(W_MAX, PAST_LEN)
    f32 = jnp.float32
    dt0 = jnp.exp(jax.random.uniform(ks[10], (DEPTH, SSD_HEADS), f32)
                  * (np.log(0.1) - np.log(0.001)) + np.log(0.001))
    return {
        "x_prompt": jax.random.normal(ks[0], (BATCH, SEQ, D_MODEL), f32),
        "x_sample": jax.random.normal(ks[1], (DEC_BATCH, DEC_SEQ, D_MODEL), f32),
        "cache_k": jax.random.normal(ks[2], (DEPTH, DEC_BATCH, win, ATT_KV_HEADS, HEAD_DIM), f32),
        "cache_v": jax.random.normal(ks[3], (DEPTH, DEC_BATCH, win, ATT_KV_HEADS, HEAD_DIM), f32),
        "state_conv": jax.random.normal(ks[4], (DEPTH, DEC_BATCH, CONV_W - 1, CONV_DIM), f32),
        "state_ssm": 0.5 * jax.random.normal(ks[5], (DEPTH, DEC_BATCH, SSD_HEADS, SSD_HEADDIM, SSD_STATE), f32),
        "norm_w": 1.0 + 0.01 * jax.random.normal(ks[6], (DEPTH, D_MODEL), f32),
        "w_in": jax.random.normal(ks[7], (DEPTH, D_MODEL, IN_COLS), f32) * D_MODEL ** -0.5,
        "conv_w": jax.random.normal(ks[8], (DEPTH, CONV_W, CONV_DIM), f32) * CONV_W ** -0.5,
        "conv_b": 0.01 * jax.random.normal(ks[9], (DEPTH, CONV_DIM), f32),
        "dt_bias": dt0 + jnp.log(-jnp.expm1(-dt0)),
        "a_log": jnp.log(jax.random.uniform(ks[11], (DEPTH, SSD_HEADS), f32, 1.0, 16.0)),
        "d_skip": 1.0 + 0.1 * jax.random.normal(ks[12], (DEPTH, SSD_HEADS), f32),
        "ssd_norm_w": 1.0 + 0.01 * jax.random.normal(ks[13], (DEPTH, SSD_INNER), f32),
        "q_norm_w": 1.0 + 0.01 * jax.random.normal(ks[14], (DEPTH, HEAD_DIM), f32),
        "k_norm_w": 1.0 + 0.01 * jax.random.normal(ks[15], (DEPTH, HEAD_DIM), f32),
        "w_out": jax.random.normal(ks[16], (DEPTH, D_MIX, D_MODEL), f32) * D_MIX ** -0.5,
    }


def reference(x_prompt, x_sample, cache_k, cache_v, state_conv, state_ssm, norm_w, w_in, conv_w,
              conv_b, dt_bias, a_log, d_skip, ssd_norm_w, q_norm_w, k_norm_w, w_out):
    Bp, Tp = x_prompt.shape[0], x_prompt.shape[1]
    Bs, Ts = x_sample.shape[0], x_sample.shape[1]
    pos_p = jnp.arange(Tp, dtype=jnp.int32)
    pos_s = PAST_LEN + jnp.arange(Ts, dtype=jnp.int32)
    yp, ys = x_prompt, x_sample
    kp, vp, cp, hp, kss, vss, css, hss = [], [], [], [], [], [], [], []
    for l in range(DEPTH):
        weights = (norm_w[l], w_in[l], conv_w[l], conv_b[l], dt_bias[l], a_log[l], d_skip[l],
                   ssd_norm_w[l], q_norm_w[l], k_norm_w[l], w_out[l])
        yp, c_new, h_new, k_new, v_new = hybrid_layer(
            yp, pos_p, jnp.zeros((Bp, CONV_W - 1, CONV_DIM), jnp.float32),
            jnp.zeros((Bp, SSD_HEADS, SSD_HEADDIM, SSD_STATE), jnp.float32),
            min(SSD_CHUNK, Tp), prompt_attention, *weights)
        kp.append(k_new); vp.append(v_new); cp.append(c_new); hp.append(h_new)
        ck, cv = cache_k[l], cache_v[l]
        ys, c_new, h_new, k_new, v_new = hybrid_layer(
            ys, pos_s, state_conv[l], state_ssm[l], Ts,
            lambda q, k, v: sample_attention(q, k, v, ck, cv), *weights)
        kss.append(k_new); vss.append(v_new); css.append(c_new); hss.append(h_new)
    return (yp, ys, jnp.stack(kp), jnp.stack(vp), jnp.stack(cp), jnp.stack(hp),
            jnp.stack(kss), jnp.stack(vss), jnp.stack(css), jnp.stack(hss))
```

```python
import functools

import jax
import jax.numpy as jnp
from jax import lax
from jax.experimental import pallas as pl
from jax.experimental.pallas import tpu as pltpu

F32 = jnp.float32
BF16 = jnp.bfloat16

D_MODEL = 2048
SSD_HEADS = 16
SSD_HEADDIM = 64
SSD_INNER = SSD_HEADS * SSD_HEADDIM
SSD_GROUPS = 2
SSD_STATE = 128
CONV_W = 4
CONV_DIM = SSD_INNER + 2 * SSD_GROUPS * SSD_STATE
SSD_CHUNK = 128
ATT_HEADS = 16
ATT_KV_HEADS = 4
HEAD_DIM = 64
ATT_GQ = ATT_HEADS // ATT_KV_HEADS
ATT_INNER = ATT_HEADS * HEAD_DIM
KV_DIM = ATT_KV_HEADS * HEAD_DIM
ROT_DIM = HEAD_DIM // 4
ROPE_THETA = 500000.0
DILATED_BRANCHES = ((128, 1), (512, 4), (2048, 16))
W_MAX = 2048
PAST_LEN = 16384
EPS = 1e-6

LANES = 128
SUBLANES = 8
Q_BLOCK = 128
SAMPLE_PAD = 16
NEG = -1e30
VMEM_LIMIT = 56 * 1024 * 1024

Z0 = 0
X0 = Z0 + SSD_INNER
DT0 = X0 + CONV_DIM
Q0 = DT0 + LANES
K0 = Q0 + ATT_INNER
V0 = K0 + KV_DIM
G0 = V0 + KV_DIM
W_COLS = G0 + ATT_INNER
HALF_INNER = SSD_INNER // SSD_GROUPS


def _dot(a, b):
    return jnp.dot(a, b, preferred_element_type=F32)


def _dot_nt(a, b):
    return lax.dot_general(a, b, (((1,), (1,)), ((), ())), preferred_element_type=F32)


def _split3(x):
    hi = x.astype(BF16)
    r1 = x - hi.astype(F32)
    mid = r1.astype(BF16)
    lo = (r1 - mid.astype(F32)).astype(BF16)
    return hi, mid, lo


def _dot_exact_rhs(x, m):
    hi, mid, lo = _split3(x)
    return _dot(hi, m) + _dot(mid, m) + _dot(lo, m)


def _dot_exact_lhs(m, x):
    hi, mid, lo = _split3(x)
    return _dot(m, hi) + _dot(m, mid) + _dot(m, lo)


def _silu(x):
    return x * jax.nn.sigmoid(x)


def _multiplicity(d):
    w = jnp.zeros(d.shape, F32)
    for window, dil in DILATED_BRANCHES:
        hit = (d >= 0) & (d <= window) & (lax.rem(d, dil) == 0)
        w = w + jnp.where(hit, 1.0, 0.0)
    return w


def _norm_rope(y, nw, cos, sa, sb, scale):
    lane = lax.broadcasted_iota(jnp.int32, (1, LANES), 1)
    first = lane < HEAD_DIM
    y2 = y * y
    s_lo = jnp.sum(jnp.where(first, y2, 0.0), axis=-1, keepdims=True)
    s_hi = jnp.sum(jnp.where(first, 0.0, y2), axis=-1, keepdims=True)
    ms = jnp.where(first, s_lo, s_hi) * (1.0 / HEAD_DIM)
    yn = y * lax.rsqrt(ms + EPS) * nw
    half = ROT_DIM // 2
    rot = yn * cos + pltpu.roll(yn, half, 1) * sa + pltpu.roll(yn, LANES - half, 1) * sb
    return rot * scale


def _inproj_kernel(x_ref, nw_ref, w_ref, qnw_ref, knw_ref, cos_ref, sa_ref, sb_ref,
                   z_ref, xbc_ref, dt_ref, q_ref, k_ref, v_ref, g_ref):
    x = x_ref[...]
    ms = jnp.mean(x * x, axis=-1, keepdims=True)
    hn = (x * lax.rsqrt(ms + EPS) * nw_ref[...]).astype(BF16)

    def proj(a, b):
        return _dot(hn, w_ref[:, a:b])

    z_ref[...] = proj(Z0, X0)
    xbc_ref[...] = proj(X0, DT0)
    dt_ref[...] = proj(DT0, Q0)
    v_ref[...] = proj(V0, G0)
    g_ref[...] = proj(G0, W_COLS)
    cos, sa, sb = cos_ref[...], sa_ref[...], sb_ref[...]
    q = proj(Q0, K0)
    for c in range(ATT_INNER // LANES):
        sl = slice(c * LANES, (c + 1) * LANES)
        q_ref[:, sl] = _norm_rope(q[:, sl], qnw_ref[...], cos, sa, sb, HEAD_DIM ** -0.5).astype(BF16)
    k = proj(K0, V0)
    for c in range(KV_DIM // LANES):
        sl = slice(c * LANES, (c + 1) * LANES)
        k_ref[:, sl] = _norm_rope(k[:, sl], knw_ref[...], cos, sa, sb, 1.0)


def _in_proj(x2d, nw, w_r, qnw, knw, tables, tm):
    rows = x2d.shape[0]
    period = tables[0].shape[0] // tm
    const = lambda i: (0, 0)
    row = lambda i: (i, 0)
    tab = lambda i: (i % period, 0)
    outs = [(SSD_INNER, F32), (CONV_DIM, F32), (LANES, F32), (ATT_INNER, BF16),
            (KV_DIM, F32), (KV_DIM, F32), (ATT_INNER, F32)]
    return pl.pallas_call(
        _inproj_kernel,
        grid=(rows // tm,),
        in_specs=[
            pl.BlockSpec((tm, D_MODEL), row),
            pl.BlockSpec((1, D_MODEL), const),
            pl.BlockSpec((D_MODEL, W_COLS), const, pipeline_mode=pl.Buffered(1)),
            pl.BlockSpec((1, LANES), const),
            pl.BlockSpec((1, LANES), const),
            pl.BlockSpec((tm, LANES), tab),
            pl.BlockSpec((tm, LANES), tab),
            pl.BlockSpec((tm, LANES), tab),
        ],
        out_specs=[pl.BlockSpec((tm, n), row) for n, _ in outs],
        out_shape=[jax.ShapeDtypeStruct((rows, n), dt) for n, dt in outs],
        compiler_params=pltpu.CompilerParams(
            dimension_semantics=("arbitrary",), vmem_limit_bytes=VMEM_LIMIT),
        name="in_proj",
    )(x2d, nw, w_r, qnw, knw, *tables)


def _ssd_kernel(xbc_ref, dtr_ref, z_ref, cprev_ref, h0_ref, cw_ref, cb_ref, dtb_ref, alog_ref,
                dexp_ref, nw_ref, y_ref, hout_ref, xext, ht, *, lin, valid):
    L = SSD_CHUNK
    c = pl.program_id(1)
    last = pl.num_programs(1) - 1

    @pl.when(c == 0)
    def _():
        xext[0:SUBLANES, :] = cprev_ref[0]
        for g in range(SSD_GROUPS):
            ht[g] = h0_ref[0, g * HALF_INNER:(g + 1) * HALF_INNER, :].T

    def pad_rows(v):
        if lin == L:
            return v
        return jnp.concatenate([v, jnp.zeros((L - lin, v.shape[1]), v.dtype)], axis=0)

    if lin < L:
        xext[SUBLANES + lin:SUBLANES + L, :] = jnp.zeros((L - lin, CONV_DIM), F32)
    xext[SUBLANES:SUBLANES + lin, :] = xbc_ref[...]
    acc = cb_ref[...]
    for j in range(CONV_W):
        r0 = SUBLANES - j
        acc = acc + cw_ref[CONV_W - 1 - j:CONV_W - j, :] * xext[r0:r0 + L, :]
    xext[0:SUBLANES, :] = xext[L:L + SUBLANES, :]
    xc = _silu(acc)
    xs = xc[:, :SSD_INNER]
    bm = xc[:, SSD_INNER:SSD_INNER + SSD_GROUPS * SSD_STATE]
    cm = xc[:, SSD_INNER + SSD_GROUPS * SSD_STATE:]

    lane = lax.broadcasted_iota(jnp.int32, (1, LANES), 1)
    rowid = lax.broadcasted_iota(jnp.int32, (L, 1), 0)
    dt = jax.nn.softplus(pad_rows(dtr_ref[...]) + dtb_ref[...])
    dt = jnp.where((lane < SSD_HEADS) & (rowid < valid), dt, 0.0)
    a = -jnp.exp(alog_ref[...])
    dta = dt * a

    r2 = lax.broadcasted_iota(jnp.int32, (L, L), 0)
    c2 = lax.broadcasted_iota(jnp.int32, (L, L), 1)
    tri = r2 >= c2
    tri_b = jnp.where(tri, 1.0, 0.0).astype(BF16)
    cum = _dot_exact_lhs(tri_b, dta)
    cum_t = cum.T
    cum_last = cum[L - 1:L, :]

    er = lax.broadcasted_iota(jnp.int32, (LANES, SSD_INNER), 0)
    ec = lax.broadcasted_iota(jnp.int32, (LANES, SSD_INNER), 1)
    expand = jnp.where(ec // SSD_HEADDIM == er, 1.0, 0.0).astype(BF16)

    xdt = (xs * _dot_exact_rhs(dt, expand)).astype(BF16)
    lane_half = lane // HEAD_DIM
    heads_per_group = SSD_HEADS // SSD_GROUPS
    y_parts = []
    for g in range(SSD_GROUPS):
        bm_g = bm[:, g * SSD_STATE:(g + 1) * SSD_STATE]
        cm_g = cm[:, g * SSD_STATE:(g + 1) * SSD_STATE]
        cb = _dot_nt(cm_g.astype(BF16), bm_g.astype(BF16))
        for pr in range(heads_per_group // 2):
            col0 = g * HALF_INNER + pr * LANES
            xp = xdt[:, col0:col0 + LANES]
            yp = jnp.zeros((L, LANES), F32)
            for e in range(2):
                hh = g * heads_per_group + pr * 2 + e
                seg = cum[:, hh:hh + 1] - cum_t[hh:hh + 1, :]
                sc = jnp.where(tri, cb * jnp.exp(jnp.minimum(seg, 0.0)), 0.0)
                xm = jnp.where(lane_half == e, xp, jnp.zeros_like(xp))
                yp = yp + _dot(sc.astype(BF16), xm)
            y_parts.append(yp)
    y_diag = jnp.concatenate(y_parts, axis=1)

    ecum = _dot_exact_rhs(jnp.exp(cum), expand)
    y_off = jnp.concatenate(
        [_dot(cm[:, g * SSD_STATE:(g + 1) * SSD_STATE].astype(BF16), ht[g].astype(BF16))
         for g in range(SSD_GROUPS)], axis=1) * ecum

    wgt = jnp.exp(cum_last - cum) * dt
    xw = (xs * _dot_exact_rhs(wgt, expand)).astype(BF16)
    cd = _dot_exact_rhs(jnp.broadcast_to(jnp.exp(cum_last), (SUBLANES, LANES)), expand)[0:1, :]
    for g in range(SSD_GROUPS):
        sl = slice(g * HALF_INNER, (g + 1) * HALF_INNER)
        bm_t = bm[:, g * SSD_STATE:(g + 1) * SSD_STATE].T.astype(BF16)
        ht[g] = ht[g] * cd[:, sl] + _dot(bm_t, xw[:, sl])

    y = (y_diag + y_off + dexp_ref[...] * xs) * _silu(pad_rows(z_ref[...]))
    yn = []
    for g in range(SSD_GROUPS):
        yg = y[:, g * HALF_INNER:(g + 1) * HALF_INNER]
        ms = jnp.mean(yg * yg, axis=-1, keepdims=True)
        yn.append(yg * lax.rsqrt(ms + EPS))
    yn = jnp.concatenate(yn, axis=1) * nw_ref[...]
    y_ref[...] = yn[:lin].astype(BF16)

    @pl.when(c == last)
    def _():
        for g in range(SSD_GROUPS):
            hout_ref[0, g * HALF_INNER:(g + 1) * HALF_INNER, :] = ht[g].T


def _ssd(xbc, dtr, z, cprev, h0, cw, cb, dtb, alog, dexp, nw, batch, lin, valid):
    rows = xbc.shape[0]
    nc = rows // (batch * lin)
    blk = lambda b, c: (b * nc + c, 0)
    per_b = lambda b, c: (b, 0, 0)
    const = lambda b, c: (0, 0)
    return pl.pallas_call(
        functools.partial(_ssd_kernel, lin=lin, valid=valid),
        grid=(batch, nc),
        in_specs=[
            pl.BlockSpec((lin, CONV_DIM), blk),
            pl.BlockSpec((lin, LANES), blk),
            pl.BlockSpec((lin, SSD_INNER), blk),
            pl.BlockSpec((1, SUBLANES, CONV_DIM), per_b),
            pl.BlockSpec((1, SSD_INNER, SSD_STATE), per_b),
            pl.BlockSpec((SUBLANES, CONV_DIM), const),
            pl.BlockSpec((1, CONV_DIM), const),
            pl.BlockSpec((1, LANES), const),
            pl.BlockSpec((1, LANES), const),
            pl.BlockSpec((1, SSD_INNER), const),
            pl.BlockSpec((1, SSD_INNER), const),
        ],
        out_specs=[
            pl.BlockSpec((lin, SSD_INNER), blk),
            pl.BlockSpec((1, SSD_INNER, SSD_STATE), per_b),
        ],
        out_shape=[
            jax.ShapeDtypeStruct((rows, SSD_INNER), BF16),
            jax.ShapeDtypeStruct((batch, SSD_INNER, SSD_STATE), F32),
        ],
        scratch_shapes=[
            pltpu.VMEM((SUBLANES + SSD_CHUNK + SUBLANES, CONV_DIM), F32),
            pltpu.VMEM((SSD_GROUPS, SSD_STATE, HALF_INNER), F32),
        ],
        compiler_params=pltpu.CompilerParams(
            dimension_semantics=("arbitrary", "arbitrary"), vmem_limit_bytes=VMEM_LIMIT),
        name="ssd",
    )(xbc, dtr, z, cprev, h0, cw, cb, dtb, alog, dexp, nw)


def _attn_prompt_kernel(q_ref, k_ref, v_ref, g_ref, o_ref):
    i = pl.program_id(1)
    tq = Q_BLOCK
    rows = ATT_GQ * tq
    lo = jnp.maximum(i - W_MAX // tq, 0)
    rpos = lax.broadcasted_iota(jnp.int32, (tq, tq), 0)
    cpos = lax.broadcasted_iota(jnp.int32, (tq, tq), 1)
    rel = rpos - cpos
    for kvh in range(ATT_KV_HEADS):
        c0 = kvh * ATT_GQ * HEAD_DIM
        qh = jnp.concatenate(
            [q_ref[:, c0 + g * HEAD_DIM:c0 + (g + 1) * HEAD_DIM] for g in range(ATT_GQ)], axis=0)
        ksl = slice(kvh * HEAD_DIM, (kvh + 1) * HEAD_DIM)

        def body(j, carry):
            m, l, acc = carry
            start = pl.multiple_of(j * tq, tq)
            kb = k_ref[pl.ds(start, tq), ksl].astype(BF16)
            vb = v_ref[pl.ds(start, tq), ksl].astype(BF16)
            s = _dot_nt(qh, kb)
            w1 = _multiplicity(rel + (i - j) * tq)
            w = jnp.concatenate([w1] * ATT_GQ, axis=0)
            sm = jnp.where(w > 0.0, s, NEG)
            m_new = jnp.maximum(m, jnp.max(sm, axis=-1, keepdims=True))
            alpha = jnp.exp(m - m_new)
            p = jnp.exp(sm - m_new) * w
            l = alpha * l + jnp.sum(p, axis=-1, keepdims=True)
            acc = alpha * acc + _dot(p.astype(BF16), vb)
            return m_new, l, acc

        init = (jnp.full((rows, 1), NEG, F32), jnp.zeros((rows, 1), F32),
                jnp.zeros((rows, HEAD_DIM), F32))
        m, l, acc = lax.fori_loop(lo, i + 1, body, init)
        o = acc / l
        o = jnp.concatenate([o[g * tq:(g + 1) * tq] for g in range(ATT_GQ)], axis=1)
        csl = slice(c0, c0 + ATT_GQ * HEAD_DIM)
        o_ref[:, csl] = (o * _silu(g_ref[:, csl])).astype(BF16)


def _attn_prompt(q, k, v, gate, batch, seq):
    rows = q.shape[0]
    nqb = seq // Q_BLOCK
    qblk = lambda b, i: (b * nqb + i, 0)
    per_b = lambda b, i: (b, 0)
    return pl.pallas_call(
        _attn_prompt_kernel,
        grid=(batch, nqb),
        in_specs=[
            pl.BlockSpec((Q_BLOCK, ATT_INNER), qblk),
            pl.BlockSpec((seq, KV_DIM), per_b),
            pl.BlockSpec((seq, KV_DIM), per_b),
            pl.BlockSpec((Q_BLOCK, ATT_INNER), qblk),
        ],
        out_specs=pl.BlockSpec((Q_BLOCK, ATT_INNER), qblk),
        out_shape=jax.ShapeDtypeStruct((rows, ATT_INNER), BF16),
        compiler_params=pltpu.CompilerParams(
            dimension_semantics=("arbitrary", "arbitrary"), vmem_limit_bytes=VMEM_LIMIT),
        name="attn_prompt",
    )(q, k, v, gate)


def _attn_sample_kernel(q_ref, kn_ref, vn_ref, g_ref, ck_ref, cv_ref, o_ref, ko_ref, vo_ref, *, ts):
    win = ck_ref.shape[1]
    tp = SAMPLE_PAD
    rows = ATT_GQ * tp
    t_c = lax.broadcasted_iota(jnp.int32, (rows, win), 0) % tp
    j_c = lax.broadcasted_iota(jnp.int32, (rows, win), 1)
    w_c = _multiplicity(win + t_c - j_c)
    t_n = lax.broadcasted_iota(jnp.int32, (rows, tp), 0) % tp
    j_n = lax.broadcasted_iota(jnp.int32, (rows, tp), 1)
    w_n = jnp.where(j_n < ts, _multiplicity(t_n - j_n), 0.0)
    for kvh in range(ATT_KV_HEADS):
        c0 = kvh * ATT_GQ * HEAD_DIM
        qh = jnp.concatenate(
            [q_ref[:, c0 + g * HEAD_DIM:c0 + (g + 1) * HEAD_DIM] for g in range(ATT_GQ)], axis=0)
        ksl = slice(kvh * HEAD_DIM, (kvh + 1) * HEAD_DIM)
        s_c = jnp.where(w_c > 0.0, _dot_nt(qh, ck_ref[0, :, ksl].astype(BF16)), NEG)
        s_n = jnp.where(w_n > 0.0, _dot_nt(qh, kn_ref[:, ksl].astype(BF16)), NEG)
        m = jnp.maximum(jnp.max(s_c, axis=-1, keepdims=True), jnp.max(s_n, axis=-1, keepdims=True))
        p_c = jnp.exp(s_c - m) * w_c
        p_n = jnp.exp(s_n - m) * w_n
        den = jnp.sum(p_c, axis=-1, keepdims=True) + jnp.sum(p_n, axis=-1, keepdims=True)
        num = (_dot(p_c.astype(BF16), cv_ref[0, :, ksl].astype(BF16))
               + _dot(p_n.astype(BF16), vn_ref[:, ksl].astype(BF16)))
        o = num / den
        o = jnp.concatenate([o[g * tp:(g + 1) * tp] for g in range(ATT_GQ)], axis=1)
        csl = slice(c0, c0 + ATT_GQ * HEAD_DIM)
        o_ref[:, csl] = (o * _silu(g_ref[:, csl])).astype(BF16)
    ko_ref[0, 0:win - ts, :] = ck_ref[0, ts:win, :]
    ko_ref[0, win - ts:win, :] = kn_ref[0:ts, :]
    vo_ref[0, 0:win - ts, :] = cv_ref[0, ts:win, :]
    vo_ref[0, win - ts:win, :] = vn_ref[0:ts, :]


def _attn_sample(q, kn, vn, gate, ck, cv, ts):
    batch, win, _ = ck.shape
    tp = SAMPLE_PAD
    blk = lambda b: (b, 0)
    per_b = lambda b: (b, 0, 0)
    return pl.pallas_call(
        functools.partial(_attn_sample_kernel, ts=ts),
        grid=(batch,),
        in_specs=[
            pl.BlockSpec((tp, ATT_INNER), blk),
            pl.BlockSpec((tp, KV_DIM), blk),
            pl.BlockSpec((tp, KV_DIM), blk),
            pl.BlockSpec((tp, ATT_INNER), blk),
            pl.BlockSpec((1, win, KV_DIM), per_b),
            pl.BlockSpec((1, win, KV_DIM), per_b),
        ],
        out_specs=[
            pl.BlockSpec((tp, ATT_INNER), blk),
            pl.BlockSpec((1, win, KV_DIM), per_b),
            pl.BlockSpec((1, win, KV_DIM), per_b),
        ],
        out_shape=[
            jax.ShapeDtypeStruct((batch * tp, ATT_INNER), BF16),
            jax.ShapeDtypeStruct((batch, win, KV_DIM), F32),
            jax.ShapeDtypeStruct((batch, win, KV_DIM), F32),
        ],
        compiler_params=pltpu.CompilerParams(
            dimension_semantics=("arbitrary",), vmem_limit_bytes=VMEM_LIMIT),
        name="attn_sample",
    )(q, kn, vn, gate, ck, cv)


def _outproj_kernel(x_ref, ys_ref, ya_ref, w_ref, o_ref):
    o_ref[...] = (x_ref[...] + _dot(ys_ref[...], w_ref[0:SSD_INNER, :])
                  + _dot(ya_ref[...], w_ref[SSD_INNER:, :]))


def _out_proj(x2d, y_ssd, y_att, w_o, tm):
    rows = x2d.shape[0]
    row = lambda i: (i, 0)
    const = lambda i: (0, 0)
    return pl.pallas_call(
        _outproj_kernel,
        grid=(rows // tm,),
        in_specs=[
            pl.BlockSpec((tm, D_MODEL), row),
            pl.BlockSpec((tm, SSD_INNER), row),
            pl.BlockSpec((tm, ATT_INNER), row),
            pl.BlockSpec((SSD_INNER + ATT_INNER, D_MODEL), const, pipeline_mode=pl.Buffered(1)),
        ],
        out_specs=pl.BlockSpec((tm, D_MODEL), row),
        out_shape=jax.ShapeDtypeStruct((rows, D_MODEL), F32),
        compiler_params=pltpu.CompilerParams(
            dimension_semantics=("arbitrary",), vmem_limit_bytes=VMEM_LIMIT),
        name="out_proj",
    )(x2d, y_ssd, y_att, w_o)


def _rope_tables(pos):
    n = pos.shape[0]
    half = ROT_DIM // 2
    inv = ROPE_THETA ** (-jnp.arange(0, ROT_DIM, 2, dtype=F32) / ROT_DIM)
    ang = pos.astype(F32)[:, None] * inv[None, :]
    cos, sin = jnp.cos(ang), jnp.sin(ang)
    rest = HEAD_DIM - ROT_DIM
    zh = jnp.zeros((n, half), F32)
    cos_h = jnp.concatenate([cos, cos, jnp.ones((n, rest), F32)], axis=1)
    sa_h = jnp.concatenate([zh, sin, jnp.zeros((n, rest), F32)], axis=1)
    sb_h = jnp.concatenate([-sin, zh, jnp.zeros((n, rest), F32)], axis=1)
    rep = LANES // HEAD_DIM
    return tuple(jnp.tile(t, (1, rep)) for t in (cos_h, sa_h, sb_h))


def _lane_pad(v, n=LANES):
    return jnp.pad(v, (0, n - v.shape[0])).reshape(1, n)


def kernel(x_prompt, x_sample, cache_k, cache_v, state_conv, state_ssm, norm_w, w_in, conv_w,
           conv_b, dt_bias, a_log, d_skip, ssd_norm_w, q_norm_w, k_norm_w, w_out):
    bp, tp_, _ = x_prompt.shape
    bs, ts, _ = x_sample.shape
    depth = w_in.shape[0]
    assert depth == 1 and tp_ % SSD_CHUNK == 0 and ts <= SAMPLE_PAD and ts >= CONV_W - 1
    l = 0
    win = cache_k.shape[2]

    wl = w_in[l]
    dt_cols = wl[:, SSD_INNER + CONV_DIM:SSD_INNER + CONV_DIM + SSD_HEADS]
    w_r = jnp.concatenate(
        [wl[:, :SSD_INNER + CONV_DIM], jnp.pad(dt_cols, ((0, 0), (0, LANES - SSD_HEADS))),
         wl[:, SSD_INNER + CONV_DIM + SSD_HEADS:]], axis=1).astype(BF16)
    w_o = w_out[l].astype(BF16)
    nw = norm_w[l].reshape(1, D_MODEL)
    rep = LANES // HEAD_DIM
    qnw = jnp.tile(q_norm_w[l], rep).reshape(1, LANES)
    knw = jnp.tile(k_norm_w[l], rep).reshape(1, LANES)
    cw = jnp.pad(conv_w[l], ((0, SUBLANES - CONV_W), (0, 0)))
    cb = conv_b[l].reshape(1, CONV_DIM)
    dtb = _lane_pad(dt_bias[l])
    alog = _lane_pad(a_log[l])
    dexp = jnp.repeat(d_skip[l], SSD_HEADDIM).reshape(1, SSD_INNER)
    snw = ssd_norm_w[l].reshape(1, SSD_INNER)

    tm = 256
    xp2 = x_prompt.reshape(bp * tp_, D_MODEL)
    tabs = _rope_tables(jnp.arange(tp_, dtype=jnp.int32))
    z, xbc, dtr, q, k, v, gate = _in_proj(xp2, nw, w_r, qnw, knw, tabs, tm)
    y_ssd, h_p = _ssd(xbc, dtr, z, jnp.zeros((bp, SUBLANES, CONV_DIM), F32),
                      jnp.zeros((bp, SSD_INNER, SSD_STATE), F32), cw, cb, dtb, alog, dexp, snw,
                      bp, SSD_CHUNK, SSD_CHUNK)
    y_att = _attn_prompt(q, k, v, gate, bp, tp_)
    y_p = _out_proj(xp2, y_ssd, y_att, w_o, tm).reshape(bp, tp_, D_MODEL)
    keep = min(W_MAX, tp_)
    k_p = k.reshape(bp, tp_, ATT_KV_HEADS, HEAD_DIM)[:, tp_ - keep:][None]
    v_p = v.reshape(bp, tp_, ATT_KV_HEADS, HEAD_DIM)[:, tp_ - keep:][None]
    c_p = xbc.reshape(bp, tp_, CONV_DIM)[:, tp_ - (CONV_W - 1):][None]
    h_p = h_p.reshape(1, bp, SSD_HEADS, SSD_HEADDIM, SSD_STATE)

    pad = SAMPLE_PAD
    xs2 = jnp.pad(x_sample, ((0, 0), (0, pad - ts), (0, 0))).reshape(bs * pad, D_MODEL)
    pos_s = PAST_LEN + (jnp.arange(bs * pad, dtype=jnp.int32) % pad)
    tabs_s = _rope_tables(pos_s)
    z, xbc, dtr, q, k, v, gate = _in_proj(xs2, nw, w_r, qnw, knw, tabs_s, bs * pad)
    cprev = jnp.pad(state_conv[l], ((0, 0), (SUBLANES - (CONV_W - 1), 0), (0, 0)))
    y_ssd, h_s = _ssd(xbc, dtr, z, cprev, state_ssm[l].reshape(bs, SSD_INNER, SSD_STATE),
                      cw, cb, dtb, alog, dexp, snw, bs, pad, ts)
    y_att, k_s, v_s = _attn_sample(q, k, v, gate, cache_k[l].reshape(bs, win, KV_DIM),
                                   cache_v[l].reshape(bs, win, KV_DIM), ts)
    y_s = _out_proj(xs2, y_ssd, y_att, w_o, bs * pad).reshape(bs, pad, D_MODEL)[:, :ts]
    k_s = k_s.reshape(1, bs, win, ATT_KV_HEADS, HEAD_DIM)
    v_s = v_s.reshape(1, bs, win, ATT_KV_HEADS, HEAD_DIM)
    c_s = xbc.reshape(bs, pad, CONV_DIM)[:, ts - (CONV_W - 1):ts][None]
    h_s = h_s.reshape(1, bs, SSD_HEADS, SSD_HEADDIM, SSD_STATE)

    return (y_p, y_s, k_p, v_p, c_p, h_p, k_s, v_s, c_s, h_s)
```

```python
import functools

import jax
import jax.numpy as jnp
from jax import lax
from jax.experimental import pallas as pl
from jax.experimental.pallas import tpu as pltpu

F32 = jnp.float32
BF16 = jnp.bfloat16

D_MODEL = 2048
SSD_HEADS = 16
SSD_HEADDIM = 64
SSD_INNER = SSD_HEADS * SSD_HEADDIM
SSD_GROUPS = 2
SSD_STATE = 128
CONV_W = 4
CONV_DIM = SSD_INNER + 2 * SSD_GROUPS * SSD_STATE
SSD_CHUNK = 128
ATT_HEADS = 16
ATT_KV_HEADS = 4
HEAD_DIM = 64
ATT_GQ = ATT_HEADS // ATT_KV_HEADS
ATT_INNER = ATT_HEADS * HEAD_DIM
KV_DIM = ATT_KV_HEADS * HEAD_DIM
ROT_DIM = HEAD_DIM // 4
ROPE_THETA = 500000.0
DILATED_BRANCHES = ((128, 1), (512, 4), (2048, 16))
W_MAX = 2048
PAST_LEN = 16384
EPS = 1e-6

LANES = 128
SUBLANES = 8
Q_BLOCK = 128
K_SUPER = 256
ONES_ROWS = 16
ACC_ROWS = HEAD_DIM + ONES_ROWS
SAMPLE_PAD = 16
NEG = -1e30
VMEM_LIMIT = 56 * 1024 * 1024

Z0 = 0
X0 = Z0 + SSD_INNER
DT0 = X0 + CONV_DIM
Q0 = DT0 + LANES
K0 = Q0 + ATT_INNER
V0 = K0 + KV_DIM
G0 = V0 + KV_DIM
W_COLS = G0 + ATT_INNER
HALF_INNER = SSD_INNER // SSD_GROUPS


def _dot(a, b):
    return jnp.dot(a, b, preferred_element_type=F32)


def _dot_nt(a, b):
    return lax.dot_general(a, b, (((1,), (1,)), ((), ())), preferred_element_type=F32)


def _split3(x):
    hi = x.astype(BF16)
    r1 = x - hi.astype(F32)
    mid = r1.astype(BF16)
    lo = (r1 - mid.astype(F32)).astype(BF16)
    return hi, mid, lo


def _dot_exact_rhs(x, m):
    hi, mid, lo = _split3(x)
    return _dot(hi, m) + _dot(mid, m) + _dot(lo, m)


def _dot_exact_lhs(m, x):
    hi, mid, lo = _split3(x)
    return _dot(m, hi) + _dot(m, mid) + _dot(m, lo)


def _silu(x):
    return x * jax.nn.sigmoid(x)


def _multiplicity(d):
    w = jnp.zeros(d.shape, F32)
    for window, dil in DILATED_BRANCHES:
        hit = (d >= 0) & (d <= window) & (lax.rem(d, dil) == 0)
        w = w + jnp.where(hit, 1.0, 0.0)
    return w


def _norm_rope(y, nw, cos, sa, sb, scale):
    lane = lax.broadcasted_iota(jnp.int32, (1, LANES), 1)
    first = lane < HEAD_DIM
    y2 = y * y
    s_lo = jnp.sum(jnp.where(first, y2, 0.0), axis=-1, keepdims=True)
    s_hi = jnp.sum(jnp.where(first, 0.0, y2), axis=-1, keepdims=True)
    ms = jnp.where(first, s_lo, s_hi) * (1.0 / HEAD_DIM)
    yn = y * lax.rsqrt(ms + EPS) * nw
    half = ROT_DIM // 2
    rot = yn * cos + pltpu.roll(yn, half, 1) * sa + pltpu.roll(yn, LANES - half, 1) * sb
    return rot * scale


def _inproj_kernel(x_ref, nw_ref, w_ref, qnw_ref, knw_ref, cos_ref, sa_ref, sb_ref,
                   z_ref, xbc_ref, dt_ref, q_ref, k_ref, v_ref, g_ref):
    x = x_ref[...]
    ms = jnp.mean(x * x, axis=-1, keepdims=True)
    hn = (x * lax.rsqrt(ms + EPS) * nw_ref[...]).astype(BF16)

    def proj(a, b):
        return _dot(hn, w_ref[:, a:b])

    z_ref[...] = proj(Z0, X0)
    xbc_ref[...] = proj(X0, DT0)
    dt_ref[...] = proj(DT0, Q0)
    v_ref[...] = proj(V0, G0)
    g_ref[...] = proj(G0, W_COLS)
    cos, sa, sb = cos_ref[...], sa_ref[...], sb_ref[...]
    q = proj(Q0, K0)
    for c in range(ATT_INNER // LANES):
        sl = slice(c * LANES, (c + 1) * LANES)
        q_ref[:, sl] = _norm_rope(q[:, sl], qnw_ref[...], cos, sa, sb, HEAD_DIM ** -0.5).astype(BF16)
    k = proj(K0, V0)
    for c in range(KV_DIM // LANES):
        sl = slice(c * LANES, (c + 1) * LANES)
        k_ref[:, sl] = _norm_rope(k[:, sl], knw_ref[...], cos, sa, sb, 1.0)


def _in_proj(x2d, nw, w_r, qnw, knw, tables, tm):
    rows = x2d.shape[0]
    period = tables[0].shape[0] // tm
    const = lambda i: (0, 0)
    row = lambda i: (i, 0)
    tab = lambda i: (i % period, 0)
    outs = [(SSD_INNER, F32), (CONV_DIM, F32), (LANES, F32), (ATT_INNER, BF16),
            (KV_DIM, F32), (KV_DIM, F32), (ATT_INNER, F32)]
    return pl.pallas_call(
        _inproj_kernel,
        grid=(rows // tm,),
        in_specs=[
            pl.BlockSpec((tm, D_MODEL), row),
            pl.BlockSpec((1, D_MODEL), const),
            pl.BlockSpec((D_MODEL, W_COLS), const, pipeline_mode=pl.Buffered(1)),
            pl.BlockSpec((1, LANES), const),
            pl.BlockSpec((1, LANES), const),
            pl.BlockSpec((tm, LANES), tab),
            pl.BlockSpec((tm, LANES), tab),
            pl.BlockSpec((tm, LANES), tab),
        ],
        out_specs=[pl.BlockSpec((tm, n), row) for n, _ in outs],
        out_shape=[jax.ShapeDtypeStruct((rows, n), dt) for n, dt in outs],
        compiler_params=pltpu.CompilerParams(
            dimension_semantics=("arbitrary",), vmem_limit_bytes=VMEM_LIMIT),
        name="in_proj",
    )(x2d, nw, w_r, qnw, knw, *tables)


def _ssd_kernel(xbc_ref, dtr_ref, z_ref, cprev_ref, h0_ref, cw_ref, cb_ref, dtb_ref, alog_ref,
                dexp_ref, nw_ref, y_ref, hout_ref, xext, ht, *, lin, valid):
    L = SSD_CHUNK
    c = pl.program_id(1)
    last = pl.num_programs(1) - 1

    @pl.when(c == 0)
    def _():
        xext[0:SUBLANES, :] = cprev_ref[0]
        for g in range(SSD_GROUPS):
            ht[g] = h0_ref[0, g * HALF_INNER:(g + 1) * HALF_INNER, :].T

    def pad_rows(v):
        if lin == L:
            return v
        return jnp.concatenate([v, jnp.zeros((L - lin, v.shape[1]), v.dtype)], axis=0)

    if lin < L:
        xext[SUBLANES + lin:SUBLANES + L, :] = jnp.zeros((L - lin, CONV_DIM), F32)
    xext[SUBLANES:SUBLANES + lin, :] = xbc_ref[...]
    acc = cb_ref[...]
    for j in range(CONV_W):
        r0 = SUBLANES - j
        acc = acc + cw_ref[CONV_W - 1 - j:CONV_W - j, :] * xext[r0:r0 + L, :]
    xext[0:SUBLANES, :] = xext[L:L + SUBLANES, :]
    xc = _silu(acc)
    xs = xc[:, :SSD_INNER]
    bm = xc[:, SSD_INNER:SSD_INNER + SSD_GROUPS * SSD_STATE]
    cm = xc[:, SSD_INNER + SSD_GROUPS * SSD_STATE:]

    lane = lax.broadcasted_iota(jnp.int32, (1, LANES), 1)
    rowid = lax.broadcasted_iota(jnp.int32, (L, 1), 0)
    dt = jax.nn.softplus(pad_rows(dtr_ref[...]) + dtb_ref[...])
    dt = jnp.where((lane < SSD_HEADS) & (rowid < valid), dt, 0.0)
    a = -jnp.exp(alog_ref[...])
    dta = dt * a

    r2 = lax.broadcasted_iota(jnp.int32, (L, L), 0)
    c2 = lax.broadcasted_iota(jnp.int32, (L, L), 1)
    tri = r2 >= c2
    tri_b = jnp.where(tri, 1.0, 0.0).astype(BF16)
    cum = _dot_exact_lhs(tri_b, dta)
    cum_t = cum.T
    cum_last = cum[L - 1:L, :]

    er = lax.broadcasted_iota(jnp.int32, (LANES, SSD_INNER), 0)
    ec = lax.broadcasted_iota(jnp.int32, (LANES, SSD_INNER), 1)
    expand = jnp.where(ec // SSD_HEADDIM == er, 1.0, 0.0).astype(BF16)

    xdt = (xs * _dot_exact_rhs(dt, expand)).astype(BF16)
    lane_half = lane // HEAD_DIM
    heads_per_group = SSD_HEADS // SSD_GROUPS
    y_parts = []
    for g in range(SSD_GROUPS):
        bm_g = bm[:, g * SSD_STATE:(g + 1) * SSD_STATE]
        cm_g = cm[:, g * SSD_STATE:(g + 1) * SSD_STATE]
        cb = _dot_nt(cm_g.astype(BF16), bm_g.astype(BF16))
        for pr in range(heads_per_group // 2):
            col0 = g * HALF_INNER + pr * LANES
            xp = xdt[:, col0:col0 + LANES]
            yp = jnp.zeros((L, LANES), F32)
            for e in range(2):
                hh = g * heads_per_group + pr * 2 + e
                seg = cum[:, hh:hh + 1] - cum_t[hh:hh + 1, :]
                sc = jnp.where(tri, cb * jnp.exp(jnp.minimum(seg, 0.0)), 0.0)
                xm = jnp.where(lane_half == e, xp, jnp.zeros_like(xp))
                yp = yp + _dot(sc.astype(BF16), xm)
            y_parts.append(yp)
    y_diag = jnp.concatenate(y_parts, axis=1)

    ecum = _dot_exact_rhs(jnp.exp(cum), expand)
    y_off = jnp.concatenate(
        [_dot(cm[:, g * SSD_STATE:(g + 1) * SSD_STATE].astype(BF16), ht[g].astype(BF16))
         for g in range(SSD_GROUPS)], axis=1) * ecum

    wgt = jnp.exp(cum_last - cum) * dt
    xw = (xs * _dot_exact_rhs(wgt, expand)).astype(BF16)
    cd = _dot_exact_rhs(jnp.broadcast_to(jnp.exp(cum_last), (SUBLANES, LANES)), expand)[0:1, :]
    for g in range(SSD_GROUPS):
        sl = slice(g * HALF_INNER, (g + 1) * HALF_INNER)
        bm_t = bm[:, g * SSD_STATE:(g + 1) * SSD_STATE].T.astype(BF16)
        ht[g] = ht[g] * cd[:, sl] + _dot(bm_t, xw[:, sl])

    y = (y_diag + y_off + dexp_ref[...] * xs) * _silu(pad_rows(z_ref[...]))
    yn = []
    for g in range(SSD_GROUPS):
        yg = y[:, g * HALF_INNER:(g + 1) * HALF_INNER]
        ms = jnp.mean(yg * yg, axis=-1, keepdims=True)
        yn.append(yg * lax.rsqrt(ms + EPS))
    yn = jnp.concatenate(yn, axis=1) * nw_ref[...]
    y_ref[...] = yn[:lin].astype(BF16)

    @pl.when(c == last)
    def _():
        for g in range(SSD_GROUPS):
            hout_ref[0, g * HALF_INNER:(g + 1) * HALF_INNER, :] = ht[g].T


def _ssd(xbc, dtr, z, cprev, h0, cw, cb, dtb, alog, dexp, nw, batch, lin, valid):
    rows = xbc.shape[0]
    nc = rows // (batch * lin)
    blk = lambda b, c: (b * nc + c, 0)
    per_b = lambda b, c: (b, 0, 0)
    const = lambda b, c: (0, 0)
    return pl.pallas_call(
        functools.partial(_ssd_kernel, lin=lin, valid=valid),
        grid=(batch, nc),
        in_specs=[
            pl.BlockSpec((lin, CONV_DIM), blk),
            pl.BlockSpec((lin, LANES), blk),
            pl.BlockSpec((lin, SSD_INNER), blk),
            pl.BlockSpec((1, SUBLANES, CONV_DIM), per_b),
            pl.BlockSpec((1, SSD_INNER, SSD_STATE), per_b),
            pl.BlockSpec((SUBLANES, CONV_DIM), const),
            pl.BlockSpec((1, CONV_DIM), const),
            pl.BlockSpec((1, LANES), const),
            pl.BlockSpec((1, LANES), const),
            pl.BlockSpec((1, SSD_INNER), const),
            pl.BlockSpec((1, SSD_INNER), const),
        ],
        out_specs=[
            pl.BlockSpec((lin, SSD_INNER), blk),
            pl.BlockSpec((1, SSD_INNER, SSD_STATE), per_b),
        ],
        out_shape=[
            jax.ShapeDtypeStruct((rows, SSD_INNER), BF16),
            jax.ShapeDtypeStruct((batch, SSD_INNER, SSD_STATE), F32),
        ],
        scratch_shapes=[
            pltpu.VMEM((SUBLANES + SSD_CHUNK + SUBLANES, CONV_DIM), F32),
            pltpu.VMEM((SSD_GROUPS, SSD_STATE, HALF_INNER), F32),
        ],
        compiler_params=pltpu.CompilerParams(
            dimension_semantics=("arbitrary", "arbitrary"), vmem_limit_bytes=VMEM_LIMIT),
        name="ssd",
    )(xbc, dtr, z, cprev, h0, cw, cb, dtb, alog, dexp, nw)


def _attn_prompt_kernel(q_ref, k_ref, v_ref, g_ref, o_ref,
                        bias_scr, kh_scr, vt_scr, qt_scr, m_scr, acc_scr, ot_scr, s_scr, *, n_tab, n_sb_max):
    b = pl.program_id(0)
    i = pl.program_id(1)
    seq = k_ref.shape[0]

    @pl.when((b == 0) & (i == 0))
    def _():
        r = lax.broadcasted_iota(jnp.int32, (K_SUPER, Q_BLOCK), 0)
        c = lax.broadcasted_iota(jnp.int32, (K_SUPER, Q_BLOCK), 1)
        for tb in range(n_tab):
            w = _multiplicity(tb * Q_BLOCK + c - r)
            bias_scr[tb] = jnp.where(w > 0.0, jnp.log(jnp.maximum(w, 1.0)), NEG)
        vt_scr[:, HEAD_DIM:, :] = jnp.ones((ATT_KV_HEADS, ONES_ROWS, seq), BF16)

    @pl.when(i == 0)
    def _():
        for cidx in range(seq // K_SUPER):
            rows = slice(cidx * K_SUPER, (cidx + 1) * K_SUPER)
            kc = k_ref[rows, :]
            vtc = v_ref[rows, :].T
            for kvh in range(ATT_KV_HEADS):
                hs = slice(kvh * HEAD_DIM, (kvh + 1) * HEAD_DIM)
                kh_scr[kvh, rows, :] = kc[:, hs].astype(BF16)
                vt_scr[kvh, 0:HEAD_DIM, rows] = vtc[hs, :].astype(BF16)

    qt_scr[...] = q_ref[...].astype(F32).T.astype(BF16)
    m_scr[...] = jnp.full(m_scr.shape, NEG, F32)
    acc_scr[...] = jnp.zeros(acc_scr.shape, F32)
    par = i % 2
    j_last = i // 2
    n_sb = jnp.minimum(j_last + 1, n_sb_max)

    def body(dl, carry):
        start = pl.multiple_of((j_last - dl) * K_SUPER, K_SUPER)
        bias = bias_scr[par + 2 * dl]
        for kvh in range(ATT_KV_HEADS):
            kb = kh_scr[kvh, pl.ds(start, K_SUPER), :]
            for g in range(ATT_GQ):
                h = kvh * ATT_GQ + g
                s_scr[h] = _dot(kb, qt_scr[h * HEAD_DIM:(h + 1) * HEAD_DIM, :])
        for kvh in range(ATT_KV_HEADS):
            vtb = vt_scr[kvh, :, pl.ds(start, K_SUPER)]
            for g in range(ATT_GQ):
                h = kvh * ATT_GQ + g
                s = s_scr[h] + bias
                m_old = m_scr[h, 0:1, :]
                m_new = jnp.maximum(m_old, jnp.max(s, axis=0, keepdims=True))
                p = jnp.exp(s - m_new).astype(BF16)
                acc_scr[h] = jnp.exp(m_old - m_new) * acc_scr[h] + _dot(vtb, p)
                m_scr[h, 0:1, :] = m_new
        return carry

    lax.fori_loop(0, n_sb, body, 0)
    for h in range(ATT_HEADS):
        acc = acc_scr[h]
        ot_scr[h * HEAD_DIM:(h + 1) * HEAD_DIM, :] = acc[0:HEAD_DIM] / acc[HEAD_DIM:HEAD_DIM + 1]
    o_ref[...] = (ot_scr[...].T * _silu(g_ref[...])).astype(BF16)


def _attn_prompt(q, k, v, gate, batch, seq):
    rows = q.shape[0]
    nqb = seq // Q_BLOCK
    assert seq % K_SUPER == 0
    n_sb_max = min(seq // K_SUPER, W_MAX // K_SUPER + 1)
    n_tab = 2 * n_sb_max
    qblk = lambda b, i: (b * nqb + i, 0)
    per_b = lambda b, i: (b, 0)
    return pl.pallas_call(
        functools.partial(_attn_prompt_kernel, n_tab=n_tab, n_sb_max=n_sb_max),
        grid=(batch, nqb),
        in_specs=[
            pl.BlockSpec((Q_BLOCK, ATT_INNER), qblk),
            pl.BlockSpec((seq, KV_DIM), per_b),
            pl.BlockSpec((seq, KV_DIM), per_b),
            pl.BlockSpec((Q_BLOCK, ATT_INNER), qblk),
        ],
        out_specs=pl.BlockSpec((Q_BLOCK, ATT_INNER), qblk),
        out_shape=jax.ShapeDtypeStruct((rows, ATT_INNER), BF16),
        scratch_shapes=[
            pltpu.VMEM((n_tab, K_SUPER, Q_BLOCK), F32),
            pltpu.VMEM((ATT_KV_HEADS, seq, HEAD_DIM), BF16),
            pltpu.VMEM((ATT_KV_HEADS, ACC_ROWS, seq), BF16),
            pltpu.VMEM((ATT_INNER, Q_BLOCK), BF16),
            pltpu.VMEM((ATT_HEADS, SUBLANES, Q_BLOCK), F32),
            pltpu.VMEM((ATT_HEADS, ACC_ROWS, Q_BLOCK), F32),
            pltpu.VMEM((ATT_INNER, Q_BLOCK), F32),
            pltpu.VMEM((ATT_HEADS, K_SUPER, Q_BLOCK), F32),
        ],
        compiler_params=pltpu.CompilerParams(
            dimension_semantics=("arbitrary", "arbitrary"), vmem_limit_bytes=VMEM_LIMIT),
        name="attn_prompt",
    )(q, k, v, gate)


def _attn_sample_kernel(q_ref, kn_ref, vn_ref, g_ref, ck_ref, cv_ref, o_ref, ko_ref, vo_ref, *, ts):
    win = ck_ref.shape[1]
    tp = SAMPLE_PAD
    rows = ATT_GQ * tp
    t_c = lax.broadcasted_iota(jnp.int32, (rows, win), 0) % tp
    j_c = lax.broadcasted_iota(jnp.int32, (rows, win), 1)
    w_c = _multiplicity(win + t_c - j_c)
    t_n = lax.broadcasted_iota(jnp.int32, (rows, tp), 0) % tp
    j_n = lax.broadcasted_iota(jnp.int32, (rows, tp), 1)
    w_n = jnp.where(j_n < ts, _multiplicity(t_n - j_n), 0.0)
    for kvh in range(ATT_KV_HEADS):
        c0 = kvh * ATT_GQ * HEAD_DIM
        qh = jnp.concatenate(
            [q_ref[:, c0 + g * HEAD_DIM:c0 + (g + 1) * HEAD_DIM] for g in range(ATT_GQ)], axis=0)
        ksl = slice(kvh * HEAD_DIM, (kvh + 1) * HEAD_DIM)
        s_c = jnp.where(w_c > 0.0, _dot_nt(qh, ck_ref[0, :, ksl].astype(BF16)), NEG)
        s_n = jnp.where(w_n > 0.0, _dot_nt(qh, kn_ref[:, ksl].astype(BF16)), NEG)
        m = jnp.maximum(jnp.max(s_c, axis=-1, keepdims=True), jnp.max(s_n, axis=-1, keepdims=True))
        p_c = jnp.exp(s_c - m) * w_c
        p_n = jnp.exp(s_n - m) * w_n
        den = jnp.sum(p_c, axis=-1, keepdims=True) + jnp.sum(p_n, axis=-1, keepdims=True)
        num = (_dot(p_c.astype(BF16), cv_ref[0, :, ksl].astype(BF16))
               + _dot(p_n.astype(BF16), vn_ref[:, ksl].astype(BF16)))
        o = num / den
        o = jnp.concatenate([o[g * tp:(g + 1) * tp] for g in range(ATT_GQ)], axis=1)
        csl = slice(c0, c0 + ATT_GQ * HEAD_DIM)
        o_ref[:, csl] = (o * _silu(g_ref[:, csl])).astype(BF16)
    ko_ref[0, 0:win - ts, :] = ck_ref[0, ts:win, :]
    ko_ref[0, win - ts:win, :] = kn_ref[0:ts, :]
    vo_ref[0, 0:win - ts, :] = cv_ref[0, ts:win, :]
    vo_ref[0, win - ts:win, :] = vn_ref[0:ts, :]


def _attn_sample(q, kn, vn, gate, ck, cv, ts):
    batch, win, _ = ck.shape
    tp = SAMPLE_PAD
    blk = lambda b: (b, 0)
    per_b = lambda b: (b, 0, 0)
    return pl.pallas_call(
        functools.partial(_attn_sample_kernel, ts=ts),
        grid=(batch,),
        in_specs=[
            pl.BlockSpec((tp, ATT_INNER), blk),
            pl.BlockSpec((tp, KV_DIM), blk),
            pl.BlockSpec((tp, KV_DIM), blk),
            pl.BlockSpec((tp, ATT_INNER), blk),
            pl.BlockSpec((1, win, KV_DIM), per_b),
            pl.BlockSpec((1, win, KV_DIM), per_b),
        ],
        out_specs=[
            pl.BlockSpec((tp, ATT_INNER), blk),
            pl.BlockSpec((1, win, KV_DIM), per_b),
            pl.BlockSpec((1, win, KV_DIM), per_b),
        ],
        out_shape=[
            jax.ShapeDtypeStruct((batch * tp, ATT_INNER), BF16),
            jax.ShapeDtypeStruct((batch, win, KV_DIM), F32),
            jax.ShapeDtypeStruct((batch, win, KV_DIM), F32),
        ],
        compiler_params=pltpu.CompilerParams(
            dimension_semantics=("arbitrary",), vmem_limit_bytes=VMEM_LIMIT),
        name="attn_sample",
    )(q, kn, vn, gate, ck, cv)


def _outproj_kernel(x_ref, ys_ref, ya_ref, w_ref, o_ref):
    o_ref[...] = (x_ref[...] + _dot(ys_ref[...], w_ref[0:SSD_INNER, :])
                  + _dot(ya_ref[...], w_ref[SSD_INNER:, :]))


def _out_proj(x2d, y_ssd, y_att, w_o, tm):
    rows = x2d.shape[0]
    row = lambda i: (i, 0)
    const = lambda i: (0, 0)
    return pl.pallas_call(
        _outproj_kernel,
        grid=(rows // tm,),
        in_specs=[
            pl.BlockSpec((tm, D_MODEL), row),
            pl.BlockSpec((tm, SSD_INNER), row),
            pl.BlockSpec((tm, ATT_INNER), row),
            pl.BlockSpec((SSD_INNER + ATT_INNER, D_MODEL), const, pipeline_mode=pl.Buffered(1)),
        ],
        out_specs=pl.BlockSpec((tm, D_MODEL), row),
        out_shape=jax.ShapeDtypeStruct((rows, D_MODEL), F32),
        compiler_params=pltpu.CompilerParams(
            dimension_semantics=("arbitrary",), vmem_limit_bytes=VMEM_LIMIT),
        name="out_proj",
    )(x2d, y_ssd, y_att, w_o)


def _rope_tables(pos):
    n = pos.shape[0]
    half = ROT_DIM // 2
    inv = ROPE_THETA ** (-jnp.arange(0, ROT_DIM, 2, dtype=F32) / ROT_DIM)
    ang = pos.astype(F32)[:, None] * inv[None, :]
    cos, sin = jnp.cos(ang), jnp.sin(ang)
    rest = HEAD_DIM - ROT_DIM
    zh = jnp.zeros((n, half), F32)
    cos_h = jnp.concatenate([cos, cos, jnp.ones((n, rest), F32)], axis=1)
    sa_h = jnp.concatenate([zh, sin, jnp.zeros((n, rest), F32)], axis=1)
    sb_h = jnp.concatenate([-sin, zh, jnp.zeros((n, rest), F32)], axis=1)
    rep = LANES // HEAD_DIM
    return tuple(jnp.tile(t, (1, rep)) for t in (cos_h, sa_h, sb_h))


def _lane_pad(v, n=LANES):
    return jnp.pad(v, (0, n - v.shape[0])).reshape(1, n)


def kernel(x_prompt, x_sample, cache_k, cache_v, state_conv, state_ssm, norm_w, w_in, conv_w,
           conv_b, dt_bias, a_log, d_skip, ssd_norm_w, q_norm_w, k_norm_w, w_out):
    bp, tp_, _ = x_prompt.shape
    bs, ts, _ = x_sample.shape
    depth = w_in.shape[0]
    assert depth == 1 and tp_ % SSD_CHUNK == 0 and ts <= SAMPLE_PAD and ts >= CONV_W - 1
    l = 0
    win = cache_k.shape[2]

    wl = w_in[l]
    dt_cols = wl[:, SSD_INNER + CONV_DIM:SSD_INNER + CONV_DIM + SSD_HEADS]
    w_r = jnp.concatenate(
        [wl[:, :SSD_INNER + CONV_DIM], jnp.pad(dt_cols, ((0, 0), (0, LANES - SSD_HEADS))),
         wl[:, SSD_INNER + CONV_DIM + SSD_HEADS:]], axis=1).astype(BF16)
    w_o = w_out[l].astype(BF16)
    nw = norm_w[l].reshape(1, D_MODEL)
    rep = LANES // HEAD_DIM
    qnw = jnp.tile(q_norm_w[l], rep).reshape(1, LANES)
    knw = jnp.tile(k_norm_w[l], rep).reshape(1, LANES)
    cw = jnp.pad(conv_w[l], ((0, SUBLANES - CONV_W), (0, 0)))
    cb = conv_b[l].reshape(1, CONV_DIM)
    dtb = _lane_pad(dt_bias[l])
    alog = _lane_pad(a_log[l])
    dexp = jnp.repeat(d_skip[l], SSD_HEADDIM).reshape(1, SSD_INNER)
    snw = ssd_norm_w[l].reshape(1, SSD_INNER)

    tm = 256
    xp2 = x_prompt.reshape(bp * tp_, D_MODEL)
    tabs = _rope_tables(jnp.arange(tp_, dtype=jnp.int32))
    z, xbc, dtr, q, k, v, gate = _in_proj(xp2, nw, w_r, qnw, knw, tabs, tm)
    y_ssd, h_p = _ssd(xbc, dtr, z, jnp.zeros((bp, SUBLANES, CONV_DIM), F32),
                      jnp.zeros((bp, SSD_INNER, SSD_STATE), F32), cw, cb, dtb, alog, dexp, snw,
                      bp, SSD_CHUNK, SSD_CHUNK)
    y_att = _attn_prompt(q, k, v, gate, bp, tp_)
    y_p = _out_proj(xp2, y_ssd, y_att, w_o, tm).reshape(bp, tp_, D_MODEL)
    keep = min(W_MAX, tp_)
    k_p = k.reshape(bp, tp_, ATT_KV_HEADS, HEAD_DIM)[:, tp_ - keep:][None]
    v_p = v.reshape(bp, tp_, ATT_KV_HEADS, HEAD_DIM)[:, tp_ - keep:][None]
    c_p = xbc.reshape(bp, tp_, CONV_DIM)[:, tp_ - (CONV_W - 1):][None]
    h_p = h_p.reshape(1, bp, SSD_HEADS, SSD_HEADDIM, SSD_STATE)

    pad = SAMPLE_PAD
    xs2 = jnp.pad(x_sample, ((0, 0), (0, pad - ts), (0, 0))).reshape(bs * pad, D_MODEL)
    pos_s = PAST_LEN + (jnp.arange(bs * pad, dtype=jnp.int32) % pad)
    tabs_s = _rope_tables(pos_s)
    z, xbc, dtr, q, k, v, gate = _in_proj(xs2, nw, w_r, qnw, knw, tabs_s, bs * pad)
    cprev = jnp.pad(state_conv[l], ((0, 0), (SUBLANES - (CONV_W - 1), 0), (0, 0)))
    y_ssd, h_s = _ssd(xbc, dtr, z, cprev, state_ssm[l].reshape(bs, SSD_INNER, SSD_STATE),
                      cw, cb, dtb, alog, dexp, snw, bs, pad, ts)
    y_att, k_s, v_s = _attn_sample(q, k, v, gate, cache_k[l].reshape(bs, win, KV_DIM),
                                   cache_v[l].reshape(bs, win, KV_DIM), ts)
    y_s = _out_proj(xs2, y_ssd, y_att, w_o, bs * pad).reshape(bs, pad, D_MODEL)[:, :ts]
    k_s = k_s.reshape(1, bs, win, ATT_KV_HEADS, HEAD_DIM)
    v_s = v_s.reshape(1, bs, win, ATT_KV_HEADS, HEAD_DIM)
    c_s = xbc.reshape(bs, pad, CONV_DIM)[:, ts - (CONV_W - 1):ts][None]
    h_s = h_s.reshape(1, bs, SSD_HEADS, SSD_HEADDIM, SSD_STATE)

    return (y_p, y_s, k_p, v_p, c_p, h_p, k_s, v_s, c_s, h_s)
```

```python
import functools

import jax
import jax.numpy as jnp
from jax import lax
from jax.experimental import pallas as pl
from jax.experimental.pallas import tpu as pltpu

F32 = jnp.float32
BF16 = jnp.bfloat16

D_MODEL = 2048
SSD_HEADS = 16
SSD_HEADDIM = 64
SSD_INNER = SSD_HEADS * SSD_HEADDIM
SSD_GROUPS = 2
SSD_STATE = 128
CONV_W = 4
CONV_DIM = SSD_INNER + 2 * SSD_GROUPS * SSD_STATE
SSD_CHUNK = 128
ATT_HEADS = 16
ATT_KV_HEADS = 4
HEAD_DIM = 64
ATT_GQ = ATT_HEADS // ATT_KV_HEADS
ATT_INNER = ATT_HEADS * HEAD_DIM
KV_DIM = ATT_KV_HEADS * HEAD_DIM
ROT_DIM = HEAD_DIM // 4
ROPE_THETA = 500000.0
DILATED_BRANCHES = ((128, 1), (512, 4), (2048, 16))
W_MAX = 2048
PAST_LEN = 16384
EPS = 1e-6

LANES = 128
SUBLANES = 8
Q_BLOCK = 256
K_SUPER = 256
Q_PER_SUPER = K_SUPER // Q_BLOCK
ONES_ROWS = 16
ACC_ROWS = HEAD_DIM + ONES_ROWS
SAMPLE_PAD = 16
NEG = -1e30
VMEM_LIMIT = 56 * 1024 * 1024

Z0 = 0
X0 = Z0 + SSD_INNER
DT0 = X0 + CONV_DIM
Q0 = DT0 + SSD_HEADS
K0 = Q0 + ATT_INNER
V0 = K0 + KV_DIM
G0 = V0 + KV_DIM
W_ROWS = G0 + ATT_INNER
HALF_INNER = SSD_INNER // SSD_GROUPS
LOG2E = 1.4426950408889634
Q_SCALE = HEAD_DIM ** -0.5 * LOG2E


def _dot(a, b):
    return jnp.dot(a, b, preferred_element_type=F32)


def _dot_nt(a, b):
    return lax.dot_general(a, b, (((1,), (1,)), ((), ())), preferred_element_type=F32)


def _split3(x):
    hi = x.astype(BF16)
    r1 = x - hi.astype(F32)
    mid = r1.astype(BF16)
    lo = (r1 - mid.astype(F32)).astype(BF16)
    return hi, mid, lo


def _dot_exact_rhs(x, m):
    hi, mid, lo = _split3(x)
    return _dot(hi, m) + _dot(mid, m) + _dot(lo, m)


def _dot_exact_lhs(m, x):
    hi, mid, lo = _split3(x)
    return _dot(m, hi) + _dot(m, mid) + _dot(m, lo)


def _silu(x):
    return x * jax.nn.sigmoid(x)


def _multiplicity(d):
    w = jnp.zeros(d.shape, F32)
    for window, dil in DILATED_BRANCHES:
        hit = (d >= 0) & (d <= window) & (lax.rem(d, dil) == 0)
        w = w + jnp.where(hit, 1.0, 0.0)
    return w


def _norm_rope(y, nw, cos, sa, sb, scale):
    lane = lax.broadcasted_iota(jnp.int32, (1, LANES), 1)
    first = lane < HEAD_DIM
    y2 = y * y
    s_lo = jnp.sum(jnp.where(first, y2, 0.0), axis=-1, keepdims=True)
    s_hi = jnp.sum(jnp.where(first, 0.0, y2), axis=-1, keepdims=True)
    ms = jnp.where(first, s_lo, s_hi) * (1.0 / HEAD_DIM)
    yn = y * lax.rsqrt(ms + EPS) * nw
    half = ROT_DIM // 2
    rot = yn * cos + pltpu.roll(yn, half, 1) * sa + pltpu.roll(yn, LANES - half, 1) * sb
    return rot * scale


def _inproj_kernel(x_ref, nw_ref, w_ref, qnw_ref, knw_ref, cos_ref, sa_ref, sb_ref,
                   z_ref, xbc_ref, dt_ref, q_ref, kt_ref, vt_ref, g_ref):
    x = x_ref[...]
    ms = jnp.mean(x * x, axis=-1, keepdims=True)
    hn = (x * lax.rsqrt(ms + EPS) * nw_ref[...]).astype(BF16)

    def proj(a, b):
        return _dot_nt(hn, w_ref[a:b, :])

    z_ref[...] = proj(Z0, X0)
    xbc_ref[...] = proj(X0, DT0)
    dt_ref[...] = proj(DT0, DT0 + LANES)
    vt_ref[0] = proj(V0, G0).T
    g_ref[...] = proj(G0, W_ROWS)
    cos, sa, sb = cos_ref[...], sa_ref[...], sb_ref[...]
    q = proj(Q0, K0)
    for c in range(ATT_INNER // LANES):
        sl = slice(c * LANES, (c + 1) * LANES)
        q_ref[:, sl] = _norm_rope(q[:, sl], qnw_ref[...], cos, sa, sb, Q_SCALE).astype(BF16)
    k = proj(K0, V0)
    kn = [_norm_rope(k[:, c * LANES:(c + 1) * LANES], knw_ref[...], cos, sa, sb, 1.0)
          for c in range(KV_DIM // LANES)]
    kt_ref[0] = jnp.concatenate(kn, axis=1).T


def _in_proj(x2d, nw, w_t, qnw, knw, tables, tm, seq):
    rows = x2d.shape[0]
    period = tables[0].shape[0] // tm
    per_seq = seq // tm
    const = lambda i: (0, 0)
    row = lambda i: (i, 0)
    tab = lambda i: (i % period, 0)
    tr = lambda i: (i // per_seq, 0, i % per_seq)
    row_outs = {0: (SSD_INNER, F32), 1: (CONV_DIM, F32), 2: (LANES, F32), 3: (ATT_INNER, BF16),
                6: (ATT_INNER, F32)}
    out_specs, out_shape = [], []
    for idx in range(7):
        if idx in row_outs:
            n, dt = row_outs[idx]
            out_specs.append(pl.BlockSpec((tm, n), row))
            out_shape.append(jax.ShapeDtypeStruct((rows, n), dt))
        else:
            out_specs.append(pl.BlockSpec((1, KV_DIM, tm), tr))
            out_shape.append(jax.ShapeDtypeStruct((rows // seq, KV_DIM, seq), F32))
    return pl.pallas_call(
        _inproj_kernel,
        grid=(rows // tm,),
        in_specs=[
            pl.BlockSpec((tm, D_MODEL), row),
            pl.BlockSpec((1, D_MODEL), const),
            pl.BlockSpec((W_ROWS, D_MODEL), const, pipeline_mode=pl.Buffered(1)),
            pl.BlockSpec((1, LANES), const),
            pl.BlockSpec((1, LANES), const),
            pl.BlockSpec((tm, LANES), tab),
            pl.BlockSpec((tm, LANES), tab),
            pl.BlockSpec((tm, LANES), tab),
        ],
        out_specs=out_specs,
        out_shape=out_shape,
        compiler_params=pltpu.CompilerParams(
            dimension_semantics=("arbitrary",), vmem_limit_bytes=VMEM_LIMIT),
        name="in_proj",
    )(x2d, nw, w_t, qnw, knw, *tables)


def _ssd_kernel(xbc_ref, dtr_ref, z_ref, cprev_ref, h0_ref, cw_ref, cb_ref, dtb_ref, alog_ref,
                dexp_ref, nw_ref, y_ref, hout_ref, xext, ht, *, lin, valid):
    L = SSD_CHUNK
    c = pl.program_id(1)
    last = pl.num_programs(1) - 1

    @pl.when(c == 0)
    def _():
        xext[0:SUBLANES, :] = cprev_ref[0]
        for g in range(SSD_GROUPS):
            ht[g] = h0_ref[0, g * HALF_INNER:(g + 1) * HALF_INNER, :].T

    def pad_rows(v):
        if lin == L:
            return v
        return jnp.concatenate([v, jnp.zeros((L - lin, v.shape[1]), v.dtype)], axis=0)

    if lin < L:
        xext[SUBLANES + lin:SUBLANES + L, :] = jnp.zeros((L - lin, CONV_DIM), F32)
    xext[SUBLANES:SUBLANES + lin, :] = xbc_ref[...]
    acc = cb_ref[...]
    for j in range(CONV_W):
        r0 = SUBLANES - j
        acc = acc + cw_ref[CONV_W - 1 - j:CONV_W - j, :] * xext[r0:r0 + L, :]
    xext[0:SUBLANES, :] = xext[L:L + SUBLANES, :]
    xc = _silu(acc)
    xs = xc[:, :SSD_INNER]
    bm = xc[:, SSD_INNER:SSD_INNER + SSD_GROUPS * SSD_STATE]
    cm = xc[:, SSD_INNER + SSD_GROUPS * SSD_STATE:]

    lane = lax.broadcasted_iota(jnp.int32, (1, LANES), 1)
    rowid = lax.broadcasted_iota(jnp.int32, (L, 1), 0)
    dt = jax.nn.softplus(pad_rows(dtr_ref[...]) + dtb_ref[...])
    dt = jnp.where((lane < SSD_HEADS) & (rowid < valid), dt, 0.0)
    a = -jnp.exp(alog_ref[...])
    dta = dt * a

    r2 = lax.broadcasted_iota(jnp.int32, (L, L), 0)
    c2 = lax.broadcasted_iota(jnp.int32, (L, L), 1)
    tri = r2 >= c2
    tri_b = jnp.where(tri, 1.0, 0.0).astype(BF16)
    cum = _dot_exact_lhs(tri_b, dta)
    cum_t = cum.T
    cum_last = cum[L - 1:L, :]

    er = lax.broadcasted_iota(jnp.int32, (LANES, SSD_INNER), 0)
    ec = lax.broadcasted_iota(jnp.int32, (LANES, SSD_INNER), 1)
    expand = jnp.where(ec // SSD_HEADDIM == er, 1.0, 0.0).astype(BF16)

    xdt = (xs * _dot_exact_rhs(dt, expand)).astype(BF16)
    lane_half = lane // HEAD_DIM
    heads_per_group = SSD_HEADS // SSD_GROUPS
    y_parts = []
    for g in range(SSD_GROUPS):
        bm_g = bm[:, g * SSD_STATE:(g + 1) * SSD_STATE]
        cm_g = cm[:, g * SSD_STATE:(g + 1) * SSD_STATE]
        cb = _dot_nt(cm_g.astype(BF16), bm_g.astype(BF16))
        for pr in range(heads_per_group // 2):
            col0 = g * HALF_INNER + pr * LANES
            xp = xdt[:, col0:col0 + LANES]
            yp = jnp.zeros((L, LANES), F32)
            for e in range(2):
                hh = g * heads_per_group + pr * 2 + e
                seg = cum[:, hh:hh + 1] - cum_t[hh:hh + 1, :]
                sc = jnp.where(tri, cb * jnp.exp(jnp.minimum(seg, 0.0)), 0.0)
                xm = jnp.where(lane_half == e, xp, jnp.zeros_like(xp))
                yp = yp + _dot(sc.astype(BF16), xm)
            y_parts.append(yp)
    y_diag = jnp.concatenate(y_parts, axis=1)

    ecum = _dot_exact_rhs(jnp.exp(cum), expand)
    y_off = jnp.concatenate(
        [_dot(cm[:, g * SSD_STATE:(g + 1) * SSD_STATE].astype(BF16), ht[g].astype(BF16))
         for g in range(SSD_GROUPS)], axis=1) * ecum

    wgt = jnp.exp(cum_last - cum) * dt
    xw = (xs * _dot_exact_rhs(wgt, expand)).astype(BF16)
    cd = _dot_exact_rhs(jnp.broadcast_to(jnp.exp(cum_last), (SUBLANES, LANES)), expand)[0:1, :]
    for g in range(SSD_GROUPS):
        sl = slice(g * HALF_INNER, (g + 1) * HALF_INNER)
        bm_t = bm[:, g * SSD_STATE:(g + 1) * SSD_STATE].T.astype(BF16)
        ht[g] = ht[g] * cd[:, sl] + _dot(bm_t, xw[:, sl])

    y = (y_diag + y_off + dexp_ref[...] * xs) * _silu(pad_rows(z_ref[...]))
    yn = []
    for g in range(SSD_GROUPS):
        yg = y[:, g * HALF_INNER:(g + 1) * HALF_INNER]
        ms = jnp.mean(yg * yg, axis=-1, keepdims=True)
        yn.append(yg * lax.rsqrt(ms + EPS))
    yn = jnp.concatenate(yn, axis=1) * nw_ref[...]
    y_ref[...] = yn[:lin].astype(BF16)

    @pl.when(c == last)
    def _():
        for g in range(SSD_GROUPS):
            hout_ref[0, g * HALF_INNER:(g + 1) * HALF_INNER, :] = ht[g].T


def _ssd(xbc, dtr, z, cprev, h0, cw, cb, dtb, alog, dexp, nw, batch, lin, valid):
    rows = xbc.shape[0]
    nc = rows // (batch * lin)
    blk = lambda b, c: (b * nc + c, 0)
    per_b = lambda b, c: (b, 0, 0)
    const = lambda b, c: (0, 0)
    return pl.pallas_call(
        functools.partial(_ssd_kernel, lin=lin, valid=valid),
        grid=(batch, nc),
        in_specs=[
            pl.BlockSpec((lin, CONV_DIM), blk),
            pl.BlockSpec((lin, LANES), blk),
            pl.BlockSpec((lin, SSD_INNER), blk),
            pl.BlockSpec((1, SUBLANES, CONV_DIM), per_b),
            pl.BlockSpec((1, SSD_INNER, SSD_STATE), per_b),
            pl.BlockSpec((SUBLANES, CONV_DIM), const),
            pl.BlockSpec((1, CONV_DIM), const),
            pl.BlockSpec((1, LANES), const),
            pl.BlockSpec((1, LANES), const),
            pl.BlockSpec((1, SSD_INNER), const),
            pl.BlockSpec((1, SSD_INNER), const),
        ],
        out_specs=[
            pl.BlockSpec((lin, SSD_INNER), blk),
            pl.BlockSpec((1, SSD_INNER, SSD_STATE), per_b),
        ],
        out_shape=[
            jax.ShapeDtypeStruct((rows, SSD_INNER), BF16),
            jax.ShapeDtypeStruct((batch, SSD_INNER, SSD_STATE), F32),
        ],
        scratch_shapes=[
            pltpu.VMEM((SUBLANES + SSD_CHUNK + SUBLANES, CONV_DIM), F32),
            pltpu.VMEM((SSD_GROUPS, SSD_STATE, HALF_INNER), F32),
        ],
        compiler_params=pltpu.CompilerParams(
            dimension_semantics=("arbitrary", "arbitrary"), vmem_limit_bytes=VMEM_LIMIT),
        name="ssd",
    )(xbc, dtr, z, cprev, h0, cw, cb, dtb, alog, dexp, nw)


def _attn_prompt_kernel(q_ref, kt_ref, vt_ref, g_ref, o_ref,
                        bias_scr, kh_scr, va_scr, qt_scr, m_scr, acc_scr, ot_scr, s_scr, *, n_tab, n_sb_max):
    b = pl.program_id(0)
    i = pl.program_id(1)
    seq = kt_ref.shape[2]
    pair_w = 2 * Q_BLOCK
    pairs_per_kv = ATT_GQ // 2

    @pl.when((b == 0) & (i == 0))
    def _():
        r = lax.broadcasted_iota(jnp.int32, (K_SUPER, Q_BLOCK), 0)
        c = lax.broadcasted_iota(jnp.int32, (K_SUPER, Q_BLOCK), 1)
        for tb in range(n_tab):
            w = _multiplicity(tb * Q_BLOCK + c - r)
            bias_scr[tb] = jnp.where(w > 0.0, jnp.log2(jnp.maximum(w, 1.0)), NEG)
        va_scr[:, HEAD_DIM:, :] = jnp.ones((ATT_KV_HEADS, ONES_ROWS, seq), BF16)

    @pl.when(i == 0)
    def _():
        for kvh in range(ATT_KV_HEADS):
            hs = slice(kvh * HEAD_DIM, (kvh + 1) * HEAD_DIM)
            va_scr[kvh, 0:HEAD_DIM, :] = vt_ref[0, hs, :].astype(BF16)
        for cidx in range(seq // K_SUPER):
            rows = slice(cidx * K_SUPER, (cidx + 1) * K_SUPER)
            kc = kt_ref[0, :, rows].T
            for kvh in range(ATT_KV_HEADS):
                hs = slice(kvh * HEAD_DIM, (kvh + 1) * HEAD_DIM)
                kh_scr[kvh, rows, :] = kc[:, hs].astype(BF16)

    qt = q_ref[...].astype(F32).T
    for kvh in range(ATT_KV_HEADS):
        qt_scr[kvh] = jnp.concatenate(
            [qt[(kvh * ATT_GQ + g) * HEAD_DIM:(kvh * ATT_GQ + g + 1) * HEAD_DIM, :]
             for g in range(ATT_GQ)], axis=1).astype(BF16)
    m_scr[...] = jnp.full(m_scr.shape, NEG, F32)
    acc_scr[...] = jnp.zeros(acc_scr.shape, F32)
    par = i % Q_PER_SUPER
    j_last = i // Q_PER_SUPER
    n_sb = jnp.minimum(j_last + 1, n_sb_max)

    def body(dl, carry):
        start = pl.multiple_of((j_last - dl) * K_SUPER, K_SUPER)
        bias = bias_scr[par + Q_PER_SUPER * dl]
        bias2 = jnp.concatenate([bias, bias], axis=1)
        for kvh in range(ATT_KV_HEADS):
            s_scr[kvh] = _dot(kh_scr[kvh, pl.ds(start, K_SUPER), :], qt_scr[kvh])
        for kvh in range(ATT_KV_HEADS):
            vtb = va_scr[kvh, :, pl.ds(start, K_SUPER)]
            for pr in range(pairs_per_kv):
                u = kvh * pairs_per_kv + pr
                s = s_scr[kvh, :, pr * pair_w:(pr + 1) * pair_w] + bias2
                m_old = m_scr[u, 0:1, :]
                m_new = jnp.maximum(m_old, jnp.max(s, axis=0, keepdims=True))
                p = jnp.exp2(s - m_new).astype(BF16)
                acc_scr[u] = jnp.exp2(m_old - m_new) * acc_scr[u] + _dot(vtb, p)
                m_scr[u, 0:1, :] = m_new
        return carry

    lax.fori_loop(0, n_sb, body, 0)
    for u in range(ATT_HEADS // 2):
        acc = acc_scr[u]
        o = acc[0:HEAD_DIM] / acc[HEAD_DIM:HEAD_DIM + 1]
        for e in range(2):
            h = 2 * u + e
            ot_scr[h * HEAD_DIM:(h + 1) * HEAD_DIM, :] = o[:, e * Q_BLOCK:(e + 1) * Q_BLOCK]
    o_ref[...] = (ot_scr[...].T * _silu(g_ref[...])).astype(BF16)


def _attn_prompt(q, kt, vt, gate, batch, seq):
    rows = q.shape[0]
    nqb = seq // Q_BLOCK
    assert seq % K_SUPER == 0
    n_sb_max = min(seq // K_SUPER, W_MAX // K_SUPER + 1)
    n_tab = Q_PER_SUPER * n_sb_max
    qblk = lambda b, i: (b * nqb + i, 0)
    per_b = lambda b, i: (b, 0, 0)
    return pl.pallas_call(
        functools.partial(_attn_prompt_kernel, n_tab=n_tab, n_sb_max=n_sb_max),
        grid=(batch, nqb),
        in_specs=[
            pl.BlockSpec((Q_BLOCK, ATT_INNER), qblk),
            pl.BlockSpec((1, KV_DIM, seq), per_b),
            pl.BlockSpec((1, KV_DIM, seq), per_b),
            pl.BlockSpec((Q_BLOCK, ATT_INNER), qblk),
        ],
        out_specs=pl.BlockSpec((Q_BLOCK, ATT_INNER), qblk),
        out_shape=jax.ShapeDtypeStruct((rows, ATT_INNER), BF16),
        scratch_shapes=[
            pltpu.VMEM((n_tab, K_SUPER, Q_BLOCK), F32),
            pltpu.VMEM((ATT_KV_HEADS, seq, HEAD_DIM), BF16),
            pltpu.VMEM((ATT_KV_HEADS, ACC_ROWS, seq), BF16),
            pltpu.VMEM((ATT_KV_HEADS, HEAD_DIM, ATT_GQ * Q_BLOCK), BF16),
            pltpu.VMEM((ATT_HEADS // 2, SUBLANES, 2 * Q_BLOCK), F32),
            pltpu.VMEM((ATT_HEADS // 2, ACC_ROWS, 2 * Q_BLOCK), F32),
            pltpu.VMEM((ATT_INNER, Q_BLOCK), F32),
            pltpu.VMEM((ATT_KV_HEADS, K_SUPER, ATT_GQ * Q_BLOCK), F32),
        ],
        compiler_params=pltpu.CompilerParams(
            dimension_semantics=("arbitrary", "arbitrary"), vmem_limit_bytes=VMEM_LIMIT),
        name="attn_prompt",
    )(q, kt, vt, gate)


def _attn_sample_kernel(q_ref, knt_ref, vnt_ref, g_ref, ckt_ref, cvt_ref, o_ref, kot_ref, vot_ref, *, ts):
    win = ckt_ref.shape[2]
    tp = SAMPLE_PAD
    rows = ATT_GQ * tp
    t_c = lax.broadcasted_iota(jnp.int32, (rows, win), 0) % tp
    j_c = lax.broadcasted_iota(jnp.int32, (rows, win), 1)
    w_c = _multiplicity(win + t_c - j_c)
    t_n = lax.broadcasted_iota(jnp.int32, (rows, tp), 0) % tp
    j_n = lax.broadcasted_iota(jnp.int32, (rows, tp), 1)
    w_n = jnp.where(j_n < ts, _multiplicity(t_n - j_n), 0.0)
    for kvh in range(ATT_KV_HEADS):
        c0 = kvh * ATT_GQ * HEAD_DIM
        qh = jnp.concatenate(
            [q_ref[:, c0 + g * HEAD_DIM:c0 + (g + 1) * HEAD_DIM] for g in range(ATT_GQ)], axis=0)
        hs = slice(kvh * HEAD_DIM, (kvh + 1) * HEAD_DIM)
        s_c = jnp.where(w_c > 0.0, _dot(qh, ckt_ref[0, hs, :].astype(BF16)), NEG)
        s_n = jnp.where(w_n > 0.0, _dot(qh, knt_ref[0, hs, :].astype(BF16)), NEG)
        m = jnp.maximum(jnp.max(s_c, axis=-1, keepdims=True), jnp.max(s_n, axis=-1, keepdims=True))
        p_c = jnp.exp2(s_c - m) * w_c
        p_n = jnp.exp2(s_n - m) * w_n
        den = jnp.sum(p_c, axis=-1, keepdims=True) + jnp.sum(p_n, axis=-1, keepdims=True)
        num = (_dot_nt(p_c.astype(BF16), cvt_ref[0, hs, :].astype(BF16))
               + _dot_nt(p_n.astype(BF16), vnt_ref[0, hs, :].astype(BF16)))
        o = num / den
        o = jnp.concatenate([o[g * tp:(g + 1) * tp] for g in range(ATT_GQ)], axis=1)
        csl = slice(c0, c0 + ATT_GQ * HEAD_DIM)
        o_ref[:, csl] = (o * _silu(g_ref[:, csl])).astype(BF16)
    kot_ref[0, :, 0:win - ts] = ckt_ref[0, :, ts:win]
    kot_ref[0, :, win - ts:win] = knt_ref[0, :, 0:ts]
    vot_ref[0, :, 0:win - ts] = cvt_ref[0, :, ts:win]
    vot_ref[0, :, win - ts:win] = vnt_ref[0, :, 0:ts]


def _attn_sample(q, knt, vnt, gate, ckt, cvt, ts):
    batch, _, win = ckt.shape
    tp = SAMPLE_PAD
    blk = lambda b: (b, 0)
    per_b = lambda b: (b, 0, 0)
    return pl.pallas_call(
        functools.partial(_attn_sample_kernel, ts=ts),
        grid=(batch,),
        in_specs=[
            pl.BlockSpec((tp, ATT_INNER), blk),
            pl.BlockSpec((1, KV_DIM, tp), per_b),
            pl.BlockSpec((1, KV_DIM, tp), per_b),
            pl.BlockSpec((tp, ATT_INNER), blk),
            pl.BlockSpec((1, KV_DIM, win), per_b),
            pl.BlockSpec((1, KV_DIM, win), per_b),
        ],
        out_specs=[
            pl.BlockSpec((tp, ATT_INNER), blk),
            pl.BlockSpec((1, KV_DIM, win), per_b),
            pl.BlockSpec((1, KV_DIM, win), per_b),
        ],
        out_shape=[
            jax.ShapeDtypeStruct((batch * tp, ATT_INNER), BF16),
            jax.ShapeDtypeStruct((batch, KV_DIM, win), F32),
            jax.ShapeDtypeStruct((batch, KV_DIM, win), F32),
        ],
        compiler_params=pltpu.CompilerParams(
            dimension_semantics=("arbitrary",), vmem_limit_bytes=VMEM_LIMIT),
        name="attn_sample",
    )(q, knt, vnt, gate, ckt, cvt)


def _outproj_kernel(x_ref, ys_ref, ya_ref, w_ref, o_ref):
    o_ref[...] = (x_ref[...] + _dot(ys_ref[...], w_ref[0:SSD_INNER, :])
                  + _dot(ya_ref[...], w_ref[SSD_INNER:, :]))


def _out_proj(x2d, y_ssd, y_att, w_o, tm):
    rows = x2d.shape[0]
    row = lambda i: (i, 0)
    const = lambda i: (0, 0)
    return pl.pallas_call(
        _outproj_kernel,
        grid=(rows // tm,),
        in_specs=[
            pl.BlockSpec((tm, D_MODEL), row),
            pl.BlockSpec((tm, SSD_INNER), row),
            pl.BlockSpec((tm, ATT_INNER), row),
            pl.BlockSpec((SSD_INNER + ATT_INNER, D_MODEL), const, pipeline_mode=pl.Buffered(1)),
        ],
        out_specs=pl.BlockSpec((tm, D_MODEL), row),
        out_shape=jax.ShapeDtypeStruct((rows, D_MODEL), F32),
        compiler_params=pltpu.CompilerParams(
            dimension_semantics=("arbitrary",), vmem_limit_bytes=VMEM_LIMIT),
        name="out_proj",
    )(x2d, y_ssd, y_att, w_o)


def _rope_tables(pos):
    n = pos.shape[0]
    half = ROT_DIM // 2
    inv = ROPE_THETA ** (-jnp.arange(0, ROT_DIM, 2, dtype=F32) / ROT_DIM)
    ang = pos.astype(F32)[:, None] * inv[None, :]
    cos, sin = jnp.cos(ang), jnp.sin(ang)
    rest = HEAD_DIM - ROT_DIM
    zh = jnp.zeros((n, half), F32)
    cos_h = jnp.concatenate([cos, cos, jnp.ones((n, rest), F32)], axis=1)
    sa_h = jnp.concatenate([zh, sin, jnp.zeros((n, rest), F32)], axis=1)
    sb_h = jnp.concatenate([-sin, zh, jnp.zeros((n, rest), F32)], axis=1)
    rep = LANES // HEAD_DIM
    return tuple(jnp.tile(t, (1, rep)) for t in (cos_h, sa_h, sb_h))


def _lane_pad(v, n=LANES):
    return jnp.pad(v, (0, n - v.shape[0])).reshape(1, n)


def _to_cache(xt, batch, seq):
    return xt.reshape(1, batch, ATT_KV_HEADS, HEAD_DIM, seq).transpose(0, 1, 4, 2, 3)


def kernel(x_prompt, x_sample, cache_k, cache_v, state_conv, state_ssm, norm_w, w_in, conv_w,
           conv_b, dt_bias, a_log, d_skip, ssd_norm_w, q_norm_w, k_norm_w, w_out):
    bp, tp_, _ = x_prompt.shape
    bs, ts, _ = x_sample.shape
    depth = w_in.shape[0]
    assert depth == 1 and tp_ % SSD_CHUNK == 0 and ts <= SAMPLE_PAD and ts >= CONV_W - 1
    l = 0
    win = cache_k.shape[2]

    w_t = jnp.swapaxes(w_in[l], 0, 1).astype(BF16)
    w_o = w_out[l].astype(BF16)
    nw = norm_w[l].reshape(1, D_MODEL)
    rep = LANES // HEAD_DIM
    qnw = jnp.tile(q_norm_w[l], rep).reshape(1, LANES)
    knw = jnp.tile(k_norm_w[l], rep).reshape(1, LANES)
    cw = jnp.pad(conv_w[l], ((0, SUBLANES - CONV_W), (0, 0)))
    cb = conv_b[l].reshape(1, CONV_DIM)
    dtb = _lane_pad(dt_bias[l])
    alog = _lane_pad(a_log[l])
    dexp = jnp.repeat(d_skip[l], SSD_HEADDIM).reshape(1, SSD_INNER)
    snw = ssd_norm_w[l].reshape(1, SSD_INNER)

    tm = 256
    xp2 = x_prompt.reshape(bp * tp_, D_MODEL)
    tabs = _rope_tables(jnp.arange(tp_, dtype=jnp.int32))
    z, xbc, dtr, q, kt, vt, gate = _in_proj(xp2, nw, w_t, qnw, knw, tabs, tm, tp_)
    y_ssd, h_p = _ssd(xbc, dtr, z, jnp.zeros((bp, SUBLANES, CONV_DIM), F32),
                      jnp.zeros((bp, SSD_INNER, SSD_STATE), F32), cw, cb, dtb, alog, dexp, snw,
                      bp, SSD_CHUNK, SSD_CHUNK)
    y_att = _attn_prompt(q, kt, vt, gate, bp, tp_)
    y_p = _out_proj(xp2, y_ssd, y_att, w_o, tm).reshape(bp, tp_, D_MODEL)
    keep = min(W_MAX, tp_)
    k_p = _to_cache(kt, bp, tp_)[:, :, tp_ - keep:]
    v_p = _to_cache(vt, bp, tp_)[:, :, tp_ - keep:]
    c_p = xbc.reshape(bp, tp_, CONV_DIM)[:, tp_ - (CONV_W - 1):][None]
    h_p = h_p.reshape(1, bp, SSD_HEADS, SSD_HEADDIM, SSD_STATE)

    pad = SAMPLE_PAD
    xs2 = jnp.pad(x_sample, ((0, 0), (0, pad - ts), (0, 0))).reshape(bs * pad, D_MODEL)
    pos_s = PAST_LEN + (jnp.arange(bs * pad, dtype=jnp.int32) % pad)
    tabs_s = _rope_tables(pos_s)
    rows_s = bs * pad
    z, xbc, dtr, q, kt, vt, gate = _in_proj(xs2, nw, w_t, qnw, knw, tabs_s, rows_s, rows_s)
    cprev = jnp.pad(state_conv[l], ((0, 0), (SUBLANES - (CONV_W - 1), 0), (0, 0)))
    y_ssd, h_s = _ssd(xbc, dtr, z, cprev, state_ssm[l].reshape(bs, SSD_INNER, SSD_STATE),
                      cw, cb, dtb, alog, dexp, snw, bs, pad, ts)
    knt = kt.reshape(KV_DIM, bs, pad).transpose(1, 0, 2)
    vnt = vt.reshape(KV_DIM, bs, pad).transpose(1, 0, 2)
    ckt = cache_k[l].transpose(0, 2, 3, 1).reshape(bs, KV_DIM, win)
    cvt = cache_v[l].transpose(0, 2, 3, 1).reshape(bs, KV_DIM, win)
    y_att, kot, vot = _attn_sample(q, knt, vnt, gate, ckt, cvt, ts)
    y_s = _out_proj(xs2, y_ssd, y_att, w_o, rows_s).reshape(bs, pad, D_MODEL)[:, :ts]
    k_s = _to_cache(kot, bs, win)
    v_s = _to_cache(vot, bs, win)
    c_s = xbc.reshape(bs, pad, CONV_DIM)[:, ts - (CONV_W - 1):ts][None]
    h_s = h_s.reshape(1, bs, SSD_HEADS, SSD_HEADDIM, SSD_STATE)

    return (y_p, y_s, k_p, v_p, c_p, h_p, k_s, v_s, c_s, h_s)
```

```python
import functools

import jax
import jax.numpy as jnp
from jax import lax
from jax.experimental import pallas as pl
from jax.experimental.pallas import tpu as pltpu

F32 = jnp.float32
BF16 = jnp.bfloat16

D_MODEL = 2048
SSD_HEADS = 16
SSD_HEADDIM = 64
SSD_INNER = SSD_HEADS * SSD_HEADDIM
SSD_GROUPS = 2
SSD_STATE = 128
CONV_W = 4
CONV_DIM = SSD_INNER + 2 * SSD_GROUPS * SSD_STATE
SSD_CHUNK = 128
ATT_HEADS = 16
ATT_KV_HEADS = 4
HEAD_DIM = 64
ATT_GQ = ATT_HEADS // ATT_KV_HEADS
ATT_INNER = ATT_HEADS * HEAD_DIM
KV_DIM = ATT_KV_HEADS * HEAD_DIM
ROT_DIM = HEAD_DIM // 4
ROPE_THETA = 500000.0
DILATED_BRANCHES = ((128, 1), (512, 4), (2048, 16))
W_MAX = 2048
PAST_LEN = 16384
EPS = 1e-6

LANES = 128
SUBLANES = 8
Q_BLOCK = 256
K_SUPER = 256
Q_PER_SUPER = K_SUPER // Q_BLOCK
ONES_ROWS = 16
ACC_ROWS = HEAD_DIM + ONES_ROWS
CONV_COLS = 256
SSD_SEQS = 2
SAMPLE_PAD = 16
NEG = -1e30
VMEM_LIMIT = 56 * 1024 * 1024

Z0 = 0
X0 = Z0 + SSD_INNER
DT0 = X0 + CONV_DIM
Q0 = DT0 + SSD_HEADS
K0 = Q0 + ATT_INNER
V0 = K0 + KV_DIM
G0 = V0 + KV_DIM
W_ROWS = G0 + ATT_INNER
HALF_INNER = SSD_INNER // SSD_GROUPS
LOG2E = 1.4426950408889634
Q_SCALE = HEAD_DIM ** -0.5 * LOG2E


def _dot(a, b):
    return jnp.dot(a, b, preferred_element_type=F32)


def _dot_nt(a, b):
    return lax.dot_general(a, b, (((1,), (1,)), ((), ())), preferred_element_type=F32)


def _split3(x):
    hi = x.astype(BF16)
    r1 = x - hi.astype(F32)
    mid = r1.astype(BF16)
    lo = (r1 - mid.astype(F32)).astype(BF16)
    return hi, mid, lo


def _dot_exact_rhs(x, m):
    hi, mid, lo = _split3(x)
    return _dot(hi, m) + _dot(mid, m) + _dot(lo, m)


def _dot_wide_rhs(x, m):
    hi = x.astype(BF16)
    lo = (x - hi.astype(F32)).astype(BF16)
    return _dot(hi, m) + _dot(lo, m)


def _dot_exact_lhs(m, x):
    hi, mid, lo = _split3(x)
    return _dot(m, hi) + _dot(m, mid) + _dot(m, lo)


def _silu(x):
    hx = 0.5 * x
    return hx * jnp.tanh(hx) + hx


def _multiplicity(d):
    w = jnp.zeros(d.shape, F32)
    for window, dil in DILATED_BRANCHES:
        hit = (d >= 0) & (d <= window) & (lax.rem(d, dil) == 0)
        w = w + jnp.where(hit, 1.0, 0.0)
    return w


def _norm_rope(y, nw, cos, sa, sb, scale):
    lane = lax.broadcasted_iota(jnp.int32, (1, LANES), 1)
    first = lane < HEAD_DIM
    y2 = y * y
    s_lo = jnp.sum(jnp.where(first, y2, 0.0), axis=-1, keepdims=True)
    s_hi = jnp.sum(jnp.where(first, 0.0, y2), axis=-1, keepdims=True)
    ms = jnp.where(first, s_lo, s_hi) * (1.0 / HEAD_DIM)
    yn = y * lax.rsqrt(ms + EPS) * nw
    half = ROT_DIM // 2
    rot = yn * cos + pltpu.roll(yn, half, 1) * sa + pltpu.roll(yn, LANES - half, 1) * sb
    return rot * scale


def _inproj_kernel(x_ref, nw_ref, w_ref, qnw_ref, knw_ref, cos_ref, sa_ref, sb_ref,
                   z_ref, xbc_ref, dt_ref, q_ref, kt_ref, vt_ref, g_ref):
    x = x_ref[...]
    ms = jnp.mean(x * x, axis=-1, keepdims=True)
    hn = (x * lax.rsqrt(ms + EPS) * nw_ref[...]).astype(BF16)

    def proj(a, b):
        return _dot_nt(hn, w_ref[a:b, :])

    z_ref[...] = proj(Z0, X0)
    xbc_ref[...] = proj(X0, DT0)
    dt_ref[...] = proj(DT0, DT0 + LANES)
    vt_ref[0] = proj(V0, G0).T
    g_ref[...] = proj(G0, W_ROWS)
    cos, sa, sb = cos_ref[...], sa_ref[...], sb_ref[...]
    q = proj(Q0, K0)
    for c in range(ATT_INNER // LANES):
        sl = slice(c * LANES, (c + 1) * LANES)
        q_ref[:, sl] = _norm_rope(q[:, sl], qnw_ref[...], cos, sa, sb, Q_SCALE).astype(BF16)
    k = proj(K0, V0)
    kn = [_norm_rope(k[:, c * LANES:(c + 1) * LANES], knw_ref[...], cos, sa, sb, 1.0)
          for c in range(KV_DIM // LANES)]
    kt_ref[0] = jnp.concatenate(kn, axis=1).T


def _in_proj(x2d, nw, w_t, qnw, knw, tables, tm, seq):
    rows = x2d.shape[0]
    period = tables[0].shape[0] // tm
    per_seq = seq // tm
    const = lambda i: (0, 0)
    row = lambda i: (i, 0)
    tab = lambda i: (i % period, 0)
    tr = lambda i: (i // per_seq, 0, i % per_seq)
    row_outs = {0: (SSD_INNER, F32), 1: (CONV_DIM, F32), 2: (LANES, F32), 3: (ATT_INNER, BF16),
                6: (ATT_INNER, F32)}
    out_specs, out_shape = [], []
    for idx in range(7):
        if idx in row_outs:
            n, dt = row_outs[idx]
            out_specs.append(pl.BlockSpec((tm, n), row))
            out_shape.append(jax.ShapeDtypeStruct((rows, n), dt))
        else:
            out_specs.append(pl.BlockSpec((1, KV_DIM, tm), tr))
            out_shape.append(jax.ShapeDtypeStruct((rows // seq, KV_DIM, seq), F32))
    return pl.pallas_call(
        _inproj_kernel,
        grid=(rows // tm,),
        in_specs=[
            pl.BlockSpec((tm, D_MODEL), row),
            pl.BlockSpec((1, D_MODEL), const),
            pl.BlockSpec((W_ROWS, D_MODEL), const, pipeline_mode=pl.Buffered(1)),
            pl.BlockSpec((1, LANES), const),
            pl.BlockSpec((1, LANES), const),
            pl.BlockSpec((tm, LANES), tab),
            pl.BlockSpec((tm, LANES), tab),
            pl.BlockSpec((tm, LANES), tab),
        ],
        out_specs=out_specs,
        out_shape=out_shape,
        compiler_params=pltpu.CompilerParams(
            dimension_semantics=("arbitrary",), vmem_limit_bytes=VMEM_LIMIT),
        name="in_proj",
    )(x2d, nw, w_t, qnw, knw, *tables)


def _ssd_chain(s, xbc_ref, dtr_ref, z_ref, cprev_ref, h0_ref, cw_ref, cb_ref, dtb_ref, alog_ref,
               dexp_ref, nw_ref, y_ref, hout_ref, xext, ht, xc_scr, lin, valid):
    L = SSD_CHUNK
    c = pl.program_id(1)
    last = pl.num_programs(1) - 1

    @pl.when(c == 0)
    def _():
        xext[s, 0:SUBLANES, :] = cprev_ref[s]
        for g in range(SSD_GROUPS):
            ht[s, g] = h0_ref[s, g * HALF_INNER:(g + 1) * HALF_INNER, :].T

    def pad_rows(v):
        if lin == L:
            return v
        return jnp.concatenate([v, jnp.zeros((L - lin, v.shape[1]), v.dtype)], axis=0)

    lane = lax.broadcasted_iota(jnp.int32, (1, LANES), 1)
    rowid = lax.broadcasted_iota(jnp.int32, (L, 1), 0)
    dt = jax.nn.softplus(pad_rows(dtr_ref[s]) + dtb_ref[...])
    dt = jnp.where((lane < SSD_HEADS) & (rowid < valid), dt, 0.0)
    dta = dt * (-jnp.exp(alog_ref[...]) * LOG2E)

    r2 = lax.broadcasted_iota(jnp.int32, (L, L), 0)
    c2 = lax.broadcasted_iota(jnp.int32, (L, L), 1)
    tri = r2 >= c2
    tri_b = jnp.where(tri, 1.0, 0.0).astype(BF16)
    cum = _dot_exact_lhs(tri_b, dta)
    cum_last = cum[L - 1:L, :]
    yield

    er = lax.broadcasted_iota(jnp.int32, (LANES, SSD_INNER), 0)
    ec = lax.broadcasted_iota(jnp.int32, (LANES, SSD_INNER), 1)
    expand = jnp.where(ec // SSD_HEADDIM == er, 1.0, 0.0).astype(BF16)
    ecum = _dot_wide_rhs(jnp.exp2(cum), expand)
    wexp = _dot_wide_rhs(jnp.exp2(cum_last - cum) * dt, expand)
    cd = _dot_exact_rhs(jnp.broadcast_to(jnp.exp2(cum_last), (SUBLANES, LANES)), expand)[0:1, :]
    cum_t = cum.T
    dt_t = dt.T
    yield

    if lin < L:
        xext[s, SUBLANES + lin:SUBLANES + L, :] = jnp.zeros((L - lin, CONV_DIM), F32)
    xext[s, SUBLANES:SUBLANES + lin, :] = xbc_ref[s]
    for c0 in range(0, CONV_DIM, CONV_COLS):
        cs = slice(c0, c0 + CONV_COLS)
        xe = xext[s, 0:SUBLANES + L, cs]
        x2 = pltpu.roll(xe, 2, 0)
        even = cw_ref[3:4, cs] * xe + cw_ref[1:2, cs] * x2
        odd = cw_ref[2:3, cs] * xe + cw_ref[0:1, cs] * x2
        xc_scr[s, :, cs] = _silu((even + pltpu.roll(odd, 1, 0))[SUBLANES:, :] + cb_ref[:, cs])
    xext[s, 0:SUBLANES, :] = xext[s, L:L + SUBLANES, :]
    xs = xc_scr[s, :, :SSD_INNER]
    bm = xc_scr[s, :, SSD_INNER:SSD_INNER + SSD_GROUPS * SSD_STATE]
    cm = xc_scr[s, :, SSD_INNER + SSD_GROUPS * SSD_STATE:]
    bm_b = bm.astype(BF16)
    cm_b = cm.astype(BF16)
    xs_b = xs.astype(BF16)

    y_off = jnp.concatenate(
        [_dot(cm_b[:, g * SSD_STATE:(g + 1) * SSD_STATE], ht[s, g].astype(BF16))
         for g in range(SSD_GROUPS)], axis=1) * ecum
    cbs = [_dot_nt(cm_b[:, g * SSD_STATE:(g + 1) * SSD_STATE], bm_b[:, g * SSD_STATE:(g + 1) * SSD_STATE])
           for g in range(SSD_GROUPS)]
    yield

    lane_half = lane // HEAD_DIM
    heads_per_group = SSD_HEADS // SSD_GROUPS
    y_parts = []
    for g in range(SSD_GROUPS):
        for pr in range(heads_per_group // 2):
            col0 = g * HALF_INNER + pr * LANES
            xp = xs_b[:, col0:col0 + LANES]
            yp = jnp.zeros((L, LANES), F32)
            for e in range(2):
                hh = g * heads_per_group + pr * 2 + e
                decay = jnp.exp2(cum[:, hh:hh + 1] - cum_t[hh:hh + 1, :])
                sc = jnp.where(tri, cbs[g] * decay, 0.0) * dt_t[hh:hh + 1, :]
                xm = jnp.where(lane_half == e, xp, jnp.zeros_like(xp))
                yp = yp + _dot(sc.astype(BF16), xm)
            y_parts.append(yp)
            if pr % 2 == 1:
                yield
    y_diag = jnp.concatenate(y_parts, axis=1)

    xw = (xs * wexp).astype(BF16)
    for g in range(SSD_GROUPS):
        sl = slice(g * HALF_INNER, (g + 1) * HALF_INNER)
        bm_t = bm[:, g * SSD_STATE:(g + 1) * SSD_STATE].T.astype(BF16)
        ht[s, g] = ht[s, g] * cd[:, sl] + _dot(bm_t, xw[:, sl])
    yield

    y = (y_diag + y_off + dexp_ref[...] * xs) * _silu(pad_rows(z_ref[s]))
    yn = []
    for g in range(SSD_GROUPS):
        yg = y[:, g * HALF_INNER:(g + 1) * HALF_INNER]
        ms = jnp.mean(yg * yg, axis=-1, keepdims=True)
        yn.append(yg * lax.rsqrt(ms + EPS))
    yn = jnp.concatenate(yn, axis=1) * nw_ref[...]
    y_ref[s] = yn[:lin].astype(BF16)

    @pl.when(c == last)
    def _():
        for g in range(SSD_GROUPS):
            hout_ref[s, g * HALF_INNER:(g + 1) * HALF_INNER, :] = ht[s, g].T


def _ssd_kernel(*refs, lin, valid):
    chains = [_ssd_chain(s, *refs, lin, valid) for s in range(SSD_SEQS)]
    while chains:
        alive = []
        for ch in chains:
            try:
                next(ch)
                alive.append(ch)
            except StopIteration:
                pass
        chains = alive


def _ssd(xbc, dtr, z, cprev, h0, cw, cb, dtb, alog, dexp, nw, batch, lin, valid):
    rows = xbc.shape[0]
    seq = rows // batch
    nc = seq // lin
    g = SSD_SEQS
    assert batch % g == 0
    blk = lambda b, c: (b, c, 0)
    per_b = lambda b, c: (b, 0, 0)
    const = lambda b, c: (0, 0)
    y, hout = pl.pallas_call(
        functools.partial(_ssd_kernel, lin=lin, valid=valid),
        grid=(batch // g, nc),
        in_specs=[
            pl.BlockSpec((g, lin, CONV_DIM), blk),
            pl.BlockSpec((g, lin, LANES), blk),
            pl.BlockSpec((g, lin, SSD_INNER), blk),
            pl.BlockSpec((g, SUBLANES, CONV_DIM), per_b),
            pl.BlockSpec((g, SSD_INNER, SSD_STATE), per_b),
            pl.BlockSpec((SUBLANES, CONV_DIM), const),
            pl.BlockSpec((1, CONV_DIM), const),
            pl.BlockSpec((1, LANES), const),
            pl.BlockSpec((1, LANES), const),
            pl.BlockSpec((1, SSD_INNER), const),
            pl.BlockSpec((1, SSD_INNER), const),
        ],
        out_specs=[
            pl.BlockSpec((g, lin, SSD_INNER), blk),
            pl.BlockSpec((g, SSD_INNER, SSD_STATE), per_b),
        ],
        out_shape=[
            jax.ShapeDtypeStruct((batch, seq, SSD_INNER), BF16),
            jax.ShapeDtypeStruct((batch, SSD_INNER, SSD_STATE), F32),
        ],
        scratch_shapes=[
            pltpu.VMEM((g, SUBLANES + SSD_CHUNK + SUBLANES, CONV_DIM), F32),
            pltpu.VMEM((g, SSD_GROUPS, SSD_STATE, HALF_INNER), F32),
            pltpu.VMEM((g, SSD_CHUNK, CONV_DIM), F32),
        ],
        compiler_params=pltpu.CompilerParams(
            dimension_semantics=("arbitrary", "arbitrary"), vmem_limit_bytes=VMEM_LIMIT),
        name="ssd",
    )(xbc.reshape(batch, seq, CONV_DIM), dtr.reshape(batch, seq, LANES),
      z.reshape(batch, seq, SSD_INNER), cprev, h0, cw, cb, dtb, alog, dexp, nw)
    return y.reshape(rows, SSD_INNER), hout


def _attn_prompt_kernel(q_ref, kt_ref, vt_ref, g_ref, o_ref,
                        bias_scr, kh_scr, va_scr, qt_scr, m_scr, acc_scr, ot_scr, s_scr, *, n_tab, n_sb_max):
    b = pl.program_id(0)
    i = pl.program_id(1)
    seq = kt_ref.shape[2]
    pair_w = 2 * Q_BLOCK
    pairs_per_kv = ATT_GQ // 2

    @pl.when((b == 0) & (i == 0))
    def _():
        r = lax.broadcasted_iota(jnp.int32, (K_SUPER, Q_BLOCK), 0)
        c = lax.broadcasted_iota(jnp.int32, (K_SUPER, Q_BLOCK), 1)
        for tb in range(n_tab):
            w = _multiplicity(tb * Q_BLOCK + c - r)
            bias_scr[tb] = jnp.where(w > 0.0, jnp.log2(jnp.maximum(w, 1.0)), NEG)
        va_scr[:, HEAD_DIM:, :] = jnp.ones((ATT_KV_HEADS, ONES_ROWS, seq), BF16)

    @pl.when(i == 0)
    def _():
        for kvh in range(ATT_KV_HEADS):
            hs = slice(kvh * HEAD_DIM, (kvh + 1) * HEAD_DIM)
            va_scr[kvh, 0:HEAD_DIM, :] = vt_ref[0, hs, :].astype(BF16)
        for cidx in range(seq // K_SUPER):
            rows = slice(cidx * K_SUPER, (cidx + 1) * K_SUPER)
            kc = kt_ref[0, :, rows].T
            for kvh in range(ATT_KV_HEADS):
                hs = slice(kvh * HEAD_DIM, (kvh + 1) * HEAD_DIM)
                kh_scr[kvh, rows, :] = kc[:, hs].astype(BF16)

    qt = q_ref[...].astype(F32).T
    for kvh in range(ATT_KV_HEADS):
        qt_scr[kvh] = jnp.concatenate(
            [qt[(kvh * ATT_GQ + g) * HEAD_DIM:(kvh * ATT_GQ + g + 1) * HEAD_DIM, :]
             for g in range(ATT_GQ)], axis=1).astype(BF16)
    m_scr[...] = jnp.full(m_scr.shape, NEG, F32)
    acc_scr[...] = jnp.zeros(acc_scr.shape, F32)
    par = i % Q_PER_SUPER
    j_last = i // Q_PER_SUPER
    n_sb = jnp.minimum(j_last + 1, n_sb_max)

    def body(dl, carry):
        start = pl.multiple_of((j_last - dl) * K_SUPER, K_SUPER)
        bias = bias_scr[par + Q_PER_SUPER * dl]
        bias2 = jnp.concatenate([bias, bias], axis=1)
        for kvh in range(ATT_KV_HEADS):
            s_scr[kvh] = _dot(kh_scr[kvh, pl.ds(start, K_SUPER), :], qt_scr[kvh])
        for kvh in range(ATT_KV_HEADS):
            vtb = va_scr[kvh, :, pl.ds(start, K_SUPER)]
            for pr in range(pairs_per_kv):
                u = kvh * pairs_per_kv + pr
                s = s_scr[kvh, :, pr * pair_w:(pr + 1) * pair_w] + bias2
                m_old = m_scr[u, 0:1, :]
                m_new = jnp.maximum(m_old, jnp.max(s, axis=0, keepdims=True))
                p = jnp.exp2(s - m_new).astype(BF16)
                acc_scr[u] = jnp.exp2(m_old - m_new) * acc_scr[u] + _dot(vtb, p)
                m_scr[u, 0:1, :] = m_new
        return carry

    lax.fori_loop(0, n_sb, body, 0)

    for u in range(ATT_HEADS // 2):
        acc = acc_scr[u]
        o = acc[0:HEAD_DIM] / acc[HEAD_DIM:HEAD_DIM + 1]
        for e in range(2):
            h = 2 * u + e
            ot_scr[h * HEAD_DIM:(h + 1) * HEAD_DIM, :] = o[:, e * Q_BLOCK:(e + 1) * Q_BLOCK]
    o_ref[...] = (ot_scr[...].T * _silu(g_ref[...])).astype(BF16)


def _attn_prompt(q, kt, vt, gate, batch, seq):
    rows = q.shape[0]
    nqb = seq // Q_BLOCK
    assert seq % K_SUPER == 0
    n_sb_max = min(seq // K_SUPER, W_MAX // K_SUPER + 1)
    n_tab = Q_PER_SUPER * n_sb_max
    qblk = lambda b, i: (b * nqb + i, 0)
    per_b = lambda b, i: (b, 0, 0)
    return pl.pallas_call(
        functools.partial(_attn_prompt_kernel, n_tab=n_tab, n_sb_max=n_sb_max),
        grid=(batch, nqb),
        in_specs=[
            pl.BlockSpec((Q_BLOCK, ATT_INNER), qblk),
            pl.BlockSpec((1, KV_DIM, seq), per_b),
            pl.BlockSpec((1, KV_DIM, seq), per_b),
            pl.BlockSpec((Q_BLOCK, ATT_INNER), qblk),
        ],
        out_specs=pl.BlockSpec((Q_BLOCK, ATT_INNER), qblk),
        out_shape=jax.ShapeDtypeStruct((rows, ATT_INNER), BF16),
        scratch_shapes=[
            pltpu.VMEM((n_tab, K_SUPER, Q_BLOCK), F32),
            pltpu.VMEM((ATT_KV_HEADS, seq, HEAD_DIM), BF16),
            pltpu.VMEM((ATT_KV_HEADS, ACC_ROWS, seq), BF16),
            pltpu.VMEM((ATT_KV_HEADS, HEAD_DIM, ATT_GQ * Q_BLOCK), BF16),
            pltpu.VMEM((ATT_HEADS // 2, SUBLANES, 2 * Q_BLOCK), F32),
            pltpu.VMEM((ATT_HEADS // 2, ACC_ROWS, 2 * Q_BLOCK), F32),
            pltpu.VMEM((ATT_INNER, Q_BLOCK), F32),
            pltpu.VMEM((ATT_KV_HEADS, K_SUPER, ATT_GQ * Q_BLOCK), F32),
        ],
        compiler_params=pltpu.CompilerParams(
            dimension_semantics=("arbitrary", "arbitrary"), vmem_limit_bytes=VMEM_LIMIT),
        name="attn_prompt",
    )(q, kt, vt, gate)


def _attn_sample_kernel(q_ref, knt_ref, vnt_ref, g_ref, ckt_ref, cvt_ref, o_ref, kot_ref, vot_ref, *, ts):
    win = ckt_ref.shape[2]
    tp = SAMPLE_PAD
    rows = ATT_GQ * tp
    t_c = lax.broadcasted_iota(jnp.int32, (rows, win), 0) % tp
    j_c = lax.broadcasted_iota(jnp.int32, (rows, win), 1)
    w_c = _multiplicity(win + t_c - j_c)
    t_n = lax.broadcasted_iota(jnp.int32, (rows, tp), 0) % tp
    j_n = lax.broadcasted_iota(jnp.int32, (rows, tp), 1)
    w_n = jnp.where(j_n < ts, _multiplicity(t_n - j_n), 0.0)
    for kvh in range(ATT_KV_HEADS):
        c0 = kvh * ATT_GQ * HEAD_DIM
        qh = jnp.concatenate(
            [q_ref[:, c0 + g * HEAD_DIM:c0 + (g + 1) * HEAD_DIM] for g in range(ATT_GQ)], axis=0)
        hs = slice(kvh * HEAD_DIM, (kvh + 1) * HEAD_DIM)
        s_c = jnp.where(w_c > 0.0, _dot(qh, ckt_ref[0, hs, :].astype(BF16)), NEG)
        s_n = jnp.where(w_n > 0.0, _dot(qh, knt_ref[0, hs, :].astype(BF16)), NEG)
        m = jnp.maximum(jnp.max(s_c, axis=-1, keepdims=True), jnp.max(s_n, axis=-1, keepdims=True))
        p_c = jnp.exp2(s_c - m) * w_c
        p_n = jnp.exp2(s_n - m) * w_n
        den = jnp.sum(p_c, axis=-1, keepdims=True) + jnp.sum(p_n, axis=-1, keepdims=True)
        num = (_dot_nt(p_c.astype(BF16), cvt_ref[0, hs, :].astype(BF16))
               + _dot_nt(p_n.astype(BF16), vnt_ref[0, hs, :].astype(BF16)))
        o = num / den
        o = jnp.concatenate([o[g * tp:(g + 1) * tp] for g in range(ATT_GQ)], axis=1)
        csl = slice(c0, c0 + ATT_GQ * HEAD_DIM)
        o_ref[:, csl] = (o * _silu(g_ref[:, csl])).astype(BF16)
    kot_ref[0, :, 0:win - ts] = ckt_ref[0, :, ts:win]
    kot_ref[0, :, win - ts:win] = knt_ref[0, :, 0:ts]
    vot_ref[0, :, 0:win - ts] = cvt_ref[0, :, ts:win]
    vot_ref[0, :, win - ts:win] = vnt_ref[0, :, 0:ts]


def _attn_sample(q, knt, vnt, gate, ckt, cvt, ts):
    batch, _, win = ckt.shape
    tp = SAMPLE_PAD
    blk = lambda b: (b, 0)
    per_b = lambda b: (b, 0, 0)
    return pl.pallas_call(
        functools.partial(_attn_sample_kernel, ts=ts),
        grid=(batch,),
        in_specs=[
            pl.BlockSpec((tp, ATT_INNER), blk),
            pl.BlockSpec((1, KV_DIM, tp), per_b),
            pl.BlockSpec((1, KV_DIM, tp), per_b),
            pl.BlockSpec((tp, ATT_INNER), blk),
            pl.BlockSpec((1, KV_DIM, win), per_b),
            pl.BlockSpec((1, KV_DIM, win), per_b),
        ],
        out_specs=[
            pl.BlockSpec((tp, ATT_INNER), blk),
            pl.BlockSpec((1, KV_DIM, win), per_b),
            pl.BlockSpec((1, KV_DIM, win), per_b),
        ],
        out_shape=[
            jax.ShapeDtypeStruct((batch * tp, ATT_INNER), BF16),
            jax.ShapeDtypeStruct((batch, KV_DIM, win), F32),
            jax.ShapeDtypeStruct((batch, KV_DIM, win), F32),
        ],
        compiler_params=pltpu.CompilerParams(
            dimension_semantics=("arbitrary",), vmem_limit_bytes=VMEM_LIMIT),
        name="attn_sample",
    )(q, knt, vnt, gate, ckt, cvt)


def _outproj_kernel(x_ref, ys_ref, ya_ref, w_ref, o_ref):
    o_ref[...] = (x_ref[...] + _dot(ys_ref[...], w_ref[0:SSD_INNER, :])
                  + _dot(ya_ref[...], w_ref[SSD_INNER:, :]))


def _out_proj(x2d, y_ssd, y_att, w_o, tm):
    rows = x2d.shape[0]
    row = lambda i: (i, 0)
    const = lambda i: (0, 0)
    return pl.pallas_call(
        _outproj_kernel,
        grid=(rows // tm,),
        in_specs=[
            pl.BlockSpec((tm, D_MODEL), row),
            pl.BlockSpec((tm, SSD_INNER), row),
            pl.BlockSpec((tm, ATT_INNER), row),
            pl.BlockSpec((SSD_INNER + ATT_INNER, D_MODEL), const, pipeline_mode=pl.Buffered(1)),
        ],
        out_specs=pl.BlockSpec((tm, D_MODEL), row),
        out_shape=jax.ShapeDtypeStruct((rows, D_MODEL), F32),
        compiler_params=pltpu.CompilerParams(
            dimension_semantics=("arbitrary",), vmem_limit_bytes=VMEM_LIMIT),
        name="out_proj",
    )(x2d, y_ssd, y_att, w_o)


def _rope_tables(pos):
    n = pos.shape[0]
    half = ROT_DIM // 2
    inv = ROPE_THETA ** (-jnp.arange(0, ROT_DIM, 2, dtype=F32) / ROT_DIM)
    ang = pos.astype(F32)[:, None] * inv[None, :]
    cos, sin = jnp.cos(ang), jnp.sin(ang)
    rest = HEAD_DIM - ROT_DIM
    zh = jnp.zeros((n, half), F32)
    cos_h = jnp.concatenate([cos, cos, jnp.ones((n, rest), F32)], axis=1)
    sa_h = jnp.concatenate([zh, sin, jnp.zeros((n, rest), F32)], axis=1)
    sb_h = jnp.concatenate([-sin, zh, jnp.zeros((n, rest), F32)], axis=1)
    rep = LANES // HEAD_DIM
    return tuple(jnp.tile(t, (1, rep)) for t in (cos_h, sa_h, sb_h))


def _lane_pad(v, n=LANES):
    return jnp.pad(v, (0, n - v.shape[0])).reshape(1, n)


def _to_cache(xt, batch, seq):
    return xt.reshape(1, batch, ATT_KV_HEADS, HEAD_DIM, seq).transpose(0, 1, 4, 2, 3)


def kernel(x_prompt, x_sample, cache_k, cache_v, state_conv, state_ssm, norm_w, w_in, conv_w,
           conv_b, dt_bias, a_log, d_skip, ssd_norm_w, q_norm_w, k_norm_w, w_out):
    bp, tp_, _ = x_prompt.shape
    bs, ts, _ = x_sample.shape
    depth = w_in.shape[0]
    assert depth == 1 and tp_ % SSD_CHUNK == 0 and ts <= SAMPLE_PAD and ts >= CONV_W - 1
    l = 0
    win = cache_k.shape[2]

    w_t = jnp.swapaxes(w_in[l], 0, 1).astype(BF16)
    w_o = w_out[l].astype(BF16)
    nw = norm_w[l].reshape(1, D_MODEL)
    rep = LANES // HEAD_DIM
    qnw = jnp.tile(q_norm_w[l], rep).reshape(1, LANES)
    knw = jnp.tile(k_norm_w[l], rep).reshape(1, LANES)
    cw = jnp.pad(conv_w[l], ((0, SUBLANES - CONV_W), (0, 0)))
    cb = conv_b[l].reshape(1, CONV_DIM)
    dtb = _lane_pad(dt_bias[l])
    alog = _lane_pad(a_log[l])
    dexp = jnp.repeat(d_skip[l], SSD_HEADDIM).reshape(1, SSD_INNER)
    snw = ssd_norm_w[l].reshape(1, SSD_INNER)

    tm = 256
    xp2 = x_prompt.reshape(bp * tp_, D_MODEL)
    tabs = _rope_tables(jnp.arange(tp_, dtype=jnp.int32))
    z, xbc, dtr, q, kt, vt, gate = _in_proj(xp2, nw, w_t, qnw, knw, tabs, tm, tp_)
    y_ssd, h_p = _ssd(xbc, dtr, z, jnp.zeros((bp, SUBLANES, CONV_DIM), F32),
                      jnp.zeros((bp, SSD_INNER, SSD_STATE), F32), cw, cb, dtb, alog, dexp, snw,
                      bp, SSD_CHUNK, SSD_CHUNK)
    y_att = _attn_prompt(q, kt, vt, gate, bp, tp_)
    y_p = _out_proj(xp2, y_ssd, y_att, w_o, tm).reshape(bp, tp_, D_MODEL)
    keep = min(W_MAX, tp_)
    k_p = _to_cache(kt, bp, tp_)[:, :, tp_ - keep:]
    v_p = _to_cache(vt, bp, tp_)[:, :, tp_ - keep:]
    c_p = xbc.reshape(bp, tp_, CONV_DIM)[:, tp_ - (CONV_W - 1):][None]
    h_p = h_p.reshape(1, bp, SSD_HEADS, SSD_HEADDIM, SSD_STATE)

    pad = SAMPLE_PAD
    xs2 = jnp.pad(x_sample, ((0, 0), (0, pad - ts), (0, 0))).reshape(bs * pad, D_MODEL)
    pos_s = PAST_LEN + (jnp.arange(bs * pad, dtype=jnp.int32) % pad)
    tabs_s = _rope_tables(pos_s)
    rows_s = bs * pad
    z, xbc, dtr, q, kt, vt, gate = _in_proj(xs2, nw, w_t, qnw, knw, tabs_s, rows_s, rows_s)
    cprev = jnp.pad(state_conv[l], ((0, 0), (SUBLANES - (CONV_W - 1), 0), (0, 0)))
    y_ssd, h_s = _ssd(xbc, dtr, z, cprev, state_ssm[l].reshape(bs, SSD_INNER, SSD_STATE),
                      cw, cb, dtb, alog, dexp, snw, bs, pad, ts)
    knt = kt.reshape(KV_DIM, bs, pad).transpose(1, 0, 2)
    vnt = vt.reshape(KV_DIM, bs, pad).transpose(1, 0, 2)
    ckt = cache_k[l].transpose(0, 2, 3, 1).reshape(bs, KV_DIM, win)
    cvt = cache_v[l].transpose(0, 2, 3, 1).reshape(bs, KV_DIM, win)
    y_att, kot, vot = _attn_sample(q, knt, vnt, gate, ckt, cvt, ts)
    y_s = _out_proj(xs2, y_ssd, y_att, w_o, rows_s).reshape(bs, pad, D_MODEL)[:, :ts]
    k_s = _to_cache(kot, bs, win)
    v_s = _to_cache(vot, bs, win)
    c_s = xbc.reshape(bs, pad, CONV_DIM)[:, ts - (CONV_W - 1):ts][None]
    h_s = h_s.reshape(1, bs, SSD_HEADS, SSD_HEADDIM, SSD_STATE)

    return (y_p, y_s, k_p, v_p, c_p, h_p, k_s, v_s, c_s, h_s)
```

```python
import functools

import jax
import jax.numpy as jnp
from jax import lax
from jax.experimental import pallas as pl
from jax.experimental.pallas import tpu as pltpu

F32 = jnp.float32
BF16 = jnp.bfloat16

D_MODEL = 2048
SSD_HEADS = 16
SSD_HEADDIM = 64
SSD_INNER = SSD_HEADS * SSD_HEADDIM
SSD_GROUPS = 2
SSD_STATE = 128
CONV_W = 4
CONV_DIM = SSD_INNER + 2 * SSD_GROUPS * SSD_STATE
SSD_CHUNK = 128
ATT_HEADS = 16
ATT_KV_HEADS = 4
HEAD_DIM = 64
ATT_GQ = ATT_HEADS // ATT_KV_HEADS
ATT_INNER = ATT_HEADS * HEAD_DIM
KV_DIM = ATT_KV_HEADS * HEAD_DIM
ROT_DIM = HEAD_DIM // 4
ROPE_THETA = 500000.0
DILATED_BRANCHES = ((128, 1), (512, 4), (2048, 16))
W_MAX = 2048
PAST_LEN = 16384
EPS = 1e-6

LANES = 128
SUBLANES = 8
Q_BLOCK = 256
K_SUPER = 256
Q_PER_SUPER = K_SUPER // Q_BLOCK
ONES_ROWS = 16
ACC_ROWS = HEAD_DIM + ONES_ROWS
CONV_COLS = 256
SSD_SEQS = 2
SAMPLE_PAD = 16
NEG = -1e30
VMEM_LIMIT = 56 * 1024 * 1024

Z0 = 0
X0 = Z0 + SSD_INNER
DT0 = X0 + CONV_DIM
Q0 = DT0 + SSD_HEADS
K0 = Q0 + ATT_INNER
V0 = K0 + KV_DIM
G0 = V0 + KV_DIM
W_ROWS = G0 + ATT_INNER
HALF_INNER = SSD_INNER // SSD_GROUPS
LOG2E = 1.4426950408889634
Q_SCALE = HEAD_DIM ** -0.5 * LOG2E


def _dot(a, b):
    return jnp.dot(a, b, preferred_element_type=F32)


def _dot_nt(a, b):
    return lax.dot_general(a, b, (((1,), (1,)), ((), ())), preferred_element_type=F32)


def _split3(x):
    hi = x.astype(BF16)
    r1 = x - hi.astype(F32)
    mid = r1.astype(BF16)
    lo = (r1 - mid.astype(F32)).astype(BF16)
    return hi, mid, lo


def _dot_exact_rhs(x, m):
    hi, mid, lo = _split3(x)
    return _dot(hi, m) + _dot(mid, m) + _dot(lo, m)


def _dot_wide_rhs(x, m):
    hi = x.astype(BF16)
    lo = (x - hi.astype(F32)).astype(BF16)
    return _dot(hi, m) + _dot(lo, m)


def _dot_exact_lhs(m, x):
    hi, mid, lo = _split3(x)
    return _dot(m, hi) + _dot(m, mid) + _dot(m, lo)


def _silu(x):
    hx = 0.5 * x
    return hx * jnp.tanh(hx) + hx


def _multiplicity(d):
    w = jnp.zeros(d.shape, F32)
    for window, dil in DILATED_BRANCHES:
        hit = (d >= 0) & (d <= window) & (lax.rem(d, dil) == 0)
        w = w + jnp.where(hit, 1.0, 0.0)
    return w


def _norm_rope(y, nw, cos, sa, sb, scale):
    lane = lax.broadcasted_iota(jnp.int32, (1, LANES), 1)
    first = lane < HEAD_DIM
    y2 = y * y
    s_lo = jnp.sum(jnp.where(first, y2, 0.0), axis=-1, keepdims=True)
    s_hi = jnp.sum(jnp.where(first, 0.0, y2), axis=-1, keepdims=True)
    ms = jnp.where(first, s_lo, s_hi) * (1.0 / HEAD_DIM)
    yn = y * lax.rsqrt(ms + EPS) * nw
    half = ROT_DIM // 2
    rot = yn * cos + pltpu.roll(yn, half, 1) * sa + pltpu.roll(yn, LANES - half, 1) * sb
    return rot * scale


def _causal_conv_silu(load, cw_ref, cb_ref, store):
    for c0 in range(0, CONV_DIM, CONV_COLS):
        cs = slice(c0, c0 + CONV_COLS)
        xe = load(cs)
        x2 = pltpu.roll(xe, 2, 0)
        even = cw_ref[3:4, cs] * xe + cw_ref[1:2, cs] * x2
        odd = cw_ref[2:3, cs] * xe + cw_ref[0:1, cs] * x2
        store(cs, _silu((even + pltpu.roll(odd, 1, 0))[SUBLANES:, :] + cb_ref[:, cs]))


def _inproj_kernel(x_ref, nw_ref, w_ref, qnw_ref, knw_ref, cos_ref, sa_ref, sb_ref, cprev_ref, cw_ref,
                   cb_ref, zs_ref, xc_ref, dt_ref, q_ref, kt_ref, vt_ref, gs_ref, ctail_ref, xext,
                   *, fuse_conv, per_seq):
    tm = x_ref.shape[0]
    if fuse_conv:
        @pl.when(pl.program_id(0) % per_seq == 0)
        def _():
            xext[0:SUBLANES, :] = cprev_ref[0]

    x = x_ref[...]
    ms = jnp.mean(x * x, axis=-1, keepdims=True)
    hn = (x * lax.rsqrt(ms + EPS) * nw_ref[...]).astype(BF16)

    def proj(a, b):
        return _dot_nt(hn, w_ref[a:b, :])

    cos, sa, sb = cos_ref[...], sa_ref[...], sb_ref[...]
    q = proj(Q0, K0)
    for c in range(ATT_INNER // LANES):
        sl = slice(c * LANES, (c + 1) * LANES)
        q_ref[:, sl] = _norm_rope(q[:, sl], qnw_ref[...], cos, sa, sb, Q_SCALE).astype(BF16)
    k = proj(K0, V0)
    kn = [_norm_rope(k[:, c * LANES:(c + 1) * LANES], knw_ref[...], cos, sa, sb, 1.0)
          for c in range(KV_DIM // LANES)]
    kt_ref[0] = jnp.concatenate(kn, axis=1).T

    if fuse_conv:
        def load(cs):
            xext[SUBLANES:SUBLANES + tm, cs] = proj(X0 + cs.start, X0 + cs.stop)
            return xext[0:SUBLANES + tm, cs]

        def store(cs, v):
            xc_ref[:, cs] = v
            tail = xext[tm:tm + SUBLANES, cs]
            ctail_ref[0, :, cs] = tail
            xext[0:SUBLANES, cs] = tail

        _causal_conv_silu(load, cw_ref, cb_ref, store)
    else:
        xbc = proj(X0, DT0)
        xc_ref[...] = xbc
        ctail_ref[0] = xbc[tm - SUBLANES:tm, :]

    vt_ref[0] = proj(V0, G0).T
    zs_ref[...] = _silu(proj(Z0, X0))
    gs_ref[...] = _silu(proj(G0, W_ROWS))
    dt_ref[...] = proj(DT0, DT0 + LANES)


def _in_proj(x2d, nw, w_t, qnw, knw, tables, cprev, cw, cb, tm, seq, fuse_conv):
    rows = x2d.shape[0]
    period = tables[0].shape[0] // tm
    per_seq = seq // tm
    const = lambda i: (0, 0)
    row = lambda i: (i, 0)
    tab = lambda i: (i % period, 0)
    tr = lambda i: (i // per_seq, 0, i % per_seq)
    slab = lambda i: (i // per_seq, 0, 0)
    row_outs = {0: (SSD_INNER, F32), 1: (CONV_DIM, F32), 2: (LANES, F32), 3: (ATT_INNER, BF16),
                6: (ATT_INNER, F32)}
    out_specs, out_shape = [], []
    for idx in range(7):
        if idx in row_outs:
            n, dt = row_outs[idx]
            out_specs.append(pl.BlockSpec((tm, n), row))
            out_shape.append(jax.ShapeDtypeStruct((rows, n), dt))
        else:
            out_specs.append(pl.BlockSpec((1, KV_DIM, tm), tr))
            out_shape.append(jax.ShapeDtypeStruct((rows // seq, KV_DIM, seq), F32))
    out_specs.append(pl.BlockSpec((1, SUBLANES, CONV_DIM), slab))
    out_shape.append(jax.ShapeDtypeStruct((rows // seq, SUBLANES, CONV_DIM), F32))
    return pl.pallas_call(
        functools.partial(_inproj_kernel, fuse_conv=fuse_conv, per_seq=per_seq),
        grid=(rows // tm,),
        in_specs=[
            pl.BlockSpec((tm, D_MODEL), row),
            pl.BlockSpec((1, D_MODEL), const),
            pl.BlockSpec((W_ROWS, D_MODEL), const, pipeline_mode=pl.Buffered(1)),
            pl.BlockSpec((1, LANES), const),
            pl.BlockSpec((1, LANES), const),
            pl.BlockSpec((tm, LANES), tab),
            pl.BlockSpec((tm, LANES), tab),
            pl.BlockSpec((tm, LANES), tab),
            pl.BlockSpec((1, SUBLANES, CONV_DIM), slab),
            pl.BlockSpec((SUBLANES, CONV_DIM), const),
            pl.BlockSpec((1, CONV_DIM), const),
        ],
        out_specs=out_specs,
        out_shape=out_shape,
        scratch_shapes=[pltpu.VMEM((SUBLANES + tm + SUBLANES, CONV_DIM), F32)],
        compiler_params=pltpu.CompilerParams(
            dimension_semantics=("arbitrary",), vmem_limit_bytes=VMEM_LIMIT),
        name="in_proj",
    )(x2d, nw, w_t, qnw, knw, *tables, cprev, cw, cb)


def _ssd_chain(s, xbc_ref, dtr_ref, z_ref, cprev_ref, h0_ref, cw_ref, cb_ref, dtb_ref, alog_ref,
               dexp_ref, nw_ref, y_ref, hout_ref, xext, ht, xc_scr, lin, valid, conv_done):
    L = SSD_CHUNK
    c = pl.program_id(1)
    last = pl.num_programs(1) - 1

    @pl.when(c == 0)
    def _():
        xext[s, 0:SUBLANES, :] = cprev_ref[s]
        for g in range(SSD_GROUPS):
            ht[s, g] = h0_ref[s, g * HALF_INNER:(g + 1) * HALF_INNER, :].T

    def pad_rows(v):
        if lin == L:
            return v
        return jnp.concatenate([v, jnp.zeros((L - lin, v.shape[1]), v.dtype)], axis=0)

    lane = lax.broadcasted_iota(jnp.int32, (1, LANES), 1)
    rowid = lax.broadcasted_iota(jnp.int32, (L, 1), 0)
    dt = jax.nn.softplus(pad_rows(dtr_ref[s]) + dtb_ref[...])
    dt = jnp.where((lane < SSD_HEADS) & (rowid < valid), dt, 0.0)
    dta = dt * (-jnp.exp(alog_ref[...]) * LOG2E)

    r2 = lax.broadcasted_iota(jnp.int32, (L, L), 0)
    c2 = lax.broadcasted_iota(jnp.int32, (L, L), 1)
    tri = r2 >= c2
    tri_b = jnp.where(tri, 1.0, 0.0).astype(BF16)
    cum = _dot_exact_lhs(tri_b, dta)
    cum_last = cum[L - 1:L, :]
    yield

    er = lax.broadcasted_iota(jnp.int32, (LANES, SSD_INNER), 0)
    ec = lax.broadcasted_iota(jnp.int32, (LANES, SSD_INNER), 1)
    expand = jnp.where(ec // SSD_HEADDIM == er, 1.0, 0.0).astype(BF16)
    ecum = _dot_wide_rhs(jnp.exp2(cum), expand)
    wexp = _dot_wide_rhs(jnp.exp2(cum_last - cum) * dt, expand)
    cd = _dot_exact_rhs(jnp.broadcast_to(jnp.exp2(cum_last), (SUBLANES, LANES)), expand)[0:1, :]
    cum_t = cum.T
    dt_t = dt.T
    yield

    if conv_done:
        xc = xbc_ref.at[s]
    else:
        if lin < L:
            xext[s, SUBLANES + lin:SUBLANES + L, :] = jnp.zeros((L - lin, CONV_DIM), F32)
        xext[s, SUBLANES:SUBLANES + lin, :] = xbc_ref[s]

        def load(cs):
            return xext[s, 0:SUBLANES + L, cs]

        def store(cs, v):
            xc_scr[s, :, cs] = v

        _causal_conv_silu(load, cw_ref, cb_ref, store)
        xext[s, 0:SUBLANES, :] = xext[s, L:L + SUBLANES, :]
        xc = xc_scr.at[s]
    xs = xc[:, :SSD_INNER]
    bm = xc[:, SSD_INNER:SSD_INNER + SSD_GROUPS * SSD_STATE]
    cm = xc[:, SSD_INNER + SSD_GROUPS * SSD_STATE:]
    bm_b = bm.astype(BF16)
    cm_b = cm.astype(BF16)
    xs_b = xs.astype(BF16)

    y_off = jnp.concatenate(
        [_dot(cm_b[:, g * SSD_STATE:(g + 1) * SSD_STATE], ht[s, g].astype(BF16))
         for g in range(SSD_GROUPS)], axis=1) * ecum
    cbs = [_dot_nt(cm_b[:, g * SSD_STATE:(g + 1) * SSD_STATE], bm_b[:, g * SSD_STATE:(g + 1) * SSD_STATE])
           for g in range(SSD_GROUPS)]
    yield

    lane_half = lane // HEAD_DIM
    heads_per_group = SSD_HEADS // SSD_GROUPS
    y_parts = []
    for g in range(SSD_GROUPS):
        for pr in range(heads_per_group // 2):
            col0 = g * HALF_INNER + pr * LANES
            xp = xs_b[:, col0:col0 + LANES]
            yp = jnp.zeros((L, LANES), F32)
            for e in range(2):
                hh = g * heads_per_group + pr * 2 + e
                decay = jnp.exp2(cum[:, hh:hh + 1] - cum_t[hh:hh + 1, :])
                sc = jnp.where(tri, cbs[g] * decay, 0.0) * dt_t[hh:hh + 1, :]
                xm = jnp.where(lane_half == e, xp, jnp.zeros_like(xp))
                yp = yp + _dot(sc.astype(BF16), xm)
            y_parts.append(yp)
            if pr % 2 == 1:
                yield
    y_diag = jnp.concatenate(y_parts, axis=1)

    xw = (xs * wexp).astype(BF16)
    for g in range(SSD_GROUPS):
        sl = slice(g * HALF_INNER, (g + 1) * HALF_INNER)
        bm_t = bm[:, g * SSD_STATE:(g + 1) * SSD_STATE].T.astype(BF16)
        ht[s, g] = ht[s, g] * cd[:, sl] + _dot(bm_t, xw[:, sl])
    yield

    y = (y_diag + y_off + dexp_ref[...] * xs) * pad_rows(z_ref[s])
    yn = []
    for g in range(SSD_GROUPS):
        yg = y[:, g * HALF_INNER:(g + 1) * HALF_INNER]
        ms = jnp.mean(yg * yg, axis=-1, keepdims=True)
        yn.append(yg * lax.rsqrt(ms + EPS))
    yn = jnp.concatenate(yn, axis=1) * nw_ref[...]
    y_ref[s] = yn[:lin].astype(BF16)

    @pl.when(c == last)
    def _():
        for g in range(SSD_GROUPS):
            hout_ref[s, g * HALF_INNER:(g + 1) * HALF_INNER, :] = ht[s, g].T


def _ssd_kernel(*refs, lin, valid, conv_done):
    chains = [_ssd_chain(s, *refs, lin, valid, conv_done) for s in range(SSD_SEQS)]
    while chains:
        alive = []
        for ch in chains:
            try:
                next(ch)
                alive.append(ch)
            except StopIteration:
                pass
        chains = alive


def _ssd(xbc, dtr, z, cprev, h0, cw, cb, dtb, alog, dexp, nw, batch, lin, valid, conv_done):
    rows = xbc.shape[0]
    seq = rows // batch
    nc = seq // lin
    g = SSD_SEQS
    assert batch % g == 0
    blk = lambda b, c: (b, c, 0)
    per_b = lambda b, c: (b, 0, 0)
    const = lambda b, c: (0, 0)
    y, hout = pl.pallas_call(
        functools.partial(_ssd_kernel, lin=lin, valid=valid, conv_done=conv_done),
        grid=(batch // g, nc),
        in_specs=[
            pl.BlockSpec((g, lin, CONV_DIM), blk),
            pl.BlockSpec((g, lin, LANES), blk),
            pl.BlockSpec((g, lin, SSD_INNER), blk),
            pl.BlockSpec((g, SUBLANES, CONV_DIM), per_b),
            pl.BlockSpec((g, SSD_INNER, SSD_STATE), per_b),
            pl.BlockSpec((SUBLANES, CONV_DIM), const),
            pl.BlockSpec((1, CONV_DIM), const),
            pl.BlockSpec((1, LANES), const),
            pl.BlockSpec((1, LANES), const),
            pl.BlockSpec((1, SSD_INNER), const),
            pl.BlockSpec((1, SSD_INNER), const),
        ],
        out_specs=[
            pl.BlockSpec((g, lin, SSD_INNER), blk),
            pl.BlockSpec((g, SSD_INNER, SSD_STATE), per_b),
        ],
        out_shape=[
            jax.ShapeDtypeStruct((batch, seq, SSD_INNER), BF16),
            jax.ShapeDtypeStruct((batch, SSD_INNER, SSD_STATE), F32),
        ],
        scratch_shapes=[
            pltpu.VMEM((g, SUBLANES + SSD_CHUNK + SUBLANES, CONV_DIM), F32),
            pltpu.VMEM((g, SSD_GROUPS, SSD_STATE, HALF_INNER), F32),
            pltpu.VMEM((g, SSD_CHUNK, CONV_DIM), F32),
        ],
        compiler_params=pltpu.CompilerParams(
            dimension_semantics=("arbitrary", "arbitrary"), vmem_limit_bytes=VMEM_LIMIT),
        name="ssd",
    )(xbc.reshape(batch, seq, CONV_DIM), dtr.reshape(batch, seq, LANES),
      z.reshape(batch, seq, SSD_INNER), cprev, h0, cw, cb, dtb, alog, dexp, nw)
    return y.reshape(rows, SSD_INNER), hout


def _attn_prompt_kernel(q_ref, kt_ref, vt_ref, g_ref, o_ref,
                        bias_scr, kh_scr, va_scr, qt_scr, m_scr, acc_scr, ot_scr, s_scr, *, n_tab, n_sb_max):
    b = pl.program_id(0)
    i = pl.program_id(1)
    seq = kt_ref.shape[2]
    pair_w = 2 * Q_BLOCK
    pairs_per_kv = ATT_GQ // 2

    @pl.when((b == 0) & (i == 0))
    def _():
        r = lax.broadcasted_iota(jnp.int32, (K_SUPER, Q_BLOCK), 0)
        c = lax.broadcasted_iota(jnp.int32, (K_SUPER, Q_BLOCK), 1)
        for tb in range(n_tab):
            w = _multiplicity(tb * Q_BLOCK + c - r)
            bias_scr[tb] = jnp.where(w > 0.0, jnp.log2(jnp.maximum(w, 1.0)), NEG)
        va_scr[:, HEAD_DIM:, :] = jnp.ones((ATT_KV_HEADS, ONES_ROWS, seq), BF16)

    @pl.when(i == 0)
    def _():
        for kvh in range(ATT_KV_HEADS):
            hs = slice(kvh * HEAD_DIM, (kvh + 1) * HEAD_DIM)
            va_scr[kvh, 0:HEAD_DIM, :] = vt_ref[0, hs, :].astype(BF16)
        for cidx in range(seq // K_SUPER):
            rows = slice(cidx * K_SUPER, (cidx + 1) * K_SUPER)
            kc = kt_ref[0, :, rows].T
            for kvh in range(ATT_KV_HEADS):
                hs = slice(kvh * HEAD_DIM, (kvh + 1) * HEAD_DIM)
                kh_scr[kvh, rows, :] = kc[:, hs].astype(BF16)

    qt = q_ref[...].astype(F32).T
    for kvh in range(ATT_KV_HEADS):
        qt_scr[kvh] = jnp.concatenate(
            [qt[(kvh * ATT_GQ + g) * HEAD_DIM:(kvh * ATT_GQ + g + 1) * HEAD_DIM, :]
             for g in range(ATT_GQ)], axis=1).astype(BF16)
    m_scr[...] = jnp.full(m_scr.shape, NEG, F32)
    acc_scr[...] = jnp.zeros(acc_scr.shape, F32)
    par = i % Q_PER_SUPER
    j_last = i // Q_PER_SUPER
    n_sb = jnp.minimum(j_last + 1, n_sb_max)

    def body(dl, carry):
        start = pl.multiple_of((j_last - dl) * K_SUPER, K_SUPER)
        bias = bias_scr[par + Q_PER_SUPER * dl]
        bias2 = jnp.concatenate([bias, bias], axis=1)
        for kvh in range(ATT_KV_HEADS):
            s_scr[kvh] = _dot(kh_scr[kvh, pl.ds(start, K_SUPER), :], qt_scr[kvh])
        for kvh in range(ATT_KV_HEADS):
            vtb = va_scr[kvh, :, pl.ds(start, K_SUPER)]
            for pr in range(pairs_per_kv):
                u = kvh * pairs_per_kv + pr
                s = s_scr[kvh, :, pr * pair_w:(pr + 1) * pair_w] + bias2
                m_old = m_scr[u, 0:1, :]
                m_new = jnp.maximum(m_old, jnp.max(s, axis=0, keepdims=True))
                p = jnp.exp2(s - m_new).astype(BF16)
                acc_scr[u] = jnp.exp2(m_old - m_new) * acc_scr[u] + _dot(vtb, p)
                m_scr[u, 0:1, :] = m_new
        return carry

    lax.fori_loop(0, n_sb, body, 0)

    for u in range(ATT_HEADS // 2):
        acc = acc_scr[u]
        o = acc[0:HEAD_DIM] / acc[HEAD_DIM:HEAD_DIM + 1]
        for e in range(2):
            h = 2 * u + e
            ot_scr[h * HEAD_DIM:(h + 1) * HEAD_DIM, :] = o[:, e * Q_BLOCK:(e + 1) * Q_BLOCK]
    o_ref[...] = (ot_scr[...].T * g_ref[...]).astype(BF16)


def _attn_prompt(q, kt, vt, gate, batch, seq):
    rows = q.shape[0]
    nqb = seq // Q_BLOCK
    assert seq % K_SUPER == 0
    n_sb_max = min(seq // K_SUPER, W_MAX // K_SUPER + 1)
    n_tab = Q_PER_SUPER * n_sb_max
    qblk = lambda b, i: (b * nqb + i, 0)
    per_b = lambda b, i: (b, 0, 0)
    return pl.pallas_call(
        functools.partial(_attn_prompt_kernel, n_tab=n_tab, n_sb_max=n_sb_max),
        grid=(batch, nqb),
        in_specs=[
            pl.BlockSpec((Q_BLOCK, ATT_INNER), qblk),
            pl.BlockSpec((1, KV_DIM, seq), per_b),
            pl.BlockSpec((1, KV_DIM, seq), per_b),
            pl.BlockSpec((Q_BLOCK, ATT_INNER), qblk),
        ],
        out_specs=pl.BlockSpec((Q_BLOCK, ATT_INNER), qblk),
        out_shape=jax.ShapeDtypeStruct((rows, ATT_INNER), BF16),
        scratch_shapes=[
            pltpu.VMEM((n_tab, K_SUPER, Q_BLOCK), F32),
            pltpu.VMEM((ATT_KV_HEADS, seq, HEAD_DIM), BF16),
            pltpu.VMEM((ATT_KV_HEADS, ACC_ROWS, seq), BF16),
            pltpu.VMEM((ATT_KV_HEADS, HEAD_DIM, ATT_GQ * Q_BLOCK), BF16),
            pltpu.VMEM((ATT_HEADS // 2, SUBLANES, 2 * Q_BLOCK), F32),
            pltpu.VMEM((ATT_HEADS // 2, ACC_ROWS, 2 * Q_BLOCK), F32),
            pltpu.VMEM((ATT_INNER, Q_BLOCK), F32),
            pltpu.VMEM((ATT_KV_HEADS, K_SUPER, ATT_GQ * Q_BLOCK), F32),
        ],
        compiler_params=pltpu.CompilerParams(
            dimension_semantics=("arbitrary", "arbitrary"), vmem_limit_bytes=VMEM_LIMIT),
        name="attn_prompt",
    )(q, kt, vt, gate)


def _attn_sample_kernel(q_ref, knt_ref, vnt_ref, g_ref, ckt_ref, cvt_ref, o_ref, kot_ref, vot_ref, *, ts):
    win = ckt_ref.shape[2]
    tp = SAMPLE_PAD
    rows = ATT_GQ * tp
    t_c = lax.broadcasted_iota(jnp.int32, (rows, win), 0) % tp
    j_c = lax.broadcasted_iota(jnp.int32, (rows, win), 1)
    w_c = _multiplicity(win + t_c - j_c)
    t_n = lax.broadcasted_iota(jnp.int32, (rows, tp), 0) % tp
    j_n = lax.broadcasted_iota(jnp.int32, (rows, tp), 1)
    w_n = jnp.where(j_n < ts, _multiplicity(t_n - j_n), 0.0)
    for kvh in range(ATT_KV_HEADS):
        c0 = kvh * ATT_GQ * HEAD_DIM
        qh = jnp.concatenate(
            [q_ref[:, c0 + g * HEAD_DIM:c0 + (g + 1) * HEAD_DIM] for g in range(ATT_GQ)], axis=0)
        hs = slice(kvh * HEAD_DIM, (kvh + 1) * HEAD_DIM)
        s_c = jnp.where(w_c > 0.0, _dot(qh, ckt_ref[0, hs, :].astype(BF16)), NEG)
        s_n = jnp.where(w_n > 0.0, _dot(qh, knt_ref[0, hs, :].astype(BF16)), NEG)
        m = jnp.maximum(jnp.max(s_c, axis=-1, keepdims=True), jnp.max(s_n, axis=-1, keepdims=True))
        p_c = jnp.exp2(s_c - m) * w_c
        p_n = jnp.exp2(s_n - m) * w_n
        den = jnp.sum(p_c, axis=-1, keepdims=True) + jnp.sum(p_n, axis=-1, keepdims=True)
        num = (_dot_nt(p_c.astype(BF16), cvt_ref[0, hs, :].astype(BF16))
               + _dot_nt(p_n.astype(BF16), vnt_ref[0, hs, :].astype(BF16)))
        o = num / den
        o = jnp.concatenate([o[g * tp:(g + 1) * tp] for g in range(ATT_GQ)], axis=1)
        csl = slice(c0, c0 + ATT_GQ * HEAD_DIM)
        o_ref[:, csl] = (o * g_ref[:, csl]).astype(BF16)
    kot_ref[0, :, 0:win - ts] = ckt_ref[0, :, ts:win]
    kot_ref[0, :, win - ts:win] = knt_ref[0, :, 0:ts]
    vot_ref[0, :, 0:win - ts] = cvt_ref[0, :, ts:win]
    vot_ref[0, :, win - ts:win] = vnt_ref[0, :, 0:ts]


def _attn_sample(q, knt, vnt, gate, ckt, cvt, ts):
    batch, _, win = ckt.shape
    tp = SAMPLE_PAD
    blk = lambda b: (b, 0)
    per_b = lambda b: (b, 0, 0)
    return pl.pallas_call(
        functools.partial(_attn_sample_kernel, ts=ts),
        grid=(batch,),
        in_specs=[
            pl.BlockSpec((tp, ATT_INNER), blk),
            pl.BlockSpec((1, KV_DIM, tp), per_b),
            pl.BlockSpec((1, KV_DIM, tp), per_b),
            pl.BlockSpec((tp, ATT_INNER), blk),
            pl.BlockSpec((1, KV_DIM, win), per_b),
            pl.BlockSpec((1, KV_DIM, win), per_b),
        ],
        out_specs=[
            pl.BlockSpec((tp, ATT_INNER), blk),
            pl.BlockSpec((1, KV_DIM, win), per_b),
            pl.BlockSpec((1, KV_DIM, win), per_b),
        ],
        out_shape=[
            jax.ShapeDtypeStruct((batch * tp, ATT_INNER), BF16),
            jax.ShapeDtypeStruct((batch, KV_DIM, win), F32),
            jax.ShapeDtypeStruct((batch, KV_DIM, win), F32),
        ],
        compiler_params=pltpu.CompilerParams(
            dimension_semantics=("arbitrary",), vmem_limit_bytes=VMEM_LIMIT),
        name="attn_sample",
    )(q, knt, vnt, gate, ckt, cvt)


def _outproj_kernel(x_ref, ys_ref, ya_ref, w_ref, o_ref):
    o_ref[...] = (x_ref[...] + _dot(ys_ref[...], w_ref[0:SSD_INNER, :])
                  + _dot(ya_ref[...], w_ref[SSD_INNER:, :]))


def _out_proj(x2d, y_ssd, y_att, w_o, tm):
    rows = x2d.shape[0]
    row = lambda i: (i, 0)
    const = lambda i: (0, 0)
    return pl.pallas_call(
        _outproj_kernel,
        grid=(rows // tm,),
        in_specs=[
            pl.BlockSpec((tm, D_MODEL), row),
            pl.BlockSpec((tm, SSD_INNER), row),
            pl.BlockSpec((tm, ATT_INNER), row),
            pl.BlockSpec((SSD_INNER + ATT_INNER, D_MODEL), const, pipeline_mode=pl.Buffered(1)),
        ],
        out_specs=pl.BlockSpec((tm, D_MODEL), row),
        out_shape=jax.ShapeDtypeStruct((rows, D_MODEL), F32),
        compiler_params=pltpu.CompilerParams(
            dimension_semantics=("arbitrary",), vmem_limit_bytes=VMEM_LIMIT),
        name="out_proj",
    )(x2d, y_ssd, y_att, w_o)


def _rope_tables(pos):
    n = pos.shape[0]
    half = ROT_DIM // 2
    inv = ROPE_THETA ** (-jnp.arange(0, ROT_DIM, 2, dtype=F32) / ROT_DIM)
    ang = pos.astype(F32)[:, None] * inv[None, :]
    cos, sin = jnp.cos(ang), jnp.sin(ang)
    rest = HEAD_DIM - ROT_DIM
    zh = jnp.zeros((n, half), F32)
    cos_h = jnp.concatenate([cos, cos, jnp.ones((n, rest), F32)], axis=1)
    sa_h = jnp.concatenate([zh, sin, jnp.zeros((n, rest), F32)], axis=1)
    sb_h = jnp.concatenate([-sin, zh, jnp.zeros((n, rest), F32)], axis=1)
    rep = LANES // HEAD_DIM
    return tuple(jnp.tile(t, (1, rep)) for t in (cos_h, sa_h, sb_h))


def _lane_pad(v, n=LANES):
    return jnp.pad(v, (0, n - v.shape[0])).reshape(1, n)


def _to_cache(xt, batch, seq):
    return xt.reshape(1, batch, ATT_KV_HEADS, HEAD_DIM, seq).transpose(0, 1, 4, 2, 3)


def kernel(x_prompt, x_sample, cache_k, cache_v, state_conv, state_ssm, norm_w, w_in, conv_w,
           conv_b, dt_bias, a_log, d_skip, ssd_norm_w, q_norm_w, k_norm_w, w_out):
    bp, tp_, _ = x_prompt.shape
    bs, ts, _ = x_sample.shape
    depth = w_in.shape[0]
    assert depth == 1 and tp_ % SSD_CHUNK == 0 and ts <= SAMPLE_PAD and ts >= CONV_W - 1
    l = 0
    win = cache_k.shape[2]

    w_t = jnp.swapaxes(w_in[l], 0, 1).astype(BF16)
    w_o = w_out[l].astype(BF16)
    nw = norm_w[l].reshape(1, D_MODEL)
    rep = LANES // HEAD_DIM
    qnw = jnp.tile(q_norm_w[l], rep).reshape(1, LANES)
    knw = jnp.tile(k_norm_w[l], rep).reshape(1, LANES)
    cw = jnp.pad(conv_w[l], ((0, SUBLANES - CONV_W), (0, 0)))
    cb = conv_b[l].reshape(1, CONV_DIM)
    dtb = _lane_pad(dt_bias[l])
    alog = _lane_pad(a_log[l])
    dexp = jnp.repeat(d_skip[l], SSD_HEADDIM).reshape(1, SSD_INNER)
    snw = ssd_norm_w[l].reshape(1, SSD_INNER)

    tm = 256
    xp2 = x_prompt.reshape(bp * tp_, D_MODEL)
    tabs = _rope_tables(jnp.arange(tp_, dtype=jnp.int32))
    cprev_p = jnp.zeros((bp, SUBLANES, CONV_DIM), F32)
    zs, xc, dtr, q, kt, vt, gs, ctail = _in_proj(xp2, nw, w_t, qnw, knw, tabs, cprev_p, cw, cb,
                                                 tm, tp_, True)
    y_ssd, h_p = _ssd(xc, dtr, zs, cprev_p, jnp.zeros((bp, SSD_INNER, SSD_STATE), F32),
                      cw, cb, dtb, alog, dexp, snw, bp, SSD_CHUNK, SSD_CHUNK, True)
    y_att = _attn_prompt(q, kt, vt, gs, bp, tp_)
    y_p = _out_proj(xp2, y_ssd, y_att, w_o, tm).reshape(bp, tp_, D_MODEL)
    keep = min(W_MAX, tp_)
    k_p = _to_cache(kt, bp, tp_)[:, :, tp_ - keep:]
    v_p = _to_cache(vt, bp, tp_)[:, :, tp_ - keep:]
    c_p = ctail[:, SUBLANES - (CONV_W - 1):][None]
    h_p = h_p.reshape(1, bp, SSD_HEADS, SSD_HEADDIM, SSD_STATE)

    pad = SAMPLE_PAD
    xs2 = jnp.pad(x_sample, ((0, 0), (0, pad - ts), (0, 0))).reshape(bs * pad, D_MODEL)
    pos_s = PAST_LEN + (jnp.arange(bs * pad, dtype=jnp.int32) % pad)
    tabs_s = _rope_tables(pos_s)
    rows_s = bs * pad
    zs, xbc, dtr, q, kt, vt, gs, _ = _in_proj(xs2, nw, w_t, qnw, knw, tabs_s,
                                              jnp.zeros((1, SUBLANES, CONV_DIM), F32), cw, cb,
                                              rows_s, rows_s, False)
    cprev = jnp.pad(state_conv[l], ((0, 0), (SUBLANES - (CONV_W - 1), 0), (0, 0)))
    y_ssd, h_s = _ssd(xbc, dtr, zs, cprev, state_ssm[l].reshape(bs, SSD_INNER, SSD_STATE),
                      cw, cb, dtb, alog, dexp, snw, bs, pad, ts, False)
    knt = kt.reshape(KV_DIM, bs, pad).transpose(1, 0, 2)
    vnt = vt.reshape(KV_DIM, bs, pad).transpose(1, 0, 2)
    ckt = cache_k[l].transpose(0, 2, 3, 1).reshape(bs, KV_DIM, win)
    cvt = cache_v[l].transpose(0, 2, 3, 1).reshape(bs, KV_DIM, win)
    y_att, kot, vot = _attn_sample(q, knt, vnt, gs, ckt, cvt, ts)
    y_s = _out_proj(xs2, y_ssd, y_att, w_o, rows_s).reshape(bs, pad, D_MODEL)[:, :ts]
    k_s = _to_cache(kot, bs, win)
    v_s = _to_cache(vot, bs, win)
    c_s = xbc.reshape(bs, pad, CONV_DIM)[:, ts - (CONV_W - 1):ts][None]
    h_s = h_s.reshape(1, bs, SSD_HEADS, SSD_HEADDIM, SSD_STATE)

    return (y_p, y_s, k_p, v_p, c_p, h_p, k_s, v_s, c_s, h_s)
```

```python
import functools

import jax
import jax.numpy as jnp
from jax import lax
from jax.experimental import pallas as pl
from jax.experimental.pallas import tpu as pltpu

F32 = jnp.float32
BF16 = jnp.bfloat16

D_MODEL = 2048
SSD_HEADS = 16
SSD_HEADDIM = 64
SSD_INNER = SSD_HEADS * SSD_HEADDIM
SSD_GROUPS = 2
SSD_STATE = 128
CONV_W = 4
CONV_DIM = SSD_INNER + 2 * SSD_GROUPS * SSD_STATE
SSD_CHUNK = 128
ATT_HEADS = 16
ATT_KV_HEADS = 4
HEAD_DIM = 64
ATT_GQ = ATT_HEADS // ATT_KV_HEADS
ATT_INNER = ATT_HEADS * HEAD_DIM
KV_DIM = ATT_KV_HEADS * HEAD_DIM
ROT_DIM = HEAD_DIM // 4
ROPE_THETA = 500000.0
DILATED_BRANCHES = ((128, 1), (512, 4), (2048, 16))
W_MAX = 2048
PAST_LEN = 16384
EPS = 1e-6

LANES = 128
SUBLANES = 8
Q_BLOCK = 256
K_SUPER = 256
Q_PER_SUPER = K_SUPER // Q_BLOCK
ONES_ROWS = 16
ACC_ROWS = HEAD_DIM + ONES_ROWS
CONV_COLS = 256
SSD_SEQS = 2
FUSE_CONV_PROMPT = False
SAMPLE_PAD = 16
NEG = -1e30
VMEM_LIMIT = 56 * 1024 * 1024

Z0 = 0
X0 = Z0 + SSD_INNER
DT0 = X0 + CONV_DIM
Q0 = DT0 + SSD_HEADS
K0 = Q0 + ATT_INNER
V0 = K0 + KV_DIM
G0 = V0 + KV_DIM
W_ROWS = G0 + ATT_INNER
HALF_INNER = SSD_INNER // SSD_GROUPS
LOG2E = 1.4426950408889634
Q_SCALE = HEAD_DIM ** -0.5 * LOG2E


def _dot(a, b):
    return jnp.dot(a, b, preferred_element_type=F32)


def _dot_nt(a, b):
    return lax.dot_general(a, b, (((1,), (1,)), ((), ())), preferred_element_type=F32)


def _split3(x):
    hi = x.astype(BF16)
    r1 = x - hi.astype(F32)
    mid = r1.astype(BF16)
    lo = (r1 - mid.astype(F32)).astype(BF16)
    return hi, mid, lo


def _dot_exact_rhs(x, m):
    hi, mid, lo = _split3(x)
    return _dot(hi, m) + _dot(mid, m) + _dot(lo, m)


def _dot_wide_rhs(x, m):
    hi = x.astype(BF16)
    lo = (x - hi.astype(F32)).astype(BF16)
    return _dot(hi, m) + _dot(lo, m)


def _dot_exact_lhs(m, x):
    hi, mid, lo = _split3(x)
    return _dot(m, hi) + _dot(m, mid) + _dot(m, lo)


def _silu(x):
    hx = 0.5 * x
    return hx * jnp.tanh(hx) + hx


def _multiplicity(d):
    w = jnp.zeros(d.shape, F32)
    for window, dil in DILATED_BRANCHES:
        hit = (d >= 0) & (d <= window) & (lax.rem(d, dil) == 0)
        w = w + jnp.where(hit, 1.0, 0.0)
    return w


def _norm_rope(y, nw, cos, sa, sb, scale):
    lane = lax.broadcasted_iota(jnp.int32, (1, LANES), 1)
    first = lane < HEAD_DIM
    y2 = y * y
    s_lo = jnp.sum(jnp.where(first, y2, 0.0), axis=-1, keepdims=True)
    s_hi = jnp.sum(jnp.where(first, 0.0, y2), axis=-1, keepdims=True)
    ms = jnp.where(first, s_lo, s_hi) * (1.0 / HEAD_DIM)
    yn = y * lax.rsqrt(ms + EPS) * nw
    half = ROT_DIM // 2
    rot = yn * cos + pltpu.roll(yn, half, 1) * sa + pltpu.roll(yn, LANES - half, 1) * sb
    return rot * scale


def _causal_conv_silu(load, cw_ref, cb_ref, store):
    for c0 in range(0, CONV_DIM, CONV_COLS):
        cs = slice(c0, c0 + CONV_COLS)
        xe = load(cs)
        x2 = pltpu.roll(xe, 2, 0)
        even = cw_ref[3:4, cs] * xe + cw_ref[1:2, cs] * x2
        odd = cw_ref[2:3, cs] * xe + cw_ref[0:1, cs] * x2
        store(cs, _silu((even + pltpu.roll(odd, 1, 0))[SUBLANES:, :] + cb_ref[:, cs]))


def _inproj_kernel(x_ref, nw_ref, w_ref, qnw_ref, knw_ref, cos_ref, sa_ref, sb_ref, cprev_ref, cw_ref,
                   cb_ref, zs_ref, xc_ref, dt_ref, q_ref, kt_ref, vt_ref, gs_ref, ctail_ref, xext,
                   *, fuse_conv, per_seq):
    tm = x_ref.shape[0]
    if fuse_conv:
        @pl.when(pl.program_id(0) % per_seq == 0)
        def _():
            xext[0:SUBLANES, :] = cprev_ref[0]

    x = x_ref[...]
    ms = jnp.mean(x * x, axis=-1, keepdims=True)
    hn = (x * lax.rsqrt(ms + EPS) * nw_ref[...]).astype(BF16)

    def proj(a, b):
        return _dot_nt(hn, w_ref[a:b, :])

    cos, sa, sb = cos_ref[...], sa_ref[...], sb_ref[...]
    q = proj(Q0, K0)
    for c in range(ATT_INNER // LANES):
        sl = slice(c * LANES, (c + 1) * LANES)
        q_ref[:, sl] = _norm_rope(q[:, sl], qnw_ref[...], cos, sa, sb, Q_SCALE).astype(BF16)
    k = proj(K0, V0)
    kn = [_norm_rope(k[:, c * LANES:(c + 1) * LANES], knw_ref[...], cos, sa, sb, 1.0)
          for c in range(KV_DIM // LANES)]
    kt_ref[0] = jnp.concatenate(kn, axis=1).T

    if fuse_conv:
        def load(cs):
            xext[SUBLANES:SUBLANES + tm, cs] = proj(X0 + cs.start, X0 + cs.stop)
            return xext[0:SUBLANES + tm, cs]

        def store(cs, v):
            xc_ref[:, cs] = v
            tail = xext[tm:tm + SUBLANES, cs]
            ctail_ref[0, :, cs] = tail
            xext[0:SUBLANES, cs] = tail

        _causal_conv_silu(load, cw_ref, cb_ref, store)
    else:
        xbc = proj(X0, DT0)
        xc_ref[...] = xbc
        ctail_ref[0] = xbc[tm - SUBLANES:tm, :]

    vt_ref[0] = proj(V0, G0).T
    zs_ref[...] = _silu(proj(Z0, X0))
    gs_ref[...] = _silu(proj(G0, W_ROWS))
    dt_ref[...] = proj(DT0, DT0 + LANES)


def _in_proj(x2d, nw, w_t, qnw, knw, tables, cprev, cw, cb, tm, seq, fuse_conv):
    rows = x2d.shape[0]
    period = tables[0].shape[0] // tm
    per_seq = seq // tm
    const = lambda i: (0, 0)
    row = lambda i: (i, 0)
    tab = lambda i: (i % period, 0)
    tr = lambda i: (i // per_seq, 0, i % per_seq)
    slab = lambda i: (i // per_seq, 0, 0)
    row_outs = {0: (SSD_INNER, F32), 1: (CONV_DIM, F32), 2: (LANES, F32), 3: (ATT_INNER, BF16),
                6: (ATT_INNER, F32)}
    out_specs, out_shape = [], []
    for idx in range(7):
        if idx in row_outs:
            n, dt = row_outs[idx]
            out_specs.append(pl.BlockSpec((tm, n), row))
            out_shape.append(jax.ShapeDtypeStruct((rows, n), dt))
        else:
            out_specs.append(pl.BlockSpec((1, KV_DIM, tm), tr))
            out_shape.append(jax.ShapeDtypeStruct((rows // seq, KV_DIM, seq), F32))
    out_specs.append(pl.BlockSpec((1, SUBLANES, CONV_DIM), slab))
    out_shape.append(jax.ShapeDtypeStruct((rows // seq, SUBLANES, CONV_DIM), F32))
    return pl.pallas_call(
        functools.partial(_inproj_kernel, fuse_conv=fuse_conv, per_seq=per_seq),
        grid=(rows // tm,),
        in_specs=[
            pl.BlockSpec((tm, D_MODEL), row),
            pl.BlockSpec((1, D_MODEL), const),
            pl.BlockSpec((W_ROWS, D_MODEL), const, pipeline_mode=pl.Buffered(1)),
            pl.BlockSpec((1, LANES), const),
            pl.BlockSpec((1, LANES), const),
            pl.BlockSpec((tm, LANES), tab),
            pl.BlockSpec((tm, LANES), tab),
            pl.BlockSpec((tm, LANES), tab),
            pl.BlockSpec((1, SUBLANES, CONV_DIM), slab),
            pl.BlockSpec((SUBLANES, CONV_DIM), const),
            pl.BlockSpec((1, CONV_DIM), const),
        ],
        out_specs=out_specs,
        out_shape=out_shape,
        scratch_shapes=[pltpu.VMEM((SUBLANES + tm + SUBLANES, CONV_DIM), F32)],
        compiler_params=pltpu.CompilerParams(
            dimension_semantics=("arbitrary",), vmem_limit_bytes=VMEM_LIMIT),
        name="in_proj",
    )(x2d, nw, w_t, qnw, knw, *tables, cprev, cw, cb)


def _ssd_chain(s, xbc_ref, dtr_ref, z_ref, cprev_ref, h0_ref, cw_ref, cb_ref, dtb_ref, alog_ref,
               dexp_ref, nw_ref, y_ref, hout_ref, xext, ht, xc_scr, lin, valid, conv_done):
    L = SSD_CHUNK
    c = pl.program_id(1)
    last = pl.num_programs(1) - 1

    @pl.when(c == 0)
    def _():
        xext[s, 0:SUBLANES, :] = cprev_ref[s]
        for g in range(SSD_GROUPS):
            ht[s, g] = h0_ref[s, g * HALF_INNER:(g + 1) * HALF_INNER, :].T

    def pad_rows(v):
        if lin == L:
            return v
        return jnp.concatenate([v, jnp.zeros((L - lin, v.shape[1]), v.dtype)], axis=0)

    lane = lax.broadcasted_iota(jnp.int32, (1, LANES), 1)
    rowid = lax.broadcasted_iota(jnp.int32, (L, 1), 0)
    dt = jax.nn.softplus(pad_rows(dtr_ref[s]) + dtb_ref[...])
    dt = jnp.where((lane < SSD_HEADS) & (rowid < valid), dt, 0.0)
    dta = dt * (-jnp.exp(alog_ref[...]) * LOG2E)

    r2 = lax.broadcasted_iota(jnp.int32, (L, L), 0)
    c2 = lax.broadcasted_iota(jnp.int32, (L, L), 1)
    tri = r2 >= c2
    tri_b = jnp.where(tri, 1.0, 0.0).astype(BF16)
    cum = _dot_exact_lhs(tri_b, dta)
    cum_last = cum[L - 1:L, :]
    yield

    er = lax.broadcasted_iota(jnp.int32, (LANES, SSD_INNER), 0)
    ec = lax.broadcasted_iota(jnp.int32, (LANES, SSD_INNER), 1)
    expand = jnp.where(ec // SSD_HEADDIM == er, 1.0, 0.0).astype(BF16)
    ecum = _dot_wide_rhs(jnp.exp2(cum), expand)
    wexp = _dot_wide_rhs(jnp.exp2(cum_last - cum) * dt, expand)
    cd = _dot_exact_rhs(jnp.broadcast_to(jnp.exp2(cum_last), (SUBLANES, LANES)), expand)[0:1, :]
    cum_t = cum.T
    dt_t = dt.T
    yield

    if conv_done:
        xc = xbc_ref.at[s]
    else:
        if lin < L:
            xext[s, SUBLANES + lin:SUBLANES + L, :] = jnp.zeros((L - lin, CONV_DIM), F32)
        xext[s, SUBLANES:SUBLANES + lin, :] = xbc_ref[s]

        def load(cs):
            return xext[s, 0:SUBLANES + L, cs]

        def store(cs, v):
            xc_scr[s, :, cs] = v

        _causal_conv_silu(load, cw_ref, cb_ref, store)
        xext[s, 0:SUBLANES, :] = xext[s, L:L + SUBLANES, :]
        xc = xc_scr.at[s]
    xs = xc[:, :SSD_INNER]
    bm = xc[:, SSD_INNER:SSD_INNER + SSD_GROUPS * SSD_STATE]
    cm = xc[:, SSD_INNER + SSD_GROUPS * SSD_STATE:]
    bm_b = bm.astype(BF16)
    cm_b = cm.astype(BF16)
    xs_b = xs.astype(BF16)

    y_off = jnp.concatenate(
        [_dot(cm_b[:, g * SSD_STATE:(g + 1) * SSD_STATE], ht[s, g].astype(BF16))
         for g in range(SSD_GROUPS)], axis=1) * ecum
    cbs = [_dot_nt(cm_b[:, g * SSD_STATE:(g + 1) * SSD_STATE], bm_b[:, g * SSD_STATE:(g + 1) * SSD_STATE])
           for g in range(SSD_GROUPS)]
    yield

    lane_half = lane // HEAD_DIM
    heads_per_group = SSD_HEADS // SSD_GROUPS
    y_parts = []
    for g in range(SSD_GROUPS):
        for pr in range(heads_per_group // 2):
            col0 = g * HALF_INNER + pr * LANES
            xp = xs_b[:, col0:col0 + LANES]
            yp = jnp.zeros((L, LANES), F32)
            for e in range(2):
                hh = g * heads_per_group + pr * 2 + e
                decay = jnp.exp2(cum[:, hh:hh + 1] - cum_t[hh:hh + 1, :])
                sc = jnp.where(tri, cbs[g] * decay, 0.0) * dt_t[hh:hh + 1, :]
                xm = jnp.where(lane_half == e, xp, jnp.zeros_like(xp))
                yp = yp + _dot(sc.astype(BF16), xm)
            y_parts.append(yp)
            if pr % 2 == 1:
                yield
    y_diag = jnp.concatenate(y_parts, axis=1)

    xw = (xs * wexp).astype(BF16)
    for g in range(SSD_GROUPS):
        sl = slice(g * HALF_INNER, (g + 1) * HALF_INNER)
        bm_t = bm[:, g * SSD_STATE:(g + 1) * SSD_STATE].T.astype(BF16)
        ht[s, g] = ht[s, g] * cd[:, sl] + _dot(bm_t, xw[:, sl])
    yield

    y = (y_diag + y_off + dexp_ref[...] * xs) * pad_rows(z_ref[s])
    yn = []
    for g in range(SSD_GROUPS):
        yg = y[:, g * HALF_INNER:(g + 1) * HALF_INNER]
        ms = jnp.mean(yg * yg, axis=-1, keepdims=True)
        yn.append(yg * lax.rsqrt(ms + EPS))
    yn = jnp.concatenate(yn, axis=1) * nw_ref[...]
    y_ref[s] = yn[:lin].astype(BF16)

    @pl.when(c == last)
    def _():
        for g in range(SSD_GROUPS):
            hout_ref[s, g * HALF_INNER:(g + 1) * HALF_INNER, :] = ht[s, g].T


def _ssd_kernel(*refs, lin, valid, conv_done):
    chains = [_ssd_chain(s, *refs, lin, valid, conv_done) for s in range(SSD_SEQS)]
    while chains:
        alive = []
        for ch in chains:
            try:
                next(ch)
                alive.append(ch)
            except StopIteration:
                pass
        chains = alive


def _ssd(xbc, dtr, z, cprev, h0, cw, cb, dtb, alog, dexp, nw, batch, lin, valid, conv_done):
    rows = xbc.shape[0]
    seq = rows // batch
    nc = seq // lin
    g = SSD_SEQS
    assert batch % g == 0
    blk = lambda b, c: (b, c, 0)
    per_b = lambda b, c: (b, 0, 0)
    const = lambda b, c: (0, 0)
    y, hout = pl.pallas_call(
        functools.partial(_ssd_kernel, lin=lin, valid=valid, conv_done=conv_done),
        grid=(batch // g, nc),
        in_specs=[
            pl.BlockSpec((g, lin, CONV_DIM), blk),
            pl.BlockSpec((g, lin, LANES), blk),
            pl.BlockSpec((g, lin, SSD_INNER), blk),
            pl.BlockSpec((g, SUBLANES, CONV_DIM), per_b),
            pl.BlockSpec((g, SSD_INNER, SSD_STATE), per_b),
            pl.BlockSpec((SUBLANES, CONV_DIM), const),
            pl.BlockSpec((1, CONV_DIM), const),
            pl.BlockSpec((1, LANES), const),
            pl.BlockSpec((1, LANES), const),
            pl.BlockSpec((1, SSD_INNER), const),
            pl.BlockSpec((1, SSD_INNER), const),
        ],
        out_specs=[
            pl.BlockSpec((g, lin, SSD_INNER), blk),
            pl.BlockSpec((g, SSD_INNER, SSD_STATE), per_b),
        ],
        out_shape=[
            jax.ShapeDtypeStruct((batch, seq, SSD_INNER), BF16),
            jax.ShapeDtypeStruct((batch, SSD_INNER, SSD_STATE), F32),
        ],
        scratch_shapes=[
            pltpu.VMEM((g, SUBLANES + SSD_CHUNK + SUBLANES, CONV_DIM), F32),
            pltpu.VMEM((g, SSD_GROUPS, SSD_STATE, HALF_INNER), F32),
            pltpu.VMEM((g, SSD_CHUNK, CONV_DIM), F32),
        ],
        compiler_params=pltpu.CompilerParams(
            dimension_semantics=("arbitrary", "arbitrary"), vmem_limit_bytes=VMEM_LIMIT),
        name="ssd",
    )(xbc.reshape(batch, seq, CONV_DIM), dtr.reshape(batch, seq, LANES),
      z.reshape(batch, seq, SSD_INNER), cprev, h0, cw, cb, dtb, alog, dexp, nw)
    return y.reshape(rows, SSD_INNER), hout


def _attn_prompt_kernel(q_ref, kt_ref, vt_ref, g_ref, o_ref,
                        bias_scr, kh_scr, va_scr, qt_scr, m_scr, acc_scr, ot_scr, s_scr, *, n_tab, n_sb_max):
    b = pl.program_id(0)
    i = pl.program_id(1)
    seq = kt_ref.shape[2]
    pair_w = 2 * Q_BLOCK
    pairs_per_kv = ATT_GQ // 2

    @pl.when((b == 0) & (i == 0))
    def _():
        r = lax.broadcasted_iota(jnp.int32, (K_SUPER, Q_BLOCK), 0)
        c = lax.broadcasted_iota(jnp.int32, (K_SUPER, Q_BLOCK), 1)
        for tb in range(n_tab):
            w = _multiplicity(tb * Q_BLOCK + c - r)
            bias_scr[tb] = jnp.where(w > 0.0, jnp.log2(jnp.maximum(w, 1.0)), NEG)
        va_scr[:, HEAD_DIM:, :] = jnp.ones((ATT_KV_HEADS, ONES_ROWS, seq), BF16)

    @pl.when(i == 0)
    def _():
        for kvh in range(ATT_KV_HEADS):
            hs = slice(kvh * HEAD_DIM, (kvh + 1) * HEAD_DIM)
            va_scr[kvh, 0:HEAD_DIM, :] = vt_ref[0, hs, :].astype(BF16)
        for cidx in range(seq // K_SUPER):
            rows = slice(cidx * K_SUPER, (cidx + 1) * K_SUPER)
            kc = kt_ref[0, :, rows].T
            for kvh in range(ATT_KV_HEADS):
                hs = slice(kvh * HEAD_DIM, (kvh + 1) * HEAD_DIM)
                kh_scr[kvh, rows, :] = kc[:, hs].astype(BF16)

    qt = q_ref[...].astype(F32).T
    for kvh in range(ATT_KV_HEADS):
        qt_scr[kvh] = jnp.concatenate(
            [qt[(kvh * ATT_GQ + g) * HEAD_DIM:(kvh * ATT_GQ + g + 1) * HEAD_DIM, :]
             for g in range(ATT_GQ)], axis=1).astype(BF16)
    m_scr[...] = jnp.full(m_scr.shape, NEG, F32)
    acc_scr[...] = jnp.zeros(acc_scr.shape, F32)
    par = i % Q_PER_SUPER
    j_last = i // Q_PER_SUPER
    n_sb = jnp.minimum(j_last + 1, n_sb_max)

    def body(dl, carry):
        start = pl.multiple_of((j_last - dl) * K_SUPER, K_SUPER)
        bias = bias_scr[par + Q_PER_SUPER * dl]
        bias2 = jnp.concatenate([bias, bias], axis=1)
        for kvh in range(ATT_KV_HEADS):
            s_scr[kvh] = _dot(kh_scr[kvh, pl.ds(start, K_SUPER), :], qt_scr[kvh])
        for kvh in range(ATT_KV_HEADS):
            vtb = va_scr[kvh, :, pl.ds(start, K_SUPER)]
            for pr in range(pairs_per_kv):
                u = kvh * pairs_per_kv + pr
                s = s_scr[kvh, :, pr * pair_w:(pr + 1) * pair_w] + bias2
                m_old = m_scr[u, 0:1, :]
                m_new = jnp.maximum(m_old, jnp.max(s, axis=0, keepdims=True))
                p = jnp.exp2(s - m_new).astype(BF16)
                acc_scr[u] = jnp.exp2(m_old - m_new) * acc_scr[u] + _dot(vtb, p)
                m_scr[u, 0:1, :] = m_new
        return carry

    lax.fori_loop(0, n_sb, body, 0)

    for u in range(ATT_HEADS // 2):
        acc = acc_scr[u]
        o = acc[0:HEAD_DIM] / acc[HEAD_DIM:HEAD_DIM + 1]
        for e in range(2):
            h = 2 * u + e
            ot_scr[h * HEAD_DIM:(h + 1) * HEAD_DIM, :] = o[:, e * Q_BLOCK:(e + 1) * Q_BLOCK]
    o_ref[...] = (ot_scr[...].T * g_ref[...]).astype(BF16)


def _attn_prompt(q, kt, vt, gate, batch, seq):
    rows = q.shape[0]
    nqb = seq // Q_BLOCK
    assert seq % K_SUPER == 0
    n_sb_max = min(seq // K_SUPER, W_MAX // K_SUPER + 1)
    n_tab = Q_PER_SUPER * n_sb_max
    qblk = lambda b, i: (b * nqb + i, 0)
    per_b = lambda b, i: (b, 0, 0)
    return pl.pallas_call(
        functools.partial(_attn_prompt_kernel, n_tab=n_tab, n_sb_max=n_sb_max),
        grid=(batch, nqb),
        in_specs=[
            pl.BlockSpec((Q_BLOCK, ATT_INNER), qblk),
            pl.BlockSpec((1, KV_DIM, seq), per_b),
            pl.BlockSpec((1, KV_DIM, seq), per_b),
            pl.BlockSpec((Q_BLOCK, ATT_INNER), qblk),
        ],
        out_specs=pl.BlockSpec((Q_BLOCK, ATT_INNER), qblk),
        out_shape=jax.ShapeDtypeStruct((rows, ATT_INNER), BF16),
        scratch_shapes=[
            pltpu.VMEM((n_tab, K_SUPER, Q_BLOCK), F32),
            pltpu.VMEM((ATT_KV_HEADS, seq, HEAD_DIM), BF16),
            pltpu.VMEM((ATT_KV_HEADS, ACC_ROWS, seq), BF16),
            pltpu.VMEM((ATT_KV_HEADS, HEAD_DIM, ATT_GQ * Q_BLOCK), BF16),
            pltpu.VMEM((ATT_HEADS // 2, SUBLANES, 2 * Q_BLOCK), F32),
            pltpu.VMEM((ATT_HEADS // 2, ACC_ROWS, 2 * Q_BLOCK), F32),
            pltpu.VMEM((ATT_INNER, Q_BLOCK), F32),
            pltpu.VMEM((ATT_KV_HEADS, K_SUPER, ATT_GQ * Q_BLOCK), F32),
        ],
        compiler_params=pltpu.CompilerParams(
            dimension_semantics=("arbitrary", "arbitrary"), vmem_limit_bytes=VMEM_LIMIT),
        name="attn_prompt",
    )(q, kt, vt, gate)


def _attn_sample_kernel(q_ref, knt_ref, vnt_ref, g_ref, ckt_ref, cvt_ref, o_ref, kot_ref, vot_ref, *, ts):
    win = ckt_ref.shape[2]
    tp = SAMPLE_PAD
    rows = ATT_GQ * tp
    t_c = lax.broadcasted_iota(jnp.int32, (rows, win), 0) % tp
    j_c = lax.broadcasted_iota(jnp.int32, (rows, win), 1)
    w_c = _multiplicity(win + t_c - j_c)
    t_n = lax.broadcasted_iota(jnp.int32, (rows, tp), 0) % tp
    j_n = lax.broadcasted_iota(jnp.int32, (rows, tp), 1)
    w_n = jnp.where(j_n < ts, _multiplicity(t_n - j_n), 0.0)
    for kvh in range(ATT_KV_HEADS):
        c0 = kvh * ATT_GQ * HEAD_DIM
        qh = jnp.concatenate(
            [q_ref[:, c0 + g * HEAD_DIM:c0 + (g + 1) * HEAD_DIM] for g in range(ATT_GQ)], axis=0)
        hs = slice(kvh * HEAD_DIM, (kvh + 1) * HEAD_DIM)
        s_c = jnp.where(w_c > 0.0, _dot(qh, ckt_ref[0, hs, :].astype(BF16)), NEG)
        s_n = jnp.where(w_n > 0.0, _dot(qh, knt_ref[0, hs, :].astype(BF16)), NEG)
        m = jnp.maximum(jnp.max(s_c, axis=-1, keepdims=True), jnp.max(s_n, axis=-1, keepdims=True))
        p_c = jnp.exp2(s_c - m) * w_c
        p_n = jnp.exp2(s_n - m) * w_n
        den = jnp.sum(p_c, axis=-1, keepdims=True) + jnp.sum(p_n, axis=-1, keepdims=True)
        num = (_dot_nt(p_c.astype(BF16), cvt_ref[0, hs, :].astype(BF16))
               + _dot_nt(p_n.astype(BF16), vnt_ref[0, hs, :].astype(BF16)))
        o = num / den
        o = jnp.concatenate([o[g * tp:(g + 1) * tp] for g in range(ATT_GQ)], axis=1)
        csl = slice(c0, c0 + ATT_GQ * HEAD_DIM)
        o_ref[:, csl] = (o * g_ref[:, csl]).astype(BF16)
    kot_ref[0, :, 0:win - ts] = ckt_ref[0, :, ts:win]
    kot_ref[0, :, win - ts:win] = knt_ref[0, :, 0:ts]
    vot_ref[0, :, 0:win - ts] = cvt_ref[0, :, ts:win]
    vot_ref[0, :, win - ts:win] = vnt_ref[0, :, 0:ts]


def _attn_sample(q, knt, vnt, gate, ckt, cvt, ts):
    batch, _, win = ckt.shape
    tp = SAMPLE_PAD
    blk = lambda b: (b, 0)
    per_b = lambda b: (b, 0, 0)
    return pl.pallas_call(
        functools.partial(_attn_sample_kernel, ts=ts),
        grid=(batch,),
        in_specs=[
            pl.BlockSpec((tp, ATT_INNER), blk),
            pl.BlockSpec((1, KV_DIM, tp), per_b),
            pl.BlockSpec((1, KV_DIM, tp), per_b),
            pl.BlockSpec((tp, ATT_INNER), blk),
            pl.BlockSpec((1, KV_DIM, win), per_b),
            pl.BlockSpec((1, KV_DIM, win), per_b),
        ],
        out_specs=[
            pl.BlockSpec((tp, ATT_INNER), blk),
            pl.BlockSpec((1, KV_DIM, win), per_b),
            pl.BlockSpec((1, KV_DIM, win), per_b),
        ],
        out_shape=[
            jax.ShapeDtypeStruct((batch * tp, ATT_INNER), BF16),
            jax.ShapeDtypeStruct((batch, KV_DIM, win), F32),
            jax.ShapeDtypeStruct((batch, KV_DIM, win), F32),
        ],
        compiler_params=pltpu.CompilerParams(
            dimension_semantics=("arbitrary",), vmem_limit_bytes=VMEM_LIMIT),
        name="attn_sample",
    )(q, knt, vnt, gate, ckt, cvt)


def _outproj_kernel(x_ref, ys_ref, ya_ref, w_ref, o_ref):
    o_ref[...] = (x_ref[...] + _dot(ys_ref[...], w_ref[0:SSD_INNER, :])
                  + _dot(ya_ref[...], w_ref[SSD_INNER:, :]))


def _out_proj(x2d, y_ssd, y_att, w_o, tm):
    rows = x2d.shape[0]
    row = lambda i: (i, 0)
    const = lambda i: (0, 0)
    return pl.pallas_call(
        _outproj_kernel,
        grid=(rows // tm,),
        in_specs=[
            pl.BlockSpec((tm, D_MODEL), row),
            pl.BlockSpec((tm, SSD_INNER), row),
            pl.BlockSpec((tm, ATT_INNER), row),
            pl.BlockSpec((SSD_INNER + ATT_INNER, D_MODEL), const, pipeline_mode=pl.Buffered(1)),
        ],
        out_specs=pl.BlockSpec((tm, D_MODEL), row),
        out_shape=jax.ShapeDtypeStruct((rows, D_MODEL), F32),
        compiler_params=pltpu.CompilerParams(
            dimension_semantics=("arbitrary",), vmem_limit_bytes=VMEM_LIMIT),
        name="out_proj",
    )(x2d, y_ssd, y_att, w_o)


def _rope_tables(pos):
    n = pos.shape[0]
    half = ROT_DIM // 2
    inv = ROPE_THETA ** (-jnp.arange(0, ROT_DIM, 2, dtype=F32) / ROT_DIM)
    ang = pos.astype(F32)[:, None] * inv[None, :]
    cos, sin = jnp.cos(ang), jnp.sin(ang)
    rest = HEAD_DIM - ROT_DIM
    zh = jnp.zeros((n, half), F32)
    cos_h = jnp.concatenate([cos, cos, jnp.ones((n, rest), F32)], axis=1)
    sa_h = jnp.concatenate([zh, sin, jnp.zeros((n, rest), F32)], axis=1)
    sb_h = jnp.concatenate([-sin, zh, jnp.zeros((n, rest), F32)], axis=1)
    rep = LANES // HEAD_DIM
    return tuple(jnp.tile(t, (1, rep)) for t in (cos_h, sa_h, sb_h))


def _lane_pad(v, n=LANES):
    return jnp.pad(v, (0, n - v.shape[0])).reshape(1, n)


def _to_cache(xt, batch, seq):
    return xt.reshape(1, batch, ATT_KV_HEADS, HEAD_DIM, seq).transpose(0, 1, 4, 2, 3)


def kernel(x_prompt, x_sample, cache_k, cache_v, state_conv, state_ssm, norm_w, w_in, conv_w,
           conv_b, dt_bias, a_log, d_skip, ssd_norm_w, q_norm_w, k_norm_w, w_out):
    bp, tp_, _ = x_prompt.shape
    bs, ts, _ = x_sample.shape
    depth = w_in.shape[0]
    assert depth == 1 and tp_ % SSD_CHUNK == 0 and ts <= SAMPLE_PAD and ts >= CONV_W - 1
    l = 0
    win = cache_k.shape[2]

    w_t = jnp.swapaxes(w_in[l], 0, 1).astype(BF16)
    w_o = w_out[l].astype(BF16)
    nw = norm_w[l].reshape(1, D_MODEL)
    rep = LANES // HEAD_DIM
    qnw = jnp.tile(q_norm_w[l], rep).reshape(1, LANES)
    knw = jnp.tile(k_norm_w[l], rep).reshape(1, LANES)
    cw = jnp.pad(conv_w[l], ((0, SUBLANES - CONV_W), (0, 0)))
    cb = conv_b[l].reshape(1, CONV_DIM)
    dtb = _lane_pad(dt_bias[l])
    alog = _lane_pad(a_log[l])
    dexp = jnp.repeat(d_skip[l], SSD_HEADDIM).reshape(1, SSD_INNER)
    snw = ssd_norm_w[l].reshape(1, SSD_INNER)

    tm = 256
    xp2 = x_prompt.reshape(bp * tp_, D_MODEL)
    tabs = _rope_tables(jnp.arange(tp_, dtype=jnp.int32))
    cprev_p = jnp.zeros((bp, SUBLANES, CONV_DIM), F32)
    zs, xc, dtr, q, kt, vt, gs, ctail = _in_proj(xp2, nw, w_t, qnw, knw, tabs, cprev_p, cw, cb,
                                                 tm, tp_, FUSE_CONV_PROMPT)
    y_ssd, h_p = _ssd(xc, dtr, zs, cprev_p, jnp.zeros((bp, SSD_INNER, SSD_STATE), F32),
                      cw, cb, dtb, alog, dexp, snw, bp, SSD_CHUNK, SSD_CHUNK, FUSE_CONV_PROMPT)
    y_att = _attn_prompt(q, kt, vt, gs, bp, tp_)
    y_p = _out_proj(xp2, y_ssd, y_att, w_o, tm).reshape(bp, tp_, D_MODEL)
    keep = min(W_MAX, tp_)
    k_p = _to_cache(kt, bp, tp_)[:, :, tp_ - keep:]
    v_p = _to_cache(vt, bp, tp_)[:, :, tp_ - keep:]
    c_p = ctail[:, SUBLANES - (CONV_W - 1):][None]
    h_p = h_p.reshape(1, bp, SSD_HEADS, SSD_HEADDIM, SSD_STATE)

    pad = SAMPLE_PAD
    xs2 = jnp.pad(x_sample, ((0, 0), (0, pad - ts), (0, 0))).reshape(bs * pad, D_MODEL)
    pos_s = PAST_LEN + (jnp.arange(bs * pad, dtype=jnp.int32) % pad)
    tabs_s = _rope_tables(pos_s)
    rows_s = bs * pad
    zs, xbc, dtr, q, kt, vt, gs, _ = _in_proj(xs2, nw, w_t, qnw, knw, tabs_s,
                                              jnp.zeros((1, SUBLANES, CONV_DIM), F32), cw, cb,
                                              rows_s, rows_s, False)
    cprev = jnp.pad(state_conv[l], ((0, 0), (SUBLANES - (CONV_W - 1), 0), (0, 0)))
    y_ssd, h_s = _ssd(xbc, dtr, zs, cprev, state_ssm[l].reshape(bs, SSD_INNER, SSD_STATE),
                      cw, cb, dtb, alog, dexp, snw, bs, pad, ts, False)
    knt = kt.reshape(KV_DIM, bs, pad).transpose(1, 0, 2)
    vnt = vt.reshape(KV_DIM, bs, pad).transpose(1, 0, 2)
    ckt = cache_k[l].transpose(0, 2, 3, 1).reshape(bs, KV_DIM, win)
    cvt = cache_v[l].transpose(0, 2, 3, 1).reshape(bs, KV_DIM, win)
    y_att, kot, vot = _attn_sample(q, knt, vnt, gs, ckt, cvt, ts)
    y_s = _out_proj(xs2, y_ssd, y_att, w_o, rows_s).reshape(bs, pad, D_MODEL)[:, :ts]
    k_s = _to_cache(kot, bs, win)
    v_s = _to_cache(vot, bs, win)
    c_s = xbc.reshape(bs, pad, CONV_DIM)[:, ts - (CONV_W - 1):ts][None]
    h_s = h_s.reshape(1, bs, SSD_HEADS, SSD_HEADDIM, SSD_STATE)

    return (y_p, y_s, k_p, v_p, c_p, h_p, k_s, v_s, c_s, h_s)
```

```python
import functools

import jax
import jax.numpy as jnp
from jax import lax
from jax.experimental import pallas as pl
from jax.experimental.pallas import tpu as pltpu

F32 = jnp.float32
BF16 = jnp.bfloat16

D_MODEL = 2048
SSD_HEADS = 16
SSD_HEADDIM = 64
SSD_INNER = SSD_HEADS * SSD_HEADDIM
SSD_GROUPS = 2
SSD_STATE = 128
CONV_W = 4
CONV_DIM = SSD_INNER + 2 * SSD_GROUPS * SSD_STATE
SSD_CHUNK = 128
ATT_HEADS = 16
ATT_KV_HEADS = 4
HEAD_DIM = 64
ATT_GQ = ATT_HEADS // ATT_KV_HEADS
ATT_INNER = ATT_HEADS * HEAD_DIM
KV_DIM = ATT_KV_HEADS * HEAD_DIM
ROT_DIM = HEAD_DIM // 4
ROPE_THETA = 500000.0
DILATED_BRANCHES = ((128, 1), (512, 4), (2048, 16))
W_MAX = 2048
PAST_LEN = 16384
EPS = 1e-6

LANES = 128
SUBLANES = 8
Q_BLOCK = 256
K_SUPER = 256
Q_PER_SUPER = K_SUPER // Q_BLOCK
ONES_ROWS = 16
ACC_ROWS = HEAD_DIM + ONES_ROWS
CONV_COLS = 256
TM_IN = 512
TM_OUT = 512
SSD_SEQS = 2
FUSE_CONV_PROMPT = False
SAMPLE_PAD = 16
NEG = -1e30
VMEM_LIMIT = 56 * 1024 * 1024

Z0 = 0
X0 = Z0 + SSD_INNER
DT0 = X0 + CONV_DIM
Q0 = DT0 + SSD_HEADS
K0 = Q0 + ATT_INNER
V0 = K0 + KV_DIM
G0 = V0 + KV_DIM
W_ROWS = G0 + ATT_INNER
HALF_INNER = SSD_INNER // SSD_GROUPS
LOG2E = 1.4426950408889634
Q_SCALE = HEAD_DIM ** -0.5 * LOG2E


def _dot(a, b):
    return jnp.dot(a, b, preferred_element_type=F32)


def _dot_nt(a, b):
    return lax.dot_general(a, b, (((1,), (1,)), ((), ())), preferred_element_type=F32)


def _split3(x):
    hi = x.astype(BF16)
    r1 = x - hi.astype(F32)
    mid = r1.astype(BF16)
    lo = (r1 - mid.astype(F32)).astype(BF16)
    return hi, mid, lo


def _dot_exact_rhs(x, m):
    hi, mid, lo = _split3(x)
    return _dot(hi, m) + _dot(mid, m) + _dot(lo, m)


def _dot_wide_rhs(x, m):
    hi = x.astype(BF16)
    lo = (x - hi.astype(F32)).astype(BF16)
    return _dot(hi, m) + _dot(lo, m)


def _dot_exact_lhs(m, x):
    hi, mid, lo = _split3(x)
    return _dot(m, hi) + _dot(m, mid) + _dot(m, lo)


def _silu(x):
    hx = 0.5 * x
    return hx * jnp.tanh(hx) + hx


def _multiplicity(d):
    w = jnp.zeros(d.shape, F32)
    for window, dil in DILATED_BRANCHES:
        hit = (d >= 0) & (d <= window) & (lax.rem(d, dil) == 0)
        w = w + jnp.where(hit, 1.0, 0.0)
    return w


def _norm_rope(y, nw, cos, sa, sb, scale):
    lane = lax.broadcasted_iota(jnp.int32, (1, LANES), 1)
    first = lane < HEAD_DIM
    y2 = y * y
    s_lo = jnp.sum(jnp.where(first, y2, 0.0), axis=-1, keepdims=True)
    s_hi = jnp.sum(jnp.where(first, 0.0, y2), axis=-1, keepdims=True)
    ms = jnp.where(first, s_lo, s_hi) * (1.0 / HEAD_DIM)
    yn = y * lax.rsqrt(ms + EPS) * nw
    half = ROT_DIM // 2
    rot = yn * cos + pltpu.roll(yn, half, 1) * sa + pltpu.roll(yn, LANES - half, 1) * sb
    return rot * scale


def _causal_conv_silu(load, cw_ref, cb_ref, store):
    for c0 in range(0, CONV_DIM, CONV_COLS):
        cs = slice(c0, c0 + CONV_COLS)
        xe = load(cs)
        x2 = pltpu.roll(xe, 2, 0)
        even = cw_ref[3:4, cs] * xe + cw_ref[1:2, cs] * x2
        odd = cw_ref[2:3, cs] * xe + cw_ref[0:1, cs] * x2
        store(cs, _silu((even + pltpu.roll(odd, 1, 0))[SUBLANES:, :] + cb_ref[:, cs]))


def _inproj_kernel(x_ref, nw_ref, w_ref, qnw_ref, knw_ref, cos_ref, sa_ref, sb_ref, cprev_ref, cw_ref,
                   cb_ref, zs_ref, xc_ref, dt_ref, q_ref, kt_ref, vt_ref, gs_ref, ctail_ref, xext,
                   *, fuse_conv, per_seq):
    tm = x_ref.shape[0]
    if fuse_conv:
        @pl.when(pl.program_id(0) % per_seq == 0)
        def _():
            xext[0:SUBLANES, :] = cprev_ref[0]

    x = x_ref[...]
    ms = jnp.mean(x * x, axis=-1, keepdims=True)
    hn = (x * lax.rsqrt(ms + EPS) * nw_ref[...]).astype(BF16)

    def proj(a, b):
        return _dot_nt(hn, w_ref[a:b, :])

    cos, sa, sb = cos_ref[...], sa_ref[...], sb_ref[...]
    q = proj(Q0, K0)
    for c in range(ATT_INNER // LANES):
        sl = slice(c * LANES, (c + 1) * LANES)
        q_ref[:, sl] = _norm_rope(q[:, sl], qnw_ref[...], cos, sa, sb, Q_SCALE).astype(BF16)
    k = proj(K0, V0)
    kn = [_norm_rope(k[:, c * LANES:(c + 1) * LANES], knw_ref[...], cos, sa, sb, 1.0)
          for c in range(KV_DIM // LANES)]
    kt_ref[0] = jnp.concatenate(kn, axis=1).T

    if fuse_conv:
        def load(cs):
            xext[SUBLANES:SUBLANES + tm, cs] = proj(X0 + cs.start, X0 + cs.stop)
            return xext[0:SUBLANES + tm, cs]

        def store(cs, v):
            xc_ref[:, cs] = v
            tail = xext[tm:tm + SUBLANES, cs]
            ctail_ref[0, :, cs] = tail
            xext[0:SUBLANES, cs] = tail

        _causal_conv_silu(load, cw_ref, cb_ref, store)
    else:
        xbc = proj(X0, DT0)
        xc_ref[...] = xbc
        ctail_ref[0] = xbc[tm - SUBLANES:tm, :]

    vt_ref[0] = proj(V0, G0).T
    zs_ref[...] = _silu(proj(Z0, X0))
    gs_ref[...] = _silu(proj(G0, W_ROWS))
    dt_ref[...] = proj(DT0, DT0 + LANES)


def _in_proj(x2d, nw, w_t, qnw, knw, tables, cprev, cw, cb, tm, seq, fuse_conv):
    rows = x2d.shape[0]
    period = tables[0].shape[0] // tm
    per_seq = seq // tm
    const = lambda i: (0, 0)
    row = lambda i: (i, 0)
    tab = lambda i: (i % period, 0)
    tr = lambda i: (i // per_seq, 0, i % per_seq)
    slab = lambda i: (i // per_seq, 0, 0)
    row_outs = {0: (SSD_INNER, F32), 1: (CONV_DIM, F32), 2: (LANES, F32), 3: (ATT_INNER, BF16),
                6: (ATT_INNER, F32)}
    out_specs, out_shape = [], []
    for idx in range(7):
        if idx in row_outs:
            n, dt = row_outs[idx]
            out_specs.append(pl.BlockSpec((tm, n), row))
            out_shape.append(jax.ShapeDtypeStruct((rows, n), dt))
        else:
            out_specs.append(pl.BlockSpec((1, KV_DIM, tm), tr))
            out_shape.append(jax.ShapeDtypeStruct((rows // seq, KV_DIM, seq), F32))
    out_specs.append(pl.BlockSpec((1, SUBLANES, CONV_DIM), slab))
    out_shape.append(jax.ShapeDtypeStruct((rows // seq, SUBLANES, CONV_DIM), F32))
    return pl.pallas_call(
        functools.partial(_inproj_kernel, fuse_conv=fuse_conv, per_seq=per_seq),
        grid=(rows // tm,),
        in_specs=[
            pl.BlockSpec((tm, D_MODEL), row),
            pl.BlockSpec((1, D_MODEL), const),
            pl.BlockSpec((W_ROWS, D_MODEL), const, pipeline_mode=pl.Buffered(1)),
            pl.BlockSpec((1, LANES), const),
            pl.BlockSpec((1, LANES), const),
            pl.BlockSpec((tm, LANES), tab),
            pl.BlockSpec((tm, LANES), tab),
            pl.BlockSpec((tm, LANES), tab),
            pl.BlockSpec((1, SUBLANES, CONV_DIM), slab),
            pl.BlockSpec((SUBLANES, CONV_DIM), const),
            pl.BlockSpec((1, CONV_DIM), const),
        ],
        out_specs=out_specs,
        out_shape=out_shape,
        scratch_shapes=[pltpu.VMEM((SUBLANES + tm + SUBLANES, CONV_DIM), F32)],
        compiler_params=pltpu.CompilerParams(
            dimension_semantics=("arbitrary",), vmem_limit_bytes=VMEM_LIMIT),
        name="in_proj",
    )(x2d, nw, w_t, qnw, knw, *tables, cprev, cw, cb)


def _ssd_chain(s, xbc_ref, dtr_ref, z_ref, cprev_ref, h0_ref, cw_ref, cb_ref, dtb_ref, alog_ref,
               dexp_ref, nw_ref, y_ref, hout_ref, xext, ht, xc_scr, lin, valid, conv_done):
    L = SSD_CHUNK
    c = pl.program_id(1)
    last = pl.num_programs(1) - 1

    @pl.when(c == 0)
    def _():
        xext[s, 0:SUBLANES, :] = cprev_ref[s]
        for g in range(SSD_GROUPS):
            ht[s, g] = h0_ref[s, g * HALF_INNER:(g + 1) * HALF_INNER, :].T

    def pad_rows(v):
        if lin == L:
            return v
        return jnp.concatenate([v, jnp.zeros((L - lin, v.shape[1]), v.dtype)], axis=0)

    lane = lax.broadcasted_iota(jnp.int32, (1, LANES), 1)
    rowid = lax.broadcasted_iota(jnp.int32, (L, 1), 0)
    dt = jax.nn.softplus(pad_rows(dtr_ref[s]) + dtb_ref[...])
    dt = jnp.where((lane < SSD_HEADS) & (rowid < valid), dt, 0.0)
    dta = dt * (-jnp.exp(alog_ref[...]) * LOG2E)

    r2 = lax.broadcasted_iota(jnp.int32, (L, L), 0)
    c2 = lax.broadcasted_iota(jnp.int32, (L, L), 1)
    tri = r2 >= c2
    tri_b = jnp.where(tri, 1.0, 0.0).astype(BF16)
    cum = _dot_exact_lhs(tri_b, dta)
    cum_last = cum[L - 1:L, :]
    yield

    er = lax.broadcasted_iota(jnp.int32, (LANES, SSD_INNER), 0)
    ec = lax.broadcasted_iota(jnp.int32, (LANES, SSD_INNER), 1)
    expand = jnp.where(ec // SSD_HEADDIM == er, 1.0, 0.0).astype(BF16)
    ecum = _dot_wide_rhs(jnp.exp2(cum), expand)
    wexp = _dot_wide_rhs(jnp.exp2(cum_last - cum) * dt, expand)
    cd = _dot_exact_rhs(jnp.broadcast_to(jnp.exp2(cum_last), (SUBLANES, LANES)), expand)[0:1, :]
    cum_t = cum.T
    dt_t = dt.T
    yield

    if conv_done:
        xc = xbc_ref.at[s]
    else:
        if lin < L:
            xext[s, SUBLANES + lin:SUBLANES + L, :] = jnp.zeros((L - lin, CONV_DIM), F32)
        xext[s, SUBLANES:SUBLANES + lin, :] = xbc_ref[s]

        def load(cs):
            return xext[s, 0:SUBLANES + L, cs]

        def store(cs, v):
            xc_scr[s, :, cs] = v

        _causal_conv_silu(load, cw_ref, cb_ref, store)
        xext[s, 0:SUBLANES, :] = xext[s, L:L + SUBLANES, :]
        xc = xc_scr.at[s]
    xs = xc[:, :SSD_INNER]
    bm = xc[:, SSD_INNER:SSD_INNER + SSD_GROUPS * SSD_STATE]
    cm = xc[:, SSD_INNER + SSD_GROUPS * SSD_STATE:]
    bm_b = bm.astype(BF16)
    cm_b = cm.astype(BF16)
    xs_b = xs.astype(BF16)

    y_off = jnp.concatenate(
        [_dot(cm_b[:, g * SSD_STATE:(g + 1) * SSD_STATE], ht[s, g].astype(BF16))
         for g in range(SSD_GROUPS)], axis=1) * ecum
    cbs = [_dot_nt(cm_b[:, g * SSD_STATE:(g + 1) * SSD_STATE], bm_b[:, g * SSD_STATE:(g + 1) * SSD_STATE])
           for g in range(SSD_GROUPS)]
    yield

    lane_half = lane // HEAD_DIM
    heads_per_group = SSD_HEADS // SSD_GROUPS
    y_parts = []
    for g in range(SSD_GROUPS):
        for pr in range(heads_per_group // 2):
            col0 = g * HALF_INNER + pr * LANES
            xp = xs_b[:, col0:col0 + LANES]
            yp = jnp.zeros((L, LANES), F32)
            for e in range(2):
                hh = g * heads_per_group + pr * 2 + e
                decay = jnp.exp2(cum[:, hh:hh + 1] - cum_t[hh:hh + 1, :])
                sc = jnp.where(tri, cbs[g] * decay, 0.0) * dt_t[hh:hh + 1, :]
                xm = jnp.where(lane_half == e, xp, jnp.zeros_like(xp))
                yp = yp + _dot(sc.astype(BF16), xm)
            y_parts.append(yp)
            if pr % 2 == 1:
                yield
    y_diag = jnp.concatenate(y_parts, axis=1)

    xw = (xs * wexp).astype(BF16)
    for g in range(SSD_GROUPS):
        sl = slice(g * HALF_INNER, (g + 1) * HALF_INNER)
        bm_t = bm[:, g * SSD_STATE:(g + 1) * SSD_STATE].T.astype(BF16)
        ht[s, g] = ht[s, g] * cd[:, sl] + _dot(bm_t, xw[:, sl])
    yield

    y = (y_diag + y_off + dexp_ref[...] * xs) * pad_rows(z_ref[s])
    yn = []
    for g in range(SSD_GROUPS):
        yg = y[:, g * HALF_INNER:(g + 1) * HALF_INNER]
        ms = jnp.mean(yg * yg, axis=-1, keepdims=True)
        yn.append(yg * lax.rsqrt(ms + EPS))
    yn = jnp.concatenate(yn, axis=1) * nw_ref[...]
    y_ref[s] = yn[:lin].astype(BF16)

    @pl.when(c == last)
    def _():
        for g in range(SSD_GROUPS):
            hout_ref[s, g * HALF_INNER:(g + 1) * HALF_INNER, :] = ht[s, g].T


def _ssd_kernel(*refs, lin, valid, conv_done):
    chains = [_ssd_chain(s, *refs, lin, valid, conv_done) for s in range(SSD_SEQS)]
    while chains:
        alive = []
        for ch in chains:
            try:
                next(ch)
                alive.append(ch)
            except StopIteration:
                pass
        chains = alive


def _ssd(xbc, dtr, z, cprev, h0, cw, cb, dtb, alog, dexp, nw, batch, lin, valid, conv_done):
    rows = xbc.shape[0]
    seq = rows // batch
    nc = seq // lin
    g = SSD_SEQS
    assert batch % g == 0
    blk = lambda b, c: (b, c, 0)
    per_b = lambda b, c: (b, 0, 0)
    const = lambda b, c: (0, 0)
    y, hout = pl.pallas_call(
        functools.partial(_ssd_kernel, lin=lin, valid=valid, conv_done=conv_done),
        grid=(batch // g, nc),
        in_specs=[
            pl.BlockSpec((g, lin, CONV_DIM), blk),
            pl.BlockSpec((g, lin, LANES), blk),
            pl.BlockSpec((g, lin, SSD_INNER), blk),
            pl.BlockSpec((g, SUBLANES, CONV_DIM), per_b),
            pl.BlockSpec((g, SSD_INNER, SSD_STATE), per_b),
            pl.BlockSpec((SUBLANES, CONV_DIM), const),
            pl.BlockSpec((1, CONV_DIM), const),
            pl.BlockSpec((1, LANES), const),
            pl.BlockSpec((1, LANES), const),
            pl.BlockSpec((1, SSD_INNER), const),
            pl.BlockSpec((1, SSD_INNER), const),
        ],
        out_specs=[
            pl.BlockSpec((g, lin, SSD_INNER), blk),
            pl.BlockSpec((g, SSD_INNER, SSD_STATE), per_b),
        ],
        out_shape=[
            jax.ShapeDtypeStruct((batch, seq, SSD_INNER), BF16),
            jax.ShapeDtypeStruct((batch, SSD_INNER, SSD_STATE), F32),
        ],
        scratch_shapes=[
            pltpu.VMEM((g, SUBLANES + SSD_CHUNK + SUBLANES, CONV_DIM), F32),
            pltpu.VMEM((g, SSD_GROUPS, SSD_STATE, HALF_INNER), F32),
            pltpu.VMEM((g, SSD_CHUNK, CONV_DIM), F32),
        ],
        compiler_params=pltpu.CompilerParams(
            dimension_semantics=("arbitrary", "arbitrary"), vmem_limit_bytes=VMEM_LIMIT),
        name="ssd",
    )(xbc.reshape(batch, seq, CONV_DIM), dtr.reshape(batch, seq, LANES),
      z.reshape(batch, seq, SSD_INNER), cprev, h0, cw, cb, dtb, alog, dexp, nw)
    return y.reshape(rows, SSD_INNER), hout


def _attn_prompt_kernel(q_ref, kt_ref, vt_ref, g_ref, o_ref,
                        bias_scr, kh_scr, va_scr, qt_scr, m_scr, acc_scr, ot_scr, s_scr, *, n_tab, n_sb_max):
    b = pl.program_id(0)
    i = pl.program_id(1)
    seq = kt_ref.shape[2]
    pair_w = 2 * Q_BLOCK
    pairs_per_kv = ATT_GQ // 2

    @pl.when((b == 0) & (i == 0))
    def _():
        r = lax.broadcasted_iota(jnp.int32, (K_SUPER, Q_BLOCK), 0)
        c = lax.broadcasted_iota(jnp.int32, (K_SUPER, Q_BLOCK), 1)
        for tb in range(n_tab):
            w = _multiplicity(tb * Q_BLOCK + c - r)
            bias_scr[tb] = jnp.where(w > 0.0, jnp.log2(jnp.maximum(w, 1.0)), NEG)
        va_scr[:, HEAD_DIM:, :] = jnp.ones((ATT_KV_HEADS, ONES_ROWS, seq), BF16)

    @pl.when(i == 0)
    def _():
        for kvh in range(ATT_KV_HEADS):
            hs = slice(kvh * HEAD_DIM, (kvh + 1) * HEAD_DIM)
            va_scr[kvh, 0:HEAD_DIM, :] = vt_ref[0, hs, :].astype(BF16)
        for cidx in range(seq // K_SUPER):
            rows = slice(cidx * K_SUPER, (cidx + 1) * K_SUPER)
            kc = kt_ref[0, :, rows].T
            for kvh in range(ATT_KV_HEADS):
                hs = slice(kvh * HEAD_DIM, (kvh + 1) * HEAD_DIM)
                kh_scr[kvh, rows, :] = kc[:, hs].astype(BF16)

    qt = q_ref[...].astype(F32).T
    for kvh in range(ATT_KV_HEADS):
        qt_scr[kvh] = jnp.concatenate(
            [qt[(kvh * ATT_GQ + g) * HEAD_DIM:(kvh * ATT_GQ + g + 1) * HEAD_DIM, :]
             for g in range(ATT_GQ)], axis=1).astype(BF16)
    m_scr[...] = jnp.full(m_scr.shape, NEG, F32)
    acc_scr[...] = jnp.zeros(acc_scr.shape, F32)
    par = i % Q_PER_SUPER
    j_last = i // Q_PER_SUPER
    n_sb = jnp.minimum(j_last + 1, n_sb_max)

    def body(dl, carry):
        start = pl.multiple_of((j_last - dl) * K_SUPER, K_SUPER)
        bias = bias_scr[par + Q_PER_SUPER * dl]
        bias2 = jnp.concatenate([bias, bias], axis=1)
        for kvh in range(ATT_KV_HEADS):
            s_scr[kvh] = _dot(kh_scr[kvh, pl.ds(start, K_SUPER), :], qt_scr[kvh])
        for kvh in range(ATT_KV_HEADS):
            vtb = va_scr[kvh, :, pl.ds(start, K_SUPER)]
            for pr in range(pairs_per_kv):
                u = kvh * pairs_per_kv + pr
                s = s_scr[kvh, :, pr * pair_w:(pr + 1) * pair_w] + bias2
                m_old = m_scr[u, 0:1, :]
                m_new = jnp.maximum(m_old, jnp.max(s, axis=0, keepdims=True))
                p = jnp.exp2(s - m_new).astype(BF16)
                acc_scr[u] = jnp.exp2(m_old - m_new) * acc_scr[u] + _dot(vtb, p)
                m_scr[u, 0:1, :] = m_new
        return carry

    lax.fori_loop(0, n_sb, body, 0)

    for u in range(ATT_HEADS // 2):
        acc = acc_scr[u]
        o = acc[0:HEAD_DIM] / acc[HEAD_DIM:HEAD_DIM + 1]
        for e in range(2):
            h = 2 * u + e
            ot_scr[h * HEAD_DIM:(h + 1) * HEAD_DIM, :] = o[:, e * Q_BLOCK:(e + 1) * Q_BLOCK]
    o_ref[...] = (ot_scr[...].T * g_ref[...]).astype(BF16)


def _attn_prompt(q, kt, vt, gate, batch, seq):
    rows = q.shape[0]
    nqb = seq // Q_BLOCK
    assert seq % K_SUPER == 0
    n_sb_max = min(seq // K_SUPER, W_MAX // K_SUPER + 1)
    n_tab = Q_PER_SUPER * n_sb_max
    qblk = lambda b, i: (b * nqb + i, 0)
    per_b = lambda b, i: (b, 0, 0)
    return pl.pallas_call(
        functools.partial(_attn_prompt_kernel, n_tab=n_tab, n_sb_max=n_sb_max),
        grid=(batch, nqb),
        in_specs=[
            pl.BlockSpec((Q_BLOCK, ATT_INNER), qblk),
            pl.BlockSpec((1, KV_DIM, seq), per_b),
            pl.BlockSpec((1, KV_DIM, seq), per_b),
            pl.BlockSpec((Q_BLOCK, ATT_INNER), qblk),
        ],
        out_specs=pl.BlockSpec((Q_BLOCK, ATT_INNER), qblk),
        out_shape=jax.ShapeDtypeStruct((rows, ATT_INNER), BF16),
        scratch_shapes=[
            pltpu.VMEM((n_tab, K_SUPER, Q_BLOCK), F32),
            pltpu.VMEM((ATT_KV_HEADS, seq, HEAD_DIM), BF16),
            pltpu.VMEM((ATT_KV_HEADS, ACC_ROWS, seq), BF16),
            pltpu.VMEM((ATT_KV_HEADS, HEAD_DIM, ATT_GQ * Q_BLOCK), BF16),
            pltpu.VMEM((ATT_HEADS // 2, SUBLANES, 2 * Q_BLOCK), F32),
            pltpu.VMEM((ATT_HEADS // 2, ACC_ROWS, 2 * Q_BLOCK), F32),
            pltpu.VMEM((ATT_INNER, Q_BLOCK), F32),
            pltpu.VMEM((ATT_KV_HEADS, K_SUPER, ATT_GQ * Q_BLOCK), F32),
        ],
        compiler_params=pltpu.CompilerParams(
            dimension_semantics=("arbitrary", "arbitrary"), vmem_limit_bytes=VMEM_LIMIT),
        name="attn_prompt",
    )(q, kt, vt, gate)


def _attn_sample_kernel(q_ref, knt_ref, vnt_ref, g_ref, ckt_ref, cvt_ref, o_ref, kot_ref, vot_ref, *, ts):
    win = ckt_ref.shape[2]
    tp = SAMPLE_PAD
    rows = ATT_GQ * tp
    t_c = lax.broadcasted_iota(jnp.int32, (rows, win), 0) % tp
    j_c = lax.broadcasted_iota(jnp.int32, (rows, win), 1)
    w_c = _multiplicity(win + t_c - j_c)
    t_n = lax.broadcasted_iota(jnp.int32, (rows, tp), 0) % tp
    j_n = lax.broadcasted_iota(jnp.int32, (rows, tp), 1)
    w_n = jnp.where(j_n < ts, _multiplicity(t_n - j_n), 0.0)
    for kvh in range(ATT_KV_HEADS):
        c0 = kvh * ATT_GQ * HEAD_DIM
        qh = jnp.concatenate(
            [q_ref[:, c0 + g * HEAD_DIM:c0 + (g + 1) * HEAD_DIM] for g in range(ATT_GQ)], axis=0)
        hs = slice(kvh * HEAD_DIM, (kvh + 1) * HEAD_DIM)
        s_c = jnp.where(w_c > 0.0, _dot(qh, ckt_ref[0, hs, :].astype(BF16)), NEG)
        s_n = jnp.where(w_n > 0.0, _dot(qh, knt_ref[0, hs, :].astype(BF16)), NEG)
        m = jnp.maximum(jnp.max(s_c, axis=-1, keepdims=True), jnp.max(s_n, axis=-1, keepdims=True))
        p_c = jnp.exp2(s_c - m) * w_c
        p_n = jnp.exp2(s_n - m) * w_n
        den = jnp.sum(p_c, axis=-1, keepdims=True) + jnp.sum(p_n, axis=-1, keepdims=True)
        num = (_dot_nt(p_c.astype(BF16), cvt_ref[0, hs, :].astype(BF16))
               + _dot_nt(p_n.astype(BF16), vnt_ref[0, hs, :].astype(BF16)))
        o = num / den
        o = jnp.concatenate([o[g * tp:(g + 1) * tp] for g in range(ATT_GQ)], axis=1)
        csl = slice(c0, c0 + ATT_GQ * HEAD_DIM)
        o_ref[:, csl] = (o * g_ref[:, csl]).astype(BF16)
    kot_ref[0, :, 0:win - ts] = ckt_ref[0, :, ts:win]
    kot_ref[0, :, win - ts:win] = knt_ref[0, :, 0:ts]
    vot_ref[0, :, 0:win - ts] = cvt_ref[0, :, ts:win]
    vot_ref[0, :, win - ts:win] = vnt_ref[0, :, 0:ts]


def _attn_sample(q, knt, vnt, gate, ckt, cvt, ts):
    batch, _, win = ckt.shape
    tp = SAMPLE_PAD
    blk = lambda b: (b, 0)
    per_b = lambda b: (b, 0, 0)
    return pl.pallas_call(
        functools.partial(_attn_sample_kernel, ts=ts),
        grid=(batch,),
        in_specs=[
            pl.BlockSpec((tp, ATT_INNER), blk),
            pl.BlockSpec((1, KV_DIM, tp), per_b),
            pl.BlockSpec((1, KV_DIM, tp), per_b),
            pl.BlockSpec((tp, ATT_INNER), blk),
            pl.BlockSpec((1, KV_DIM, win), per_b),
            pl.BlockSpec((1, KV_DIM, win), per_b),
        ],
        out_specs=[
            pl.BlockSpec((tp, ATT_INNER), blk),
            pl.BlockSpec((1, KV_DIM, win), per_b),
            pl.BlockSpec((1, KV_DIM, win), per_b),
        ],
        out_shape=[
            jax.ShapeDtypeStruct((batch * tp, ATT_INNER), BF16),
            jax.ShapeDtypeStruct((batch, KV_DIM, win), F32),
            jax.ShapeDtypeStruct((batch, KV_DIM, win), F32),
        ],
        compiler_params=pltpu.CompilerParams(
            dimension_semantics=("arbitrary",), vmem_limit_bytes=VMEM_LIMIT),
        name="attn_sample",
    )(q, knt, vnt, gate, ckt, cvt)


def _outproj_kernel(x_ref, ys_ref, ya_ref, w_ref, o_ref):
    o_ref[...] = (x_ref[...] + _dot(ys_ref[...], w_ref[0:SSD_INNER, :])
                  + _dot(ya_ref[...], w_ref[SSD_INNER:, :]))


def _out_proj(x2d, y_ssd, y_att, w_o, tm):
    rows = x2d.shape[0]
    row = lambda i: (i, 0)
    const = lambda i: (0, 0)
    return pl.pallas_call(
        _outproj_kernel,
        grid=(rows // tm,),
        in_specs=[
            pl.BlockSpec((tm, D_MODEL), row),
            pl.BlockSpec((tm, SSD_INNER), row),
            pl.BlockSpec((tm, ATT_INNER), row),
            pl.BlockSpec((SSD_INNER + ATT_INNER, D_MODEL), const, pipeline_mode=pl.Buffered(1)),
        ],
        out_specs=pl.BlockSpec((tm, D_MODEL), row),
        out_shape=jax.ShapeDtypeStruct((rows, D_MODEL), F32),
        compiler_params=pltpu.CompilerParams(
            dimension_semantics=("arbitrary",), vmem_limit_bytes=VMEM_LIMIT),
        name="out_proj",
    )(x2d, y_ssd, y_att, w_o)


def _rope_tables(pos):
    n = pos.shape[0]
    half = ROT_DIM // 2
    inv = ROPE_THETA ** (-jnp.arange(0, ROT_DIM, 2, dtype=F32) / ROT_DIM)
    ang = pos.astype(F32)[:, None] * inv[None, :]
    cos, sin = jnp.cos(ang), jnp.sin(ang)
    rest = HEAD_DIM - ROT_DIM
    zh = jnp.zeros((n, half), F32)
    cos_h = jnp.concatenate([cos, cos, jnp.ones((n, rest), F32)], axis=1)
    sa_h = jnp.concatenate([zh, sin, jnp.zeros((n, rest), F32)], axis=1)
    sb_h = jnp.concatenate([-sin, zh, jnp.zeros((n, rest), F32)], axis=1)
    rep = LANES // HEAD_DIM
    return tuple(jnp.tile(t, (1, rep)) for t in (cos_h, sa_h, sb_h))


def _lane_pad(v, n=LANES):
    return jnp.pad(v, (0, n - v.shape[0])).reshape(1, n)


def _to_cache(xt, batch, seq):
    return xt.reshape(1, batch, ATT_KV_HEADS, HEAD_DIM, seq).transpose(0, 1, 4, 2, 3)


def kernel(x_prompt, x_sample, cache_k, cache_v, state_conv, state_ssm, norm_w, w_in, conv_w,
           conv_b, dt_bias, a_log, d_skip, ssd_norm_w, q_norm_w, k_norm_w, w_out):
    bp, tp_, _ = x_prompt.shape
    bs, ts, _ = x_sample.shape
    depth = w_in.shape[0]
    assert depth == 1 and tp_ % SSD_CHUNK == 0 and ts <= SAMPLE_PAD and ts >= CONV_W - 1
    l = 0
    win = cache_k.shape[2]

    w_t = jnp.swapaxes(w_in[l], 0, 1).astype(BF16)
    w_o = w_out[l].astype(BF16)
    nw = norm_w[l].reshape(1, D_MODEL)
    rep = LANES // HEAD_DIM
    qnw = jnp.tile(q_norm_w[l], rep).reshape(1, LANES)
    knw = jnp.tile(k_norm_w[l], rep).reshape(1, LANES)
    cw = jnp.pad(conv_w[l], ((0, SUBLANES - CONV_W), (0, 0)))
    cb = conv_b[l].reshape(1, CONV_DIM)
    dtb = _lane_pad(dt_bias[l])
    alog = _lane_pad(a_log[l])
    dexp = jnp.repeat(d_skip[l], SSD_HEADDIM).reshape(1, SSD_INNER)
    snw = ssd_norm_w[l].reshape(1, SSD_INNER)

    tm = TM_IN
    xp2 = x_prompt.reshape(bp * tp_, D_MODEL)
    tabs = _rope_tables(jnp.arange(tp_, dtype=jnp.int32))
    cprev_p = jnp.zeros((bp, SUBLANES, CONV_DIM), F32)
    zs, xc, dtr, q, kt, vt, gs, ctail = _in_proj(xp2, nw, w_t, qnw, knw, tabs, cprev_p, cw, cb,
                                                 tm, tp_, FUSE_CONV_PROMPT)
    y_ssd, h_p = _ssd(xc, dtr, zs, cprev_p, jnp.zeros((bp, SSD_INNER, SSD_STATE), F32),
                      cw, cb, dtb, alog, dexp, snw, bp, SSD_CHUNK, SSD_CHUNK, FUSE_CONV_PROMPT)
    y_att = _attn_prompt(q, kt, vt, gs, bp, tp_)
    y_p = _out_proj(xp2, y_ssd, y_att, w_o, TM_OUT).reshape(bp, tp_, D_MODEL)
    keep = min(W_MAX, tp_)
    k_p = _to_cache(kt, bp, tp_)[:, :, tp_ - keep:]
    v_p = _to_cache(vt, bp, tp_)[:, :, tp_ - keep:]
    c_p = ctail[:, SUBLANES - (CONV_W - 1):][None]
    h_p = h_p.reshape(1, bp, SSD_HEADS, SSD_HEADDIM, SSD_STATE)

    pad = SAMPLE_PAD
    xs2 = jnp.pad(x_sample, ((0, 0), (0, pad - ts), (0, 0))).reshape(bs * pad, D_MODEL)
    pos_s = PAST_LEN + (jnp.arange(bs * pad, dtype=jnp.int32) % pad)
    tabs_s = _rope_tables(pos_s)
    rows_s = bs * pad
    zs, xbc, dtr, q, kt, vt, gs, _ = _in_proj(xs2, nw, w_t, qnw, knw, tabs_s,
                                              jnp.zeros((1, SUBLANES, CONV_DIM), F32), cw, cb,
                                              rows_s, rows_s, False)
    cprev = jnp.pad(state_conv[l], ((0, 0), (SUBLANES - (CONV_W - 1), 0), (0, 0)))
    y_ssd, h_s = _ssd(xbc, dtr, zs, cprev, state_ssm[l].reshape(bs, SSD_INNER, SSD_STATE),
                      cw, cb, dtb, alog, dexp, snw, bs, pad, ts, False)
    knt = kt.reshape(KV_DIM, bs, pad).transpose(1, 0, 2)
    vnt = vt.reshape(KV_DIM, bs, pad).transpose(1, 0, 2)
    ckt = cache_k[l].transpose(0, 2, 3, 1).reshape(bs, KV_DIM, win)
    cvt = cache_v[l].transpose(0, 2, 3, 1).reshape(bs, KV_DIM, win)
    y_att, kot, vot = _attn_sample(q, knt, vnt, gs, ckt, cvt, ts)
    y_s = _out_proj(xs2, y_ssd, y_att, w_o, rows_s).reshape(bs, pad, D_MODEL)[:, :ts]
    k_s = _to_cache(kot, bs, win)
    v_s = _to_cache(vot, bs, win)
    c_s = xbc.reshape(bs, pad, CONV_DIM)[:, ts - (CONV_W - 1):ts][None]
    h_s = h_s.reshape(1, bs, SSD_HEADS, SSD_HEADDIM, SSD_STATE)

    return (y_p, y_s, k_p, v_p, c_p, h_p, k_s, v_s, c_s, h_s)
```

```python
import functools

import jax
import jax.numpy as jnp
import numpy as np
from jax import lax
from jax.experimental import pallas as pl
from jax.experimental.pallas import tpu as pltpu

F32 = jnp.float32
BF16 = jnp.bfloat16

D_MODEL = 2048
SSD_HEADS = 16
SSD_HEADDIM = 64
SSD_INNER = SSD_HEADS * SSD_HEADDIM
SSD_GROUPS = 2
SSD_STATE = 128
CONV_W = 4
CONV_DIM = SSD_INNER + 2 * SSD_GROUPS * SSD_STATE
SSD_CHUNK = 128
ATT_HEADS = 16
ATT_KV_HEADS = 4
HEAD_DIM = 64
ATT_GQ = ATT_HEADS // ATT_KV_HEADS
ATT_INNER = ATT_HEADS * HEAD_DIM
KV_DIM = ATT_KV_HEADS * HEAD_DIM
ROT_DIM = HEAD_DIM // 4
ROPE_THETA = 500000.0
DILATED_BRANCHES = ((128, 1), (512, 4), (2048, 16))
W_MAX = 2048
PAST_LEN = 16384
EPS = 1e-6

LANES = 128
SUBLANES = 8
Q_BLOCK = 256
K_SUPER = 256
Q_PER_SUPER = K_SUPER // Q_BLOCK
ONES_ROWS = 16
ACC_ROWS = HEAD_DIM + ONES_ROWS
CONV_COLS = 256
TM_IN = 512
TM_OUT = 512
SSD_SEQS = 2
SAMPLE_PAD = 16
NEG = -1e30
VMEM_LIMIT = 56 * 1024 * 1024

Z0 = 0
X0 = Z0 + SSD_INNER
DT0 = X0 + CONV_DIM
Q0 = DT0 + SSD_HEADS
K0 = Q0 + ATT_INNER
V0 = K0 + KV_DIM
G0 = V0 + KV_DIM
W_ROWS = G0 + ATT_INNER
HALF_INNER = SSD_INNER // SSD_GROUPS
LOG2E = 1.4426950408889634
Q_SCALE = HEAD_DIM ** -0.5 * LOG2E


def _dot(a, b):
    return jnp.dot(a, b, preferred_element_type=F32)


def _dot_nt(a, b):
    return lax.dot_general(a, b, (((1,), (1,)), ((), ())), preferred_element_type=F32)


def _split3(x):
    hi = x.astype(BF16)
    r1 = x - hi.astype(F32)
    mid = r1.astype(BF16)
    lo = (r1 - mid.astype(F32)).astype(BF16)
    return hi, mid, lo


def _dot_exact_rhs(x, m):
    hi, mid, lo = _split3(x)
    return _dot(hi, m) + _dot(mid, m) + _dot(lo, m)


def _dot_wide_rhs(x, m):
    hi = x.astype(BF16)
    lo = (x - hi.astype(F32)).astype(BF16)
    return _dot(hi, m) + _dot(lo, m)


def _dot_exact_lhs(m, x):
    hi, mid, lo = _split3(x)
    return _dot(m, hi) + _dot(m, mid) + _dot(m, lo)


def _silu(x):
    hx = 0.5 * x
    return hx * jnp.tanh(hx) + hx


def _multiplicity(d):
    w = jnp.zeros(d.shape, F32)
    for window, dil in DILATED_BRANCHES:
        hit = (d >= 0) & (d <= window) & (lax.rem(d, dil) == 0)
        w = w + jnp.where(hit, 1.0, 0.0)
    return w


def _norm_rope(y, nw, cos, sa, sb, scale):
    lane = lax.broadcasted_iota(jnp.int32, (1, LANES), 1)
    first = lane < HEAD_DIM
    y2 = y * y
    s_lo = jnp.sum(jnp.where(first, y2, 0.0), axis=-1, keepdims=True)
    s_hi = jnp.sum(jnp.where(first, 0.0, y2), axis=-1, keepdims=True)
    ms = jnp.where(first, s_lo, s_hi) * (1.0 / HEAD_DIM)
    yn = y * lax.rsqrt(ms + EPS) * nw
    half = ROT_DIM // 2
    rot = yn * cos + pltpu.roll(yn, half, 1) * sa + pltpu.roll(yn, LANES - half, 1) * sb
    return rot * scale


def _causal_conv_silu(load, cw_ref, cb_ref, store):
    for c0 in range(0, CONV_DIM, CONV_COLS):
        cs = slice(c0, c0 + CONV_COLS)
        xe = load(cs)
        x2 = pltpu.roll(xe, 2, 0)
        even = cw_ref[3:4, cs] * xe + cw_ref[1:2, cs] * x2
        odd = cw_ref[2:3, cs] * xe + cw_ref[0:1, cs] * x2
        store(cs, _silu((even + pltpu.roll(odd, 1, 0))[SUBLANES:, :] + cb_ref[:, cs]))


def _inproj_kernel(x_ref, nw_ref, w_ref, qnw_ref, knw_ref, cos_ref, sa_ref, sb_ref,
                   zs_ref, xbc_ref, dt_ref, q_ref, kt_ref, vt_ref, gs_ref, ctail_ref):
    tm = x_ref.shape[0]
    x = x_ref[...]
    ms = jnp.mean(x * x, axis=-1, keepdims=True)
    hn = (x * lax.rsqrt(ms + EPS) * nw_ref[...]).astype(BF16)

    def proj(a, b):
        return _dot_nt(hn, w_ref[a:b, :])

    cos, sa, sb = cos_ref[...], sa_ref[...], sb_ref[...]
    q = proj(Q0, K0)
    for c in range(ATT_INNER // LANES):
        sl = slice(c * LANES, (c + 1) * LANES)
        q_ref[:, sl] = _norm_rope(q[:, sl], qnw_ref[...], cos, sa, sb, Q_SCALE).astype(BF16)
    k = proj(K0, V0)
    kn = [_norm_rope(k[:, c * LANES:(c + 1) * LANES], knw_ref[...], cos, sa, sb, 1.0)
          for c in range(KV_DIM // LANES)]
    kt_ref[0] = jnp.concatenate(kn, axis=1).T

    xbc = proj(X0, DT0)
    xbc_ref[...] = xbc
    ctail_ref[0] = xbc[tm - SUBLANES:tm, :]
    vt_ref[0] = proj(V0, G0).T
    zs_ref[...] = _silu(proj(Z0, X0))
    gs_ref[...] = _silu(proj(G0, W_ROWS))
    dt_ref[...] = proj(DT0, DT0 + LANES)


def _in_proj(x2d, nw, w_t, qnw, knw, tables, tm, seq):
    rows = x2d.shape[0]
    period = tables[0].shape[0] // tm
    per_seq = seq // tm
    const = lambda i: (0, 0)
    row = lambda i: (i, 0)
    tab = lambda i: (i % period, 0)
    tr = lambda i: (i // per_seq, 0, i % per_seq)
    slab = lambda i: (i // per_seq, 0, 0)
    row_outs = {0: (SSD_INNER, F32), 1: (CONV_DIM, F32), 2: (LANES, F32), 3: (ATT_INNER, BF16),
                6: (ATT_INNER, F32)}
    out_specs, out_shape = [], []
    for idx in range(7):
        if idx in row_outs:
            n, dt = row_outs[idx]
            out_specs.append(pl.BlockSpec((tm, n), row))
            out_shape.append(jax.ShapeDtypeStruct((rows, n), dt))
        else:
            out_specs.append(pl.BlockSpec((1, KV_DIM, tm), tr))
            out_shape.append(jax.ShapeDtypeStruct((rows // seq, KV_DIM, seq), F32))
    out_specs.append(pl.BlockSpec((1, SUBLANES, CONV_DIM), slab))
    out_shape.append(jax.ShapeDtypeStruct((rows // seq, SUBLANES, CONV_DIM), F32))
    return pl.pallas_call(
        _inproj_kernel,
        grid=(rows // tm,),
        in_specs=[
            pl.BlockSpec((tm, D_MODEL), row),
            pl.BlockSpec((1, D_MODEL), const),
            pl.BlockSpec((W_ROWS, D_MODEL), const, pipeline_mode=pl.Buffered(1)),
            pl.BlockSpec((1, LANES), const),
            pl.BlockSpec((1, LANES), const),
            pl.BlockSpec((tm, LANES), tab),
            pl.BlockSpec((tm, LANES), tab),
            pl.BlockSpec((tm, LANES), tab),
        ],
        out_specs=out_specs,
        out_shape=out_shape,
        compiler_params=pltpu.CompilerParams(
            dimension_semantics=("arbitrary",), vmem_limit_bytes=VMEM_LIMIT),
        name="in_proj",
    )(x2d, nw, w_t, qnw, knw, *tables)


def _ssd_chain(s, xbc_ref, dtr_ref, z_ref, cprev_ref, h0_ref, cw_ref, cb_ref, dtb_ref, alog_ref,
               dexp_ref, nw_ref, y_ref, hout_ref, xext, ht, xc_scr, lin, valid):
    L = SSD_CHUNK
    c = pl.program_id(1)
    last = pl.num_programs(1) - 1

    @pl.when(c == 0)
    def _():
        xext[s, 0:SUBLANES, :] = cprev_ref[s]
        for g in range(SSD_GROUPS):
            ht[s, g] = h0_ref[s, g * HALF_INNER:(g + 1) * HALF_INNER, :].T

    def pad_rows(v):
        if lin == L:
            return v
        return jnp.concatenate([v, jnp.zeros((L - lin, v.shape[1]), v.dtype)], axis=0)

    lane = lax.broadcasted_iota(jnp.int32, (1, LANES), 1)
    rowid = lax.broadcasted_iota(jnp.int32, (L, 1), 0)
    dt = jax.nn.softplus(pad_rows(dtr_ref[s]) + dtb_ref[...])
    dt = jnp.where((lane < SSD_HEADS) & (rowid < valid), dt, 0.0)
    dta = dt * (-jnp.exp(alog_ref[...]) * LOG2E)

    r2 = lax.broadcasted_iota(jnp.int32, (L, L), 0)
    c2 = lax.broadcasted_iota(jnp.int32, (L, L), 1)
    tri = r2 >= c2
    tri_b = jnp.where(tri, 1.0, 0.0).astype(BF16)
    cum = _dot_exact_lhs(tri_b, dta)
    cum_last = cum[L - 1:L, :]
    yield

    er = lax.broadcasted_iota(jnp.int32, (LANES, SSD_INNER), 0)
    ec = lax.broadcasted_iota(jnp.int32, (LANES, SSD_INNER), 1)
    expand = jnp.where(ec // SSD_HEADDIM == er, 1.0, 0.0).astype(BF16)
    ecum = _dot_wide_rhs(jnp.exp2(cum), expand)
    wexp = _dot_wide_rhs(jnp.exp2(cum_last - cum) * dt, expand)
    cd = _dot_exact_rhs(jnp.broadcast_to(jnp.exp2(cum_last), (SUBLANES, LANES)), expand)[0:1, :]
    cum_t = cum.T
    dt_t = dt.T
    yield

    if lin < L:
        xext[s, SUBLANES + lin:SUBLANES + L, :] = jnp.zeros((L - lin, CONV_DIM), F32)
    xext[s, SUBLANES:SUBLANES + lin, :] = xbc_ref[s]

    def load(cs):
        return xext[s, 0:SUBLANES + L, cs]

    def store(cs, v):
        xc_scr[s, :, cs] = v

    _causal_conv_silu(load, cw_ref, cb_ref, store)
    xext[s, 0:SUBLANES, :] = xext[s, L:L + SUBLANES, :]
    xc = xc_scr.at[s]
    xs = xc[:, :SSD_INNER]
    bm = xc[:, SSD_INNER:SSD_INNER + SSD_GROUPS * SSD_STATE]
    cm = xc[:, SSD_INNER + SSD_GROUPS * SSD_STATE:]
    bm_b = bm.astype(BF16)
    cm_b = cm.astype(BF16)
    xs_b = xs.astype(BF16)

    y_off = jnp.concatenate(
        [_dot(cm_b[:, g * SSD_STATE:(g + 1) * SSD_STATE], ht[s, g].astype(BF16))
         for g in range(SSD_GROUPS)], axis=1) * ecum
    cbs = [_dot_nt(cm_b[:, g * SSD_STATE:(g + 1) * SSD_STATE], bm_b[:, g * SSD_STATE:(g + 1) * SSD_STATE])
           for g in range(SSD_GROUPS)]
    yield

    lane_half = lane // HEAD_DIM
    heads_per_group = SSD_HEADS // SSD_GROUPS
    y_parts = []
    for g in range(SSD_GROUPS):
        for pr in range(heads_per_group // 2):
            col0 = g * HALF_INNER + pr * LANES
            xp = xs_b[:, col0:col0 + LANES]
            yp = jnp.zeros((L, LANES), F32)
            for e in range(2):
                hh = g * heads_per_group + pr * 2 + e
                decay = jnp.exp2(cum[:, hh:hh + 1] - cum_t[hh:hh + 1, :])
                sc = jnp.where(tri, cbs[g] * decay, 0.0) * dt_t[hh:hh + 1, :]
                xm = jnp.where(lane_half == e, xp, jnp.zeros_like(xp))
                yp = yp + _dot(sc.astype(BF16), xm)
            y_parts.append(yp)
            if pr % 2 == 1:
                yield
    y_diag = jnp.concatenate(y_parts, axis=1)

    xw = (xs * wexp).astype(BF16)
    for g in range(SSD_GROUPS):
        sl = slice(g * HALF_INNER, (g + 1) * HALF_INNER)
        bm_t = bm[:, g * SSD_STATE:(g + 1) * SSD_STATE].T.astype(BF16)
        ht[s, g] = ht[s, g] * cd[:, sl] + _dot(bm_t, xw[:, sl])
    yield

    y = (y_diag + y_off + dexp_ref[...] * xs) * pad_rows(z_ref[s])
    yn = []
    for g in range(SSD_GROUPS):
        yg = y[:, g * HALF_INNER:(g + 1) * HALF_INNER]
        ms = jnp.mean(yg * yg, axis=-1, keepdims=True)
        yn.append(yg * lax.rsqrt(ms + EPS))
    yn = jnp.concatenate(yn, axis=1) * nw_ref[...]
    y_ref[s] = yn[:lin].astype(BF16)

    @pl.when(c == last)
    def _():
        for g in range(SSD_GROUPS):
            hout_ref[s, g * HALF_INNER:(g + 1) * HALF_INNER, :] = ht[s, g].T


def _ssd_kernel(*refs, lin, valid):
    chains = [_ssd_chain(s, *refs, lin, valid) for s in range(SSD_SEQS)]
    while chains:
        alive = []
        for ch in chains:
            try:
                next(ch)
                alive.append(ch)
            except StopIteration:
                pass
        chains = alive


def _ssd(xbc, dtr, z, cprev, h0, cw, cb, dtb, alog, dexp, nw, batch, lin, valid):
    rows = xbc.shape[0]
    seq = rows // batch
    nc = seq // lin
    g = SSD_SEQS
    assert batch % g == 0
    blk = lambda b, c: (b, c, 0)
    per_b = lambda b, c: (b, 0, 0)
    const = lambda b, c: (0, 0)
    y, hout = pl.pallas_call(
        functools.partial(_ssd_kernel, lin=lin, valid=valid),
        grid=(batch // g, nc),
        in_specs=[
            pl.BlockSpec((g, lin, CONV_DIM), blk),
            pl.BlockSpec((g, lin, LANES), blk),
            pl.BlockSpec((g, lin, SSD_INNER), blk),
            pl.BlockSpec((g, SUBLANES, CONV_DIM), per_b),
            pl.BlockSpec((g, SSD_INNER, SSD_STATE), per_b),
            pl.BlockSpec((SUBLANES, CONV_DIM), const),
            pl.BlockSpec((1, CONV_DIM), const),
            pl.BlockSpec((1, LANES), const),
            pl.BlockSpec((1, LANES), const),
            pl.BlockSpec((1, SSD_INNER), const),
            pl.BlockSpec((1, SSD_INNER), const),
        ],
        out_specs=[
            pl.BlockSpec((g, lin, SSD_INNER), blk),
            pl.BlockSpec((g, SSD_INNER, SSD_STATE), per_b),
        ],
        out_shape=[
            jax.ShapeDtypeStruct((batch, seq, SSD_INNER), BF16),
            jax.ShapeDtypeStruct((batch, SSD_INNER, SSD_STATE), F32),
        ],
        scratch_shapes=[
            pltpu.VMEM((g, SUBLANES + SSD_CHUNK + SUBLANES, CONV_DIM), F32),
            pltpu.VMEM((g, SSD_GROUPS, SSD_STATE, HALF_INNER), F32),
            pltpu.VMEM((g, SSD_CHUNK, CONV_DIM), F32),
        ],
        compiler_params=pltpu.CompilerParams(
            dimension_semantics=("arbitrary", "arbitrary"), vmem_limit_bytes=VMEM_LIMIT),
        name="ssd",
    )(xbc.reshape(batch, seq, CONV_DIM), dtr.reshape(batch, seq, LANES),
      z.reshape(batch, seq, SSD_INNER), cprev, h0, cw, cb, dtb, alog, dexp, nw)
    return y.reshape(rows, SSD_INNER), hout


def _attn_prompt_kernel(q_ref, kt_ref, vt_ref, g_ref, o_ref,
                        bias_scr, kh_scr, va_scr, qt_scr, m_scr, acc_scr, ot_scr, s_scr, *, n_tab, n_sb_max):
    b = pl.program_id(0)
    i = pl.program_id(1)
    seq = kt_ref.shape[2]
    pair_w = 2 * Q_BLOCK
    pairs_per_kv = ATT_GQ // 2

    @pl.when((b == 0) & (i == 0))
    def _():
        r = lax.broadcasted_iota(jnp.int32, (K_SUPER, Q_BLOCK), 0)
        c = lax.broadcasted_iota(jnp.int32, (K_SUPER, Q_BLOCK), 1)
        for tb in range(n_tab):
            w = _multiplicity(tb * Q_BLOCK + c - r)
            bias_scr[tb] = jnp.where(w > 0.0, jnp.log2(jnp.maximum(w, 1.0)), NEG)
        va_scr[:, HEAD_DIM:, :] = jnp.ones((ATT_KV_HEADS, ONES_ROWS, seq), BF16)

    @pl.when(i == 0)
    def _():
        for kvh in range(ATT_KV_HEADS):
            hs = slice(kvh * HEAD_DIM, (kvh + 1) * HEAD_DIM)
            va_scr[kvh, 0:HEAD_DIM, :] = vt_ref[0, hs, :].astype(BF16)
        for cidx in range(seq // K_SUPER):
            rows = slice(cidx * K_SUPER, (cidx + 1) * K_SUPER)
            kc = kt_ref[0, :, rows].T
            for kvh in range(ATT_KV_HEADS):
                hs = slice(kvh * HEAD_DIM, (kvh + 1) * HEAD_DIM)
                kh_scr[kvh, rows, :] = kc[:, hs].astype(BF16)

    qt = q_ref[...].astype(F32).T
    for kvh in range(ATT_KV_HEADS):
        qt_scr[kvh] = jnp.concatenate(
            [qt[(kvh * ATT_GQ + g) * HEAD_DIM:(kvh * ATT_GQ + g + 1) * HEAD_DIM, :]
             for g in range(ATT_GQ)], axis=1).astype(BF16)
    m_scr[...] = jnp.full(m_scr.shape, NEG, F32)
    acc_scr[...] = jnp.zeros(acc_scr.shape, F32)
    par = i % Q_PER_SUPER
    j_last = i // Q_PER_SUPER
    n_sb = jnp.minimum(j_last + 1, n_sb_max)

    def body(dl, carry):
        start = pl.multiple_of((j_last - dl) * K_SUPER, K_SUPER)
        bias = bias_scr[par + Q_PER_SUPER * dl]
        bias2 = jnp.concatenate([bias, bias], axis=1)
        for kvh in range(ATT_KV_HEADS):
            s_scr[kvh] = _dot(kh_scr[kvh, pl.ds(start, K_SUPER), :], qt_scr[kvh])
        for kvh in range(ATT_KV_HEADS):
            vtb = va_scr[kvh, :, pl.ds(start, K_SUPER)]
            for pr in range(pairs_per_kv):
                u = kvh * pairs_per_kv + pr
                s = s_scr[kvh, :, pr * pair_w:(pr + 1) * pair_w] + bias2
                m_old = m_scr[u, 0:1, :]
                m_new = jnp.maximum(m_old, jnp.max(s, axis=0, keepdims=True))
                p = jnp.exp2(s - m_new).astype(BF16)
                acc_scr[u] = jnp.exp2(m_old - m_new) * acc_scr[u] + _dot(vtb, p)
                m_scr[u, 0:1, :] = m_new
        return carry

    lax.fori_loop(0, n_sb, body, 0)

    for u in range(ATT_HEADS // 2):
        acc = acc_scr[u]
        o = acc[0:HEAD_DIM] / acc[HEAD_DIM:HEAD_DIM + 1]
        for e in range(2):
            h = 2 * u + e
            ot_scr[h * HEAD_DIM:(h + 1) * HEAD_DIM, :] = o[:, e * Q_BLOCK:(e + 1) * Q_BLOCK]
    o_ref[...] = (ot_scr[...].T * g_ref[...]).astype(BF16)


def _attn_prompt(q, kt, vt, gate, batch, seq):
    rows = q.shape[0]
    nqb = seq // Q_BLOCK
    assert seq % K_SUPER == 0
    n_sb_max = min(seq // K_SUPER, W_MAX // K_SUPER + 1)
    n_tab = Q_PER_SUPER * n_sb_max
    qblk = lambda b, i: (b * nqb + i, 0)
    per_b = lambda b, i: (b, 0, 0)
    return pl.pallas_call(
        functools.partial(_attn_prompt_kernel, n_tab=n_tab, n_sb_max=n_sb_max),
        grid=(batch, nqb),
        in_specs=[
            pl.BlockSpec((Q_BLOCK, ATT_INNER), qblk),
            pl.BlockSpec((1, KV_DIM, seq), per_b),
            pl.BlockSpec((1, KV_DIM, seq), per_b),
            pl.BlockSpec((Q_BLOCK, ATT_INNER), qblk),
        ],
        out_specs=pl.BlockSpec((Q_BLOCK, ATT_INNER), qblk),
        out_shape=jax.ShapeDtypeStruct((rows, ATT_INNER), BF16),
        scratch_shapes=[
            pltpu.VMEM((n_tab, K_SUPER, Q_BLOCK), F32),
            pltpu.VMEM((ATT_KV_HEADS, seq, HEAD_DIM), BF16),
            pltpu.VMEM((ATT_KV_HEADS, ACC_ROWS, seq), BF16),
            pltpu.VMEM((ATT_KV_HEADS, HEAD_DIM, ATT_GQ * Q_BLOCK), BF16),
            pltpu.VMEM((ATT_HEADS // 2, SUBLANES, 2 * Q_BLOCK), F32),
            pltpu.VMEM((ATT_HEADS // 2, ACC_ROWS, 2 * Q_BLOCK), F32),
            pltpu.VMEM((ATT_INNER, Q_BLOCK), F32),
            pltpu.VMEM((ATT_KV_HEADS, K_SUPER, ATT_GQ * Q_BLOCK), F32),
        ],
        compiler_params=pltpu.CompilerParams(
            dimension_semantics=("arbitrary", "arbitrary"), vmem_limit_bytes=VMEM_LIMIT),
        name="attn_prompt",
    )(q, kt, vt, gate)


def _attn_sample_kernel(q_ref, knt_ref, vnt_ref, g_ref, ckt_ref, cvt_ref, o_ref, kot_ref, vot_ref, *, ts):
    win = ckt_ref.shape[2]
    tp = SAMPLE_PAD
    rows = ATT_GQ * tp
    t_c = lax.broadcasted_iota(jnp.int32, (rows, win), 0) % tp
    j_c = lax.broadcasted_iota(jnp.int32, (rows, win), 1)
    w_c = _multiplicity(win + t_c - j_c)
    t_n = lax.broadcasted_iota(jnp.int32, (rows, tp), 0) % tp
    j_n = lax.broadcasted_iota(jnp.int32, (rows, tp), 1)
    w_n = jnp.where(j_n < ts, _multiplicity(t_n - j_n), 0.0)
    for kvh in range(ATT_KV_HEADS):
        c0 = kvh * ATT_GQ * HEAD_DIM
        qh = jnp.concatenate(
            [q_ref[:, c0 + g * HEAD_DIM:c0 + (g + 1) * HEAD_DIM] for g in range(ATT_GQ)], axis=0)
        hs = slice(kvh * HEAD_DIM, (kvh + 1) * HEAD_DIM)
        s_c = jnp.where(w_c > 0.0, _dot(qh, ckt_ref[0, hs, :].astype(BF16)), NEG)
        s_n = jnp.where(w_n > 0.0, _dot(qh, knt_ref[0, hs, :].astype(BF16)), NEG)
        m = jnp.maximum(jnp.max(s_c, axis=-1, keepdims=True), jnp.max(s_n, axis=-1, keepdims=True))
        p_c = jnp.exp2(s_c - m) * w_c
        p_n = jnp.exp2(s_n - m) * w_n
        den = jnp.sum(p_c, axis=-1, keepdims=True) + jnp.sum(p_n, axis=-1, keepdims=True)
        num = (_dot_nt(p_c.astype(BF16), cvt_ref[0, hs, :].astype(BF16))
               + _dot_nt(p_n.astype(BF16), vnt_ref[0, hs, :].astype(BF16)))
        o = num / den
        o = jnp.concatenate([o[g * tp:(g + 1) * tp] for g in range(ATT_GQ)], axis=1)
        csl = slice(c0, c0 + ATT_GQ * HEAD_DIM)
        o_ref[:, csl] = (o * g_ref[:, csl]).astype(BF16)
    kot_ref[0, :, 0:win - ts] = ckt_ref[0, :, ts:win]
    kot_ref[0, :, win - ts:win] = knt_ref[0, :, 0:ts]
    vot_ref[0, :, 0:win - ts] = cvt_ref[0, :, ts:win]
    vot_ref[0, :, win - ts:win] = vnt_ref[0, :, 0:ts]


def _attn_sample(q, knt, vnt, gate, ckt, cvt, ts):
    batch, _, win = ckt.shape
    tp = SAMPLE_PAD
    blk = lambda b: (b, 0)
    per_b = lambda b: (b, 0, 0)
    return pl.pallas_call(
        functools.partial(_attn_sample_kernel, ts=ts),
        grid=(batch,),
        in_specs=[
            pl.BlockSpec((tp, ATT_INNER), blk),
            pl.BlockSpec((1, KV_DIM, tp), per_b),
            pl.BlockSpec((1, KV_DIM, tp), per_b),
            pl.BlockSpec((tp, ATT_INNER), blk),
            pl.BlockSpec((1, KV_DIM, win), per_b),
            pl.BlockSpec((1, KV_DIM, win), per_b),
        ],
        out_specs=[
            pl.BlockSpec((tp, ATT_INNER), blk),
            pl.BlockSpec((1, KV_DIM, win), per_b),
            pl.BlockSpec((1, KV_DIM, win), per_b),
        ],
        out_shape=[
            jax.ShapeDtypeStruct((batch * tp, ATT_INNER), BF16),
            jax.ShapeDtypeStruct((batch, KV_DIM, win), F32),
            jax.ShapeDtypeStruct((batch, KV_DIM, win), F32),
        ],
        compiler_params=pltpu.CompilerParams(
            dimension_semantics=("arbitrary",), vmem_limit_bytes=VMEM_LIMIT),
        name="attn_sample",
    )(q, knt, vnt, gate, ckt, cvt)


def _outproj_kernel(x_ref, ys_ref, ya_ref, w_ref, o_ref):
    o_ref[...] = (x_ref[...] + _dot(ys_ref[...], w_ref[0:SSD_INNER, :])
                  + _dot(ya_ref[...], w_ref[SSD_INNER:, :]))


def _out_proj(x2d, y_ssd, y_att, w_o, tm):
    rows = x2d.shape[0]
    row = lambda i: (i, 0)
    const = lambda i: (0, 0)
    return pl.pallas_call(
        _outproj_kernel,
        grid=(rows // tm,),
        in_specs=[
            pl.BlockSpec((tm, D_MODEL), row),
            pl.BlockSpec((tm, SSD_INNER), row),
            pl.BlockSpec((tm, ATT_INNER), row),
            pl.BlockSpec((SSD_INNER + ATT_INNER, D_MODEL), const, pipeline_mode=pl.Buffered(1)),
        ],
        out_specs=pl.BlockSpec((tm, D_MODEL), row),
        out_shape=jax.ShapeDtypeStruct((rows, D_MODEL), F32),
        compiler_params=pltpu.CompilerParams(
            dimension_semantics=("arbitrary",), vmem_limit_bytes=VMEM_LIMIT),
        name="out_proj",
    )(x2d, y_ssd, y_att, w_o)


def _rope_tables(pos):
    n = pos.shape[0]
    half = ROT_DIM // 2
    inv = ROPE_THETA ** (-np.arange(0, ROT_DIM, 2, dtype=np.float64) / ROT_DIM)
    ang = pos.astype(np.float64)[:, None] * inv[None, :]
    cos, sin = np.cos(ang), np.sin(ang)
    rest = HEAD_DIM - ROT_DIM
    zh = np.zeros((n, half))
    cos_h = np.concatenate([cos, cos, np.ones((n, rest))], axis=1)
    sa_h = np.concatenate([zh, sin, np.zeros((n, rest))], axis=1)
    sb_h = np.concatenate([-sin, zh, np.zeros((n, rest))], axis=1)
    rep = LANES // HEAD_DIM
    return tuple(jnp.asarray(np.tile(t, (1, rep)), dtype=F32) for t in (cos_h, sa_h, sb_h))


def _lane_pad(v, n=LANES):
    return jnp.pad(v, (0, n - v.shape[0])).reshape(1, n)


def _to_cache(xt, batch, seq):
    return xt.reshape(1, batch, ATT_KV_HEADS, HEAD_DIM, seq).transpose(0, 1, 4, 2, 3)


def kernel(x_prompt, x_sample, cache_k, cache_v, state_conv, state_ssm, norm_w, w_in, conv_w,
           conv_b, dt_bias, a_log, d_skip, ssd_norm_w, q_norm_w, k_norm_w, w_out):
    bp, tp_, _ = x_prompt.shape
    bs, ts, _ = x_sample.shape
    depth = w_in.shape[0]
    assert depth == 1 and tp_ % SSD_CHUNK == 0 and ts <= SAMPLE_PAD and ts >= CONV_W - 1
    l = 0
    win = cache_k.shape[2]

    w_t = jnp.swapaxes(w_in[l], 0, 1).astype(BF16)
    w_o = w_out[l].astype(BF16)
    nw = norm_w[l].reshape(1, D_MODEL)
    rep = LANES // HEAD_DIM
    qnw = jnp.tile(q_norm_w[l], rep).reshape(1, LANES)
    knw = jnp.tile(k_norm_w[l], rep).reshape(1, LANES)
    cw = jnp.pad(conv_w[l], ((0, SUBLANES - CONV_W), (0, 0)))
    cb = conv_b[l].reshape(1, CONV_DIM)
    dtb = _lane_pad(dt_bias[l])
    alog = _lane_pad(a_log[l])
    dexp = jnp.repeat(d_skip[l], SSD_HEADDIM).reshape(1, SSD_INNER)
    snw = ssd_norm_w[l].reshape(1, SSD_INNER)

    tm = TM_IN
    xp2 = x_prompt.reshape(bp * tp_, D_MODEL)
    tabs = _rope_tables(np.arange(tp_))
    zs, xbc, dtr, q, kt, vt, gs, ctail = _in_proj(xp2, nw, w_t, qnw, knw, tabs, tm, tp_)
    y_ssd, h_p = _ssd(xbc, dtr, zs, jnp.zeros((bp, SUBLANES, CONV_DIM), F32),
                      jnp.zeros((bp, SSD_INNER, SSD_STATE), F32),
                      cw, cb, dtb, alog, dexp, snw, bp, SSD_CHUNK, SSD_CHUNK)
    y_att = _attn_prompt(q, kt, vt, gs, bp, tp_)
    y_p = _out_proj(xp2, y_ssd, y_att, w_o, TM_OUT).reshape(bp, tp_, D_MODEL)
    keep = min(W_MAX, tp_)
    k_p = _to_cache(kt, bp, tp_)[:, :, tp_ - keep:]
    v_p = _to_cache(vt, bp, tp_)[:, :, tp_ - keep:]
    c_p = ctail[:, SUBLANES - (CONV_W - 1):][None]
    h_p = h_p.reshape(1, bp, SSD_HEADS, SSD_HEADDIM, SSD_STATE)

    pad = SAMPLE_PAD
    xs2 = jnp.pad(x_sample, ((0, 0), (0, pad - ts), (0, 0))).reshape(bs * pad, D_MODEL)
    tabs_s = _rope_tables(PAST_LEN + np.arange(bs * pad) % pad)
    rows_s = bs * pad
    zs, xbc, dtr, q, kt, vt, gs, _ = _in_proj(xs2, nw, w_t, qnw, knw, tabs_s, rows_s, rows_s)
    cprev = jnp.pad(state_conv[l], ((0, 0), (SUBLANES - (CONV_W - 1), 0), (0, 0)))
    y_ssd, h_s = _ssd(xbc, dtr, zs, cprev, state_ssm[l].reshape(bs, SSD_INNER, SSD_STATE),
                      cw, cb, dtb, alog, dexp, snw, bs, pad, ts)
    knt = kt.reshape(KV_DIM, bs, pad).transpose(1, 0, 2)
    vnt = vt.reshape(KV_DIM, bs, pad).transpose(1, 0, 2)
    ckt = cache_k[l].transpose(0, 2, 3, 1).reshape(bs, KV_DIM, win)
    cvt = cache_v[l].transpose(0, 2, 3, 1).reshape(bs, KV_DIM, win)
    y_att, kot, vot = _attn_sample(q, knt, vnt, gs, ckt, cvt, ts)
    y_s = _out_proj(xs2, y_ssd, y_att, w_o, rows_s).reshape(bs, pad, D_MODEL)[:, :ts]
    k_s = _to_cache(kot, bs, win)
    v_s = _to_cache(vot, bs, win)
    c_s = xbc.reshape(bs, pad, CONV_DIM)[:, ts - (CONV_W - 1):ts][None]
    h_s = h_s.reshape(1, bs, SSD_HEADS, SSD_HEADDIM, SSD_STATE)

    return (y_p, y_s, k_p, v_p, c_p, h_p, k_s, v_s, c_s, h_s)
```

```python
import functools

import jax
import jax.numpy as jnp
import numpy as np
from jax import lax
from jax.experimental import pallas as pl
from jax.experimental.pallas import tpu as pltpu

F32 = jnp.float32
BF16 = jnp.bfloat16

D_MODEL = 2048
SSD_HEADS = 16
SSD_HEADDIM = 64
SSD_INNER = SSD_HEADS * SSD_HEADDIM
SSD_GROUPS = 2
SSD_STATE = 128
CONV_W = 4
CONV_DIM = SSD_INNER + 2 * SSD_GROUPS * SSD_STATE
SSD_CHUNK = 128
ATT_HEADS = 16
ATT_KV_HEADS = 4
HEAD_DIM = 64
ATT_GQ = ATT_HEADS // ATT_KV_HEADS
ATT_INNER = ATT_HEADS * HEAD_DIM
KV_DIM = ATT_KV_HEADS * HEAD_DIM
ROT_DIM = HEAD_DIM // 4
ROPE_THETA = 500000.0
DILATED_BRANCHES = ((128, 1), (512, 4), (2048, 16))
W_MAX = 2048
PAST_LEN = 16384
EPS = 1e-6

LANES = 128
SUBLANES = 8
Q_BLOCK = 256
K_SUPER = 256
Q_PER_SUPER = K_SUPER // Q_BLOCK
ONES_ROWS = 16
ACC_ROWS = HEAD_DIM + ONES_ROWS
CONV_COLS = 256
TM_IN = 512
TM_OUT = 512
SSD_SEQS = 2
SAMPLE_PAD = 16
NEG = -1e30
VMEM_LIMIT = 56 * 1024 * 1024

Z0 = 0
X0 = Z0 + SSD_INNER
DT0 = X0 + CONV_DIM
Q0 = DT0 + SSD_HEADS
K0 = Q0 + ATT_INNER
V0 = K0 + KV_DIM
G0 = V0 + KV_DIM
W_ROWS = G0 + ATT_INNER
W_CHUNK = W_ROWS // 3
WO_CHUNK = 512
HALF_INNER = SSD_INNER // SSD_GROUPS
LOG2E = 1.4426950408889634
Q_SCALE = HEAD_DIM ** -0.5 * LOG2E


def _dot(a, b):
    return jnp.dot(a, b, preferred_element_type=F32)


def _dot_nt(a, b):
    return lax.dot_general(a, b, (((1,), (1,)), ((), ())), preferred_element_type=F32)


def _split3(x):
    hi = x.astype(BF16)
    r1 = x - hi.astype(F32)
    mid = r1.astype(BF16)
    lo = (r1 - mid.astype(F32)).astype(BF16)
    return hi, mid, lo


def _dot_exact_rhs(x, m):
    hi, mid, lo = _split3(x)
    return _dot(hi, m) + _dot(mid, m) + _dot(lo, m)


def _dot_wide_rhs(x, m):
    hi = x.astype(BF16)
    lo = (x - hi.astype(F32)).astype(BF16)
    return _dot(hi, m) + _dot(lo, m)


def _dot_exact_lhs(m, x):
    hi, mid, lo = _split3(x)
    return _dot(m, hi) + _dot(m, mid) + _dot(m, lo)


def _silu(x):
    hx = 0.5 * x
    return hx * jnp.tanh(hx) + hx


def _multiplicity(d):
    w = jnp.zeros(d.shape, F32)
    for window, dil in DILATED_BRANCHES:
        hit = (d >= 0) & (d <= window) & (lax.rem(d, dil) == 0)
        w = w + jnp.where(hit, 1.0, 0.0)
    return w


def _norm_rope(y, nw, cos, sa, sb, scale):
    lane = lax.broadcasted_iota(jnp.int32, (1, LANES), 1)
    first = lane < HEAD_DIM
    y2 = y * y
    s_lo = jnp.sum(jnp.where(first, y2, 0.0), axis=-1, keepdims=True)
    s_hi = jnp.sum(jnp.where(first, 0.0, y2), axis=-1, keepdims=True)
    ms = jnp.where(first, s_lo, s_hi) * (1.0 / HEAD_DIM)
    yn = y * lax.rsqrt(ms + EPS) * nw
    half = ROT_DIM // 2
    rot = yn * cos + pltpu.roll(yn, half, 1) * sa + pltpu.roll(yn, LANES - half, 1) * sb
    return rot * scale


def _causal_conv_silu(load, cw_ref, cb_ref, store):
    for c0 in range(0, CONV_DIM, CONV_COLS):
        cs = slice(c0, c0 + CONV_COLS)
        xe = load(cs)
        x2 = pltpu.roll(xe, 2, 0)
        even = cw_ref[3:4, cs] * xe + cw_ref[1:2, cs] * x2
        odd = cw_ref[2:3, cs] * xe + cw_ref[0:1, cs] * x2
        store(cs, _silu((even + pltpu.roll(odd, 1, 0))[SUBLANES:, :] + cb_ref[:, cs]))


def _pre_norm(x_ref, nw_ref):
    x = x_ref[...]
    ms = jnp.mean(x * x, axis=-1, keepdims=True)
    return (x * lax.rsqrt(ms + EPS) * nw_ref[...]).astype(BF16)


def _inproj_kernel(x_ref, nw_ref, w_ref, *refs):
    hn = _pre_norm(x_ref, nw_ref)

    def proj(a, b):
        return _dot_nt(hn, w_ref[a:b, :])

    _emit_sections(proj, x_ref.shape[0], *refs)


def _emit_sections(proj, tm, qnw_ref, knw_ref, cos_ref, sa_ref, sb_ref,
                   zs_ref, xbc_ref, dt_ref, q_ref, kt_ref, vt_ref, gs_ref, ctail_ref):
    cos, sa, sb = cos_ref[...], sa_ref[...], sb_ref[...]
    q = proj(Q0, K0)
    for c in range(ATT_INNER // LANES):
        sl = slice(c * LANES, (c + 1) * LANES)
        q_ref[:, sl] = _norm_rope(q[:, sl], qnw_ref[...], cos, sa, sb, Q_SCALE).astype(BF16)
    k = proj(K0, V0)
    kn = [_norm_rope(k[:, c * LANES:(c + 1) * LANES], knw_ref[...], cos, sa, sb, 1.0)
          for c in range(KV_DIM // LANES)]
    kt_ref[0] = jnp.concatenate(kn, axis=1).T

    xbc = proj(X0, DT0)
    xbc_ref[...] = xbc
    ctail_ref[0] = xbc[tm - SUBLANES:tm, :]
    vt_ref[0] = proj(V0, G0).T
    zs_ref[...] = _silu(proj(Z0, X0))
    gs_ref[...] = _silu(proj(G0, W_ROWS))
    dt_ref[...] = proj(DT0, DT0 + LANES)


def _in_proj(x2d, nw, w_t, qnw, knw, tables, tm, seq):
    rows = x2d.shape[0]
    const = lambda i: (0, 0)
    row = lambda i: (i, 0)
    sec_in, out_specs, out_shape = _section_specs(rows, tm, seq, tables[0].shape[0] // tm)
    return pl.pallas_call(
        _inproj_kernel,
        grid=(rows // tm,),
        in_specs=[
            pl.BlockSpec((tm, D_MODEL), row),
            pl.BlockSpec((1, D_MODEL), const),
            pl.BlockSpec((W_ROWS, D_MODEL), const, pipeline_mode=pl.Buffered(1)),
        ] + sec_in,
        out_specs=out_specs,
        out_shape=out_shape,
        compiler_params=pltpu.CompilerParams(
            dimension_semantics=("arbitrary",), vmem_limit_bytes=VMEM_LIMIT),
        name="in_proj",
    )(x2d, nw, w_t, qnw, knw, *tables)


def _section_specs(rows, tm, seq, period):
    per_seq = seq // tm
    const = lambda i: (0, 0)
    row = lambda i: (i, 0)
    tab = lambda i: (i % period, 0)
    tr = lambda i: (i // per_seq, 0, i % per_seq)
    slab = lambda i: (i // per_seq, 0, 0)
    in_specs = [pl.BlockSpec((1, LANES), const)] * 2 + [pl.BlockSpec((tm, LANES), tab)] * 3
    row_outs = {0: (SSD_INNER, F32), 1: (CONV_DIM, F32), 2: (LANES, F32), 3: (ATT_INNER, BF16),
                6: (ATT_INNER, F32)}
    out_specs, out_shape = [], []
    for idx in range(7):
        if idx in row_outs:
            n, dt = row_outs[idx]
            out_specs.append(pl.BlockSpec((tm, n), row))
            out_shape.append(jax.ShapeDtypeStruct((rows, n), dt))
        else:
            out_specs.append(pl.BlockSpec((1, KV_DIM, tm), tr))
            out_shape.append(jax.ShapeDtypeStruct((rows // seq, KV_DIM, seq), F32))
    out_specs.append(pl.BlockSpec((1, SUBLANES, CONV_DIM), slab))
    out_shape.append(jax.ShapeDtypeStruct((rows // seq, SUBLANES, CONV_DIM), F32))
    return in_specs, out_specs, out_shape


def _wconv_in_kernel(x_ref, nw_ref, w_ref, wb_ref, p_ref, hn_scr):
    @pl.when(pl.program_id(0) == 0)
    def _():
        hn_scr[...] = _pre_norm(x_ref, nw_ref)

    wb = w_ref[...].astype(BF16)
    wb_ref[...] = wb
    p_ref[0] = _dot_nt(hn_scr[...], wb)


def _wconv_in(x2d, nw, w_f32):
    rows = x2d.shape[0]
    n_chunks = W_ROWS // W_CHUNK
    const = lambda j: (0, 0)
    return pl.pallas_call(
        _wconv_in_kernel,
        grid=(n_chunks,),
        in_specs=[
            pl.BlockSpec((rows, D_MODEL), const),
            pl.BlockSpec((1, D_MODEL), const),
            pl.BlockSpec((W_CHUNK, D_MODEL), lambda j: (j, 0)),
        ],
        out_specs=[
            pl.BlockSpec((W_CHUNK, D_MODEL), lambda j: (j, 0)),
            pl.BlockSpec((1, rows, W_CHUNK), lambda j: (j, 0, 0)),
        ],
        out_shape=[
            jax.ShapeDtypeStruct((W_ROWS, D_MODEL), BF16),
            jax.ShapeDtypeStruct((n_chunks, rows, W_CHUNK), F32),
        ],
        scratch_shapes=[pltpu.VMEM((rows, D_MODEL), BF16)],
        compiler_params=pltpu.CompilerParams(
            dimension_semantics=("arbitrary",), vmem_limit_bytes=VMEM_LIMIT),
        name="wconv_in",
    )(x2d, nw, w_f32)


def _sections_kernel(p_ref, *refs):
    pfull = jnp.concatenate([p_ref[j] for j in range(p_ref.shape[0])], axis=1)
    _emit_sections(lambda a, b: pfull[:, a:b], p_ref.shape[1], *refs)


def _sections(p, qnw, knw, tables):
    rows = p.shape[1]
    sec_in, out_specs, out_shape = _section_specs(rows, rows, rows, 1)
    return pl.pallas_call(
        _sections_kernel,
        grid=(1,),
        in_specs=[pl.BlockSpec(p.shape, lambda i: (0, 0, 0))] + sec_in,
        out_specs=out_specs,
        out_shape=out_shape,
        compiler_params=pltpu.CompilerParams(
            dimension_semantics=("arbitrary",), vmem_limit_bytes=VMEM_LIMIT),
        name="sections",
    )(p, qnw, knw, *tables)


def _ssd_chain(s, xbc_ref, dtr_ref, z_ref, cprev_ref, h0_ref, cw_ref, cb_ref, dtb_ref, alog_ref,
               dexp_ref, nw_ref, y_ref, hout_ref, xext, ht, xc_scr, lin, valid):
    L = SSD_CHUNK
    c = pl.program_id(1)
    last = pl.num_programs(1) - 1

    @pl.when(c == 0)
    def _():
        xext[s, 0:SUBLANES, :] = cprev_ref[s]
        for g in range(SSD_GROUPS):
            ht[s, g] = h0_ref[s, g * HALF_INNER:(g + 1) * HALF_INNER, :].T

    def pad_rows(v):
        if lin == L:
            return v
        return jnp.concatenate([v, jnp.zeros((L - lin, v.shape[1]), v.dtype)], axis=0)

    lane = lax.broadcasted_iota(jnp.int32, (1, LANES), 1)
    rowid = lax.broadcasted_iota(jnp.int32, (L, 1), 0)
    dt = jax.nn.softplus(pad_rows(dtr_ref[s]) + dtb_ref[...])
    dt = jnp.where((lane < SSD_HEADS) & (rowid < valid), dt, 0.0)
    dta = dt * (-jnp.exp(alog_ref[...]) * LOG2E)

    r2 = lax.broadcasted_iota(jnp.int32, (L, L), 0)
    c2 = lax.broadcasted_iota(jnp.int32, (L, L), 1)
    tri = r2 >= c2
    tri_b = jnp.where(tri, 1.0, 0.0).astype(BF16)
    cum = _dot_exact_lhs(tri_b, dta)
    cum_last = cum[L - 1:L, :]
    yield

    er = lax.broadcasted_iota(jnp.int32, (LANES, SSD_INNER), 0)
    ec = lax.broadcasted_iota(jnp.int32, (LANES, SSD_INNER), 1)
    expand = jnp.where(ec // SSD_HEADDIM == er, 1.0, 0.0).astype(BF16)
    ecum = _dot_wide_rhs(jnp.exp2(cum), expand)
    wexp = _dot_wide_rhs(jnp.exp2(cum_last - cum) * dt, expand)
    cd = _dot_exact_rhs(jnp.broadcast_to(jnp.exp2(cum_last), (SUBLANES, LANES)), expand)[0:1, :]
    cum_t = cum.T
    dt_t = dt.T
    yield

    if lin < L:
        xext[s, SUBLANES + lin:SUBLANES + L, :] = jnp.zeros((L - lin, CONV_DIM), F32)
    xext[s, SUBLANES:SUBLANES + lin, :] = xbc_ref[s]

    def load(cs):
        return xext[s, 0:SUBLANES + L, cs]

    def store(cs, v):
        xc_scr[s, :, cs] = v

    _causal_conv_silu(load, cw_ref, cb_ref, store)
    xext[s, 0:SUBLANES, :] = xext[s, L:L + SUBLANES, :]
    xc = xc_scr.at[s]
    xs = xc[:, :SSD_INNER]
    bm = xc[:, SSD_INNER:SSD_INNER + SSD_GROUPS * SSD_STATE]
    cm = xc[:, SSD_INNER + SSD_GROUPS * SSD_STATE:]
    bm_b = bm.astype(BF16)
    cm_b = cm.astype(BF16)
    xs_b = xs.astype(BF16)

    y_off = jnp.concatenate(
        [_dot(cm_b[:, g * SSD_STATE:(g + 1) * SSD_STATE], ht[s, g].astype(BF16))
         for g in range(SSD_GROUPS)], axis=1) * ecum
    cbs = [_dot_nt(cm_b[:, g * SSD_STATE:(g + 1) * SSD_STATE], bm_b[:, g * SSD_STATE:(g + 1) * SSD_STATE])
           for g in range(SSD_GROUPS)]
    yield

    lane_half = lane // HEAD_DIM
    heads_per_group = SSD_HEADS // SSD_GROUPS
    y_parts = []
    for g in range(SSD_GROUPS):
        for pr in range(heads_per_group // 2):
            col0 = g * HALF_INNER + pr * LANES
            xp = xs_b[:, col0:col0 + LANES]
            yp = jnp.zeros((L, LANES), F32)
            for e in range(2):
                hh = g * heads_per_group + pr * 2 + e
                decay = jnp.exp2(cum[:, hh:hh + 1] - cum_t[hh:hh + 1, :])
                sc = jnp.where(tri, cbs[g] * decay, 0.0) * dt_t[hh:hh + 1, :]
                xm = jnp.where(lane_half == e, xp, jnp.zeros_like(xp))
                yp = yp + _dot(sc.astype(BF16), xm)
            y_parts.append(yp)
            if pr % 2 == 1:
                yield
    y_diag = jnp.concatenate(y_parts, axis=1)

    xw = (xs * wexp).astype(BF16)
    for g in range(SSD_GROUPS):
        sl = slice(g * HALF_INNER, (g + 1) * HALF_INNER)
        bm_t = bm[:, g * SSD_STATE:(g + 1) * SSD_STATE].T.astype(BF16)
        ht[s, g] = ht[s, g] * cd[:, sl] + _dot(bm_t, xw[:, sl])
    yield

    y = (y_diag + y_off + dexp_ref[...] * xs) * pad_rows(z_ref[s])
    yn = []
    for g in range(SSD_GROUPS):
        yg = y[:, g * HALF_INNER:(g + 1) * HALF_INNER]
        ms = jnp.mean(yg * yg, axis=-1, keepdims=True)
        yn.append(yg * lax.rsqrt(ms + EPS))
    yn = jnp.concatenate(yn, axis=1) * nw_ref[...]
    y_ref[s] = yn[:lin].astype(BF16)

    @pl.when(c == last)
    def _():
        for g in range(SSD_GROUPS):
            hout_ref[s, g * HALF_INNER:(g + 1) * HALF_INNER, :] = ht[s, g].T


def _ssd_kernel(*refs, lin, valid):
    chains = [_ssd_chain(s, *refs, lin, valid) for s in range(SSD_SEQS)]
    while chains:
        alive = []
        for ch in chains:
            try:
                next(ch)
                alive.append(ch)
            except StopIteration:
                pass
        chains = alive


def _ssd(xbc, dtr, z, cprev, h0, cw, cb, dtb, alog, dexp, nw, batch, lin, valid):
    rows = xbc.shape[0]
    seq = rows // batch
    nc = seq // lin
    g = SSD_SEQS
    assert batch % g == 0
    blk = lambda b, c: (b, c, 0)
    per_b = lambda b, c: (b, 0, 0)
    const = lambda b, c: (0, 0)
    y, hout = pl.pallas_call(
        functools.partial(_ssd_kernel, lin=lin, valid=valid),
        grid=(batch // g, nc),
        in_specs=[
            pl.BlockSpec((g, lin, CONV_DIM), blk),
            pl.BlockSpec((g, lin, LANES), blk),
            pl.BlockSpec((g, lin, SSD_INNER), blk),
            pl.BlockSpec((g, SUBLANES, CONV_DIM), per_b),
            pl.BlockSpec((g, SSD_INNER, SSD_STATE), per_b),
            pl.BlockSpec((SUBLANES, CONV_DIM), const),
            pl.BlockSpec((1, CONV_DIM), const),
            pl.BlockSpec((1, LANES), const),
            pl.BlockSpec((1, LANES), const),
            pl.BlockSpec((1, SSD_INNER), const),
            pl.BlockSpec((1, SSD_INNER), const),
        ],
        out_specs=[
            pl.BlockSpec((g, lin, SSD_INNER), blk),
            pl.BlockSpec((g, SSD_INNER, SSD_STATE), per_b),
        ],
        out_shape=[
            jax.ShapeDtypeStruct((batch, seq, SSD_INNER), BF16),
            jax.ShapeDtypeStruct((batch, SSD_INNER, SSD_STATE), F32),
        ],
        scratch_shapes=[
            pltpu.VMEM((g, SUBLANES + SSD_CHUNK + SUBLANES, CONV_DIM), F32),
            pltpu.VMEM((g, SSD_GROUPS, SSD_STATE, HALF_INNER), F32),
            pltpu.VMEM((g, SSD_CHUNK, CONV_DIM), F32),
        ],
        compiler_params=pltpu.CompilerParams(
            dimension_semantics=("arbitrary", "arbitrary"), vmem_limit_bytes=VMEM_LIMIT),
        name="ssd",
    )(xbc.reshape(batch, seq, CONV_DIM), dtr.reshape(batch, seq, LANES),
      z.reshape(batch, seq, SSD_INNER), cprev, h0, cw, cb, dtb, alog, dexp, nw)
    return y.reshape(rows, SSD_INNER), hout


def _attn_prompt_kernel(q_ref, kt_ref, vt_ref, g_ref, o_ref,
                        bias_scr, kh_scr, va_scr, qt_scr, m_scr, acc_scr, ot_scr, s_scr, *, n_tab, n_sb_max):
    b = pl.program_id(0)
    i = pl.program_id(1)
    seq = kt_ref.shape[2]
    pair_w = 2 * Q_BLOCK
    pairs_per_kv = ATT_GQ // 2

    @pl.when((b == 0) & (i == 0))
    def _():
        r = lax.broadcasted_iota(jnp.int32, (K_SUPER, Q_BLOCK), 0)
        c = lax.broadcasted_iota(jnp.int32, (K_SUPER, Q_BLOCK), 1)
        for tb in range(n_tab):
            w = _multiplicity(tb * Q_BLOCK + c - r)
            bias_scr[tb] = jnp.where(w > 0.0, jnp.log2(jnp.maximum(w, 1.0)), NEG)
        va_scr[:, HEAD_DIM:, :] = jnp.ones((ATT_KV_HEADS, ONES_ROWS, seq), BF16)

    @pl.when(i == 0)
    def _():
        for kvh in range(ATT_KV_HEADS):
            hs = slice(kvh * HEAD_DIM, (kvh + 1) * HEAD_DIM)
            va_scr[kvh, 0:HEAD_DIM, :] = vt_ref[0, hs, :].astype(BF16)
        for cidx in range(seq // K_SUPER):
            rows = slice(cidx * K_SUPER, (cidx + 1) * K_SUPER)
            kc = kt_ref[0, :, rows].T
            for kvh in range(ATT_KV_HEADS):
                hs = slice(kvh * HEAD_DIM, (kvh + 1) * HEAD_DIM)
                kh_scr[kvh, rows, :] = kc[:, hs].astype(BF16)

    qt = q_ref[...].astype(F32).T
    for kvh in range(ATT_KV_HEADS):
        qt_scr[kvh] = jnp.concatenate(
            [qt[(kvh * ATT_GQ + g) * HEAD_DIM:(kvh * ATT_GQ + g + 1) * HEAD_DIM, :]
             for g in range(ATT_GQ)], axis=1).astype(BF16)
    m_scr[...] = jnp.full(m_scr.shape, NEG, F32)
    acc_scr[...] = jnp.zeros(acc_scr.shape, F32)
    par = i % Q_PER_SUPER
    j_last = i // Q_PER_SUPER
    n_sb = jnp.minimum(j_last + 1, n_sb_max)

    def body(dl, carry):
        start = pl.multiple_of((j_last - dl) * K_SUPER, K_SUPER)
        bias = bias_scr[par + Q_PER_SUPER * dl]
        bias2 = jnp.concatenate([bias, bias], axis=1)
        for kvh in range(ATT_KV_HEADS):
            s_scr[kvh] = _dot(kh_scr[kvh, pl.ds(start, K_SUPER), :], qt_scr[kvh])
        for kvh in range(ATT_KV_HEADS):
            vtb = va_scr[kvh, :, pl.ds(start, K_SUPER)]
            for pr in range(pairs_per_kv):
                u = kvh * pairs_per_kv + pr
                s = s_scr[kvh, :, pr * pair_w:(pr + 1) * pair_w] + bias2
                m_old = m_scr[u, 0:1, :]
                m_new = jnp.maximum(m_old, jnp.max(s, axis=0, keepdims=True))
                p = jnp.exp2(s - m_new).astype(BF16)
                acc_scr[u] = jnp.exp2(m_old - m_new) * acc_scr[u] + _dot(vtb, p)
                m_scr[u, 0:1, :] = m_new
        return carry

    lax.fori_loop(0, n_sb, body, 0)

    for u in range(ATT_HEADS // 2):
        acc = acc_scr[u]
        o = acc[0:HEAD_DIM] / acc[HEAD_DIM:HEAD_DIM + 1]
        for e in range(2):
            h = 2 * u + e
            ot_scr[h * HEAD_DIM:(h + 1) * HEAD_DIM, :] = o[:, e * Q_BLOCK:(e + 1) * Q_BLOCK]
    o_ref[...] = (ot_scr[...].T * g_ref[...]).astype(BF16)


def _attn_prompt(q, kt, vt, gate, batch, seq):
    rows = q.shape[0]
    nqb = seq // Q_BLOCK
    assert seq % K_SUPER == 0
    n_sb_max = min(seq // K_SUPER, W_MAX // K_SUPER + 1)
    n_tab = Q_PER_SUPER * n_sb_max
    qblk = lambda b, i: (b * nqb + i, 0)
    per_b = lambda b, i: (b, 0, 0)
    return pl.pallas_call(
        functools.partial(_attn_prompt_kernel, n_tab=n_tab, n_sb_max=n_sb_max),
        grid=(batch, nqb),
        in_specs=[
            pl.BlockSpec((Q_BLOCK, ATT_INNER), qblk),
            pl.BlockSpec((1, KV_DIM, seq), per_b),
            pl.BlockSpec((1, KV_DIM, seq), per_b),
            pl.BlockSpec((Q_BLOCK, ATT_INNER), qblk),
        ],
        out_specs=pl.BlockSpec((Q_BLOCK, ATT_INNER), qblk),
        out_shape=jax.ShapeDtypeStruct((rows, ATT_INNER), BF16),
        scratch_shapes=[
            pltpu.VMEM((n_tab, K_SUPER, Q_BLOCK), F32),
            pltpu.VMEM((ATT_KV_HEADS, seq, HEAD_DIM), BF16),
            pltpu.VMEM((ATT_KV_HEADS, ACC_ROWS, seq), BF16),
            pltpu.VMEM((ATT_KV_HEADS, HEAD_DIM, ATT_GQ * Q_BLOCK), BF16),
            pltpu.VMEM((ATT_HEADS // 2, SUBLANES, 2 * Q_BLOCK), F32),
            pltpu.VMEM((ATT_HEADS // 2, ACC_ROWS, 2 * Q_BLOCK), F32),
            pltpu.VMEM((ATT_INNER, Q_BLOCK), F32),
            pltpu.VMEM((ATT_KV_HEADS, K_SUPER, ATT_GQ * Q_BLOCK), F32),
        ],
        compiler_params=pltpu.CompilerParams(
            dimension_semantics=("arbitrary", "arbitrary"), vmem_limit_bytes=VMEM_LIMIT),
        name="attn_prompt",
    )(q, kt, vt, gate)


def _attn_sample_kernel(q_ref, knt_ref, vnt_ref, g_ref, ckt_ref, cvt_ref, o_ref, kot_ref, vot_ref, *, ts):
    win = ckt_ref.shape[2]
    tp = SAMPLE_PAD
    rows = ATT_GQ * tp
    t_c = lax.broadcasted_iota(jnp.int32, (rows, win), 0) % tp
    j_c = lax.broadcasted_iota(jnp.int32, (rows, win), 1)
    w_c = _multiplicity(win + t_c - j_c)
    t_n = lax.broadcasted_iota(jnp.int32, (rows, tp), 0) % tp
    j_n = lax.broadcasted_iota(jnp.int32, (rows, tp), 1)
    w_n = jnp.where(j_n < ts, _multiplicity(t_n - j_n), 0.0)
    for kvh in range(ATT_KV_HEADS):
        c0 = kvh * ATT_GQ * HEAD_DIM
        qh = jnp.concatenate(
            [q_ref[:, c0 + g * HEAD_DIM:c0 + (g + 1) * HEAD_DIM] for g in range(ATT_GQ)], axis=0)
        hs = slice(kvh * HEAD_DIM, (kvh + 1) * HEAD_DIM)
        s_c = jnp.where(w_c > 0.0, _dot(qh, ckt_ref[0, hs, :].astype(BF16)), NEG)
        s_n = jnp.where(w_n > 0.0, _dot(qh, knt_ref[0, hs, :].astype(BF16)), NEG)
        m = jnp.maximum(jnp.max(s_c, axis=-1, keepdims=True), jnp.max(s_n, axis=-1, keepdims=True))
        p_c = jnp.exp2(s_c - m) * w_c
        p_n = jnp.exp2(s_n - m) * w_n
        den = jnp.sum(p_c, axis=-1, keepdims=True) + jnp.sum(p_n, axis=-1, keepdims=True)
        num = (_dot_nt(p_c.astype(BF16), cvt_ref[0, hs, :].astype(BF16))
               + _dot_nt(p_n.astype(BF16), vnt_ref[0, hs, :].astype(BF16)))
        o = num / den
        o = jnp.concatenate([o[g * tp:(g + 1) * tp] for g in range(ATT_GQ)], axis=1)
        csl = slice(c0, c0 + ATT_GQ * HEAD_DIM)
        o_ref[:, csl] = (o * g_ref[:, csl]).astype(BF16)
    kot_ref[0, :, 0:win - ts] = ckt_ref[0, :, ts:win]
    kot_ref[0, :, win - ts:win] = knt_ref[0, :, 0:ts]
    vot_ref[0, :, 0:win - ts] = cvt_ref[0, :, ts:win]
    vot_ref[0, :, win - ts:win] = vnt_ref[0, :, 0:ts]


def _attn_sample(q, knt, vnt, gate, ckt, cvt, ts):
    batch, _, win = ckt.shape
    tp = SAMPLE_PAD
    blk = lambda b: (b, 0)
    per_b = lambda b: (b, 0, 0)
    return pl.pallas_call(
        functools.partial(_attn_sample_kernel, ts=ts),
        grid=(batch,),
        in_specs=[
            pl.BlockSpec((tp, ATT_INNER), blk),
            pl.BlockSpec((1, KV_DIM, tp), per_b),
            pl.BlockSpec((1, KV_DIM, tp), per_b),
            pl.BlockSpec((tp, ATT_INNER), blk),
            pl.BlockSpec((1, KV_DIM, win), per_b),
            pl.BlockSpec((1, KV_DIM, win), per_b),
        ],
        out_specs=[
            pl.BlockSpec((tp, ATT_INNER), blk),
            pl.BlockSpec((1, KV_DIM, win), per_b),
            pl.BlockSpec((1, KV_DIM, win), per_b),
        ],
        out_shape=[
            jax.ShapeDtypeStruct((batch * tp, ATT_INNER), BF16),
            jax.ShapeDtypeStruct((batch, KV_DIM, win), F32),
            jax.ShapeDtypeStruct((batch, KV_DIM, win), F32),
        ],
        compiler_params=pltpu.CompilerParams(
            dimension_semantics=("arbitrary",), vmem_limit_bytes=VMEM_LIMIT),
        name="attn_sample",
    )(q, knt, vnt, gate, ckt, cvt)


def _outproj_kernel(x_ref, ys_ref, ya_ref, w_ref, o_ref):
    o_ref[...] = (x_ref[...] + _dot(ys_ref[...], w_ref[0:SSD_INNER, :])
                  + _dot(ya_ref[...], w_ref[SSD_INNER:, :]))


def _out_proj(x2d, y_ssd, y_att, w_o, tm):
    rows = x2d.shape[0]
    row = lambda i: (i, 0)
    const = lambda i: (0, 0)
    return pl.pallas_call(
        _outproj_kernel,
        grid=(rows // tm,),
        in_specs=[
            pl.BlockSpec((tm, D_MODEL), row),
            pl.BlockSpec((tm, SSD_INNER), row),
            pl.BlockSpec((tm, ATT_INNER), row),
            pl.BlockSpec((SSD_INNER + ATT_INNER, D_MODEL), const, pipeline_mode=pl.Buffered(1)),
        ],
        out_specs=pl.BlockSpec((tm, D_MODEL), row),
        out_shape=jax.ShapeDtypeStruct((rows, D_MODEL), F32),
        compiler_params=pltpu.CompilerParams(
            dimension_semantics=("arbitrary",), vmem_limit_bytes=VMEM_LIMIT),
        name="out_proj",
    )(x2d, y_ssd, y_att, w_o)


def _wconv_out_kernel(x_ref, ys_ref, ya_ref, w_ref, wb_ref, o_ref):
    j = pl.program_id(0)
    wb = w_ref[...].astype(BF16)
    wb_ref[...] = wb
    y = jnp.where(j < SSD_INNER // WO_CHUNK, ys_ref[...], ya_ref[...])
    part = _dot(y, wb)

    @pl.when(j == 0)
    def _():
        o_ref[...] = x_ref[...] + part

    @pl.when(j > 0)
    def _():
        o_ref[...] += part


def _wconv_out(x2d, y_ssd, y_att, w_f32):
    rows = x2d.shape[0]
    n_ssd = SSD_INNER // WO_CHUNK
    n_chunks = (SSD_INNER + ATT_INNER) // WO_CHUNK
    const = lambda j: (0, 0)
    return pl.pallas_call(
        _wconv_out_kernel,
        grid=(n_chunks,),
        in_specs=[
            pl.BlockSpec((rows, D_MODEL), const),
            pl.BlockSpec((rows, WO_CHUNK), lambda j: (0, jnp.minimum(j, n_ssd - 1))),
            pl.BlockSpec((rows, WO_CHUNK), lambda j: (0, jnp.maximum(j - n_ssd, 0))),
            pl.BlockSpec((WO_CHUNK, D_MODEL), lambda j: (j, 0)),
        ],
        out_specs=[
            pl.BlockSpec((WO_CHUNK, D_MODEL), lambda j: (j, 0)),
            pl.BlockSpec((rows, D_MODEL), const),
        ],
        out_shape=[
            jax.ShapeDtypeStruct((SSD_INNER + ATT_INNER, D_MODEL), BF16),
            jax.ShapeDtypeStruct((rows, D_MODEL), F32),
        ],
        compiler_params=pltpu.CompilerParams(
            dimension_semantics=("arbitrary",), vmem_limit_bytes=VMEM_LIMIT),
        name="wconv_out",
    )(x2d, y_ssd, y_att, w_f32)


def _rope_tables(pos):
    n = pos.shape[0]
    half = ROT_DIM // 2
    inv = ROPE_THETA ** (-np.arange(0, ROT_DIM, 2, dtype=np.float64) / ROT_DIM)
    ang = pos.astype(np.float64)[:, None] * inv[None, :]
    cos, sin = np.cos(ang), np.sin(ang)
    rest = HEAD_DIM - ROT_DIM
    zh = np.zeros((n, half))
    cos_h = np.concatenate([cos, cos, np.ones((n, rest))], axis=1)
    sa_h = np.concatenate([zh, sin, np.zeros((n, rest))], axis=1)
    sb_h = np.concatenate([-sin, zh, np.zeros((n, rest))], axis=1)
    rep = LANES // HEAD_DIM
    return tuple(jnp.asarray(np.tile(t, (1, rep)), dtype=F32) for t in (cos_h, sa_h, sb_h))


def _lane_pad(v, n=LANES):
    return jnp.pad(v, (0, n - v.shape[0])).reshape(1, n)


def _to_cache(xt, batch, seq):
    return xt.reshape(1, batch, ATT_KV_HEADS, HEAD_DIM, seq).transpose(0, 1, 4, 2, 3)


def kernel(x_prompt, x_sample, cache_k, cache_v, state_conv, state_ssm, norm_w, w_in, conv_w,
           conv_b, dt_bias, a_log, d_skip, ssd_norm_w, q_norm_w, k_norm_w, w_out):
    bp, tp_, _ = x_prompt.shape
    bs, ts, _ = x_sample.shape
    depth = w_in.shape[0]
    assert depth == 1 and tp_ % SSD_CHUNK == 0 and ts <= SAMPLE_PAD and ts >= CONV_W - 1
    l = 0
    win = cache_k.shape[2]

    nw = norm_w[l].reshape(1, D_MODEL)
    rep = LANES // HEAD_DIM
    qnw = jnp.tile(q_norm_w[l], rep).reshape(1, LANES)
    knw = jnp.tile(k_norm_w[l], rep).reshape(1, LANES)
    cw = jnp.pad(conv_w[l], ((0, SUBLANES - CONV_W), (0, 0)))
    cb = conv_b[l].reshape(1, CONV_DIM)
    dtb = _lane_pad(dt_bias[l])
    alog = _lane_pad(a_log[l])
    dexp = jnp.repeat(d_skip[l], SSD_HEADDIM).reshape(1, SSD_INNER)
    snw = ssd_norm_w[l].reshape(1, SSD_INNER)

    pad = SAMPLE_PAD
    xs2 = jnp.pad(x_sample, ((0, 0), (0, pad - ts), (0, 0))).reshape(bs * pad, D_MODEL)
    tabs_s = _rope_tables(PAST_LEN + np.arange(bs * pad) % pad)
    w_t, proj_s = _wconv_in(xs2, nw, jnp.swapaxes(w_in[l], 0, 1))
    zs, xbc, dtr, q, kt, vt, gs, _ = _sections(proj_s, qnw, knw, tabs_s)
    cprev = jnp.pad(state_conv[l], ((0, 0), (SUBLANES - (CONV_W - 1), 0), (0, 0)))
    y_ssd, h_s = _ssd(xbc, dtr, zs, cprev, state_ssm[l].reshape(bs, SSD_INNER, SSD_STATE),
                      cw, cb, dtb, alog, dexp, snw, bs, pad, ts)
    knt = kt.reshape(KV_DIM, bs, pad).transpose(1, 0, 2)
    vnt = vt.reshape(KV_DIM, bs, pad).transpose(1, 0, 2)
    ckt = cache_k[l].transpose(0, 2, 3, 1).reshape(bs, KV_DIM, win)
    cvt = cache_v[l].transpose(0, 2, 3, 1).reshape(bs, KV_DIM, win)
    y_att, kot, vot = _attn_sample(q, knt, vnt, gs, ckt, cvt, ts)
    w_o, y_s = _wconv_out(xs2, y_ssd, y_att, w_out[l])
    y_s = y_s.reshape(bs, pad, D_MODEL)[:, :ts]
    k_s = _to_cache(kot, bs, win)
    v_s = _to_cache(vot, bs, win)
    c_s = xbc.reshape(bs, pad, CONV_DIM)[:, ts - (CONV_W - 1):ts][None]
    h_s = h_s.reshape(1, bs, SSD_HEADS, SSD_HEADDIM, SSD_STATE)

    tm = TM_IN
    xp2 = x_prompt.reshape(bp * tp_, D_MODEL)
    tabs = _rope_tables(np.arange(tp_))
    zs, xbc, dtr, q, kt, vt, gs, ctail = _in_proj(xp2, nw, w_t, qnw, knw, tabs, tm, tp_)
    y_ssd, h_p = _ssd(xbc, dtr, zs, jnp.zeros((bp, SUBLANES, CONV_DIM), F32),
                      jnp.zeros((bp, SSD_INNER, SSD_STATE), F32),
                      cw, cb, dtb, alog, dexp, snw, bp, SSD_CHUNK, SSD_CHUNK)
    y_att = _attn_prompt(q, kt, vt, gs, bp, tp_)
    y_p = _out_proj(xp2, y_ssd, y_att, w_o, TM_OUT).reshape(bp, tp_, D_MODEL)
    keep = min(W_MAX, tp_)
    k_p = _to_cache(kt, bp, tp_)[:, :, tp_ - keep:]
    v_p = _to_cache(vt, bp, tp_)[:, :, tp_ - keep:]
    c_p = ctail[:, SUBLANES - (CONV_W - 1):][None]
    h_p = h_p.reshape(1, bp, SSD_HEADS, SSD_HEADDIM, SSD_STATE)

    return (y_p, y_s, k_p, v_p, c_p, h_p, k_s, v_s, c_s, h_s)
```

```python
import functools

import jax
import jax.numpy as jnp
import numpy as np
from jax import lax
from jax.experimental import pallas as pl
from jax.experimental.pallas import tpu as pltpu

F32 = jnp.float32
BF16 = jnp.bfloat16

D_MODEL = 2048
SSD_HEADS = 16
SSD_HEADDIM = 64
SSD_INNER = SSD_HEADS * SSD_HEADDIM
SSD_GROUPS = 2
SSD_STATE = 128
CONV_W = 4
CONV_DIM = SSD_INNER + 2 * SSD_GROUPS * SSD_STATE
SSD_CHUNK = 128
ATT_HEADS = 16
ATT_KV_HEADS = 4
HEAD_DIM = 64
ATT_GQ = ATT_HEADS // ATT_KV_HEADS
ATT_INNER = ATT_HEADS * HEAD_DIM
KV_DIM = ATT_KV_HEADS * HEAD_DIM
ROT_DIM = HEAD_DIM // 4
ROPE_THETA = 500000.0
DILATED_BRANCHES = ((128, 1), (512, 4), (2048, 16))
W_MAX = 2048
PAST_LEN = 16384
EPS = 1e-6

LANES = 128
SUBLANES = 8
Q_BLOCK = 256
K_SUPER = 256
Q_PER_SUPER = K_SUPER // Q_BLOCK
ONES_ROWS = 16
ACC_ROWS = HEAD_DIM + ONES_ROWS
CONV_COLS = 256
TM_IN = 512
TM_OUT = 512
SSD_SEQS = 2
SAMPLE_PAD = 16
NEG = -1e30
VMEM_LIMIT = 56 * 1024 * 1024

Z0 = 0
X0 = Z0 + SSD_INNER
DT0 = X0 + CONV_DIM
Q0 = DT0 + SSD_HEADS
K0 = Q0 + ATT_INNER
V0 = K0 + KV_DIM
G0 = V0 + KV_DIM
W_ROWS = G0 + ATT_INNER
W_CHUNK = 512
WO_CHUNK = 256
HALF_INNER = SSD_INNER // SSD_GROUPS
LOG2E = 1.4426950408889634
Q_SCALE = HEAD_DIM ** -0.5 * LOG2E


def _dot(a, b):
    return jnp.dot(a, b, preferred_element_type=F32)


def _dot_nt(a, b):
    return lax.dot_general(a, b, (((1,), (1,)), ((), ())), preferred_element_type=F32)


def _split3(x):
    hi = x.astype(BF16)
    r1 = x - hi.astype(F32)
    mid = r1.astype(BF16)
    lo = (r1 - mid.astype(F32)).astype(BF16)
    return hi, mid, lo


def _dot_exact_rhs(x, m):
    hi, mid, lo = _split3(x)
    return _dot(hi, m) + _dot(mid, m) + _dot(lo, m)


def _dot_wide_rhs(x, m):
    hi = x.astype(BF16)
    lo = (x - hi.astype(F32)).astype(BF16)
    return _dot(hi, m) + _dot(lo, m)


def _dot_exact_lhs(m, x):
    hi, mid, lo = _split3(x)
    return _dot(m, hi) + _dot(m, mid) + _dot(m, lo)


def _silu(x):
    hx = 0.5 * x
    return hx * jnp.tanh(hx) + hx


def _multiplicity(d):
    w = jnp.zeros(d.shape, F32)
    for window, dil in DILATED_BRANCHES:
        hit = (d >= 0) & (d <= window) & (lax.rem(d, dil) == 0)
        w = w + jnp.where(hit, 1.0, 0.0)
    return w


def _norm_rope(y, nw, cos, sa, sb, scale):
    lane = lax.broadcasted_iota(jnp.int32, (1, LANES), 1)
    first = lane < HEAD_DIM
    y2 = y * y
    s_lo = jnp.sum(jnp.where(first, y2, 0.0), axis=-1, keepdims=True)
    s_hi = jnp.sum(jnp.where(first, 0.0, y2), axis=-1, keepdims=True)
    ms = jnp.where(first, s_lo, s_hi) * (1.0 / HEAD_DIM)
    yn = y * lax.rsqrt(ms + EPS) * nw
    half = ROT_DIM // 2
    rot = yn * cos + pltpu.roll(yn, half, 1) * sa + pltpu.roll(yn, LANES - half, 1) * sb
    return rot * scale


def _causal_conv_silu(load, cw_ref, cb_ref, store):
    for c0 in range(0, CONV_DIM, CONV_COLS):
        cs = slice(c0, c0 + CONV_COLS)
        xe = load(cs)
        x2 = pltpu.roll(xe, 2, 0)
        even = cw_ref[3:4, cs] * xe + cw_ref[1:2, cs] * x2
        odd = cw_ref[2:3, cs] * xe + cw_ref[0:1, cs] * x2
        store(cs, _silu((even + pltpu.roll(odd, 1, 0))[SUBLANES:, :] + cb_ref[:, cs]))


def _pre_norm(x_ref, nw_ref):
    x = x_ref[...]
    ms = jnp.mean(x * x, axis=-1, keepdims=True)
    return (x * lax.rsqrt(ms + EPS) * nw_ref[...]).astype(BF16)


def _inproj_kernel(x_ref, nw_ref, w_ref, *refs):
    hn = _pre_norm(x_ref, nw_ref)

    def proj(a, b):
        return _dot_nt(hn, w_ref[a:b, :])

    _emit_sections(proj, x_ref.shape[0], *refs)


def _emit_sections(proj, tm, qnw_ref, knw_ref, cos_ref, sa_ref, sb_ref,
                   zs_ref, xbc_ref, dt_ref, q_ref, kt_ref, vt_ref, gs_ref, ctail_ref):
    cos, sa, sb = cos_ref[...], sa_ref[...], sb_ref[...]
    q = proj(Q0, K0)
    for c in range(ATT_INNER // LANES):
        sl = slice(c * LANES, (c + 1) * LANES)
        q_ref[:, sl] = _norm_rope(q[:, sl], qnw_ref[...], cos, sa, sb, Q_SCALE).astype(BF16)
    k = proj(K0, V0)
    kn = [_norm_rope(k[:, c * LANES:(c + 1) * LANES], knw_ref[...], cos, sa, sb, 1.0)
          for c in range(KV_DIM // LANES)]
    kt_ref[0] = jnp.concatenate(kn, axis=1).T

    xbc = proj(X0, DT0)
    xbc_ref[...] = xbc
    ctail_ref[0] = xbc[tm - SUBLANES:tm, :]
    vt_ref[0] = proj(V0, G0).T
    zs_ref[...] = _silu(proj(Z0, X0))
    gs_ref[...] = _silu(proj(G0, W_ROWS))
    dt_ref[...] = proj(DT0, DT0 + LANES)


def _in_proj(x2d, nw, w_t, qnw, knw, tables, tm, seq):
    rows = x2d.shape[0]
    const = lambda i: (0, 0)
    row = lambda i: (i, 0)
    sec_in, out_specs, out_shape = _section_specs(rows, tm, seq, tables[0].shape[0] // tm)
    return pl.pallas_call(
        _inproj_kernel,
        grid=(rows // tm,),
        in_specs=[
            pl.BlockSpec((tm, D_MODEL), row),
            pl.BlockSpec((1, D_MODEL), const),
            pl.BlockSpec((W_ROWS, D_MODEL), const, pipeline_mode=pl.Buffered(1)),
        ] + sec_in,
        out_specs=out_specs,
        out_shape=out_shape,
        compiler_params=pltpu.CompilerParams(
            dimension_semantics=("arbitrary",), vmem_limit_bytes=VMEM_LIMIT),
        name="in_proj",
    )(x2d, nw, w_t, qnw, knw, *tables)


def _section_specs(rows, tm, seq, period):
    per_seq = seq // tm
    const = lambda i: (0, 0)
    row = lambda i: (i, 0)
    tab = lambda i: (i % period, 0)
    tr = lambda i: (i // per_seq, 0, i % per_seq)
    slab = lambda i: (i // per_seq, 0, 0)
    in_specs = [pl.BlockSpec((1, LANES), const)] * 2 + [pl.BlockSpec((tm, LANES), tab)] * 3
    row_outs = {0: (SSD_INNER, F32), 1: (CONV_DIM, F32), 2: (LANES, F32), 3: (ATT_INNER, BF16),
                6: (ATT_INNER, F32)}
    out_specs, out_shape = [], []
    for idx in range(7):
        if idx in row_outs:
            n, dt = row_outs[idx]
            out_specs.append(pl.BlockSpec((tm, n), row))
            out_shape.append(jax.ShapeDtypeStruct((rows, n), dt))
        else:
            out_specs.append(pl.BlockSpec((1, KV_DIM, tm), tr))
            out_shape.append(jax.ShapeDtypeStruct((rows // seq, KV_DIM, seq), F32))
    out_specs.append(pl.BlockSpec((1, SUBLANES, CONV_DIM), slab))
    out_shape.append(jax.ShapeDtypeStruct((rows // seq, SUBLANES, CONV_DIM), F32))
    return in_specs, out_specs, out_shape


def _wconv_in_kernel(x_ref, nw_ref, w_ref, wb_ref, p_ref, hn_scr):
    @pl.when(pl.program_id(0) == 0)
    def _():
        hn_scr[...] = _pre_norm(x_ref, nw_ref)

    row = pl.program_id(0) * W_CHUNK + lax.broadcasted_iota(jnp.int32, (W_CHUNK, 1), 0)
    wb = jnp.where(row < W_ROWS, w_ref[...], 0.0).astype(BF16)
    wb_ref[...] = wb
    p_ref[0] = _dot_nt(hn_scr[...], wb)


def _wconv_in(x2d, nw, w_f32):
    rows = x2d.shape[0]
    n_chunks = pl.cdiv(W_ROWS, W_CHUNK)
    const = lambda j: (0, 0)
    return pl.pallas_call(
        _wconv_in_kernel,
        grid=(n_chunks,),
        in_specs=[
            pl.BlockSpec((rows, D_MODEL), const),
            pl.BlockSpec((1, D_MODEL), const),
            pl.BlockSpec((W_CHUNK, D_MODEL), lambda j: (j, 0)),
        ],
        out_specs=[
            pl.BlockSpec((W_CHUNK, D_MODEL), lambda j: (j, 0)),
            pl.BlockSpec((1, rows, W_CHUNK), lambda j: (j, 0, 0)),
        ],
        out_shape=[
            jax.ShapeDtypeStruct((W_ROWS, D_MODEL), BF16),
            jax.ShapeDtypeStruct((n_chunks, rows, W_CHUNK), F32),
        ],
        scratch_shapes=[pltpu.VMEM((rows, D_MODEL), BF16)],
        compiler_params=pltpu.CompilerParams(
            dimension_semantics=("arbitrary",), vmem_limit_bytes=VMEM_LIMIT),
        name="wconv_in",
    )(x2d, nw, w_f32)


def _sections_kernel(p_ref, *refs):
    *refs, tail_scr = refs
    pfull = jnp.concatenate([p_ref[j] for j in range(p_ref.shape[0])], axis=1)
    tail_scr[...] = pfull[:, Q0:W_ROWS]

    def proj(a, b):
        if a >= Q0:
            return tail_scr[:, a - Q0:b - Q0]
        return pfull[:, a:b]

    _emit_sections(proj, p_ref.shape[1], *refs)


def _sections(p, qnw, knw, tables):
    rows = p.shape[1]
    sec_in, out_specs, out_shape = _section_specs(rows, rows, rows, 1)
    return pl.pallas_call(
        _sections_kernel,
        grid=(1,),
        in_specs=[pl.BlockSpec(p.shape, lambda i: (0, 0, 0))] + sec_in,
        out_specs=out_specs,
        out_shape=out_shape,
        scratch_shapes=[pltpu.VMEM((rows, W_ROWS - Q0), F32)],
        compiler_params=pltpu.CompilerParams(
            dimension_semantics=("arbitrary",), vmem_limit_bytes=VMEM_LIMIT),
        name="sections",
    )(p, qnw, knw, *tables)


def _ssd_chain(s, xbc_ref, dtr_ref, z_ref, cprev_ref, h0_ref, cw_ref, cb_ref, dtb_ref, alog_ref,
               dexp_ref, nw_ref, y_ref, hout_ref, xext, ht, xc_scr, lin, valid):
    L = SSD_CHUNK
    c = pl.program_id(1)
    last = pl.num_programs(1) - 1

    @pl.when(c == 0)
    def _():
        xext[s, 0:SUBLANES, :] = cprev_ref[s]
        for g in range(SSD_GROUPS):
            ht[s, g] = h0_ref[s, g * HALF_INNER:(g + 1) * HALF_INNER, :].T

    def pad_rows(v):
        if lin == L:
            return v
        return jnp.concatenate([v, jnp.zeros((L - lin, v.shape[1]), v.dtype)], axis=0)

    lane = lax.broadcasted_iota(jnp.int32, (1, LANES), 1)
    rowid = lax.broadcasted_iota(jnp.int32, (L, 1), 0)
    dt = jax.nn.softplus(pad_rows(dtr_ref[s]) + dtb_ref[...])
    dt = jnp.where((lane < SSD_HEADS) & (rowid < valid), dt, 0.0)
    dta = dt * (-jnp.exp(alog_ref[...]) * LOG2E)

    r2 = lax.broadcasted_iota(jnp.int32, (L, L), 0)
    c2 = lax.broadcasted_iota(jnp.int32, (L, L), 1)
    tri = r2 >= c2
    tri_b = jnp.where(tri, 1.0, 0.0).astype(BF16)
    cum = _dot_exact_lhs(tri_b, dta)
    cum_last = cum[L - 1:L, :]
    yield

    er = lax.broadcasted_iota(jnp.int32, (LANES, SSD_INNER), 0)
    ec = lax.broadcasted_iota(jnp.int32, (LANES, SSD_INNER), 1)
    expand = jnp.where(ec // SSD_HEADDIM == er, 1.0, 0.0).astype(BF16)
    ecum = _dot_wide_rhs(jnp.exp2(cum), expand)
    wexp = _dot_wide_rhs(jnp.exp2(cum_last - cum) * dt, expand)
    cd = _dot_exact_rhs(jnp.broadcast_to(jnp.exp2(cum_last), (SUBLANES, LANES)), expand)[0:1, :]
    cum_t = cum.T
    dt_t = dt.T
    yield

    if lin < L:
        xext[s, SUBLANES + lin:SUBLANES + L, :] = jnp.zeros((L - lin, CONV_DIM), F32)
    xext[s, SUBLANES:SUBLANES + lin, :] = xbc_ref[s]

    def load(cs):
        return xext[s, 0:SUBLANES + L, cs]

    def store(cs, v):
        xc_scr[s, :, cs] = v

    _causal_conv_silu(load, cw_ref, cb_ref, store)
    xext[s, 0:SUBLANES, :] = xext[s, L:L + SUBLANES, :]
    xc = xc_scr.at[s]
    xs = xc[:, :SSD_INNER]
    bm = xc[:, SSD_INNER:SSD_INNER + SSD_GROUPS * SSD_STATE]
    cm = xc[:, SSD_INNER + SSD_GROUPS * SSD_STATE:]
    bm_b = bm.astype(BF16)
    cm_b = cm.astype(BF16)
    xs_b = xs.astype(BF16)

    y_off = jnp.concatenate(
        [_dot(cm_b[:, g * SSD_STATE:(g + 1) * SSD_STATE], ht[s, g].astype(BF16))
         for g in range(SSD_GROUPS)], axis=1) * ecum
    cbs = [_dot_nt(cm_b[:, g * SSD_STATE:(g + 1) * SSD_STATE], bm_b[:, g * SSD_STATE:(g + 1) * SSD_STATE])
           for g in range(SSD_GROUPS)]
    yield

    lane_half = lane // HEAD_DIM
    heads_per_group = SSD_HEADS // SSD_GROUPS
    y_parts = []
    for g in range(SSD_GROUPS):
        for pr in range(heads_per_group // 2):
            col0 = g * HALF_INNER + pr * LANES
            xp = xs_b[:, col0:col0 + LANES]
            yp = jnp.zeros((L, LANES), F32)
            for e in range(2):
                hh = g * heads_per_group + pr * 2 + e
                decay = jnp.exp2(cum[:, hh:hh + 1] - cum_t[hh:hh + 1, :])
                sc = jnp.where(tri, cbs[g] * decay, 0.0) * dt_t[hh:hh + 1, :]
                xm = jnp.where(lane_half == e, xp, jnp.zeros_like(xp))
                yp = yp + _dot(sc.astype(BF16), xm)
            y_parts.append(yp)
            if pr % 2 == 1:
                yield
    y_diag = jnp.concatenate(y_parts, axis=1)

    xw = (xs * wexp).astype(BF16)
    for g in range(SSD_GROUPS):
        sl = slice(g * HALF_INNER, (g + 1) * HALF_INNER)
        bm_t = bm[:, g * SSD_STATE:(g + 1) * SSD_STATE].T.astype(BF16)
        ht[s, g] = ht[s, g] * cd[:, sl] + _dot(bm_t, xw[:, sl])
    yield

    y = (y_diag + y_off + dexp_ref[...] * xs) * pad_rows(z_ref[s])
    yn = []
    for g in range(SSD_GROUPS):
        yg = y[:, g * HALF_INNER:(g + 1) * HALF_INNER]
        ms = jnp.mean(yg * yg, axis=-1, keepdims=True)
        yn.append(yg * lax.rsqrt(ms + EPS))
    yn = jnp.concatenate(yn, axis=1) * nw_ref[...]
    y_ref[s] = yn[:lin].astype(BF16)

    @pl.when(c == last)
    def _():
        for g in range(SSD_GROUPS):
            hout_ref[s, g * HALF_INNER:(g + 1) * HALF_INNER, :] = ht[s, g].T


def _ssd_kernel(*refs, lin, valid):
    chains = [_ssd_chain(s, *refs, lin, valid) for s in range(SSD_SEQS)]
    while chains:
        alive = []
        for ch in chains:
            try:
                next(ch)
                alive.append(ch)
            except StopIteration:
                pass
        chains = alive


def _ssd(xbc, dtr, z, cprev, h0, cw, cb, dtb, alog, dexp, nw, batch, lin, valid):
    rows = xbc.shape[0]
    seq = rows // batch
    nc = seq // lin
    g = SSD_SEQS
    assert batch % g == 0
    blk = lambda b, c: (b, c, 0)
    per_b = lambda b, c: (b, 0, 0)
    const = lambda b, c: (0, 0)
    y, hout = pl.pallas_call(
        functools.partial(_ssd_kernel, lin=lin, valid=valid),
        grid=(batch // g, nc),
        in_specs=[
            pl.BlockSpec((g, lin, CONV_DIM), blk),
            pl.BlockSpec((g, lin, LANES), blk),
            pl.BlockSpec((g, lin, SSD_INNER), blk),
            pl.BlockSpec((g, SUBLANES, CONV_DIM), per_b),
            pl.BlockSpec((g, SSD_INNER, SSD_STATE), per_b),
            pl.BlockSpec((SUBLANES, CONV_DIM), const),
            pl.BlockSpec((1, CONV_DIM), const),
            pl.BlockSpec((1, LANES), const),
            pl.BlockSpec((1, LANES), const),
            pl.BlockSpec((1, SSD_INNER), const),
            pl.BlockSpec((1, SSD_INNER), const),
        ],
        out_specs=[
            pl.BlockSpec((g, lin, SSD_INNER), blk),
            pl.BlockSpec((g, SSD_INNER, SSD_STATE), per_b),
        ],
        out_shape=[
            jax.ShapeDtypeStruct((batch, seq, SSD_INNER), BF16),
            jax.ShapeDtypeStruct((batch, SSD_INNER, SSD_STATE), F32),
        ],
        scratch_shapes=[
            pltpu.VMEM((g, SUBLANES + SSD_CHUNK + SUBLANES, CONV_DIM), F32),
            pltpu.VMEM((g, SSD_GROUPS, SSD_STATE, HALF_INNER), F32),
            pltpu.VMEM((g, SSD_CHUNK, CONV_DIM), F32),
        ],
        compiler_params=pltpu.CompilerParams(
            dimension_semantics=("arbitrary", "arbitrary"), vmem_limit_bytes=VMEM_LIMIT),
        name="ssd",
    )(xbc.reshape(batch, seq, CONV_DIM), dtr.reshape(batch, seq, LANES),
      z.reshape(batch, seq, SSD_INNER), cprev, h0, cw, cb, dtb, alog, dexp, nw)
    return y.reshape(rows, SSD_INNER), hout


def _attn_prompt_kernel(q_ref, kt_ref, vt_ref, g_ref, o_ref,
                        bias_scr, kh_scr, va_scr, qt_scr, m_scr, acc_scr, ot_scr, s_scr, *, n_tab, n_sb_max):
    b = pl.program_id(0)
    i = pl.program_id(1)
    seq = kt_ref.shape[2]
    pair_w = 2 * Q_BLOCK
    pairs_per_kv = ATT_GQ // 2

    @pl.when((b == 0) & (i == 0))
    def _():
        r = lax.broadcasted_iota(jnp.int32, (K_SUPER, Q_BLOCK), 0)
        c = lax.broadcasted_iota(jnp.int32, (K_SUPER, Q_BLOCK), 1)
        for tb in range(n_tab):
            w = _multiplicity(tb * Q_BLOCK + c - r)
            bias_scr[tb] = jnp.where(w > 0.0, jnp.log2(jnp.maximum(w, 1.0)), NEG)
        va_scr[:, HEAD_DIM:, :] = jnp.ones((ATT_KV_HEADS, ONES_ROWS, seq), BF16)

    @pl.when(i == 0)
    def _():
        for kvh in range(ATT_KV_HEADS):
            hs = slice(kvh * HEAD_DIM, (kvh + 1) * HEAD_DIM)
            va_scr[kvh, 0:HEAD_DIM, :] = vt_ref[0, hs, :].astype(BF16)
        for cidx in range(seq // K_SUPER):
            rows = slice(cidx * K_SUPER, (cidx + 1) * K_SUPER)
            kc = kt_ref[0, :, rows].T
            for kvh in range(ATT_KV_HEADS):
                hs = slice(kvh * HEAD_DIM, (kvh + 1) * HEAD_DIM)
                kh_scr[kvh, rows, :] = kc[:, hs].astype(BF16)

    qt = q_ref[...].astype(F32).T
    for kvh in range(ATT_KV_HEADS):
        qt_scr[kvh] = jnp.concatenate(
            [qt[(kvh * ATT_GQ + g) * HEAD_DIM:(kvh * ATT_GQ + g + 1) * HEAD_DIM, :]
             for g in range(ATT_GQ)], axis=1).astype(BF16)
    m_scr[...] = jnp.full(m_scr.shape, NEG, F32)
    acc_scr[...] = jnp.zeros(acc_scr.shape, F32)
    par = i % Q_PER_SUPER
    j_last = i // Q_PER_SUPER
    n_sb = jnp.minimum(j_last + 1, n_sb_max)

    def body(dl, carry):
        start = pl.multiple_of((j_last - dl) * K_SUPER, K_SUPER)
        bias = bias_scr[par + Q_PER_SUPER * dl]
        bias2 = jnp.concatenate([bias, bias], axis=1)
        for kvh in range(ATT_KV_HEADS):
            s_scr[kvh] = _dot(kh_scr[kvh, pl.ds(start, K_SUPER), :], qt_scr[kvh])
        for kvh in range(ATT_KV_HEADS):
            vtb = va_scr[kvh, :, pl.ds(start, K_SUPER)]
            for pr in range(pairs_per_kv):
                u = kvh * pairs_per_kv + pr
                s = s_scr[kvh, :, pr * pair_w:(pr + 1) * pair_w] + bias2
                m_old = m_scr[u, 0:1, :]
                m_new = jnp.maximum(m_old, jnp.max(s, axis=0, keepdims=True))
                p = jnp.exp2(s - m_new).astype(BF16)
                acc_scr[u] = jnp.exp2(m_old - m_new) * acc_scr[u] + _dot(vtb, p)
                m_scr[u, 0:1, :] = m_new
        return carry

    lax.fori_loop(0, n_sb, body, 0)

    for u in range(ATT_HEADS // 2):
        acc = acc_scr[u]
        o = acc[0:HEAD_DIM] / acc[HEAD_DIM:HEAD_DIM + 1]
        for e in range(2):
            h = 2 * u + e
            ot_scr[h * HEAD_DIM:(h + 1) * HEAD_DIM, :] = o[:, e * Q_BLOCK:(e + 1) * Q_BLOCK]
    o_ref[...] = (ot_scr[...].T * g_ref[...]).astype(BF16)


def _attn_prompt(q, kt, vt, gate, batch, seq):
    rows = q.shape[0]
    nqb = seq // Q_BLOCK
    assert seq % K_SUPER == 0
    n_sb_max = min(seq // K_SUPER, W_MAX // K_SUPER + 1)
    n_tab = Q_PER_SUPER * n_sb_max
    qblk = lambda b, i: (b * nqb + i, 0)
    per_b = lambda b, i: (b, 0, 0)
    return pl.pallas_call(
        functools.partial(_attn_prompt_kernel, n_tab=n_tab, n_sb_max=n_sb_max),
        grid=(batch, nqb),
        in_specs=[
            pl.BlockSpec((Q_BLOCK, ATT_INNER), qblk),
            pl.BlockSpec((1, KV_DIM, seq), per_b),
            pl.BlockSpec((1, KV_DIM, seq), per_b),
            pl.BlockSpec((Q_BLOCK, ATT_INNER), qblk),
        ],
        out_specs=pl.BlockSpec((Q_BLOCK, ATT_INNER), qblk),
        out_shape=jax.ShapeDtypeStruct((rows, ATT_INNER), BF16),
        scratch_shapes=[
            pltpu.VMEM((n_tab, K_SUPER, Q_BLOCK), F32),
            pltpu.VMEM((ATT_KV_HEADS, seq, HEAD_DIM), BF16),
            pltpu.VMEM((ATT_KV_HEADS, ACC_ROWS, seq), BF16),
            pltpu.VMEM((ATT_KV_HEADS, HEAD_DIM, ATT_GQ * Q_BLOCK), BF16),
            pltpu.VMEM((ATT_HEADS // 2, SUBLANES, 2 * Q_BLOCK), F32),
            pltpu.VMEM((ATT_HEADS // 2, ACC_ROWS, 2 * Q_BLOCK), F32),
            pltpu.VMEM((ATT_INNER, Q_BLOCK), F32),
            pltpu.VMEM((ATT_KV_HEADS, K_SUPER, ATT_GQ * Q_BLOCK), F32),
        ],
        compiler_params=pltpu.CompilerParams(
            dimension_semantics=("arbitrary", "arbitrary"), vmem_limit_bytes=VMEM_LIMIT),
        name="attn_prompt",
    )(q, kt, vt, gate)


def _attn_sample_kernel(q_ref, knt_ref, vnt_ref, g_ref, ckt_ref, cvt_ref, o_ref, kot_ref, vot_ref, *, ts):
    win = ckt_ref.shape[2]
    tp = SAMPLE_PAD
    rows = ATT_GQ * tp
    t_c = lax.broadcasted_iota(jnp.int32, (rows, win), 0) % tp
    j_c = lax.broadcasted_iota(jnp.int32, (rows, win), 1)
    w_c = _multiplicity(win + t_c - j_c)
    t_n = lax.broadcasted_iota(jnp.int32, (rows, tp), 0) % tp
    j_n = lax.broadcasted_iota(jnp.int32, (rows, tp), 1)
    w_n = jnp.where(j_n < ts, _multiplicity(t_n - j_n), 0.0)
    for kvh in range(ATT_KV_HEADS):
        c0 = kvh * ATT_GQ * HEAD_DIM
        qh = jnp.concatenate(
            [q_ref[:, c0 + g * HEAD_DIM:c0 + (g + 1) * HEAD_DIM] for g in range(ATT_GQ)], axis=0)
        hs = slice(kvh * HEAD_DIM, (kvh + 1) * HEAD_DIM)
        s_c = jnp.where(w_c > 0.0, _dot(qh, ckt_ref[0, hs, :].astype(BF16)), NEG)
        s_n = jnp.where(w_n > 0.0, _dot(qh, knt_ref[0, hs, :].astype(BF16)), NEG)
        m = jnp.maximum(jnp.max(s_c, axis=-1, keepdims=True), jnp.max(s_n, axis=-1, keepdims=True))
        p_c = jnp.exp2(s_c - m) * w_c
        p_n = jnp.exp2(s_n - m) * w_n
        den = jnp.sum(p_c, axis=-1, keepdims=True) + jnp.sum(p_n, axis=-1, keepdims=True)
        num = (_dot_nt(p_c.astype(BF16), cvt_ref[0, hs, :].astype(BF16))
               + _dot_nt(p_n.astype(BF16), vnt_ref[0, hs, :].astype(BF16)))
        o = num / den
        o = jnp.concatenate([o[g * tp:(g + 1) * tp] for g in range(ATT_GQ)], axis=1)
        csl = slice(c0, c0 + ATT_GQ * HEAD_DIM)
        o_ref[:, csl] = (o * g_ref[:, csl]).astype(BF16)
    kot_ref[0, :, 0:win - ts] = ckt_ref[0, :, ts:win]
    kot_ref[0, :, win - ts:win] = knt_ref[0, :, 0:ts]
    vot_ref[0, :, 0:win - ts] = cvt_ref[0, :, ts:win]
    vot_ref[0, :, win - ts:win] = vnt_ref[0, :, 0:ts]


def _attn_sample(q, knt, vnt, gate, ckt, cvt, ts):
    batch, _, win = ckt.shape
    tp = SAMPLE_PAD
    blk = lambda b: (b, 0)
    per_b = lambda b: (b, 0, 0)
    return pl.pallas_call(
        functools.partial(_attn_sample_kernel, ts=ts),
        grid=(batch,),
        in_specs=[
            pl.BlockSpec((tp, ATT_INNER), blk),
            pl.BlockSpec((1, KV_DIM, tp), per_b),
            pl.BlockSpec((1, KV_DIM, tp), per_b),
            pl.BlockSpec((tp, ATT_INNER), blk),
            pl.BlockSpec((1, KV_DIM, win), per_b),
            pl.BlockSpec((1, KV_DIM, win), per_b),
        ],
        out_specs=[
            pl.BlockSpec((tp, ATT_INNER), blk),
            pl.BlockSpec((1, KV_DIM, win), per_b),
            pl.BlockSpec((1, KV_DIM, win), per_b),
        ],
        out_shape=[
            jax.ShapeDtypeStruct((batch * tp, ATT_INNER), BF16),
            jax.ShapeDtypeStruct((batch, KV_DIM, win), F32),
            jax.ShapeDtypeStruct((batch, KV_DIM, win), F32),
        ],
        compiler_params=pltpu.CompilerParams(
            dimension_semantics=("arbitrary",), vmem_limit_bytes=VMEM_LIMIT),
        name="attn_sample",
    )(q, knt, vnt, gate, ckt, cvt)


def _outproj_kernel(x_ref, ys_ref, ya_ref, w_ref, o_ref):
    o_ref[...] = (x_ref[...] + _dot(ys_ref[...], w_ref[0:SSD_INNER, :])
                  + _dot(ya_ref[...], w_ref[SSD_INNER:, :]))


def _out_proj(x2d, y_ssd, y_att, w_o, tm):
    rows = x2d.shape[0]
    row = lambda i: (i, 0)
    const = lambda i: (0, 0)
    return pl.pallas_call(
        _outproj_kernel,
        grid=(rows // tm,),
        in_specs=[
            pl.BlockSpec((tm, D_MODEL), row),
            pl.BlockSpec((tm, SSD_INNER), row),
            pl.BlockSpec((tm, ATT_INNER), row),
            pl.BlockSpec((SSD_INNER + ATT_INNER, D_MODEL), const, pipeline_mode=pl.Buffered(1)),
        ],
        out_specs=pl.BlockSpec((tm, D_MODEL), row),
        out_shape=jax.ShapeDtypeStruct((rows, D_MODEL), F32),
        compiler_params=pltpu.CompilerParams(
            dimension_semantics=("arbitrary",), vmem_limit_bytes=VMEM_LIMIT),
        name="out_proj",
    )(x2d, y_ssd, y_att, w_o)


def _wconv_out_kernel(x_ref, ys_ref, ya_ref, w_ref, wb_ref, o_ref):
    j = pl.program_id(0)
    wb = w_ref[...].astype(BF16)
    wb_ref[...] = wb
    y = jnp.where(j < SSD_INNER // WO_CHUNK, ys_ref[...], ya_ref[...])
    part = _dot(y, wb)

    @pl.when(j == 0)
    def _():
        o_ref[...] = x_ref[...] + part

    @pl.when(j > 0)
    def _():
        o_ref[...] += part


def _wconv_out(x2d, y_ssd, y_att, w_f32):
    rows = x2d.shape[0]
    n_ssd = SSD_INNER // WO_CHUNK
    n_chunks = (SSD_INNER + ATT_INNER) // WO_CHUNK
    const = lambda j: (0, 0)
    return pl.pallas_call(
        _wconv_out_kernel,
        grid=(n_chunks,),
        in_specs=[
            pl.BlockSpec((rows, D_MODEL), const),
            pl.BlockSpec((rows, WO_CHUNK), lambda j: (0, jnp.minimum(j, n_ssd - 1))),
            pl.BlockSpec((rows, WO_CHUNK), lambda j: (0, jnp.maximum(j - n_ssd, 0))),
            pl.BlockSpec((WO_CHUNK, D_MODEL), lambda j: (j, 0)),
        ],
        out_specs=[
            pl.BlockSpec((WO_CHUNK, D_MODEL), lambda j: (j, 0)),
            pl.BlockSpec((rows, D_MODEL), const),
        ],
        out_shape=[
            jax.ShapeDtypeStruct((SSD_INNER + ATT_INNER, D_MODEL), BF16),
            jax.ShapeDtypeStruct((rows, D_MODEL), F32),
        ],
        compiler_params=pltpu.CompilerParams(
            dimension_semantics=("arbitrary",), vmem_limit_bytes=VMEM_LIMIT),
        name="wconv_out",
    )(x2d, y_ssd, y_att, w_f32)


def _rope_tables(pos):
    n = pos.shape[0]
    half = ROT_DIM // 2
    inv = ROPE_THETA ** (-np.arange(0, ROT_DIM, 2, dtype=np.float64) / ROT_DIM)
    ang = pos.astype(np.float64)[:, None] * inv[None, :]
    cos, sin = np.cos(ang), np.sin(ang)
    rest = HEAD_DIM - ROT_DIM
    zh = np.zeros((n, half))
    cos_h = np.concatenate([cos, cos, np.ones((n, rest))], axis=1)
    sa_h = np.concatenate([zh, sin, np.zeros((n, rest))], axis=1)
    sb_h = np.concatenate([-sin, zh, np.zeros((n, rest))], axis=1)
    rep = LANES // HEAD_DIM
    return tuple(jnp.asarray(np.tile(t, (1, rep)), dtype=F32) for t in (cos_h, sa_h, sb_h))


def _lane_pad(v, n=LANES):
    return jnp.pad(v, (0, n - v.shape[0])).reshape(1, n)


def _to_cache(xt, batch, seq):
    return xt.reshape(1, batch, ATT_KV_HEADS, HEAD_DIM, seq).transpose(0, 1, 4, 2, 3)


def kernel(x_prompt, x_sample, cache_k, cache_v, state_conv, state_ssm, norm_w, w_in, conv_w,
           conv_b, dt_bias, a_log, d_skip, ssd_norm_w, q_norm_w, k_norm_w, w_out):
    bp, tp_, _ = x_prompt.shape
    bs, ts, _ = x_sample.shape
    depth = w_in.shape[0]
    assert depth == 1 and tp_ % SSD_CHUNK == 0 and ts <= SAMPLE_PAD and ts >= CONV_W - 1
    l = 0
    win = cache_k.shape[2]

    nw = norm_w[l].reshape(1, D_MODEL)
    rep = LANES // HEAD_DIM
    qnw = jnp.tile(q_norm_w[l], rep).reshape(1, LANES)
    knw = jnp.tile(k_norm_w[l], rep).reshape(1, LANES)
    cw = jnp.pad(conv_w[l], ((0, SUBLANES - CONV_W), (0, 0)))
    cb = conv_b[l].reshape(1, CONV_DIM)
    dtb = _lane_pad(dt_bias[l])
    alog = _lane_pad(a_log[l])
    dexp = jnp.repeat(d_skip[l], SSD_HEADDIM).reshape(1, SSD_INNER)
    snw = ssd_norm_w[l].reshape(1, SSD_INNER)

    pad = SAMPLE_PAD
    xs2 = jnp.pad(x_sample, ((0, 0), (0, pad - ts), (0, 0))).reshape(bs * pad, D_MODEL)
    tabs_s = _rope_tables(PAST_LEN + np.arange(bs * pad) % pad)
    w_t, proj_s = _wconv_in(xs2, nw, jnp.swapaxes(w_in[l], 0, 1))
    zs, xbc, dtr, q, kt, vt, gs, _ = _sections(proj_s, qnw, knw, tabs_s)
    cprev = jnp.pad(state_conv[l], ((0, 0), (SUBLANES - (CONV_W - 1), 0), (0, 0)))
    y_ssd, h_s = _ssd(xbc, dtr, zs, cprev, state_ssm[l].reshape(bs, SSD_INNER, SSD_STATE),
                      cw, cb, dtb, alog, dexp, snw, bs, pad, ts)
    knt = kt.reshape(KV_DIM, bs, pad).transpose(1, 0, 2)
    vnt = vt.reshape(KV_DIM, bs, pad).transpose(1, 0, 2)
    ckt = cache_k[l].transpose(0, 2, 3, 1).reshape(bs, KV_DIM, win)
    cvt = cache_v[l].transpose(0, 2, 3, 1).reshape(bs, KV_DIM, win)
    y_att, kot, vot = _attn_sample(q, knt, vnt, gs, ckt, cvt, ts)
    w_o, y_s = _wconv_out(xs2, y_ssd, y_att, w_out[l])
    y_s = y_s.reshape(bs, pad, D_MODEL)[:, :ts]
    k_s = _to_cache(kot, bs, win)
    v_s = _to_cache(vot, bs, win)
    c_s = xbc.reshape(bs, pad, CONV_DIM)[:, ts - (CONV_W - 1):ts][None]
    h_s = h_s.reshape(1, bs, SSD_HEADS, SSD_HEADDIM, SSD_STATE)

    tm = TM_IN
    xp2 = x_prompt.reshape(bp * tp_, D_MODEL)
    tabs = _rope_tables(np.arange(tp_))
    zs, xbc, dtr, q, kt, vt, gs, ctail = _in_proj(xp2, nw, w_t, qnw, knw, tabs, tm, tp_)
    y_ssd, h_p = _ssd(xbc, dtr, zs, jnp.zeros((bp, SUBLANES, CONV_DIM), F32),
                      jnp.zeros((bp, SSD_INNER, SSD_STATE), F32),
                      cw, cb, dtb, alog, dexp, snw, bp, SSD_CHUNK, SSD_CHUNK)
    y_att = _attn_prompt(q, kt, vt, gs, bp, tp_)
    y_p = _out_proj(xp2, y_ssd, y_att, w_o, TM_OUT).reshape(bp, tp_, D_MODEL)
    keep = min(W_MAX, tp_)
    k_p = _to_cache(kt, bp, tp_)[:, :, tp_ - keep:]
    v_p = _to_cache(vt, bp, tp_)[:, :, tp_ - keep:]
    c_p = ctail[:, SUBLANES - (CONV_W - 1):][None]
    h_p = h_p.reshape(1, bp, SSD_HEADS, SSD_HEADDIM, SSD_STATE)

    return (y_p, y_s, k_p, v_p, c_p, h_p, k_s, v_s, c_s, h_s)
```

```python
import functools

import jax
import jax.numpy as jnp
import numpy as np
from jax import lax
from jax.experimental import pallas as pl
from jax.experimental.pallas import tpu as pltpu

F32 = jnp.float32
BF16 = jnp.bfloat16

D_MODEL = 2048
SSD_HEADS = 16
SSD_HEADDIM = 64
SSD_INNER = SSD_HEADS * SSD_HEADDIM
SSD_GROUPS = 2
SSD_STATE = 128
CONV_W = 4
CONV_DIM = SSD_INNER + 2 * SSD_GROUPS * SSD_STATE
SSD_CHUNK = 128
ATT_HEADS = 16
ATT_KV_HEADS = 4
HEAD_DIM = 64
ATT_GQ = ATT_HEADS // ATT_KV_HEADS
ATT_INNER = ATT_HEADS * HEAD_DIM
KV_DIM = ATT_KV_HEADS * HEAD_DIM
ROT_DIM = HEAD_DIM // 4
ROPE_THETA = 500000.0
DILATED_BRANCHES = ((128, 1), (512, 4), (2048, 16))
W_MAX = 2048
PAST_LEN = 16384
EPS = 1e-6

LANES = 128
SUBLANES = 8
Q_BLOCK = 256
K_SUPER = 256
Q_PER_SUPER = K_SUPER // Q_BLOCK
ONES_ROWS = 16
ACC_ROWS = HEAD_DIM + ONES_ROWS
CONV_COLS = 256
TM_IN = 512
TM_OUT = 512
SSD_SEQS = 2
SAMPLE_PAD = 16
NEG = -1e30
VMEM_LIMIT = 56 * 1024 * 1024

Z0 = 0
X0 = Z0 + SSD_INNER
DT0 = X0 + CONV_DIM
Q0 = DT0 + SSD_HEADS
K0 = Q0 + ATT_INNER
V0 = K0 + KV_DIM
G0 = V0 + KV_DIM
W_ROWS = G0 + ATT_INNER
W_CHUNK = 1024
WO_CHUNK = 512
HALF_INNER = SSD_INNER // SSD_GROUPS
LOG2E = 1.4426950408889634
Q_SCALE = HEAD_DIM ** -0.5 * LOG2E


def _dot(a, b):
    return jnp.dot(a, b, preferred_element_type=F32)


def _dot_nt(a, b):
    return lax.dot_general(a, b, (((1,), (1,)), ((), ())), preferred_element_type=F32)


def _split3(x):
    hi = x.astype(BF16)
    r1 = x - hi.astype(F32)
    mid = r1.astype(BF16)
    lo = (r1 - mid.astype(F32)).astype(BF16)
    return hi, mid, lo


def _dot_exact_rhs(x, m):
    hi, mid, lo = _split3(x)
    return _dot(hi, m) + _dot(mid, m) + _dot(lo, m)


def _dot_wide_rhs(x, m):
    hi = x.astype(BF16)
    lo = (x - hi.astype(F32)).astype(BF16)
    return _dot(hi, m) + _dot(lo, m)


def _dot_exact_lhs(m, x):
    hi, mid, lo = _split3(x)
    return _dot(m, hi) + _dot(m, mid) + _dot(m, lo)


def _silu(x):
    hx = 0.5 * x
    return hx * jnp.tanh(hx) + hx


def _multiplicity(d):
    w = jnp.zeros(d.shape, F32)
    for window, dil in DILATED_BRANCHES:
        hit = (d >= 0) & (d <= window) & (lax.rem(d, dil) == 0)
        w = w + jnp.where(hit, 1.0, 0.0)
    return w


def _norm_rope(y, nw, cos, sa, sb, scale):
    lane = lax.broadcasted_iota(jnp.int32, (1, LANES), 1)
    first = lane < HEAD_DIM
    y2 = y * y
    s_lo = jnp.sum(jnp.where(first, y2, 0.0), axis=-1, keepdims=True)
    s_hi = jnp.sum(jnp.where(first, 0.0, y2), axis=-1, keepdims=True)
    ms = jnp.where(first, s_lo, s_hi) * (1.0 / HEAD_DIM)
    yn = y * lax.rsqrt(ms + EPS) * nw
    half = ROT_DIM // 2
    rot = yn * cos + pltpu.roll(yn, half, 1) * sa + pltpu.roll(yn, LANES - half, 1) * sb
    return rot * scale


def _causal_conv_silu(load, cw_ref, cb_ref, store):
    for c0 in range(0, CONV_DIM, CONV_COLS):
        cs = slice(c0, c0 + CONV_COLS)
        xe = load(cs)
        x2 = pltpu.roll(xe, 2, 0)
        even = cw_ref[3:4, cs] * xe + cw_ref[1:2, cs] * x2
        odd = cw_ref[2:3, cs] * xe + cw_ref[0:1, cs] * x2
        store(cs, _silu((even + pltpu.roll(odd, 1, 0))[SUBLANES:, :] + cb_ref[:, cs]))


def _pre_norm(x_ref, nw_ref):
    x = x_ref[...]
    ms = jnp.mean(x * x, axis=-1, keepdims=True)
    return (x * lax.rsqrt(ms + EPS) * nw_ref[...]).astype(BF16)


def _inproj_kernel(x_ref, nw_ref, w_ref, *refs):
    hn = _pre_norm(x_ref, nw_ref)

    def proj(a, b):
        return _dot_nt(hn, w_ref[a:b, :])

    _emit_sections(proj, x_ref.shape[0], *refs)


def _emit_sections(proj, tm, qnw_ref, knw_ref, cos_ref, sa_ref, sb_ref,
                   zs_ref, xbc_ref, dt_ref, q_ref, kt_ref, vt_ref, gs_ref, ctail_ref):
    cos, sa, sb = cos_ref[...], sa_ref[...], sb_ref[...]
    q = proj(Q0, K0)
    for c in range(ATT_INNER // LANES):
        sl = slice(c * LANES, (c + 1) * LANES)
        q_ref[:, sl] = _norm_rope(q[:, sl], qnw_ref[...], cos, sa, sb, Q_SCALE).astype(BF16)
    k = proj(K0, V0)
    kn = [_norm_rope(k[:, c * LANES:(c + 1) * LANES], knw_ref[...], cos, sa, sb, 1.0)
          for c in range(KV_DIM // LANES)]
    kt_ref[0] = jnp.concatenate(kn, axis=1).T

    xbc = proj(X0, DT0)
    xbc_ref[...] = xbc
    ctail_ref[0] = xbc[tm - SUBLANES:tm, :]
    vt_ref[0] = proj(V0, G0).T
    zs_ref[...] = _silu(proj(Z0, X0))
    gs_ref[...] = _silu(proj(G0, W_ROWS))
    dt_ref[...] = proj(DT0, DT0 + LANES)


def _in_proj(x2d, nw, w_t, qnw, knw, tables, tm, seq):
    rows = x2d.shape[0]
    const = lambda i: (0, 0)
    row = lambda i: (i, 0)
    sec_in, out_specs, out_shape = _section_specs(rows, tm, seq, tables[0].shape[0] // tm)
    return pl.pallas_call(
        _inproj_kernel,
        grid=(rows // tm,),
        in_specs=[
            pl.BlockSpec((tm, D_MODEL), row),
            pl.BlockSpec((1, D_MODEL), const),
            pl.BlockSpec((W_ROWS, D_MODEL), const, pipeline_mode=pl.Buffered(1)),
        ] + sec_in,
        out_specs=out_specs,
        out_shape=out_shape,
        compiler_params=pltpu.CompilerParams(
            dimension_semantics=("arbitrary",), vmem_limit_bytes=VMEM_LIMIT),
        name="in_proj",
    )(x2d, nw, w_t, qnw, knw, *tables)


def _section_specs(rows, tm, seq, period):
    per_seq = seq // tm
    const = lambda i: (0, 0)
    row = lambda i: (i, 0)
    tab = lambda i: (i % period, 0)
    tr = lambda i: (i // per_seq, 0, i % per_seq)
    slab = lambda i: (i // per_seq, 0, 0)
    in_specs = [pl.BlockSpec((1, LANES), const)] * 2 + [pl.BlockSpec((tm, LANES), tab)] * 3
    row_outs = {0: (SSD_INNER, F32), 1: (CONV_DIM, F32), 2: (LANES, F32), 3: (ATT_INNER, BF16),
                6: (ATT_INNER, F32)}
    out_specs, out_shape = [], []
    for idx in range(7):
        if idx in row_outs:
            n, dt = row_outs[idx]
            out_specs.append(pl.BlockSpec((tm, n), row))
            out_shape.append(jax.ShapeDtypeStruct((rows, n), dt))
        else:
            out_specs.append(pl.BlockSpec((1, KV_DIM, tm), tr))
            out_shape.append(jax.ShapeDtypeStruct((rows // seq, KV_DIM, seq), F32))
    out_specs.append(pl.BlockSpec((1, SUBLANES, CONV_DIM), slab))
    out_shape.append(jax.ShapeDtypeStruct((rows // seq, SUBLANES, CONV_DIM), F32))
    return in_specs, out_specs, out_shape


def _wconv_in_kernel(x_ref, nw_ref, w_ref, wb_ref, p_ref, hn_scr):
    @pl.when(pl.program_id(0) == 0)
    def _():
        hn_scr[...] = _pre_norm(x_ref, nw_ref)

    row = pl.program_id(0) * W_CHUNK + lax.broadcasted_iota(jnp.int32, (W_CHUNK, 1), 0)
    wb = jnp.where(row < W_ROWS, w_ref[...], 0.0).astype(BF16)
    wb_ref[...] = wb
    p_ref[0] = _dot_nt(hn_scr[...], wb)


def _wconv_in(x2d, nw, w_f32):
    rows = x2d.shape[0]
    n_chunks = pl.cdiv(W_ROWS, W_CHUNK)
    const = lambda j: (0, 0)
    return pl.pallas_call(
        _wconv_in_kernel,
        grid=(n_chunks,),
        in_specs=[
            pl.BlockSpec((rows, D_MODEL), const),
            pl.BlockSpec((1, D_MODEL), const),
            pl.BlockSpec((W_CHUNK, D_MODEL), lambda j: (j, 0)),
        ],
        out_specs=[
            pl.BlockSpec((W_CHUNK, D_MODEL), lambda j: (j, 0)),
            pl.BlockSpec((1, rows, W_CHUNK), lambda j: (j, 0, 0)),
        ],
        out_shape=[
            jax.ShapeDtypeStruct((W_ROWS, D_MODEL), BF16),
            jax.ShapeDtypeStruct((n_chunks, rows, W_CHUNK), F32),
        ],
        scratch_shapes=[pltpu.VMEM((rows, D_MODEL), BF16)],
        compiler_params=pltpu.CompilerParams(
            dimension_semantics=("arbitrary",), vmem_limit_bytes=VMEM_LIMIT),
        name="wconv_in",
    )(x2d, nw, w_f32)


def _sections_kernel(p_ref, *refs):
    *refs, tail_scr = refs
    pfull = jnp.concatenate([p_ref[j] for j in range(p_ref.shape[0])], axis=1)
    tail_scr[...] = pfull[:, Q0:W_ROWS]

    def proj(a, b):
        if a >= Q0:
            return tail_scr[:, a - Q0:b - Q0]
        return pfull[:, a:b]

    _emit_sections(proj, p_ref.shape[1], *refs)


def _sections(p, qnw, knw, tables):
    rows = p.shape[1]
    sec_in, out_specs, out_shape = _section_specs(rows, rows, rows, 1)
    return pl.pallas_call(
        _sections_kernel,
        grid=(1,),
        in_specs=[pl.BlockSpec(p.shape, lambda i: (0, 0, 0))] + sec_in,
        out_specs=out_specs,
        out_shape=out_shape,
        scratch_shapes=[pltpu.VMEM((rows, W_ROWS - Q0), F32)],
        compiler_params=pltpu.CompilerParams(
            dimension_semantics=("arbitrary",), vmem_limit_bytes=VMEM_LIMIT),
        name="sections",
    )(p, qnw, knw, *tables)


def _ssd_chain(s, xbc_ref, dtr_ref, z_ref, cprev_ref, h0_ref, cw_ref, cb_ref, dtb_ref, alog_ref,
               dexp_ref, nw_ref, y_ref, hout_ref, xext, ht, xc_scr, lin, valid):
    L = SSD_CHUNK
    c = pl.program_id(1)
    last = pl.num_programs(1) - 1

    @pl.when(c == 0)
    def _():
        xext[s, 0:SUBLANES, :] = cprev_ref[s]
        for g in range(SSD_GROUPS):
            ht[s, g] = h0_ref[s, g * HALF_INNER:(g + 1) * HALF_INNER, :].T

    def pad_rows(v):
        if lin == L:
            return v
        return jnp.concatenate([v, jnp.zeros((L - lin, v.shape[1]), v.dtype)], axis=0)

    lane = lax.broadcasted_iota(jnp.int32, (1, LANES), 1)
    rowid = lax.broadcasted_iota(jnp.int32, (L, 1), 0)
    dt = jax.nn.softplus(pad_rows(dtr_ref[s]) + dtb_ref[...])
    dt = jnp.where((lane < SSD_HEADS) & (rowid < valid), dt, 0.0)
    dta = dt * (-jnp.exp(alog_ref[...]) * LOG2E)

    r2 = lax.broadcasted_iota(jnp.int32, (L, L), 0)
    c2 = lax.broadcasted_iota(jnp.int32, (L, L), 1)
    tri = r2 >= c2
    tri_b = jnp.where(tri, 1.0, 0.0).astype(BF16)
    cum = _dot_exact_lhs(tri_b, dta)
    cum_last = cum[L - 1:L, :]
    yield

    er = lax.broadcasted_iota(jnp.int32, (LANES, SSD_INNER), 0)
    ec = lax.broadcasted_iota(jnp.int32, (LANES, SSD_INNER), 1)
    expand = jnp.where(ec // SSD_HEADDIM == er, 1.0, 0.0).astype(BF16)
    ecum = _dot_wide_rhs(jnp.exp2(cum), expand)
    wexp = _dot_wide_rhs(jnp.exp2(cum_last - cum) * dt, expand)
    cd = _dot_exact_rhs(jnp.broadcast_to(jnp.exp2(cum_last), (SUBLANES, LANES)), expand)[0:1, :]
    cum_t = cum.T
    dt_t = dt.T
    yield

    if lin < L:
        xext[s, SUBLANES + lin:SUBLANES + L, :] = jnp.zeros((L - lin, CONV_DIM), F32)
    xext[s, SUBLANES:SUBLANES + lin, :] = xbc_ref[s]

    def load(cs):
        return xext[s, 0:SUBLANES + L, cs]

    def store(cs, v):
        xc_scr[s, :, cs] = v

    _causal_conv_silu(load, cw_ref, cb_ref, store)
    xext[s, 0:SUBLANES, :] = xext[s, L:L + SUBLANES, :]
    xc = xc_scr.at[s]
    xs = xc[:, :SSD_INNER]
    bm = xc[:, SSD_INNER:SSD_INNER + SSD_GROUPS * SSD_STATE]
    cm = xc[:, SSD_INNER + SSD_GROUPS * SSD_STATE:]
    bm_b = bm.astype(BF16)
    cm_b = cm.astype(BF16)
    xs_b = xs.astype(BF16)

    y_off = jnp.concatenate(
        [_dot(cm_b[:, g * SSD_STATE:(g + 1) * SSD_STATE], ht[s, g].astype(BF16))
         for g in range(SSD_GROUPS)], axis=1) * ecum
    cbs = [_dot_nt(cm_b[:, g * SSD_STATE:(g + 1) * SSD_STATE], bm_b[:, g * SSD_STATE:(g + 1) * SSD_STATE])
           for g in range(SSD_GROUPS)]
    yield

    lane_half = lane // HEAD_DIM
    heads_per_group = SSD_HEADS // SSD_GROUPS
    y_parts = []
    for g in range(SSD_GROUPS):
        for pr in range(heads_per_group // 2):
            col0 = g * HALF_INNER + pr * LANES
            xp = xs_b[:, col0:col0 + LANES]
            yp = jnp.zeros((L, LANES), F32)
            for e in range(2):
                hh = g * heads_per_group + pr * 2 + e
                decay = jnp.exp2(cum[:, hh:hh + 1] - cum_t[hh:hh + 1, :])
                sc = jnp.where(tri, cbs[g] * decay, 0.0) * dt_t[hh:hh + 1, :]
                xm = jnp.where(lane_half == e, xp, jnp.zeros_like(xp))
                yp = yp + _dot(sc.astype(BF16), xm)
            y_parts.append(yp)
            if pr % 2 == 1:
                yield
    y_diag = jnp.concatenate(y_parts, axis=1)

    xw = (xs * wexp).astype(BF16)
    for g in range(SSD_GROUPS):
        sl = slice(g * HALF_INNER, (g + 1) * HALF_INNER)
        bm_t = bm[:, g * SSD_STATE:(g + 1) * SSD_STATE].T.astype(BF16)
        ht[s, g] = ht[s, g] * cd[:, sl] + _dot(bm_t, xw[:, sl])
    yield

    y = (y_diag + y_off + dexp_ref[...] * xs) * pad_rows(z_ref[s])
    yn = []
    for g in range(SSD_GROUPS):
        yg = y[:, g * HALF_INNER:(g + 1) * HALF_INNER]
        ms = jnp.mean(yg * yg, axis=-1, keepdims=True)
        yn.append(yg * lax.rsqrt(ms + EPS))
    yn = jnp.concatenate(yn, axis=1) * nw_ref[...]
    y_ref[s] = yn[:lin].astype(BF16)

    @pl.when(c == last)
    def _():
        for g in range(SSD_GROUPS):
            hout_ref[s, g * HALF_INNER:(g + 1) * HALF_INNER, :] = ht[s, g].T


def _ssd_kernel(*refs, lin, valid):
    chains = [_ssd_chain(s, *refs, lin, valid) for s in range(SSD_SEQS)]
    while chains:
        alive = []
        for ch in chains:
            try:
                next(ch)
                alive.append(ch)
            except StopIteration:
                pass
        chains = alive


def _ssd(xbc, dtr, z, cprev, h0, cw, cb, dtb, alog, dexp, nw, batch, lin, valid):
    rows = xbc.shape[0]
    seq = rows // batch
    nc = seq // lin
    g = SSD_SEQS
    assert batch % g == 0
    blk = lambda b, c: (b, c, 0)
    per_b = lambda b, c: (b, 0, 0)
    const = lambda b, c: (0, 0)
    y, hout = pl.pallas_call(
        functools.partial(_ssd_kernel, lin=lin, valid=valid),
        grid=(batch // g, nc),
        in_specs=[
            pl.BlockSpec((g, lin, CONV_DIM), blk),
            pl.BlockSpec((g, lin, LANES), blk),
            pl.BlockSpec((g, lin, SSD_INNER), blk),
            pl.BlockSpec((g, SUBLANES, CONV_DIM), per_b),
            pl.BlockSpec((g, SSD_INNER, SSD_STATE), per_b),
            pl.BlockSpec((SUBLANES, CONV_DIM), const),
            pl.BlockSpec((1, CONV_DIM), const),
            pl.BlockSpec((1, LANES), const),
            pl.BlockSpec((1, LANES), const),
            pl.BlockSpec((1, SSD_INNER), const),
            pl.BlockSpec((1, SSD_INNER), const),
        ],
        out_specs=[
            pl.BlockSpec((g, lin, SSD_INNER), blk),
            pl.BlockSpec((g, SSD_INNER, SSD_STATE), per_b),
        ],
        out_shape=[
            jax.ShapeDtypeStruct((batch, seq, SSD_INNER), BF16),
            jax.ShapeDtypeStruct((batch, SSD_INNER, SSD_STATE), F32),
        ],
        scratch_shapes=[
            pltpu.VMEM((g, SUBLANES + SSD_CHUNK + SUBLANES, CONV_DIM), F32),
            pltpu.VMEM((g, SSD_GROUPS, SSD_STATE, HALF_INNER), F32),
            pltpu.VMEM((g, SSD_CHUNK, CONV_DIM), F32),
        ],
        compiler_params=pltpu.CompilerParams(
            dimension_semantics=("arbitrary", "arbitrary"), vmem_limit_bytes=VMEM_LIMIT),
        name="ssd",
    )(xbc.reshape(batch, seq, CONV_DIM), dtr.reshape(batch, seq, LANES),
      z.reshape(batch, seq, SSD_INNER), cprev, h0, cw, cb, dtb, alog, dexp, nw)
    return y.reshape(rows, SSD_INNER), hout


def _attn_prompt_kernel(q_ref, kt_ref, vt_ref, g_ref, o_ref,
                        bias_scr, kh_scr, va_scr, qt_scr, m_scr, acc_scr, ot_scr, s_scr, *, n_tab, n_sb_max):
    b = pl.program_id(0)
    i = pl.program_id(1)
    seq = kt_ref.shape[2]
    pair_w = 2 * Q_BLOCK
    pairs_per_kv = ATT_GQ // 2

    @pl.when((b == 0) & (i == 0))
    def _():
        r = lax.broadcasted_iota(jnp.int32, (K_SUPER, Q_BLOCK), 0)
        c = lax.broadcasted_iota(jnp.int32, (K_SUPER, Q_BLOCK), 1)
        for tb in range(n_tab):
            w = _multiplicity(tb * Q_BLOCK + c - r)
            bias_scr[tb] = jnp.where(w > 0.0, jnp.log2(jnp.maximum(w, 1.0)), NEG)
        va_scr[:, HEAD_DIM:, :] = jnp.ones((ATT_KV_HEADS, ONES_ROWS, seq), BF16)

    @pl.when(i == 0)
    def _():
        for kvh in range(ATT_KV_HEADS):
            hs = slice(kvh * HEAD_DIM, (kvh + 1) * HEAD_DIM)
            va_scr[kvh, 0:HEAD_DIM, :] = vt_ref[0, hs, :].astype(BF16)
        for cidx in range(seq // K_SUPER):
            rows = slice(cidx * K_SUPER, (cidx + 1) * K_SUPER)
            kc = kt_ref[0, :, rows].T
            for kvh in range(ATT_KV_HEADS):
                hs = slice(kvh * HEAD_DIM, (kvh + 1) * HEAD_DIM)
                kh_scr[kvh, rows, :] = kc[:, hs].astype(BF16)

    qt = q_ref[...].astype(F32).T
    for kvh in range(ATT_KV_HEADS):
        qt_scr[kvh] = jnp.concatenate(
            [qt[(kvh * ATT_GQ + g) * HEAD_DIM:(kvh * ATT_GQ + g + 1) * HEAD_DIM, :]
             for g in range(ATT_GQ)], axis=1).astype(BF16)
    m_scr[...] = jnp.full(m_scr.shape, NEG, F32)
    acc_scr[...] = jnp.zeros(acc_scr.shape, F32)
    par = i % Q_PER_SUPER
    j_last = i // Q_PER_SUPER
    n_sb = jnp.minimum(j_last + 1, n_sb_max)

    def body(dl, carry):
        start = pl.multiple_of((j_last - dl) * K_SUPER, K_SUPER)
        bias = bias_scr[par + Q_PER_SUPER * dl]
        bias2 = jnp.concatenate([bias, bias], axis=1)
        for kvh in range(ATT_KV_HEADS):
            s_scr[kvh] = _dot(kh_scr[kvh, pl.ds(start, K_SUPER), :], qt_scr[kvh])
        for kvh in range(ATT_KV_HEADS):
            vtb = va_scr[kvh, :, pl.ds(start, K_SUPER)]
            for pr in range(pairs_per_kv):
                u = kvh * pairs_per_kv + pr
                s = s_scr[kvh, :, pr * pair_w:(pr + 1) * pair_w] + bias2
                m_old = m_scr[u, 0:1, :]
                m_new = jnp.maximum(m_old, jnp.max(s, axis=0, keepdims=True))
                p = jnp.exp2(s - m_new).astype(BF16)
                acc_scr[u] = jnp.exp2(m_old - m_new) * acc_scr[u] + _dot(vtb, p)
                m_scr[u, 0:1, :] = m_new
        return carry

    lax.fori_loop(0, n_sb, body, 0)

    for u in range(ATT_HEADS // 2):
        acc = acc_scr[u]
        o = acc[0:HEAD_DIM] / acc[HEAD_DIM:HEAD_DIM + 1]
        for e in range(2):
            h = 2 * u + e
            ot_scr[h * HEAD_DIM:(h + 1) * HEAD_DIM, :] = o[:, e * Q_BLOCK:(e + 1) * Q_BLOCK]
    o_ref[...] = (ot_scr[...].T * g_ref[...]).astype(BF16)


def _attn_prompt(q, kt, vt, gate, batch, seq):
    rows = q.shape[0]
    nqb = seq // Q_BLOCK
    assert seq % K_SUPER == 0
    n_sb_max = min(seq // K_SUPER, W_MAX // K_SUPER + 1)
    n_tab = Q_PER_SUPER * n_sb_max
    qblk = lambda b, i: (b * nqb + i, 0)
    per_b = lambda b, i: (b, 0, 0)
    return pl.pallas_call(
        functools.partial(_attn_prompt_kernel, n_tab=n_tab, n_sb_max=n_sb_max),
        grid=(batch, nqb),
        in_specs=[
            pl.BlockSpec((Q_BLOCK, ATT_INNER), qblk),
            pl.BlockSpec((1, KV_DIM, seq), per_b),
            pl.BlockSpec((1, KV_DIM, seq), per_b),
            pl.BlockSpec((Q_BLOCK, ATT_INNER), qblk),
        ],
        out_specs=pl.BlockSpec((Q_BLOCK, ATT_INNER), qblk),
        out_shape=jax.ShapeDtypeStruct((rows, ATT_INNER), BF16),
        scratch_shapes=[
            pltpu.VMEM((n_tab, K_SUPER, Q_BLOCK), F32),
            pltpu.VMEM((ATT_KV_HEADS, seq, HEAD_DIM), BF16),
            pltpu.VMEM((ATT_KV_HEADS, ACC_ROWS, seq), BF16),
            pltpu.VMEM((ATT_KV_HEADS, HEAD_DIM, ATT_GQ * Q_BLOCK), BF16),
            pltpu.VMEM((ATT_HEADS // 2, SUBLANES, 2 * Q_BLOCK), F32),
            pltpu.VMEM((ATT_HEADS // 2, ACC_ROWS, 2 * Q_BLOCK), F32),
            pltpu.VMEM((ATT_INNER, Q_BLOCK), F32),
            pltpu.VMEM((ATT_KV_HEADS, K_SUPER, ATT_GQ * Q_BLOCK), F32),
        ],
        compiler_params=pltpu.CompilerParams(
            dimension_semantics=("arbitrary", "arbitrary"), vmem_limit_bytes=VMEM_LIMIT),
        name="attn_prompt",
    )(q, kt, vt, gate)


def _attn_sample_kernel(q_ref, knt_ref, vnt_ref, g_ref, ckt_ref, cvt_ref, o_ref, kot_ref, vot_ref, *, ts):
    win = ckt_ref.shape[2]
    tp = SAMPLE_PAD
    rows = ATT_GQ * tp
    t_c = lax.broadcasted_iota(jnp.int32, (rows, win), 0) % tp
    j_c = lax.broadcasted_iota(jnp.int32, (rows, win), 1)
    w_c = _multiplicity(win + t_c - j_c)
    t_n = lax.broadcasted_iota(jnp.int32, (rows, tp), 0) % tp
    j_n = lax.broadcasted_iota(jnp.int32, (rows, tp), 1)
    w_n = jnp.where(j_n < ts, _multiplicity(t_n - j_n), 0.0)
    for kvh in range(ATT_KV_HEADS):
        c0 = kvh * ATT_GQ * HEAD_DIM
        qh = jnp.concatenate(
            [q_ref[:, c0 + g * HEAD_DIM:c0 + (g + 1) * HEAD_DIM] for g in range(ATT_GQ)], axis=0)
        hs = slice(kvh * HEAD_DIM, (kvh + 1) * HEAD_DIM)
        s_c = jnp.where(w_c > 0.0, _dot(qh, ckt_ref[0, hs, :].astype(BF16)), NEG)
        s_n = jnp.where(w_n > 0.0, _dot(qh, knt_ref[0, hs, :].astype(BF16)), NEG)
        m = jnp.maximum(jnp.max(s_c, axis=-1, keepdims=True), jnp.max(s_n, axis=-1, keepdims=True))
        p_c = jnp.exp2(s_c - m) * w_c
        p_n = jnp.exp2(s_n - m) * w_n
        den = jnp.sum(p_c, axis=-1, keepdims=True) + jnp.sum(p_n, axis=-1, keepdims=True)
        num = (_dot_nt(p_c.astype(BF16), cvt_ref[0, hs, :].astype(BF16))
               + _dot_nt(p_n.astype(BF16), vnt_ref[0, hs, :].astype(BF16)))
        o = num / den
        o = jnp.concatenate([o[g * tp:(g + 1) * tp] for g in range(ATT_GQ)], axis=1)
        csl = slice(c0, c0 + ATT_GQ * HEAD_DIM)
        o_ref[:, csl] = (o * g_ref[:, csl]).astype(BF16)
    kot_ref[0, :, 0:win - ts] = ckt_ref[0, :, ts:win]
    kot_ref[0, :, win - ts:win] = knt_ref[0, :, 0:ts]
    vot_ref[0, :, 0:win - ts] = cvt_ref[0, :, ts:win]
    vot_ref[0, :, win - ts:win] = vnt_ref[0, :, 0:ts]


def _attn_sample(q, knt, vnt, gate, ckt, cvt, ts):
    batch, _, win = ckt.shape
    tp = SAMPLE_PAD
    blk = lambda b: (b, 0)
    per_b = lambda b: (b, 0, 0)
    return pl.pallas_call(
        functools.partial(_attn_sample_kernel, ts=ts),
        grid=(batch,),
        in_specs=[
            pl.BlockSpec((tp, ATT_INNER), blk),
            pl.BlockSpec((1, KV_DIM, tp), per_b),
            pl.BlockSpec((1, KV_DIM, tp), per_b),
            pl.BlockSpec((tp, ATT_INNER), blk),
            pl.BlockSpec((1, KV_DIM, win), per_b),
            pl.BlockSpec((1, KV_DIM, win), per_b),
        ],
        out_specs=[
            pl.BlockSpec((tp, ATT_INNER), blk),
            pl.BlockSpec((1, KV_DIM, win), per_b),
            pl.BlockSpec((1, KV_DIM, win), per_b),
        ],
        out_shape=[
            jax.ShapeDtypeStruct((batch * tp, ATT_INNER), BF16),
            jax.ShapeDtypeStruct((batch, KV_DIM, win), F32),
            jax.ShapeDtypeStruct((batch, KV_DIM, win), F32),
        ],
        compiler_params=pltpu.CompilerParams(
            dimension_semantics=("arbitrary",), vmem_limit_bytes=VMEM_LIMIT),
        name="attn_sample",
    )(q, knt, vnt, gate, ckt, cvt)


def _outproj_kernel(x_ref, ys_ref, ya_ref, w_ref, o_ref):
    o_ref[...] = (x_ref[...] + _dot(ys_ref[...], w_ref[0:SSD_INNER, :])
                  + _dot(ya_ref[...], w_ref[SSD_INNER:, :]))


def _out_proj(x2d, y_ssd, y_att, w_o, tm):
    rows = x2d.shape[0]
    row = lambda i: (i, 0)
    const = lambda i: (0, 0)
    return pl.pallas_call(
        _outproj_kernel,
        grid=(rows // tm,),
        in_specs=[
            pl.BlockSpec((tm, D_MODEL), row),
            pl.BlockSpec((tm, SSD_INNER), row),
            pl.BlockSpec((tm, ATT_INNER), row),
            pl.BlockSpec((SSD_INNER + ATT_INNER, D_MODEL), const, pipeline_mode=pl.Buffered(1)),
        ],
        out_specs=pl.BlockSpec((tm, D_MODEL), row),
        out_shape=jax.ShapeDtypeStruct((rows, D_MODEL), F32),
        compiler_params=pltpu.CompilerParams(
            dimension_semantics=("arbitrary",), vmem_limit_bytes=VMEM_LIMIT),
        name="out_proj",
    )(x2d, y_ssd, y_att, w_o)


def _wconv_out_kernel(x_ref, ys_ref, ya_ref, w_ref, wb_ref, o_ref):
    j = pl.program_id(0)
    wb = w_ref[...].astype(BF16)
    wb_ref[...] = wb
    y = jnp.where(j < SSD_INNER // WO_CHUNK, ys_ref[...], ya_ref[...])
    part = _dot(y, wb)

    @pl.when(j == 0)
    def _():
        o_ref[...] = x_ref[...] + part

    @pl.when(j > 0)
    def _():
        o_ref[...] += part


def _wconv_out(x2d, y_ssd, y_att, w_f32):
    rows = x2d.shape[0]
    n_ssd = SSD_INNER // WO_CHUNK
    n_chunks = (SSD_INNER + ATT_INNER) // WO_CHUNK
    const = lambda j: (0, 0)
    return pl.pallas_call(
        _wconv_out_kernel,
        grid=(n_chunks,),
        in_specs=[
            pl.BlockSpec((rows, D_MODEL), const),
            pl.BlockSpec((rows, WO_CHUNK), lambda j: (0, jnp.minimum(j, n_ssd - 1))),
            pl.BlockSpec((rows, WO_CHUNK), lambda j: (0, jnp.maximum(j - n_ssd, 0))),
            pl.BlockSpec((WO_CHUNK, D_MODEL), lambda j: (j, 0)),
        ],
        out_specs=[
            pl.BlockSpec((WO_CHUNK, D_MODEL), lambda j: (j, 0)),
            pl.BlockSpec((rows, D_MODEL), const),
        ],
        out_shape=[
            jax.ShapeDtypeStruct((SSD_INNER + ATT_INNER, D_MODEL), BF16),
            jax.ShapeDtypeStruct((rows, D_MODEL), F32),
        ],
        compiler_params=pltpu.CompilerParams(
            dimension_semantics=("arbitrary",), vmem_limit_bytes=VMEM_LIMIT),
        name="wconv_out",
    )(x2d, y_ssd, y_att, w_f32)


def _rope_tables(pos):
    n = pos.shape[0]
    half = ROT_DIM // 2
    inv = ROPE_THETA ** (-np.arange(0, ROT_DIM, 2, dtype=np.float64) / ROT_DIM)
    ang = pos.astype(np.float64)[:, None] * inv[None, :]
    cos, sin = np.cos(ang), np.sin(ang)
    rest = HEAD_DIM - ROT_DIM
    zh = np.zeros((n, half))
    cos_h = np.concatenate([cos, cos, np.ones((n, rest))], axis=1)
    sa_h = np.concatenate([zh, sin, np.zeros((n, rest))], axis=1)
    sb_h = np.concatenate([-sin, zh, np.zeros((n, rest))], axis=1)
    rep = LANES // HEAD_DIM
    return tuple(jnp.asarray(np.tile(t, (1, rep)), dtype=F32) for t in (cos_h, sa_h, sb_h))


def _lane_pad(v, n=LANES):
    return jnp.pad(v, (0, n - v.shape[0])).reshape(1, n)


def _to_cache(xt, batch, seq):
    return xt.reshape(1, batch, ATT_KV_HEADS, HEAD_DIM, seq).transpose(0, 1, 4, 2, 3)


def kernel(x_prompt, x_sample, cache_k, cache_v, state_conv, state_ssm, norm_w, w_in, conv_w,
           conv_b, dt_bias, a_log, d_skip, ssd_norm_w, q_norm_w, k_norm_w, w_out):
    bp, tp_, _ = x_prompt.shape
    bs, ts, _ = x_sample.shape
    depth = w_in.shape[0]
    assert depth == 1 and tp_ % SSD_CHUNK == 0 and ts <= SAMPLE_PAD and ts >= CONV_W - 1
    l = 0
    win = cache_k.shape[2]

    nw = norm_w[l].reshape(1, D_MODEL)
    rep = LANES // HEAD_DIM
    qnw = jnp.tile(q_norm_w[l], rep).reshape(1, LANES)
    knw = jnp.tile(k_norm_w[l], rep).reshape(1, LANES)
    cw = jnp.pad(conv_w[l], ((0, SUBLANES - CONV_W), (0, 0)))
    cb = conv_b[l].reshape(1, CONV_DIM)
    dtb = _lane_pad(dt_bias[l])
    alog = _lane_pad(a_log[l])
    dexp = jnp.repeat(d_skip[l], SSD_HEADDIM).reshape(1, SSD_INNER)
    snw = ssd_norm_w[l].reshape(1, SSD_INNER)

    pad = SAMPLE_PAD
    xs2 = jnp.pad(x_sample, ((0, 0), (0, pad - ts), (0, 0))).reshape(bs * pad, D_MODEL)
    tabs_s = _rope_tables(PAST_LEN + np.arange(bs * pad) % pad)
    w_t, proj_s = _wconv_in(xs2, nw, jnp.swapaxes(w_in[l], 0, 1))
    zs, xbc, dtr, q, kt, vt, gs, _ = _sections(proj_s, qnw, knw, tabs_s)
    cprev = jnp.pad(state_conv[l], ((0, 0), (SUBLANES - (CONV_W - 1), 0), (0, 0)))
    y_ssd, h_s = _ssd(xbc, dtr, zs, cprev, state_ssm[l].reshape(bs, SSD_INNER, SSD_STATE),
                      cw, cb, dtb, alog, dexp, snw, bs, pad, ts)
    knt = kt.reshape(KV_DIM, bs, pad).transpose(1, 0, 2)
    vnt = vt.reshape(KV_DIM, bs, pad).transpose(1, 0, 2)
    ckt = cache_k[l].transpose(0, 2, 3, 1).reshape(bs, KV_DIM, win)
    cvt = cache_v[l].transpose(0, 2, 3, 1).reshape(bs, KV_DIM, win)
    y_att, kot, vot = _attn_sample(q, knt, vnt, gs, ckt, cvt, ts)
    w_o, y_s = _wconv_out(xs2, y_ssd, y_att, w_out[l])
    y_s = y_s.reshape(bs, pad, D_MODEL)[:, :ts]
    k_s = _to_cache(kot, bs, win)
    v_s = _to_cache(vot, bs, win)
    c_s = xbc.reshape(bs, pad, CONV_DIM)[:, ts - (CONV_W - 1):ts][None]
    h_s = h_s.reshape(1, bs, SSD_HEADS, SSD_HEADDIM, SSD_STATE)

    tm = TM_IN
    xp2 = x_prompt.reshape(bp * tp_, D_MODEL)
    tabs = _rope_tables(np.arange(tp_))
    zs, xbc, dtr, q, kt, vt, gs, ctail = _in_proj(xp2, nw, w_t, qnw, knw, tabs, tm, tp_)
    y_ssd, h_p = _ssd(xbc, dtr, zs, jnp.zeros((bp, SUBLANES, CONV_DIM), F32),
                      jnp.zeros((bp, SSD_INNER, SSD_STATE), F32),
                      cw, cb, dtb, alog, dexp, snw, bp, SSD_CHUNK, SSD_CHUNK)
    y_att = _attn_prompt(q, kt, vt, gs, bp, tp_)
    y_p = _out_proj(xp2, y_ssd, y_att, w_o, TM_OUT).reshape(bp, tp_, D_MODEL)
    keep = min(W_MAX, tp_)
    k_p = _to_cache(kt, bp, tp_)[:, :, tp_ - keep:]
    v_p = _to_cache(vt, bp, tp_)[:, :, tp_ - keep:]
    c_p = ctail[:, SUBLANES - (CONV_W - 1):][None]
    h_p = h_p.reshape(1, bp, SSD_HEADS, SSD_HEADDIM, SSD_STATE)

    return (y_p, y_s, k_p, v_p, c_p, h_p, k_s, v_s, c_s, h_s)
```

```python
import functools

import jax
import jax.numpy as jnp
import numpy as np
from jax import lax
from jax.experimental import pallas as pl
from jax.experimental.pallas import tpu as pltpu

F32 = jnp.float32
BF16 = jnp.bfloat16

D_MODEL = 2048
SSD_HEADS = 16
SSD_HEADDIM = 64
SSD_INNER = SSD_HEADS * SSD_HEADDIM
SSD_GROUPS = 2
SSD_STATE = 128
CONV_W = 4
CONV_DIM = SSD_INNER + 2 * SSD_GROUPS * SSD_STATE
SSD_CHUNK = 128
ATT_HEADS = 16
ATT_KV_HEADS = 4
HEAD_DIM = 64
ATT_GQ = ATT_HEADS // ATT_KV_HEADS
ATT_INNER = ATT_HEADS * HEAD_DIM
KV_DIM = ATT_KV_HEADS * HEAD_DIM
ROT_DIM = HEAD_DIM // 4
ROPE_THETA = 500000.0
DILATED_BRANCHES = ((128, 1), (512, 4), (2048, 16))
W_MAX = 2048
PAST_LEN = 16384
EPS = 1e-6

LANES = 128
SUBLANES = 8
Q_BLOCK = 256
K_SUPER = 256
Q_PER_SUPER = K_SUPER // Q_BLOCK
ONES_ROWS = 16
ACC_ROWS = HEAD_DIM + ONES_ROWS
CONV_COLS = 256
TM_IN = 512
TM_OUT = 512
SSD_SEQS = 2
SAMPLE_PAD = 16
NEG = -1e30
VMEM_LIMIT = 58 * 1024 * 1024

Z0 = 0
X0 = Z0 + SSD_INNER
DT0 = X0 + CONV_DIM
Q0 = DT0 + SSD_HEADS
K0 = Q0 + ATT_INNER
V0 = K0 + KV_DIM
G0 = V0 + KV_DIM
W_ROWS = G0 + ATT_INNER
W_CHUNK = 1024
WO_CHUNK = 512
HALF_INNER = SSD_INNER // SSD_GROUPS
LOG2E = 1.4426950408889634
Q_SCALE = HEAD_DIM ** -0.5 * LOG2E


def _dot(a, b):
    return jnp.dot(a, b, preferred_element_type=F32)


def _dot_nt(a, b):
    return lax.dot_general(a, b, (((1,), (1,)), ((), ())), preferred_element_type=F32)


def _split3(x):
    hi = x.astype(BF16)
    r1 = x - hi.astype(F32)
    mid = r1.astype(BF16)
    lo = (r1 - mid.astype(F32)).astype(BF16)
    return hi, mid, lo


def _dot_exact_rhs(x, m):
    hi, mid, lo = _split3(x)
    return _dot(hi, m) + _dot(mid, m) + _dot(lo, m)


def _dot_wide_rhs(x, m):
    hi = x.astype(BF16)
    lo = (x - hi.astype(F32)).astype(BF16)
    return _dot(hi, m) + _dot(lo, m)


def _dot_exact_lhs(m, x):
    hi, mid, lo = _split3(x)
    return _dot(m, hi) + _dot(m, mid) + _dot(m, lo)


def _silu(x):
    hx = 0.5 * x
    return hx * jnp.tanh(hx) + hx


def _multiplicity(d):
    w = jnp.zeros(d.shape, F32)
    for window, dil in DILATED_BRANCHES:
        hit = (d >= 0) & (d <= window) & (lax.rem(d, dil) == 0)
        w = w + jnp.where(hit, 1.0, 0.0)
    return w


def _norm_rope(y, nw, cos, sa, sb, scale):
    lane = lax.broadcasted_iota(jnp.int32, (1, LANES), 1)
    first = lane < HEAD_DIM
    y2 = y * y
    s_lo = jnp.sum(jnp.where(first, y2, 0.0), axis=-1, keepdims=True)
    s_hi = jnp.sum(jnp.where(first, 0.0, y2), axis=-1, keepdims=True)
    ms = jnp.where(first, s_lo, s_hi) * (1.0 / HEAD_DIM)
    yn = y * lax.rsqrt(ms + EPS) * nw
    half = ROT_DIM // 2
    rot = yn * cos + pltpu.roll(yn, half, 1) * sa + pltpu.roll(yn, LANES - half, 1) * sb
    return rot * scale


def _causal_conv_silu(load, cw_ref, cb_ref, store):
    for c0 in range(0, CONV_DIM, CONV_COLS):
        cs = slice(c0, c0 + CONV_COLS)
        xe = load(cs)
        x2 = pltpu.roll(xe, 2, 0)
        even = cw_ref[3:4, cs] * xe + cw_ref[1:2, cs] * x2
        odd = cw_ref[2:3, cs] * xe + cw_ref[0:1, cs] * x2
        store(cs, _silu((even + pltpu.roll(odd, 1, 0))[SUBLANES:, :] + cb_ref[:, cs]))


def _pre_norm(x_ref, nw_ref):
    x = x_ref[...]
    ms = jnp.mean(x * x, axis=-1, keepdims=True)
    return (x * lax.rsqrt(ms + EPS) * nw_ref[...]).astype(BF16)


def _inproj_kernel(x_ref, nw_ref, w_ref, *refs):
    hn = _pre_norm(x_ref, nw_ref)

    def proj(a, b):
        return _dot_nt(hn, w_ref[a:b, :])

    _emit_sections(proj, x_ref.shape[0], *refs)


def _emit_sections(proj, tm, qnw_ref, knw_ref, cos_ref, sa_ref, sb_ref,
                   zs_ref, xbc_ref, dt_ref, q_ref, kt_ref, vt_ref, gs_ref, ctail_ref):
    cos, sa, sb = cos_ref[...], sa_ref[...], sb_ref[...]
    q = proj(Q0, K0)
    for c in range(ATT_INNER // LANES):
        sl = slice(c * LANES, (c + 1) * LANES)
        qn = _norm_rope(q[:, sl], qnw_ref[...], cos, sa, sb, Q_SCALE)
        if len(q_ref.shape) == 3:
            q_ref[0, sl, :] = qn.T.astype(BF16)
        else:
            q_ref[:, sl] = qn.astype(BF16)
    k = proj(K0, V0)
    kn = [_norm_rope(k[:, c * LANES:(c + 1) * LANES], knw_ref[...], cos, sa, sb, 1.0)
          for c in range(KV_DIM // LANES)]
    kt_ref[0] = jnp.concatenate(kn, axis=1).T

    xbc = proj(X0, DT0)
    xbc_ref[...] = xbc
    ctail_ref[0] = xbc[tm - SUBLANES:tm, :]
    vt_ref[0] = proj(V0, G0).T
    zs_ref[...] = _silu(proj(Z0, X0))
    gs_ref[...] = _silu(proj(G0, W_ROWS))
    dt_ref[...] = proj(DT0, DT0 + LANES)


def _in_proj(x2d, nw, w_t, qnw, knw, tables, tm, seq):
    rows = x2d.shape[0]
    const = lambda i: (0, 0)
    row = lambda i: (i, 0)
    sec_in, out_specs, out_shape = _section_specs(rows, tm, seq, tables[0].shape[0] // tm, True)
    return pl.pallas_call(
        _inproj_kernel,
        grid=(rows // tm,),
        in_specs=[
            pl.BlockSpec((tm, D_MODEL), row),
            pl.BlockSpec((1, D_MODEL), const),
            pl.BlockSpec((W_ROWS, D_MODEL), const, pipeline_mode=pl.Buffered(1)),
        ] + sec_in,
        out_specs=out_specs,
        out_shape=out_shape,
        compiler_params=pltpu.CompilerParams(
            dimension_semantics=("arbitrary",), vmem_limit_bytes=VMEM_LIMIT),
        name="in_proj",
    )(x2d, nw, w_t, qnw, knw, *tables)


def _section_specs(rows, tm, seq, period, q_transposed):
    per_seq = seq // tm
    const = lambda i: (0, 0)
    row = lambda i: (i, 0)
    tab = lambda i: (i % period, 0)
    tr = lambda i: (i // per_seq, 0, i % per_seq)
    slab = lambda i: (i // per_seq, 0, 0)
    in_specs = [pl.BlockSpec((1, LANES), const)] * 2 + [pl.BlockSpec((tm, LANES), tab)] * 3
    row_outs = {0: (SSD_INNER, F32), 1: (CONV_DIM, F32), 2: (LANES, F32), 3: (ATT_INNER, BF16),
                6: (ATT_INNER, F32)}
    out_specs, out_shape = [], []
    for idx in range(7):
        if idx == 3 and q_transposed:
            out_specs.append(pl.BlockSpec((1, ATT_INNER, tm), tr))
            out_shape.append(jax.ShapeDtypeStruct((rows // seq, ATT_INNER, seq), BF16))
        elif idx in row_outs:
            n, dt = row_outs[idx]
            out_specs.append(pl.BlockSpec((tm, n), row))
            out_shape.append(jax.ShapeDtypeStruct((rows, n), dt))
        else:
            out_specs.append(pl.BlockSpec((1, KV_DIM, tm), tr))
            out_shape.append(jax.ShapeDtypeStruct((rows // seq, KV_DIM, seq), F32))
    out_specs.append(pl.BlockSpec((1, SUBLANES, CONV_DIM), slab))
    out_shape.append(jax.ShapeDtypeStruct((rows // seq, SUBLANES, CONV_DIM), F32))
    return in_specs, out_specs, out_shape


def _wconv_in_kernel(x_ref, nw_ref, w_ref, wb_ref, p_ref, hn_scr):
    @pl.when(pl.program_id(0) == 0)
    def _():
        hn_scr[...] = _pre_norm(x_ref, nw_ref)

    row = pl.program_id(0) * W_CHUNK + lax.broadcasted_iota(jnp.int32, (W_CHUNK, 1), 0)
    wb = jnp.where(row < W_ROWS, w_ref[...], 0.0).astype(BF16)
    wb_ref[...] = wb
    p_ref[0] = _dot_nt(hn_scr[...], wb)


def _wconv_in(x2d, nw, w_f32):
    rows = x2d.shape[0]
    n_chunks = pl.cdiv(W_ROWS, W_CHUNK)
    const = lambda j: (0, 0)
    return pl.pallas_call(
        _wconv_in_kernel,
        grid=(n_chunks,),
        in_specs=[
            pl.BlockSpec((rows, D_MODEL), const),
            pl.BlockSpec((1, D_MODEL), const),
            pl.BlockSpec((W_CHUNK, D_MODEL), lambda j: (j, 0)),
        ],
        out_specs=[
            pl.BlockSpec((W_CHUNK, D_MODEL), lambda j: (j, 0)),
            pl.BlockSpec((1, rows, W_CHUNK), lambda j: (j, 0, 0)),
        ],
        out_shape=[
            jax.ShapeDtypeStruct((W_ROWS, D_MODEL), BF16),
            jax.ShapeDtypeStruct((n_chunks, rows, W_CHUNK), F32),
        ],
        scratch_shapes=[pltpu.VMEM((rows, D_MODEL), BF16)],
        compiler_params=pltpu.CompilerParams(
            dimension_semantics=("arbitrary",), vmem_limit_bytes=VMEM_LIMIT),
        name="wconv_in",
    )(x2d, nw, w_f32)


def _sections_kernel(p_ref, *refs):
    *refs, tail_scr = refs
    pfull = jnp.concatenate([p_ref[j] for j in range(p_ref.shape[0])], axis=1)
    tail_scr[...] = pfull[:, Q0:W_ROWS]

    def proj(a, b):
        if a >= Q0:
            return tail_scr[:, a - Q0:b - Q0]
        return pfull[:, a:b]

    _emit_sections(proj, p_ref.shape[1], *refs)


def _sections(p, qnw, knw, tables):
    rows = p.shape[1]
    sec_in, out_specs, out_shape = _section_specs(rows, rows, rows, 1, False)
    return pl.pallas_call(
        _sections_kernel,
        grid=(1,),
        in_specs=[pl.BlockSpec(p.shape, lambda i: (0, 0, 0))] + sec_in,
        out_specs=out_specs,
        out_shape=out_shape,
        scratch_shapes=[pltpu.VMEM((rows, W_ROWS - Q0), F32)],
        compiler_params=pltpu.CompilerParams(
            dimension_semantics=("arbitrary",), vmem_limit_bytes=VMEM_LIMIT),
        name="sections",
    )(p, qnw, knw, *tables)


def _ssd_chain(s, xbc_ref, dtr_ref, z_ref, cprev_ref, h0_ref, cw_ref, cb_ref, dtb_ref, alog_ref,
               dexp_ref, nw_ref, y_ref, hout_ref, xext, ht, xc_scr, lin, valid):
    L = SSD_CHUNK
    c = pl.program_id(1)
    last = pl.num_programs(1) - 1

    @pl.when(c == 0)
    def _():
        xext[s, 0:SUBLANES, :] = cprev_ref[s]
        for g in range(SSD_GROUPS):
            ht[s, g] = h0_ref[s, g * HALF_INNER:(g + 1) * HALF_INNER, :].T

    def pad_rows(v):
        if lin == L:
            return v
        return jnp.concatenate([v, jnp.zeros((L - lin, v.shape[1]), v.dtype)], axis=0)

    lane = lax.broadcasted_iota(jnp.int32, (1, LANES), 1)
    rowid = lax.broadcasted_iota(jnp.int32, (L, 1), 0)
    dt = jax.nn.softplus(pad_rows(dtr_ref[s]) + dtb_ref[...])
    dt = jnp.where((lane < SSD_HEADS) & (rowid < valid), dt, 0.0)
    dta = dt * (-jnp.exp(alog_ref[...]) * LOG2E)

    r2 = lax.broadcasted_iota(jnp.int32, (L, L), 0)
    c2 = lax.broadcasted_iota(jnp.int32, (L, L), 1)
    tri = r2 >= c2
    tri_b = jnp.where(tri, 1.0, 0.0).astype(BF16)
    cum = _dot_exact_lhs(tri_b, dta)
    cum_last = cum[L - 1:L, :]
    yield

    er = lax.broadcasted_iota(jnp.int32, (LANES, SSD_INNER), 0)
    ec = lax.broadcasted_iota(jnp.int32, (LANES, SSD_INNER), 1)
    expand = jnp.where(ec // SSD_HEADDIM == er, 1.0, 0.0).astype(BF16)
    ecum = _dot_wide_rhs(jnp.exp2(cum), expand)
    wexp = _dot_wide_rhs(jnp.exp2(cum_last - cum) * dt, expand)
    cd = _dot_exact_rhs(jnp.broadcast_to(jnp.exp2(cum_last), (SUBLANES, LANES)), expand)[0:1, :]
    cum_t = cum.T
    dt_t = dt.T
    yield

    if lin < L:
        xext[s, SUBLANES + lin:SUBLANES + L, :] = jnp.zeros((L - lin, CONV_DIM), F32)
    xext[s, SUBLANES:SUBLANES + lin, :] = xbc_ref[s]

    def load(cs):
        return xext[s, 0:SUBLANES + L, cs]

    def store(cs, v):
        xc_scr[s, :, cs] = v

    _causal_conv_silu(load, cw_ref, cb_ref, store)
    xext[s, 0:SUBLANES, :] = xext[s, L:L + SUBLANES, :]
    xc = xc_scr.at[s]
    xs = xc[:, :SSD_INNER]
    bm = xc[:, SSD_INNER:SSD_INNER + SSD_GROUPS * SSD_STATE]
    cm = xc[:, SSD_INNER + SSD_GROUPS * SSD_STATE:]
    bm_b = bm.astype(BF16)
    cm_b = cm.astype(BF16)
    xs_b = xs.astype(BF16)

    y_off = jnp.concatenate(
        [_dot(cm_b[:, g * SSD_STATE:(g + 1) * SSD_STATE], ht[s, g].astype(BF16))
         for g in range(SSD_GROUPS)], axis=1) * ecum
    cbs = [_dot_nt(cm_b[:, g * SSD_STATE:(g + 1) * SSD_STATE], bm_b[:, g * SSD_STATE:(g + 1) * SSD_STATE])
           for g in range(SSD_GROUPS)]
    yield

    lane_half = lane // HEAD_DIM
    heads_per_group = SSD_HEADS // SSD_GROUPS
    y_parts = []
    for g in range(SSD_GROUPS):
        for pr in range(heads_per_group // 2):
            col0 = g * HALF_INNER + pr * LANES
            xp = xs_b[:, col0:col0 + LANES]
            yp = jnp.zeros((L, LANES), F32)
            for e in range(2):
                hh = g * heads_per_group + pr * 2 + e
                decay = jnp.exp2(cum[:, hh:hh + 1] - cum_t[hh:hh + 1, :])
                sc = jnp.where(tri, cbs[g] * decay, 0.0) * dt_t[hh:hh + 1, :]
                xm = jnp.where(lane_half == e, xp, jnp.zeros_like(xp))
                yp = yp + _dot(sc.astype(BF16), xm)
            y_parts.append(yp)
            if pr % 2 == 1:
                yield
    y_diag = jnp.concatenate(y_parts, axis=1)

    xw = (xs * wexp).astype(BF16)
    for g in range(SSD_GROUPS):
        sl = slice(g * HALF_INNER, (g + 1) * HALF_INNER)
        bm_t = bm[:, g * SSD_STATE:(g + 1) * SSD_STATE].T.astype(BF16)
        ht[s, g] = ht[s, g] * cd[:, sl] + _dot(bm_t, xw[:, sl])
    yield

    y = (y_diag + y_off + dexp_ref[...] * xs) * pad_rows(z_ref[s])
    yn = []
    for g in range(SSD_GROUPS):
        yg = y[:, g * HALF_INNER:(g + 1) * HALF_INNER]
        ms = jnp.mean(yg * yg, axis=-1, keepdims=True)
        yn.append(yg * lax.rsqrt(ms + EPS))
    yn = jnp.concatenate(yn, axis=1) * nw_ref[...]
    y_ref[s] = yn[:lin].astype(BF16)

    @pl.when(c == last)
    def _():
        for g in range(SSD_GROUPS):
            hout_ref[s, g * HALF_INNER:(g + 1) * HALF_INNER, :] = ht[s, g].T


def _ssd_kernel(*refs, lin, valid):
    chains = [_ssd_chain(s, *refs, lin, valid) for s in range(SSD_SEQS)]
    while chains:
        alive = []
        for ch in chains:
            try:
                next(ch)
                alive.append(ch)
            except StopIteration:
                pass
        chains = alive


def _ssd(xbc, dtr, z, cprev, h0, cw, cb, dtb, alog, dexp, nw, batch, lin, valid):
    rows = xbc.shape[0]
    seq = rows // batch
    nc = seq // lin
    g = SSD_SEQS
    assert batch % g == 0
    blk = lambda b, c: (b, c, 0)
    per_b = lambda b, c: (b, 0, 0)
    const = lambda b, c: (0, 0)
    y, hout = pl.pallas_call(
        functools.partial(_ssd_kernel, lin=lin, valid=valid),
        grid=(batch // g, nc),
        in_specs=[
            pl.BlockSpec((g, lin, CONV_DIM), blk),
            pl.BlockSpec((g, lin, LANES), blk),
            pl.BlockSpec((g, lin, SSD_INNER), blk),
            pl.BlockSpec((g, SUBLANES, CONV_DIM), per_b),
            pl.BlockSpec((g, SSD_INNER, SSD_STATE), per_b),
            pl.BlockSpec((SUBLANES, CONV_DIM), const),
            pl.BlockSpec((1, CONV_DIM), const),
            pl.BlockSpec((1, LANES), const),
            pl.BlockSpec((1, LANES), const),
            pl.BlockSpec((1, SSD_INNER), const),
            pl.BlockSpec((1, SSD_INNER), const),
        ],
        out_specs=[
            pl.BlockSpec((g, lin, SSD_INNER), blk),
            pl.BlockSpec((g, SSD_INNER, SSD_STATE), per_b),
        ],
        out_shape=[
            jax.ShapeDtypeStruct((batch, seq, SSD_INNER), BF16),
            jax.ShapeDtypeStruct((batch, SSD_INNER, SSD_STATE), F32),
        ],
        scratch_shapes=[
            pltpu.VMEM((g, SUBLANES + SSD_CHUNK + SUBLANES, CONV_DIM), F32),
            pltpu.VMEM((g, SSD_GROUPS, SSD_STATE, HALF_INNER), F32),
            pltpu.VMEM((g, SSD_CHUNK, CONV_DIM), F32),
        ],
        compiler_params=pltpu.CompilerParams(
            dimension_semantics=("arbitrary", "arbitrary"), vmem_limit_bytes=VMEM_LIMIT),
        name="ssd",
    )(xbc.reshape(batch, seq, CONV_DIM), dtr.reshape(batch, seq, LANES),
      z.reshape(batch, seq, SSD_INNER), cprev, h0, cw, cb, dtb, alog, dexp, nw)
    return y.reshape(rows, SSD_INNER), hout


def _attn_prompt_kernel(q_ref, kt_ref, vt_ref, g_ref, o_ref,
                        bias_scr, kh_scr, va_scr, qt_scr, m_scr, acc_scr, ot_scr, s_scr, *, n_tab, n_sb_max):
    b = pl.program_id(0)
    i = pl.program_id(1)
    seq = kt_ref.shape[2]
    pair_w = 2 * Q_BLOCK
    pairs_per_kv = ATT_GQ // 2

    @pl.when((b == 0) & (i == 0))
    def _():
        r = lax.broadcasted_iota(jnp.int32, (K_SUPER, Q_BLOCK), 0)
        c = lax.broadcasted_iota(jnp.int32, (K_SUPER, Q_BLOCK), 1)
        for tb in range(n_tab):
            w = _multiplicity(tb * Q_BLOCK + c - r)
            bias_scr[tb] = jnp.where(w > 0.0, jnp.log2(jnp.maximum(w, 1.0)), NEG)
        va_scr[:, HEAD_DIM:, :] = jnp.ones((ATT_KV_HEADS, ONES_ROWS, seq), BF16)

    @pl.when(i == 0)
    def _():
        for kvh in range(ATT_KV_HEADS):
            hs = slice(kvh * HEAD_DIM, (kvh + 1) * HEAD_DIM)
            va_scr[kvh, 0:HEAD_DIM, :] = vt_ref[0, hs, :].astype(BF16)
        for cidx in range(seq // K_SUPER):
            rows = slice(cidx * K_SUPER, (cidx + 1) * K_SUPER)
            kc = kt_ref[0, :, rows].T
            for kvh in range(ATT_KV_HEADS):
                hs = slice(kvh * HEAD_DIM, (kvh + 1) * HEAD_DIM)
                kh_scr[kvh, rows, :] = kc[:, hs].astype(BF16)

    for kvh in range(ATT_KV_HEADS):
        qt_scr[kvh] = jnp.concatenate(
            [q_ref[0, (kvh * ATT_GQ + g) * HEAD_DIM:(kvh * ATT_GQ + g + 1) * HEAD_DIM, :]
             for g in range(ATT_GQ)], axis=1)
    m_scr[...] = jnp.full(m_scr.shape, NEG, F32)
    acc_scr[...] = jnp.zeros(acc_scr.shape, F32)
    par = i % Q_PER_SUPER
    j_last = i // Q_PER_SUPER
    n_sb = jnp.minimum(j_last + 1, n_sb_max)

    def body(dl, carry):
        start = pl.multiple_of((j_last - dl) * K_SUPER, K_SUPER)
        bias = bias_scr[par + Q_PER_SUPER * dl]
        bias2 = jnp.concatenate([bias, bias], axis=1)
        for kvh in range(ATT_KV_HEADS):
            s_scr[kvh] = _dot(kh_scr[kvh, pl.ds(start, K_SUPER), :], qt_scr[kvh])
        for kvh in range(ATT_KV_HEADS):
            vtb = va_scr[kvh, :, pl.ds(start, K_SUPER)]
            for pr in range(pairs_per_kv):
                u = kvh * pairs_per_kv + pr
                s = s_scr[kvh, :, pr * pair_w:(pr + 1) * pair_w] + bias2
                m_old = m_scr[u, 0:1, :]
                m_new = jnp.maximum(m_old, jnp.max(s, axis=0, keepdims=True))
                p = jnp.exp2(s - m_new).astype(BF16)
                acc_scr[u] = jnp.exp2(m_old - m_new) * acc_scr[u] + _dot(vtb, p)
                m_scr[u, 0:1, :] = m_new
        return carry

    lax.fori_loop(0, n_sb, body, 0)

    for u in range(ATT_HEADS // 2):
        acc = acc_scr[u]
        o = acc[0:HEAD_DIM] / acc[HEAD_DIM:HEAD_DIM + 1]
        for e in range(2):
            h = 2 * u + e
            ot_scr[h * HEAD_DIM:(h + 1) * HEAD_DIM, :] = o[:, e * Q_BLOCK:(e + 1) * Q_BLOCK]
    o_ref[...] = (ot_scr[...].T * g_ref[...]).astype(BF16)


def _attn_prompt(qt, kt, vt, gate, batch, seq):
    rows = batch * seq
    nqb = seq // Q_BLOCK
    assert seq % K_SUPER == 0
    n_sb_max = min(seq // K_SUPER, W_MAX // K_SUPER + 1)
    n_tab = Q_PER_SUPER * n_sb_max
    qblk = lambda b, i: (b * nqb + i, 0)
    per_b = lambda b, i: (b, 0, 0)
    return pl.pallas_call(
        functools.partial(_attn_prompt_kernel, n_tab=n_tab, n_sb_max=n_sb_max),
        grid=(batch, nqb),
        in_specs=[
            pl.BlockSpec((1, ATT_INNER, Q_BLOCK), lambda b, i: (b, 0, i)),
            pl.BlockSpec((1, KV_DIM, seq), per_b),
            pl.BlockSpec((1, KV_DIM, seq), per_b),
            pl.BlockSpec((Q_BLOCK, ATT_INNER), qblk),
        ],
        out_specs=pl.BlockSpec((Q_BLOCK, ATT_INNER), qblk),
        out_shape=jax.ShapeDtypeStruct((rows, ATT_INNER), BF16),
        scratch_shapes=[
            pltpu.VMEM((n_tab, K_SUPER, Q_BLOCK), F32),
            pltpu.VMEM((ATT_KV_HEADS, seq, HEAD_DIM), BF16),
            pltpu.VMEM((ATT_KV_HEADS, ACC_ROWS, seq), BF16),
            pltpu.VMEM((ATT_KV_HEADS, HEAD_DIM, ATT_GQ * Q_BLOCK), BF16),
            pltpu.VMEM((ATT_HEADS // 2, SUBLANES, 2 * Q_BLOCK), F32),
            pltpu.VMEM((ATT_HEADS // 2, ACC_ROWS, 2 * Q_BLOCK), F32),
            pltpu.VMEM((ATT_INNER, Q_BLOCK), F32),
            pltpu.VMEM((ATT_KV_HEADS, K_SUPER, ATT_GQ * Q_BLOCK), F32),
        ],
        compiler_params=pltpu.CompilerParams(
            dimension_semantics=("arbitrary", "arbitrary"), vmem_limit_bytes=VMEM_LIMIT),
        name="attn_prompt",
    )(qt, kt, vt, gate)


def _attn_sample_kernel(q_ref, knt_ref, vnt_ref, g_ref, ckt_ref, cvt_ref, o_ref, kot_ref, vot_ref, *, ts):
    win = ckt_ref.shape[2]
    tp = SAMPLE_PAD
    rows = ATT_GQ * tp
    t_c = lax.broadcasted_iota(jnp.int32, (rows, win), 0) % tp
    j_c = lax.broadcasted_iota(jnp.int32, (rows, win), 1)
    w_c = _multiplicity(win + t_c - j_c)
    t_n = lax.broadcasted_iota(jnp.int32, (rows, tp), 0) % tp
    j_n = lax.broadcasted_iota(jnp.int32, (rows, tp), 1)
    w_n = jnp.where(j_n < ts, _multiplicity(t_n - j_n), 0.0)
    for kvh in range(ATT_KV_HEADS):
        c0 = kvh * ATT_GQ * HEAD_DIM
        qh = jnp.concatenate(
            [q_ref[:, c0 + g * HEAD_DIM:c0 + (g + 1) * HEAD_DIM] for g in range(ATT_GQ)], axis=0)
        hs = slice(kvh * HEAD_DIM, (kvh + 1) * HEAD_DIM)
        s_c = jnp.where(w_c > 0.0, _dot(qh, ckt_ref[0, hs, :].astype(BF16)), NEG)
        s_n = jnp.where(w_n > 0.0, _dot(qh, knt_ref[0, hs, :].astype(BF16)), NEG)
        m = jnp.maximum(jnp.max(s_c, axis=-1, keepdims=True), jnp.max(s_n, axis=-1, keepdims=True))
        p_c = jnp.exp2(s_c - m) * w_c
        p_n = jnp.exp2(s_n - m) * w_n
        den = jnp.sum(p_c, axis=-1, keepdims=True) + jnp.sum(p_n, axis=-1, keepdims=True)
        num = (_dot_nt(p_c.astype(BF16), cvt_ref[0, hs, :].astype(BF16))
               + _dot_nt(p_n.astype(BF16), vnt_ref[0, hs, :].astype(BF16)))
        o = num / den
        o = jnp.concatenate([o[g * tp:(g + 1) * tp] for g in range(ATT_GQ)], axis=1)
        csl = slice(c0, c0 + ATT_GQ * HEAD_DIM)
        o_ref[:, csl] = (o * g_ref[:, csl]).astype(BF16)
    kot_ref[0, :, 0:win - ts] = ckt_ref[0, :, ts:win]
    kot_ref[0, :, win - ts:win] = knt_ref[0, :, 0:ts]
    vot_ref[0, :, 0:win - ts] = cvt_ref[0, :, ts:win]
    vot_ref[0, :, win - ts:win] = vnt_ref[0, :, 0:ts]


def _attn_sample(q, knt, vnt, gate, ckt, cvt, ts):
    batch, _, win = ckt.shape
    tp = SAMPLE_PAD
    blk = lambda b: (b, 0)
    per_b = lambda b: (b, 0, 0)
    return pl.pallas_call(
        functools.partial(_attn_sample_kernel, ts=ts),
        grid=(batch,),
        in_specs=[
            pl.BlockSpec((tp, ATT_INNER), blk),
            pl.BlockSpec((1, KV_DIM, tp), per_b),
            pl.BlockSpec((1, KV_DIM, tp), per_b),
            pl.BlockSpec((tp, ATT_INNER), blk),
            pl.BlockSpec((1, KV_DIM, win), per_b),
            pl.BlockSpec((1, KV_DIM, win), per_b),
        ],
        out_specs=[
            pl.BlockSpec((tp, ATT_INNER), blk),
            pl.BlockSpec((1, KV_DIM, win), per_b),
            pl.BlockSpec((1, KV_DIM, win), per_b),
        ],
        out_shape=[
            jax.ShapeDtypeStruct((batch * tp, ATT_INNER), BF16),
            jax.ShapeDtypeStruct((batch, KV_DIM, win), F32),
            jax.ShapeDtypeStruct((batch, KV_DIM, win), F32),
        ],
        compiler_params=pltpu.CompilerParams(
            dimension_semantics=("arbitrary",), vmem_limit_bytes=VMEM_LIMIT),
        name="attn_sample",
    )(q, knt, vnt, gate, ckt, cvt)


def _outproj_kernel(x_ref, ys_ref, ya_ref, w_ref, o_ref):
    o_ref[...] = (x_ref[...] + _dot(ys_ref[...], w_ref[0:SSD_INNER, :])
                  + _dot(ya_ref[...], w_ref[SSD_INNER:, :]))


def _out_proj(x2d, y_ssd, y_att, w_o, tm):
    rows = x2d.shape[0]
    row = lambda i: (i, 0)
    const = lambda i: (0, 0)
    return pl.pallas_call(
        _outproj_kernel,
        grid=(rows // tm,),
        in_specs=[
            pl.BlockSpec((tm, D_MODEL), row),
            pl.BlockSpec((tm, SSD_INNER), row),
            pl.BlockSpec((tm, ATT_INNER), row),
            pl.BlockSpec((SSD_INNER + ATT_INNER, D_MODEL), const, pipeline_mode=pl.Buffered(1)),
        ],
        out_specs=pl.BlockSpec((tm, D_MODEL), row),
        out_shape=jax.ShapeDtypeStruct((rows, D_MODEL), F32),
        compiler_params=pltpu.CompilerParams(
            dimension_semantics=("arbitrary",), vmem_limit_bytes=VMEM_LIMIT),
        name="out_proj",
    )(x2d, y_ssd, y_att, w_o)


def _wconv_out_kernel(x_ref, ys_ref, ya_ref, w_ref, wb_ref, o_ref):
    j = pl.program_id(0)
    wb = w_ref[...].astype(BF16)
    wb_ref[...] = wb
    y = jnp.where(j < SSD_INNER // WO_CHUNK, ys_ref[...], ya_ref[...])
    part = _dot(y, wb)

    @pl.when(j == 0)
    def _():
        o_ref[...] = x_ref[...] + part

    @pl.when(j > 0)
    def _():
        o_ref[...] += part


def _wconv_out(x2d, y_ssd, y_att, w_f32):
    rows = x2d.shape[0]
    n_ssd = SSD_INNER // WO_CHUNK
    n_chunks = (SSD_INNER + ATT_INNER) // WO_CHUNK
    const = lambda j: (0, 0)
    return pl.pallas_call(
        _wconv_out_kernel,
        grid=(n_chunks,),
        in_specs=[
            pl.BlockSpec((rows, D_MODEL), const),
            pl.BlockSpec((rows, WO_CHUNK), lambda j: (0, jnp.minimum(j, n_ssd - 1))),
            pl.BlockSpec((rows, WO_CHUNK), lambda j: (0, jnp.maximum(j - n_ssd, 0))),
            pl.BlockSpec((WO_CHUNK, D_MODEL), lambda j: (j, 0)),
        ],
        out_specs=[
            pl.BlockSpec((WO_CHUNK, D_MODEL), lambda j: (j, 0)),
            pl.BlockSpec((rows, D_MODEL), const),
        ],
        out_shape=[
            jax.ShapeDtypeStruct((SSD_INNER + ATT_INNER, D_MODEL), BF16),
            jax.ShapeDtypeStruct((rows, D_MODEL), F32),
        ],
        compiler_params=pltpu.CompilerParams(
            dimension_semantics=("arbitrary",), vmem_limit_bytes=VMEM_LIMIT),
        name="wconv_out",
    )(x2d, y_ssd, y_att, w_f32)


def _rope_tables(pos):
    n = pos.shape[0]
    half = ROT_DIM // 2
    inv = ROPE_THETA ** (-np.arange(0, ROT_DIM, 2, dtype=np.float64) / ROT_DIM)
    ang = pos.astype(np.float64)[:, None] * inv[None, :]
    cos, sin = np.cos(ang), np.sin(ang)
    rest = HEAD_DIM - ROT_DIM
    zh = np.zeros((n, half))
    cos_h = np.concatenate([cos, cos, np.ones((n, rest))], axis=1)
    sa_h = np.concatenate([zh, sin, np.zeros((n, rest))], axis=1)
    sb_h = np.concatenate([-sin, zh, np.zeros((n, rest))], axis=1)
    rep = LANES // HEAD_DIM
    return tuple(jnp.asarray(np.tile(t, (1, rep)), dtype=F32) for t in (cos_h, sa_h, sb_h))


def _lane_pad(v, n=LANES):
    return jnp.pad(v, (0, n - v.shape[0])).reshape(1, n)


def _to_cache(xt, batch, seq):
    return xt.reshape(1, batch, ATT_KV_HEADS, HEAD_DIM, seq).transpose(0, 1, 4, 2, 3)


def kernel(x_prompt, x_sample, cache_k, cache_v, state_conv, state_ssm, norm_w, w_in, conv_w,
           conv_b, dt_bias, a_log, d_skip, ssd_norm_w, q_norm_w, k_norm_w, w_out):
    bp, tp_, _ = x_prompt.shape
    bs, ts, _ = x_sample.shape
    depth = w_in.shape[0]
    assert depth == 1 and tp_ % SSD_CHUNK == 0 and ts <= SAMPLE_PAD and ts >= CONV_W - 1
    l = 0
    win = cache_k.shape[2]

    nw = norm_w[l].reshape(1, D_MODEL)
    rep = LANES // HEAD_DIM
    qnw = jnp.tile(q_norm_w[l], rep).reshape(1, LANES)
    knw = jnp.tile(k_norm_w[l], rep).reshape(1, LANES)
    cw = jnp.pad(conv_w[l], ((0, SUBLANES - CONV_W), (0, 0)))
    cb = conv_b[l].reshape(1, CONV_DIM)
    dtb = _lane_pad(dt_bias[l])
    alog = _lane_pad(a_log[l])
    dexp = jnp.repeat(d_skip[l], SSD_HEADDIM).reshape(1, SSD_INNER)
    snw = ssd_norm_w[l].reshape(1, SSD_INNER)

    pad = SAMPLE_PAD
    xs2 = jnp.pad(x_sample, ((0, 0), (0, pad - ts), (0, 0))).reshape(bs * pad, D_MODEL)
    tabs_s = _rope_tables(PAST_LEN + np.arange(bs * pad) % pad)
    w_t, proj_s = _wconv_in(xs2, nw, jnp.swapaxes(w_in[l], 0, 1))
    zs, xbc, dtr, q, kt, vt, gs, _ = _sections(proj_s, qnw, knw, tabs_s)
    cprev = jnp.pad(state_conv[l], ((0, 0), (SUBLANES - (CONV_W - 1), 0), (0, 0)))
    y_ssd, h_s = _ssd(xbc, dtr, zs, cprev, state_ssm[l].reshape(bs, SSD_INNER, SSD_STATE),
                      cw, cb, dtb, alog, dexp, snw, bs, pad, ts)
    knt = kt.reshape(KV_DIM, bs, pad).transpose(1, 0, 2)
    vnt = vt.reshape(KV_DIM, bs, pad).transpose(1, 0, 2)
    ckt = cache_k[l].transpose(0, 2, 3, 1).reshape(bs, KV_DIM, win)
    cvt = cache_v[l].transpose(0, 2, 3, 1).reshape(bs, KV_DIM, win)
    y_att, kot, vot = _attn_sample(q, knt, vnt, gs, ckt, cvt, ts)
    w_o, y_s = _wconv_out(xs2, y_ssd, y_att, w_out[l])
    y_s = y_s.reshape(bs, pad, D_MODEL)[:, :ts]
    k_s = _to_cache(kot, bs, win)
    v_s = _to_cache(vot, bs, win)
    c_s = xbc.reshape(bs, pad, CONV_DIM)[:, ts - (CONV_W - 1):ts][None]
    h_s = h_s.reshape(1, bs, SSD_HEADS, SSD_HEADDIM, SSD_STATE)

    tm = TM_IN
    xp2 = x_prompt.reshape(bp * tp_, D_MODEL)
    tabs = _rope_tables(np.arange(tp_))
    zs, xbc, dtr, q, kt, vt, gs, ctail = _in_proj(xp2, nw, w_t, qnw, knw, tabs, tm, tp_)
    y_ssd, h_p = _ssd(xbc, dtr, zs, jnp.zeros((bp, SUBLANES, CONV_DIM), F32),
                      jnp.zeros((bp, SSD_INNER, SSD_STATE), F32),
                      cw, cb, dtb, alog, dexp, snw, bp, SSD_CHUNK, SSD_CHUNK)
    y_att = _attn_prompt(q, kt, vt, gs, bp, tp_)
    y_p = _out_proj(xp2, y_ssd, y_att, w_o, TM_OUT).reshape(bp, tp_, D_MODEL)
    keep = min(W_MAX, tp_)
    k_p = _to_cache(kt, bp, tp_)[:, :, tp_ - keep:]
    v_p = _to_cache(vt, bp, tp_)[:, :, tp_ - keep:]
    c_p = ctail[:, SUBLANES - (CONV_W - 1):][None]
    h_p = h_p.reshape(1, bp, SSD_HEADS, SSD_HEADDIM, SSD_STATE)

    return (y_p, y_s, k_p, v_p, c_p, h_p, k_s, v_s, c_s, h_s)
```

```python
import functools

import jax
import jax.numpy as jnp
import numpy as np
from jax import lax
from jax.experimental import pallas as pl
from jax.experimental.pallas import tpu as pltpu

F32 = jnp.float32
BF16 = jnp.bfloat16

D_MODEL = 2048
SSD_HEADS = 16
SSD_HEADDIM = 64
SSD_INNER = SSD_HEADS * SSD_HEADDIM
SSD_GROUPS = 2
SSD_STATE = 128
CONV_W = 4
CONV_DIM = SSD_INNER + 2 * SSD_GROUPS * SSD_STATE
SSD_CHUNK = 128
ATT_HEADS = 16
ATT_KV_HEADS = 4
HEAD_DIM = 64
ATT_GQ = ATT_HEADS // ATT_KV_HEADS
ATT_INNER = ATT_HEADS * HEAD_DIM
KV_DIM = ATT_KV_HEADS * HEAD_DIM
ROT_DIM = HEAD_DIM // 4
ROPE_THETA = 500000.0
DILATED_BRANCHES = ((128, 1), (512, 4), (2048, 16))
W_MAX = 2048
PAST_LEN = 16384
EPS = 1e-6

LANES = 128
SUBLANES = 8
Q_BLOCK = 256
K_SUPER = 256
Q_PER_SUPER = K_SUPER // Q_BLOCK
ONES_ROWS = 16
ACC_ROWS = HEAD_DIM + ONES_ROWS
CONV_COLS = 256
TM_IN = 512
TM_OUT = 512
SSD_SEQS = 2
SAMPLE_PAD = 16
NEG = -1e30
VMEM_LIMIT = 58 * 1024 * 1024

Z0 = 0
X0 = Z0 + SSD_INNER
DT0 = X0 + CONV_DIM
Q0 = DT0 + SSD_HEADS
K0 = Q0 + ATT_INNER
V0 = K0 + KV_DIM
G0 = V0 + KV_DIM
W_ROWS = G0 + ATT_INNER
W_CHUNK = 1024
WO_CHUNK = 512
HALF_INNER = SSD_INNER // SSD_GROUPS
LOG2E = 1.4426950408889634
Q_SCALE = HEAD_DIM ** -0.5 * LOG2E


def _dot(a, b):
    return jnp.dot(a, b, preferred_element_type=F32)


def _dot_nt(a, b):
    return lax.dot_general(a, b, (((1,), (1,)), ((), ())), preferred_element_type=F32)


def _split3(x):
    hi = x.astype(BF16)
    r1 = x - hi.astype(F32)
    mid = r1.astype(BF16)
    lo = (r1 - mid.astype(F32)).astype(BF16)
    return hi, mid, lo


def _dot_exact_rhs(x, m):
    hi, mid, lo = _split3(x)
    return _dot(hi, m) + _dot(mid, m) + _dot(lo, m)


def _dot_wide_rhs(x, m):
    hi = x.astype(BF16)
    lo = (x - hi.astype(F32)).astype(BF16)
    return _dot(hi, m) + _dot(lo, m)


def _dot_exact_lhs(m, x):
    hi, mid, lo = _split3(x)
    return _dot(m, hi) + _dot(m, mid) + _dot(m, lo)


def _silu(x):
    hx = 0.5 * x
    return hx * jnp.tanh(hx) + hx


def _multiplicity(d):
    w = jnp.zeros(d.shape, F32)
    for window, dil in DILATED_BRANCHES:
        hit = (d >= 0) & (d <= window) & (lax.rem(d, dil) == 0)
        w = w + jnp.where(hit, 1.0, 0.0)
    return w


def _norm_rope(y, nw, cos, sa, sb, scale):
    lane = lax.broadcasted_iota(jnp.int32, (1, LANES), 1)
    first = lane < HEAD_DIM
    y2 = y * y
    s_lo = jnp.sum(jnp.where(first, y2, 0.0), axis=-1, keepdims=True)
    s_hi = jnp.sum(jnp.where(first, 0.0, y2), axis=-1, keepdims=True)
    ms = jnp.where(first, s_lo, s_hi) * (1.0 / HEAD_DIM)
    yn = y * lax.rsqrt(ms + EPS) * nw
    half = ROT_DIM // 2
    rot = yn * cos + pltpu.roll(yn, half, 1) * sa + pltpu.roll(yn, LANES - half, 1) * sb
    return rot * scale


def _causal_conv_silu(load, cw_ref, cb_ref, store):
    for c0 in range(0, CONV_DIM, CONV_COLS):
        cs = slice(c0, c0 + CONV_COLS)
        xe = load(cs)
        x2 = pltpu.roll(xe, 2, 0)
        even = cw_ref[3:4, cs] * xe + cw_ref[1:2, cs] * x2
        odd = cw_ref[2:3, cs] * xe + cw_ref[0:1, cs] * x2
        store(cs, _silu((even + pltpu.roll(odd, 1, 0))[SUBLANES:, :] + cb_ref[:, cs]))


def _pre_norm(x_ref, nw_ref):
    x = x_ref[...]
    ms = jnp.mean(x * x, axis=-1, keepdims=True)
    return (x * lax.rsqrt(ms + EPS) * nw_ref[...]).astype(BF16)


def _inproj_kernel(x_ref, nw_ref, w_ref, *refs):
    hn = _pre_norm(x_ref, nw_ref)

    def proj(a, b):
        return _dot_nt(hn, w_ref[a:b, :])

    _emit_sections(proj, x_ref.shape[0], *refs)


def _emit_sections(proj, tm, qnw_ref, knw_ref, cos_ref, sa_ref, sb_ref,
                   zs_ref, xbc_ref, dt_ref, q_ref, kt_ref, vt_ref, gs_ref, ctail_ref):
    cos, sa, sb = cos_ref[...], sa_ref[...], sb_ref[...]
    q = proj(Q0, K0)
    for c in range(ATT_INNER // LANES):
        sl = slice(c * LANES, (c + 1) * LANES)
        qn = _norm_rope(q[:, sl], qnw_ref[...], cos, sa, sb, Q_SCALE)
        if len(q_ref.shape) == 3:
            q_ref[0, sl, :] = qn.T.astype(BF16)
        else:
            q_ref[:, sl] = qn.astype(BF16)
    k = proj(K0, V0)
    kn = [_norm_rope(k[:, c * LANES:(c + 1) * LANES], knw_ref[...], cos, sa, sb, 1.0)
          for c in range(KV_DIM // LANES)]
    kt_ref[0] = jnp.concatenate(kn, axis=1).T

    xbc = proj(X0, DT0)
    xbc_ref[...] = xbc
    ctail_ref[0] = xbc[tm - SUBLANES:tm, :]
    vt_ref[0] = proj(V0, G0).T
    zs_ref[...] = _silu(proj(Z0, X0))
    gs_ref[...] = _silu(proj(G0, W_ROWS))
    dt_ref[...] = proj(DT0, DT0 + LANES)


def _in_proj(x2d, nw, w_t, qnw, knw, tables, tm, seq):
    rows = x2d.shape[0]
    const = lambda i: (0, 0)
    row = lambda i: (i, 0)
    sec_in, out_specs, out_shape = _section_specs(rows, tm, seq, tables[0].shape[0] // tm, True)
    return pl.pallas_call(
        _inproj_kernel,
        grid=(rows // tm,),
        in_specs=[
            pl.BlockSpec((tm, D_MODEL), row),
            pl.BlockSpec((1, D_MODEL), const),
            pl.BlockSpec((W_ROWS, D_MODEL), const, pipeline_mode=pl.Buffered(1)),
        ] + sec_in,
        out_specs=out_specs,
        out_shape=out_shape,
        compiler_params=pltpu.CompilerParams(
            dimension_semantics=("arbitrary",), vmem_limit_bytes=VMEM_LIMIT),
        name="in_proj",
    )(x2d, nw, w_t, qnw, knw, *tables)


def _section_specs(rows, tm, seq, period, q_transposed):
    per_seq = seq // tm
    const = lambda i: (0, 0)
    row = lambda i: (i, 0)
    tab = lambda i: (i % period, 0)
    tr = lambda i: (i // per_seq, 0, i % per_seq)
    slab = lambda i: (i // per_seq, 0, 0)
    in_specs = [pl.BlockSpec((1, LANES), const)] * 2 + [pl.BlockSpec((tm, LANES), tab)] * 3
    row_outs = {0: (SSD_INNER, F32), 1: (CONV_DIM, F32), 2: (LANES, F32), 3: (ATT_INNER, BF16),
                6: (ATT_INNER, F32)}
    out_specs, out_shape = [], []
    for idx in range(7):
        if idx == 3 and q_transposed:
            out_specs.append(pl.BlockSpec((1, ATT_INNER, tm), tr))
            out_shape.append(jax.ShapeDtypeStruct((rows // seq, ATT_INNER, seq), BF16))
        elif idx in row_outs:
            n, dt = row_outs[idx]
            out_specs.append(pl.BlockSpec((tm, n), row))
            out_shape.append(jax.ShapeDtypeStruct((rows, n), dt))
        else:
            out_specs.append(pl.BlockSpec((1, KV_DIM, tm), tr))
            out_shape.append(jax.ShapeDtypeStruct((rows // seq, KV_DIM, seq), F32))
    out_specs.append(pl.BlockSpec((1, SUBLANES, CONV_DIM), slab))
    out_shape.append(jax.ShapeDtypeStruct((rows // seq, SUBLANES, CONV_DIM), F32))
    return in_specs, out_specs, out_shape


def _wconv_in_kernel(x_ref, nw_ref, w_ref, wb_ref, p_ref, hn_scr):
    @pl.when(pl.program_id(0) == 0)
    def _():
        hn_scr[...] = _pre_norm(x_ref, nw_ref)

    row = pl.program_id(0) * W_CHUNK + lax.broadcasted_iota(jnp.int32, (W_CHUNK, 1), 0)
    wb = jnp.where(row < W_ROWS, w_ref[...], 0.0).astype(BF16)
    wb_ref[...] = wb
    p_ref[0] = _dot_nt(hn_scr[...], wb)


def _wconv_in(x2d, nw, w_f32):
    rows = x2d.shape[0]
    n_chunks = pl.cdiv(W_ROWS, W_CHUNK)
    const = lambda j: (0, 0)
    return pl.pallas_call(
        _wconv_in_kernel,
        grid=(n_chunks,),
        in_specs=[
            pl.BlockSpec((rows, D_MODEL), const),
            pl.BlockSpec((1, D_MODEL), const),
            pl.BlockSpec((W_CHUNK, D_MODEL), lambda j: (j, 0)),
        ],
        out_specs=[
            pl.BlockSpec((W_CHUNK, D_MODEL), lambda j: (j, 0)),
            pl.BlockSpec((1, rows, W_CHUNK), lambda j: (j, 0, 0)),
        ],
        out_shape=[
            jax.ShapeDtypeStruct((W_ROWS, D_MODEL), BF16),
            jax.ShapeDtypeStruct((n_chunks, rows, W_CHUNK), F32),
        ],
        scratch_shapes=[pltpu.VMEM((rows, D_MODEL), BF16)],
        compiler_params=pltpu.CompilerParams(
            dimension_semantics=("arbitrary",), vmem_limit_bytes=VMEM_LIMIT),
        name="wconv_in",
    )(x2d, nw, w_f32)


def _sections_kernel(p_ref, *refs):
    *refs, tail_scr = refs
    pfull = jnp.concatenate([p_ref[j] for j in range(p_ref.shape[0])], axis=1)
    tail_scr[...] = pfull[:, Q0:W_ROWS]

    def proj(a, b):
        if a >= Q0:
            return tail_scr[:, a - Q0:b - Q0]
        return pfull[:, a:b]

    _emit_sections(proj, p_ref.shape[1], *refs)


def _sections(p, qnw, knw, tables):
    rows = p.shape[1]
    sec_in, out_specs, out_shape = _section_specs(rows, rows, rows, 1, False)
    return pl.pallas_call(
        _sections_kernel,
        grid=(1,),
        in_specs=[pl.BlockSpec(p.shape, lambda i: (0, 0, 0))] + sec_in,
        out_specs=out_specs,
        out_shape=out_shape,
        scratch_shapes=[pltpu.VMEM((rows, W_ROWS - Q0), F32)],
        compiler_params=pltpu.CompilerParams(
            dimension_semantics=("arbitrary",), vmem_limit_bytes=VMEM_LIMIT),
        name="sections",
    )(p, qnw, knw, *tables)


def _ssd_chain(s, xbc_ref, dtr_ref, z_ref, cprev_ref, h0_ref, cw_ref, cb_ref, dtb_ref, alog_ref,
               dexp_ref, nw_ref, y_ref, hout_ref, xext, ht, xc_scr, lin, valid):
    L = SSD_CHUNK
    c = pl.program_id(1)
    last = pl.num_programs(1) - 1

    @pl.when(c == 0)
    def _():
        xext[s, 0:SUBLANES, :] = cprev_ref[s]
        for g in range(SSD_GROUPS):
            ht[s, g] = h0_ref[s, g * HALF_INNER:(g + 1) * HALF_INNER, :].T

    def pad_rows(v):
        if lin == L:
            return v
        return jnp.concatenate([v, jnp.zeros((L - lin, v.shape[1]), v.dtype)], axis=0)

    lane = lax.broadcasted_iota(jnp.int32, (1, LANES), 1)
    rowid = lax.broadcasted_iota(jnp.int32, (L, 1), 0)
    dt = jax.nn.softplus(pad_rows(dtr_ref[s]) + dtb_ref[...])
    dt = jnp.where((lane < SSD_HEADS) & (rowid < valid), dt, 0.0)
    dta = dt * (-jnp.exp(alog_ref[...]) * LOG2E)

    r2 = lax.broadcasted_iota(jnp.int32, (L, L), 0)
    c2 = lax.broadcasted_iota(jnp.int32, (L, L), 1)
    tri = r2 >= c2
    tri_b = jnp.where(tri, 1.0, 0.0).astype(BF16)
    cum = _dot_exact_lhs(tri_b, dta)
    cum_last = cum[L - 1:L, :]
    yield

    er = lax.broadcasted_iota(jnp.int32, (LANES, SSD_INNER), 0)
    ec = lax.broadcasted_iota(jnp.int32, (LANES, SSD_INNER), 1)
    expand = jnp.where(ec // SSD_HEADDIM == er, 1.0, 0.0).astype(BF16)
    ecum = _dot_wide_rhs(jnp.exp2(cum), expand)
    wexp = _dot_wide_rhs(jnp.exp2(cum_last - cum) * dt, expand)
    cd = _dot_exact_rhs(jnp.broadcast_to(jnp.exp2(cum_last), (SUBLANES, LANES)), expand)[0:1, :]
    cum_t = cum.T
    dt_t = dt.T
    yield

    if lin < L:
        xext[s, SUBLANES + lin:SUBLANES + L, :] = jnp.zeros((L - lin, CONV_DIM), F32)
    xext[s, SUBLANES:SUBLANES + lin, :] = xbc_ref[s]

    def load(cs):
        return xext[s, 0:SUBLANES + L, cs]

    def store(cs, v):
        xc_scr[s, :, cs] = v

    _causal_conv_silu(load, cw_ref, cb_ref, store)
    xext[s, 0:SUBLANES, :] = xext[s, L:L + SUBLANES, :]
    xc = xc_scr.at[s]
    xs = xc[:, :SSD_INNER]
    bm = xc[:, SSD_INNER:SSD_INNER + SSD_GROUPS * SSD_STATE]
    cm = xc[:, SSD_INNER + SSD_GROUPS * SSD_STATE:]
    bm_b = bm.astype(BF16)
    cm_b = cm.astype(BF16)
    xs_b = xs.astype(BF16)

    y_off = jnp.concatenate(
        [_dot(cm_b[:, g * SSD_STATE:(g + 1) * SSD_STATE], ht[s, g].astype(BF16))
         for g in range(SSD_GROUPS)], axis=1) * ecum
    cbs = [_dot_nt(cm_b[:, g * SSD_STATE:(g + 1) * SSD_STATE], bm_b[:, g * SSD_STATE:(g + 1) * SSD_STATE])
           for g in range(SSD_GROUPS)]
    yield

    lane_half = lane // HEAD_DIM
    heads_per_group = SSD_HEADS // SSD_GROUPS
    y_parts = []
    for g in range(SSD_GROUPS):
        for pr in range(heads_per_group // 2):
            col0 = g * HALF_INNER + pr * LANES
            xp = xs_b[:, col0:col0 + LANES]
            yp = jnp.zeros((L, LANES), F32)
            for e in range(2):
                hh = g * heads_per_group + pr * 2 + e
                decay = jnp.exp2(cum[:, hh:hh + 1] - cum_t[hh:hh + 1, :])
                sc = jnp.where(tri, cbs[g] * decay, 0.0) * dt_t[hh:hh + 1, :]
                xm = jnp.where(lane_half == e, xp, jnp.zeros_like(xp))
                yp = yp + _dot(sc.astype(BF16), xm)
            y_parts.append(yp)
            if pr % 2 == 1:
                yield
    y_diag = jnp.concatenate(y_parts, axis=1)

    xw = (xs * wexp).astype(BF16)
    for g in range(SSD_GROUPS):
        sl = slice(g * HALF_INNER, (g + 1) * HALF_INNER)
        bm_t = bm[:, g * SSD_STATE:(g + 1) * SSD_STATE].T.astype(BF16)
        ht[s, g] = ht[s, g] * cd[:, sl] + _dot(bm_t, xw[:, sl])
    yield

    y = (y_diag + y_off + dexp_ref[...] * xs) * pad_rows(z_ref[s])
    yn = []
    for g in range(SSD_GROUPS):
        yg = y[:, g * HALF_INNER:(g + 1) * HALF_INNER]
        ms = jnp.mean(yg * yg, axis=-1, keepdims=True)
        yn.append(yg * lax.rsqrt(ms + EPS))
    yn = jnp.concatenate(yn, axis=1) * nw_ref[...]
    y_ref[s] = yn[:lin].astype(BF16)

    @pl.when(c == last)
    def _():
        for g in range(SSD_GROUPS):
            hout_ref[s, g * HALF_INNER:(g + 1) * HALF_INNER, :] = ht[s, g].T


def _ssd_kernel(*refs, lin, valid):
    chains = [_ssd_chain(s, *refs, lin, valid) for s in range(SSD_SEQS)]
    while chains:
        alive = []
        for ch in chains:
            try:
                next(ch)
                alive.append(ch)
            except StopIteration:
                pass
        chains = alive


def _ssd(xbc, dtr, z, cprev, h0, cw, cb, dtb, alog, dexp, nw, batch, lin, valid):
    rows = xbc.shape[0]
    seq = rows // batch
    nc = seq // lin
    g = SSD_SEQS
    assert batch % g == 0
    blk = lambda b, c: (b, c, 0)
    per_b = lambda b, c: (b, 0, 0)
    const = lambda b, c: (0, 0)
    y, hout = pl.pallas_call(
        functools.partial(_ssd_kernel, lin=lin, valid=valid),
        grid=(batch // g, nc),
        in_specs=[
            pl.BlockSpec((g, lin, CONV_DIM), blk),
            pl.BlockSpec((g, lin, LANES), blk),
            pl.BlockSpec((g, lin, SSD_INNER), blk),
            pl.BlockSpec((g, SUBLANES, CONV_DIM), per_b),
            pl.BlockSpec((g, SSD_INNER, SSD_STATE), per_b),
            pl.BlockSpec((SUBLANES, CONV_DIM), const),
            pl.BlockSpec((1, CONV_DIM), const),
            pl.BlockSpec((1, LANES), const),
            pl.BlockSpec((1, LANES), const),
            pl.BlockSpec((1, SSD_INNER), const),
            pl.BlockSpec((1, SSD_INNER), const),
        ],
        out_specs=[
            pl.BlockSpec((g, lin, SSD_INNER), blk),
            pl.BlockSpec((g, SSD_INNER, SSD_STATE), per_b),
        ],
        out_shape=[
            jax.ShapeDtypeStruct((batch, seq, SSD_INNER), BF16),
            jax.ShapeDtypeStruct((batch, SSD_INNER, SSD_STATE), F32),
        ],
        scratch_shapes=[
            pltpu.VMEM((g, SUBLANES + SSD_CHUNK + SUBLANES, CONV_DIM), F32),
            pltpu.VMEM((g, SSD_GROUPS, SSD_STATE, HALF_INNER), F32),
            pltpu.VMEM((g, SSD_CHUNK, CONV_DIM), F32),
        ],
        compiler_params=pltpu.CompilerParams(
            dimension_semantics=("arbitrary", "arbitrary"), vmem_limit_bytes=VMEM_LIMIT),
        name="ssd",
    )(xbc.reshape(batch, seq, CONV_DIM), dtr.reshape(batch, seq, LANES),
      z.reshape(batch, seq, SSD_INNER), cprev, h0, cw, cb, dtb, alog, dexp, nw)
    return y.reshape(rows, SSD_INNER), hout


def _attn_prompt_kernel(q_ref, kt_ref, vt_ref, g_ref, o_ref,
                        bias_scr, kh_scr, va_scr, qt_scr, m_scr, acc_scr, ot_scr, s_scr, s2_scr, *, n_tab, n_sb_max):
    b = pl.program_id(0)
    i = pl.program_id(1)
    seq = kt_ref.shape[2]
    pair_w = 2 * Q_BLOCK
    pairs_per_kv = ATT_GQ // 2

    @pl.when((b == 0) & (i == 0))
    def _():
        r = lax.broadcasted_iota(jnp.int32, (K_SUPER, Q_BLOCK), 0)
        c = lax.broadcasted_iota(jnp.int32, (K_SUPER, Q_BLOCK), 1)
        for tb in range(n_tab):
            w = _multiplicity(tb * Q_BLOCK + c - r)
            bias_scr[tb] = jnp.where(w > 0.0, jnp.log2(jnp.maximum(w, 1.0)), NEG)
        va_scr[:, HEAD_DIM:, :] = jnp.ones((ATT_KV_HEADS, ONES_ROWS, seq), BF16)

    @pl.when(i == 0)
    def _():
        for kvh in range(ATT_KV_HEADS):
            hs = slice(kvh * HEAD_DIM, (kvh + 1) * HEAD_DIM)
            va_scr[kvh, 0:HEAD_DIM, :] = vt_ref[0, hs, :].astype(BF16)
        for cidx in range(seq // K_SUPER):
            rows = slice(cidx * K_SUPER, (cidx + 1) * K_SUPER)
            kc = kt_ref[0, :, rows].T
            for kvh in range(ATT_KV_HEADS):
                hs = slice(kvh * HEAD_DIM, (kvh + 1) * HEAD_DIM)
                kh_scr[kvh, rows, :] = kc[:, hs].astype(BF16)

    for kvh in range(ATT_KV_HEADS):
        qt_scr[kvh] = jnp.concatenate(
            [q_ref[0, (kvh * ATT_GQ + g) * HEAD_DIM:(kvh * ATT_GQ + g + 1) * HEAD_DIM, :]
             for g in range(ATT_GQ)], axis=1)
    m_scr[...] = jnp.full(m_scr.shape, NEG, F32)
    acc_scr[...] = jnp.zeros(acc_scr.shape, F32)
    par = i % Q_PER_SUPER
    j_last = i // Q_PER_SUPER
    n_sb = jnp.minimum(j_last + 1, n_sb_max)

    def key_start(dl):
        return pl.multiple_of((j_last - dl) * K_SUPER, K_SUPER)

    def scores(dl, dst):
        for kvh in range(ATT_KV_HEADS):
            dst[kvh] = _dot(kh_scr[kvh, pl.ds(key_start(dl), K_SUPER), :], qt_scr[kvh])

    def softmax_pv(dl, src):
        start = key_start(dl)
        bias = bias_scr[par + Q_PER_SUPER * dl]
        bias2 = jnp.concatenate([bias, bias], axis=1)
        for kvh in range(ATT_KV_HEADS):
            vtb = va_scr[kvh, :, pl.ds(start, K_SUPER)]
            for pr in range(pairs_per_kv):
                u = kvh * pairs_per_kv + pr
                s = src[kvh, :, pr * pair_w:(pr + 1) * pair_w] + bias2
                m_old = m_scr[u, 0:1, :]
                m_new = jnp.maximum(m_old, jnp.max(s, axis=0, keepdims=True))
                p = jnp.exp2(s - m_new).astype(BF16)
                acc_scr[u] = jnp.exp2(m_old - m_new) * acc_scr[u] + _dot(vtb, p)
                m_scr[u, 0:1, :] = m_new

    def body(t, carry):
        scores(2 * t, s_scr)
        scores(2 * t + 1, s2_scr)
        softmax_pv(2 * t, s_scr)
        softmax_pv(2 * t + 1, s2_scr)
        return carry

    lax.fori_loop(0, n_sb // 2, body, 0)

    @pl.when(n_sb % 2 == 1)
    def _():
        scores(n_sb - 1, s_scr)
        softmax_pv(n_sb - 1, s_scr)

    for u in range(ATT_HEADS // 2):
        acc = acc_scr[u]
        o = acc[0:HEAD_DIM] / acc[HEAD_DIM:HEAD_DIM + 1]
        for e in range(2):
            h = 2 * u + e
            ot_scr[h * HEAD_DIM:(h + 1) * HEAD_DIM, :] = o[:, e * Q_BLOCK:(e + 1) * Q_BLOCK]
    o_ref[...] = (ot_scr[...].T * g_ref[...]).astype(BF16)


def _attn_prompt(qt, kt, vt, gate, batch, seq):
    rows = batch * seq
    nqb = seq // Q_BLOCK
    assert seq % K_SUPER == 0
    n_sb_max = min(seq // K_SUPER, W_MAX // K_SUPER + 1)
    n_tab = Q_PER_SUPER * n_sb_max
    qblk = lambda b, i: (b * nqb + i, 0)
    per_b = lambda b, i: (b, 0, 0)
    return pl.pallas_call(
        functools.partial(_attn_prompt_kernel, n_tab=n_tab, n_sb_max=n_sb_max),
        grid=(batch, nqb),
        in_specs=[
            pl.BlockSpec((1, ATT_INNER, Q_BLOCK), lambda b, i: (b, 0, i)),
            pl.BlockSpec((1, KV_DIM, seq), per_b),
            pl.BlockSpec((1, KV_DIM, seq), per_b),
            pl.BlockSpec((Q_BLOCK, ATT_INNER), qblk),
        ],
        out_specs=pl.BlockSpec((Q_BLOCK, ATT_INNER), qblk),
        out_shape=jax.ShapeDtypeStruct((rows, ATT_INNER), BF16),
        scratch_shapes=[
            pltpu.VMEM((n_tab, K_SUPER, Q_BLOCK), F32),
            pltpu.VMEM((ATT_KV_HEADS, seq, HEAD_DIM), BF16),
            pltpu.VMEM((ATT_KV_HEADS, ACC_ROWS, seq), BF16),
            pltpu.VMEM((ATT_KV_HEADS, HEAD_DIM, ATT_GQ * Q_BLOCK), BF16),
            pltpu.VMEM((ATT_HEADS // 2, SUBLANES, 2 * Q_BLOCK), F32),
            pltpu.VMEM((ATT_HEADS // 2, ACC_ROWS, 2 * Q_BLOCK), F32),
            pltpu.VMEM((ATT_INNER, Q_BLOCK), F32),
            pltpu.VMEM((ATT_KV_HEADS, K_SUPER, ATT_GQ * Q_BLOCK), F32),
            pltpu.VMEM((ATT_KV_HEADS, K_SUPER, ATT_GQ * Q_BLOCK), F32),
        ],
        compiler_params=pltpu.CompilerParams(
            dimension_semantics=("arbitrary", "arbitrary"), vmem_limit_bytes=VMEM_LIMIT),
        name="attn_prompt",
    )(qt, kt, vt, gate)


def _attn_sample_kernel(q_ref, knt_ref, vnt_ref, g_ref, ckt_ref, cvt_ref, o_ref, kot_ref, vot_ref, *, ts):
    win = ckt_ref.shape[2]
    tp = SAMPLE_PAD
    rows = ATT_GQ * tp
    t_c = lax.broadcasted_iota(jnp.int32, (rows, win), 0) % tp
    j_c = lax.broadcasted_iota(jnp.int32, (rows, win), 1)
    w_c = _multiplicity(win + t_c - j_c)
    t_n = lax.broadcasted_iota(jnp.int32, (rows, tp), 0) % tp
    j_n = lax.broadcasted_iota(jnp.int32, (rows, tp), 1)
    w_n = jnp.where(j_n < ts, _multiplicity(t_n - j_n), 0.0)
    for kvh in range(ATT_KV_HEADS):
        c0 = kvh * ATT_GQ * HEAD_DIM
        qh = jnp.concatenate(
            [q_ref[:, c0 + g * HEAD_DIM:c0 + (g + 1) * HEAD_DIM] for g in range(ATT_GQ)], axis=0)
        hs = slice(kvh * HEAD_DIM, (kvh + 1) * HEAD_DIM)
        s_c = jnp.where(w_c > 0.0, _dot(qh, ckt_ref[0, hs, :].astype(BF16)), NEG)
        s_n = jnp.where(w_n > 0.0, _dot(qh, knt_ref[0, hs, :].astype(BF16)), NEG)
        m = jnp.maximum(jnp.max(s_c, axis=-1, keepdims=True), jnp.max(s_n, axis=-1, keepdims=True))
        p_c = jnp.exp2(s_c - m) * w_c
        p_n = jnp.exp2(s_n - m) * w_n
        den = jnp.sum(p_c, axis=-1, keepdims=True) + jnp.sum(p_n, axis=-1, keepdims=True)
        num = (_dot_nt(p_c.astype(BF16), cvt_ref[0, hs, :].astype(BF16))
               + _dot_nt(p_n.astype(BF16), vnt_ref[0, hs, :].astype(BF16)))
        o = num / den
        o = jnp.concatenate([o[g * tp:(g + 1) * tp] for g in range(ATT_GQ)], axis=1)
        csl = slice(c0, c0 + ATT_GQ * HEAD_DIM)
        o_ref[:, csl] = (o * g_ref[:, csl]).astype(BF16)
    kot_ref[0, :, 0:win - ts] = ckt_ref[0, :, ts:win]
    kot_ref[0, :, win - ts:win] = knt_ref[0, :, 0:ts]
    vot_ref[0, :, 0:win - ts] = cvt_ref[0, :, ts:win]
    vot_ref[0, :, win - ts:win] = vnt_ref[0, :, 0:ts]


def _attn_sample(q, knt, vnt, gate, ckt, cvt, ts):
    batch, _, win = ckt.shape
    tp = SAMPLE_PAD
    blk = lambda b: (b, 0)
    per_b = lambda b: (b, 0, 0)
    return pl.pallas_call(
        functools.partial(_attn_sample_kernel, ts=ts),
        grid=(batch,),
        in_specs=[
            pl.BlockSpec((tp, ATT_INNER), blk),
            pl.BlockSpec((1, KV_DIM, tp), per_b),
            pl.BlockSpec((1, KV_DIM, tp), per_b),
            pl.BlockSpec((tp, ATT_INNER), blk),
            pl.BlockSpec((1, KV_DIM, win), per_b),
            pl.BlockSpec((1, KV_DIM, win), per_b),
        ],
        out_specs=[
            pl.BlockSpec((tp, ATT_INNER), blk),
            pl.BlockSpec((1, KV_DIM, win), per_b),
            pl.BlockSpec((1, KV_DIM, win), per_b),
        ],
        out_shape=[
            jax.ShapeDtypeStruct((batch * tp, ATT_INNER), BF16),
            jax.ShapeDtypeStruct((batch, KV_DIM, win), F32),
            jax.ShapeDtypeStruct((batch, KV_DIM, win), F32),
        ],
        compiler_params=pltpu.CompilerParams(
            dimension_semantics=("arbitrary",), vmem_limit_bytes=VMEM_LIMIT),
        name="attn_sample",
    )(q, knt, vnt, gate, ckt, cvt)


def _outproj_kernel(x_ref, ys_ref, ya_ref, w_ref, o_ref):
    o_ref[...] = (x_ref[...] + _dot(ys_ref[...], w_ref[0:SSD_INNER, :])
                  + _dot(ya_ref[...], w_ref[SSD_INNER:, :]))


def _out_proj(x2d, y_ssd, y_att, w_o, tm):
    rows = x2d.shape[0]
    row = lambda i: (i, 0)
    const = lambda i: (0, 0)
    return pl.pallas_call(
        _outproj_kernel,
        grid=(rows // tm,),
        in_specs=[
            pl.BlockSpec((tm, D_MODEL), row),
            pl.BlockSpec((tm, SSD_INNER), row),
            pl.BlockSpec((tm, ATT_INNER), row),
            pl.BlockSpec((SSD_INNER + ATT_INNER, D_MODEL), const, pipeline_mode=pl.Buffered(1)),
        ],
        out_specs=pl.BlockSpec((tm, D_MODEL), row),
        out_shape=jax.ShapeDtypeStruct((rows, D_MODEL), F32),
        compiler_params=pltpu.CompilerParams(
            dimension_semantics=("arbitrary",), vmem_limit_bytes=VMEM_LIMIT),
        name="out_proj",
    )(x2d, y_ssd, y_att, w_o)


def _wconv_out_kernel(x_ref, ys_ref, ya_ref, w_ref, wb_ref, o_ref):
    j = pl.program_id(0)
    wb = w_ref[...].astype(BF16)
    wb_ref[...] = wb
    y = jnp.where(j < SSD_INNER // WO_CHUNK, ys_ref[...], ya_ref[...])
    part = _dot(y, wb)

    @pl.when(j == 0)
    def _():
        o_ref[...] = x_ref[...] + part

    @pl.when(j > 0)
    def _():
        o_ref[...] += part


def _wconv_out(x2d, y_ssd, y_att, w_f32):
    rows = x2d.shape[0]
    n_ssd = SSD_INNER // WO_CHUNK
    n_chunks = (SSD_INNER + ATT_INNER) // WO_CHUNK
    const = lambda j: (0, 0)
    return pl.pallas_call(
        _wconv_out_kernel,
        grid=(n_chunks,),
        in_specs=[
            pl.BlockSpec((rows, D_MODEL), const),
            pl.BlockSpec((rows, WO_CHUNK), lambda j: (0, jnp.minimum(j, n_ssd - 1))),
            pl.BlockSpec((rows, WO_CHUNK), lambda j: (0, jnp.maximum(j - n_ssd, 0))),
            pl.BlockSpec((WO_CHUNK, D_MODEL), lambda j: (j, 0)),
        ],
        out_specs=[
            pl.BlockSpec((WO_CHUNK, D_MODEL), lambda j: (j, 0)),
            pl.BlockSpec((rows, D_MODEL), const),
        ],
        out_shape=[
            jax.ShapeDtypeStruct((SSD_INNER + ATT_INNER, D_MODEL), BF16),
            jax.ShapeDtypeStruct((rows, D_MODEL), F32),
        ],
        compiler_params=pltpu.CompilerParams(
            dimension_semantics=("arbitrary",), vmem_limit_bytes=VMEM_LIMIT),
        name="wconv_out",
    )(x2d, y_ssd, y_att, w_f32)


def _rope_tables(pos):
    n = pos.shape[0]
    half = ROT_DIM // 2
    inv = ROPE_THETA ** (-np.arange(0, ROT_DIM, 2, dtype=np.float64) / ROT_DIM)
    ang = pos.astype(np.float64)[:, None] * inv[None, :]
    cos, sin = np.cos(ang), np.sin(ang)
    rest = HEAD_DIM - ROT_DIM
    zh = np.zeros((n, half))
    cos_h = np.concatenate([cos, cos, np.ones((n, rest))], axis=1)
    sa_h = np.concatenate([zh, sin, np.zeros((n, rest))], axis=1)
    sb_h = np.concatenate([-sin, zh, np.zeros((n, rest))], axis=1)
    rep = LANES // HEAD_DIM
    return tuple(jnp.asarray(np.tile(t, (1, rep)), dtype=F32) for t in (cos_h, sa_h, sb_h))


def _lane_pad(v, n=LANES):
    return jnp.pad(v, (0, n - v.shape[0])).reshape(1, n)


def _to_cache(xt, batch, seq):
    return xt.reshape(1, batch, ATT_KV_HEADS, HEAD_DIM, seq).transpose(0, 1, 4, 2, 3)


def kernel(x_prompt, x_sample, cache_k, cache_v, state_conv, state_ssm, norm_w, w_in, conv_w,
           conv_b, dt_bias, a_log, d_skip, ssd_norm_w, q_norm_w, k_norm_w, w_out):
    bp, tp_, _ = x_prompt.shape
    bs, ts, _ = x_sample.shape
    depth = w_in.shape[0]
    assert depth == 1 and tp_ % SSD_CHUNK == 0 and ts <= SAMPLE_PAD and ts >= CONV_W - 1
    l = 0
    win = cache_k.shape[2]

    nw = norm_w[l].reshape(1, D_MODEL)
    rep = LANES // HEAD_DIM
    qnw = jnp.tile(q_norm_w[l], rep).reshape(1, LANES)
    knw = jnp.tile(k_norm_w[l], rep).reshape(1, LANES)
    cw = jnp.pad(conv_w[l], ((0, SUBLANES - CONV_W), (0, 0)))
    cb = conv_b[l].reshape(1, CONV_DIM)
    dtb = _lane_pad(dt_bias[l])
    alog = _lane_pad(a_log[l])
    dexp = jnp.repeat(d_skip[l], SSD_HEADDIM).reshape(1, SSD_INNER)
    snw = ssd_norm_w[l].reshape(1, SSD_INNER)

    pad = SAMPLE_PAD
    xs2 = jnp.pad(x_sample, ((0, 0), (0, pad - ts), (0, 0))).reshape(bs * pad, D_MODEL)
    tabs_s = _rope_tables(PAST_LEN + np.arange(bs * pad) % pad)
    w_t, proj_s = _wconv_in(xs2, nw, jnp.swapaxes(w_in[l], 0, 1))
    zs, xbc, dtr, q, kt, vt, gs, _ = _sections(proj_s, qnw, knw, tabs_s)
    cprev = jnp.pad(state_conv[l], ((0, 0), (SUBLANES - (CONV_W - 1), 0), (0, 0)))
    y_ssd, h_s = _ssd(xbc, dtr, zs, cprev, state_ssm[l].reshape(bs, SSD_INNER, SSD_STATE),
                      cw, cb, dtb, alog, dexp, snw, bs, pad, ts)
    knt = kt.reshape(KV_DIM, bs, pad).transpose(1, 0, 2)
    vnt = vt.reshape(KV_DIM, bs, pad).transpose(1, 0, 2)
    ckt = cache_k[l].transpose(0, 2, 3, 1).reshape(bs, KV_DIM, win)
    cvt = cache_v[l].transpose(0, 2, 3, 1).reshape(bs, KV_DIM, win)
    y_att, kot, vot = _attn_sample(q, knt, vnt, gs, ckt, cvt, ts)
    w_o, y_s = _wconv_out(xs2, y_ssd, y_att, w_out[l])
    y_s = y_s.reshape(bs, pad, D_MODEL)[:, :ts]
    k_s = _to_cache(kot, bs, win)
    v_s = _to_cache(vot, bs, win)
    c_s = xbc.reshape(bs, pad, CONV_DIM)[:, ts - (CONV_W - 1):ts][None]
    h_s = h_s.reshape(1, bs, SSD_HEADS, SSD_HEADDIM, SSD_STATE)

    tm = TM_IN
    xp2 = x_prompt.reshape(bp * tp_, D_MODEL)
    tabs = _rope_tables(np.arange(tp_))
    zs, xbc, dtr, q, kt, vt, gs, ctail = _in_proj(xp2, nw, w_t, qnw, knw, tabs, tm, tp_)
    y_ssd, h_p = _ssd(xbc, dtr, zs, jnp.zeros((bp, SUBLANES, CONV_DIM), F32),
                      jnp.zeros((bp, SSD_INNER, SSD_STATE), F32),
                      cw, cb, dtb, alog, dexp, snw, bp, SSD_CHUNK, SSD_CHUNK)
    y_att = _attn_prompt(q, kt, vt, gs, bp, tp_)
    y_p = _out_proj(xp2, y_ssd, y_att, w_o, TM_OUT).reshape(bp, tp_, D_MODEL)
    keep = min(W_MAX, tp_)
    k_p = _to_cache(kt, bp, tp_)[:, :, tp_ - keep:]
    v_p = _to_cache(vt, bp, tp_)[:, :, tp_ - keep:]
    c_p = ctail[:, SUBLANES - (CONV_W - 1):][None]
    h_p = h_p.reshape(1, bp, SSD_HEADS, SSD_HEADDIM, SSD_STATE)

    return (y_p, y_s, k_p, v_p, c_p, h_p, k_s, v_s, c_s, h_s)
```

```python
import functools

import jax
import jax.numpy as jnp
import numpy as np
from jax import lax
from jax.experimental import pallas as pl
from jax.experimental.pallas import tpu as pltpu

F32 = jnp.float32
BF16 = jnp.bfloat16

D_MODEL = 2048
SSD_HEADS = 16
SSD_HEADDIM = 64
SSD_INNER = SSD_HEADS * SSD_HEADDIM
SSD_GROUPS = 2
SSD_STATE = 128
CONV_W = 4
CONV_DIM = SSD_INNER + 2 * SSD_GROUPS * SSD_STATE
SSD_CHUNK = 128
ATT_HEADS = 16
ATT_KV_HEADS = 4
HEAD_DIM = 64
ATT_GQ = ATT_HEADS // ATT_KV_HEADS
ATT_INNER = ATT_HEADS * HEAD_DIM
KV_DIM = ATT_KV_HEADS * HEAD_DIM
ROT_DIM = HEAD_DIM // 4
ROPE_THETA = 500000.0
DILATED_BRANCHES = ((128, 1), (512, 4), (2048, 16))
W_MAX = 2048
PAST_LEN = 16384
EPS = 1e-6

LANES = 128
SUBLANES = 8
Q_BLOCK = 256
K_SUPER = 256
Q_PER_SUPER = K_SUPER // Q_BLOCK
KEY_BLOCKS_PER_BODY = 4
ONES_ROWS = 16
ACC_ROWS = HEAD_DIM + ONES_ROWS
CONV_COLS = 256
TM_IN = 512
TM_OUT = 512
SSD_SEQS = 2
SAMPLE_PAD = 16
NEG = -1e30
VMEM_LIMIT = 58 * 1024 * 1024

Z0 = 0
X0 = Z0 + SSD_INNER
DT0 = X0 + CONV_DIM
Q0 = DT0 + SSD_HEADS
K0 = Q0 + ATT_INNER
V0 = K0 + KV_DIM
G0 = V0 + KV_DIM
W_ROWS = G0 + ATT_INNER
W_CHUNK = 1024
WO_CHUNK = 512
HALF_INNER = SSD_INNER // SSD_GROUPS
LOG2E = 1.4426950408889634
Q_SCALE = HEAD_DIM ** -0.5 * LOG2E


def _dot(a, b):
    return jnp.dot(a, b, preferred_element_type=F32)


def _dot_nt(a, b):
    return lax.dot_general(a, b, (((1,), (1,)), ((), ())), preferred_element_type=F32)


def _split3(x):
    hi = x.astype(BF16)
    r1 = x - hi.astype(F32)
    mid = r1.astype(BF16)
    lo = (r1 - mid.astype(F32)).astype(BF16)
    return hi, mid, lo


def _dot_exact_rhs(x, m):
    hi, mid, lo = _split3(x)
    return _dot(hi, m) + _dot(mid, m) + _dot(lo, m)


def _dot_wide_rhs(x, m):
    hi = x.astype(BF16)
    lo = (x - hi.astype(F32)).astype(BF16)
    return _dot(hi, m) + _dot(lo, m)


def _dot_exact_lhs(m, x):
    hi, mid, lo = _split3(x)
    return _dot(m, hi) + _dot(m, mid) + _dot(m, lo)


def _silu(x):
    hx = 0.5 * x
    return hx * jnp.tanh(hx) + hx


def _multiplicity(d):
    w = jnp.zeros(d.shape, F32)
    for window, dil in DILATED_BRANCHES:
        hit = (d >= 0) & (d <= window) & (lax.rem(d, dil) == 0)
        w = w + jnp.where(hit, 1.0, 0.0)
    return w


def _norm_rope(y, nw, cos, sa, sb, scale):
    lane = lax.broadcasted_iota(jnp.int32, (1, LANES), 1)
    first = lane < HEAD_DIM
    y2 = y * y
    s_lo = jnp.sum(jnp.where(first, y2, 0.0), axis=-1, keepdims=True)
    s_hi = jnp.sum(jnp.where(first, 0.0, y2), axis=-1, keepdims=True)
    ms = jnp.where(first, s_lo, s_hi) * (1.0 / HEAD_DIM)
    yn = y * lax.rsqrt(ms + EPS) * nw
    half = ROT_DIM // 2
    rot = yn * cos + pltpu.roll(yn, half, 1) * sa + pltpu.roll(yn, LANES - half, 1) * sb
    return rot * scale


def _causal_conv_silu(load, cw_ref, cb_ref, store):
    for c0 in range(0, CONV_DIM, CONV_COLS):
        cs = slice(c0, c0 + CONV_COLS)
        xe = load(cs)
        x2 = pltpu.roll(xe, 2, 0)
        even = cw_ref[3:4, cs] * xe + cw_ref[1:2, cs] * x2
        odd = cw_ref[2:3, cs] * xe + cw_ref[0:1, cs] * x2
        store(cs, _silu((even + pltpu.roll(odd, 1, 0))[SUBLANES:, :] + cb_ref[:, cs]))


def _pre_norm(x_ref, nw_ref):
    x = x_ref[...]
    ms = jnp.mean(x * x, axis=-1, keepdims=True)
    return (x * lax.rsqrt(ms + EPS) * nw_ref[...]).astype(BF16)


def _inproj_kernel(x_ref, nw_ref, w_ref, *refs):
    hn = _pre_norm(x_ref, nw_ref)

    def proj(a, b):
        return _dot_nt(hn, w_ref[a:b, :])

    _emit_sections(proj, x_ref.shape[0], *refs)


def _emit_sections(proj, tm, qnw_ref, knw_ref, cos_ref, sa_ref, sb_ref,
                   zs_ref, xbc_ref, dt_ref, q_ref, kt_ref, vt_ref, gs_ref, ctail_ref):
    cos, sa, sb = cos_ref[...], sa_ref[...], sb_ref[...]
    q = proj(Q0, K0)
    for c in range(ATT_INNER // LANES):
        sl = slice(c * LANES, (c + 1) * LANES)
        qn = _norm_rope(q[:, sl], qnw_ref[...], cos, sa, sb, Q_SCALE)
        if len(q_ref.shape) == 3:
            q_ref[0, sl, :] = qn.T.astype(BF16)
        else:
            q_ref[:, sl] = qn.astype(BF16)
    k = proj(K0, V0)
    kn = [_norm_rope(k[:, c * LANES:(c + 1) * LANES], knw_ref[...], cos, sa, sb, 1.0)
          for c in range(KV_DIM // LANES)]
    kt_ref[0] = jnp.concatenate(kn, axis=1).T

    xbc = proj(X0, DT0)
    xbc_ref[...] = xbc
    ctail_ref[0] = xbc[tm - SUBLANES:tm, :]
    vt_ref[0] = proj(V0, G0).T
    zs_ref[...] = _silu(proj(Z0, X0))
    gs_ref[...] = _silu(proj(G0, W_ROWS))
    dt_ref[...] = proj(DT0, DT0 + LANES)


def _in_proj(x2d, nw, w_t, qnw, knw, tables, tm, seq):
    rows = x2d.shape[0]
    const = lambda i: (0, 0)
    row = lambda i: (i, 0)
    sec_in, out_specs, out_shape = _section_specs(rows, tm, seq, tables[0].shape[0] // tm, True)
    return pl.pallas_call(
        _inproj_kernel,
        grid=(rows // tm,),
        in_specs=[
            pl.BlockSpec((tm, D_MODEL), row),
            pl.BlockSpec((1, D_MODEL), const),
            pl.BlockSpec((W_ROWS, D_MODEL), const, pipeline_mode=pl.Buffered(1)),
        ] + sec_in,
        out_specs=out_specs,
        out_shape=out_shape,
        compiler_params=pltpu.CompilerParams(
            dimension_semantics=("arbitrary",), vmem_limit_bytes=VMEM_LIMIT),
        name="in_proj",
    )(x2d, nw, w_t, qnw, knw, *tables)


def _section_specs(rows, tm, seq, period, q_transposed):
    per_seq = seq // tm
    const = lambda i: (0, 0)
    row = lambda i: (i, 0)
    tab = lambda i: (i % period, 0)
    tr = lambda i: (i // per_seq, 0, i % per_seq)
    slab = lambda i: (i // per_seq, 0, 0)
    in_specs = [pl.BlockSpec((1, LANES), const)] * 2 + [pl.BlockSpec((tm, LANES), tab)] * 3
    row_outs = {0: (SSD_INNER, F32), 1: (CONV_DIM, F32), 2: (LANES, F32), 3: (ATT_INNER, BF16),
                6: (ATT_INNER, F32)}
    out_specs, out_shape = [], []
    for idx in range(7):
        if idx == 3 and q_transposed:
            out_specs.append(pl.BlockSpec((1, ATT_INNER, tm), tr))
            out_shape.append(jax.ShapeDtypeStruct((rows // seq, ATT_INNER, seq), BF16))
        elif idx in row_outs:
            n, dt = row_outs[idx]
            out_specs.append(pl.BlockSpec((tm, n), row))
            out_shape.append(jax.ShapeDtypeStruct((rows, n), dt))
        else:
            out_specs.append(pl.BlockSpec((1, KV_DIM, tm), tr))
            out_shape.append(jax.ShapeDtypeStruct((rows // seq, KV_DIM, seq), F32))
    out_specs.append(pl.BlockSpec((1, SUBLANES, CONV_DIM), slab))
    out_shape.append(jax.ShapeDtypeStruct((rows // seq, SUBLANES, CONV_DIM), F32))
    return in_specs, out_specs, out_shape


def _wconv_in_kernel(x_ref, nw_ref, w_ref, wb_ref, p_ref, hn_scr):
    @pl.when(pl.program_id(0) == 0)
    def _():
        hn_scr[...] = _pre_norm(x_ref, nw_ref)

    row = pl.program_id(0) * W_CHUNK + lax.broadcasted_iota(jnp.int32, (W_CHUNK, 1), 0)
    wb = jnp.where(row < W_ROWS, w_ref[...], 0.0).astype(BF16)
    wb_ref[...] = wb
    p_ref[0] = _dot_nt(hn_scr[...], wb)


def _wconv_in(x2d, nw, w_f32):
    rows = x2d.shape[0]
    n_chunks = pl.cdiv(W_ROWS, W_CHUNK)
    const = lambda j: (0, 0)
    return pl.pallas_call(
        _wconv_in_kernel,
        grid=(n_chunks,),
        in_specs=[
            pl.BlockSpec((rows, D_MODEL), const),
            pl.BlockSpec((1, D_MODEL), const),
            pl.BlockSpec((W_CHUNK, D_MODEL), lambda j: (j, 0)),
        ],
        out_specs=[
            pl.BlockSpec((W_CHUNK, D_MODEL), lambda j: (j, 0)),
            pl.BlockSpec((1, rows, W_CHUNK), lambda j: (j, 0, 0)),
        ],
        out_shape=[
            jax.ShapeDtypeStruct((W_ROWS, D_MODEL), BF16),
            jax.ShapeDtypeStruct((n_chunks, rows, W_CHUNK), F32),
        ],
        scratch_shapes=[pltpu.VMEM((rows, D_MODEL), BF16)],
        compiler_params=pltpu.CompilerParams(
            dimension_semantics=("arbitrary",), vmem_limit_bytes=VMEM_LIMIT),
        name="wconv_in",
    )(x2d, nw, w_f32)


def _sections_kernel(p_ref, *refs):
    *refs, tail_scr = refs
    pfull = jnp.concatenate([p_ref[j] for j in range(p_ref.shape[0])], axis=1)
    tail_scr[...] = pfull[:, Q0:W_ROWS]

    def proj(a, b):
        if a >= Q0:
            return tail_scr[:, a - Q0:b - Q0]
        return pfull[:, a:b]

    _emit_sections(proj, p_ref.shape[1], *refs)


def _sections(p, qnw, knw, tables):
    rows = p.shape[1]
    sec_in, out_specs, out_shape = _section_specs(rows, rows, rows, 1, False)
    return pl.pallas_call(
        _sections_kernel,
        grid=(1,),
        in_specs=[pl.BlockSpec(p.shape, lambda i: (0, 0, 0))] + sec_in,
        out_specs=out_specs,
        out_shape=out_shape,
        scratch_shapes=[pltpu.VMEM((rows, W_ROWS - Q0), F32)],
        compiler_params=pltpu.CompilerParams(
            dimension_semantics=("arbitrary",), vmem_limit_bytes=VMEM_LIMIT),
        name="sections",
    )(p, qnw, knw, *tables)


def _ssd_chain(s, xbc_ref, dtr_ref, z_ref, cprev_ref, h0_ref, cw_ref, cb_ref, dtb_ref, alog_ref,
               dexp_ref, nw_ref, y_ref, hout_ref, xext, ht, xc_scr, lin, valid):
    L = SSD_CHUNK
    c = pl.program_id(1)
    last = pl.num_programs(1) - 1

    @pl.when(c == 0)
    def _():
        xext[s, 0:SUBLANES, :] = cprev_ref[s]
        for g in range(SSD_GROUPS):
            ht[s, g] = h0_ref[s, g * HALF_INNER:(g + 1) * HALF_INNER, :].T

    def pad_rows(v):
        if lin == L:
            return v
        return jnp.concatenate([v, jnp.zeros((L - lin, v.shape[1]), v.dtype)], axis=0)

    lane = lax.broadcasted_iota(jnp.int32, (1, LANES), 1)
    rowid = lax.broadcasted_iota(jnp.int32, (L, 1), 0)
    dt = jax.nn.softplus(pad_rows(dtr_ref[s]) + dtb_ref[...])
    dt = jnp.where((lane < SSD_HEADS) & (rowid < valid), dt, 0.0)
    dta = dt * (-jnp.exp(alog_ref[...]) * LOG2E)

    r2 = lax.broadcasted_iota(jnp.int32, (L, L), 0)
    c2 = lax.broadcasted_iota(jnp.int32, (L, L), 1)
    tri = r2 >= c2
    tri_b = jnp.where(tri, 1.0, 0.0).astype(BF16)
    cum = _dot_exact_lhs(tri_b, dta)
    cum_last = cum[L - 1:L, :]
    yield

    er = lax.broadcasted_iota(jnp.int32, (LANES, SSD_INNER), 0)
    ec = lax.broadcasted_iota(jnp.int32, (LANES, SSD_INNER), 1)
    expand = jnp.where(ec // SSD_HEADDIM == er, 1.0, 0.0).astype(BF16)
    ecum = _dot_wide_rhs(jnp.exp2(cum), expand)
    wexp = _dot_wide_rhs(jnp.exp2(cum_last - cum) * dt, expand)
    cd = _dot_exact_rhs(jnp.broadcast_to(jnp.exp2(cum_last), (SUBLANES, LANES)), expand)[0:1, :]
    cum_t = cum.T
    dt_t = dt.T
    yield

    if lin < L:
        xext[s, SUBLANES + lin:SUBLANES + L, :] = jnp.zeros((L - lin, CONV_DIM), F32)
    xext[s, SUBLANES:SUBLANES + lin, :] = xbc_ref[s]

    def load(cs):
        return xext[s, 0:SUBLANES + L, cs]

    def store(cs, v):
        xc_scr[s, :, cs] = v

    _causal_conv_silu(load, cw_ref, cb_ref, store)
    xext[s, 0:SUBLANES, :] = xext[s, L:L + SUBLANES, :]
    xc = xc_scr.at[s]
    xs = xc[:, :SSD_INNER]
    bm = xc[:, SSD_INNER:SSD_INNER + SSD_GROUPS * SSD_STATE]
    cm = xc[:, SSD_INNER + SSD_GROUPS * SSD_STATE:]
    bm_b = bm.astype(BF16)
    cm_b = cm.astype(BF16)
    xs_b = xs.astype(BF16)

    y_off = jnp.concatenate(
        [_dot(cm_b[:, g * SSD_STATE:(g + 1) * SSD_STATE], ht[s, g].astype(BF16))
         for g in range(SSD_GROUPS)], axis=1) * ecum
    cbs = [_dot_nt(cm_b[:, g * SSD_STATE:(g + 1) * SSD_STATE], bm_b[:, g * SSD_STATE:(g + 1) * SSD_STATE])
           for g in range(SSD_GROUPS)]
    yield

    lane_half = lane // HEAD_DIM
    heads_per_group = SSD_HEADS // SSD_GROUPS
    y_parts = []
    for g in range(SSD_GROUPS):
        for pr in range(heads_per_group // 2):
            col0 = g * HALF_INNER + pr * LANES
            xp = xs_b[:, col0:col0 + LANES]
            yp = jnp.zeros((L, LANES), F32)
            for e in range(2):
                hh = g * heads_per_group + pr * 2 + e
                decay = jnp.exp2(cum[:, hh:hh + 1] - cum_t[hh:hh + 1, :])
                sc = jnp.where(tri, cbs[g] * decay, 0.0) * dt_t[hh:hh + 1, :]
                xm = jnp.where(lane_half == e, xp, jnp.zeros_like(xp))
                yp = yp + _dot(sc.astype(BF16), xm)
            y_parts.append(yp)
            if pr % 2 == 1:
                yield
    y_diag = jnp.concatenate(y_parts, axis=1)

    xw = (xs * wexp).astype(BF16)
    for g in range(SSD_GROUPS):
        sl = slice(g * HALF_INNER, (g + 1) * HALF_INNER)
        bm_t = bm[:, g * SSD_STATE:(g + 1) * SSD_STATE].T.astype(BF16)
        ht[s, g] = ht[s, g] * cd[:, sl] + _dot(bm_t, xw[:, sl])
    yield

    y = (y_diag + y_off + dexp_ref[...] * xs) * pad_rows(z_ref[s])
    yn = []
    for g in range(SSD_GROUPS):
        yg = y[:, g * HALF_INNER:(g + 1) * HALF_INNER]
        ms = jnp.mean(yg * yg, axis=-1, keepdims=True)
        yn.append(yg * lax.rsqrt(ms + EPS))
    yn = jnp.concatenate(yn, axis=1) * nw_ref[...]
    y_ref[s] = yn[:lin].astype(BF16)

    @pl.when(c == last)
    def _():
        for g in range(SSD_GROUPS):
            hout_ref[s, g * HALF_INNER:(g + 1) * HALF_INNER, :] = ht[s, g].T


def _ssd_kernel(*refs, lin, valid):
    chains = [_ssd_chain(s, *refs, lin, valid) for s in range(SSD_SEQS)]
    while chains:
        alive = []
        for ch in chains:
            try:
                next(ch)
                alive.append(ch)
            except StopIteration:
                pass
        chains = alive


def _ssd(xbc, dtr, z, cprev, h0, cw, cb, dtb, alog, dexp, nw, batch, lin, valid):
    rows = xbc.shape[0]
    seq = rows // batch
    nc = seq // lin
    g = SSD_SEQS
    assert batch % g == 0
    blk = lambda b, c: (b, c, 0)
    per_b = lambda b, c: (b, 0, 0)
    const = lambda b, c: (0, 0)
    y, hout = pl.pallas_call(
        functools.partial(_ssd_kernel, lin=lin, valid=valid),
        grid=(batch // g, nc),
        in_specs=[
            pl.BlockSpec((g, lin, CONV_DIM), blk),
            pl.BlockSpec((g, lin, LANES), blk),
            pl.BlockSpec((g, lin, SSD_INNER), blk),
            pl.BlockSpec((g, SUBLANES, CONV_DIM), per_b),
            pl.BlockSpec((g, SSD_INNER, SSD_STATE), per_b),
            pl.BlockSpec((SUBLANES, CONV_DIM), const),
            pl.BlockSpec((1, CONV_DIM), const),
            pl.BlockSpec((1, LANES), const),
            pl.BlockSpec((1, LANES), const),
            pl.BlockSpec((1, SSD_INNER), const),
            pl.BlockSpec((1, SSD_INNER), const),
        ],
        out_specs=[
            pl.BlockSpec((g, lin, SSD_INNER), blk),
            pl.BlockSpec((g, SSD_INNER, SSD_STATE), per_b),
        ],
        out_shape=[
            jax.ShapeDtypeStruct((batch, seq, SSD_INNER), BF16),
            jax.ShapeDtypeStruct((batch, SSD_INNER, SSD_STATE), F32),
        ],
        scratch_shapes=[
            pltpu.VMEM((g, SUBLANES + SSD_CHUNK + SUBLANES, CONV_DIM), F32),
            pltpu.VMEM((g, SSD_GROUPS, SSD_STATE, HALF_INNER), F32),
            pltpu.VMEM((g, SSD_CHUNK, CONV_DIM), F32),
        ],
        compiler_params=pltpu.CompilerParams(
            dimension_semantics=("arbitrary", "arbitrary"), vmem_limit_bytes=VMEM_LIMIT),
        name="ssd",
    )(xbc.reshape(batch, seq, CONV_DIM), dtr.reshape(batch, seq, LANES),
      z.reshape(batch, seq, SSD_INNER), cprev, h0, cw, cb, dtb, alog, dexp, nw)
    return y.reshape(rows, SSD_INNER), hout


def _attn_prompt_kernel(q_ref, kt_ref, vt_ref, g_ref, o_ref,
                        bias_scr, kh_scr, va_scr, qt_scr, m_scr, acc_scr, ot_scr, s_scr, *, n_tab, n_sb_max):
    b = pl.program_id(0)
    i = pl.program_id(1)
    seq = kt_ref.shape[2]
    pair_w = 2 * Q_BLOCK
    pairs_per_kv = ATT_GQ // 2

    @pl.when((b == 0) & (i == 0))
    def _():
        r = lax.broadcasted_iota(jnp.int32, (K_SUPER, Q_BLOCK), 0)
        c = lax.broadcasted_iota(jnp.int32, (K_SUPER, Q_BLOCK), 1)
        for tb in range(n_tab):
            w = _multiplicity(tb * Q_BLOCK + c - r)
            bias_scr[tb] = jnp.where(w > 0.0, jnp.log2(jnp.maximum(w, 1.0)), NEG)
        va_scr[:, HEAD_DIM:, :] = jnp.ones((ATT_KV_HEADS, ONES_ROWS, seq), BF16)

    @pl.when(i == 0)
    def _():
        for kvh in range(ATT_KV_HEADS):
            hs = slice(kvh * HEAD_DIM, (kvh + 1) * HEAD_DIM)
            va_scr[kvh, 0:HEAD_DIM, :] = vt_ref[0, hs, :].astype(BF16)
        for cidx in range(seq // K_SUPER):
            rows = slice(cidx * K_SUPER, (cidx + 1) * K_SUPER)
            kc = kt_ref[0, :, rows].T
            for kvh in range(ATT_KV_HEADS):
                hs = slice(kvh * HEAD_DIM, (kvh + 1) * HEAD_DIM)
                kh_scr[kvh, rows, :] = kc[:, hs].astype(BF16)

    for kvh in range(ATT_KV_HEADS):
        qt_scr[kvh] = jnp.concatenate(
            [q_ref[0, (kvh * ATT_GQ + g) * HEAD_DIM:(kvh * ATT_GQ + g + 1) * HEAD_DIM, :]
             for g in range(ATT_GQ)], axis=1)
    m_scr[...] = jnp.full(m_scr.shape, NEG, F32)
    acc_scr[...] = jnp.zeros(acc_scr.shape, F32)
    par = i % Q_PER_SUPER
    j_last = i // Q_PER_SUPER
    n_sb = jnp.minimum(j_last + 1, n_sb_max)

    def key_start(dl):
        return pl.multiple_of((j_last - dl) * K_SUPER, K_SUPER)

    def scores(dl, dst):
        for kvh in range(ATT_KV_HEADS):
            dst[kvh] = _dot(kh_scr[kvh, pl.ds(key_start(dl), K_SUPER), :], qt_scr[kvh])

    def softmax_pv(dl, src):
        start = key_start(dl)
        bias = bias_scr[par + Q_PER_SUPER * dl]
        bias2 = jnp.concatenate([bias, bias], axis=1)
        for kvh in range(ATT_KV_HEADS):
            vtb = va_scr[kvh, :, pl.ds(start, K_SUPER)]
            for pr in range(pairs_per_kv):
                u = kvh * pairs_per_kv + pr
                s = src[kvh, :, pr * pair_w:(pr + 1) * pair_w] + bias2
                m_old = m_scr[u, 0:1, :]
                m_new = jnp.maximum(m_old, jnp.max(s, axis=0, keepdims=True))
                p = jnp.exp2(s - m_new).astype(BF16)
                acc_scr[u] = jnp.exp2(m_old - m_new) * acc_scr[u] + _dot(vtb, p)
                m_scr[u, 0:1, :] = m_new

    def group(dl0, n):
        for e in range(n):
            scores(dl0 + e, s_scr.at[e])
        for e in range(n):
            softmax_pv(dl0 + e, s_scr.at[e])

    def body(t, carry):
        group(KEY_BLOCKS_PER_BODY * t, KEY_BLOCKS_PER_BODY)
        return carry

    n_full = n_sb // KEY_BLOCKS_PER_BODY
    lax.fori_loop(0, n_full, body, 0)
    done = n_full * KEY_BLOCKS_PER_BODY
    size = KEY_BLOCKS_PER_BODY // 2
    while size >= 1:
        take = ((n_sb - done) // size) % 2 == 1

        @pl.when(take)
        def _(done=done, size=size):
            group(done, size)

        done = done + jnp.where(take, size, 0)
        size //= 2

    for u in range(ATT_HEADS // 2):
        acc = acc_scr[u]
        o = acc[0:HEAD_DIM] / acc[HEAD_DIM:HEAD_DIM + 1]
        for e in range(2):
            h = 2 * u + e
            ot_scr[h * HEAD_DIM:(h + 1) * HEAD_DIM, :] = o[:, e * Q_BLOCK:(e + 1) * Q_BLOCK]
    o_ref[...] = (ot_scr[...].T * g_ref[...]).astype(BF16)


def _attn_prompt(qt, kt, vt, gate, batch, seq):
    rows = batch * seq
    nqb = seq // Q_BLOCK
    assert seq % K_SUPER == 0
    n_sb_max = min(seq // K_SUPER, W_MAX // K_SUPER + 1)
    n_tab = Q_PER_SUPER * n_sb_max
    qblk = lambda b, i: (b * nqb + i, 0)
    per_b = lambda b, i: (b, 0, 0)
    return pl.pallas_call(
        functools.partial(_attn_prompt_kernel, n_tab=n_tab, n_sb_max=n_sb_max),
        grid=(batch, nqb),
        in_specs=[
            pl.BlockSpec((1, ATT_INNER, Q_BLOCK), lambda b, i: (b, 0, i)),
            pl.BlockSpec((1, KV_DIM, seq), per_b),
            pl.BlockSpec((1, KV_DIM, seq), per_b),
            pl.BlockSpec((Q_BLOCK, ATT_INNER), qblk),
        ],
        out_specs=pl.BlockSpec((Q_BLOCK, ATT_INNER), qblk),
        out_shape=jax.ShapeDtypeStruct((rows, ATT_INNER), BF16),
        scratch_shapes=[
            pltpu.VMEM((n_tab, K_SUPER, Q_BLOCK), F32),
            pltpu.VMEM((ATT_KV_HEADS, seq, HEAD_DIM), BF16),
            pltpu.VMEM((ATT_KV_HEADS, ACC_ROWS, seq), BF16),
            pltpu.VMEM((ATT_KV_HEADS, HEAD_DIM, ATT_GQ * Q_BLOCK), BF16),
            pltpu.VMEM((ATT_HEADS // 2, SUBLANES, 2 * Q_BLOCK), F32),
            pltpu.VMEM((ATT_HEADS // 2, ACC_ROWS, 2 * Q_BLOCK), F32),
            pltpu.VMEM((ATT_INNER, Q_BLOCK), F32),
            pltpu.VMEM((KEY_BLOCKS_PER_BODY, ATT_KV_HEADS, K_SUPER, ATT_GQ * Q_BLOCK), F32),
        ],
        compiler_params=pltpu.CompilerParams(
            dimension_semantics=("arbitrary", "arbitrary"), vmem_limit_bytes=VMEM_LIMIT),
        name="attn_prompt",
    )(qt, kt, vt, gate)


def _attn_sample_kernel(q_ref, knt_ref, vnt_ref, g_ref, ckt_ref, cvt_ref, o_ref, kot_ref, vot_ref, *, ts):
    win = ckt_ref.shape[2]
    tp = SAMPLE_PAD
    rows = ATT_GQ * tp
    t_c = lax.broadcasted_iota(jnp.int32, (rows, win), 0) % tp
    j_c = lax.broadcasted_iota(jnp.int32, (rows, win), 1)
    w_c = _multiplicity(win + t_c - j_c)
    t_n = lax.broadcasted_iota(jnp.int32, (rows, tp), 0) % tp
    j_n = lax.broadcasted_iota(jnp.int32, (rows, tp), 1)
    w_n = jnp.where(j_n < ts, _multiplicity(t_n - j_n), 0.0)
    for kvh in range(ATT_KV_HEADS):
        c0 = kvh * ATT_GQ * HEAD_DIM
        qh = jnp.concatenate(
            [q_ref[:, c0 + g * HEAD_DIM:c0 + (g + 1) * HEAD_DIM] for g in range(ATT_GQ)], axis=0)
        hs = slice(kvh * HEAD_DIM, (kvh + 1) * HEAD_DIM)
        s_c = jnp.where(w_c > 0.0, _dot(qh, ckt_ref[0, hs, :].astype(BF16)), NEG)
        s_n = jnp.where(w_n > 0.0, _dot(qh, knt_ref[0, hs, :].astype(BF16)), NEG)
        m = jnp.maximum(jnp.max(s_c, axis=-1, keepdims=True), jnp.max(s_n, axis=-1, keepdims=True))
        p_c = jnp.exp2(s_c - m) * w_c
        p_n = jnp.exp2(s_n - m) * w_n
        den = jnp.sum(p_c, axis=-1, keepdims=True) + jnp.sum(p_n, axis=-1, keepdims=True)
        num = (_dot_nt(p_c.astype(BF16), cvt_ref[0, hs, :].astype(BF16))
               + _dot_nt(p_n.astype(BF16), vnt_ref[0, hs, :].astype(BF16)))
        o = num / den
        o = jnp.concatenate([o[g * tp:(g + 1) * tp] for g in range(ATT_GQ)], axis=1)
        csl = slice(c0, c0 + ATT_GQ * HEAD_DIM)
        o_ref[:, csl] = (o * g_ref[:, csl]).astype(BF16)
    kot_ref[0, :, 0:win - ts] = ckt_ref[0, :, ts:win]
    kot_ref[0, :, win - ts:win] = knt_ref[0, :, 0:ts]
    vot_ref[0, :, 0:win - ts] = cvt_ref[0, :, ts:win]
    vot_ref[0, :, win - ts:win] = vnt_ref[0, :, 0:ts]


def _attn_sample(q, knt, vnt, gate, ckt, cvt, ts):
    batch, _, win = ckt.shape
    tp = SAMPLE_PAD
    blk = lambda b: (b, 0)
    per_b = lambda b: (b, 0, 0)
    return pl.pallas_call(
        functools.partial(_attn_sample_kernel, ts=ts),
        grid=(batch,),
        in_specs=[
            pl.BlockSpec((tp, ATT_INNER), blk),
            pl.BlockSpec((1, KV_DIM, tp), per_b),
            pl.BlockSpec((1, KV_DIM, tp), per_b),
            pl.BlockSpec((tp, ATT_INNER), blk),
            pl.BlockSpec((1, KV_DIM, win), per_b),
            pl.BlockSpec((1, KV_DIM, win), per_b),
        ],
        out_specs=[
            pl.BlockSpec((tp, ATT_INNER), blk),
            pl.BlockSpec((1, KV_DIM, win), per_b),
            pl.BlockSpec((1, KV_DIM, win), per_b),
        ],
        out_shape=[
            jax.ShapeDtypeStruct((batch * tp, ATT_INNER), BF16),
            jax.ShapeDtypeStruct((batch, KV_DIM, win), F32),
            jax.ShapeDtypeStruct((batch, KV_DIM, win), F32),
        ],
        compiler_params=pltpu.CompilerParams(
            dimension_semantics=("arbitrary",), vmem_limit_bytes=VMEM_LIMIT),
        name="attn_sample",
    )(q, knt, vnt, gate, ckt, cvt)


def _outproj_kernel(x_ref, ys_ref, ya_ref, w_ref, o_ref):
    o_ref[...] = (x_ref[...] + _dot(ys_ref[...], w_ref[0:SSD_INNER, :])
                  + _dot(ya_ref[...], w_ref[SSD_INNER:, :]))


def _out_proj(x2d, y_ssd, y_att, w_o, tm):
    rows = x2d.shape[0]
    row = lambda i: (i, 0)
    const = lambda i: (0, 0)
    return pl.pallas_call(
        _outproj_kernel,
        grid=(rows // tm,),
        in_specs=[
            pl.BlockSpec((tm, D_MODEL), row),
            pl.BlockSpec((tm, SSD_INNER), row),
            pl.BlockSpec((tm, ATT_INNER), row),
            pl.BlockSpec((SSD_INNER + ATT_INNER, D_MODEL), const, pipeline_mode=pl.Buffered(1)),
        ],
        out_specs=pl.BlockSpec((tm, D_MODEL), row),
        out_shape=jax.ShapeDtypeStruct((rows, D_MODEL), F32),
        compiler_params=pltpu.CompilerParams(
            dimension_semantics=("arbitrary",), vmem_limit_bytes=VMEM_LIMIT),
        name="out_proj",
    )(x2d, y_ssd, y_att, w_o)


def _wconv_out_kernel(x_ref, ys_ref, ya_ref, w_ref, wb_ref, o_ref):
    j = pl.program_id(0)
    wb = w_ref[...].astype(BF16)
    wb_ref[...] = wb
    y = jnp.where(j < SSD_INNER // WO_CHUNK, ys_ref[...], ya_ref[...])
    part = _dot(y, wb)

    @pl.when(j == 0)
    def _():
        o_ref[...] = x_ref[...] + part

    @pl.when(j > 0)
    def _():
        o_ref[...] += part


def _wconv_out(x2d, y_ssd, y_att, w_f32):
    rows = x2d.shape[0]
    n_ssd = SSD_INNER // WO_CHUNK
    n_chunks = (SSD_INNER + ATT_INNER) // WO_CHUNK
    const = lambda j: (0, 0)
    return pl.pallas_call(
        _wconv_out_kernel,
        grid=(n_chunks,),
        in_specs=[
            pl.BlockSpec((rows, D_MODEL), const),
            pl.BlockSpec((rows, WO_CHUNK), lambda j: (0, jnp.minimum(j, n_ssd - 1))),
            pl.BlockSpec((rows, WO_CHUNK), lambda j: (0, jnp.maximum(j - n_ssd, 0))),
            pl.BlockSpec((WO_CHUNK, D_MODEL), lambda j: (j, 0)),
        ],
        out_specs=[
            pl.BlockSpec((WO_CHUNK, D_MODEL), lambda j: (j, 0)),
            pl.BlockSpec((rows, D_MODEL), const),
        ],
        out_shape=[
            jax.ShapeDtypeStruct((SSD_INNER + ATT_INNER, D_MODEL), BF16),
            jax.ShapeDtypeStruct((rows, D_MODEL), F32),
        ],
        compiler_params=pltpu.CompilerParams(
            dimension_semantics=("arbitrary",), vmem_limit_bytes=VMEM_LIMIT),
        name="wconv_out",
    )(x2d, y_ssd, y_att, w_f32)


def _rope_tables(pos):
    n = pos.shape[0]
    half = ROT_DIM // 2
    inv = ROPE_THETA ** (-np.arange(0, ROT_DIM, 2, dtype=np.float64) / ROT_DIM)
    ang = pos.astype(np.float64)[:, None] * inv[None, :]
    cos, sin = np.cos(ang), np.sin(ang)
    rest = HEAD_DIM - ROT_DIM
    zh = np.zeros((n, half))
    cos_h = np.concatenate([cos, cos, np.ones((n, rest))], axis=1)
    sa_h = np.concatenate([zh, sin, np.zeros((n, rest))], axis=1)
    sb_h = np.concatenate([-sin, zh, np.zeros((n, rest))], axis=1)
    rep = LANES // HEAD_DIM
    return tuple(jnp.asarray(np.tile(t, (1, rep)), dtype=F32) for t in (cos_h, sa_h, sb_h))


def _lane_pad(v, n=LANES):
    return jnp.pad(v, (0, n - v.shape[0])).reshape(1, n)


def _to_cache(xt, batch, seq):
    return xt.reshape(1, batch, ATT_KV_HEADS, HEAD_DIM, seq).transpose(0, 1, 4, 2, 3)


def kernel(x_prompt, x_sample, cache_k, cache_v, state_conv, state_ssm, norm_w, w_in, conv_w,
           conv_b, dt_bias, a_log, d_skip, ssd_norm_w, q_norm_w, k_norm_w, w_out):
    bp, tp_, _ = x_prompt.shape
    bs, ts, _ = x_sample.shape
    depth = w_in.shape[0]
    assert depth == 1 and tp_ % SSD_CHUNK == 0 and ts <= SAMPLE_PAD and ts >= CONV_W - 1
    l = 0
    win = cache_k.shape[2]

    nw = norm_w[l].reshape(1, D_MODEL)
    rep = LANES // HEAD_DIM
    qnw = jnp.tile(q_norm_w[l], rep).reshape(1, LANES)
    knw = jnp.tile(k_norm_w[l], rep).reshape(1, LANES)
    cw = jnp.pad(conv_w[l], ((0, SUBLANES - CONV_W), (0, 0)))
    cb = conv_b[l].reshape(1, CONV_DIM)
    dtb = _lane_pad(dt_bias[l])
    alog = _lane_pad(a_log[l])
    dexp = jnp.repeat(d_skip[l], SSD_HEADDIM).reshape(1, SSD_INNER)
    snw = ssd_norm_w[l].reshape(1, SSD_INNER)

    pad = SAMPLE_PAD
    xs2 = jnp.pad(x_sample, ((0, 0), (0, pad - ts), (0, 0))).reshape(bs * pad, D_MODEL)
    tabs_s = _rope_tables(PAST_LEN + np.arange(bs * pad) % pad)
    w_t, proj_s = _wconv_in(xs2, nw, jnp.swapaxes(w_in[l], 0, 1))
    zs, xbc, dtr, q, kt, vt, gs, _ = _sections(proj_s, qnw, knw, tabs_s)
    cprev = jnp.pad(state_conv[l], ((0, 0), (SUBLANES - (CONV_W - 1), 0), (0, 0)))
    y_ssd, h_s = _ssd(xbc, dtr, zs, cprev, state_ssm[l].reshape(bs, SSD_INNER, SSD_STATE),
                      cw, cb, dtb, alog, dexp, snw, bs, pad, ts)
    knt = kt.reshape(KV_DIM, bs, pad).transpose(1, 0, 2)
    vnt = vt.reshape(KV_DIM, bs, pad).transpose(1, 0, 2)
    ckt = cache_k[l].transpose(0, 2, 3, 1).reshape(bs, KV_DIM, win)
    cvt = cache_v[l].transpose(0, 2, 3, 1).reshape(bs, KV_DIM, win)
    y_att, kot, vot = _attn_sample(q, knt, vnt, gs, ckt, cvt, ts)
    w_o, y_s = _wconv_out(xs2, y_ssd, y_att, w_out[l])
    y_s = y_s.reshape(bs, pad, D_MODEL)[:, :ts]
    k_s = _to_cache(kot, bs, win)
    v_s = _to_cache(vot, bs, win)
    c_s = xbc.reshape(bs, pad, CONV_DIM)[:, ts - (CONV_W - 1):ts][None]
    h_s = h_s.reshape(1, bs, SSD_HEADS, SSD_HEADDIM, SSD_STATE)

    tm = TM_IN
    xp2 = x_prompt.reshape(bp * tp_, D_MODEL)
    tabs = _rope_tables(np.arange(tp_))
    zs, xbc, dtr, q, kt, vt, gs, ctail = _in_proj(xp2, nw, w_t, qnw, knw, tabs, tm, tp_)
    y_ssd, h_p = _ssd(xbc, dtr, zs, jnp.zeros((bp, SUBLANES, CONV_DIM), F32),
                      jnp.zeros((bp, SSD_INNER, SSD_STATE), F32),
                      cw, cb, dtb, alog, dexp, snw, bp, SSD_CHUNK, SSD_CHUNK)
    y_att = _attn_prompt(q, kt, vt, gs, bp, tp_)
    y_p = _out_proj(xp2, y_ssd, y_att, w_o, TM_OUT).reshape(bp, tp_, D_MODEL)
    keep = min(W_MAX, tp_)
    k_p = _to_cache(kt, bp, tp_)[:, :, tp_ - keep:]
    v_p = _to_cache(vt, bp, tp_)[:, :, tp_ - keep:]
    c_p = ctail[:, SUBLANES - (CONV_W - 1):][None]
    h_p = h_p.reshape(1, bp, SSD_HEADS, SSD_HEADDIM, SSD_STATE)

    return (y_p, y_s, k_p, v_p, c_p, h_p, k_s, v_s, c_s, h_s)
```

```python
import functools

import jax
import jax.numpy as jnp
import numpy as np
from jax import lax
from jax.experimental import pallas as pl
from jax.experimental.pallas import tpu as pltpu

F32 = jnp.float32
BF16 = jnp.bfloat16

D_MODEL = 2048
SSD_HEADS = 16
SSD_HEADDIM = 64
SSD_INNER = SSD_HEADS * SSD_HEADDIM
SSD_GROUPS = 2
SSD_STATE = 128
CONV_W = 4
CONV_DIM = SSD_INNER + 2 * SSD_GROUPS * SSD_STATE
SSD_CHUNK = 128
ATT_HEADS = 16
ATT_KV_HEADS = 4
HEAD_DIM = 64
ATT_GQ = ATT_HEADS // ATT_KV_HEADS
ATT_INNER = ATT_HEADS * HEAD_DIM
KV_DIM = ATT_KV_HEADS * HEAD_DIM
ROT_DIM = HEAD_DIM // 4
ROPE_THETA = 500000.0
DILATED_BRANCHES = ((128, 1), (512, 4), (2048, 16))
W_MAX = 2048
PAST_LEN = 16384
EPS = 1e-6

LANES = 128
SUBLANES = 8
Q_BLOCK = 256
K_SUPER = 256
Q_PER_SUPER = K_SUPER // Q_BLOCK
KEY_BLOCKS_PER_BODY = 2
ONES_ROWS = 16
ACC_ROWS = HEAD_DIM + ONES_ROWS
CONV_COLS = 256
TM_IN = 512
TM_OUT = 512
SSD_SEQS = 4
SAMPLE_PAD = 16
NEG = -1e30
VMEM_LIMIT = 58 * 1024 * 1024

Z0 = 0
X0 = Z0 + SSD_INNER
DT0 = X0 + CONV_DIM
Q0 = DT0 + SSD_HEADS
K0 = Q0 + ATT_INNER
V0 = K0 + KV_DIM
G0 = V0 + KV_DIM
W_ROWS = G0 + ATT_INNER
W_CHUNK = 1024
WO_CHUNK = 512
HALF_INNER = SSD_INNER // SSD_GROUPS
LOG2E = 1.4426950408889634
Q_SCALE = HEAD_DIM ** -0.5 * LOG2E


def _dot(a, b):
    return jnp.dot(a, b, preferred_element_type=F32)


def _dot_nt(a, b):
    return lax.dot_general(a, b, (((1,), (1,)), ((), ())), preferred_element_type=F32)


def _split3(x):
    hi = x.astype(BF16)
    r1 = x - hi.astype(F32)
    mid = r1.astype(BF16)
    lo = (r1 - mid.astype(F32)).astype(BF16)
    return hi, mid, lo


def _dot_exact_rhs(x, m):
    hi, mid, lo = _split3(x)
    return _dot(hi, m) + _dot(mid, m) + _dot(lo, m)


def _dot_wide_rhs(x, m):
    hi = x.astype(BF16)
    lo = (x - hi.astype(F32)).astype(BF16)
    return _dot(hi, m) + _dot(lo, m)


def _dot_exact_lhs(m, x):
    hi, mid, lo = _split3(x)
    return _dot(m, hi) + _dot(m, mid) + _dot(m, lo)


def _silu(x):
    hx = 0.5 * x
    return hx * jnp.tanh(hx) + hx


def _multiplicity(d):
    w = jnp.zeros(d.shape, F32)
    for window, dil in DILATED_BRANCHES:
        hit = (d >= 0) & (d <= window) & (lax.rem(d, dil) == 0)
        w = w + jnp.where(hit, 1.0, 0.0)
    return w


def _norm_rope(y, nw, cos, sa, sb, scale):
    lane = lax.broadcasted_iota(jnp.int32, (1, LANES), 1)
    first = lane < HEAD_DIM
    y2 = y * y
    s_lo = jnp.sum(jnp.where(first, y2, 0.0), axis=-1, keepdims=True)
    s_hi = jnp.sum(jnp.where(first, 0.0, y2), axis=-1, keepdims=True)
    ms = jnp.where(first, s_lo, s_hi) * (1.0 / HEAD_DIM)
    yn = y * lax.rsqrt(ms + EPS) * nw
    half = ROT_DIM // 2
    rot = yn * cos + pltpu.roll(yn, half, 1) * sa + pltpu.roll(yn, LANES - half, 1) * sb
    return rot * scale


def _causal_conv_silu(load, cw_ref, cb_ref, store):
    for c0 in range(0, CONV_DIM, CONV_COLS):
        cs = slice(c0, c0 + CONV_COLS)
        xe = load(cs)
        x2 = pltpu.roll(xe, 2, 0)
        even = cw_ref[3:4, cs] * xe + cw_ref[1:2, cs] * x2
        odd = cw_ref[2:3, cs] * xe + cw_ref[0:1, cs] * x2
        store(cs, _silu((even + pltpu.roll(odd, 1, 0))[SUBLANES:, :] + cb_ref[:, cs]))


def _pre_norm(x_ref, nw_ref):
    x = x_ref[...]
    ms = jnp.mean(x * x, axis=-1, keepdims=True)
    return (x * lax.rsqrt(ms + EPS) * nw_ref[...]).astype(BF16)


def _inproj_kernel(x_ref, nw_ref, w_ref, *refs):
    hn = _pre_norm(x_ref, nw_ref)

    def proj(a, b):
        return _dot_nt(hn, w_ref[a:b, :])

    _emit_sections(proj, x_ref.shape[0], *refs)


def _emit_sections(proj, tm, qnw_ref, knw_ref, cos_ref, sa_ref, sb_ref,
                   zs_ref, xbc_ref, dt_ref, q_ref, kt_ref, vt_ref, gs_ref, ctail_ref):
    cos, sa, sb = cos_ref[...], sa_ref[...], sb_ref[...]
    q = proj(Q0, K0)
    for c in range(ATT_INNER // LANES):
        sl = slice(c * LANES, (c + 1) * LANES)
        qn = _norm_rope(q[:, sl], qnw_ref[...], cos, sa, sb, Q_SCALE)
        if len(q_ref.shape) == 3:
            q_ref[0, sl, :] = qn.T.astype(BF16)
        else:
            q_ref[:, sl] = qn.astype(BF16)
    k = proj(K0, V0)
    kn = [_norm_rope(k[:, c * LANES:(c + 1) * LANES], knw_ref[...], cos, sa, sb, 1.0)
          for c in range(KV_DIM // LANES)]
    kt_ref[0] = jnp.concatenate(kn, axis=1).T

    xbc = proj(X0, DT0)
    xbc_ref[...] = xbc
    ctail_ref[0] = xbc[tm - SUBLANES:tm, :]
    vt_ref[0] = proj(V0, G0).T
    zs_ref[...] = _silu(proj(Z0, X0))
    gs_ref[...] = _silu(proj(G0, W_ROWS))
    dt_ref[...] = proj(DT0, DT0 + LANES)


def _in_proj(x2d, nw, w_t, qnw, knw, tables, tm, seq):
    rows = x2d.shape[0]
    const = lambda i: (0, 0)
    row = lambda i: (i, 0)
    sec_in, out_specs, out_shape = _section_specs(rows, tm, seq, tables[0].shape[0] // tm, True)
    return pl.pallas_call(
        _inproj_kernel,
        grid=(rows // tm,),
        in_specs=[
            pl.BlockSpec((tm, D_MODEL), row),
            pl.BlockSpec((1, D_MODEL), const),
            pl.BlockSpec((W_ROWS, D_MODEL), const, pipeline_mode=pl.Buffered(1)),
        ] + sec_in,
        out_specs=out_specs,
        out_shape=out_shape,
        compiler_params=pltpu.CompilerParams(
            dimension_semantics=("arbitrary",), vmem_limit_bytes=VMEM_LIMIT),
        name="in_proj",
    )(x2d, nw, w_t, qnw, knw, *tables)


def _section_specs(rows, tm, seq, period, q_transposed):
    per_seq = seq // tm
    const = lambda i: (0, 0)
    row = lambda i: (i, 0)
    tab = lambda i: (i % period, 0)
    tr = lambda i: (i // per_seq, 0, i % per_seq)
    slab = lambda i: (i // per_seq, 0, 0)
    in_specs = [pl.BlockSpec((1, LANES), const)] * 2 + [pl.BlockSpec((tm, LANES), tab)] * 3
    row_outs = {0: (SSD_INNER, F32), 1: (CONV_DIM, F32), 2: (LANES, F32), 3: (ATT_INNER, BF16),
                6: (ATT_INNER, F32)}
    out_specs, out_shape = [], []
    for idx in range(7):
        if idx == 3 and q_transposed:
            out_specs.append(pl.BlockSpec((1, ATT_INNER, tm), tr))
            out_shape.append(jax.ShapeDtypeStruct((rows // seq, ATT_INNER, seq), BF16))
        elif idx in row_outs:
            n, dt = row_outs[idx]
            out_specs.append(pl.BlockSpec((tm, n), row))
            out_shape.append(jax.ShapeDtypeStruct((rows, n), dt))
        else:
            out_specs.append(pl.BlockSpec((1, KV_DIM, tm), tr))
            out_shape.append(jax.ShapeDtypeStruct((rows // seq, KV_DIM, seq), F32))
    out_specs.append(pl.BlockSpec((1, SUBLANES, CONV_DIM), slab))
    out_shape.append(jax.ShapeDtypeStruct((rows // seq, SUBLANES, CONV_DIM), F32))
    return in_specs, out_specs, out_shape


def _wconv_in_kernel(x_ref, nw_ref, w_ref, wb_ref, p_ref, hn_scr):
    @pl.when(pl.program_id(0) == 0)
    def _():
        hn_scr[...] = _pre_norm(x_ref, nw_ref)

    row = pl.program_id(0) * W_CHUNK + lax.broadcasted_iota(jnp.int32, (W_CHUNK, 1), 0)
    wb = jnp.where(row < W_ROWS, w_ref[...], 0.0).astype(BF16)
    wb_ref[...] = wb
    p_ref[0] = _dot_nt(hn_scr[...], wb)


def _wconv_in(x2d, nw, w_f32):
    rows = x2d.shape[0]
    n_chunks = pl.cdiv(W_ROWS, W_CHUNK)
    const = lambda j: (0, 0)
    return pl.pallas_call(
        _wconv_in_kernel,
        grid=(n_chunks,),
        in_specs=[
            pl.BlockSpec((rows, D_MODEL), const),
            pl.BlockSpec((1, D_MODEL), const),
            pl.BlockSpec((W_CHUNK, D_MODEL), lambda j: (j, 0)),
        ],
        out_specs=[
            pl.BlockSpec((W_CHUNK, D_MODEL), lambda j: (j, 0)),
            pl.BlockSpec((1, rows, W_CHUNK), lambda j: (j, 0, 0)),
        ],
        out_shape=[
            jax.ShapeDtypeStruct((W_ROWS, D_MODEL), BF16),
            jax.ShapeDtypeStruct((n_chunks, rows, W_CHUNK), F32),
        ],
        scratch_shapes=[pltpu.VMEM((rows, D_MODEL), BF16)],
        compiler_params=pltpu.CompilerParams(
            dimension_semantics=("arbitrary",), vmem_limit_bytes=VMEM_LIMIT),
        name="wconv_in",
    )(x2d, nw, w_f32)


def _sections_kernel(p_ref, *refs):
    *refs, tail_scr = refs
    pfull = jnp.concatenate([p_ref[j] for j in range(p_ref.shape[0])], axis=1)
    tail_scr[...] = pfull[:, Q0:W_ROWS]

    def proj(a, b):
        if a >= Q0:
            return tail_scr[:, a - Q0:b - Q0]
        return pfull[:, a:b]

    _emit_sections(proj, p_ref.shape[1], *refs)


def _sections(p, qnw, knw, tables):
    rows = p.shape[1]
    sec_in, out_specs, out_shape = _section_specs(rows, rows, rows, 1, False)
    return pl.pallas_call(
        _sections_kernel,
        grid=(1,),
        in_specs=[pl.BlockSpec(p.shape, lambda i: (0, 0, 0))] + sec_in,
        out_specs=out_specs,
        out_shape=out_shape,
        scratch_shapes=[pltpu.VMEM((rows, W_ROWS - Q0), F32)],
        compiler_params=pltpu.CompilerParams(
            dimension_semantics=("arbitrary",), vmem_limit_bytes=VMEM_LIMIT),
        name="sections",
    )(p, qnw, knw, *tables)


def _ssd_chain(s, xbc_ref, dtr_ref, z_ref, cprev_ref, h0_ref, cw_ref, cb_ref, dtb_ref, alog_ref,
               dexp_ref, nw_ref, y_ref, hout_ref, xext, ht, xc_scr, lin, valid):
    L = SSD_CHUNK
    c = pl.program_id(1)
    last = pl.num_programs(1) - 1

    @pl.when(c == 0)
    def _():
        xext[s, 0:SUBLANES, :] = cprev_ref[s]
        for g in range(SSD_GROUPS):
            ht[s, g] = h0_ref[s, g * HALF_INNER:(g + 1) * HALF_INNER, :].T

    def pad_rows(v):
        if lin == L:
            return v
        return jnp.concatenate([v, jnp.zeros((L - lin, v.shape[1]), v.dtype)], axis=0)

    lane = lax.broadcasted_iota(jnp.int32, (1, LANES), 1)
    rowid = lax.broadcasted_iota(jnp.int32, (L, 1), 0)
    dt = jax.nn.softplus(pad_rows(dtr_ref[s]) + dtb_ref[...])
    dt = jnp.where((lane < SSD_HEADS) & (rowid < valid), dt, 0.0)
    dta = dt * (-jnp.exp(alog_ref[...]) * LOG2E)

    r2 = lax.broadcasted_iota(jnp.int32, (L, L), 0)
    c2 = lax.broadcasted_iota(jnp.int32, (L, L), 1)
    tri = r2 >= c2
    tri_b = jnp.where(tri, 1.0, 0.0).astype(BF16)
    cum = _dot_exact_lhs(tri_b, dta)
    cum_last = cum[L - 1:L, :]
    yield

    er = lax.broadcasted_iota(jnp.int32, (LANES, SSD_INNER), 0)
    ec = lax.broadcasted_iota(jnp.int32, (LANES, SSD_INNER), 1)
    expand = jnp.where(ec // SSD_HEADDIM == er, 1.0, 0.0).astype(BF16)
    ecum = _dot_wide_rhs(jnp.exp2(cum), expand)
    wexp = _dot_wide_rhs(jnp.exp2(cum_last - cum) * dt, expand)
    cd = _dot_exact_rhs(jnp.broadcast_to(jnp.exp2(cum_last), (SUBLANES, LANES)), expand)[0:1, :]
    cum_t = cum.T
    dt_t = dt.T
    yield

    if lin < L:
        xext[s, SUBLANES + lin:SUBLANES + L, :] = jnp.zeros((L - lin, CONV_DIM), F32)
    xext[s, SUBLANES:SUBLANES + lin, :] = xbc_ref[s]

    def load(cs):
        return xext[s, 0:SUBLANES + L, cs]

    def store(cs, v):
        xc_scr[s, :, cs] = v

    _causal_conv_silu(load, cw_ref, cb_ref, store)
    xext[s, 0:SUBLANES, :] = xext[s, L:L + SUBLANES, :]
    xc = xc_scr.at[s]
    xs = xc[:, :SSD_INNER]
    bm = xc[:, SSD_INNER:SSD_INNER + SSD_GROUPS * SSD_STATE]
    cm = xc[:, SSD_INNER + SSD_GROUPS * SSD_STATE:]
    bm_b = bm.astype(BF16)
    cm_b = cm.astype(BF16)
    xs_b = xs.astype(BF16)

    y_off = jnp.concatenate(
        [_dot(cm_b[:, g * SSD_STATE:(g + 1) * SSD_STATE], ht[s, g].astype(BF16))
         for g in range(SSD_GROUPS)], axis=1) * ecum
    cbs = [_dot_nt(cm_b[:, g * SSD_STATE:(g + 1) * SSD_STATE], bm_b[:, g * SSD_STATE:(g + 1) * SSD_STATE])
           for g in range(SSD_GROUPS)]
    yield

    lane_half = lane // HEAD_DIM
    heads_per_group = SSD_HEADS // SSD_GROUPS
    y_parts = []
    for g in range(SSD_GROUPS):
        for pr in range(heads_per_group // 2):
            col0 = g * HALF_INNER + pr * LANES
            xp = xs_b[:, col0:col0 + LANES]
            yp = jnp.zeros((L, LANES), F32)
            for e in range(2):
                hh = g * heads_per_group + pr * 2 + e
                decay = jnp.exp2(cum[:, hh:hh + 1] - cum_t[hh:hh + 1, :])
                sc = jnp.where(tri, cbs[g] * decay, 0.0) * dt_t[hh:hh + 1, :]
                xm = jnp.where(lane_half == e, xp, jnp.zeros_like(xp))
                yp = yp + _dot(sc.astype(BF16), xm)
            y_parts.append(yp)
            if pr % 2 == 1:
                yield
    y_diag = jnp.concatenate(y_parts, axis=1)

    xw = (xs * wexp).astype(BF16)
    for g in range(SSD_GROUPS):
        sl = slice(g * HALF_INNER, (g + 1) * HALF_INNER)
        bm_t = bm[:, g * SSD_STATE:(g + 1) * SSD_STATE].T.astype(BF16)
        ht[s, g] = ht[s, g] * cd[:, sl] + _dot(bm_t, xw[:, sl])
    yield

    y = (y_diag + y_off + dexp_ref[...] * xs) * pad_rows(z_ref[s])
    yn = []
    for g in range(SSD_GROUPS):
        yg = y[:, g * HALF_INNER:(g + 1) * HALF_INNER]
        ms = jnp.mean(yg * yg, axis=-1, keepdims=True)
        yn.append(yg * lax.rsqrt(ms + EPS))
    yn = jnp.concatenate(yn, axis=1) * nw_ref[...]
    y_ref[s] = yn[:lin].astype(BF16)

    @pl.when(c == last)
    def _():
        for g in range(SSD_GROUPS):
            hout_ref[s, g * HALF_INNER:(g + 1) * HALF_INNER, :] = ht[s, g].T


def _ssd_kernel(*refs, lin, valid):
    chains = [_ssd_chain(s, *refs, lin, valid) for s in range(SSD_SEQS)]
    while chains:
        alive = []
        for ch in chains:
            try:
                next(ch)
                alive.append(ch)
            except StopIteration:
                pass
        chains = alive


def _ssd(xbc, dtr, z, cprev, h0, cw, cb, dtb, alog, dexp, nw, batch, lin, valid):
    rows = xbc.shape[0]
    seq = rows // batch
    nc = seq // lin
    g = SSD_SEQS
    assert batch % g == 0
    blk = lambda b, c: (b, c, 0)
    per_b = lambda b, c: (b, 0, 0)
    const = lambda b, c: (0, 0)
    y, hout = pl.pallas_call(
        functools.partial(_ssd_kernel, lin=lin, valid=valid),
        grid=(batch // g, nc),
        in_specs=[
            pl.BlockSpec((g, lin, CONV_DIM), blk),
            pl.BlockSpec((g, lin, LANES), blk),
            pl.BlockSpec((g, lin, SSD_INNER), blk),
            pl.BlockSpec((g, SUBLANES, CONV_DIM), per_b),
            pl.BlockSpec((g, SSD_INNER, SSD_STATE), per_b),
            pl.BlockSpec((SUBLANES, CONV_DIM), const),
            pl.BlockSpec((1, CONV_DIM), const),
            pl.BlockSpec((1, LANES), const),
            pl.BlockSpec((1, LANES), const),
            pl.BlockSpec((1, SSD_INNER), const),
            pl.BlockSpec((1, SSD_INNER), const),
        ],
        out_specs=[
            pl.BlockSpec((g, lin, SSD_INNER), blk),
            pl.BlockSpec((g, SSD_INNER, SSD_STATE), per_b),
        ],
        out_shape=[
            jax.ShapeDtypeStruct((batch, seq, SSD_INNER), BF16),
            jax.ShapeDtypeStruct((batch, SSD_INNER, SSD_STATE), F32),
        ],
        scratch_shapes=[
            pltpu.VMEM((g, SUBLANES + SSD_CHUNK + SUBLANES, CONV_DIM), F32),
            pltpu.VMEM((g, SSD_GROUPS, SSD_STATE, HALF_INNER), F32),
            pltpu.VMEM((g, SSD_CHUNK, CONV_DIM), F32),
        ],
        compiler_params=pltpu.CompilerParams(
            dimension_semantics=("arbitrary", "arbitrary"), vmem_limit_bytes=VMEM_LIMIT),
        name="ssd",
    )(xbc.reshape(batch, seq, CONV_DIM), dtr.reshape(batch, seq, LANES),
      z.reshape(batch, seq, SSD_INNER), cprev, h0, cw, cb, dtb, alog, dexp, nw)
    return y.reshape(rows, SSD_INNER), hout


def _attn_prompt_kernel(q_ref, kt_ref, vt_ref, g_ref, o_ref,
                        bias_scr, kh_scr, va_scr, qt_scr, m_scr, acc_scr, ot_scr, s_scr, *, n_tab, n_sb_max):
    b = pl.program_id(0)
    i = pl.program_id(1)
    seq = kt_ref.shape[2]
    pair_w = 2 * Q_BLOCK
    pairs_per_kv = ATT_GQ // 2

    @pl.when((b == 0) & (i == 0))
    def _():
        r = lax.broadcasted_iota(jnp.int32, (K_SUPER, Q_BLOCK), 0)
        c = lax.broadcasted_iota(jnp.int32, (K_SUPER, Q_BLOCK), 1)
        for tb in range(n_tab):
            w = _multiplicity(tb * Q_BLOCK + c - r)
            bias_scr[tb] = jnp.where(w > 0.0, jnp.log2(jnp.maximum(w, 1.0)), NEG)
        va_scr[:, HEAD_DIM:, :] = jnp.ones((ATT_KV_HEADS, ONES_ROWS, seq), BF16)

    @pl.when(i == 0)
    def _():
        for kvh in range(ATT_KV_HEADS):
            hs = slice(kvh * HEAD_DIM, (kvh + 1) * HEAD_DIM)
            va_scr[kvh, 0:HEAD_DIM, :] = vt_ref[0, hs, :].astype(BF16)
        for cidx in range(seq // K_SUPER):
            rows = slice(cidx * K_SUPER, (cidx + 1) * K_SUPER)
            kc = kt_ref[0, :, rows].T
            for kvh in range(ATT_KV_HEADS):
                hs = slice(kvh * HEAD_DIM, (kvh + 1) * HEAD_DIM)
                kh_scr[kvh, rows, :] = kc[:, hs].astype(BF16)

    for kvh in range(ATT_KV_HEADS):
        qt_scr[kvh] = jnp.concatenate(
            [q_ref[0, (kvh * ATT_GQ + g) * HEAD_DIM:(kvh * ATT_GQ + g + 1) * HEAD_DIM, :]
             for g in range(ATT_GQ)], axis=1)
    m_scr[...] = jnp.full(m_scr.shape, NEG, F32)
    acc_scr[...] = jnp.zeros(acc_scr.shape, F32)
    par = i % Q_PER_SUPER
    j_last = i // Q_PER_SUPER
    n_sb = jnp.minimum(j_last + 1, n_sb_max)

    def key_start(dl):
        return pl.multiple_of((j_last - dl) * K_SUPER, K_SUPER)

    def scores(dl, dst):
        for kvh in range(ATT_KV_HEADS):
            dst[kvh] = _dot(kh_scr[kvh, pl.ds(key_start(dl), K_SUPER), :], qt_scr[kvh])

    def softmax_pv(dl, src):
        start = key_start(dl)
        bias = bias_scr[par + Q_PER_SUPER * dl]
        bias2 = jnp.concatenate([bias, bias], axis=1)
        for kvh in range(ATT_KV_HEADS):
            vtb = va_scr[kvh, :, pl.ds(start, K_SUPER)]
            for pr in range(pairs_per_kv):
                u = kvh * pairs_per_kv + pr
                s = src[kvh, :, pr * pair_w:(pr + 1) * pair_w] + bias2
                m_old = m_scr[u, 0:1, :]
                m_new = jnp.maximum(m_old, jnp.max(s, axis=0, keepdims=True))
                p = jnp.exp2(s - m_new).astype(BF16)
                acc_scr[u] = jnp.exp2(m_old - m_new) * acc_scr[u] + _dot(vtb, p)
                m_scr[u, 0:1, :] = m_new

    def group(dl0, n):
        for e in range(n):
            scores(dl0 + e, s_scr.at[e])
        for e in range(n):
            softmax_pv(dl0 + e, s_scr.at[e])

    def body(t, carry):
        group(KEY_BLOCKS_PER_BODY * t, KEY_BLOCKS_PER_BODY)
        return carry

    n_full = n_sb // KEY_BLOCKS_PER_BODY
    lax.fori_loop(0, n_full, body, 0)
    done = n_full * KEY_BLOCKS_PER_BODY
    size = KEY_BLOCKS_PER_BODY // 2
    while size >= 1:
        take = ((n_sb - done) // size) % 2 == 1

        @pl.when(take)
        def _(done=done, size=size):
            group(done, size)

        done = done + jnp.where(take, size, 0)
        size //= 2

    for u in range(ATT_HEADS // 2):
        acc = acc_scr[u]
        o = acc[0:HEAD_DIM] / acc[HEAD_DIM:HEAD_DIM + 1]
        for e in range(2):
            h = 2 * u + e
            ot_scr[h * HEAD_DIM:(h + 1) * HEAD_DIM, :] = o[:, e * Q_BLOCK:(e + 1) * Q_BLOCK]
    o_ref[...] = (ot_scr[...].T * g_ref[...]).astype(BF16)


def _attn_prompt(qt, kt, vt, gate, batch, seq):
    rows = batch * seq
    nqb = seq // Q_BLOCK
    assert seq % K_SUPER == 0
    n_sb_max = min(seq // K_SUPER, W_MAX // K_SUPER + 1)
    n_tab = Q_PER_SUPER * n_sb_max
    qblk = lambda b, i: (b * nqb + i, 0)
    per_b = lambda b, i: (b, 0, 0)
    return pl.pallas_call(
        functools.partial(_attn_prompt_kernel, n_tab=n_tab, n_sb_max=n_sb_max),
        grid=(batch, nqb),
        in_specs=[
            pl.BlockSpec((1, ATT_INNER, Q_BLOCK), lambda b, i: (b, 0, i)),
            pl.BlockSpec((1, KV_DIM, seq), per_b),
            pl.BlockSpec((1, KV_DIM, seq), per_b),
            pl.BlockSpec((Q_BLOCK, ATT_INNER), qblk),
        ],
        out_specs=pl.BlockSpec((Q_BLOCK, ATT_INNER), qblk),
        out_shape=jax.ShapeDtypeStruct((rows, ATT_INNER), BF16),
        scratch_shapes=[
            pltpu.VMEM((n_tab, K_SUPER, Q_BLOCK), F32),
            pltpu.VMEM((ATT_KV_HEADS, seq, HEAD_DIM), BF16),
            pltpu.VMEM((ATT_KV_HEADS, ACC_ROWS, seq), BF16),
            pltpu.VMEM((ATT_KV_HEADS, HEAD_DIM, ATT_GQ * Q_BLOCK), BF16),
            pltpu.VMEM((ATT_HEADS // 2, SUBLANES, 2 * Q_BLOCK), F32),
            pltpu.VMEM((ATT_HEADS // 2, ACC_ROWS, 2 * Q_BLOCK), F32),
            pltpu.VMEM((ATT_INNER, Q_BLOCK), F32),
            pltpu.VMEM((KEY_BLOCKS_PER_BODY, ATT_KV_HEADS, K_SUPER, ATT_GQ * Q_BLOCK), F32),
        ],
        compiler_params=pltpu.CompilerParams(
            dimension_semantics=("arbitrary", "arbitrary"), vmem_limit_bytes=VMEM_LIMIT),
        name="attn_prompt",
    )(qt, kt, vt, gate)


def _attn_sample_kernel(q_ref, knt_ref, vnt_ref, g_ref, ckt_ref, cvt_ref, o_ref, kot_ref, vot_ref, *, ts):
    win = ckt_ref.shape[2]
    tp = SAMPLE_PAD
    rows = ATT_GQ * tp
    t_c = lax.broadcasted_iota(jnp.int32, (rows, win), 0) % tp
    j_c = lax.broadcasted_iota(jnp.int32, (rows, win), 1)
    w_c = _multiplicity(win + t_c - j_c)
    t_n = lax.broadcasted_iota(jnp.int32, (rows, tp), 0) % tp
    j_n = lax.broadcasted_iota(jnp.int32, (rows, tp), 1)
    w_n = jnp.where(j_n < ts, _multiplicity(t_n - j_n), 0.0)
    for kvh in range(ATT_KV_HEADS):
        c0 = kvh * ATT_GQ * HEAD_DIM
        qh = jnp.concatenate(
            [q_ref[:, c0 + g * HEAD_DIM:c0 + (g + 1) * HEAD_DIM] for g in range(ATT_GQ)], axis=0)
        hs = slice(kvh * HEAD_DIM, (kvh + 1) * HEAD_DIM)
        s_c = jnp.where(w_c > 0.0, _dot(qh, ckt_ref[0, hs, :].astype(BF16)), NEG)
        s_n = jnp.where(w_n > 0.0, _dot(qh, knt_ref[0, hs, :].astype(BF16)), NEG)
        m = jnp.maximum(jnp.max(s_c, axis=-1, keepdims=True), jnp.max(s_n, axis=-1, keepdims=True))
        p_c = jnp.exp2(s_c - m) * w_c
        p_n = jnp.exp2(s_n - m) * w_n
        den = jnp.sum(p_c, axis=-1, keepdims=True) + jnp.sum(p_n, axis=-1, keepdims=True)
        num = (_dot_nt(p_c.astype(BF16), cvt_ref[0, hs, :].astype(BF16))
               + _dot_nt(p_n.astype(BF16), vnt_ref[0, hs, :].astype(BF16)))
        o = num / den
        o = jnp.concatenate([o[g * tp:(g + 1) * tp] for g in range(ATT_GQ)], axis=1)
        csl = slice(c0, c0 + ATT_GQ * HEAD_DIM)
        o_ref[:, csl] = (o * g_ref[:, csl]).astype(BF16)
    kot_ref[0, :, 0:win - ts] = ckt_ref[0, :, ts:win]
    kot_ref[0, :, win - ts:win] = knt_ref[0, :, 0:ts]
    vot_ref[0, :, 0:win - ts] = cvt_ref[0, :, ts:win]
    vot_ref[0, :, win - ts:win] = vnt_ref[0, :, 0:ts]


def _attn_sample(q, knt, vnt, gate, ckt, cvt, ts):
    batch, _, win = ckt.shape
    tp = SAMPLE_PAD
    blk = lambda b: (b, 0)
    per_b = lambda b: (b, 0, 0)
    return pl.pallas_call(
        functools.partial(_attn_sample_kernel, ts=ts),
        grid=(batch,),
        in_specs=[
            pl.BlockSpec((tp, ATT_INNER), blk),
            pl.BlockSpec((1, KV_DIM, tp), per_b),
            pl.BlockSpec((1, KV_DIM, tp), per_b),
            pl.BlockSpec((tp, ATT_INNER), blk),
            pl.BlockSpec((1, KV_DIM, win), per_b),
            pl.BlockSpec((1, KV_DIM, win), per_b),
        ],
        out_specs=[
            pl.BlockSpec((tp, ATT_INNER), blk),
            pl.BlockSpec((1, KV_DIM, win), per_b),
            pl.BlockSpec((1, KV_DIM, win), per_b),
        ],
        out_shape=[
            jax.ShapeDtypeStruct((batch * tp, ATT_INNER), BF16),
            jax.ShapeDtypeStruct((batch, KV_DIM, win), F32),
            jax.ShapeDtypeStruct((batch, KV_DIM, win), F32),
        ],
        compiler_params=pltpu.CompilerParams(
            dimension_semantics=("arbitrary",), vmem_limit_bytes=VMEM_LIMIT),
        name="attn_sample",
    )(q, knt, vnt, gate, ckt, cvt)


def _outproj_kernel(x_ref, ys_ref, ya_ref, w_ref, o_ref):
    o_ref[...] = (x_ref[...] + _dot(ys_ref[...], w_ref[0:SSD_INNER, :])
                  + _dot(ya_ref[...], w_ref[SSD_INNER:, :]))


def _out_proj(x2d, y_ssd, y_att, w_o, tm):
    rows = x2d.shape[0]
    row = lambda i: (i, 0)
    const = lambda i: (0, 0)
    return pl.pallas_call(
        _outproj_kernel,
        grid=(rows // tm,),
        in_specs=[
            pl.BlockSpec((tm, D_MODEL), row),
            pl.BlockSpec((tm, SSD_INNER), row),
            pl.BlockSpec((tm, ATT_INNER), row),
            pl.BlockSpec((SSD_INNER + ATT_INNER, D_MODEL), const, pipeline_mode=pl.Buffered(1)),
        ],
        out_specs=pl.BlockSpec((tm, D_MODEL), row),
        out_shape=jax.ShapeDtypeStruct((rows, D_MODEL), F32),
        compiler_params=pltpu.CompilerParams(
            dimension_semantics=("arbitrary",), vmem_limit_bytes=VMEM_LIMIT),
        name="out_proj",
    )(x2d, y_ssd, y_att, w_o)


def _wconv_out_kernel(x_ref, ys_ref, ya_ref, w_ref, wb_ref, o_ref):
    j = pl.program_id(0)
    wb = w_ref[...].astype(BF16)
    wb_ref[...] = wb
    y = jnp.where(j < SSD_INNER // WO_CHUNK, ys_ref[...], ya_ref[...])
    part = _dot(y, wb)

    @pl.when(j == 0)
    def _():
        o_ref[...] = x_ref[...] + part

    @pl.when(j > 0)
    def _():
        o_ref[...] += part


def _wconv_out(x2d, y_ssd, y_att, w_f32):
    rows = x2d.shape[0]
    n_ssd = SSD_INNER // WO_CHUNK
    n_chunks = (SSD_INNER + ATT_INNER) // WO_CHUNK
    const = lambda j: (0, 0)
    return pl.pallas_call(
        _wconv_out_kernel,
        grid=(n_chunks,),
        in_specs=[
            pl.BlockSpec((rows, D_MODEL), const),
            pl.BlockSpec((rows, WO_CHUNK), lambda j: (0, jnp.minimum(j, n_ssd - 1))),
            pl.BlockSpec((rows, WO_CHUNK), lambda j: (0, jnp.maximum(j - n_ssd, 0))),
            pl.BlockSpec((WO_CHUNK, D_MODEL), lambda j: (j, 0)),
        ],
        out_specs=[
            pl.BlockSpec((WO_CHUNK, D_MODEL), lambda j: (j, 0)),
            pl.BlockSpec((rows, D_MODEL), const),
        ],
        out_shape=[
            jax.ShapeDtypeStruct((SSD_INNER + ATT_INNER, D_MODEL), BF16),
            jax.ShapeDtypeStruct((rows, D_MODEL), F32),
        ],
        compiler_params=pltpu.CompilerParams(
            dimension_semantics=("arbitrary",), vmem_limit_bytes=VMEM_LIMIT),
        name="wconv_out",
    )(x2d, y_ssd, y_att, w_f32)


def _rope_tables(pos):
    n = pos.shape[0]
    half = ROT_DIM // 2
    inv = ROPE_THETA ** (-np.arange(0, ROT_DIM, 2, dtype=np.float64) / ROT_DIM)
    ang = pos.astype(np.float64)[:, None] * inv[None, :]
    cos, sin = np.cos(ang), np.sin(ang)
    rest = HEAD_DIM - ROT_DIM
    zh = np.zeros((n, half))
    cos_h = np.concatenate([cos, cos, np.ones((n, rest))], axis=1)
    sa_h = np.concatenate([zh, sin, np.zeros((n, rest))], axis=1)
    sb_h = np.concatenate([-sin, zh, np.zeros((n, rest))], axis=1)
    rep = LANES // HEAD_DIM
    return tuple(jnp.asarray(np.tile(t, (1, rep)), dtype=F32) for t in (cos_h, sa_h, sb_h))


def _lane_pad(v, n=LANES):
    return jnp.pad(v, (0, n - v.shape[0])).reshape(1, n)


def _to_cache(xt, batch, seq):
    return xt.reshape(1, batch, ATT_KV_HEADS, HEAD_DIM, seq).transpose(0, 1, 4, 2, 3)


def kernel(x_prompt, x_sample, cache_k, cache_v, state_conv, state_ssm, norm_w, w_in, conv_w,
           conv_b, dt_bias, a_log, d_skip, ssd_norm_w, q_norm_w, k_norm_w, w_out):
    bp, tp_, _ = x_prompt.shape
    bs, ts, _ = x_sample.shape
    depth = w_in.shape[0]
    assert depth == 1 and tp_ % SSD_CHUNK == 0 and ts <= SAMPLE_PAD and ts >= CONV_W - 1
    l = 0
    win = cache_k.shape[2]

    nw = norm_w[l].reshape(1, D_MODEL)
    rep = LANES // HEAD_DIM
    qnw = jnp.tile(q_norm_w[l], rep).reshape(1, LANES)
    knw = jnp.tile(k_norm_w[l], rep).reshape(1, LANES)
    cw = jnp.pad(conv_w[l], ((0, SUBLANES - CONV_W), (0, 0)))
    cb = conv_b[l].reshape(1, CONV_DIM)
    dtb = _lane_pad(dt_bias[l])
    alog = _lane_pad(a_log[l])
    dexp = jnp.repeat(d_skip[l], SSD_HEADDIM).reshape(1, SSD_INNER)
    snw = ssd_norm_w[l].reshape(1, SSD_INNER)

    pad = SAMPLE_PAD
    xs2 = jnp.pad(x_sample, ((0, 0), (0, pad - ts), (0, 0))).reshape(bs * pad, D_MODEL)
    tabs_s = _rope_tables(PAST_LEN + np.arange(bs * pad) % pad)
    w_t, proj_s = _wconv_in(xs2, nw, jnp.swapaxes(w_in[l], 0, 1))
    zs, xbc, dtr, q, kt, vt, gs, _ = _sections(proj_s, qnw, knw, tabs_s)
    cprev = jnp.pad(state_conv[l], ((0, 0), (SUBLANES - (CONV_W - 1), 0), (0, 0)))
    y_ssd, h_s = _ssd(xbc, dtr, zs, cprev, state_ssm[l].reshape(bs, SSD_INNER, SSD_STATE),
                      cw, cb, dtb, alog, dexp, snw, bs, pad, ts)
    knt = kt.reshape(KV_DIM, bs, pad).transpose(1, 0, 2)
    vnt = vt.reshape(KV_DIM, bs, pad).transpose(1, 0, 2)
    ckt = cache_k[l].transpose(0, 2, 3, 1).reshape(bs, KV_DIM, win)
    cvt = cache_v[l].transpose(0, 2, 3, 1).reshape(bs, KV_DIM, win)
    y_att, kot, vot = _attn_sample(q, knt, vnt, gs, ckt, cvt, ts)
    w_o, y_s = _wconv_out(xs2, y_ssd, y_att, w_out[l])
    y_s = y_s.reshape(bs, pad, D_MODEL)[:, :ts]
    k_s = _to_cache(kot, bs, win)
    v_s = _to_cache(vot, bs, win)
    c_s = xbc.reshape(bs, pad, CONV_DIM)[:, ts - (CONV_W - 1):ts][None]
    h_s = h_s.reshape(1, bs, SSD_HEADS, SSD_HEADDIM, SSD_STATE)

    tm = TM_IN
    xp2 = x_prompt.reshape(bp * tp_, D_MODEL)
    tabs = _rope_tables(np.arange(tp_))
    zs, xbc, dtr, q, kt, vt, gs, ctail = _in_proj(xp2, nw, w_t, qnw, knw, tabs, tm, tp_)
    y_ssd, h_p = _ssd(xbc, dtr, zs, jnp.zeros((bp, SUBLANES, CONV_DIM), F32),
                      jnp.zeros((bp, SSD_INNER, SSD_STATE), F32),
                      cw, cb, dtb, alog, dexp, snw, bp, SSD_CHUNK, SSD_CHUNK)
    y_att = _attn_prompt(q, kt, vt, gs, bp, tp_)
    y_p = _out_proj(xp2, y_ssd, y_att, w_o, TM_OUT).reshape(bp, tp_, D_MODEL)
    keep = min(W_MAX, tp_)
    k_p = _to_cache(kt, bp, tp_)[:, :, tp_ - keep:]
    v_p = _to_cache(vt, bp, tp_)[:, :, tp_ - keep:]
    c_p = ctail[:, SUBLANES - (CONV_W - 1):][None]
    h_p = h_p.reshape(1, bp, SSD_HEADS, SSD_HEADDIM, SSD_STATE)

    return (y_p, y_s, k_p, v_p, c_p, h_p, k_s, v_s, c_s, h_s)
```

```python
import functools

import jax
import jax.numpy as jnp
import numpy as np
from jax import lax
from jax.experimental import pallas as pl
from jax.experimental.pallas import tpu as pltpu

F32 = jnp.float32
BF16 = jnp.bfloat16

D_MODEL = 2048
SSD_HEADS = 16
SSD_HEADDIM = 64
SSD_INNER = SSD_HEADS * SSD_HEADDIM
SSD_GROUPS = 2
SSD_STATE = 128
CONV_W = 4
CONV_DIM = SSD_INNER + 2 * SSD_GROUPS * SSD_STATE
SSD_CHUNK = 128
ATT_HEADS = 16
ATT_KV_HEADS = 4
HEAD_DIM = 64
ATT_GQ = ATT_HEADS // ATT_KV_HEADS
ATT_INNER = ATT_HEADS * HEAD_DIM
KV_DIM = ATT_KV_HEADS * HEAD_DIM
ROT_DIM = HEAD_DIM // 4
ROPE_THETA = 500000.0
DILATED_BRANCHES = ((128, 1), (512, 4), (2048, 16))
W_MAX = 2048
PAST_LEN = 16384
EPS = 1e-6

LANES = 128
SUBLANES = 8
Q_BLOCK = 256
K_SUPER = 256
Q_PER_SUPER = K_SUPER // Q_BLOCK
ONES_ROWS = 16
ACC_ROWS = HEAD_DIM + ONES_ROWS
CONV_COLS = 256
TM_IN = 512
TM_OUT = 1024
SSD_SEQS = 4
SAMPLE_PAD = 16
NEG = -1e30
VMEM_LIMIT = 58 * 1024 * 1024

Z0 = 0
X0 = Z0 + SSD_INNER
DT0 = X0 + CONV_DIM
Q0 = DT0 + SSD_HEADS
K0 = Q0 + ATT_INNER
V0 = K0 + KV_DIM
G0 = V0 + KV_DIM
W_ROWS = G0 + ATT_INNER
W_CHUNK = 1024
WO_CHUNK = 512
HALF_INNER = SSD_INNER // SSD_GROUPS
LOG2E = 1.4426950408889634
Q_SCALE = HEAD_DIM ** -0.5 * LOG2E


def _dot(a, b):
    return jnp.dot(a, b, preferred_element_type=F32)


def _dot_nt(a, b):
    return lax.dot_general(a, b, (((1,), (1,)), ((), ())), preferred_element_type=F32)


def _split3(x):
    hi = x.astype(BF16)
    r1 = x - hi.astype(F32)
    mid = r1.astype(BF16)
    lo = (r1 - mid.astype(F32)).astype(BF16)
    return hi, mid, lo


def _dot_exact_rhs(x, m):
    hi, mid, lo = _split3(x)
    return _dot(hi, m) + _dot(mid, m) + _dot(lo, m)


def _dot_wide_rhs(x, m):
    hi = x.astype(BF16)
    lo = (x - hi.astype(F32)).astype(BF16)
    return _dot(hi, m) + _dot(lo, m)


def _dot_exact_lhs(m, x):
    hi, mid, lo = _split3(x)
    return _dot(m, hi) + _dot(m, mid) + _dot(m, lo)


def _silu(x):
    hx = 0.5 * x
    return hx * jnp.tanh(hx) + hx


def _multiplicity(d):
    w = jnp.zeros(d.shape, F32)
    for window, dil in DILATED_BRANCHES:
        hit = (d >= 0) & (d <= window) & (lax.rem(d, dil) == 0)
        w = w + jnp.where(hit, 1.0, 0.0)
    return w


def _norm_rope(y, nw, cos, sa, sb, scale):
    lane = lax.broadcasted_iota(jnp.int32, (1, LANES), 1)
    first = lane < HEAD_DIM
    y2 = y * y
    s_lo = jnp.sum(jnp.where(first, y2, 0.0), axis=-1, keepdims=True)
    s_hi = jnp.sum(jnp.where(first, 0.0, y2), axis=-1, keepdims=True)
    ms = jnp.where(first, s_lo, s_hi) * (1.0 / HEAD_DIM)
    yn = y * lax.rsqrt(ms + EPS) * nw
    half = ROT_DIM // 2
    rot = yn * cos + pltpu.roll(yn, half, 1) * sa + pltpu.roll(yn, LANES - half, 1) * sb
    return rot * scale


def _causal_conv_silu(load, cw_ref, cb_ref, store):
    for c0 in range(0, CONV_DIM, CONV_COLS):
        cs = slice(c0, c0 + CONV_COLS)
        xe = load(cs)
        x2 = pltpu.roll(xe, 2, 0)
        even = cw_ref[3:4, cs] * xe + cw_ref[1:2, cs] * x2
        odd = cw_ref[2:3, cs] * xe + cw_ref[0:1, cs] * x2
        store(cs, _silu((even + pltpu.roll(odd, 1, 0))[SUBLANES:, :] + cb_ref[:, cs]))


def _pre_norm(x_ref, nw_ref):
    x = x_ref[...]
    ms = jnp.mean(x * x, axis=-1, keepdims=True)
    return (x * lax.rsqrt(ms + EPS) * nw_ref[...]).astype(BF16)


def _inproj_kernel(x_ref, nw_ref, w_ref, *refs):
    hn = _pre_norm(x_ref, nw_ref)

    def proj(a, b):
        return _dot_nt(hn, w_ref[a:b, :])

    _emit_sections(proj, x_ref.shape[0], *refs)


def _emit_sections(proj, tm, qnw_ref, knw_ref, cos_ref, sa_ref, sb_ref,
                   zs_ref, xbc_ref, dt_ref, q_ref, kt_ref, vt_ref, gs_ref, ctail_ref):
    cos, sa, sb = cos_ref[...], sa_ref[...], sb_ref[...]
    q = proj(Q0, K0)
    for c in range(ATT_INNER // LANES):
        sl = slice(c * LANES, (c + 1) * LANES)
        qn = _norm_rope(q[:, sl], qnw_ref[...], cos, sa, sb, Q_SCALE)
        if len(q_ref.shape) == 3:
            q_ref[0, sl, :] = qn.T.astype(BF16)
        else:
            q_ref[:, sl] = qn.astype(BF16)
    k = proj(K0, V0)
    kn = [_norm_rope(k[:, c * LANES:(c + 1) * LANES], knw_ref[...], cos, sa, sb, 1.0)
          for c in range(KV_DIM // LANES)]
    kt_ref[0] = jnp.concatenate(kn, axis=1).T

    xbc = proj(X0, DT0)
    xbc_ref[...] = xbc
    ctail_ref[0] = xbc[tm - SUBLANES:tm, :]
    vt_ref[0] = proj(V0, G0).T
    zs_ref[...] = _silu(proj(Z0, X0))
    gs_ref[...] = _silu(proj(G0, W_ROWS))
    dt_ref[...] = proj(DT0, DT0 + LANES)


def _in_proj(x2d, nw, w_t, qnw, knw, tables, tm, seq):
    rows = x2d.shape[0]
    const = lambda i: (0, 0)
    row = lambda i: (i, 0)
    sec_in, out_specs, out_shape = _section_specs(rows, tm, seq, tables[0].shape[0] // tm, True)
    return pl.pallas_call(
        _inproj_kernel,
        grid=(rows // tm,),
        in_specs=[
            pl.BlockSpec((tm, D_MODEL), row),
            pl.BlockSpec((1, D_MODEL), const),
            pl.BlockSpec((W_ROWS, D_MODEL), const, pipeline_mode=pl.Buffered(1)),
        ] + sec_in,
        out_specs=out_specs,
        out_shape=out_shape,
        compiler_params=pltpu.CompilerParams(
            dimension_semantics=("arbitrary",), vmem_limit_bytes=VMEM_LIMIT),
        name="in_proj",
    )(x2d, nw, w_t, qnw, knw, *tables)


def _section_specs(rows, tm, seq, period, q_transposed):
    per_seq = seq // tm
    const = lambda i: (0, 0)
    row = lambda i: (i, 0)
    tab = lambda i: (i % period, 0)
    tr = lambda i: (i // per_seq, 0, i % per_seq)
    slab = lambda i: (i // per_seq, 0, 0)
    in_specs = [pl.BlockSpec((1, LANES), const)] * 2 + [pl.BlockSpec((tm, LANES), tab)] * 3
    row_outs = {0: (SSD_INNER, F32), 1: (CONV_DIM, F32), 2: (LANES, F32), 3: (ATT_INNER, BF16),
                6: (ATT_INNER, F32)}
    out_specs, out_shape = [], []
    for idx in range(7):
        if idx == 3 and q_transposed:
            out_specs.append(pl.BlockSpec((1, ATT_INNER, tm), tr))
            out_shape.append(jax.ShapeDtypeStruct((rows // seq, ATT_INNER, seq), BF16))
        elif idx in row_outs:
            n, dt = row_outs[idx]
            out_specs.append(pl.BlockSpec((tm, n), row))
            out_shape.append(jax.ShapeDtypeStruct((rows, n), dt))
        else:
            out_specs.append(pl.BlockSpec((1, KV_DIM, tm), tr))
            out_shape.append(jax.ShapeDtypeStruct((rows // seq, KV_DIM, seq), F32))
    out_specs.append(pl.BlockSpec((1, SUBLANES, CONV_DIM), slab))
    out_shape.append(jax.ShapeDtypeStruct((rows // seq, SUBLANES, CONV_DIM), F32))
    return in_specs, out_specs, out_shape


def _wconv_in_kernel(x_ref, nw_ref, w_ref, wb_ref, p_ref, hn_scr):
    @pl.when(pl.program_id(0) == 0)
    def _():
        hn_scr[...] = _pre_norm(x_ref, nw_ref)

    row = pl.program_id(0) * W_CHUNK + lax.broadcasted_iota(jnp.int32, (W_CHUNK, 1), 0)
    wb = jnp.where(row < W_ROWS, w_ref[...], 0.0).astype(BF16)
    wb_ref[...] = wb
    p_ref[0] = _dot_nt(hn_scr[...], wb)


def _wconv_in(x2d, nw, w_f32):
    rows = x2d.shape[0]
    n_chunks = pl.cdiv(W_ROWS, W_CHUNK)
    const = lambda j: (0, 0)
    return pl.pallas_call(
        _wconv_in_kernel,
        grid=(n_chunks,),
        in_specs=[
            pl.BlockSpec((rows, D_MODEL), const),
            pl.BlockSpec((1, D_MODEL), const),
            pl.BlockSpec((W_CHUNK, D_MODEL), lambda j: (j, 0)),
        ],
        out_specs=[
            pl.BlockSpec((W_CHUNK, D_MODEL), lambda j: (j, 0)),
            pl.BlockSpec((1, rows, W_CHUNK), lambda j: (j, 0, 0)),
        ],
        out_shape=[
            jax.ShapeDtypeStruct((W_ROWS, D_MODEL), BF16),
            jax.ShapeDtypeStruct((n_chunks, rows, W_CHUNK), F32),
        ],
        scratch_shapes=[pltpu.VMEM((rows, D_MODEL), BF16)],
        compiler_params=pltpu.CompilerParams(
            dimension_semantics=("arbitrary",), vmem_limit_bytes=VMEM_LIMIT),
        name="wconv_in",
    )(x2d, nw, w_f32)


def _sections_kernel(p_ref, *refs):
    *refs, tail_scr = refs
    pfull = jnp.concatenate([p_ref[j] for j in range(p_ref.shape[0])], axis=1)
    tail_scr[...] = pfull[:, Q0:W_ROWS]

    def proj(a, b):
        if a >= Q0:
            return tail_scr[:, a - Q0:b - Q0]
        return pfull[:, a:b]

    _emit_sections(proj, p_ref.shape[1], *refs)


def _sections(p, qnw, knw, tables):
    rows = p.shape[1]
    sec_in, out_specs, out_shape = _section_specs(rows, rows, rows, 1, False)
    return pl.pallas_call(
        _sections_kernel,
        grid=(1,),
        in_specs=[pl.BlockSpec(p.shape, lambda i: (0, 0, 0))] + sec_in,
        out_specs=out_specs,
        out_shape=out_shape,
        scratch_shapes=[pltpu.VMEM((rows, W_ROWS - Q0), F32)],
        compiler_params=pltpu.CompilerParams(
            dimension_semantics=("arbitrary",), vmem_limit_bytes=VMEM_LIMIT),
        name="sections",
    )(p, qnw, knw, *tables)


def _ssd_chain(s, xbc_ref, dtr_ref, z_ref, cprev_ref, h0_ref, cw_ref, cb_ref, dtb_ref, alog_ref,
               dexp_ref, nw_ref, y_ref, hout_ref, xext, ht, xc_scr, lin, valid):
    L = SSD_CHUNK
    c = pl.program_id(1)
    last = pl.num_programs(1) - 1

    @pl.when(c == 0)
    def _():
        xext[s, 0:SUBLANES, :] = cprev_ref[s]
        for g in range(SSD_GROUPS):
            ht[s, g] = h0_ref[s, g * HALF_INNER:(g + 1) * HALF_INNER, :].T

    def pad_rows(v):
        if lin == L:
            return v
        return jnp.concatenate([v, jnp.zeros((L - lin, v.shape[1]), v.dtype)], axis=0)

    lane = lax.broadcasted_iota(jnp.int32, (1, LANES), 1)
    rowid = lax.broadcasted_iota(jnp.int32, (L, 1), 0)
    dt = jax.nn.softplus(pad_rows(dtr_ref[s]) + dtb_ref[...])
    dt = jnp.where((lane < SSD_HEADS) & (rowid < valid), dt, 0.0)
    dta = dt * (-jnp.exp(alog_ref[...]) * LOG2E)

    r2 = lax.broadcasted_iota(jnp.int32, (L, L), 0)
    c2 = lax.broadcasted_iota(jnp.int32, (L, L), 1)
    tri = r2 >= c2
    tri_b = jnp.where(tri, 1.0, 0.0).astype(BF16)
    cum = _dot_exact_lhs(tri_b, dta)
    cum_last = cum[L - 1:L, :]
    yield

    er = lax.broadcasted_iota(jnp.int32, (LANES, SSD_INNER), 0)
    ec = lax.broadcasted_iota(jnp.int32, (LANES, SSD_INNER), 1)
    expand = jnp.where(ec // SSD_HEADDIM == er, 1.0, 0.0).astype(BF16)
    ecum = _dot_wide_rhs(jnp.exp2(cum), expand)
    wexp = _dot_wide_rhs(jnp.exp2(cum_last - cum) * dt, expand)
    cd = _dot_exact_rhs(jnp.broadcast_to(jnp.exp2(cum_last), (SUBLANES, LANES)), expand)[0:1, :]
    cum_t = cum.T
    dt_t = dt.T
    yield

    if lin < L:
        xext[s, SUBLANES + lin:SUBLANES + L, :] = jnp.zeros((L - lin, CONV_DIM), F32)
    xext[s, SUBLANES:SUBLANES + lin, :] = xbc_ref[s]

    def load(cs):
        return xext[s, 0:SUBLANES + L, cs]

    def store(cs, v):
        xc_scr[s, :, cs] = v

    _causal_conv_silu(load, cw_ref, cb_ref, store)
    xext[s, 0:SUBLANES, :] = xext[s, L:L + SUBLANES, :]
    xc = xc_scr.at[s]
    xs = xc[:, :SSD_INNER]
    bm = xc[:, SSD_INNER:SSD_INNER + SSD_GROUPS * SSD_STATE]
    cm = xc[:, SSD_INNER + SSD_GROUPS * SSD_STATE:]
    bm_b = bm.astype(BF16)
    cm_b = cm.astype(BF16)
    xs_b = xs.astype(BF16)

    y_off = jnp.concatenate(
        [_dot(cm_b[:, g * SSD_STATE:(g + 1) * SSD_STATE], ht[s, g].astype(BF16))
         for g in range(SSD_GROUPS)], axis=1) * ecum
    cbs = [_dot_nt(cm_b[:, g * SSD_STATE:(g + 1) * SSD_STATE], bm_b[:, g * SSD_STATE:(g + 1) * SSD_STATE])
           for g in range(SSD_GROUPS)]
    yield

    lane_half = lane // HEAD_DIM
    heads_per_group = SSD_HEADS // SSD_GROUPS
    y_parts = []
    for g in range(SSD_GROUPS):
        for pr in range(heads_per_group // 2):
            col0 = g * HALF_INNER + pr * LANES
            xp = xs_b[:, col0:col0 + LANES]
            yp = jnp.zeros((L, LANES), F32)
            for e in range(2):
                hh = g * heads_per_group + pr * 2 + e
                decay = jnp.exp2(cum[:, hh:hh + 1] - cum_t[hh:hh + 1, :])
                sc = jnp.where(tri, cbs[g] * decay, 0.0) * dt_t[hh:hh + 1, :]
                xm = jnp.where(lane_half == e, xp, jnp.zeros_like(xp))
                yp = yp + _dot(sc.astype(BF16), xm)
            y_parts.append(yp)
            if pr % 2 == 1:
                yield
    y_diag = jnp.concatenate(y_parts, axis=1)

    xw = (xs * wexp).astype(BF16)
    for g in range(SSD_GROUPS):
        sl = slice(g * HALF_INNER, (g + 1) * HALF_INNER)
        bm_t = bm[:, g * SSD_STATE:(g + 1) * SSD_STATE].T.astype(BF16)
        ht[s, g] = ht[s, g] * cd[:, sl] + _dot(bm_t, xw[:, sl])
    yield

    y = (y_diag + y_off + dexp_ref[...] * xs) * pad_rows(z_ref[s])
    yn = []
    for g in range(SSD_GROUPS):
        yg = y[:, g * HALF_INNER:(g + 1) * HALF_INNER]
        ms = jnp.mean(yg * yg, axis=-1, keepdims=True)
        yn.append(yg * lax.rsqrt(ms + EPS))
    yn = jnp.concatenate(yn, axis=1) * nw_ref[...]
    y_ref[s] = yn[:lin].astype(BF16)

    @pl.when(c == last)
    def _():
        for g in range(SSD_GROUPS):
            hout_ref[s, g * HALF_INNER:(g + 1) * HALF_INNER, :] = ht[s, g].T


def _ssd_kernel(*refs, lin, valid):
    chains = [_ssd_chain(s, *refs, lin, valid) for s in range(SSD_SEQS)]
    while chains:
        alive = []
        for ch in chains:
            try:
                next(ch)
                alive.append(ch)
            except StopIteration:
                pass
        chains = alive


def _ssd(xbc, dtr, z, cprev, h0, cw, cb, dtb, alog, dexp, nw, batch, lin, valid):
    rows = xbc.shape[0]
    seq = rows // batch
    nc = seq // lin
    g = SSD_SEQS
    assert batch % g == 0
    blk = lambda b, c: (b, c, 0)
    per_b = lambda b, c: (b, 0, 0)
    const = lambda b, c: (0, 0)
    y, hout = pl.pallas_call(
        functools.partial(_ssd_kernel, lin=lin, valid=valid),
        grid=(batch // g, nc),
        in_specs=[
            pl.BlockSpec((g, lin, CONV_DIM), blk),
            pl.BlockSpec((g, lin, LANES), blk),
            pl.BlockSpec((g, lin, SSD_INNER), blk),
            pl.BlockSpec((g, SUBLANES, CONV_DIM), per_b),
            pl.BlockSpec((g, SSD_INNER, SSD_STATE), per_b),
            pl.BlockSpec((SUBLANES, CONV_DIM), const),
            pl.BlockSpec((1, CONV_DIM), const),
            pl.BlockSpec((1, LANES), const),
            pl.BlockSpec((1, LANES), const),
            pl.BlockSpec((1, SSD_INNER), const),
            pl.BlockSpec((1, SSD_INNER), const),
        ],
        out_specs=[
            pl.BlockSpec((g, lin, SSD_INNER), blk),
            pl.BlockSpec((g, SSD_INNER, SSD_STATE), per_b),
        ],
        out_shape=[
            jax.ShapeDtypeStruct((batch, seq, SSD_INNER), BF16),
            jax.ShapeDtypeStruct((batch, SSD_INNER, SSD_STATE), F32),
        ],
        scratch_shapes=[
            pltpu.VMEM((g, SUBLANES + SSD_CHUNK + SUBLANES, CONV_DIM), F32),
            pltpu.VMEM((g, SSD_GROUPS, SSD_STATE, HALF_INNER), F32),
            pltpu.VMEM((g, SSD_CHUNK, CONV_DIM), F32),
        ],
        compiler_params=pltpu.CompilerParams(
            dimension_semantics=("arbitrary", "arbitrary"), vmem_limit_bytes=VMEM_LIMIT),
        name="ssd",
    )(xbc.reshape(batch, seq, CONV_DIM), dtr.reshape(batch, seq, LANES),
      z.reshape(batch, seq, SSD_INNER), cprev, h0, cw, cb, dtb, alog, dexp, nw)
    return y.reshape(rows, SSD_INNER), hout


def _attn_prompt_kernel(q_ref, kt_ref, vt_ref, g_ref, o_ref,
                        bias_scr, kh_scr, va_scr, qt_scr, m_scr, acc_scr, ot_scr, s_scr, s2_scr, *, n_tab, n_sb_max):
    b = pl.program_id(0)
    i = pl.program_id(1)
    seq = kt_ref.shape[2]
    pair_w = 2 * Q_BLOCK
    pairs_per_kv = ATT_GQ // 2

    @pl.when((b == 0) & (i == 0))
    def _():
        r = lax.broadcasted_iota(jnp.int32, (K_SUPER, Q_BLOCK), 0)
        c = lax.broadcasted_iota(jnp.int32, (K_SUPER, Q_BLOCK), 1)
        for tb in range(n_tab):
            w = _multiplicity(tb * Q_BLOCK + c - r)
            bias_scr[tb] = jnp.where(w > 0.0, jnp.log2(jnp.maximum(w, 1.0)), NEG)
        va_scr[:, HEAD_DIM:, :] = jnp.ones((ATT_KV_HEADS, ONES_ROWS, seq), BF16)

    @pl.when(i == 0)
    def _():
        for kvh in range(ATT_KV_HEADS):
            hs = slice(kvh * HEAD_DIM, (kvh + 1) * HEAD_DIM)
            va_scr[kvh, 0:HEAD_DIM, :] = vt_ref[0, hs, :].astype(BF16)
        for cidx in range(seq // K_SUPER):
            rows = slice(cidx * K_SUPER, (cidx + 1) * K_SUPER)
            kc = kt_ref[0, :, rows].T
            for kvh in range(ATT_KV_HEADS):
                hs = slice(kvh * HEAD_DIM, (kvh + 1) * HEAD_DIM)
                kh_scr[kvh, rows, :] = kc[:, hs].astype(BF16)

    for kvh in range(ATT_KV_HEADS):
        qt_scr[kvh] = jnp.concatenate(
            [q_ref[0, (kvh * ATT_GQ + g) * HEAD_DIM:(kvh * ATT_GQ + g + 1) * HEAD_DIM, :]
             for g in range(ATT_GQ)], axis=1)
    m_scr[...] = jnp.full(m_scr.shape, NEG, F32)
    acc_scr[...] = jnp.zeros(acc_scr.shape, F32)
    par = i % Q_PER_SUPER
    j_last = i // Q_PER_SUPER
    n_sb = jnp.minimum(j_last + 1, n_sb_max)

    def key_start(dl):
        return pl.multiple_of((j_last - dl) * K_SUPER, K_SUPER)

    def scores(dl, dst):
        for kvh in range(ATT_KV_HEADS):
            dst[kvh] = _dot(kh_scr[kvh, pl.ds(key_start(dl), K_SUPER), :], qt_scr[kvh])

    def softmax_pv(dl, src):
        start = key_start(dl)
        bias = bias_scr[par + Q_PER_SUPER * dl]
        bias2 = jnp.concatenate([bias, bias], axis=1)
        for kvh in range(ATT_KV_HEADS):
            vtb = va_scr[kvh, :, pl.ds(start, K_SUPER)]
            for pr in range(pairs_per_kv):
                u = kvh * pairs_per_kv + pr
                s = src[kvh, :, pr * pair_w:(pr + 1) * pair_w] + bias2
                m_old = m_scr[u, 0:1, :]
                m_new = jnp.maximum(m_old, jnp.max(s, axis=0, keepdims=True))
                p = jnp.exp2(s - m_new).astype(BF16)
                acc_scr[u] = jnp.exp2(m_old - m_new) * acc_scr[u] + _dot(vtb, p)
                m_scr[u, 0:1, :] = m_new

    def body(t, carry):
        scores(2 * t, s_scr)
        scores(2 * t + 1, s2_scr)
        softmax_pv(2 * t, s_scr)
        softmax_pv(2 * t + 1, s2_scr)
        return carry

    lax.fori_loop(0, n_sb // 2, body, 0)

    @pl.when(n_sb % 2 == 1)
    def _():
        scores(n_sb - 1, s_scr)
        softmax_pv(n_sb - 1, s_scr)

    for u in range(ATT_HEADS // 2):
        acc = acc_scr[u]
        o = acc[0:HEAD_DIM] / acc[HEAD_DIM:HEAD_DIM + 1]
        for e in range(2):
            h = 2 * u + e
            ot_scr[h * HEAD_DIM:(h + 1) * HEAD_DIM, :] = o[:, e * Q_BLOCK:(e + 1) * Q_BLOCK]
    o_ref[...] = (ot_scr[...].T * g_ref[...]).astype(BF16)


def _attn_prompt(qt, kt, vt, gate, batch, seq):
    rows = batch * seq
    nqb = seq // Q_BLOCK
    assert seq % K_SUPER == 0
    n_sb_max = min(seq // K_SUPER, W_MAX // K_SUPER + 1)
    n_tab = Q_PER_SUPER * n_sb_max
    qblk = lambda b, i: (b * nqb + i, 0)
    per_b = lambda b, i: (b, 0, 0)
    return pl.pallas_call(
        functools.partial(_attn_prompt_kernel, n_tab=n_tab, n_sb_max=n_sb_max),
        grid=(batch, nqb),
        in_specs=[
            pl.BlockSpec((1, ATT_INNER, Q_BLOCK), lambda b, i: (b, 0, i)),
            pl.BlockSpec((1, KV_DIM, seq), per_b),
            pl.BlockSpec((1, KV_DIM, seq), per_b),
            pl.BlockSpec((Q_BLOCK, ATT_INNER), qblk),
        ],
        out_specs=pl.BlockSpec((Q_BLOCK, ATT_INNER), qblk),
        out_shape=jax.ShapeDtypeStruct((rows, ATT_INNER), BF16),
        scratch_shapes=[
            pltpu.VMEM((n_tab, K_SUPER, Q_BLOCK), F32),
            pltpu.VMEM((ATT_KV_HEADS, seq, HEAD_DIM), BF16),
            pltpu.VMEM((ATT_KV_HEADS, ACC_ROWS, seq), BF16),
            pltpu.VMEM((ATT_KV_HEADS, HEAD_DIM, ATT_GQ * Q_BLOCK), BF16),
            pltpu.VMEM((ATT_HEADS // 2, SUBLANES, 2 * Q_BLOCK), F32),
            pltpu.VMEM((ATT_HEADS // 2, ACC_ROWS, 2 * Q_BLOCK), F32),
            pltpu.VMEM((ATT_INNER, Q_BLOCK), F32),
            pltpu.VMEM((ATT_KV_HEADS, K_SUPER, ATT_GQ * Q_BLOCK), F32),
            pltpu.VMEM((ATT_KV_HEADS, K_SUPER, ATT_GQ * Q_BLOCK), F32),
        ],
        compiler_params=pltpu.CompilerParams(
            dimension_semantics=("arbitrary", "arbitrary"), vmem_limit_bytes=VMEM_LIMIT),
        name="attn_prompt",
    )(qt, kt, vt, gate)


def _attn_sample_kernel(q_ref, knt_ref, vnt_ref, g_ref, ckt_ref, cvt_ref, o_ref, kot_ref, vot_ref, *, ts):
    win = ckt_ref.shape[2]
    tp = SAMPLE_PAD
    rows = ATT_GQ * tp
    t_c = lax.broadcasted_iota(jnp.int32, (rows, win), 0) % tp
    j_c = lax.broadcasted_iota(jnp.int32, (rows, win), 1)
    w_c = _multiplicity(win + t_c - j_c)
    t_n = lax.broadcasted_iota(jnp.int32, (rows, tp), 0) % tp
    j_n = lax.broadcasted_iota(jnp.int32, (rows, tp), 1)
    w_n = jnp.where(j_n < ts, _multiplicity(t_n - j_n), 0.0)
    for kvh in range(ATT_KV_HEADS):
        c0 = kvh * ATT_GQ * HEAD_DIM
        qh = jnp.concatenate(
            [q_ref[:, c0 + g * HEAD_DIM:c0 + (g + 1) * HEAD_DIM] for g in range(ATT_GQ)], axis=0)
        hs = slice(kvh * HEAD_DIM, (kvh + 1) * HEAD_DIM)
        s_c = jnp.where(w_c > 0.0, _dot(qh, ckt_ref[0, hs, :].astype(BF16)), NEG)
        s_n = jnp.where(w_n > 0.0, _dot(qh, knt_ref[0, hs, :].astype(BF16)), NEG)
        m = jnp.maximum(jnp.max(s_c, axis=-1, keepdims=True), jnp.max(s_n, axis=-1, keepdims=True))
        p_c = jnp.exp2(s_c - m) * w_c
        p_n = jnp.exp2(s_n - m) * w_n
        den = jnp.sum(p_c, axis=-1, keepdims=True) + jnp.sum(p_n, axis=-1, keepdims=True)
        num = (_dot_nt(p_c.astype(BF16), cvt_ref[0, hs, :].astype(BF16))
               + _dot_nt(p_n.astype(BF16), vnt_ref[0, hs, :].astype(BF16)))
        o = num / den
        o = jnp.concatenate([o[g * tp:(g + 1) * tp] for g in range(ATT_GQ)], axis=1)
        csl = slice(c0, c0 + ATT_GQ * HEAD_DIM)
        o_ref[:, csl] = (o * g_ref[:, csl]).astype(BF16)
    kot_ref[0, :, 0:win - ts] = ckt_ref[0, :, ts:win]
    kot_ref[0, :, win - ts:win] = knt_ref[0, :, 0:ts]
    vot_ref[0, :, 0:win - ts] = cvt_ref[0, :, ts:win]
    vot_ref[0, :, win - ts:win] = vnt_ref[0, :, 0:ts]


def _attn_sample(q, knt, vnt, gate, ckt, cvt, ts):
    batch, _, win = ckt.shape
    tp = SAMPLE_PAD
    blk = lambda b: (b, 0)
    per_b = lambda b: (b, 0, 0)
    return pl.pallas_call(
        functools.partial(_attn_sample_kernel, ts=ts),
        grid=(batch,),
        in_specs=[
            pl.BlockSpec((tp, ATT_INNER), blk),
            pl.BlockSpec((1, KV_DIM, tp), per_b),
            pl.BlockSpec((1, KV_DIM, tp), per_b),
            pl.BlockSpec((tp, ATT_INNER), blk),
            pl.BlockSpec((1, KV_DIM, win), per_b),
            pl.BlockSpec((1, KV_DIM, win), per_b),
        ],
        out_specs=[
            pl.BlockSpec((tp, ATT_INNER), blk),
            pl.BlockSpec((1, KV_DIM, win), per_b),
            pl.BlockSpec((1, KV_DIM, win), per_b),
        ],
        out_shape=[
            jax.ShapeDtypeStruct((batch * tp, ATT_INNER), BF16),
            jax.ShapeDtypeStruct((batch, KV_DIM, win), F32),
            jax.ShapeDtypeStruct((batch, KV_DIM, win), F32),
        ],
        compiler_params=pltpu.CompilerParams(
            dimension_semantics=("arbitrary",), vmem_limit_bytes=VMEM_LIMIT),
        name="attn_sample",
    )(q, knt, vnt, gate, ckt, cvt)


def _outproj_kernel(x_ref, ys_ref, ya_ref, w_ref, o_ref):
    o_ref[...] = (x_ref[...] + _dot(ys_ref[...], w_ref[0:SSD_INNER, :])
                  + _dot(ya_ref[...], w_ref[SSD_INNER:, :]))


def _out_proj(x2d, y_ssd, y_att, w_o, tm):
    rows = x2d.shape[0]
    row = lambda i: (i, 0)
    const = lambda i: (0, 0)
    return pl.pallas_call(
        _outproj_kernel,
        grid=(rows // tm,),
        in_specs=[
            pl.BlockSpec((tm, D_MODEL), row),
            pl.BlockSpec((tm, SSD_INNER), row),
            pl.BlockSpec((tm, ATT_INNER), row),
            pl.BlockSpec((SSD_INNER + ATT_INNER, D_MODEL), const, pipeline_mode=pl.Buffered(1)),
        ],
        out_specs=pl.BlockSpec((tm, D_MODEL), row),
        out_shape=jax.ShapeDtypeStruct((rows, D_MODEL), F32),
        compiler_params=pltpu.CompilerParams(
            dimension_semantics=("arbitrary",), vmem_limit_bytes=VMEM_LIMIT),
        name="out_proj",
    )(x2d, y_ssd, y_att, w_o)


def _wconv_out_kernel(x_ref, ys_ref, ya_ref, w_ref, wb_ref, o_ref):
    j = pl.program_id(0)
    wb = w_ref[...].astype(BF16)
    wb_ref[...] = wb
    y = jnp.where(j < SSD_INNER // WO_CHUNK, ys_ref[...], ya_ref[...])
    part = _dot(y, wb)

    @pl.when(j == 0)
    def _():
        o_ref[...] = x_ref[...] + part

    @pl.when(j > 0)
    def _():
        o_ref[...] += part


def _wconv_out(x2d, y_ssd, y_att, w_f32):
    rows = x2d.shape[0]
    n_ssd = SSD_INNER // WO_CHUNK
    n_chunks = (SSD_INNER + ATT_INNER) // WO_CHUNK
    const = lambda j: (0, 0)
    return pl.pallas_call(
        _wconv_out_kernel,
        grid=(n_chunks,),
        in_specs=[
            pl.BlockSpec((rows, D_MODEL), const),
            pl.BlockSpec((rows, WO_CHUNK), lambda j: (0, jnp.minimum(j, n_ssd - 1))),
            pl.BlockSpec((rows, WO_CHUNK), lambda j: (0, jnp.maximum(j - n_ssd, 0))),
            pl.BlockSpec((WO_CHUNK, D_MODEL), lambda j: (j, 0)),
        ],
        out_specs=[
            pl.BlockSpec((WO_CHUNK, D_MODEL), lambda j: (j, 0)),
            pl.BlockSpec((rows, D_MODEL), const),
        ],
        out_shape=[
            jax.ShapeDtypeStruct((SSD_INNER + ATT_INNER, D_MODEL), BF16),
            jax.ShapeDtypeStruct((rows, D_MODEL), F32),
        ],
        compiler_params=pltpu.CompilerParams(
            dimension_semantics=("arbitrary",), vmem_limit_bytes=VMEM_LIMIT),
        name="wconv_out",
    )(x2d, y_ssd, y_att, w_f32)


def _rope_tables(pos):
    n = pos.shape[0]
    half = ROT_DIM // 2
    inv = ROPE_THETA ** (-np.arange(0, ROT_DIM, 2, dtype=np.float64) / ROT_DIM)
    ang = pos.astype(np.float64)[:, None] * inv[None, :]
    cos, sin = np.cos(ang), np.sin(ang)
    rest = HEAD_DIM - ROT_DIM
    zh = np.zeros((n, half))
    cos_h = np.concatenate([cos, cos, np.ones((n, rest))], axis=1)
    sa_h = np.concatenate([zh, sin, np.zeros((n, rest))], axis=1)
    sb_h = np.concatenate([-sin, zh, np.zeros((n, rest))], axis=1)
    rep = LANES // HEAD_DIM
    return tuple(jnp.asarray(np.tile(t, (1, rep)), dtype=F32) for t in (cos_h, sa_h, sb_h))


def _lane_pad(v, n=LANES):
    return jnp.pad(v, (0, n - v.shape[0])).reshape(1, n)


def _to_cache(xt, batch, seq):
    return xt.reshape(1, batch, ATT_KV_HEADS, HEAD_DIM, seq).transpose(0, 1, 4, 2, 3)


def kernel(x_prompt, x_sample, cache_k, cache_v, state_conv, state_ssm, norm_w, w_in, conv_w,
           conv_b, dt_bias, a_log, d_skip, ssd_norm_w, q_norm_w, k_norm_w, w_out):
    bp, tp_, _ = x_prompt.shape
    bs, ts, _ = x_sample.shape
    depth = w_in.shape[0]
    assert depth == 1 and tp_ % SSD_CHUNK == 0 and ts <= SAMPLE_PAD and ts >= CONV_W - 1
    l = 0
    win = cache_k.shape[2]

    nw = norm_w[l].reshape(1, D_MODEL)
    rep = LANES // HEAD_DIM
    qnw = jnp.tile(q_norm_w[l], rep).reshape(1, LANES)
    knw = jnp.tile(k_norm_w[l], rep).reshape(1, LANES)
    cw = jnp.pad(conv_w[l], ((0, SUBLANES - CONV_W), (0, 0)))
    cb = conv_b[l].reshape(1, CONV_DIM)
    dtb = _lane_pad(dt_bias[l])
    alog = _lane_pad(a_log[l])
    dexp = jnp.repeat(d_skip[l], SSD_HEADDIM).reshape(1, SSD_INNER)
    snw = ssd_norm_w[l].reshape(1, SSD_INNER)

    pad = SAMPLE_PAD
    xs2 = jnp.pad(x_sample, ((0, 0), (0, pad - ts), (0, 0))).reshape(bs * pad, D_MODEL)
    tabs_s = _rope_tables(PAST_LEN + np.arange(bs * pad) % pad)
    w_t, proj_s = _wconv_in(xs2, nw, jnp.swapaxes(w_in[l], 0, 1))
    zs, xbc, dtr, q, kt, vt, gs, _ = _sections(proj_s, qnw, knw, tabs_s)
    cprev = jnp.pad(state_conv[l], ((0, 0), (SUBLANES - (CONV_W - 1), 0), (0, 0)))
    y_ssd, h_s = _ssd(xbc, dtr, zs, cprev, state_ssm[l].reshape(bs, SSD_INNER, SSD_STATE),
                      cw, cb, dtb, alog, dexp, snw, bs, pad, ts)
    knt = kt.reshape(KV_DIM, bs, pad).transpose(1, 0, 2)
    vnt = vt.reshape(KV_DIM, bs, pad).transpose(1, 0, 2)
    ckt = cache_k[l].transpose(0, 2, 3, 1).reshape(bs, KV_DIM, win)
    cvt = cache_v[l].transpose(0, 2, 3, 1).reshape(bs, KV_DIM, win)
    y_att, kot, vot = _attn_sample(q, knt, vnt, gs, ckt, cvt, ts)
    w_o, y_s = _wconv_out(xs2, y_ssd, y_att, w_out[l])
    y_s = y_s.reshape(bs, pad, D_MODEL)[:, :ts]
    k_s = _to_cache(kot, bs, win)
    v_s = _to_cache(vot, bs, win)
    c_s = xbc.reshape(bs, pad, CONV_DIM)[:, ts - (CONV_W - 1):ts][None]
    h_s = h_s.reshape(1, bs, SSD_HEADS, SSD_HEADDIM, SSD_STATE)

    tm = TM_IN
    xp2 = x_prompt.reshape(bp * tp_, D_MODEL)
    tabs = _rope_tables(np.arange(tp_))
    zs, xbc, dtr, q, kt, vt, gs, ctail = _in_proj(xp2, nw, w_t, qnw, knw, tabs, tm, tp_)
    y_ssd, h_p = _ssd(xbc, dtr, zs, jnp.zeros((bp, SUBLANES, CONV_DIM), F32),
                      jnp.zeros((bp, SSD_INNER, SSD_STATE), F32),
                      cw, cb, dtb, alog, dexp, snw, bp, SSD_CHUNK, SSD_CHUNK)
    y_att = _attn_prompt(q, kt, vt, gs, bp, tp_)
    y_p = _out_proj(xp2, y_ssd, y_att, w_o, TM_OUT).reshape(bp, tp_, D_MODEL)
    keep = min(W_MAX, tp_)
    k_p = _to_cache(kt, bp, tp_)[:, :, tp_ - keep:]
    v_p = _to_cache(vt, bp, tp_)[:, :, tp_ - keep:]
    c_p = ctail[:, SUBLANES - (CONV_W - 1):][None]
    h_p = h_p.reshape(1, bp, SSD_HEADS, SSD_HEADDIM, SSD_STATE)

    return (y_p, y_s, k_p, v_p, c_p, h_p, k_s, v_s, c_s, h_s)
```

```python
import functools

import jax
import jax.numpy as jnp
import numpy as np
from jax import lax
from jax.experimental import pallas as pl
from jax.experimental.pallas import tpu as pltpu

F32 = jnp.float32
BF16 = jnp.bfloat16

D_MODEL = 2048
SSD_HEADS = 16
SSD_HEADDIM = 64
SSD_INNER = SSD_HEADS * SSD_HEADDIM
SSD_GROUPS = 2
SSD_STATE = 128
CONV_W = 4
CONV_DIM = SSD_INNER + 2 * SSD_GROUPS * SSD_STATE
SSD_CHUNK = 128
ATT_HEADS = 16
ATT_KV_HEADS = 4
HEAD_DIM = 64
ATT_GQ = ATT_HEADS // ATT_KV_HEADS
ATT_INNER = ATT_HEADS * HEAD_DIM
KV_DIM = ATT_KV_HEADS * HEAD_DIM
ROT_DIM = HEAD_DIM // 4
ROPE_THETA = 500000.0
DILATED_BRANCHES = ((128, 1), (512, 4), (2048, 16))
W_MAX = 2048
PAST_LEN = 16384
EPS = 1e-6

LANES = 128
SUBLANES = 8
Q_BLOCK = 256
K_SUPER = 256
Q_PER_SUPER = K_SUPER // Q_BLOCK
ONES_ROWS = 16
ACC_ROWS = HEAD_DIM + ONES_ROWS
CONV_COLS = 256
TM_IN = 512
TM_OUT = 512
SSD_SEQS = 4
SAMPLE_PAD = 16
NEG = -1e30
VMEM_LIMIT = 58 * 1024 * 1024

Z0 = 0
X0 = Z0 + SSD_INNER
DT0 = X0 + CONV_DIM
Q0 = DT0 + SSD_HEADS
K0 = Q0 + ATT_INNER
V0 = K0 + KV_DIM
G0 = V0 + KV_DIM
W_ROWS = G0 + ATT_INNER
W_CHUNK = 1024
WO_CHUNK = 512
HALF_INNER = SSD_INNER // SSD_GROUPS
LOG2E = 1.4426950408889634
Q_SCALE = HEAD_DIM ** -0.5 * LOG2E


def _dot(a, b):
    return jnp.dot(a, b, preferred_element_type=F32)


def _dot_nt(a, b):
    return lax.dot_general(a, b, (((1,), (1,)), ((), ())), preferred_element_type=F32)


def _split3(x):
    hi = x.astype(BF16)
    r1 = x - hi.astype(F32)
    mid = r1.astype(BF16)
    lo = (r1 - mid.astype(F32)).astype(BF16)
    return hi, mid, lo


def _dot_exact_rhs(x, m):
    hi, mid, lo = _split3(x)
    return _dot(hi, m) + _dot(mid, m) + _dot(lo, m)


def _dot_wide_rhs(x, m):
    hi = x.astype(BF16)
    lo = (x - hi.astype(F32)).astype(BF16)
    return _dot(hi, m) + _dot(lo, m)


def _dot_exact_lhs(m, x):
    hi, mid, lo = _split3(x)
    return _dot(m, hi) + _dot(m, mid) + _dot(m, lo)


def _silu(x):
    hx = 0.5 * x
    return hx * jnp.tanh(hx) + hx


def _multiplicity(d):
    w = jnp.zeros(d.shape, F32)
    for window, dil in DILATED_BRANCHES:
        hit = (d >= 0) & (d <= window) & (lax.rem(d, dil) == 0)
        w = w + jnp.where(hit, 1.0, 0.0)
    return w


def _norm_rope(y, nw, cos, sa, sb, scale):
    lane = lax.broadcasted_iota(jnp.int32, (1, LANES), 1)
    first = lane < HEAD_DIM
    y2 = y * y
    s_lo = jnp.sum(jnp.where(first, y2, 0.0), axis=-1, keepdims=True)
    s_hi = jnp.sum(jnp.where(first, 0.0, y2), axis=-1, keepdims=True)
    ms = jnp.where(first, s_lo, s_hi) * (1.0 / HEAD_DIM)
    yn = y * lax.rsqrt(ms + EPS) * nw
    half = ROT_DIM // 2
    rot = yn * cos + pltpu.roll(yn, half, 1) * sa + pltpu.roll(yn, LANES - half, 1) * sb
    return rot * scale


def _causal_conv_silu(load, cw_ref, cb_ref, store):
    for c0 in range(0, CONV_DIM, CONV_COLS):
        cs = slice(c0, c0 + CONV_COLS)
        xe = load(cs)
        x2 = pltpu.roll(xe, 2, 0)
        even = cw_ref[3:4, cs] * xe + cw_ref[1:2, cs] * x2
        odd = cw_ref[2:3, cs] * xe + cw_ref[0:1, cs] * x2
        store(cs, _silu((even + pltpu.roll(odd, 1, 0))[SUBLANES:, :] + cb_ref[:, cs]))


def _pre_norm(x_ref, nw_ref):
    x = x_ref[...]
    ms = jnp.mean(x * x, axis=-1, keepdims=True)
    return (x * lax.rsqrt(ms + EPS) * nw_ref[...]).astype(BF16)


def _inproj_kernel(x_ref, nw_ref, w_ref, *refs):
    hn = _pre_norm(x_ref, nw_ref)

    def proj(a, b):
        return _dot_nt(hn, w_ref[a:b, :])

    _emit_sections(proj, x_ref.shape[0], *refs)


def _emit_sections(proj, tm, qnw_ref, knw_ref, cos_ref, sa_ref, sb_ref,
                   zs_ref, xbc_ref, dt_ref, q_ref, kt_ref, vt_ref, gs_ref, ctail_ref):
    cos, sa, sb = cos_ref[...], sa_ref[...], sb_ref[...]
    q = proj(Q0, K0)
    for c in range(ATT_INNER // LANES):
        sl = slice(c * LANES, (c + 1) * LANES)
        qn = _norm_rope(q[:, sl], qnw_ref[...], cos, sa, sb, Q_SCALE)
        if len(q_ref.shape) == 3:
            q_ref[0, sl, :] = qn.T.astype(BF16)
        else:
            q_ref[:, sl] = qn.astype(BF16)
    k = proj(K0, V0)
    kn = [_norm_rope(k[:, c * LANES:(c + 1) * LANES], knw_ref[...], cos, sa, sb, 1.0)
          for c in range(KV_DIM // LANES)]
    kt_ref[0] = jnp.concatenate(kn, axis=1).T

    xbc = proj(X0, DT0)
    xbc_ref[...] = xbc
    ctail_ref[0] = xbc[tm - SUBLANES:tm, :]
    vt_ref[0] = proj(V0, G0).T
    zs_ref[...] = _silu(proj(Z0, X0))
    gs_ref[...] = _silu(proj(G0, W_ROWS))
    dt_ref[...] = proj(DT0, DT0 + LANES)


def _in_proj(x2d, nw, w_t, qnw, knw, tables, tm, seq):
    rows = x2d.shape[0]
    const = lambda i: (0, 0)
    row = lambda i: (i, 0)
    sec_in, out_specs, out_shape = _section_specs(rows, tm, seq, tables[0].shape[0] // tm, True)
    return pl.pallas_call(
        _inproj_kernel,
        grid=(rows // tm,),
        in_specs=[
            pl.BlockSpec((tm, D_MODEL), row),
            pl.BlockSpec((1, D_MODEL), const),
            pl.BlockSpec((W_ROWS, D_MODEL), const, pipeline_mode=pl.Buffered(1)),
        ] + sec_in,
        out_specs=out_specs,
        out_shape=out_shape,
        compiler_params=pltpu.CompilerParams(
            dimension_semantics=("arbitrary",), vmem_limit_bytes=VMEM_LIMIT),
        name="in_proj",
    )(x2d, nw, w_t, qnw, knw, *tables)


def _section_specs(rows, tm, seq, period, q_transposed):
    per_seq = seq // tm
    const = lambda i: (0, 0)
    row = lambda i: (i, 0)
    tab = lambda i: (i % period, 0)
    tr = lambda i: (i // per_seq, 0, i % per_seq)
    slab = lambda i: (i // per_seq, 0, 0)
    in_specs = [pl.BlockSpec((1, LANES), const)] * 2 + [pl.BlockSpec((tm, LANES), tab)] * 3
    row_outs = {0: (SSD_INNER, F32), 1: (CONV_DIM, F32), 2: (LANES, F32), 3: (ATT_INNER, BF16),
                6: (ATT_INNER, F32)}
    out_specs, out_shape = [], []
    for idx in range(7):
        if idx == 3 and q_transposed:
            out_specs.append(pl.BlockSpec((1, ATT_INNER, tm), tr))
            out_shape.append(jax.ShapeDtypeStruct((rows // seq, ATT_INNER, seq), BF16))
        elif idx in row_outs:
            n, dt = row_outs[idx]
            out_specs.append(pl.BlockSpec((tm, n), row))
            out_shape.append(jax.ShapeDtypeStruct((rows, n), dt))
        else:
            out_specs.append(pl.BlockSpec((1, KV_DIM, tm), tr))
            out_shape.append(jax.ShapeDtypeStruct((rows // seq, KV_DIM, seq), F32))
    out_specs.append(pl.BlockSpec((1, SUBLANES, CONV_DIM), slab))
    out_shape.append(jax.ShapeDtypeStruct((rows // seq, SUBLANES, CONV_DIM), F32))
    return in_specs, out_specs, out_shape


def _wconv_in_kernel(x_ref, nw_ref, w_ref, wb_ref, p_ref, hn_scr):
    @pl.when(pl.program_id(0) == 0)
    def _():
        hn_scr[...] = _pre_norm(x_ref, nw_ref)

    row = pl.program_id(0) * W_CHUNK + lax.broadcasted_iota(jnp.int32, (W_CHUNK, 1), 0)
    wb = jnp.where(row < W_ROWS, w_ref[...], 0.0).astype(BF16)
    wb_ref[...] = wb
    p_ref[0] = _dot_nt(hn_scr[...], wb)


def _wconv_in(x2d, nw, w_f32):
    rows = x2d.shape[0]
    n_chunks = pl.cdiv(W_ROWS, W_CHUNK)
    const = lambda j: (0, 0)
    return pl.pallas_call(
        _wconv_in_kernel,
        grid=(n_chunks,),
        in_specs=[
            pl.BlockSpec((rows, D_MODEL), const),
            pl.BlockSpec((1, D_MODEL), const),
            pl.BlockSpec((W_CHUNK, D_MODEL), lambda j: (j, 0)),
        ],
        out_specs=[
            pl.BlockSpec((W_CHUNK, D_MODEL), lambda j: (j, 0)),
            pl.BlockSpec((1, rows, W_CHUNK), lambda j: (j, 0, 0)),
        ],
        out_shape=[
            jax.ShapeDtypeStruct((W_ROWS, D_MODEL), BF16),
            jax.ShapeDtypeStruct((n_chunks, rows, W_CHUNK), F32),
        ],
        scratch_shapes=[pltpu.VMEM((rows, D_MODEL), BF16)],
        compiler_params=pltpu.CompilerParams(
            dimension_semantics=("arbitrary",), vmem_limit_bytes=VMEM_LIMIT),
        name="wconv_in",
    )(x2d, nw, w_f32)


def _sections_kernel(p_ref, *refs):
    *refs, tail_scr = refs
    pfull = jnp.concatenate([p_ref[j] for j in range(p_ref.shape[0])], axis=1)
    tail_scr[...] = pfull[:, Q0:W_ROWS]

    def proj(a, b):
        if a >= Q0:
            return tail_scr[:, a - Q0:b - Q0]
        return pfull[:, a:b]

    _emit_sections(proj, p_ref.shape[1], *refs)


def _sections(p, qnw, knw, tables):
    rows = p.shape[1]
    sec_in, out_specs, out_shape = _section_specs(rows, rows, rows, 1, False)
    return pl.pallas_call(
        _sections_kernel,
        grid=(1,),
        in_specs=[pl.BlockSpec(p.shape, lambda i: (0, 0, 0))] + sec_in,
        out_specs=out_specs,
        out_shape=out_shape,
        scratch_shapes=[pltpu.VMEM((rows, W_ROWS - Q0), F32)],
        compiler_params=pltpu.CompilerParams(
            dimension_semantics=("arbitrary",), vmem_limit_bytes=VMEM_LIMIT),
        name="sections",
    )(p, qnw, knw, *tables)


def _ssd_chain(s, xbc_ref, dtr_ref, z_ref, cprev_ref, h0_ref, cw_ref, cb_ref, dtb_ref, alog_ref,
               dexp_ref, nw_ref, y_ref, hout_ref, xext, ht, xc_scr, lin, valid):
    L = SSD_CHUNK
    c = pl.program_id(1)
    last = pl.num_programs(1) - 1

    @pl.when(c == 0)
    def _():
        xext[s, 0:SUBLANES, :] = cprev_ref[s]
        for g in range(SSD_GROUPS):
            ht[s, g] = h0_ref[s, g * HALF_INNER:(g + 1) * HALF_INNER, :].T

    def pad_rows(v):
        if lin == L:
            return v
        return jnp.concatenate([v, jnp.zeros((L - lin, v.shape[1]), v.dtype)], axis=0)

    lane = lax.broadcasted_iota(jnp.int32, (1, LANES), 1)
    rowid = lax.broadcasted_iota(jnp.int32, (L, 1), 0)
    dt = jax.nn.softplus(pad_rows(dtr_ref[s]) + dtb_ref[...])
    dt = jnp.where((lane < SSD_HEADS) & (rowid < valid), dt, 0.0)
    dta = dt * (-jnp.exp(alog_ref[...]) * LOG2E)

    r2 = lax.broadcasted_iota(jnp.int32, (L, L), 0)
    c2 = lax.broadcasted_iota(jnp.int32, (L, L), 1)
    tri = r2 >= c2
    tri_b = jnp.where(tri, 1.0, 0.0).astype(BF16)
    cum = _dot_exact_lhs(tri_b, dta)
    cum_last = cum[L - 1:L, :]
    yield

    er = lax.broadcasted_iota(jnp.int32, (LANES, SSD_INNER), 0)
    ec = lax.broadcasted_iota(jnp.int32, (LANES, SSD_INNER), 1)
    expand = jnp.where(ec // SSD_HEADDIM == er, 1.0, 0.0).astype(BF16)
    ecum = _dot_wide_rhs(jnp.exp2(cum), expand)
    wexp = _dot_wide_rhs(jnp.exp2(cum_last - cum) * dt, expand)
    cd = _dot_exact_rhs(jnp.broadcast_to(jnp.exp2(cum_last), (SUBLANES, LANES)), expand)[0:1, :]
    cum_t = cum.T
    dt_t = dt.T
    yield

    if lin < L:
        xext[s, SUBLANES + lin:SUBLANES + L, :] = jnp.zeros((L - lin, CONV_DIM), F32)
    xext[s, SUBLANES:SUBLANES + lin, :] = xbc_ref[s]

    def load(cs):
        return xext[s, 0:SUBLANES + L, cs]

    def store(cs, v):
        xc_scr[s, :, cs] = v

    _causal_conv_silu(load, cw_ref, cb_ref, store)
    xext[s, 0:SUBLANES, :] = xext[s, L:L + SUBLANES, :]
    xc = xc_scr.at[s]
    xs = xc[:, :SSD_INNER]
    bm = xc[:, SSD_INNER:SSD_INNER + SSD_GROUPS * SSD_STATE]
    cm = xc[:, SSD_INNER + SSD_GROUPS * SSD_STATE:]
    bm_b = bm.astype(BF16)
    cm_b = cm.astype(BF16)
    xs_b = xs.astype(BF16)

    y_off = jnp.concatenate(
        [_dot(cm_b[:, g * SSD_STATE:(g + 1) * SSD_STATE], ht[s, g].astype(BF16))
         for g in range(SSD_GROUPS)], axis=1) * ecum
    cbs = [_dot_nt(cm_b[:, g * SSD_STATE:(g + 1) * SSD_STATE], bm_b[:, g * SSD_STATE:(g + 1) * SSD_STATE])
           for g in range(SSD_GROUPS)]
    yield

    lane_half = lane // HEAD_DIM
    heads_per_group = SSD_HEADS // SSD_GROUPS
    y_parts = []
    for g in range(SSD_GROUPS):
        for pr in range(heads_per_group // 2):
            col0 = g * HALF_INNER + pr * LANES
            xp = xs_b[:, col0:col0 + LANES]
            yp = jnp.zeros((L, LANES), F32)
            for e in range(2):
                hh = g * heads_per_group + pr * 2 + e
                decay = jnp.exp2(cum[:, hh:hh + 1] - cum_t[hh:hh + 1, :])
                sc = jnp.where(tri, cbs[g] * decay, 0.0) * dt_t[hh:hh + 1, :]
                xm = jnp.where(lane_half == e, xp, jnp.zeros_like(xp))
                yp = yp + _dot(sc.astype(BF16), xm)
            y_parts.append(yp)
            if pr % 2 == 1:
                yield
    y_diag = jnp.concatenate(y_parts, axis=1)

    xw = (xs * wexp).astype(BF16)
    for g in range(SSD_GROUPS):
        sl = slice(g * HALF_INNER, (g + 1) * HALF_INNER)
        bm_t = bm[:, g * SSD_STATE:(g + 1) * SSD_STATE].T.astype(BF16)
        ht[s, g] = ht[s, g] * cd[:, sl] + _dot(bm_t, xw[:, sl])
    yield

    y = (y_diag + y_off + dexp_ref[...] * xs) * pad_rows(z_ref[s])
    yn = []
    for g in range(SSD_GROUPS):
        yg = y[:, g * HALF_INNER:(g + 1) * HALF_INNER]
        ms = jnp.mean(yg * yg, axis=-1, keepdims=True)
        yn.append(yg * lax.rsqrt(ms + EPS))
    yn = jnp.concatenate(yn, axis=1) * nw_ref[...]
    y_ref[s] = yn[:lin].astype(BF16)

    @pl.when(c == last)
    def _():
        for g in range(SSD_GROUPS):
            hout_ref[s, g * HALF_INNER:(g + 1) * HALF_INNER, :] = ht[s, g].T


def _ssd_kernel(*refs, lin, valid):
    chains = [_ssd_chain(s, *refs, lin, valid) for s in range(SSD_SEQS)]
    while chains:
        alive = []
        for ch in chains:
            try:
                next(ch)
                alive.append(ch)
            except StopIteration:
                pass
        chains = alive


def _ssd(xbc, dtr, z, cprev, h0, cw, cb, dtb, alog, dexp, nw, batch, lin, valid):
    rows = xbc.shape[0]
    seq = rows // batch
    nc = seq // lin
    g = SSD_SEQS
    assert batch % g == 0
    blk = lambda b, c: (b, c, 0)
    per_b = lambda b, c: (b, 0, 0)
    const = lambda b, c: (0, 0)
    y, hout = pl.pallas_call(
        functools.partial(_ssd_kernel, lin=lin, valid=valid),
        grid=(batch // g, nc),
        in_specs=[
            pl.BlockSpec((g, lin, CONV_DIM), blk),
            pl.BlockSpec((g, lin, LANES), blk),
            pl.BlockSpec((g, lin, SSD_INNER), blk),
            pl.BlockSpec((g, SUBLANES, CONV_DIM), per_b),
            pl.BlockSpec((g, SSD_INNER, SSD_STATE), per_b),
            pl.BlockSpec((SUBLANES, CONV_DIM), const),
            pl.BlockSpec((1, CONV_DIM), const),
            pl.BlockSpec((1, LANES), const),
            pl.BlockSpec((1, LANES), const),
            pl.BlockSpec((1, SSD_INNER), const),
            pl.BlockSpec((1, SSD_INNER), const),
        ],
        out_specs=[
            pl.BlockSpec((g, lin, SSD_INNER), blk),
            pl.BlockSpec((g, SSD_INNER, SSD_STATE), per_b),
        ],
        out_shape=[
            jax.ShapeDtypeStruct((batch, seq, SSD_INNER), BF16),
            jax.ShapeDtypeStruct((batch, SSD_INNER, SSD_STATE), F32),
        ],
        scratch_shapes=[
            pltpu.VMEM((g, SUBLANES + SSD_CHUNK + SUBLANES, CONV_DIM), F32),
            pltpu.VMEM((g, SSD_GROUPS, SSD_STATE, HALF_INNER), F32),
            pltpu.VMEM((g, SSD_CHUNK, CONV_DIM), F32),
        ],
        compiler_params=pltpu.CompilerParams(
            dimension_semantics=("arbitrary", "arbitrary"), vmem_limit_bytes=VMEM_LIMIT),
        name="ssd",
    )(xbc.reshape(batch, seq, CONV_DIM), dtr.reshape(batch, seq, LANES),
      z.reshape(batch, seq, SSD_INNER), cprev, h0, cw, cb, dtb, alog, dexp, nw)
    return y.reshape(rows, SSD_INNER), hout


def _attn_prompt_kernel(q_ref, kt_ref, vt_ref, g_ref, o_ref,
                        bias_scr, kh_scr, va_scr, qt_scr, m_scr, acc_scr, ot_scr, s_scr, s2_scr, *, n_tab, n_sb_max):
    b = pl.program_id(0)
    i = pl.program_id(1)
    seq = kt_ref.shape[2]
    pair_w = 2 * Q_BLOCK
    pairs_per_kv = ATT_GQ // 2

    @pl.when((b == 0) & (i == 0))
    def _():
        r = lax.broadcasted_iota(jnp.int32, (K_SUPER, Q_BLOCK), 0)
        c = lax.broadcasted_iota(jnp.int32, (K_SUPER, Q_BLOCK), 1)
        for tb in range(n_tab):
            w = _multiplicity(tb * Q_BLOCK + c - r)
            bias_scr[tb] = jnp.where(w > 0.0, jnp.log2(jnp.maximum(w, 1.0)), NEG)
        va_scr[:, HEAD_DIM:, :] = jnp.ones((ATT_KV_HEADS, ONES_ROWS, seq), BF16)

    @pl.when(i == 0)
    def _():
        for kvh in range(ATT_KV_HEADS):
            hs = slice(kvh * HEAD_DIM, (kvh + 1) * HEAD_DIM)
            va_scr[kvh, 0:HEAD_DIM, :] = vt_ref[0, hs, :].astype(BF16)
        for cidx in range(seq // K_SUPER):
            rows = slice(cidx * K_SUPER, (cidx + 1) * K_SUPER)
            kc = kt_ref[0, :, rows].T
            for kvh in range(ATT_KV_HEADS):
                hs = slice(kvh * HEAD_DIM, (kvh + 1) * HEAD_DIM)
                kh_scr[kvh, rows, :] = kc[:, hs].astype(BF16)

    for kvh in range(ATT_KV_HEADS):
        qt_scr[kvh] = jnp.concatenate(
            [q_ref[0, (kvh * ATT_GQ + g) * HEAD_DIM:(kvh * ATT_GQ + g + 1) * HEAD_DIM, :]
             for g in range(ATT_GQ)], axis=1)
    m_scr[...] = jnp.full(m_scr.shape, NEG, F32)
    acc_scr[...] = jnp.zeros(acc_scr.shape, F32)
    par = i % Q_PER_SUPER
    j_last = i // Q_PER_SUPER
    n_sb = jnp.minimum(j_last + 1, n_sb_max)

    def key_start(dl):
        return pl.multiple_of((j_last - dl) * K_SUPER, K_SUPER)

    def scores(dl, dst):
        for kvh in range(ATT_KV_HEADS):
            dst[kvh] = _dot(kh_scr[kvh, pl.ds(key_start(dl), K_SUPER), :], qt_scr[kvh])

    def softmax_pv(dl, src):
        start = key_start(dl)
        bias = bias_scr[par + Q_PER_SUPER * dl]
        bias2 = jnp.concatenate([bias, bias], axis=1)
        for kvh in range(ATT_KV_HEADS):
            vtb = va_scr[kvh, :, pl.ds(start, K_SUPER)]
            for pr in range(pairs_per_kv):
                u = kvh * pairs_per_kv + pr
                s = src[kvh, :, pr * pair_w:(pr + 1) * pair_w] + bias2
                m_old = m_scr[u, 0:1, :]
                m_new = jnp.maximum(m_old, jnp.max(s, axis=0, keepdims=True))
                p = jnp.exp2(s - m_new).astype(BF16)
                acc_scr[u] = jnp.exp2(m_old - m_new) * acc_scr[u] + _dot(vtb, p)
                m_scr[u, 0:1, :] = m_new

    def body(t, carry):
        scores(2 * t, s_scr)
        scores(2 * t + 1, s2_scr)
        softmax_pv(2 * t, s_scr)
        softmax_pv(2 * t + 1, s2_scr)
        return carry

    lax.fori_loop(0, n_sb // 2, body, 0)

    @pl.when(n_sb % 2 == 1)
    def _():
        scores(n_sb - 1, s_scr)
        softmax_pv(n_sb - 1, s_scr)

    for u in range(ATT_HEADS // 2):
        acc = acc_scr[u]
        o = acc[0:HEAD_DIM] / acc[HEAD_DIM:HEAD_DIM + 1]
        for e in range(2):
            h = 2 * u + e
            ot_scr[h * HEAD_DIM:(h + 1) * HEAD_DIM, :] = o[:, e * Q_BLOCK:(e + 1) * Q_BLOCK]
    o_ref[...] = (ot_scr[...].T * g_ref[...]).astype(BF16)


def _attn_prompt(qt, kt, vt, gate, batch, seq):
    rows = batch * seq
    nqb = seq // Q_BLOCK
    assert seq % K_SUPER == 0
    n_sb_max = min(seq // K_SUPER, W_MAX // K_SUPER + 1)
    n_tab = Q_PER_SUPER * n_sb_max
    qblk = lambda b, i: (b * nqb + i, 0)
    per_b = lambda b, i: (b, 0, 0)
    return pl.pallas_call(
        functools.partial(_attn_prompt_kernel, n_tab=n_tab, n_sb_max=n_sb_max),
        grid=(batch, nqb),
        in_specs=[
            pl.BlockSpec((1, ATT_INNER, Q_BLOCK), lambda b, i: (b, 0, i)),
            pl.BlockSpec((1, KV_DIM, seq), per_b),
            pl.BlockSpec((1, KV_DIM, seq), per_b),
            pl.BlockSpec((Q_BLOCK, ATT_INNER), qblk),
        ],
        out_specs=pl.BlockSpec((Q_BLOCK, ATT_INNER), qblk),
        out_shape=jax.ShapeDtypeStruct((rows, ATT_INNER), BF16),
        scratch_shapes=[
            pltpu.VMEM((n_tab, K_SUPER, Q_BLOCK), F32),
            pltpu.VMEM((ATT_KV_HEADS, seq, HEAD_DIM), BF16),
            pltpu.VMEM((ATT_KV_HEADS, ACC_ROWS, seq), BF16),
            pltpu.VMEM((ATT_KV_HEADS, HEAD_DIM, ATT_GQ * Q_BLOCK), BF16),
            pltpu.VMEM((ATT_HEADS // 2, SUBLANES, 2 * Q_BLOCK), F32),
            pltpu.VMEM((ATT_HEADS // 2, ACC_ROWS, 2 * Q_BLOCK), F32),
            pltpu.VMEM((ATT_INNER, Q_BLOCK), F32),
            pltpu.VMEM((ATT_KV_HEADS, K_SUPER, ATT_GQ * Q_BLOCK), F32),
            pltpu.VMEM((ATT_KV_HEADS, K_SUPER, ATT_GQ * Q_BLOCK), F32),
        ],
        compiler_params=pltpu.CompilerParams(
            dimension_semantics=("arbitrary", "arbitrary"), vmem_limit_bytes=VMEM_LIMIT),
        name="attn_prompt",
    )(qt, kt, vt, gate)


def _attn_sample_kernel(q_ref, knt_ref, vnt_ref, g_ref, ckt_ref, cvt_ref, o_ref, kot_ref, vot_ref, *, ts):
    win = ckt_ref.shape[2]
    tp = SAMPLE_PAD
    rows = ATT_GQ * tp
    t_c = lax.broadcasted_iota(jnp.int32, (rows, win), 0) % tp
    j_c = lax.broadcasted_iota(jnp.int32, (rows, win), 1)
    w_c = _multiplicity(win + t_c - j_c)
    t_n = lax.broadcasted_iota(jnp.int32, (rows, tp), 0) % tp
    j_n = lax.broadcasted_iota(jnp.int32, (rows, tp), 1)
    w_n = jnp.where(j_n < ts, _multiplicity(t_n - j_n), 0.0)
    for kvh in range(ATT_KV_HEADS):
        c0 = kvh * ATT_GQ * HEAD_DIM
        qh = jnp.concatenate(
            [q_ref[:, c0 + g * HEAD_DIM:c0 + (g + 1) * HEAD_DIM] for g in range(ATT_GQ)], axis=0)
        hs = slice(kvh * HEAD_DIM, (kvh + 1) * HEAD_DIM)
        s_c = jnp.where(w_c > 0.0, _dot(qh, ckt_ref[0, hs, :].astype(BF16)), NEG)
        s_n = jnp.where(w_n > 0.0, _dot(qh, knt_ref[0, hs, :].astype(BF16)), NEG)
        m = jnp.maximum(jnp.max(s_c, axis=-1, keepdims=True), jnp.max(s_n, axis=-1, keepdims=True))
        p_c = jnp.exp2(s_c - m) * w_c
        p_n = jnp.exp2(s_n - m) * w_n
        den = jnp.sum(p_c, axis=-1, keepdims=True) + jnp.sum(p_n, axis=-1, keepdims=True)
        num = (_dot_nt(p_c.astype(BF16), cvt_ref[0, hs, :].astype(BF16))
               + _dot_nt(p_n.astype(BF16), vnt_ref[0, hs, :].astype(BF16)))
        o = num / den
        o = jnp.concatenate([o[g * tp:(g + 1) * tp] for g in range(ATT_GQ)], axis=1)
        csl = slice(c0, c0 + ATT_GQ * HEAD_DIM)
        o_ref[:, csl] = (o * g_ref[:, csl]).astype(BF16)
    kot_ref[0, :, 0:win - ts] = ckt_ref[0, :, ts:win]
    kot_ref[0, :, win - ts:win] = knt_ref[0, :, 0:ts]
    vot_ref[0, :, 0:win - ts] = cvt_ref[0, :, ts:win]
    vot_ref[0, :, win - ts:win] = vnt_ref[0, :, 0:ts]


def _attn_sample(q, knt, vnt, gate, ckt, cvt, ts):
    batch, _, win = ckt.shape
    tp = SAMPLE_PAD
    blk = lambda b: (b, 0)
    per_b = lambda b: (b, 0, 0)
    return pl.pallas_call(
        functools.partial(_attn_sample_kernel, ts=ts),
        grid=(batch,),
        in_specs=[
            pl.BlockSpec((tp, ATT_INNER), blk),
            pl.BlockSpec((1, KV_DIM, tp), per_b),
            pl.BlockSpec((1, KV_DIM, tp), per_b),
            pl.BlockSpec((tp, ATT_INNER), blk),
            pl.BlockSpec((1, KV_DIM, win), per_b),
            pl.BlockSpec((1, KV_DIM, win), per_b),
        ],
        out_specs=[
            pl.BlockSpec((tp, ATT_INNER), blk),
            pl.BlockSpec((1, KV_DIM, win), per_b),
            pl.BlockSpec((1, KV_DIM, win), per_b),
        ],
        out_shape=[
            jax.ShapeDtypeStruct((batch * tp, ATT_INNER), BF16),
            jax.ShapeDtypeStruct((batch, KV_DIM, win), F32),
            jax.ShapeDtypeStruct((batch, KV_DIM, win), F32),
        ],
        compiler_params=pltpu.CompilerParams(
            dimension_semantics=("arbitrary",), vmem_limit_bytes=VMEM_LIMIT),
        name="attn_sample",
    )(q, knt, vnt, gate, ckt, cvt)


def _outproj_kernel(x_ref, ys_ref, ya_ref, w_ref, o_ref):
    o_ref[...] = (x_ref[...] + _dot(ys_ref[...], w_ref[0:SSD_INNER, :])
                  + _dot(ya_ref[...], w_ref[SSD_INNER:, :]))


def _out_proj(x2d, y_ssd, y_att, w_o, tm):
    rows = x2d.shape[0]
    row = lambda i: (i, 0)
    const = lambda i: (0, 0)
    return pl.pallas_call(
        _outproj_kernel,
        grid=(rows // tm,),
        in_specs=[
            pl.BlockSpec((tm, D_MODEL), row),
            pl.BlockSpec((tm, SSD_INNER), row),
            pl.BlockSpec((tm, ATT_INNER), row),
            pl.BlockSpec((SSD_INNER + ATT_INNER, D_MODEL), const, pipeline_mode=pl.Buffered(1)),
        ],
        out_specs=pl.BlockSpec((tm, D_MODEL), row),
        out_shape=jax.ShapeDtypeStruct((rows, D_MODEL), F32),
        compiler_params=pltpu.CompilerParams(
            dimension_semantics=("arbitrary",), vmem_limit_bytes=VMEM_LIMIT),
        name="out_proj",
    )(x2d, y_ssd, y_att, w_o)


def _wconv_out_kernel(x_ref, ys_ref, ya_ref, w_ref, wb_ref, o_ref):
    j = pl.program_id(0)
    wb = w_ref[...].astype(BF16)
    wb_ref[...] = wb
    y = jnp.where(j < SSD_INNER // WO_CHUNK, ys_ref[...], ya_ref[...])
    part = _dot(y, wb)

    @pl.when(j == 0)
    def _():
        o_ref[...] = x_ref[...] + part

    @pl.when(j > 0)
    def _():
        o_ref[...] += part


def _wconv_out(x2d, y_ssd, y_att, w_f32):
    rows = x2d.shape[0]
    n_ssd = SSD_INNER // WO_CHUNK
    n_chunks = (SSD_INNER + ATT_INNER) // WO_CHUNK
    const = lambda j: (0, 0)
    return pl.pallas_call(
        _wconv_out_kernel,
        grid=(n_chunks,),
        in_specs=[
            pl.BlockSpec((rows, D_MODEL), const),
            pl.BlockSpec((rows, WO_CHUNK), lambda j: (0, jnp.minimum(j, n_ssd - 1))),
            pl.BlockSpec((rows, WO_CHUNK), lambda j: (0, jnp.maximum(j - n_ssd, 0))),
            pl.BlockSpec((WO_CHUNK, D_MODEL), lambda j: (j, 0)),
        ],
        out_specs=[
            pl.BlockSpec((WO_CHUNK, D_MODEL), lambda j: (j, 0)),
            pl.BlockSpec((rows, D_MODEL), const),
        ],
        out_shape=[
            jax.ShapeDtypeStruct((SSD_INNER + ATT_INNER, D_MODEL), BF16),
            jax.ShapeDtypeStruct((rows, D_MODEL), F32),
        ],
        compiler_params=pltpu.CompilerParams(
            dimension_semantics=("arbitrary",), vmem_limit_bytes=VMEM_LIMIT),
        name="wconv_out",
    )(x2d, y_ssd, y_att, w_f32)


def _rope_tables(pos):
    n = pos.shape[0]
    half = ROT_DIM // 2
    inv = ROPE_THETA ** (-np.arange(0, ROT_DIM, 2, dtype=np.float64) / ROT_DIM)
    ang = pos.astype(np.float64)[:, None] * inv[None, :]
    cos, sin = np.cos(ang), np.sin(ang)
    rest = HEAD_DIM - ROT_DIM
    zh = np.zeros((n, half))
    cos_h = np.concatenate([cos, cos, np.ones((n, rest))], axis=1)
    sa_h = np.concatenate([zh, sin, np.zeros((n, rest))], axis=1)
    sb_h = np.concatenate([-sin, zh, np.zeros((n, rest))], axis=1)
    rep = LANES // HEAD_DIM
    return tuple(jnp.asarray(np.tile(t, (1, rep)), dtype=F32) for t in (cos_h, sa_h, sb_h))


def _lane_pad(v, n=LANES):
    return jnp.pad(v, (0, n - v.shape[0])).reshape(1, n)


def _to_cache(xt, batch, seq):
    return xt.reshape(1, batch, ATT_KV_HEADS, HEAD_DIM, seq).transpose(0, 1, 4, 2, 3)


def kernel(x_prompt, x_sample, cache_k, cache_v, state_conv, state_ssm, norm_w, w_in, conv_w,
           conv_b, dt_bias, a_log, d_skip, ssd_norm_w, q_norm_w, k_norm_w, w_out):
    bp, tp_, _ = x_prompt.shape
    bs, ts, _ = x_sample.shape
    depth = w_in.shape[0]
    assert depth == 1 and tp_ % SSD_CHUNK == 0 and ts <= SAMPLE_PAD and ts >= CONV_W - 1
    l = 0
    win = cache_k.shape[2]

    nw = norm_w[l].reshape(1, D_MODEL)
    rep = LANES // HEAD_DIM
    qnw = jnp.tile(q_norm_w[l], rep).reshape(1, LANES)
    knw = jnp.tile(k_norm_w[l], rep).reshape(1, LANES)
    cw = jnp.pad(conv_w[l], ((0, SUBLANES - CONV_W), (0, 0)))
    cb = conv_b[l].reshape(1, CONV_DIM)
    dtb = _lane_pad(dt_bias[l])
    alog = _lane_pad(a_log[l])
    dexp = jnp.repeat(d_skip[l], SSD_HEADDIM).reshape(1, SSD_INNER)
    snw = ssd_norm_w[l].reshape(1, SSD_INNER)

    pad = SAMPLE_PAD
    xs2 = jnp.pad(x_sample, ((0, 0), (0, pad - ts), (0, 0))).reshape(bs * pad, D_MODEL)
    tabs_s = _rope_tables(PAST_LEN + np.arange(bs * pad) % pad)
    w_t, proj_s = _wconv_in(xs2, nw, jnp.swapaxes(w_in[l], 0, 1))
    zs, xbc, dtr, q, kt, vt, gs, _ = _sections(proj_s, qnw, knw, tabs_s)
    cprev = jnp.pad(state_conv[l], ((0, 0), (SUBLANES - (CONV_W - 1), 0), (0, 0)))
    y_ssd, h_s = _ssd(xbc, dtr, zs, cprev, state_ssm[l].reshape(bs, SSD_INNER, SSD_STATE),
                      cw, cb, dtb, alog, dexp, snw, bs, pad, ts)
    knt = kt.reshape(KV_DIM, bs, pad).transpose(1, 0, 2)
    vnt = vt.reshape(KV_DIM, bs, pad).transpose(1, 0, 2)
    ckt = cache_k[l].transpose(0, 2, 3, 1).reshape(bs, KV_DIM, win)
    cvt = cache_v[l].transpose(0, 2, 3, 1).reshape(bs, KV_DIM, win)
    y_att, kot, vot = _attn_sample(q, knt, vnt, gs, ckt, cvt, ts)
    w_o, y_s = _wconv_out(xs2, y_ssd, y_att, w_out[l])
    y_s = y_s.reshape(bs, pad, D_MODEL)[:, :ts]
    k_s = _to_cache(kot, bs, win)
    v_s = _to_cache(vot, bs, win)
    c_s = xbc.reshape(bs, pad, CONV_DIM)[:, ts - (CONV_W - 1):ts][None]
    h_s = h_s.reshape(1, bs, SSD_HEADS, SSD_HEADDIM, SSD_STATE)

    tm = TM_IN
    xp2 = x_prompt.reshape(bp * tp_, D_MODEL)
    tabs = _rope_tables(np.arange(tp_))
    zs, xbc, dtr, q, kt, vt, gs, ctail = _in_proj(xp2, nw, w_t, qnw, knw, tabs, tm, tp_)
    y_ssd, h_p = _ssd(xbc, dtr, zs, jnp.zeros((bp, SUBLANES, CONV_DIM), F32),
                      jnp.zeros((bp, SSD_INNER, SSD_STATE), F32),
                      cw, cb, dtb, alog, dexp, snw, bp, SSD_CHUNK, SSD_CHUNK)
    y_att = _attn_prompt(q, kt, vt, gs, bp, tp_)
    y_p = _out_proj(xp2, y_ssd, y_att, w_o, TM_OUT).reshape(bp, tp_, D_MODEL)
    keep = min(W_MAX, tp_)
    k_p = _to_cache(kt, bp, tp_)[:, :, tp_ - keep:]
    v_p = _to_cache(vt, bp, tp_)[:, :, tp_ - keep:]
    c_p = ctail[:, SUBLANES - (CONV_W - 1):][None]
    h_p = h_p.reshape(1, bp, SSD_HEADS, SSD_HEADDIM, SSD_STATE)

    return (y_p, y_s, k_p, v_p, c_p, h_p, k_s, v_s, c_s, h_s)
```

```python
import functools

import jax
import jax.numpy as jnp
import numpy as np
from jax import lax
from jax.experimental import pallas as pl
from jax.experimental.pallas import tpu as pltpu

F32 = jnp.float32
BF16 = jnp.bfloat16

D_MODEL = 2048
SSD_HEADS = 16
SSD_HEADDIM = 64
SSD_INNER = SSD_HEADS * SSD_HEADDIM
SSD_GROUPS = 2
SSD_STATE = 128
CONV_W = 4
CONV_DIM = SSD_INNER + 2 * SSD_GROUPS * SSD_STATE
SSD_CHUNK = 128
ATT_HEADS = 16
ATT_KV_HEADS = 4
HEAD_DIM = 64
ATT_GQ = ATT_HEADS // ATT_KV_HEADS
ATT_INNER = ATT_HEADS * HEAD_DIM
KV_DIM = ATT_KV_HEADS * HEAD_DIM
ROT_DIM = HEAD_DIM // 4
ROPE_THETA = 500000.0
DILATED_BRANCHES = ((128, 1), (512, 4), (2048, 16))
W_MAX = 2048
PAST_LEN = 16384
EPS = 1e-6

LANES = 128
SUBLANES = 8
Q_BLOCK = 256
K_SUPER = 256
Q_PER_SUPER = K_SUPER // Q_BLOCK
ONES_ROWS = 16
ACC_ROWS = HEAD_DIM + ONES_ROWS
CONV_COLS = 256
TM_IN = 512
TM_OUT = 512
SSD_SEQS = 4
SAMPLE_PAD = 16
NEG = -1e30
VMEM_LIMIT = 58 * 1024 * 1024

Z0 = 0
X0 = Z0 + SSD_INNER
DT0 = X0 + CONV_DIM
Q0 = DT0 + SSD_HEADS
K0 = Q0 + ATT_INNER
V0 = K0 + KV_DIM
G0 = V0 + KV_DIM
W_ROWS = G0 + ATT_INNER
W_CHUNK = 1024
WO_CHUNK = 512
HALF_INNER = SSD_INNER // SSD_GROUPS
LOG2E = 1.4426950408889634
Q_SCALE = HEAD_DIM ** -0.5 * LOG2E


def _dot(a, b):
    return jnp.dot(a, b, preferred_element_type=F32)


def _dot_nt(a, b):
    return lax.dot_general(a, b, (((1,), (1,)), ((), ())), preferred_element_type=F32)


def _split3(x):
    hi = x.astype(BF16)
    r1 = x - hi.astype(F32)
    mid = r1.astype(BF16)
    lo = (r1 - mid.astype(F32)).astype(BF16)
    return hi, mid, lo


def _dot_exact_rhs(x, m):
    hi, mid, lo = _split3(x)
    return _dot(hi, m) + _dot(mid, m) + _dot(lo, m)


def _dot_wide_rhs(x, m):
    hi = x.astype(BF16)
    lo = (x - hi.astype(F32)).astype(BF16)
    return _dot(hi, m) + _dot(lo, m)


def _dot_exact_lhs(m, x):
    hi, mid, lo = _split3(x)
    return _dot(m, hi) + _dot(m, mid) + _dot(m, lo)


def _silu(x):
    hx = 0.5 * x
    return hx * jnp.tanh(hx) + hx


def _multiplicity(d):
    w = jnp.zeros(d.shape, F32)
    for window, dil in DILATED_BRANCHES:
        hit = (d >= 0) & (d <= window) & (lax.rem(d, dil) == 0)
        w = w + jnp.where(hit, 1.0, 0.0)
    return w


def _norm_rope(y, nw, cos, sa, sb, scale):
    lane = lax.broadcasted_iota(jnp.int32, (1, LANES), 1)
    first = lane < HEAD_DIM
    y2 = y * y
    s_lo = jnp.sum(jnp.where(first, y2, 0.0), axis=-1, keepdims=True)
    s_hi = jnp.sum(jnp.where(first, 0.0, y2), axis=-1, keepdims=True)
    ms = jnp.where(first, s_lo, s_hi) * (1.0 / HEAD_DIM)
    yn = y * lax.rsqrt(ms + EPS) * nw
    half = ROT_DIM // 2
    rot = yn * cos + pltpu.roll(yn, half, 1) * sa + pltpu.roll(yn, LANES - half, 1) * sb
    return rot * scale


def _causal_conv_silu(load, cw_ref, cb_ref, store):
    for c0 in range(0, CONV_DIM, CONV_COLS):
        cs = slice(c0, c0 + CONV_COLS)
        xe = load(cs)
        x2 = pltpu.roll(xe, 2, 0)
        even = cw_ref[3:4, cs] * xe + cw_ref[1:2, cs] * x2
        odd = cw_ref[2:3, cs] * xe + cw_ref[0:1, cs] * x2
        store(cs, _silu((even + pltpu.roll(odd, 1, 0))[SUBLANES:, :] + cb_ref[:, cs]))


def _pre_norm(x_ref, nw_ref):
    x = x_ref[...]
    ms = jnp.mean(x * x, axis=-1, keepdims=True)
    return (x * lax.rsqrt(ms + EPS) * nw_ref[...]).astype(BF16)


def _inproj_kernel(x_ref, nw_ref, w_ref, *refs):
    hn = _pre_norm(x_ref, nw_ref)

    def proj(a, b):
        return _dot_nt(hn, w_ref[a:b, :])

    _emit_sections(proj, x_ref.shape[0], *refs)


def _emit_sections(proj, tm, qnw_ref, knw_ref, cos_ref, sa_ref, sb_ref,
                   zs_ref, xbc_ref, dt_ref, q_ref, kt_ref, vt_ref, gs_ref, ctail_ref):
    cos, sa, sb = cos_ref[...], sa_ref[...], sb_ref[...]
    q = proj(Q0, K0)
    for c in range(ATT_INNER // LANES):
        sl = slice(c * LANES, (c + 1) * LANES)
        qn = _norm_rope(q[:, sl], qnw_ref[...], cos, sa, sb, Q_SCALE)
        if len(q_ref.shape) == 3:
            q_ref[0, sl, :] = qn.T.astype(BF16)
        else:
            q_ref[:, sl] = qn.astype(BF16)
    k = proj(K0, V0)
    kn = [_norm_rope(k[:, c * LANES:(c + 1) * LANES], knw_ref[...], cos, sa, sb, 1.0)
          for c in range(KV_DIM // LANES)]
    kt_ref[0] = jnp.concatenate(kn, axis=1).T

    xbc = proj(X0, DT0)
    xbc_ref[...] = xbc
    ctail_ref[0] = xbc[tm - SUBLANES:tm, :]
    vt_ref[0] = proj(V0, G0).T
    zs_ref[...] = _silu(proj(Z0, X0))
    gs = _silu(proj(G0, W_ROWS))
    if len(gs_ref.shape) == 3:
        gs_ref[0] = gs.T
    else:
        gs_ref[...] = gs
    dt_ref[...] = proj(DT0, DT0 + LANES)


def _in_proj(x2d, nw, w_t, qnw, knw, tables, tm, seq):
    rows = x2d.shape[0]
    const = lambda i: (0, 0)
    row = lambda i: (i, 0)
    sec_in, out_specs, out_shape = _section_specs(rows, tm, seq, tables[0].shape[0] // tm, True)
    return pl.pallas_call(
        _inproj_kernel,
        grid=(rows // tm,),
        in_specs=[
            pl.BlockSpec((tm, D_MODEL), row),
            pl.BlockSpec((1, D_MODEL), const),
            pl.BlockSpec((W_ROWS, D_MODEL), const, pipeline_mode=pl.Buffered(1)),
        ] + sec_in,
        out_specs=out_specs,
        out_shape=out_shape,
        compiler_params=pltpu.CompilerParams(
            dimension_semantics=("arbitrary",), vmem_limit_bytes=VMEM_LIMIT),
        name="in_proj",
    )(x2d, nw, w_t, qnw, knw, *tables)


def _section_specs(rows, tm, seq, period, q_transposed):
    per_seq = seq // tm
    const = lambda i: (0, 0)
    row = lambda i: (i, 0)
    tab = lambda i: (i % period, 0)
    tr = lambda i: (i // per_seq, 0, i % per_seq)
    slab = lambda i: (i // per_seq, 0, 0)
    in_specs = [pl.BlockSpec((1, LANES), const)] * 2 + [pl.BlockSpec((tm, LANES), tab)] * 3
    row_outs = {0: (SSD_INNER, F32), 1: (CONV_DIM, F32), 2: (LANES, F32), 3: (ATT_INNER, BF16),
                6: (ATT_INNER, F32)}
    out_specs, out_shape = [], []
    for idx in range(7):
        if idx in (3, 6) and q_transposed:
            dt = row_outs[idx][1]
            out_specs.append(pl.BlockSpec((1, ATT_INNER, tm), tr))
            out_shape.append(jax.ShapeDtypeStruct((rows // seq, ATT_INNER, seq), dt))
        elif idx in row_outs:
            n, dt = row_outs[idx]
            out_specs.append(pl.BlockSpec((tm, n), row))
            out_shape.append(jax.ShapeDtypeStruct((rows, n), dt))
        else:
            out_specs.append(pl.BlockSpec((1, KV_DIM, tm), tr))
            out_shape.append(jax.ShapeDtypeStruct((rows // seq, KV_DIM, seq), F32))
    out_specs.append(pl.BlockSpec((1, SUBLANES, CONV_DIM), slab))
    out_shape.append(jax.ShapeDtypeStruct((rows // seq, SUBLANES, CONV_DIM), F32))
    return in_specs, out_specs, out_shape


def _wconv_in_kernel(x_ref, nw_ref, w_ref, wb_ref, p_ref, hn_scr):
    @pl.when(pl.program_id(0) == 0)
    def _():
        hn_scr[...] = _pre_norm(x_ref, nw_ref)

    row = pl.program_id(0) * W_CHUNK + lax.broadcasted_iota(jnp.int32, (W_CHUNK, 1), 0)
    wb = jnp.where(row < W_ROWS, w_ref[...], 0.0).astype(BF16)
    wb_ref[...] = wb
    p_ref[0] = _dot_nt(hn_scr[...], wb)


def _wconv_in(x2d, nw, w_f32):
    rows = x2d.shape[0]
    n_chunks = pl.cdiv(W_ROWS, W_CHUNK)
    const = lambda j: (0, 0)
    return pl.pallas_call(
        _wconv_in_kernel,
        grid=(n_chunks,),
        in_specs=[
            pl.BlockSpec((rows, D_MODEL), const),
            pl.BlockSpec((1, D_MODEL), const),
            pl.BlockSpec((W_CHUNK, D_MODEL), lambda j: (j, 0)),
        ],
        out_specs=[
            pl.BlockSpec((W_CHUNK, D_MODEL), lambda j: (j, 0)),
            pl.BlockSpec((1, rows, W_CHUNK), lambda j: (j, 0, 0)),
        ],
        out_shape=[
            jax.ShapeDtypeStruct((W_ROWS, D_MODEL), BF16),
            jax.ShapeDtypeStruct((n_chunks, rows, W_CHUNK), F32),
        ],
        scratch_shapes=[pltpu.VMEM((rows, D_MODEL), BF16)],
        compiler_params=pltpu.CompilerParams(
            dimension_semantics=("arbitrary",), vmem_limit_bytes=VMEM_LIMIT),
        name="wconv_in",
    )(x2d, nw, w_f32)


def _sections_kernel(p_ref, *refs):
    *refs, tail_scr = refs
    pfull = jnp.concatenate([p_ref[j] for j in range(p_ref.shape[0])], axis=1)
    tail_scr[...] = pfull[:, Q0:W_ROWS]

    def proj(a, b):
        if a >= Q0:
            return tail_scr[:, a - Q0:b - Q0]
        return pfull[:, a:b]

    _emit_sections(proj, p_ref.shape[1], *refs)


def _sections(p, qnw, knw, tables):
    rows = p.shape[1]
    sec_in, out_specs, out_shape = _section_specs(rows, rows, rows, 1, False)
    return pl.pallas_call(
        _sections_kernel,
        grid=(1,),
        in_specs=[pl.BlockSpec(p.shape, lambda i: (0, 0, 0))] + sec_in,
        out_specs=out_specs,
        out_shape=out_shape,
        scratch_shapes=[pltpu.VMEM((rows, W_ROWS - Q0), F32)],
        compiler_params=pltpu.CompilerParams(
            dimension_semantics=("arbitrary",), vmem_limit_bytes=VMEM_LIMIT),
        name="sections",
    )(p, qnw, knw, *tables)


def _ssd_chain(s, xbc_ref, dtr_ref, z_ref, cprev_ref, h0_ref, cw_ref, cb_ref, dtb_ref, alog_ref,
               dexp_ref, nw_ref, y_ref, hout_ref, xext, ht, xc_scr, lin, valid):
    L = SSD_CHUNK
    c = pl.program_id(1)
    last = pl.num_programs(1) - 1

    @pl.when(c == 0)
    def _():
        xext[s, 0:SUBLANES, :] = cprev_ref[s]
        for g in range(SSD_GROUPS):
            ht[s, g] = h0_ref[s, g * HALF_INNER:(g + 1) * HALF_INNER, :].T

    def pad_rows(v):
        if lin == L:
            return v
        return jnp.concatenate([v, jnp.zeros((L - lin, v.shape[1]), v.dtype)], axis=0)

    lane = lax.broadcasted_iota(jnp.int32, (1, LANES), 1)
    rowid = lax.broadcasted_iota(jnp.int32, (L, 1), 0)
    dt = jax.nn.softplus(pad_rows(dtr_ref[s]) + dtb_ref[...])
    dt = jnp.where((lane < SSD_HEADS) & (rowid < valid), dt, 0.0)
    dta = dt * (-jnp.exp(alog_ref[...]) * LOG2E)

    r2 = lax.broadcasted_iota(jnp.int32, (L, L), 0)
    c2 = lax.broadcasted_iota(jnp.int32, (L, L), 1)
    tri = r2 >= c2
    tri_b = jnp.where(tri, 1.0, 0.0).astype(BF16)
    cum = _dot_exact_lhs(tri_b, dta)
    cum_last = cum[L - 1:L, :]
    yield

    er = lax.broadcasted_iota(jnp.int32, (LANES, SSD_INNER), 0)
    ec = lax.broadcasted_iota(jnp.int32, (LANES, SSD_INNER), 1)
    expand = jnp.where(ec // SSD_HEADDIM == er, 1.0, 0.0).astype(BF16)
    ecum = _dot_wide_rhs(jnp.exp2(cum), expand)
    wexp = _dot_wide_rhs(jnp.exp2(cum_last - cum) * dt, expand)
    cd = _dot_exact_rhs(jnp.broadcast_to(jnp.exp2(cum_last), (SUBLANES, LANES)), expand)[0:1, :]
    cum_t = cum.T
    dt_t = dt.T
    yield

    if lin < L:
        xext[s, SUBLANES + lin:SUBLANES + L, :] = jnp.zeros((L - lin, CONV_DIM), F32)
    xext[s, SUBLANES:SUBLANES + lin, :] = xbc_ref[s]

    def load(cs):
        return xext[s, 0:SUBLANES + L, cs]

    def store(cs, v):
        xc_scr[s, :, cs] = v

    _causal_conv_silu(load, cw_ref, cb_ref, store)
    xext[s, 0:SUBLANES, :] = xext[s, L:L + SUBLANES, :]
    xc = xc_scr.at[s]
    xs = xc[:, :SSD_INNER]
    bm = xc[:, SSD_INNER:SSD_INNER + SSD_GROUPS * SSD_STATE]
    cm = xc[:, SSD_INNER + SSD_GROUPS * SSD_STATE:]
    bm_b = bm.astype(BF16)
    cm_b = cm.astype(BF16)
    xs_b = xs.astype(BF16)

    y_off = jnp.concatenate(
        [_dot(cm_b[:, g * SSD_STATE:(g + 1) * SSD_STATE], ht[s, g].astype(BF16))
         for g in range(SSD_GROUPS)], axis=1) * ecum
    cbs = [_dot_nt(cm_b[:, g * SSD_STATE:(g + 1) * SSD_STATE], bm_b[:, g * SSD_STATE:(g + 1) * SSD_STATE])
           for g in range(SSD_GROUPS)]
    yield

    lane_half = lane // HEAD_DIM
    heads_per_group = SSD_HEADS // SSD_GROUPS
    y_parts = []
    for g in range(SSD_GROUPS):
        for pr in range(heads_per_group // 2):
            col0 = g * HALF_INNER + pr * LANES
            xp = xs_b[:, col0:col0 + LANES]
            yp = jnp.zeros((L, LANES), F32)
            for e in range(2):
                hh = g * heads_per_group + pr * 2 + e
                decay = jnp.exp2(cum[:, hh:hh + 1] - cum_t[hh:hh + 1, :])
                sc = jnp.where(tri, cbs[g] * decay, 0.0) * dt_t[hh:hh + 1, :]
                xm = jnp.where(lane_half == e, xp, jnp.zeros_like(xp))
                yp = yp + _dot(sc.astype(BF16), xm)
            y_parts.append(yp)
            if pr % 2 == 1:
                yield
    y_diag = jnp.concatenate(y_parts, axis=1)

    xw = (xs * wexp).astype(BF16)
    for g in range(SSD_GROUPS):
        sl = slice(g * HALF_INNER, (g + 1) * HALF_INNER)
        bm_t = bm[:, g * SSD_STATE:(g + 1) * SSD_STATE].T.astype(BF16)
        ht[s, g] = ht[s, g] * cd[:, sl] + _dot(bm_t, xw[:, sl])
    yield

    y = (y_diag + y_off + dexp_ref[...] * xs) * pad_rows(z_ref[s])
    yn = []
    for g in range(SSD_GROUPS):
        yg = y[:, g * HALF_INNER:(g + 1) * HALF_INNER]
        ms = jnp.mean(yg * yg, axis=-1, keepdims=True)
        yn.append(yg * lax.rsqrt(ms + EPS))
    yn = jnp.concatenate(yn, axis=1) * nw_ref[...]
    y_ref[s] = yn[:lin].astype(BF16)

    @pl.when(c == last)
    def _():
        for g in range(SSD_GROUPS):
            hout_ref[s, g * HALF_INNER:(g + 1) * HALF_INNER, :] = ht[s, g].T


def _ssd_kernel(*refs, lin, valid):
    chains = [_ssd_chain(s, *refs, lin, valid) for s in range(SSD_SEQS)]
    while chains:
        alive = []
        for ch in chains:
            try:
                next(ch)
                alive.append(ch)
            except StopIteration:
                pass
        chains = alive


def _ssd(xbc, dtr, z, cprev, h0, cw, cb, dtb, alog, dexp, nw, batch, lin, valid):
    rows = xbc.shape[0]
    seq = rows // batch
    nc = seq // lin
    g = SSD_SEQS
    assert batch % g == 0
    blk = lambda b, c: (b, c, 0)
    per_b = lambda b, c: (b, 0, 0)
    const = lambda b, c: (0, 0)
    y, hout = pl.pallas_call(
        functools.partial(_ssd_kernel, lin=lin, valid=valid),
        grid=(batch // g, nc),
        in_specs=[
            pl.BlockSpec((g, lin, CONV_DIM), blk),
            pl.BlockSpec((g, lin, LANES), blk),
            pl.BlockSpec((g, lin, SSD_INNER), blk),
            pl.BlockSpec((g, SUBLANES, CONV_DIM), per_b),
            pl.BlockSpec((g, SSD_INNER, SSD_STATE), per_b),
            pl.BlockSpec((SUBLANES, CONV_DIM), const),
            pl.BlockSpec((1, CONV_DIM), const),
            pl.BlockSpec((1, LANES), const),
            pl.BlockSpec((1, LANES), const),
            pl.BlockSpec((1, SSD_INNER), const),
            pl.BlockSpec((1, SSD_INNER), const),
        ],
        out_specs=[
            pl.BlockSpec((g, lin, SSD_INNER), blk),
            pl.BlockSpec((g, SSD_INNER, SSD_STATE), per_b),
        ],
        out_shape=[
            jax.ShapeDtypeStruct((batch, seq, SSD_INNER), BF16),
            jax.ShapeDtypeStruct((batch, SSD_INNER, SSD_STATE), F32),
        ],
        scratch_shapes=[
            pltpu.VMEM((g, SUBLANES + SSD_CHUNK + SUBLANES, CONV_DIM), F32),
            pltpu.VMEM((g, SSD_GROUPS, SSD_STATE, HALF_INNER), F32),
            pltpu.VMEM((g, SSD_CHUNK, CONV_DIM), F32),
        ],
        compiler_params=pltpu.CompilerParams(
            dimension_semantics=("arbitrary", "arbitrary"), vmem_limit_bytes=VMEM_LIMIT),
        name="ssd",
    )(xbc.reshape(batch, seq, CONV_DIM), dtr.reshape(batch, seq, LANES),
      z.reshape(batch, seq, SSD_INNER), cprev, h0, cw, cb, dtb, alog, dexp, nw)
    return y.reshape(rows, SSD_INNER), hout


def _attn_prompt_kernel(q_ref, kt_ref, vt_ref, g_ref, o_ref,
                        bias_scr, kh_scr, va_scr, qt_scr, m_scr, acc_scr, s_scr, s2_scr, *, n_tab, n_sb_max):
    b = pl.program_id(0)
    i = pl.program_id(1)
    seq = kt_ref.shape[2]
    pair_w = 2 * Q_BLOCK
    pairs_per_kv = ATT_GQ // 2

    @pl.when((b == 0) & (i == 0))
    def _():
        r = lax.broadcasted_iota(jnp.int32, (K_SUPER, Q_BLOCK), 0)
        c = lax.broadcasted_iota(jnp.int32, (K_SUPER, Q_BLOCK), 1)
        for tb in range(n_tab):
            w = _multiplicity(tb * Q_BLOCK + c - r)
            bias_scr[tb] = jnp.where(w > 0.0, jnp.log2(jnp.maximum(w, 1.0)), NEG)
        va_scr[:, HEAD_DIM:, :] = jnp.ones((ATT_KV_HEADS, ONES_ROWS, seq), BF16)

    @pl.when(i == 0)
    def _():
        for kvh in range(ATT_KV_HEADS):
            hs = slice(kvh * HEAD_DIM, (kvh + 1) * HEAD_DIM)
            va_scr[kvh, 0:HEAD_DIM, :] = vt_ref[0, hs, :].astype(BF16)
        for cidx in range(seq // K_SUPER):
            rows = slice(cidx * K_SUPER, (cidx + 1) * K_SUPER)
            kc = kt_ref[0, :, rows].T
            for kvh in range(ATT_KV_HEADS):
                hs = slice(kvh * HEAD_DIM, (kvh + 1) * HEAD_DIM)
                kh_scr[kvh, rows, :] = kc[:, hs].astype(BF16)

    for kvh in range(ATT_KV_HEADS):
        qt_scr[kvh] = jnp.concatenate(
            [q_ref[0, (kvh * ATT_GQ + g) * HEAD_DIM:(kvh * ATT_GQ + g + 1) * HEAD_DIM, :]
             for g in range(ATT_GQ)], axis=1)
    m_scr[...] = jnp.full(m_scr.shape, NEG, F32)
    acc_scr[...] = jnp.zeros(acc_scr.shape, F32)
    par = i % Q_PER_SUPER
    j_last = i // Q_PER_SUPER
    n_sb = jnp.minimum(j_last + 1, n_sb_max)

    def key_start(dl):
        return pl.multiple_of((j_last - dl) * K_SUPER, K_SUPER)

    def scores(dl, dst):
        for kvh in range(ATT_KV_HEADS):
            dst[kvh] = _dot(kh_scr[kvh, pl.ds(key_start(dl), K_SUPER), :], qt_scr[kvh])

    def softmax_pv(dl, src):
        start = key_start(dl)
        bias = bias_scr[par + Q_PER_SUPER * dl]
        bias2 = jnp.concatenate([bias, bias], axis=1)
        for kvh in range(ATT_KV_HEADS):
            vtb = va_scr[kvh, :, pl.ds(start, K_SUPER)]
            for pr in range(pairs_per_kv):
                u = kvh * pairs_per_kv + pr
                s = src[kvh, :, pr * pair_w:(pr + 1) * pair_w] + bias2
                m_old = m_scr[u, 0:1, :]
                m_new = jnp.maximum(m_old, jnp.max(s, axis=0, keepdims=True))
                p = jnp.exp2(s - m_new).astype(BF16)
                acc_scr[u] = jnp.exp2(m_old - m_new) * acc_scr[u] + _dot(vtb, p)
                m_scr[u, 0:1, :] = m_new

    def body(t, carry):
        scores(2 * t, s_scr)
        scores(2 * t + 1, s2_scr)
        softmax_pv(2 * t, s_scr)
        softmax_pv(2 * t + 1, s2_scr)
        return carry

    lax.fori_loop(0, n_sb // 2, body, 0)

    @pl.when(n_sb % 2 == 1)
    def _():
        scores(n_sb - 1, s_scr)
        softmax_pv(n_sb - 1, s_scr)

    for u in range(ATT_HEADS // 2):
        acc = acc_scr[u]
        o = acc[0:HEAD_DIM] / acc[HEAD_DIM:HEAD_DIM + 1]
        for e in range(2):
            hs = slice((2 * u + e) * HEAD_DIM, (2 * u + e + 1) * HEAD_DIM)
            o_ref[0, hs, :] = (o[:, e * Q_BLOCK:(e + 1) * Q_BLOCK] * g_ref[0, hs, :]).astype(BF16)


def _attn_prompt(qt, kt, vt, gate_t, batch, seq):
    nqb = seq // Q_BLOCK
    assert seq % K_SUPER == 0
    n_sb_max = min(seq // K_SUPER, W_MAX // K_SUPER + 1)
    n_tab = Q_PER_SUPER * n_sb_max
    qblk = lambda b, i: (b, 0, i)
    per_b = lambda b, i: (b, 0, 0)
    return pl.pallas_call(
        functools.partial(_attn_prompt_kernel, n_tab=n_tab, n_sb_max=n_sb_max),
        grid=(batch, nqb),
        in_specs=[
            pl.BlockSpec((1, ATT_INNER, Q_BLOCK), qblk),
            pl.BlockSpec((1, KV_DIM, seq), per_b),
            pl.BlockSpec((1, KV_DIM, seq), per_b),
            pl.BlockSpec((1, ATT_INNER, Q_BLOCK), qblk),
        ],
        out_specs=pl.BlockSpec((1, ATT_INNER, Q_BLOCK), qblk),
        out_shape=jax.ShapeDtypeStruct((batch, ATT_INNER, seq), BF16),
        scratch_shapes=[
            pltpu.VMEM((n_tab, K_SUPER, Q_BLOCK), F32),
            pltpu.VMEM((ATT_KV_HEADS, seq, HEAD_DIM), BF16),
            pltpu.VMEM((ATT_KV_HEADS, ACC_ROWS, seq), BF16),
            pltpu.VMEM((ATT_KV_HEADS, HEAD_DIM, ATT_GQ * Q_BLOCK), BF16),
            pltpu.VMEM((ATT_HEADS // 2, SUBLANES, 2 * Q_BLOCK), F32),
            pltpu.VMEM((ATT_HEADS // 2, ACC_ROWS, 2 * Q_BLOCK), F32),
            pltpu.VMEM((ATT_KV_HEADS, K_SUPER, ATT_GQ * Q_BLOCK), F32),
            pltpu.VMEM((ATT_KV_HEADS, K_SUPER, ATT_GQ * Q_BLOCK), F32),
        ],
        compiler_params=pltpu.CompilerParams(
            dimension_semantics=("arbitrary", "arbitrary"), vmem_limit_bytes=VMEM_LIMIT),
        name="attn_prompt",
    )(qt, kt, vt, gate_t)


def _attn_sample_kernel(q_ref, knt_ref, vnt_ref, g_ref, ckt_ref, cvt_ref, o_ref, kot_ref, vot_ref, *, ts):
    win = ckt_ref.shape[2]
    tp = SAMPLE_PAD
    rows = ATT_GQ * tp
    t_c = lax.broadcasted_iota(jnp.int32, (rows, win), 0) % tp
    j_c = lax.broadcasted_iota(jnp.int32, (rows, win), 1)
    w_c = _multiplicity(win + t_c - j_c)
    t_n = lax.broadcasted_iota(jnp.int32, (rows, tp), 0) % tp
    j_n = lax.broadcasted_iota(jnp.int32, (rows, tp), 1)
    w_n = jnp.where(j_n < ts, _multiplicity(t_n - j_n), 0.0)
    for kvh in range(ATT_KV_HEADS):
        c0 = kvh * ATT_GQ * HEAD_DIM
        qh = jnp.concatenate(
            [q_ref[:, c0 + g * HEAD_DIM:c0 + (g + 1) * HEAD_DIM] for g in range(ATT_GQ)], axis=0)
        hs = slice(kvh * HEAD_DIM, (kvh + 1) * HEAD_DIM)
        s_c = jnp.where(w_c > 0.0, _dot(qh, ckt_ref[0, hs, :].astype(BF16)), NEG)
        s_n = jnp.where(w_n > 0.0, _dot(qh, knt_ref[0, hs, :].astype(BF16)), NEG)
        m = jnp.maximum(jnp.max(s_c, axis=-1, keepdims=True), jnp.max(s_n, axis=-1, keepdims=True))
        p_c = jnp.exp2(s_c - m) * w_c
        p_n = jnp.exp2(s_n - m) * w_n
        den = jnp.sum(p_c, axis=-1, keepdims=True) + jnp.sum(p_n, axis=-1, keepdims=True)
        num = (_dot_nt(p_c.astype(BF16), cvt_ref[0, hs, :].astype(BF16))
               + _dot_nt(p_n.astype(BF16), vnt_ref[0, hs, :].astype(BF16)))
        o = num / den
        o = jnp.concatenate([o[g * tp:(g + 1) * tp] for g in range(ATT_GQ)], axis=1)
        csl = slice(c0, c0 + ATT_GQ * HEAD_DIM)
        o_ref[:, csl] = (o * g_ref[:, csl]).astype(BF16)
    kot_ref[0, :, 0:win - ts] = ckt_ref[0, :, ts:win]
    kot_ref[0, :, win - ts:win] = knt_ref[0, :, 0:ts]
    vot_ref[0, :, 0:win - ts] = cvt_ref[0, :, ts:win]
    vot_ref[0, :, win - ts:win] = vnt_ref[0, :, 0:ts]


def _attn_sample(q, knt, vnt, gate, ckt, cvt, ts):
    batch, _, win = ckt.shape
    tp = SAMPLE_PAD
    blk = lambda b: (b, 0)
    per_b = lambda b: (b, 0, 0)
    return pl.pallas_call(
        functools.partial(_attn_sample_kernel, ts=ts),
        grid=(batch,),
        in_specs=[
            pl.BlockSpec((tp, ATT_INNER), blk),
            pl.BlockSpec((1, KV_DIM, tp), per_b),
            pl.BlockSpec((1, KV_DIM, tp), per_b),
            pl.BlockSpec((tp, ATT_INNER), blk),
            pl.BlockSpec((1, KV_DIM, win), per_b),
            pl.BlockSpec((1, KV_DIM, win), per_b),
        ],
        out_specs=[
            pl.BlockSpec((tp, ATT_INNER), blk),
            pl.BlockSpec((1, KV_DIM, win), per_b),
            pl.BlockSpec((1, KV_DIM, win), per_b),
        ],
        out_shape=[
            jax.ShapeDtypeStruct((batch * tp, ATT_INNER), BF16),
            jax.ShapeDtypeStruct((batch, KV_DIM, win), F32),
            jax.ShapeDtypeStruct((batch, KV_DIM, win), F32),
        ],
        compiler_params=pltpu.CompilerParams(
            dimension_semantics=("arbitrary",), vmem_limit_bytes=VMEM_LIMIT),
        name="attn_sample",
    )(q, knt, vnt, gate, ckt, cvt)


def _outproj_kernel(x_ref, ys_ref, yat_ref, w_ref, o_ref):
    part = x_ref[...] + _dot(ys_ref[...], w_ref[0:SSD_INNER, :])
    ya = yat_ref[0].astype(F32).T.astype(BF16)
    o_ref[...] = part + _dot(ya, w_ref[SSD_INNER:, :])


def _out_proj(x2d, y_ssd, y_att_t, w_o, tm):
    rows = x2d.shape[0]
    per_seq = y_att_t.shape[2] // tm
    row = lambda i: (i, 0)
    const = lambda i: (0, 0)
    return pl.pallas_call(
        _outproj_kernel,
        grid=(rows // tm,),
        in_specs=[
            pl.BlockSpec((tm, D_MODEL), row),
            pl.BlockSpec((tm, SSD_INNER), row),
            pl.BlockSpec((1, ATT_INNER, tm), lambda i: (i // per_seq, 0, i % per_seq)),
            pl.BlockSpec((SSD_INNER + ATT_INNER, D_MODEL), const, pipeline_mode=pl.Buffered(1)),
        ],
        out_specs=pl.BlockSpec((tm, D_MODEL), row),
        out_shape=jax.ShapeDtypeStruct((rows, D_MODEL), F32),
        compiler_params=pltpu.CompilerParams(
            dimension_semantics=("arbitrary",), vmem_limit_bytes=VMEM_LIMIT),
        name="out_proj",
    )(x2d, y_ssd, y_att_t, w_o)


def _wconv_out_kernel(x_ref, ys_ref, ya_ref, w_ref, wb_ref, o_ref):
    j = pl.program_id(0)
    wb = w_ref[...].astype(BF16)
    wb_ref[...] = wb
    y = jnp.where(j < SSD_INNER // WO_CHUNK, ys_ref[...], ya_ref[...])
    part = _dot(y, wb)

    @pl.when(j == 0)
    def _():
        o_ref[...] = x_ref[...] + part

    @pl.when(j > 0)
    def _():
        o_ref[...] += part


def _wconv_out(x2d, y_ssd, y_att, w_f32):
    rows = x2d.shape[0]
    n_ssd = SSD_INNER // WO_CHUNK
    n_chunks = (SSD_INNER + ATT_INNER) // WO_CHUNK
    const = lambda j: (0, 0)
    return pl.pallas_call(
        _wconv_out_kernel,
        grid=(n_chunks,),
        in_specs=[
            pl.BlockSpec((rows, D_MODEL), const),
            pl.BlockSpec((rows, WO_CHUNK), lambda j: (0, jnp.minimum(j, n_ssd - 1))),
            pl.BlockSpec((rows, WO_CHUNK), lambda j: (0, jnp.maximum(j - n_ssd, 0))),
            pl.BlockSpec((WO_CHUNK, D_MODEL), lambda j: (j, 0)),
        ],
        out_specs=[
            pl.BlockSpec((WO_CHUNK, D_MODEL), lambda j: (j, 0)),
            pl.BlockSpec((rows, D_MODEL), const),
        ],
        out_shape=[
            jax.ShapeDtypeStruct((SSD_INNER + ATT_INNER, D_MODEL), BF16),
            jax.ShapeDtypeStruct((rows, D_MODEL), F32),
        ],
        compiler_params=pltpu.CompilerParams(
            dimension_semantics=("arbitrary",), vmem_limit_bytes=VMEM_LIMIT),
        name="wconv_out",
    )(x2d, y_ssd, y_att, w_f32)


def _rope_tables(pos):
    n = pos.shape[0]
    half = ROT_DIM // 2
    inv = ROPE_THETA ** (-np.arange(0, ROT_DIM, 2, dtype=np.float64) / ROT_DIM)
    ang = pos.astype(np.float64)[:, None] * inv[None, :]
    cos, sin = np.cos(ang), np.sin(ang)
    rest = HEAD_DIM - ROT_DIM
    zh = np.zeros((n, half))
    cos_h = np.concatenate([cos, cos, np.ones((n, rest))], axis=1)
    sa_h = np.concatenate([zh, sin, np.zeros((n, rest))], axis=1)
    sb_h = np.concatenate([-sin, zh, np.zeros((n, rest))], axis=1)
    rep = LANES // HEAD_DIM
    return tuple(jnp.asarray(np.tile(t, (1, rep)), dtype=F32) for t in (cos_h, sa_h, sb_h))


def _lane_pad(v, n=LANES):
    return jnp.pad(v, (0, n - v.shape[0])).reshape(1, n)


def _to_cache(xt, batch, seq):
    return xt.reshape(1, batch, ATT_KV_HEADS, HEAD_DIM, seq).transpose(0, 1, 4, 2, 3)


def kernel(x_prompt, x_sample, cache_k, cache_v, state_conv, state_ssm, norm_w, w_in, conv_w,
           conv_b, dt_bias, a_log, d_skip, ssd_norm_w, q_norm_w, k_norm_w, w_out):
    bp, tp_, _ = x_prompt.shape
    bs, ts, _ = x_sample.shape
    depth = w_in.shape[0]
    assert depth == 1 and tp_ % SSD_CHUNK == 0 and ts <= SAMPLE_PAD and ts >= CONV_W - 1
    l = 0
    win = cache_k.shape[2]

    nw = norm_w[l].reshape(1, D_MODEL)
    rep = LANES // HEAD_DIM
    qnw = jnp.tile(q_norm_w[l], rep).reshape(1, LANES)
    knw = jnp.tile(k_norm_w[l], rep).reshape(1, LANES)
    cw = jnp.pad(conv_w[l], ((0, SUBLANES - CONV_W), (0, 0)))
    cb = conv_b[l].reshape(1, CONV_DIM)
    dtb = _lane_pad(dt_bias[l])
    alog = _lane_pad(a_log[l])
    dexp = jnp.repeat(d_skip[l], SSD_HEADDIM).reshape(1, SSD_INNER)
    snw = ssd_norm_w[l].reshape(1, SSD_INNER)

    pad = SAMPLE_PAD
    xs2 = jnp.pad(x_sample, ((0, 0), (0, pad - ts), (0, 0))).reshape(bs * pad, D_MODEL)
    tabs_s = _rope_tables(PAST_LEN + np.arange(bs * pad) % pad)
    w_t, proj_s = _wconv_in(xs2, nw, jnp.swapaxes(w_in[l], 0, 1))
    zs, xbc, dtr, q, kt, vt, gs, _ = _sections(proj_s, qnw, knw, tabs_s)
    cprev = jnp.pad(state_conv[l], ((0, 0), (SUBLANES - (CONV_W - 1), 0), (0, 0)))
    y_ssd, h_s = _ssd(xbc, dtr, zs, cprev, state_ssm[l].reshape(bs, SSD_INNER, SSD_STATE),
                      cw, cb, dtb, alog, dexp, snw, bs, pad, ts)
    knt = kt.reshape(KV_DIM, bs, pad).transpose(1, 0, 2)
    vnt = vt.reshape(KV_DIM, bs, pad).transpose(1, 0, 2)
    ckt = cache_k[l].transpose(0, 2, 3, 1).reshape(bs, KV_DIM, win)
    cvt = cache_v[l].transpose(0, 2, 3, 1).reshape(bs, KV_DIM, win)
    y_att, kot, vot = _attn_sample(q, knt, vnt, gs, ckt, cvt, ts)
    w_o, y_s = _wconv_out(xs2, y_ssd, y_att, w_out[l])
    y_s = y_s.reshape(bs, pad, D_MODEL)[:, :ts]
    k_s = _to_cache(kot, bs, win)
    v_s = _to_cache(vot, bs, win)
    c_s = xbc.reshape(bs, pad, CONV_DIM)[:, ts - (CONV_W - 1):ts][None]
    h_s = h_s.reshape(1, bs, SSD_HEADS, SSD_HEADDIM, SSD_STATE)

    tm = TM_IN
    xp2 = x_prompt.reshape(bp * tp_, D_MODEL)
    tabs = _rope_tables(np.arange(tp_))
    zs, xbc, dtr, q, kt, vt, gs, ctail = _in_proj(xp2, nw, w_t, qnw, knw, tabs, tm, tp_)
    y_ssd, h_p = _ssd(xbc, dtr, zs, jnp.zeros((bp, SUBLANES, CONV_DIM), F32),
                      jnp.zeros((bp, SSD_INNER, SSD_STATE), F32),
                      cw, cb, dtb, alog, dexp, snw, bp, SSD_CHUNK, SSD_CHUNK)
    y_att = _attn_prompt(q, kt, vt, gs, bp, tp_)
    y_p = _out_proj(xp2, y_ssd, y_att, w_o, TM_OUT).reshape(bp, tp_, D_MODEL)
    keep = min(W_MAX, tp_)
    k_p = _to_cache(kt, bp, tp_)[:, :, tp_ - keep:]
    v_p = _to_cache(vt, bp, tp_)[:, :, tp_ - keep:]
    c_p = ctail[:, SUBLANES - (CONV_W - 1):][None]
    h_p = h_p.reshape(1, bp, SSD_HEADS, SSD_HEADDIM, SSD_STATE)

    return (y_p, y_s, k_p, v_p, c_p, h_p, k_s, v_s, c_s, h_s)
```

```python
import functools

import jax
import jax.numpy as jnp
import numpy as np
from jax import lax
from jax.experimental import pallas as pl
from jax.experimental.pallas import tpu as pltpu

F32 = jnp.float32
BF16 = jnp.bfloat16

D_MODEL = 2048
SSD_HEADS = 16
SSD_HEADDIM = 64
SSD_INNER = SSD_HEADS * SSD_HEADDIM
SSD_GROUPS = 2
SSD_STATE = 128
CONV_W = 4
CONV_DIM = SSD_INNER + 2 * SSD_GROUPS * SSD_STATE
SSD_CHUNK = 128
ATT_HEADS = 16
ATT_KV_HEADS = 4
HEAD_DIM = 64
ATT_GQ = ATT_HEADS // ATT_KV_HEADS
ATT_INNER = ATT_HEADS * HEAD_DIM
KV_DIM = ATT_KV_HEADS * HEAD_DIM
ROT_DIM = HEAD_DIM // 4
ROPE_THETA = 500000.0
DILATED_BRANCHES = ((128, 1), (512, 4), (2048, 16))
W_MAX = 2048
PAST_LEN = 16384
EPS = 1e-6

LANES = 128
SUBLANES = 8
Q_BLOCK = 256
K_SUPER = 256
Q_PER_SUPER = K_SUPER // Q_BLOCK
ONES_ROWS = 16
ACC_ROWS = HEAD_DIM + ONES_ROWS
CONV_COLS = 256
TM_IN = 512
TM_OUT = 512
SSD_SEQS = 4
SAMPLE_PAD = 16
NEG = -1e30
VMEM_LIMIT = 58 * 1024 * 1024

Z0 = 0
X0 = Z0 + SSD_INNER
DT0 = X0 + CONV_DIM
Q0 = DT0 + SSD_HEADS
K0 = Q0 + ATT_INNER
V0 = K0 + KV_DIM
G0 = V0 + KV_DIM
W_ROWS = G0 + ATT_INNER
W_CHUNK = 1024
HALF_INNER = SSD_INNER // SSD_GROUPS
WO_CHUNK = HALF_INNER
LOG2E = 1.4426950408889634
Q_SCALE = HEAD_DIM ** -0.5 * LOG2E


def _dot(a, b):
    return jnp.dot(a, b, preferred_element_type=F32)


def _dot_nt(a, b):
    return lax.dot_general(a, b, (((1,), (1,)), ((), ())), preferred_element_type=F32)


def _split3(x):
    hi = x.astype(BF16)
    r1 = x - hi.astype(F32)
    mid = r1.astype(BF16)
    lo = (r1 - mid.astype(F32)).astype(BF16)
    return hi, mid, lo


def _dot_exact_rhs(x, m):
    hi, mid, lo = _split3(x)
    return _dot(hi, m) + _dot(mid, m) + _dot(lo, m)


def _dot_wide_rhs(x, m):
    hi = x.astype(BF16)
    lo = (x - hi.astype(F32)).astype(BF16)
    return _dot(hi, m) + _dot(lo, m)


def _dot_exact_lhs(m, x):
    hi, mid, lo = _split3(x)
    return _dot(m, hi) + _dot(m, mid) + _dot(m, lo)


def _silu(x):
    hx = 0.5 * x
    return hx * jnp.tanh(hx) + hx


def _multiplicity(d):
    w = jnp.zeros(d.shape, F32)
    for window, dil in DILATED_BRANCHES:
        hit = (d >= 0) & (d <= window) & (lax.rem(d, dil) == 0)
        w = w + jnp.where(hit, 1.0, 0.0)
    return w


def _norm_rope(y, nw, cos, sa, sb, scale):
    lane = lax.broadcasted_iota(jnp.int32, (1, LANES), 1)
    first = lane < HEAD_DIM
    y2 = y * y
    s_lo = jnp.sum(jnp.where(first, y2, 0.0), axis=-1, keepdims=True)
    s_hi = jnp.sum(jnp.where(first, 0.0, y2), axis=-1, keepdims=True)
    ms = jnp.where(first, s_lo, s_hi) * (1.0 / HEAD_DIM)
    yn = y * lax.rsqrt(ms + EPS) * nw
    half = ROT_DIM // 2
    rot = yn * cos + pltpu.roll(yn, half, 1) * sa + pltpu.roll(yn, LANES - half, 1) * sb
    return rot * scale


def _causal_conv_silu(load, cw_ref, cb_ref, store):
    for c0 in range(0, CONV_DIM, CONV_COLS):
        cs = slice(c0, c0 + CONV_COLS)
        xe = load(cs)
        x2 = pltpu.roll(xe, 2, 0)
        even = cw_ref[3:4, cs] * xe + cw_ref[1:2, cs] * x2
        odd = cw_ref[2:3, cs] * xe + cw_ref[0:1, cs] * x2
        store(cs, _silu((even + pltpu.roll(odd, 1, 0))[SUBLANES:, :] + cb_ref[:, cs]))


def _pre_norm(x_ref, nw_ref):
    x = x_ref[...]
    ms = jnp.mean(x * x, axis=-1, keepdims=True)
    return (x * lax.rsqrt(ms + EPS) * nw_ref[...]).astype(BF16)


def _inproj_kernel(x_ref, nw_ref, w_ref, *refs):
    hn = _pre_norm(x_ref, nw_ref)

    def proj(a, b):
        return _dot_nt(hn, w_ref[a:b, :])

    _emit_sections(proj, x_ref.shape[0], *refs)


def _emit_sections(proj, tm, qnw_ref, knw_ref, cos_ref, sa_ref, sb_ref,
                   zs_ref, xbc_ref, dt_ref, q_ref, kt_ref, vt_ref, gs_ref, ctail_ref):
    cos, sa, sb = cos_ref[...], sa_ref[...], sb_ref[...]
    q = proj(Q0, K0)
    for c in range(ATT_INNER // LANES):
        sl = slice(c * LANES, (c + 1) * LANES)
        qn = _norm_rope(q[:, sl], qnw_ref[...], cos, sa, sb, Q_SCALE)
        if len(q_ref.shape) == 3:
            q_ref[0, sl, :] = qn.T.astype(BF16)
        else:
            q_ref[:, sl] = qn.astype(BF16)
    k = proj(K0, V0)
    kn = [_norm_rope(k[:, c * LANES:(c + 1) * LANES], knw_ref[...], cos, sa, sb, 1.0)
          for c in range(KV_DIM // LANES)]
    kt_ref[0] = jnp.concatenate(kn, axis=1).T

    xbc = proj(X0, DT0)
    xbc_ref[...] = xbc
    ctail_ref[0] = xbc[tm - SUBLANES:tm, :]
    vt_ref[0] = proj(V0, G0).T
    zs_ref[...] = _silu(proj(Z0, X0))
    gs = _silu(proj(G0, W_ROWS))
    if len(gs_ref.shape) == 3:
        gs_ref[0] = gs.T
    else:
        gs_ref[...] = gs
    dt_ref[...] = proj(DT0, DT0 + LANES)


def _in_proj(x2d, nw, w_t, qnw, knw, tables, tm, seq):
    rows = x2d.shape[0]
    const = lambda i: (0, 0)
    row = lambda i: (i, 0)
    sec_in, out_specs, out_shape = _section_specs(rows, tm, seq, tables[0].shape[0] // tm, True)
    return pl.pallas_call(
        _inproj_kernel,
        grid=(rows // tm,),
        in_specs=[
            pl.BlockSpec((tm, D_MODEL), row),
            pl.BlockSpec((1, D_MODEL), const),
            pl.BlockSpec((W_ROWS, D_MODEL), const, pipeline_mode=pl.Buffered(1)),
        ] + sec_in,
        out_specs=out_specs,
        out_shape=out_shape,
        compiler_params=pltpu.CompilerParams(
            dimension_semantics=("arbitrary",), vmem_limit_bytes=VMEM_LIMIT),
        name="in_proj",
    )(x2d, nw, w_t, qnw, knw, *tables)


def _section_specs(rows, tm, seq, period, q_transposed):
    per_seq = seq // tm
    const = lambda i: (0, 0)
    row = lambda i: (i, 0)
    tab = lambda i: (i % period, 0)
    tr = lambda i: (i // per_seq, 0, i % per_seq)
    slab = lambda i: (i // per_seq, 0, 0)
    in_specs = [pl.BlockSpec((1, LANES), const)] * 2 + [pl.BlockSpec((tm, LANES), tab)] * 3
    row_outs = {0: (SSD_INNER, F32), 1: (CONV_DIM, F32), 2: (LANES, F32), 3: (ATT_INNER, BF16),
                6: (ATT_INNER, F32)}
    out_specs, out_shape = [], []
    for idx in range(7):
        if idx in (3, 6) and q_transposed:
            dt = row_outs[idx][1]
            out_specs.append(pl.BlockSpec((1, ATT_INNER, tm), tr))
            out_shape.append(jax.ShapeDtypeStruct((rows // seq, ATT_INNER, seq), dt))
        elif idx in row_outs:
            n, dt = row_outs[idx]
            out_specs.append(pl.BlockSpec((tm, n), row))
            out_shape.append(jax.ShapeDtypeStruct((rows, n), dt))
        else:
            out_specs.append(pl.BlockSpec((1, KV_DIM, tm), tr))
            out_shape.append(jax.ShapeDtypeStruct((rows // seq, KV_DIM, seq), F32))
    out_specs.append(pl.BlockSpec((1, SUBLANES, CONV_DIM), slab))
    out_shape.append(jax.ShapeDtypeStruct((rows // seq, SUBLANES, CONV_DIM), F32))
    return in_specs, out_specs, out_shape


def _wconv_in_kernel(x_ref, nw_ref, w_ref, wb_ref, p_ref, hn_scr):
    @pl.when(pl.program_id(0) == 0)
    def _():
        hn_scr[...] = _pre_norm(x_ref, nw_ref)

    row = pl.program_id(0) * W_CHUNK + lax.broadcasted_iota(jnp.int32, (W_CHUNK, 1), 0)
    wb = jnp.where(row < W_ROWS, w_ref[...], 0.0).astype(BF16)
    wb_ref[...] = wb
    p_ref[0] = _dot_nt(hn_scr[...], wb)


def _wconv_in(x2d, nw, w_f32):
    rows = x2d.shape[0]
    n_chunks = pl.cdiv(W_ROWS, W_CHUNK)
    const = lambda j: (0, 0)
    return pl.pallas_call(
        _wconv_in_kernel,
        grid=(n_chunks,),
        in_specs=[
            pl.BlockSpec((rows, D_MODEL), const),
            pl.BlockSpec((1, D_MODEL), const),
            pl.BlockSpec((W_CHUNK, D_MODEL), lambda j: (j, 0)),
        ],
        out_specs=[
            pl.BlockSpec((W_CHUNK, D_MODEL), lambda j: (j, 0)),
            pl.BlockSpec((1, rows, W_CHUNK), lambda j: (j, 0, 0)),
        ],
        out_shape=[
            jax.ShapeDtypeStruct((W_ROWS, D_MODEL), BF16),
            jax.ShapeDtypeStruct((n_chunks, rows, W_CHUNK), F32),
        ],
        scratch_shapes=[pltpu.VMEM((rows, D_MODEL), BF16)],
        compiler_params=pltpu.CompilerParams(
            dimension_semantics=("arbitrary",), vmem_limit_bytes=VMEM_LIMIT),
        name="wconv_in",
    )(x2d, nw, w_f32)


def _sections_kernel(p_ref, *refs):
    *refs, tail_scr = refs
    pfull = jnp.concatenate([p_ref[j] for j in range(p_ref.shape[0])], axis=1)
    tail_scr[...] = pfull[:, Q0:W_ROWS]

    def proj(a, b):
        if a >= Q0:
            return tail_scr[:, a - Q0:b - Q0]
        return pfull[:, a:b]

    _emit_sections(proj, p_ref.shape[1], *refs)


def _sections(p, qnw, knw, tables):
    rows = p.shape[1]
    sec_in, out_specs, out_shape = _section_specs(rows, rows, rows, 1, False)
    return pl.pallas_call(
        _sections_kernel,
        grid=(1,),
        in_specs=[pl.BlockSpec(p.shape, lambda i: (0, 0, 0))] + sec_in,
        out_specs=out_specs,
        out_shape=out_shape,
        scratch_shapes=[pltpu.VMEM((rows, W_ROWS - Q0), F32)],
        compiler_params=pltpu.CompilerParams(
            dimension_semantics=("arbitrary",), vmem_limit_bytes=VMEM_LIMIT),
        name="sections",
    )(p, qnw, knw, *tables)


def _ssd_chain(s, xbc_ref, dtr_ref, z_ref, cprev_ref, h0_ref, cw_ref, cb_ref, dtb_ref, alog_ref,
               dexp_ref, y_ref, hout_ref, xext, ht, xc_scr, lin, valid):
    L = SSD_CHUNK
    c = pl.program_id(1)
    last = pl.num_programs(1) - 1

    @pl.when(c == 0)
    def _():
        xext[s, 0:SUBLANES, :] = cprev_ref[s]
        for g in range(SSD_GROUPS):
            ht[s, g] = h0_ref[s, g * HALF_INNER:(g + 1) * HALF_INNER, :].T

    def pad_rows(v):
        if lin == L:
            return v
        return jnp.concatenate([v, jnp.zeros((L - lin, v.shape[1]), v.dtype)], axis=0)

    lane = lax.broadcasted_iota(jnp.int32, (1, LANES), 1)
    rowid = lax.broadcasted_iota(jnp.int32, (L, 1), 0)
    dt = jax.nn.softplus(pad_rows(dtr_ref[s]) + dtb_ref[...])
    dt = jnp.where((lane < SSD_HEADS) & (rowid < valid), dt, 0.0)
    dta = dt * (-jnp.exp(alog_ref[...]) * LOG2E)

    r2 = lax.broadcasted_iota(jnp.int32, (L, L), 0)
    c2 = lax.broadcasted_iota(jnp.int32, (L, L), 1)
    tri = r2 >= c2
    tri_b = jnp.where(tri, 1.0, 0.0).astype(BF16)
    cum = _dot_exact_lhs(tri_b, dta)
    cum_last = cum[L - 1:L, :]
    yield

    er = lax.broadcasted_iota(jnp.int32, (LANES, SSD_INNER), 0)
    ec = lax.broadcasted_iota(jnp.int32, (LANES, SSD_INNER), 1)
    expand = jnp.where(ec // SSD_HEADDIM == er, 1.0, 0.0).astype(BF16)
    ecum = _dot_wide_rhs(jnp.exp2(cum), expand)
    wexp = _dot_wide_rhs(jnp.exp2(cum_last - cum) * dt, expand)
    cd = _dot_exact_rhs(jnp.broadcast_to(jnp.exp2(cum_last), (SUBLANES, LANES)), expand)[0:1, :]
    cum_t = cum.T
    dt_t = dt.T
    yield

    if lin < L:
        xext[s, SUBLANES + lin:SUBLANES + L, :] = jnp.zeros((L - lin, CONV_DIM), F32)
    xext[s, SUBLANES:SUBLANES + lin, :] = xbc_ref[s]

    def load(cs):
        return xext[s, 0:SUBLANES + L, cs]

    def store(cs, v):
        xc_scr[s, :, cs] = v

    _causal_conv_silu(load, cw_ref, cb_ref, store)
    xext[s, 0:SUBLANES, :] = xext[s, L:L + SUBLANES, :]
    xc = xc_scr.at[s]
    xs = xc[:, :SSD_INNER]
    bm = xc[:, SSD_INNER:SSD_INNER + SSD_GROUPS * SSD_STATE]
    cm = xc[:, SSD_INNER + SSD_GROUPS * SSD_STATE:]
    bm_b = bm.astype(BF16)
    cm_b = cm.astype(BF16)
    xs_b = xs.astype(BF16)

    y_off = jnp.concatenate(
        [_dot(cm_b[:, g * SSD_STATE:(g + 1) * SSD_STATE], ht[s, g].astype(BF16))
         for g in range(SSD_GROUPS)], axis=1) * ecum
    cbs = [_dot_nt(cm_b[:, g * SSD_STATE:(g + 1) * SSD_STATE], bm_b[:, g * SSD_STATE:(g + 1) * SSD_STATE])
           for g in range(SSD_GROUPS)]
    yield

    lane_half = lane // HEAD_DIM
    heads_per_group = SSD_HEADS // SSD_GROUPS
    y_parts = []
    for g in range(SSD_GROUPS):
        for pr in range(heads_per_group // 2):
            col0 = g * HALF_INNER + pr * LANES
            xp = xs_b[:, col0:col0 + LANES]
            yp = jnp.zeros((L, LANES), F32)
            for e in range(2):
                hh = g * heads_per_group + pr * 2 + e
                decay = jnp.exp2(cum[:, hh:hh + 1] - cum_t[hh:hh + 1, :])
                sc = jnp.where(tri, cbs[g] * decay, 0.0) * dt_t[hh:hh + 1, :]
                xm = jnp.where(lane_half == e, xp, jnp.zeros_like(xp))
                yp = yp + _dot(sc.astype(BF16), xm)
            y_parts.append(yp)
            if pr % 2 == 1:
                yield
    y_diag = jnp.concatenate(y_parts, axis=1)

    xw = (xs * wexp).astype(BF16)
    for g in range(SSD_GROUPS):
        sl = slice(g * HALF_INNER, (g + 1) * HALF_INNER)
        bm_t = bm[:, g * SSD_STATE:(g + 1) * SSD_STATE].T.astype(BF16)
        ht[s, g] = ht[s, g] * cd[:, sl] + _dot(bm_t, xw[:, sl])
    yield

    y = (y_diag + y_off + dexp_ref[...] * xs) * pad_rows(z_ref[s])
    y_ref[s] = y[:lin]

    @pl.when(c == last)
    def _():
        for g in range(SSD_GROUPS):
            hout_ref[s, g * HALF_INNER:(g + 1) * HALF_INNER, :] = ht[s, g].T


def _ssd_kernel(*refs, lin, valid):
    chains = [_ssd_chain(s, *refs, lin, valid) for s in range(SSD_SEQS)]
    while chains:
        alive = []
        for ch in chains:
            try:
                next(ch)
                alive.append(ch)
            except StopIteration:
                pass
        chains = alive


def _ssd(xbc, dtr, z, cprev, h0, cw, cb, dtb, alog, dexp, batch, lin, valid):
    rows = xbc.shape[0]
    seq = rows // batch
    nc = seq // lin
    g = SSD_SEQS
    assert batch % g == 0
    blk = lambda b, c: (b, c, 0)
    per_b = lambda b, c: (b, 0, 0)
    const = lambda b, c: (0, 0)
    y, hout = pl.pallas_call(
        functools.partial(_ssd_kernel, lin=lin, valid=valid),
        grid=(batch // g, nc),
        in_specs=[
            pl.BlockSpec((g, lin, CONV_DIM), blk),
            pl.BlockSpec((g, lin, LANES), blk),
            pl.BlockSpec((g, lin, SSD_INNER), blk),
            pl.BlockSpec((g, SUBLANES, CONV_DIM), per_b),
            pl.BlockSpec((g, SSD_INNER, SSD_STATE), per_b),
            pl.BlockSpec((SUBLANES, CONV_DIM), const),
            pl.BlockSpec((1, CONV_DIM), const),
            pl.BlockSpec((1, LANES), const),
            pl.BlockSpec((1, LANES), const),
            pl.BlockSpec((1, SSD_INNER), const),
        ],
        out_specs=[
            pl.BlockSpec((g, lin, SSD_INNER), blk),
            pl.BlockSpec((g, SSD_INNER, SSD_STATE), per_b),
        ],
        out_shape=[
            jax.ShapeDtypeStruct((batch, seq, SSD_INNER), F32),
            jax.ShapeDtypeStruct((batch, SSD_INNER, SSD_STATE), F32),
        ],
        scratch_shapes=[
            pltpu.VMEM((g, SUBLANES + SSD_CHUNK + SUBLANES, CONV_DIM), F32),
            pltpu.VMEM((g, SSD_GROUPS, SSD_STATE, HALF_INNER), F32),
            pltpu.VMEM((g, SSD_CHUNK, CONV_DIM), F32),
        ],
        compiler_params=pltpu.CompilerParams(
            dimension_semantics=("arbitrary", "arbitrary"), vmem_limit_bytes=VMEM_LIMIT),
        name="ssd",
    )(xbc.reshape(batch, seq, CONV_DIM), dtr.reshape(batch, seq, LANES),
      z.reshape(batch, seq, SSD_INNER), cprev, h0, cw, cb, dtb, alog, dexp)
    return y.reshape(rows, SSD_INNER), hout


def _attn_prompt_kernel(q_ref, kt_ref, vt_ref, g_ref, o_ref,
                        bias_scr, kh_scr, va_scr, qt_scr, m_scr, acc_scr, s_scr, s2_scr, *, n_tab, n_sb_max):
    b = pl.program_id(0)
    i = pl.program_id(1)
    seq = kt_ref.shape[2]
    pair_w = 2 * Q_BLOCK
    pairs_per_kv = ATT_GQ // 2

    @pl.when((b == 0) & (i == 0))
    def _():
        r = lax.broadcasted_iota(jnp.int32, (K_SUPER, Q_BLOCK), 0)
        c = lax.broadcasted_iota(jnp.int32, (K_SUPER, Q_BLOCK), 1)
        for tb in range(n_tab):
            w = _multiplicity(tb * Q_BLOCK + c - r)
            bias_scr[tb] = jnp.where(w > 0.0, jnp.log2(jnp.maximum(w, 1.0)), NEG)
        va_scr[:, HEAD_DIM:, :] = jnp.ones((ATT_KV_HEADS, ONES_ROWS, seq), BF16)

    @pl.when(i == 0)
    def _():
        for kvh in range(ATT_KV_HEADS):
            hs = slice(kvh * HEAD_DIM, (kvh + 1) * HEAD_DIM)
            va_scr[kvh, 0:HEAD_DIM, :] = vt_ref[0, hs, :].astype(BF16)
        for cidx in range(seq // K_SUPER):
            rows = slice(cidx * K_SUPER, (cidx + 1) * K_SUPER)
            kc = kt_ref[0, :, rows].T
            for kvh in range(ATT_KV_HEADS):
                hs = slice(kvh * HEAD_DIM, (kvh + 1) * HEAD_DIM)
                kh_scr[kvh, rows, :] = kc[:, hs].astype(BF16)

    for kvh in range(ATT_KV_HEADS):
        qt_scr[kvh] = jnp.concatenate(
            [q_ref[0, (kvh * ATT_GQ + g) * HEAD_DIM:(kvh * ATT_GQ + g + 1) * HEAD_DIM, :]
             for g in range(ATT_GQ)], axis=1)
    m_scr[...] = jnp.full(m_scr.shape, NEG, F32)
    acc_scr[...] = jnp.zeros(acc_scr.shape, F32)
    par = i % Q_PER_SUPER
    j_last = i // Q_PER_SUPER
    n_sb = jnp.minimum(j_last + 1, n_sb_max)

    def key_start(dl):
        return pl.multiple_of((j_last - dl) * K_SUPER, K_SUPER)

    def scores(dl, dst):
        for kvh in range(ATT_KV_HEADS):
            dst[kvh] = _dot(kh_scr[kvh, pl.ds(key_start(dl), K_SUPER), :], qt_scr[kvh])

    def softmax_pv(dl, src):
        start = key_start(dl)
        bias = bias_scr[par + Q_PER_SUPER * dl]
        bias2 = jnp.concatenate([bias, bias], axis=1)
        for kvh in range(ATT_KV_HEADS):
            vtb = va_scr[kvh, :, pl.ds(start, K_SUPER)]
            for pr in range(pairs_per_kv):
                u = kvh * pairs_per_kv + pr
                s = src[kvh, :, pr * pair_w:(pr + 1) * pair_w] + bias2
                m_old = m_scr[u, 0:1, :]
                m_new = jnp.maximum(m_old, jnp.max(s, axis=0, keepdims=True))
                p = jnp.exp2(s - m_new).astype(BF16)
                acc_scr[u] = jnp.exp2(m_old - m_new) * acc_scr[u] + _dot(vtb, p)
                m_scr[u, 0:1, :] = m_new

    def body(t, carry):
        scores(2 * t, s_scr)
        scores(2 * t + 1, s2_scr)
        softmax_pv(2 * t, s_scr)
        softmax_pv(2 * t + 1, s2_scr)
        return carry

    lax.fori_loop(0, n_sb // 2, body, 0)

    @pl.when(n_sb % 2 == 1)
    def _():
        scores(n_sb - 1, s_scr)
        softmax_pv(n_sb - 1, s_scr)

    for u in range(ATT_HEADS // 2):
        acc = acc_scr[u]
        o = acc[0:HEAD_DIM] / acc[HEAD_DIM:HEAD_DIM + 1]
        for e in range(2):
            hs = slice((2 * u + e) * HEAD_DIM, (2 * u + e + 1) * HEAD_DIM)
            o_ref[0, hs, :] = (o[:, e * Q_BLOCK:(e + 1) * Q_BLOCK] * g_ref[0, hs, :]).astype(BF16)


def _attn_prompt(qt, kt, vt, gate_t, batch, seq):
    nqb = seq // Q_BLOCK
    assert seq % K_SUPER == 0
    n_sb_max = min(seq // K_SUPER, W_MAX // K_SUPER + 1)
    n_tab = Q_PER_SUPER * n_sb_max
    qblk = lambda b, i: (b, 0, i)
    per_b = lambda b, i: (b, 0, 0)
    return pl.pallas_call(
        functools.partial(_attn_prompt_kernel, n_tab=n_tab, n_sb_max=n_sb_max),
        grid=(batch, nqb),
        in_specs=[
            pl.BlockSpec((1, ATT_INNER, Q_BLOCK), qblk),
            pl.BlockSpec((1, KV_DIM, seq), per_b),
            pl.BlockSpec((1, KV_DIM, seq), per_b),
            pl.BlockSpec((1, ATT_INNER, Q_BLOCK), qblk),
        ],
        out_specs=pl.BlockSpec((1, ATT_INNER, Q_BLOCK), qblk),
        out_shape=jax.ShapeDtypeStruct((batch, ATT_INNER, seq), BF16),
        scratch_shapes=[
            pltpu.VMEM((n_tab, K_SUPER, Q_BLOCK), F32),
            pltpu.VMEM((ATT_KV_HEADS, seq, HEAD_DIM), BF16),
            pltpu.VMEM((ATT_KV_HEADS, ACC_ROWS, seq), BF16),
            pltpu.VMEM((ATT_KV_HEADS, HEAD_DIM, ATT_GQ * Q_BLOCK), BF16),
            pltpu.VMEM((ATT_HEADS // 2, SUBLANES, 2 * Q_BLOCK), F32),
            pltpu.VMEM((ATT_HEADS // 2, ACC_ROWS, 2 * Q_BLOCK), F32),
            pltpu.VMEM((ATT_KV_HEADS, K_SUPER, ATT_GQ * Q_BLOCK), F32),
            pltpu.VMEM((ATT_KV_HEADS, K_SUPER, ATT_GQ * Q_BLOCK), F32),
        ],
        compiler_params=pltpu.CompilerParams(
            dimension_semantics=("arbitrary", "arbitrary"), vmem_limit_bytes=VMEM_LIMIT),
        name="attn_prompt",
    )(qt, kt, vt, gate_t)


def _attn_sample_kernel(q_ref, knt_ref, vnt_ref, g_ref, ckt_ref, cvt_ref, o_ref, kot_ref, vot_ref, *, ts):
    win = ckt_ref.shape[2]
    tp = SAMPLE_PAD
    rows = ATT_GQ * tp
    t_c = lax.broadcasted_iota(jnp.int32, (rows, win), 0) % tp
    j_c = lax.broadcasted_iota(jnp.int32, (rows, win), 1)
    w_c = _multiplicity(win + t_c - j_c)
    t_n = lax.broadcasted_iota(jnp.int32, (rows, tp), 0) % tp
    j_n = lax.broadcasted_iota(jnp.int32, (rows, tp), 1)
    w_n = jnp.where(j_n < ts, _multiplicity(t_n - j_n), 0.0)
    for kvh in range(ATT_KV_HEADS):
        c0 = kvh * ATT_GQ * HEAD_DIM
        qh = jnp.concatenate(
            [q_ref[:, c0 + g * HEAD_DIM:c0 + (g + 1) * HEAD_DIM] for g in range(ATT_GQ)], axis=0)
        hs = slice(kvh * HEAD_DIM, (kvh + 1) * HEAD_DIM)
        s_c = jnp.where(w_c > 0.0, _dot(qh, ckt_ref[0, hs, :].astype(BF16)), NEG)
        s_n = jnp.where(w_n > 0.0, _dot(qh, knt_ref[0, hs, :].astype(BF16)), NEG)
        m = jnp.maximum(jnp.max(s_c, axis=-1, keepdims=True), jnp.max(s_n, axis=-1, keepdims=True))
        p_c = jnp.exp2(s_c - m) * w_c
        p_n = jnp.exp2(s_n - m) * w_n
        den = jnp.sum(p_c, axis=-1, keepdims=True) + jnp.sum(p_n, axis=-1, keepdims=True)
        num = (_dot_nt(p_c.astype(BF16), cvt_ref[0, hs, :].astype(BF16))
               + _dot_nt(p_n.astype(BF16), vnt_ref[0, hs, :].astype(BF16)))
        o = num / den
        o = jnp.concatenate([o[g * tp:(g + 1) * tp] for g in range(ATT_GQ)], axis=1)
        csl = slice(c0, c0 + ATT_GQ * HEAD_DIM)
        o_ref[:, csl] = (o * g_ref[:, csl]).astype(BF16)
    kot_ref[0, :, 0:win - ts] = ckt_ref[0, :, ts:win]
    kot_ref[0, :, win - ts:win] = knt_ref[0, :, 0:ts]
    vot_ref[0, :, 0:win - ts] = cvt_ref[0, :, ts:win]
    vot_ref[0, :, win - ts:win] = vnt_ref[0, :, 0:ts]


def _attn_sample(q, knt, vnt, gate, ckt, cvt, ts):
    batch, _, win = ckt.shape
    tp = SAMPLE_PAD
    blk = lambda b: (b, 0)
    per_b = lambda b: (b, 0, 0)
    return pl.pallas_call(
        functools.partial(_attn_sample_kernel, ts=ts),
        grid=(batch,),
        in_specs=[
            pl.BlockSpec((tp, ATT_INNER), blk),
            pl.BlockSpec((1, KV_DIM, tp), per_b),
            pl.BlockSpec((1, KV_DIM, tp), per_b),
            pl.BlockSpec((tp, ATT_INNER), blk),
            pl.BlockSpec((1, KV_DIM, win), per_b),
            pl.BlockSpec((1, KV_DIM, win), per_b),
        ],
        out_specs=[
            pl.BlockSpec((tp, ATT_INNER), blk),
            pl.BlockSpec((1, KV_DIM, win), per_b),
            pl.BlockSpec((1, KV_DIM, win), per_b),
        ],
        out_shape=[
            jax.ShapeDtypeStruct((batch * tp, ATT_INNER), BF16),
            jax.ShapeDtypeStruct((batch, KV_DIM, win), F32),
            jax.ShapeDtypeStruct((batch, KV_DIM, win), F32),
        ],
        compiler_params=pltpu.CompilerParams(
            dimension_semantics=("arbitrary",), vmem_limit_bytes=VMEM_LIMIT),
        name="attn_sample",
    )(q, knt, vnt, gate, ckt, cvt)


def _group_norm(yg, nw):
    ms = jnp.mean(yg * yg, axis=-1, keepdims=True)
    return (yg * lax.rsqrt(ms + EPS) * nw).astype(BF16)


def _outproj_kernel(x_ref, ys_ref, yat_ref, nw_ref, w_ref, o_ref):
    acc = x_ref[...]
    for g in range(SSD_GROUPS):
        gs = slice(g * HALF_INNER, (g + 1) * HALF_INNER)
        acc = acc + _dot(_group_norm(ys_ref[:, gs], nw_ref[:, gs]), w_ref[gs, :])
    ya = yat_ref[0].astype(F32).T.astype(BF16)
    o_ref[...] = acc + _dot(ya, w_ref[SSD_INNER:, :])


def _out_proj(x2d, y_ssd, y_att_t, nw, w_o, tm):
    rows = x2d.shape[0]
    per_seq = y_att_t.shape[2] // tm
    row = lambda i: (i, 0)
    const = lambda i: (0, 0)
    return pl.pallas_call(
        _outproj_kernel,
        grid=(rows // tm,),
        in_specs=[
            pl.BlockSpec((tm, D_MODEL), row),
            pl.BlockSpec((tm, SSD_INNER), row),
            pl.BlockSpec((1, ATT_INNER, tm), lambda i: (i // per_seq, 0, i % per_seq)),
            pl.BlockSpec((1, SSD_INNER), const),
            pl.BlockSpec((SSD_INNER + ATT_INNER, D_MODEL), const, pipeline_mode=pl.Buffered(1)),
        ],
        out_specs=pl.BlockSpec((tm, D_MODEL), row),
        out_shape=jax.ShapeDtypeStruct((rows, D_MODEL), F32),
        compiler_params=pltpu.CompilerParams(
            dimension_semantics=("arbitrary",), vmem_limit_bytes=VMEM_LIMIT),
        name="out_proj",
    )(x2d, y_ssd, y_att_t, nw, w_o)


def _wconv_out_kernel(x_ref, ys_ref, ya_ref, nw_ref, w_ref, wb_ref, o_ref):
    j = pl.program_id(0)
    wb = w_ref[...].astype(BF16)
    wb_ref[...] = wb
    y = jnp.where(j < SSD_GROUPS, _group_norm(ys_ref[...], nw_ref[...]), ya_ref[...])
    part = _dot(y, wb)

    @pl.when(j == 0)
    def _():
        o_ref[...] = x_ref[...] + part

    @pl.when(j > 0)
    def _():
        o_ref[...] += part


def _wconv_out(x2d, y_ssd, y_att, nw, w_f32):
    rows = x2d.shape[0]
    n_ssd = SSD_GROUPS
    n_chunks = (SSD_INNER + ATT_INNER) // WO_CHUNK
    const = lambda j: (0, 0)
    ssd_chunk = lambda j: (0, jnp.minimum(j, n_ssd - 1))
    return pl.pallas_call(
        _wconv_out_kernel,
        grid=(n_chunks,),
        in_specs=[
            pl.BlockSpec((rows, D_MODEL), const),
            pl.BlockSpec((rows, WO_CHUNK), ssd_chunk),
            pl.BlockSpec((rows, WO_CHUNK), lambda j: (0, jnp.maximum(j - n_ssd, 0))),
            pl.BlockSpec((1, WO_CHUNK), ssd_chunk),
            pl.BlockSpec((WO_CHUNK, D_MODEL), lambda j: (j, 0)),
        ],
        out_specs=[
            pl.BlockSpec((WO_CHUNK, D_MODEL), lambda j: (j, 0)),
            pl.BlockSpec((rows, D_MODEL), const),
        ],
        out_shape=[
            jax.ShapeDtypeStruct((SSD_INNER + ATT_INNER, D_MODEL), BF16),
            jax.ShapeDtypeStruct((rows, D_MODEL), F32),
        ],
        compiler_params=pltpu.CompilerParams(
            dimension_semantics=("arbitrary",), vmem_limit_bytes=VMEM_LIMIT),
        name="wconv_out",
    )(x2d, y_ssd, y_att, nw, w_f32)


def _rope_tables(pos):
    n = pos.shape[0]
    half = ROT_DIM // 2
    inv = ROPE_THETA ** (-np.arange(0, ROT_DIM, 2, dtype=np.float64) / ROT_DIM)
    ang = pos.astype(np.float64)[:, None] * inv[None, :]
    cos, sin = np.cos(ang), np.sin(ang)
    rest = HEAD_DIM - ROT_DIM
    zh = np.zeros((n, half))
    cos_h = np.concatenate([cos, cos, np.ones((n, rest))], axis=1)
    sa_h = np.concatenate([zh, sin, np.zeros((n, rest))], axis=1)
    sb_h = np.concatenate([-sin, zh, np.zeros((n, rest))], axis=1)
    rep = LANES // HEAD_DIM
    return tuple(jnp.asarray(np.tile(t, (1, rep)), dtype=F32) for t in (cos_h, sa_h, sb_h))


def _lane_pad(v, n=LANES):
    return jnp.pad(v, (0, n - v.shape[0])).reshape(1, n)


def _to_cache(xt, batch, seq):
    return xt.reshape(1, batch, ATT_KV_HEADS, HEAD_DIM, seq).transpose(0, 1, 4, 2, 3)


def kernel(x_prompt, x_sample, cache_k, cache_v, state_conv, state_ssm, norm_w, w_in, conv_w,
           conv_b, dt_bias, a_log, d_skip, ssd_norm_w, q_norm_w, k_norm_w, w_out):
    bp, tp_, _ = x_prompt.shape
    bs, ts, _ = x_sample.shape
    depth = w_in.shape[0]
    assert depth == 1 and tp_ % SSD_CHUNK == 0 and ts <= SAMPLE_PAD and ts >= CONV_W - 1
    l = 0
    win = cache_k.shape[2]

    nw = norm_w[l].reshape(1, D_MODEL)
    rep = LANES // HEAD_DIM
    qnw = jnp.tile(q_norm_w[l], rep).reshape(1, LANES)
    knw = jnp.tile(k_norm_w[l], rep).reshape(1, LANES)
    cw = jnp.pad(conv_w[l], ((0, SUBLANES - CONV_W), (0, 0)))
    cb = conv_b[l].reshape(1, CONV_DIM)
    dtb = _lane_pad(dt_bias[l])
    alog = _lane_pad(a_log[l])
    dexp = jnp.repeat(d_skip[l], SSD_HEADDIM).reshape(1, SSD_INNER)
    snw = ssd_norm_w[l].reshape(1, SSD_INNER)

    pad = SAMPLE_PAD
    xs2 = jnp.pad(x_sample, ((0, 0), (0, pad - ts), (0, 0))).reshape(bs * pad, D_MODEL)
    tabs_s = _rope_tables(PAST_LEN + np.arange(bs * pad) % pad)
    w_t, proj_s = _wconv_in(xs2, nw, jnp.swapaxes(w_in[l], 0, 1))
    zs, xbc, dtr, q, kt, vt, gs, _ = _sections(proj_s, qnw, knw, tabs_s)
    cprev = jnp.pad(state_conv[l], ((0, 0), (SUBLANES - (CONV_W - 1), 0), (0, 0)))
    y_ssd, h_s = _ssd(xbc, dtr, zs, cprev, state_ssm[l].reshape(bs, SSD_INNER, SSD_STATE),
                      cw, cb, dtb, alog, dexp, bs, pad, ts)
    knt = kt.reshape(KV_DIM, bs, pad).transpose(1, 0, 2)
    vnt = vt.reshape(KV_DIM, bs, pad).transpose(1, 0, 2)
    ckt = cache_k[l].transpose(0, 2, 3, 1).reshape(bs, KV_DIM, win)
    cvt = cache_v[l].transpose(0, 2, 3, 1).reshape(bs, KV_DIM, win)
    y_att, kot, vot = _attn_sample(q, knt, vnt, gs, ckt, cvt, ts)
    w_o, y_s = _wconv_out(xs2, y_ssd, y_att, snw, w_out[l])
    y_s = y_s.reshape(bs, pad, D_MODEL)[:, :ts]
    k_s = _to_cache(kot, bs, win)
    v_s = _to_cache(vot, bs, win)
    c_s = xbc.reshape(bs, pad, CONV_DIM)[:, ts - (CONV_W - 1):ts][None]
    h_s = h_s.reshape(1, bs, SSD_HEADS, SSD_HEADDIM, SSD_STATE)

    tm = TM_IN
    xp2 = x_prompt.reshape(bp * tp_, D_MODEL)
    tabs = _rope_tables(np.arange(tp_))
    zs, xbc, dtr, q, kt, vt, gs, ctail = _in_proj(xp2, nw, w_t, qnw, knw, tabs, tm, tp_)
    y_ssd, h_p = _ssd(xbc, dtr, zs, jnp.zeros((bp, SUBLANES, CONV_DIM), F32),
                      jnp.zeros((bp, SSD_INNER, SSD_STATE), F32),
                      cw, cb, dtb, alog, dexp, bp, SSD_CHUNK, SSD_CHUNK)
    y_att = _attn_prompt(q, kt, vt, gs, bp, tp_)
    y_p = _out_proj(xp2, y_ssd, y_att, snw, w_o, TM_OUT).reshape(bp, tp_, D_MODEL)
    keep = min(W_MAX, tp_)
    k_p = _to_cache(kt, bp, tp_)[:, :, tp_ - keep:]
    v_p = _to_cache(vt, bp, tp_)[:, :, tp_ - keep:]
    c_p = ctail[:, SUBLANES - (CONV_W - 1):][None]
    h_p = h_p.reshape(1, bp, SSD_HEADS, SSD_HEADDIM, SSD_STATE)

    return (y_p, y_s, k_p, v_p, c_p, h_p, k_s, v_s, c_s, h_s)
```

```python
import functools

import jax
import jax.numpy as jnp
import numpy as np
from jax import lax
from jax.experimental import pallas as pl
from jax.experimental.pallas import tpu as pltpu

F32 = jnp.float32
BF16 = jnp.bfloat16

D_MODEL = 2048
SSD_HEADS = 16
SSD_HEADDIM = 64
SSD_INNER = SSD_HEADS * SSD_HEADDIM
SSD_GROUPS = 2
SSD_STATE = 128
CONV_W = 4
CONV_DIM = SSD_INNER + 2 * SSD_GROUPS * SSD_STATE
SSD_CHUNK = 128
ATT_HEADS = 16
ATT_KV_HEADS = 4
HEAD_DIM = 64
ATT_GQ = ATT_HEADS // ATT_KV_HEADS
ATT_INNER = ATT_HEADS * HEAD_DIM
KV_DIM = ATT_KV_HEADS * HEAD_DIM
ROT_DIM = HEAD_DIM // 4
ROPE_THETA = 500000.0
DILATED_BRANCHES = ((128, 1), (512, 4), (2048, 16))
W_MAX = 2048
PAST_LEN = 16384
EPS = 1e-6

LANES = 128
SUBLANES = 8
Q_BLOCK = 256
K_SUPER = 256
Q_PER_SUPER = K_SUPER // Q_BLOCK
ONES_ROWS = 16
ACC_ROWS = HEAD_DIM + ONES_ROWS
CONV_COLS = 256
TM_IN = 512
TM_OUT = 512
SSD_SEQS = 4
SAMPLE_PAD = 16
NEG = -1e30
VMEM_LIMIT = 58 * 1024 * 1024

Z0 = 0
X0 = Z0 + SSD_INNER
DT0 = X0 + CONV_DIM
Q0 = DT0 + SSD_HEADS
K0 = Q0 + ATT_INNER
V0 = K0 + KV_DIM
G0 = V0 + KV_DIM
W_ROWS = G0 + ATT_INNER
W_CHUNK = 1024
HALF_INNER = SSD_INNER // SSD_GROUPS
WO_CHUNK = HALF_INNER
LOG2E = 1.4426950408889634
Q_SCALE = HEAD_DIM ** -0.5 * LOG2E


def _dot(a, b):
    return jnp.dot(a, b, preferred_element_type=F32)


def _dot_nt(a, b):
    return lax.dot_general(a, b, (((1,), (1,)), ((), ())), preferred_element_type=F32)


def _split3(x):
    hi = x.astype(BF16)
    r1 = x - hi.astype(F32)
    mid = r1.astype(BF16)
    lo = (r1 - mid.astype(F32)).astype(BF16)
    return hi, mid, lo


def _dot_exact_rhs(x, m):
    hi, mid, lo = _split3(x)
    return _dot(hi, m) + _dot(mid, m) + _dot(lo, m)


def _dot_wide_rhs(x, m):
    hi = x.astype(BF16)
    lo = (x - hi.astype(F32)).astype(BF16)
    return _dot(hi, m) + _dot(lo, m)


def _dot_exact_lhs(m, x):
    hi, mid, lo = _split3(x)
    return _dot(m, hi) + _dot(m, mid) + _dot(m, lo)


def _silu(x):
    hx = 0.5 * x
    return hx * jnp.tanh(hx) + hx


def _multiplicity(d):
    w = jnp.zeros(d.shape, F32)
    for window, dil in DILATED_BRANCHES:
        hit = (d >= 0) & (d <= window) & (lax.rem(d, dil) == 0)
        w = w + jnp.where(hit, 1.0, 0.0)
    return w


def _norm_rope(y, nw, cos, sa, sb, scale):
    lane = lax.broadcasted_iota(jnp.int32, (1, LANES), 1)
    first = lane < HEAD_DIM
    y2 = y * y
    s_lo = jnp.sum(jnp.where(first, y2, 0.0), axis=-1, keepdims=True)
    s_hi = jnp.sum(jnp.where(first, 0.0, y2), axis=-1, keepdims=True)
    ms = jnp.where(first, s_lo, s_hi) * (1.0 / HEAD_DIM)
    yn = y * lax.rsqrt(ms + EPS) * nw
    half = ROT_DIM // 2
    rot = yn * cos + pltpu.roll(yn, half, 1) * sa + pltpu.roll(yn, LANES - half, 1) * sb
    return rot * scale


def _causal_conv_silu(load, cw_ref, cb_ref, store):
    for c0 in range(0, CONV_DIM, CONV_COLS):
        cs = slice(c0, c0 + CONV_COLS)
        xe = load(cs)
        x2 = pltpu.roll(xe, 2, 0)
        even = cw_ref[3:4, cs] * xe + cw_ref[1:2, cs] * x2
        odd = cw_ref[2:3, cs] * xe + cw_ref[0:1, cs] * x2
        store(cs, _silu((even + pltpu.roll(odd, 1, 0))[SUBLANES:, :] + cb_ref[:, cs]))


def _pre_norm(x_ref, nw_ref):
    x = x_ref[...]
    ms = jnp.mean(x * x, axis=-1, keepdims=True)
    return (x * lax.rsqrt(ms + EPS) * nw_ref[...]).astype(BF16)


def _inproj_kernel(x_ref, nw_ref, w_ref, *refs):
    hn = _pre_norm(x_ref, nw_ref)

    def proj(a, b):
        return _dot_nt(hn, w_ref[a:b, :])

    _emit_sections(proj, x_ref.shape[0], *refs)


def _emit_sections(proj, tm, qnw_ref, knw_ref, cos_ref, sa_ref, sb_ref,
                   zs_ref, xbc_ref, dt_ref, q_ref, kt_ref, vt_ref, gs_ref, ctail_ref):
    cos, sa, sb = cos_ref[...], sa_ref[...], sb_ref[...]
    q = proj(Q0, K0)
    for c in range(ATT_INNER // LANES):
        sl = slice(c * LANES, (c + 1) * LANES)
        qn = _norm_rope(q[:, sl], qnw_ref[...], cos, sa, sb, Q_SCALE)
        if len(q_ref.shape) == 3:
            q_ref[0, sl, :] = qn.T.astype(BF16)
        else:
            q_ref[:, sl] = qn.astype(BF16)
    k = proj(K0, V0)
    kn = [_norm_rope(k[:, c * LANES:(c + 1) * LANES], knw_ref[...], cos, sa, sb, 1.0)
          for c in range(KV_DIM // LANES)]
    kt_ref[0] = jnp.concatenate(kn, axis=1).T

    xbc = proj(X0, DT0)
    xbc_ref[...] = xbc
    ctail_ref[0] = xbc[tm - SUBLANES:tm, :]
    vt_ref[0] = proj(V0, G0).T
    zs_ref[...] = _silu(proj(Z0, X0))
    gs = _silu(proj(G0, W_ROWS))
    if len(gs_ref.shape) == 3:
        gs_ref[0] = gs.T
    else:
        gs_ref[...] = gs
    dt_ref[...] = proj(DT0, DT0 + LANES)


def _in_proj(x2d, nw, w_t, qnw, knw, tables, tm, seq):
    rows = x2d.shape[0]
    const = lambda i: (0, 0)
    row = lambda i: (i, 0)
    sec_in, out_specs, out_shape = _section_specs(rows, tm, seq, tables[0].shape[0] // tm, True)
    return pl.pallas_call(
        _inproj_kernel,
        grid=(rows // tm,),
        in_specs=[
            pl.BlockSpec((tm, D_MODEL), row),
            pl.BlockSpec((1, D_MODEL), const),
            pl.BlockSpec((W_ROWS, D_MODEL), const, pipeline_mode=pl.Buffered(1)),
        ] + sec_in,
        out_specs=out_specs,
        out_shape=out_shape,
        compiler_params=pltpu.CompilerParams(
            dimension_semantics=("arbitrary",), vmem_limit_bytes=VMEM_LIMIT),
        name="in_proj",
    )(x2d, nw, w_t, qnw, knw, *tables)


def _section_specs(rows, tm, seq, period, q_transposed):
    per_seq = seq // tm
    const = lambda i: (0, 0)
    row = lambda i: (i, 0)
    tab = lambda i: (i % period, 0)
    tr = lambda i: (i // per_seq, 0, i % per_seq)
    slab = lambda i: (i // per_seq, 0, 0)
    in_specs = [pl.BlockSpec((1, LANES), const)] * 2 + [pl.BlockSpec((tm, LANES), tab)] * 3
    row_outs = {0: (SSD_INNER, F32), 1: (CONV_DIM, F32), 2: (LANES, F32), 3: (ATT_INNER, BF16),
                6: (ATT_INNER, F32)}
    out_specs, out_shape = [], []
    for idx in range(7):
        if idx in (3, 6) and q_transposed:
            dt = row_outs[idx][1]
            out_specs.append(pl.BlockSpec((1, ATT_INNER, tm), tr))
            out_shape.append(jax.ShapeDtypeStruct((rows // seq, ATT_INNER, seq), dt))
        elif idx in row_outs:
            n, dt = row_outs[idx]
            out_specs.append(pl.BlockSpec((tm, n), row))
            out_shape.append(jax.ShapeDtypeStruct((rows, n), dt))
        else:
            out_specs.append(pl.BlockSpec((1, KV_DIM, tm), tr))
            out_shape.append(jax.ShapeDtypeStruct((rows // seq, KV_DIM, seq), F32))
    out_specs.append(pl.BlockSpec((1, SUBLANES, CONV_DIM), slab))
    out_shape.append(jax.ShapeDtypeStruct((rows // seq, SUBLANES, CONV_DIM), F32))
    return in_specs, out_specs, out_shape


def _wconv_in_kernel(x_ref, nw_ref, w_ref, wb_ref, p_ref, xpad_scr, hn_scr):
    @pl.when(pl.program_id(0) == 0)
    def _():
        bs, ts, _ = x_ref.shape
        xpad_scr[...] = jnp.zeros(xpad_scr.shape, F32)
        for b in range(bs):
            xpad_scr[b * SAMPLE_PAD:b * SAMPLE_PAD + ts, :] = x_ref[b]
        hn_scr[...] = _pre_norm(xpad_scr, nw_ref)

    row = pl.program_id(0) * W_CHUNK + lax.broadcasted_iota(jnp.int32, (W_CHUNK, 1), 0)
    wb = jnp.where(row < W_ROWS, w_ref[...], 0.0).astype(BF16)
    wb_ref[...] = wb
    p_ref[0] = _dot_nt(hn_scr[...], wb)


def _wconv_in(x_s, nw, w_f32):
    rows = x_s.shape[0] * SAMPLE_PAD
    n_chunks = pl.cdiv(W_ROWS, W_CHUNK)
    const = lambda j: (0, 0)
    return pl.pallas_call(
        _wconv_in_kernel,
        grid=(n_chunks,),
        in_specs=[
            pl.BlockSpec(x_s.shape, lambda j: (0, 0, 0)),
            pl.BlockSpec((1, D_MODEL), const),
            pl.BlockSpec((W_CHUNK, D_MODEL), lambda j: (j, 0)),
        ],
        out_specs=[
            pl.BlockSpec((W_CHUNK, D_MODEL), lambda j: (j, 0)),
            pl.BlockSpec((1, rows, W_CHUNK), lambda j: (j, 0, 0)),
        ],
        out_shape=[
            jax.ShapeDtypeStruct((W_ROWS, D_MODEL), BF16),
            jax.ShapeDtypeStruct((n_chunks, rows, W_CHUNK), F32),
        ],
        scratch_shapes=[pltpu.VMEM((rows, D_MODEL), F32), pltpu.VMEM((rows, D_MODEL), BF16)],
        compiler_params=pltpu.CompilerParams(
            dimension_semantics=("arbitrary",), vmem_limit_bytes=VMEM_LIMIT),
        name="wconv_in",
    )(x_s, nw, w_f32)


def _sections_kernel(p_ref, *refs):
    *refs, tail_scr = refs
    pfull = jnp.concatenate([p_ref[j] for j in range(p_ref.shape[0])], axis=1)
    tail_scr[...] = pfull[:, Q0:W_ROWS]

    def proj(a, b):
        if a >= Q0:
            return tail_scr[:, a - Q0:b - Q0]
        return pfull[:, a:b]

    _emit_sections(proj, p_ref.shape[1], *refs)


def _sections(p, qnw, knw, tables):
    rows = p.shape[1]
    sec_in, out_specs, out_shape = _section_specs(rows, rows, rows, 1, False)
    return pl.pallas_call(
        _sections_kernel,
        grid=(1,),
        in_specs=[pl.BlockSpec(p.shape, lambda i: (0, 0, 0))] + sec_in,
        out_specs=out_specs,
        out_shape=out_shape,
        scratch_shapes=[pltpu.VMEM((rows, W_ROWS - Q0), F32)],
        compiler_params=pltpu.CompilerParams(
            dimension_semantics=("arbitrary",), vmem_limit_bytes=VMEM_LIMIT),
        name="sections",
    )(p, qnw, knw, *tables)


def _ssd_chain(s, xbc_ref, dtr_ref, z_ref, cprev_ref, h0_ref, cw_ref, cb_ref, dtb_ref, alog_ref,
               dexp_ref, y_ref, hout_ref, xext, ht, xc_scr, lin, valid):
    L = SSD_CHUNK
    c = pl.program_id(1)
    last = pl.num_programs(1) - 1

    @pl.when(c == 0)
    def _():
        xext[s, 0:SUBLANES, :] = cprev_ref[s]
        for g in range(SSD_GROUPS):
            ht[s, g] = h0_ref[s, g * HALF_INNER:(g + 1) * HALF_INNER, :].T

    def pad_rows(v):
        if lin == L:
            return v
        return jnp.concatenate([v, jnp.zeros((L - lin, v.shape[1]), v.dtype)], axis=0)

    lane = lax.broadcasted_iota(jnp.int32, (1, LANES), 1)
    rowid = lax.broadcasted_iota(jnp.int32, (L, 1), 0)
    dt = jax.nn.softplus(pad_rows(dtr_ref[s]) + dtb_ref[...])
    dt = jnp.where((lane < SSD_HEADS) & (rowid < valid), dt, 0.0)
    dta = dt * (-jnp.exp(alog_ref[...]) * LOG2E)

    r2 = lax.broadcasted_iota(jnp.int32, (L, L), 0)
    c2 = lax.broadcasted_iota(jnp.int32, (L, L), 1)
    tri = r2 >= c2
    tri_b = jnp.where(tri, 1.0, 0.0).astype(BF16)
    cum = _dot_exact_lhs(tri_b, dta)
    cum_last = cum[L - 1:L, :]
    yield

    er = lax.broadcasted_iota(jnp.int32, (LANES, SSD_INNER), 0)
    ec = lax.broadcasted_iota(jnp.int32, (LANES, SSD_INNER), 1)
    expand = jnp.where(ec // SSD_HEADDIM == er, 1.0, 0.0).astype(BF16)
    ecum = _dot_wide_rhs(jnp.exp2(cum), expand)
    wexp = _dot_wide_rhs(jnp.exp2(cum_last - cum) * dt, expand)
    cd = _dot_exact_rhs(jnp.broadcast_to(jnp.exp2(cum_last), (SUBLANES, LANES)), expand)[0:1, :]
    cum_t = cum.T
    dt_t = dt.T
    yield

    if lin < L:
        xext[s, SUBLANES + lin:SUBLANES + L, :] = jnp.zeros((L - lin, CONV_DIM), F32)
    xext[s, SUBLANES:SUBLANES + lin, :] = xbc_ref[s]

    def load(cs):
        return xext[s, 0:SUBLANES + L, cs]

    def store(cs, v):
        xc_scr[s, :, cs] = v

    _causal_conv_silu(load, cw_ref, cb_ref, store)
    xext[s, 0:SUBLANES, :] = xext[s, L:L + SUBLANES, :]
    xc = xc_scr.at[s]
    xs = xc[:, :SSD_INNER]
    bm = xc[:, SSD_INNER:SSD_INNER + SSD_GROUPS * SSD_STATE]
    cm = xc[:, SSD_INNER + SSD_GROUPS * SSD_STATE:]
    bm_b = bm.astype(BF16)
    cm_b = cm.astype(BF16)
    xs_b = xs.astype(BF16)

    y_off = jnp.concatenate(
        [_dot(cm_b[:, g * SSD_STATE:(g + 1) * SSD_STATE], ht[s, g].astype(BF16))
         for g in range(SSD_GROUPS)], axis=1) * ecum
    cbs = [_dot_nt(cm_b[:, g * SSD_STATE:(g + 1) * SSD_STATE], bm_b[:, g * SSD_STATE:(g + 1) * SSD_STATE])
           for g in range(SSD_GROUPS)]
    yield

    lane_half = lane // HEAD_DIM
    heads_per_group = SSD_HEADS // SSD_GROUPS
    y_parts = []
    for g in range(SSD_GROUPS):
        for pr in range(heads_per_group // 2):
            col0 = g * HALF_INNER + pr * LANES
            xp = xs_b[:, col0:col0 + LANES]
            yp = jnp.zeros((L, LANES), F32)
            for e in range(2):
                hh = g * heads_per_group + pr * 2 + e
                decay = jnp.exp2(cum[:, hh:hh + 1] - cum_t[hh:hh + 1, :])
                sc = jnp.where(tri, cbs[g] * decay, 0.0) * dt_t[hh:hh + 1, :]
                xm = jnp.where(lane_half == e, xp, jnp.zeros_like(xp))
                yp = yp + _dot(sc.astype(BF16), xm)
            y_parts.append(yp)
            if pr % 2 == 1:
                yield
    y_diag = jnp.concatenate(y_parts, axis=1)

    xw = (xs * wexp).astype(BF16)
    for g in range(SSD_GROUPS):
        sl = slice(g * HALF_INNER, (g + 1) * HALF_INNER)
        bm_t = bm[:, g * SSD_STATE:(g + 1) * SSD_STATE].T.astype(BF16)
        ht[s, g] = ht[s, g] * cd[:, sl] + _dot(bm_t, xw[:, sl])
    yield

    y = (y_diag + y_off + dexp_ref[...] * xs) * pad_rows(z_ref[s])
    y_ref[s] = y[:lin]

    @pl.when(c == last)
    def _():
        for g in range(SSD_GROUPS):
            hout_ref[s, g * HALF_INNER:(g + 1) * HALF_INNER, :] = ht[s, g].T


def _ssd_kernel(*refs, lin, valid):
    chains = [_ssd_chain(s, *refs, lin, valid) for s in range(SSD_SEQS)]
    while chains:
        alive = []
        for ch in chains:
            try:
                next(ch)
                alive.append(ch)
            except StopIteration:
                pass
        chains = alive


def _ssd(xbc, dtr, z, cprev, h0, cw, cb, dtb, alog, dexp, batch, lin, valid):
    rows = xbc.shape[0]
    seq = rows // batch
    nc = seq // lin
    g = SSD_SEQS
    assert batch % g == 0
    blk = lambda b, c: (b, c, 0)
    per_b = lambda b, c: (b, 0, 0)
    const = lambda b, c: (0, 0)
    y, hout = pl.pallas_call(
        functools.partial(_ssd_kernel, lin=lin, valid=valid),
        grid=(batch // g, nc),
        in_specs=[
            pl.BlockSpec((g, lin, CONV_DIM), blk),
            pl.BlockSpec((g, lin, LANES), blk),
            pl.BlockSpec((g, lin, SSD_INNER), blk),
            pl.BlockSpec((g, SUBLANES, CONV_DIM), per_b),
            pl.BlockSpec((g, SSD_INNER, SSD_STATE), per_b),
            pl.BlockSpec((SUBLANES, CONV_DIM), const),
            pl.BlockSpec((1, CONV_DIM), const),
            pl.BlockSpec((1, LANES), const),
            pl.BlockSpec((1, LANES), const),
            pl.BlockSpec((1, SSD_INNER), const),
        ],
        out_specs=[
            pl.BlockSpec((g, lin, SSD_INNER), blk),
            pl.BlockSpec((g, SSD_INNER, SSD_STATE), per_b),
        ],
        out_shape=[
            jax.ShapeDtypeStruct((batch, seq, SSD_INNER), F32),
            jax.ShapeDtypeStruct((batch, SSD_INNER, SSD_STATE), F32),
        ],
        scratch_shapes=[
            pltpu.VMEM((g, SUBLANES + SSD_CHUNK + SUBLANES, CONV_DIM), F32),
            pltpu.VMEM((g, SSD_GROUPS, SSD_STATE, HALF_INNER), F32),
            pltpu.VMEM((g, SSD_CHUNK, CONV_DIM), F32),
        ],
        compiler_params=pltpu.CompilerParams(
            dimension_semantics=("arbitrary", "arbitrary"), vmem_limit_bytes=VMEM_LIMIT),
        name="ssd",
    )(xbc.reshape(batch, seq, CONV_DIM), dtr.reshape(batch, seq, LANES),
      z.reshape(batch, seq, SSD_INNER), cprev, h0, cw, cb, dtb, alog, dexp)
    return y.reshape(rows, SSD_INNER), hout


def _attn_prompt_kernel(q_ref, kt_ref, vt_ref, g_ref, o_ref,
                        bias_scr, kh_scr, va_scr, qt_scr, m_scr, acc_scr, s_scr, s2_scr, *, n_tab, n_sb_max):
    b = pl.program_id(0)
    i = pl.program_id(1)
    seq = kt_ref.shape[2]
    pair_w = 2 * Q_BLOCK
    pairs_per_kv = ATT_GQ // 2

    @pl.when((b == 0) & (i == 0))
    def _():
        r = lax.broadcasted_iota(jnp.int32, (K_SUPER, Q_BLOCK), 0)
        c = lax.broadcasted_iota(jnp.int32, (K_SUPER, Q_BLOCK), 1)
        for tb in range(n_tab):
            w = _multiplicity(tb * Q_BLOCK + c - r)
            bias_scr[tb] = jnp.where(w > 0.0, jnp.log2(jnp.maximum(w, 1.0)), NEG)
        va_scr[:, HEAD_DIM:, :] = jnp.ones((ATT_KV_HEADS, ONES_ROWS, seq), BF16)

    @pl.when(i == 0)
    def _():
        for kvh in range(ATT_KV_HEADS):
            hs = slice(kvh * HEAD_DIM, (kvh + 1) * HEAD_DIM)
            va_scr[kvh, 0:HEAD_DIM, :] = vt_ref[0, hs, :].astype(BF16)
        for cidx in range(seq // K_SUPER):
            rows = slice(cidx * K_SUPER, (cidx + 1) * K_SUPER)
            kc = kt_ref[0, :, rows].T
            for kvh in range(ATT_KV_HEADS):
                hs = slice(kvh * HEAD_DIM, (kvh + 1) * HEAD_DIM)
                kh_scr[kvh, rows, :] = kc[:, hs].astype(BF16)

    for kvh in range(ATT_KV_HEADS):
        qt_scr[kvh] = jnp.concatenate(
            [q_ref[0, (kvh * ATT_GQ + g) * HEAD_DIM:(kvh * ATT_GQ + g + 1) * HEAD_DIM, :]
             for g in range(ATT_GQ)], axis=1)
    m_scr[...] = jnp.full(m_scr.shape, NEG, F32)
    acc_scr[...] = jnp.zeros(acc_scr.shape, F32)
    par = i % Q_PER_SUPER
    j_last = i // Q_PER_SUPER
    n_sb = jnp.minimum(j_last + 1, n_sb_max)

    def key_start(dl):
        return pl.multiple_of((j_last - dl) * K_SUPER, K_SUPER)

    def scores(dl, dst):
        for kvh in range(ATT_KV_HEADS):
            dst[kvh] = _dot(kh_scr[kvh, pl.ds(key_start(dl), K_SUPER), :], qt_scr[kvh])

    def softmax_pv(dl, src):
        start = key_start(dl)
        bias = bias_scr[par + Q_PER_SUPER * dl]
        bias2 = jnp.concatenate([bias, bias], axis=1)
        for kvh in range(ATT_KV_HEADS):
            vtb = va_scr[kvh, :, pl.ds(start, K_SUPER)]
            for pr in range(pairs_per_kv):
                u = kvh * pairs_per_kv + pr
                s = src[kvh, :, pr * pair_w:(pr + 1) * pair_w] + bias2
                m_old = m_scr[u, 0:1, :]
                m_new = jnp.maximum(m_old, jnp.max(s, axis=0, keepdims=True))
                p = jnp.exp2(s - m_new).astype(BF16)
                acc_scr[u] = jnp.exp2(m_old - m_new) * acc_scr[u] + _dot(vtb, p)
                m_scr[u, 0:1, :] = m_new

    def body(t, carry):
        scores(2 * t, s_scr)
        scores(2 * t + 1, s2_scr)
        softmax_pv(2 * t, s_scr)
        softmax_pv(2 * t + 1, s2_scr)
        return carry

    lax.fori_loop(0, n_sb // 2, body, 0)

    @pl.when(n_sb % 2 == 1)
    def _():
        scores(n_sb - 1, s_scr)
        softmax_pv(n_sb - 1, s_scr)

    for u in range(ATT_HEADS // 2):
        acc = acc_scr[u]
        o = acc[0:HEAD_DIM] * (1.0 / acc[HEAD_DIM:HEAD_DIM + 1])
        for e in range(2):
            hs = slice((2 * u + e) * HEAD_DIM, (2 * u + e + 1) * HEAD_DIM)
            o_ref[0, hs, :] = (o[:, e * Q_BLOCK:(e + 1) * Q_BLOCK] * g_ref[0, hs, :]).astype(BF16)


def _attn_prompt(qt, kt, vt, gate_t, batch, seq):
    nqb = seq // Q_BLOCK
    assert seq % K_SUPER == 0
    n_sb_max = min(seq // K_SUPER, W_MAX // K_SUPER + 1)
    n_tab = Q_PER_SUPER * n_sb_max
    qblk = lambda b, i: (b, 0, i)
    per_b = lambda b, i: (b, 0, 0)
    return pl.pallas_call(
        functools.partial(_attn_prompt_kernel, n_tab=n_tab, n_sb_max=n_sb_max),
        grid=(batch, nqb),
        in_specs=[
            pl.BlockSpec((1, ATT_INNER, Q_BLOCK), qblk),
            pl.BlockSpec((1, KV_DIM, seq), per_b),
            pl.BlockSpec((1, KV_DIM, seq), per_b),
            pl.BlockSpec((1, ATT_INNER, Q_BLOCK), qblk),
        ],
        out_specs=pl.BlockSpec((1, ATT_INNER, Q_BLOCK), qblk),
        out_shape=jax.ShapeDtypeStruct((batch, ATT_INNER, seq), BF16),
        scratch_shapes=[
            pltpu.VMEM((n_tab, K_SUPER, Q_BLOCK), F32),
            pltpu.VMEM((ATT_KV_HEADS, seq, HEAD_DIM), BF16),
            pltpu.VMEM((ATT_KV_HEADS, ACC_ROWS, seq), BF16),
            pltpu.VMEM((ATT_KV_HEADS, HEAD_DIM, ATT_GQ * Q_BLOCK), BF16),
            pltpu.VMEM((ATT_HEADS // 2, SUBLANES, 2 * Q_BLOCK), F32),
            pltpu.VMEM((ATT_HEADS // 2, ACC_ROWS, 2 * Q_BLOCK), F32),
            pltpu.VMEM((ATT_KV_HEADS, K_SUPER, ATT_GQ * Q_BLOCK), F32),
            pltpu.VMEM((ATT_KV_HEADS, K_SUPER, ATT_GQ * Q_BLOCK), F32),
        ],
        compiler_params=pltpu.CompilerParams(
            dimension_semantics=("arbitrary", "arbitrary"), vmem_limit_bytes=VMEM_LIMIT),
        name="attn_prompt",
    )(qt, kt, vt, gate_t)


def _attn_sample_kernel(q_ref, knt_ref, vnt_ref, g_ref, ckt_ref, cvt_ref, o_ref, kot_ref, vot_ref, *, ts):
    win = ckt_ref.shape[2]
    tp = SAMPLE_PAD
    rows = ATT_GQ * tp
    t_c = lax.broadcasted_iota(jnp.int32, (rows, win), 0) % tp
    j_c = lax.broadcasted_iota(jnp.int32, (rows, win), 1)
    w_c = _multiplicity(win + t_c - j_c)
    t_n = lax.broadcasted_iota(jnp.int32, (rows, tp), 0) % tp
    j_n = lax.broadcasted_iota(jnp.int32, (rows, tp), 1)
    w_n = jnp.where(j_n < ts, _multiplicity(t_n - j_n), 0.0)
    for kvh in range(ATT_KV_HEADS):
        c0 = kvh * ATT_GQ * HEAD_DIM
        qh = jnp.concatenate(
            [q_ref[:, c0 + g * HEAD_DIM:c0 + (g + 1) * HEAD_DIM] for g in range(ATT_GQ)], axis=0)
        hs = slice(kvh * HEAD_DIM, (kvh + 1) * HEAD_DIM)
        s_c = jnp.where(w_c > 0.0, _dot(qh, ckt_ref[0, hs, :].astype(BF16)), NEG)
        s_n = jnp.where(w_n > 0.0, _dot(qh, knt_ref[0, hs, :].astype(BF16)), NEG)
        m = jnp.maximum(jnp.max(s_c, axis=-1, keepdims=True), jnp.max(s_n, axis=-1, keepdims=True))
        p_c = jnp.exp2(s_c - m) * w_c
        p_n = jnp.exp2(s_n - m) * w_n
        den = jnp.sum(p_c, axis=-1, keepdims=True) + jnp.sum(p_n, axis=-1, keepdims=True)
        num = (_dot_nt(p_c.astype(BF16), cvt_ref[0, hs, :].astype(BF16))
               + _dot_nt(p_n.astype(BF16), vnt_ref[0, hs, :].astype(BF16)))
        o = num / den
        o = jnp.concatenate([o[g * tp:(g + 1) * tp] for g in range(ATT_GQ)], axis=1)
        csl = slice(c0, c0 + ATT_GQ * HEAD_DIM)
        o_ref[:, csl] = (o * g_ref[:, csl]).astype(BF16)
    kot_ref[0, :, 0:win - ts] = ckt_ref[0, :, ts:win]
    kot_ref[0, :, win - ts:win] = knt_ref[0, :, 0:ts]
    vot_ref[0, :, 0:win - ts] = cvt_ref[0, :, ts:win]
    vot_ref[0, :, win - ts:win] = vnt_ref[0, :, 0:ts]


def _attn_sample(q, knt, vnt, gate, ckt, cvt, ts):
    batch, _, win = ckt.shape
    tp = SAMPLE_PAD
    blk = lambda b: (b, 0)
    per_b = lambda b: (b, 0, 0)
    return pl.pallas_call(
        functools.partial(_attn_sample_kernel, ts=ts),
        grid=(batch,),
        in_specs=[
            pl.BlockSpec((tp, ATT_INNER), blk),
            pl.BlockSpec((1, KV_DIM, tp), per_b),
            pl.BlockSpec((1, KV_DIM, tp), per_b),
            pl.BlockSpec((tp, ATT_INNER), blk),
            pl.BlockSpec((1, KV_DIM, win), per_b),
            pl.BlockSpec((1, KV_DIM, win), per_b),
        ],
        out_specs=[
            pl.BlockSpec((tp, ATT_INNER), blk),
            pl.BlockSpec((1, KV_DIM, win), per_b),
            pl.BlockSpec((1, KV_DIM, win), per_b),
        ],
        out_shape=[
            jax.ShapeDtypeStruct((batch * tp, ATT_INNER), BF16),
            jax.ShapeDtypeStruct((batch, KV_DIM, win), F32),
            jax.ShapeDtypeStruct((batch, KV_DIM, win), F32),
        ],
        compiler_params=pltpu.CompilerParams(
            dimension_semantics=("arbitrary",), vmem_limit_bytes=VMEM_LIMIT),
        name="attn_sample",
    )(q, knt, vnt, gate, ckt, cvt)


def _group_norm(yg, nw):
    ms = jnp.mean(yg * yg, axis=-1, keepdims=True)
    return (yg * lax.rsqrt(ms + EPS) * nw).astype(BF16)


def _outproj_kernel(x_ref, ys_ref, yat_ref, nw_ref, w_ref, o_ref):
    acc = x_ref[...]
    for g in range(SSD_GROUPS):
        gs = slice(g * HALF_INNER, (g + 1) * HALF_INNER)
        acc = acc + _dot(_group_norm(ys_ref[:, gs], nw_ref[:, gs]), w_ref[gs, :])
    ya = yat_ref[0].astype(F32).T.astype(BF16)
    o_ref[...] = acc + _dot(ya, w_ref[SSD_INNER:, :])


def _out_proj(x2d, y_ssd, y_att_t, nw, w_o, tm):
    rows = x2d.shape[0]
    per_seq = y_att_t.shape[2] // tm
    row = lambda i: (i, 0)
    const = lambda i: (0, 0)
    return pl.pallas_call(
        _outproj_kernel,
        grid=(rows // tm,),
        in_specs=[
            pl.BlockSpec((tm, D_MODEL), row),
            pl.BlockSpec((tm, SSD_INNER), row),
            pl.BlockSpec((1, ATT_INNER, tm), lambda i: (i // per_seq, 0, i % per_seq)),
            pl.BlockSpec((1, SSD_INNER), const),
            pl.BlockSpec((SSD_INNER + ATT_INNER, D_MODEL), const, pipeline_mode=pl.Buffered(1)),
        ],
        out_specs=pl.BlockSpec((tm, D_MODEL), row),
        out_shape=jax.ShapeDtypeStruct((rows, D_MODEL), F32),
        compiler_params=pltpu.CompilerParams(
            dimension_semantics=("arbitrary",), vmem_limit_bytes=VMEM_LIMIT),
        name="out_proj",
    )(x2d, y_ssd, y_att_t, nw, w_o)


def _wconv_out_kernel(x_ref, ys_ref, ya_ref, nw_ref, w_ref, wb_ref, o_ref):
    j = pl.program_id(0)
    wb = w_ref[...].astype(BF16)
    wb_ref[...] = wb
    y = jnp.where(j < SSD_GROUPS, _group_norm(ys_ref[...], nw_ref[...]), ya_ref[...])
    part = _dot(y, wb)
    bs, ts, _ = x_ref.shape

    @pl.when(j == 0)
    def _():
        for b in range(bs):
            o_ref[b] = x_ref[b] + part[b * SAMPLE_PAD:b * SAMPLE_PAD + ts]

    @pl.when(j > 0)
    def _():
        for b in range(bs):
            o_ref[b] += part[b * SAMPLE_PAD:b * SAMPLE_PAD + ts]


def _wconv_out(x_s, y_ssd, y_att, nw, w_f32):
    rows = x_s.shape[0] * SAMPLE_PAD
    n_ssd = SSD_GROUPS
    n_chunks = (SSD_INNER + ATT_INNER) // WO_CHUNK
    const = lambda j: (0, 0)
    ssd_chunk = lambda j: (0, jnp.minimum(j, n_ssd - 1))
    return pl.pallas_call(
        _wconv_out_kernel,
        grid=(n_chunks,),
        in_specs=[
            pl.BlockSpec(x_s.shape, lambda j: (0, 0, 0)),
            pl.BlockSpec((rows, WO_CHUNK), ssd_chunk),
            pl.BlockSpec((rows, WO_CHUNK), lambda j: (0, jnp.maximum(j - n_ssd, 0))),
            pl.BlockSpec((1, WO_CHUNK), ssd_chunk),
            pl.BlockSpec((WO_CHUNK, D_MODEL), lambda j: (j, 0)),
        ],
        out_specs=[
            pl.BlockSpec((WO_CHUNK, D_MODEL), lambda j: (j, 0)),
            pl.BlockSpec(x_s.shape, lambda j: (0, 0, 0)),
        ],
        out_shape=[
            jax.ShapeDtypeStruct((SSD_INNER + ATT_INNER, D_MODEL), BF16),
            jax.ShapeDtypeStruct(x_s.shape, F32),
        ],
        compiler_params=pltpu.CompilerParams(
            dimension_semantics=("arbitrary",), vmem_limit_bytes=VMEM_LIMIT),
        name="wconv_out",
    )(x_s, y_ssd, y_att, nw, w_f32)


def _rope_tables(pos):
    n = pos.shape[0]
    half = ROT_DIM // 2
    inv = ROPE_THETA ** (-np.arange(0, ROT_DIM, 2, dtype=np.float64) / ROT_DIM)
    ang = pos.astype(np.float64)[:, None] * inv[None, :]
    cos, sin = np.cos(ang), np.sin(ang)
    rest = HEAD_DIM - ROT_DIM
    zh = np.zeros((n, half))
    cos_h = np.concatenate([cos, cos, np.ones((n, rest))], axis=1)
    sa_h = np.concatenate([zh, sin, np.zeros((n, rest))], axis=1)
    sb_h = np.concatenate([-sin, zh, np.zeros((n, rest))], axis=1)
    rep = LANES // HEAD_DIM
    return tuple(jnp.asarray(np.tile(t, (1, rep)), dtype=F32) for t in (cos_h, sa_h, sb_h))


def _lane_pad(v, n=LANES):
    return jnp.pad(v, (0, n - v.shape[0])).reshape(1, n)


def _to_cache(xt, batch, seq):
    return xt.reshape(1, batch, ATT_KV_HEADS, HEAD_DIM, seq).transpose(0, 1, 4, 2, 3)


def kernel(x_prompt, x_sample, cache_k, cache_v, state_conv, state_ssm, norm_w, w_in, conv_w,
           conv_b, dt_bias, a_log, d_skip, ssd_norm_w, q_norm_w, k_norm_w, w_out):
    bp, tp_, _ = x_prompt.shape
    bs, ts, _ = x_sample.shape
    depth = w_in.shape[0]
    assert depth == 1 and tp_ % SSD_CHUNK == 0 and ts <= SAMPLE_PAD and ts >= CONV_W - 1
    l = 0
    win = cache_k.shape[2]

    nw = norm_w[l].reshape(1, D_MODEL)
    rep = LANES // HEAD_DIM
    qnw = jnp.tile(q_norm_w[l], rep).reshape(1, LANES)
    knw = jnp.tile(k_norm_w[l], rep).reshape(1, LANES)
    cw = jnp.pad(conv_w[l], ((0, SUBLANES - CONV_W), (0, 0)))
    cb = conv_b[l].reshape(1, CONV_DIM)
    dtb = _lane_pad(dt_bias[l])
    alog = _lane_pad(a_log[l])
    dexp = jnp.repeat(d_skip[l], SSD_HEADDIM).reshape(1, SSD_INNER)
    snw = ssd_norm_w[l].reshape(1, SSD_INNER)

    pad = SAMPLE_PAD
    tabs_s = _rope_tables(PAST_LEN + np.arange(bs * pad) % pad)
    w_t, proj_s = _wconv_in(x_sample, nw, jnp.swapaxes(w_in[l], 0, 1))
    zs, xbc, dtr, q, kt, vt, gs, _ = _sections(proj_s, qnw, knw, tabs_s)
    cprev = jnp.pad(state_conv[l], ((0, 0), (SUBLANES - (CONV_W - 1), 0), (0, 0)))
    y_ssd, h_s = _ssd(xbc, dtr, zs, cprev, state_ssm[l].reshape(bs, SSD_INNER, SSD_STATE),
                      cw, cb, dtb, alog, dexp, bs, pad, ts)
    knt = kt.reshape(KV_DIM, bs, pad).transpose(1, 0, 2)
    vnt = vt.reshape(KV_DIM, bs, pad).transpose(1, 0, 2)
    ckt = cache_k[l].transpose(0, 2, 3, 1).reshape(bs, KV_DIM, win)
    cvt = cache_v[l].transpose(0, 2, 3, 1).reshape(bs, KV_DIM, win)
    y_att, kot, vot = _attn_sample(q, knt, vnt, gs, ckt, cvt, ts)
    w_o, y_s = _wconv_out(x_sample, y_ssd, y_att, snw, w_out[l])
    k_s = _to_cache(kot, bs, win)
    v_s = _to_cache(vot, bs, win)
    c_s = xbc.reshape(bs, pad, CONV_DIM)[:, ts - (CONV_W - 1):ts][None]
    h_s = h_s.reshape(1, bs, SSD_HEADS, SSD_HEADDIM, SSD_STATE)

    tm = TM_IN
    xp2 = x_prompt.reshape(bp * tp_, D_MODEL)
    tabs = _rope_tables(np.arange(tp_))
    zs, xbc, dtr, q, kt, vt, gs, ctail = _in_proj(xp2, nw, w_t, qnw, knw, tabs, tm, tp_)
    y_ssd, h_p = _ssd(xbc, dtr, zs, jnp.zeros((bp, SUBLANES, CONV_DIM), F32),
                      jnp.zeros((bp, SSD_INNER, SSD_STATE), F32),
                      cw, cb, dtb, alog, dexp, bp, SSD_CHUNK, SSD_CHUNK)
    y_att = _attn_prompt(q, kt, vt, gs, bp, tp_)
    y_p = _out_proj(xp2, y_ssd, y_att, snw, w_o, TM_OUT).reshape(bp, tp_, D_MODEL)
    keep = min(W_MAX, tp_)
    k_p = _to_cache(kt, bp, tp_)[:, :, tp_ - keep:]
    v_p = _to_cache(vt, bp, tp_)[:, :, tp_ - keep:]
    c_p = ctail[:, SUBLANES - (CONV_W - 1):][None]
    h_p = h_p.reshape(1, bp, SSD_HEADS, SSD_HEADDIM, SSD_STATE)

    return (y_p, y_s, k_p, v_p, c_p, h_p, k_s, v_s, c_s, h_s)
```

```python
import functools

import jax
import jax.numpy as jnp
import numpy as np
from jax import lax
from jax.experimental import pallas as pl
from jax.experimental.pallas import tpu as pltpu

F32 = jnp.float32
BF16 = jnp.bfloat16

D_MODEL = 2048
SSD_HEADS = 16
SSD_HEADDIM = 64
SSD_INNER = SSD_HEADS * SSD_HEADDIM
SSD_GROUPS = 2
SSD_STATE = 128
CONV_W = 4
CONV_DIM = SSD_INNER + 2 * SSD_GROUPS * SSD_STATE
SSD_CHUNK = 128
ATT_HEADS = 16
ATT_KV_HEADS = 4
HEAD_DIM = 64
ATT_GQ = ATT_HEADS // ATT_KV_HEADS
ATT_INNER = ATT_HEADS * HEAD_DIM
KV_DIM = ATT_KV_HEADS * HEAD_DIM
ROT_DIM = HEAD_DIM // 4
ROPE_THETA = 500000.0
DILATED_BRANCHES = ((128, 1), (512, 4), (2048, 16))
W_MAX = 2048
PAST_LEN = 16384
EPS = 1e-6

LANES = 128
SUBLANES = 8
Q_BLOCK = 256
K_SUPER = 256
Q_PER_SUPER = K_SUPER // Q_BLOCK
ONES_ROWS = 16
ACC_ROWS = HEAD_DIM + ONES_ROWS
CONV_COLS = 256
TM_IN = 512
TM_OUT = 512
SSD_SEQS = 4
SAMPLE_PAD = 16
NEG = -1e30
VMEM_LIMIT = 58 * 1024 * 1024

Z0 = 0
X0 = Z0 + SSD_INNER
DT0 = X0 + CONV_DIM
Q0 = DT0 + SSD_HEADS
K0 = Q0 + ATT_INNER
V0 = K0 + KV_DIM
G0 = V0 + KV_DIM
W_ROWS = G0 + ATT_INNER
W_CHUNK = 1024
HALF_INNER = SSD_INNER // SSD_GROUPS
WO_CHUNK = HALF_INNER
LOG2E = 1.4426950408889634
Q_SCALE = HEAD_DIM ** -0.5 * LOG2E


def _dot(a, b):
    return jnp.dot(a, b, preferred_element_type=F32)


def _dot_nt(a, b):
    return lax.dot_general(a, b, (((1,), (1,)), ((), ())), preferred_element_type=F32)


def _split3(x):
    hi = x.astype(BF16)
    r1 = x - hi.astype(F32)
    mid = r1.astype(BF16)
    lo = (r1 - mid.astype(F32)).astype(BF16)
    return hi, mid, lo


def _dot_exact_rhs(x, m):
    hi, mid, lo = _split3(x)
    return _dot(hi, m) + _dot(mid, m) + _dot(lo, m)


def _dot_wide_rhs(x, m):
    hi = x.astype(BF16)
    lo = (x - hi.astype(F32)).astype(BF16)
    return _dot(hi, m) + _dot(lo, m)


def _dot_exact_lhs(m, x):
    hi, mid, lo = _split3(x)
    return _dot(m, hi) + _dot(m, mid) + _dot(m, lo)


def _silu(x):
    hx = 0.5 * x
    return hx * jnp.tanh(hx) + hx


def _multiplicity(d):
    w = jnp.zeros(d.shape, F32)
    for window, dil in DILATED_BRANCHES:
        hit = (d >= 0) & (d <= window) & (lax.rem(d, dil) == 0)
        w = w + jnp.where(hit, 1.0, 0.0)
    return w


def _norm_rope(y, nw, cos, sa, sb, scale):
    lane = lax.broadcasted_iota(jnp.int32, (1, LANES), 1)
    first = lane < HEAD_DIM
    y2 = y * y
    s_lo = jnp.sum(jnp.where(first, y2, 0.0), axis=-1, keepdims=True)
    s_hi = jnp.sum(jnp.where(first, 0.0, y2), axis=-1, keepdims=True)
    ms = jnp.where(first, s_lo, s_hi) * (1.0 / HEAD_DIM)
    yn = y * lax.rsqrt(ms + EPS) * nw
    half = ROT_DIM // 2
    rot = yn * cos + pltpu.roll(yn, half, 1) * sa + pltpu.roll(yn, LANES - half, 1) * sb
    return rot * scale


def _causal_conv_silu(load, cw_ref, cb_ref, store):
    for c0 in range(0, CONV_DIM, CONV_COLS):
        cs = slice(c0, c0 + CONV_COLS)
        xe = load(cs)
        x2 = pltpu.roll(xe, 2, 0)
        even = cw_ref[3:4, cs] * xe + cw_ref[1:2, cs] * x2
        odd = cw_ref[2:3, cs] * xe + cw_ref[0:1, cs] * x2
        store(cs, _silu((even + pltpu.roll(odd, 1, 0))[SUBLANES:, :] + cb_ref[:, cs]))


def _pre_norm(x_ref, nw_ref):
    x = x_ref[...]
    ms = jnp.mean(x * x, axis=-1, keepdims=True)
    return (x * lax.rsqrt(ms + EPS) * nw_ref[...]).astype(BF16)


def _inproj_kernel(x_ref, nw_ref, w_ref, *refs):
    hn = _pre_norm(x_ref, nw_ref)

    def proj(a, b):
        return _dot_nt(hn, w_ref[a:b, :])

    _emit_sections(proj, x_ref.shape[0], *refs)


def _emit_sections(proj, tm, qnw_ref, knw_ref, cos_ref, sa_ref, sb_ref,
                   zs_ref, xbc_ref, dt_ref, q_ref, kt_ref, vt_ref, gs_ref, ctail_ref):
    cos, sa, sb = cos_ref[...], sa_ref[...], sb_ref[...]
    q = proj(Q0, K0)
    for c in range(ATT_INNER // LANES):
        sl = slice(c * LANES, (c + 1) * LANES)
        qn = _norm_rope(q[:, sl], qnw_ref[...], cos, sa, sb, Q_SCALE)
        if len(q_ref.shape) == 3:
            q_ref[0, sl, :] = qn.T.astype(BF16)
        else:
            q_ref[:, sl] = qn.astype(BF16)
    k = proj(K0, V0)
    kn = [_norm_rope(k[:, c * LANES:(c + 1) * LANES], knw_ref[...], cos, sa, sb, 1.0)
          for c in range(KV_DIM // LANES)]
    kt_ref[0] = jnp.concatenate(kn, axis=1).T

    xbc = proj(X0, DT0)
    xbc_ref[...] = xbc
    ctail_ref[0] = xbc[tm - SUBLANES:tm, :]
    vt_ref[0] = proj(V0, G0).T
    zs_ref[...] = _silu(proj(Z0, X0))
    gs = _silu(proj(G0, W_ROWS))
    if len(gs_ref.shape) == 3:
        gs_ref[0] = gs.T
    else:
        gs_ref[...] = gs
    dt_ref[...] = proj(DT0, DT0 + LANES)


def _in_proj(x2d, nw, w_t, qnw, knw, tables, tm, seq):
    rows = x2d.shape[0]
    const = lambda i: (0, 0)
    row = lambda i: (i, 0)
    sec_in, out_specs, out_shape = _section_specs(rows, tm, seq, tables[0].shape[0] // tm, True)
    return pl.pallas_call(
        _inproj_kernel,
        grid=(rows // tm,),
        in_specs=[
            pl.BlockSpec((tm, D_MODEL), row),
            pl.BlockSpec((1, D_MODEL), const),
            pl.BlockSpec((W_ROWS, D_MODEL), const, pipeline_mode=pl.Buffered(1)),
        ] + sec_in,
        out_specs=out_specs,
        out_shape=out_shape,
        compiler_params=pltpu.CompilerParams(
            dimension_semantics=("arbitrary",), vmem_limit_bytes=VMEM_LIMIT),
        name="in_proj",
    )(x2d, nw, w_t, qnw, knw, *tables)


def _section_specs(rows, tm, seq, period, q_transposed):
    per_seq = seq // tm
    const = lambda i: (0, 0)
    row = lambda i: (i, 0)
    tab = lambda i: (i % period, 0)
    tr = lambda i: (i // per_seq, 0, i % per_seq)
    slab = lambda i: (i // per_seq, 0, 0)
    in_specs = [pl.BlockSpec((1, LANES), const)] * 2 + [pl.BlockSpec((tm, LANES), tab)] * 3
    row_outs = {0: (SSD_INNER, F32), 1: (CONV_DIM, F32), 2: (LANES, F32), 3: (ATT_INNER, BF16),
                6: (ATT_INNER, F32)}
    out_specs, out_shape = [], []
    for idx in range(7):
        if idx in (3, 6) and q_transposed:
            dt = row_outs[idx][1]
            out_specs.append(pl.BlockSpec((1, ATT_INNER, tm), tr))
            out_shape.append(jax.ShapeDtypeStruct((rows // seq, ATT_INNER, seq), dt))
        elif idx in row_outs:
            n, dt = row_outs[idx]
            out_specs.append(pl.BlockSpec((tm, n), row))
            out_shape.append(jax.ShapeDtypeStruct((rows, n), dt))
        else:
            out_specs.append(pl.BlockSpec((1, KV_DIM, tm), tr))
            out_shape.append(jax.ShapeDtypeStruct((rows // seq, KV_DIM, seq), F32))
    out_specs.append(pl.BlockSpec((1, SUBLANES, CONV_DIM), slab))
    out_shape.append(jax.ShapeDtypeStruct((rows // seq, SUBLANES, CONV_DIM), F32))
    return in_specs, out_specs, out_shape


def _wconv_in_kernel(x_ref, nw_ref, w_ref, wb_ref, p_ref, xpad_scr, hn_scr):
    @pl.when(pl.program_id(0) == 0)
    def _():
        bs, ts, _ = x_ref.shape
        xpad_scr[...] = jnp.zeros(xpad_scr.shape, F32)
        for b in range(bs):
            xpad_scr[b * SAMPLE_PAD:b * SAMPLE_PAD + ts, :] = x_ref[b]
        hn_scr[...] = _pre_norm(xpad_scr, nw_ref)

    row = pl.program_id(0) * W_CHUNK + lax.broadcasted_iota(jnp.int32, (W_CHUNK, 1), 0)
    wb = jnp.where(row < W_ROWS, w_ref[...], 0.0).astype(BF16)
    wb_ref[...] = wb
    p_ref[0] = _dot_nt(hn_scr[...], wb)


def _wconv_in(x_s, nw, w_f32):
    rows = x_s.shape[0] * SAMPLE_PAD
    n_chunks = pl.cdiv(W_ROWS, W_CHUNK)
    const = lambda j: (0, 0)
    return pl.pallas_call(
        _wconv_in_kernel,
        grid=(n_chunks,),
        in_specs=[
            pl.BlockSpec(x_s.shape, lambda j: (0, 0, 0)),
            pl.BlockSpec((1, D_MODEL), const),
            pl.BlockSpec((W_CHUNK, D_MODEL), lambda j: (j, 0)),
        ],
        out_specs=[
            pl.BlockSpec((W_CHUNK, D_MODEL), lambda j: (j, 0)),
            pl.BlockSpec((1, rows, W_CHUNK), lambda j: (j, 0, 0)),
        ],
        out_shape=[
            jax.ShapeDtypeStruct((W_ROWS, D_MODEL), BF16),
            jax.ShapeDtypeStruct((n_chunks, rows, W_CHUNK), F32),
        ],
        scratch_shapes=[pltpu.VMEM((rows, D_MODEL), F32), pltpu.VMEM((rows, D_MODEL), BF16)],
        compiler_params=pltpu.CompilerParams(
            dimension_semantics=("arbitrary",), vmem_limit_bytes=VMEM_LIMIT),
        name="wconv_in",
    )(x_s, nw, w_f32)


def _sections_kernel(p_ref, *refs):
    *refs, tail_scr = refs
    pfull = jnp.concatenate([p_ref[j] for j in range(p_ref.shape[0])], axis=1)
    tail_scr[...] = pfull[:, Q0:W_ROWS]

    def proj(a, b):
        if a >= Q0:
            return tail_scr[:, a - Q0:b - Q0]
        return pfull[:, a:b]

    _emit_sections(proj, p_ref.shape[1], *refs)


def _sections(p, qnw, knw, tables):
    rows = p.shape[1]
    sec_in, out_specs, out_shape = _section_specs(rows, rows, rows, 1, False)
    return pl.pallas_call(
        _sections_kernel,
        grid=(1,),
        in_specs=[pl.BlockSpec(p.shape, lambda i: (0, 0, 0))] + sec_in,
        out_specs=out_specs,
        out_shape=out_shape,
        scratch_shapes=[pltpu.VMEM((rows, W_ROWS - Q0), F32)],
        compiler_params=pltpu.CompilerParams(
            dimension_semantics=("arbitrary",), vmem_limit_bytes=VMEM_LIMIT),
        name="sections",
    )(p, qnw, knw, *tables)


def _ssd_chain(s, xbc_ref, dtr_ref, z_ref, cprev_ref, h0_ref, cw_ref, cb_ref, dtb_ref, alog_ref,
               dexp_ref, y_ref, hout_ref, xext, ht, xc_scr, lin, valid):
    L = SSD_CHUNK
    c = pl.program_id(1)
    last = pl.num_programs(1) - 1

    @pl.when(c == 0)
    def _():
        xext[s, 0:SUBLANES, :] = cprev_ref[s]
        for g in range(SSD_GROUPS):
            ht[s, g] = h0_ref[s, g * HALF_INNER:(g + 1) * HALF_INNER, :].T

    def pad_rows(v):
        if lin == L:
            return v
        return jnp.concatenate([v, jnp.zeros((L - lin, v.shape[1]), v.dtype)], axis=0)

    lane = lax.broadcasted_iota(jnp.int32, (1, LANES), 1)
    rowid = lax.broadcasted_iota(jnp.int32, (L, 1), 0)
    dt = jax.nn.softplus(pad_rows(dtr_ref[s]) + dtb_ref[...])
    dt = jnp.where((lane < SSD_HEADS) & (rowid < valid), dt, 0.0)
    dta = dt * (-jnp.exp(alog_ref[...]) * LOG2E)

    r2 = lax.broadcasted_iota(jnp.int32, (L, L), 0)
    c2 = lax.broadcasted_iota(jnp.int32, (L, L), 1)
    tri = r2 >= c2
    tri_b = jnp.where(tri, 1.0, 0.0).astype(BF16)
    cum = _dot_exact_lhs(tri_b, dta)
    cum_last = cum[L - 1:L, :]
    yield

    er = lax.broadcasted_iota(jnp.int32, (LANES, SSD_INNER), 0)
    ec = lax.broadcasted_iota(jnp.int32, (LANES, SSD_INNER), 1)
    expand = jnp.where(ec // SSD_HEADDIM == er, 1.0, 0.0).astype(BF16)
    ecum = _dot_wide_rhs(jnp.exp2(cum), expand)
    wexp = _dot_wide_rhs(jnp.exp2(cum_last - cum) * dt, expand)
    cd = _dot_exact_rhs(jnp.broadcast_to(jnp.exp2(cum_last), (SUBLANES, LANES)), expand)[0:1, :]
    cum_t = cum.T
    dt_t = dt.T
    yield

    if lin < L:
        xext[s, SUBLANES + lin:SUBLANES + L, :] = jnp.zeros((L - lin, CONV_DIM), F32)
    xext[s, SUBLANES:SUBLANES + lin, :] = xbc_ref[s]

    def load(cs):
        return xext[s, 0:SUBLANES + L, cs]

    def store(cs, v):
        xc_scr[s, :, cs] = v

    _causal_conv_silu(load, cw_ref, cb_ref, store)
    xext[s, 0:SUBLANES, :] = xext[s, L:L + SUBLANES, :]
    xc = xc_scr.at[s]
    xs = xc[:, :SSD_INNER]
    bm = xc[:, SSD_INNER:SSD_INNER + SSD_GROUPS * SSD_STATE]
    cm = xc[:, SSD_INNER + SSD_GROUPS * SSD_STATE:]
    bm_b = bm.astype(BF16)
    cm_b = cm.astype(BF16)
    xs_b = xs.astype(BF16)

    y_off = jnp.concatenate(
        [_dot(cm_b[:, g * SSD_STATE:(g + 1) * SSD_STATE], ht[s, g].astype(BF16))
         for g in range(SSD_GROUPS)], axis=1) * ecum
    cbs = [_dot_nt(cm_b[:, g * SSD_STATE:(g + 1) * SSD_STATE], bm_b[:, g * SSD_STATE:(g + 1) * SSD_STATE])
           for g in range(SSD_GROUPS)]
    yield

    lane_half = lane // HEAD_DIM
    heads_per_group = SSD_HEADS // SSD_GROUPS
    y_parts = []
    for g in range(SSD_GROUPS):
        for pr in range(heads_per_group // 2):
            col0 = g * HALF_INNER + pr * LANES
            xp = xs_b[:, col0:col0 + LANES]
            yp = jnp.zeros((L, LANES), F32)
            for e in range(2):
                hh = g * heads_per_group + pr * 2 + e
                decay = jnp.exp2(cum[:, hh:hh + 1] - cum_t[hh:hh + 1, :])
                sc = jnp.where(tri, cbs[g] * decay, 0.0) * dt_t[hh:hh + 1, :]
                xm = jnp.where(lane_half == e, xp, jnp.zeros_like(xp))
                yp = yp + _dot(sc.astype(BF16), xm)
            y_parts.append(yp)
            if pr % 2 == 1:
                yield
    y_diag = jnp.concatenate(y_parts, axis=1)

    xw = (xs * wexp).astype(BF16)
    for g in range(SSD_GROUPS):
        sl = slice(g * HALF_INNER, (g + 1) * HALF_INNER)
        bm_t = bm[:, g * SSD_STATE:(g + 1) * SSD_STATE].T.astype(BF16)
        ht[s, g] = ht[s, g] * cd[:, sl] + _dot(bm_t, xw[:, sl])
    yield

    y = (y_diag + y_off + dexp_ref[...] * xs) * pad_rows(z_ref[s])
    y_ref[s] = y[:lin]

    @pl.when(c == last)
    def _():
        for g in range(SSD_GROUPS):
            hout_ref[s, g * HALF_INNER:(g + 1) * HALF_INNER, :] = ht[s, g].T


def _ssd_kernel(*refs, lin, valid):
    chains = [_ssd_chain(s, *refs, lin, valid) for s in range(SSD_SEQS)]
    while chains:
        alive = []
        for ch in chains:
            try:
                next(ch)
                alive.append(ch)
            except StopIteration:
                pass
        chains = alive


def _ssd(xbc, dtr, z, cprev, h0, cw, cb, dtb, alog, dexp, batch, lin, valid):
    rows = xbc.shape[0]
    seq = rows // batch
    nc = seq // lin
    g = SSD_SEQS
    assert batch % g == 0
    blk = lambda b, c: (b, c, 0)
    per_b = lambda b, c: (b, 0, 0)
    const = lambda b, c: (0, 0)
    y, hout = pl.pallas_call(
        functools.partial(_ssd_kernel, lin=lin, valid=valid),
        grid=(batch // g, nc),
        in_specs=[
            pl.BlockSpec((g, lin, CONV_DIM), blk),
            pl.BlockSpec((g, lin, LANES), blk),
            pl.BlockSpec((g, lin, SSD_INNER), blk),
            pl.BlockSpec((g, SUBLANES, CONV_DIM), per_b),
            pl.BlockSpec((g, SSD_INNER, SSD_STATE), per_b),
            pl.BlockSpec((SUBLANES, CONV_DIM), const),
            pl.BlockSpec((1, CONV_DIM), const),
            pl.BlockSpec((1, LANES), const),
            pl.BlockSpec((1, LANES), const),
            pl.BlockSpec((1, SSD_INNER), const),
        ],
        out_specs=[
            pl.BlockSpec((g, lin, SSD_INNER), blk),
            pl.BlockSpec((g, SSD_INNER, SSD_STATE), per_b),
        ],
        out_shape=[
            jax.ShapeDtypeStruct((batch, seq, SSD_INNER), F32),
            jax.ShapeDtypeStruct((batch, SSD_INNER, SSD_STATE), F32),
        ],
        scratch_shapes=[
            pltpu.VMEM((g, SUBLANES + SSD_CHUNK + SUBLANES, CONV_DIM), F32),
            pltpu.VMEM((g, SSD_GROUPS, SSD_STATE, HALF_INNER), F32),
            pltpu.VMEM((g, SSD_CHUNK, CONV_DIM), F32),
        ],
        compiler_params=pltpu.CompilerParams(
            dimension_semantics=("arbitrary", "arbitrary"), vmem_limit_bytes=VMEM_LIMIT),
        name="ssd",
    )(xbc.reshape(batch, seq, CONV_DIM), dtr.reshape(batch, seq, LANES),
      z.reshape(batch, seq, SSD_INNER), cprev, h0, cw, cb, dtb, alog, dexp)
    return y.reshape(rows, SSD_INNER), hout


def _attn_prompt_kernel(q_ref, kt_ref, vt_ref, g_ref, o_ref,
                        bias_scr, kh_scr, va_scr, qt_scr, m_scr, acc_scr, s_scr, s2_scr, *, n_tab, n_sb_max):
    b = pl.program_id(0)
    i = pl.program_id(1)
    seq = kt_ref.shape[2]
    pair_w = 2 * Q_BLOCK
    pairs_per_kv = ATT_GQ // 2

    @pl.when((b == 0) & (i == 0))
    def _():
        r = lax.broadcasted_iota(jnp.int32, (K_SUPER, Q_BLOCK), 0)
        c = lax.broadcasted_iota(jnp.int32, (K_SUPER, Q_BLOCK), 1)
        for tb in range(n_tab):
            w = _multiplicity(tb * Q_BLOCK + c - r)
            bias_scr[tb] = jnp.where(w > 0.0, jnp.log2(jnp.maximum(w, 1.0)), NEG)
        va_scr[:, HEAD_DIM:, :] = jnp.ones((ATT_KV_HEADS, ONES_ROWS, seq), BF16)

    @pl.when(i == 0)
    def _():
        for kvh in range(ATT_KV_HEADS):
            hs = slice(kvh * HEAD_DIM, (kvh + 1) * HEAD_DIM)
            va_scr[kvh, 0:HEAD_DIM, :] = vt_ref[0, hs, :].astype(BF16)
        for cidx in range(seq // K_SUPER):
            rows = slice(cidx * K_SUPER, (cidx + 1) * K_SUPER)
            kc = kt_ref[0, :, rows].T
            for kvh in range(ATT_KV_HEADS):
                hs = slice(kvh * HEAD_DIM, (kvh + 1) * HEAD_DIM)
                kh_scr[kvh, rows, :] = kc[:, hs].astype(BF16)

    for kvh in range(ATT_KV_HEADS):
        qt_scr[kvh] = jnp.concatenate(
            [q_ref[0, (kvh * ATT_GQ + g) * HEAD_DIM:(kvh * ATT_GQ + g + 1) * HEAD_DIM, :]
             for g in range(ATT_GQ)], axis=1)
    m_scr[...] = jnp.full(m_scr.shape, NEG, F32)
    acc_scr[...] = jnp.zeros(acc_scr.shape, F32)
    par = i % Q_PER_SUPER
    j_last = i // Q_PER_SUPER
    n_sb = jnp.minimum(j_last + 1, n_sb_max)

    def key_start(dl):
        return pl.multiple_of((j_last - dl) * K_SUPER, K_SUPER)

    def scores(dl, dst):
        for kvh in range(ATT_KV_HEADS):
            dst[kvh] = _dot(kh_scr[kvh, pl.ds(key_start(dl), K_SUPER), :], qt_scr[kvh])

    def softmax_pv(dl, src):
        start = key_start(dl)
        bias = bias_scr[par + Q_PER_SUPER * dl]
        bias2 = jnp.concatenate([bias, bias], axis=1)
        for kvh in range(ATT_KV_HEADS):
            vtb = va_scr[kvh, :, pl.ds(start, K_SUPER)]
            for pr in range(pairs_per_kv):
                u = kvh * pairs_per_kv + pr
                s = src[kvh, :, pr * pair_w:(pr + 1) * pair_w] + bias2
                m_old = m_scr[u, 0:1, :]
                m_new = jnp.maximum(m_old, jnp.max(s, axis=0, keepdims=True))
                p = jnp.exp2(s - m_new).astype(BF16)
                acc_scr[u] = jnp.exp2(m_old - m_new) * acc_scr[u] + _dot(vtb, p)
                m_scr[u, 0:1, :] = m_new

    def diagonal_block():
        half = K_SUPER // 2
        start = key_start(0)
        bias = bias_scr[0]
        for kvh in range(ATT_KV_HEADS):
            s_scr[kvh, 0:half, :] = _dot(kh_scr[kvh, pl.ds(start, half), :], qt_scr[kvh])
        for kvh in range(ATT_KV_HEADS):
            q_late = jnp.concatenate(
                [qt_scr[kvh, :, g * Q_BLOCK + half:(g + 1) * Q_BLOCK] for g in range(ATT_GQ)], axis=1)
            s2_scr[kvh, 0:half, 0:ATT_GQ * half] = _dot(
                kh_scr[kvh, pl.ds(start + half, half), :], q_late)
        bias_a = jnp.concatenate([bias[0:half, :]] * 2, axis=1)
        for kvh in range(ATT_KV_HEADS):
            vtb = va_scr[kvh, :, pl.ds(start, half)]
            for pr in range(pairs_per_kv):
                u = kvh * pairs_per_kv + pr
                s = s_scr[kvh, 0:half, pr * pair_w:(pr + 1) * pair_w] + bias_a
                m_old = m_scr[u, 0:1, :]
                m_new = jnp.maximum(m_old, jnp.max(s, axis=0, keepdims=True))
                p = jnp.exp2(s - m_new).astype(BF16)
                acc_scr[u] = jnp.exp2(m_old - m_new) * acc_scr[u] + _dot(vtb, p)
                m_scr[u, 0:1, :] = m_new
        bias_b = jnp.concatenate([bias[half:, half:]] * 2, axis=1)
        late = [slice(e * Q_BLOCK + half, (e + 1) * Q_BLOCK) for e in range(2)]
        for kvh in range(ATT_KV_HEADS):
            vtb = va_scr[kvh, :, pl.ds(start + half, half)]
            for pr in range(pairs_per_kv):
                u = kvh * pairs_per_kv + pr
                s = s2_scr[kvh, 0:half, pr * Q_BLOCK:(pr + 1) * Q_BLOCK] + bias_b
                m_old = jnp.concatenate([m_scr[u, 0:1, sl] for sl in late], axis=1)
                m_new = jnp.maximum(m_old, jnp.max(s, axis=0, keepdims=True))
                p = jnp.exp2(s - m_new).astype(BF16)
                alpha = jnp.exp2(m_old - m_new)
                pv = _dot(vtb, p)
                for e, sl in enumerate(late):
                    es = slice(e * half, (e + 1) * half)
                    acc_scr[u, :, sl] = alpha[:, es] * acc_scr[u, :, sl] + pv[:, es]
                    m_scr[u, 0:1, sl] = m_new[:, es]

    diagonal_block()
    n_rest = n_sb - 1

    def body(t, carry):
        scores(1 + 2 * t, s_scr)
        scores(2 + 2 * t, s2_scr)
        softmax_pv(1 + 2 * t, s_scr)
        softmax_pv(2 + 2 * t, s2_scr)
        return carry

    lax.fori_loop(0, n_rest // 2, body, 0)

    @pl.when(n_rest % 2 == 1)
    def _():
        scores(n_sb - 1, s_scr)
        softmax_pv(n_sb - 1, s_scr)

    for u in range(ATT_HEADS // 2):
        acc = acc_scr[u]
        o = acc[0:HEAD_DIM] * (1.0 / acc[HEAD_DIM:HEAD_DIM + 1])
        for e in range(2):
            hs = slice((2 * u + e) * HEAD_DIM, (2 * u + e + 1) * HEAD_DIM)
            o_ref[0, hs, :] = (o[:, e * Q_BLOCK:(e + 1) * Q_BLOCK] * g_ref[0, hs, :]).astype(BF16)


def _attn_prompt(qt, kt, vt, gate_t, batch, seq):
    nqb = seq // Q_BLOCK
    assert seq % K_SUPER == 0 and Q_BLOCK == K_SUPER
    n_sb_max = min(seq // K_SUPER, W_MAX // K_SUPER + 1)
    n_tab = Q_PER_SUPER * n_sb_max
    qblk = lambda b, i: (b, 0, i)
    per_b = lambda b, i: (b, 0, 0)
    return pl.pallas_call(
        functools.partial(_attn_prompt_kernel, n_tab=n_tab, n_sb_max=n_sb_max),
        grid=(batch, nqb),
        in_specs=[
            pl.BlockSpec((1, ATT_INNER, Q_BLOCK), qblk),
            pl.BlockSpec((1, KV_DIM, seq), per_b),
            pl.BlockSpec((1, KV_DIM, seq), per_b),
            pl.BlockSpec((1, ATT_INNER, Q_BLOCK), qblk),
        ],
        out_specs=pl.BlockSpec((1, ATT_INNER, Q_BLOCK), qblk),
        out_shape=jax.ShapeDtypeStruct((batch, ATT_INNER, seq), BF16),
        scratch_shapes=[
            pltpu.VMEM((n_tab, K_SUPER, Q_BLOCK), F32),
            pltpu.VMEM((ATT_KV_HEADS, seq, HEAD_DIM), BF16),
            pltpu.VMEM((ATT_KV_HEADS, ACC_ROWS, seq), BF16),
            pltpu.VMEM((ATT_KV_HEADS, HEAD_DIM, ATT_GQ * Q_BLOCK), BF16),
            pltpu.VMEM((ATT_HEADS // 2, SUBLANES, 2 * Q_BLOCK), F32),
            pltpu.VMEM((ATT_HEADS // 2, ACC_ROWS, 2 * Q_BLOCK), F32),
            pltpu.VMEM((ATT_KV_HEADS, K_SUPER, ATT_GQ * Q_BLOCK), F32),
            pltpu.VMEM((ATT_KV_HEADS, K_SUPER, ATT_GQ * Q_BLOCK), F32),
        ],
        compiler_params=pltpu.CompilerParams(
            dimension_semantics=("arbitrary", "arbitrary"), vmem_limit_bytes=VMEM_LIMIT),
        name="attn_prompt",
    )(qt, kt, vt, gate_t)


def _attn_sample_kernel(q_ref, knt_ref, vnt_ref, g_ref, ckt_ref, cvt_ref, o_ref, kot_ref, vot_ref,
                        bc_scr, bn_scr, *, ts):
    win = ckt_ref.shape[2]
    tp = SAMPLE_PAD
    rows = ATT_GQ * tp

    @pl.when(pl.program_id(0) == 0)
    def _():
        t_c = lax.broadcasted_iota(jnp.int32, (rows, win), 0) % tp
        j_c = lax.broadcasted_iota(jnp.int32, (rows, win), 1)
        w_c = _multiplicity(win + t_c - j_c)
        bc_scr[...] = jnp.where(w_c > 0.0, jnp.log2(jnp.maximum(w_c, 1.0)), NEG)
        t_n = lax.broadcasted_iota(jnp.int32, (rows, tp), 0) % tp
        j_n = lax.broadcasted_iota(jnp.int32, (rows, tp), 1)
        w_n = jnp.where(j_n < ts, _multiplicity(t_n - j_n), 0.0)
        bn_scr[...] = jnp.where(w_n > 0.0, jnp.log2(jnp.maximum(w_n, 1.0)), NEG)

    for kvh in range(ATT_KV_HEADS):
        c0 = kvh * ATT_GQ * HEAD_DIM
        qh = jnp.concatenate(
            [q_ref[:, c0 + g * HEAD_DIM:c0 + (g + 1) * HEAD_DIM] for g in range(ATT_GQ)], axis=0)
        hs = slice(kvh * HEAD_DIM, (kvh + 1) * HEAD_DIM)
        s_c = _dot(qh, ckt_ref[0, hs, :].astype(BF16)) + bc_scr[...]
        s_n = _dot(qh, knt_ref[0, hs, :].astype(BF16)) + bn_scr[...]
        m = jnp.maximum(jnp.max(s_c, axis=-1, keepdims=True), jnp.max(s_n, axis=-1, keepdims=True))
        p_c = jnp.exp2(s_c - m)
        p_n = jnp.exp2(s_n - m)
        den = jnp.sum(p_c, axis=-1, keepdims=True) + jnp.sum(p_n, axis=-1, keepdims=True)
        num = (_dot_nt(p_c.astype(BF16), cvt_ref[0, hs, :].astype(BF16))
               + _dot_nt(p_n.astype(BF16), vnt_ref[0, hs, :].astype(BF16)))
        o = num / den
        o = jnp.concatenate([o[g * tp:(g + 1) * tp] for g in range(ATT_GQ)], axis=1)
        csl = slice(c0, c0 + ATT_GQ * HEAD_DIM)
        o_ref[:, csl] = (o * g_ref[:, csl]).astype(BF16)
    kot_ref[0, :, 0:win - ts] = ckt_ref[0, :, ts:win]
    kot_ref[0, :, win - ts:win] = knt_ref[0, :, 0:ts]
    vot_ref[0, :, 0:win - ts] = cvt_ref[0, :, ts:win]
    vot_ref[0, :, win - ts:win] = vnt_ref[0, :, 0:ts]


def _attn_sample(q, knt, vnt, gate, ckt, cvt, ts):
    batch, _, win = ckt.shape
    tp = SAMPLE_PAD
    blk = lambda b: (b, 0)
    per_b = lambda b: (b, 0, 0)
    return pl.pallas_call(
        functools.partial(_attn_sample_kernel, ts=ts),
        grid=(batch,),
        in_specs=[
            pl.BlockSpec((tp, ATT_INNER), blk),
            pl.BlockSpec((1, KV_DIM, tp), per_b),
            pl.BlockSpec((1, KV_DIM, tp), per_b),
            pl.BlockSpec((tp, ATT_INNER), blk),
            pl.BlockSpec((1, KV_DIM, win), per_b),
            pl.BlockSpec((1, KV_DIM, win), per_b),
        ],
        out_specs=[
            pl.BlockSpec((tp, ATT_INNER), blk),
            pl.BlockSpec((1, KV_DIM, win), per_b),
            pl.BlockSpec((1, KV_DIM, win), per_b),
        ],
        out_shape=[
            jax.ShapeDtypeStruct((batch * tp, ATT_INNER), BF16),
            jax.ShapeDtypeStruct((batch, KV_DIM, win), F32),
            jax.ShapeDtypeStruct((batch, KV_DIM, win), F32),
        ],
        scratch_shapes=[pltpu.VMEM((ATT_GQ * tp, win), F32), pltpu.VMEM((ATT_GQ * tp, tp), F32)],
        compiler_params=pltpu.CompilerParams(
            dimension_semantics=("arbitrary",), vmem_limit_bytes=VMEM_LIMIT),
        name="attn_sample",
    )(q, knt, vnt, gate, ckt, cvt)


def _group_norm(yg, nw):
    ms = jnp.mean(yg * yg, axis=-1, keepdims=True)
    return (yg * lax.rsqrt(ms + EPS) * nw).astype(BF16)


def _outproj_kernel(x_ref, ys_ref, yat_ref, nw_ref, w_ref, o_ref):
    acc = x_ref[...]
    for g in range(SSD_GROUPS):
        gs = slice(g * HALF_INNER, (g + 1) * HALF_INNER)
        acc = acc + _dot(_group_norm(ys_ref[:, gs], nw_ref[:, gs]), w_ref[gs, :])
    ya = yat_ref[0].astype(F32).T.astype(BF16)
    o_ref[...] = acc + _dot(ya, w_ref[SSD_INNER:, :])


def _out_proj(x2d, y_ssd, y_att_t, nw, w_o, tm):
    rows = x2d.shape[0]
    per_seq = y_att_t.shape[2] // tm
    row = lambda i: (i, 0)
    const = lambda i: (0, 0)
    return pl.pallas_call(
        _outproj_kernel,
        grid=(rows // tm,),
        in_specs=[
            pl.BlockSpec((tm, D_MODEL), row),
            pl.BlockSpec((tm, SSD_INNER), row),
            pl.BlockSpec((1, ATT_INNER, tm), lambda i: (i // per_seq, 0, i % per_seq)),
            pl.BlockSpec((1, SSD_INNER), const),
            pl.BlockSpec((SSD_INNER + ATT_INNER, D_MODEL), const, pipeline_mode=pl.Buffered(1)),
        ],
        out_specs=pl.BlockSpec((tm, D_MODEL), row),
        out_shape=jax.ShapeDtypeStruct((rows, D_MODEL), F32),
        compiler_params=pltpu.CompilerParams(
            dimension_semantics=("arbitrary",), vmem_limit_bytes=VMEM_LIMIT),
        name="out_proj",
    )(x2d, y_ssd, y_att_t, nw, w_o)


def _wconv_out_kernel(x_ref, ys_ref, ya_ref, nw_ref, w_ref, wb_ref, o_ref):
    j = pl.program_id(0)
    wb = w_ref[...].astype(BF16)
    wb_ref[...] = wb
    y = jnp.where(j < SSD_GROUPS, _group_norm(ys_ref[...], nw_ref[...]), ya_ref[...])
    part = _dot(y, wb)
    bs, ts, _ = x_ref.shape

    @pl.when(j == 0)
    def _():
        for b in range(bs):
            o_ref[b] = x_ref[b] + part[b * SAMPLE_PAD:b * SAMPLE_PAD + ts]

    @pl.when(j > 0)
    def _():
        for b in range(bs):
            o_ref[b] += part[b * SAMPLE_PAD:b * SAMPLE_PAD + ts]


def _wconv_out(x_s, y_ssd, y_att, nw, w_f32):
    rows = x_s.shape[0] * SAMPLE_PAD
    n_ssd = SSD_GROUPS
    n_chunks = (SSD_INNER + ATT_INNER) // WO_CHUNK
    const = lambda j: (0, 0)
    ssd_chunk = lambda j: (0, jnp.minimum(j, n_ssd - 1))
    return pl.pallas_call(
        _wconv_out_kernel,
        grid=(n_chunks,),
        in_specs=[
            pl.BlockSpec(x_s.shape, lambda j: (0, 0, 0)),
            pl.BlockSpec((rows, WO_CHUNK), ssd_chunk),
            pl.BlockSpec((rows, WO_CHUNK), lambda j: (0, jnp.maximum(j - n_ssd, 0))),
            pl.BlockSpec((1, WO_CHUNK), ssd_chunk),
            pl.BlockSpec((WO_CHUNK, D_MODEL), lambda j: (j, 0)),
        ],
        out_specs=[
            pl.BlockSpec((WO_CHUNK, D_MODEL), lambda j: (j, 0)),
            pl.BlockSpec(x_s.shape, lambda j: (0, 0, 0)),
        ],
        out_shape=[
            jax.ShapeDtypeStruct((SSD_INNER + ATT_INNER, D_MODEL), BF16),
            jax.ShapeDtypeStruct(x_s.shape, F32),
        ],
        compiler_params=pltpu.CompilerParams(
            dimension_semantics=("arbitrary",), vmem_limit_bytes=VMEM_LIMIT),
        name="wconv_out",
    )(x_s, y_ssd, y_att, nw, w_f32)


def _rope_tables(pos):
    n = pos.shape[0]
    half = ROT_DIM // 2
    inv = ROPE_THETA ** (-np.arange(0, ROT_DIM, 2, dtype=np.float64) / ROT_DIM)
    ang = pos.astype(np.float64)[:, None] * inv[None, :]
    cos, sin = np.cos(ang), np.sin(ang)
    rest = HEAD_DIM - ROT_DIM
    zh = np.zeros((n, half))
    cos_h = np.concatenate([cos, cos, np.ones((n, rest))], axis=1)
    sa_h = np.concatenate([zh, sin, np.zeros((n, rest))], axis=1)
    sb_h = np.concatenate([-sin, zh, np.zeros((n, rest))], axis=1)
    rep = LANES // HEAD_DIM
    return tuple(jnp.asarray(np.tile(t, (1, rep)), dtype=F32) for t in (cos_h, sa_h, sb_h))


def _lane_pad(v, n=LANES):
    return jnp.pad(v, (0, n - v.shape[0])).reshape(1, n)


def _to_cache(xt, batch, seq):
    return xt.reshape(1, batch, ATT_KV_HEADS, HEAD_DIM, seq).transpose(0, 1, 4, 2, 3)


def kernel(x_prompt, x_sample, cache_k, cache_v, state_conv, state_ssm, norm_w, w_in, conv_w,
           conv_b, dt_bias, a_log, d_skip, ssd_norm_w, q_norm_w, k_norm_w, w_out):
    bp, tp_, _ = x_prompt.shape
    bs, ts, _ = x_sample.shape
    depth = w_in.shape[0]
    assert depth == 1 and tp_ % SSD_CHUNK == 0 and ts <= SAMPLE_PAD and ts >= CONV_W - 1
    l = 0
    win = cache_k.shape[2]

    nw = norm_w[l].reshape(1, D_MODEL)
    rep = LANES // HEAD_DIM
    qnw = jnp.tile(q_norm_w[l], rep).reshape(1, LANES)
    knw = jnp.tile(k_norm_w[l], rep).reshape(1, LANES)
    cw = jnp.pad(conv_w[l], ((0, SUBLANES - CONV_W), (0, 0)))
    cb = conv_b[l].reshape(1, CONV_DIM)
    dtb = _lane_pad(dt_bias[l])
    alog = _lane_pad(a_log[l])
    dexp = jnp.repeat(d_skip[l], SSD_HEADDIM).reshape(1, SSD_INNER)
    snw = ssd_norm_w[l].reshape(1, SSD_INNER)

    pad = SAMPLE_PAD
    tabs_s = _rope_tables(PAST_LEN + np.arange(bs * pad) % pad)
    w_t, proj_s = _wconv_in(x_sample, nw, jnp.swapaxes(w_in[l], 0, 1))
    zs, xbc, dtr, q, kt, vt, gs, _ = _sections(proj_s, qnw, knw, tabs_s)
    cprev = jnp.pad(state_conv[l], ((0, 0), (SUBLANES - (CONV_W - 1), 0), (0, 0)))
    y_ssd, h_s = _ssd(xbc, dtr, zs, cprev, state_ssm[l].reshape(bs, SSD_INNER, SSD_STATE),
                      cw, cb, dtb, alog, dexp, bs, pad, ts)
    knt = kt.reshape(KV_DIM, bs, pad).transpose(1, 0, 2)
    vnt = vt.reshape(KV_DIM, bs, pad).transpose(1, 0, 2)
    ckt = cache_k[l].transpose(0, 2, 3, 1).reshape(bs, KV_DIM, win)
    cvt = cache_v[l].transpose(0, 2, 3, 1).reshape(bs, KV_DIM, win)
    y_att, kot, vot = _attn_sample(q, knt, vnt, gs, ckt, cvt, ts)
    w_o, y_s = _wconv_out(x_sample, y_ssd, y_att, snw, w_out[l])
    k_s = _to_cache(kot, bs, win)
    v_s = _to_cache(vot, bs, win)
    c_s = xbc.reshape(bs, pad, CONV_DIM)[:, ts - (CONV_W - 1):ts][None]
    h_s = h_s.reshape(1, bs, SSD_HEADS, SSD_HEADDIM, SSD_STATE)

    tm = TM_IN
    xp2 = x_prompt.reshape(bp * tp_, D_MODEL)
    tabs = _rope_tables(np.arange(tp_))
    zs, xbc, dtr, q, kt, vt, gs, ctail = _in_proj(xp2, nw, w_t, qnw, knw, tabs, tm, tp_)
    y_ssd, h_p = _ssd(xbc, dtr, zs, jnp.zeros((bp, SUBLANES, CONV_DIM), F32),
                      jnp.zeros((bp, SSD_INNER, SSD_STATE), F32),
                      cw, cb, dtb, alog, dexp, bp, SSD_CHUNK, SSD_CHUNK)
    y_att = _attn_prompt(q, kt, vt, gs, bp, tp_)
    y_p = _out_proj(xp2, y_ssd, y_att, snw, w_o, TM_OUT).reshape(bp, tp_, D_MODEL)
    keep = min(W_MAX, tp_)
    k_p = _to_cache(kt, bp, tp_)[:, :, tp_ - keep:]
    v_p = _to_cache(vt, bp, tp_)[:, :, tp_ - keep:]
    c_p = ctail[:, SUBLANES - (CONV_W - 1):][None]
    h_p = h_p.reshape(1, bp, SSD_HEADS, SSD_HEADDIM, SSD_STATE)

    return (y_p, y_s, k_p, v_p, c_p, h_p, k_s, v_s, c_s, h_s)
```

```python
import functools

import jax
import jax.numpy as jnp
import numpy as np
from jax import lax
from jax.experimental import pallas as pl
from jax.experimental.pallas import tpu as pltpu

F32 = jnp.float32
BF16 = jnp.bfloat16

D_MODEL = 2048
SSD_HEADS = 16
SSD_HEADDIM = 64
SSD_INNER = SSD_HEADS * SSD_HEADDIM
SSD_GROUPS = 2
SSD_STATE = 128
CONV_W = 4
CONV_DIM = SSD_INNER + 2 * SSD_GROUPS * SSD_STATE
SSD_CHUNK = 128
ATT_HEADS = 16
ATT_KV_HEADS = 4
HEAD_DIM = 64
ATT_GQ = ATT_HEADS // ATT_KV_HEADS
ATT_INNER = ATT_HEADS * HEAD_DIM
KV_DIM = ATT_KV_HEADS * HEAD_DIM
ROT_DIM = HEAD_DIM // 4
ROPE_THETA = 500000.0
DILATED_BRANCHES = ((128, 1), (512, 4), (2048, 16))
W_MAX = 2048
PAST_LEN = 16384
EPS = 1e-6

LANES = 128
SUBLANES = 8
Q_BLOCK = 256
K_SUPER = 256
Q_PER_SUPER = K_SUPER // Q_BLOCK
ONES_ROWS = 16
ACC_ROWS = HEAD_DIM + ONES_ROWS
CONV_COLS = 256
TM_IN = 512
TM_OUT = 512
SSD_SEQS = 4
SAMPLE_PAD = 16
NEG = -1e30
VMEM_LIMIT = 58 * 1024 * 1024

Z0 = 0
X0 = Z0 + SSD_INNER
DT0 = X0 + CONV_DIM
Q0 = DT0 + SSD_HEADS
K0 = Q0 + ATT_INNER
V0 = K0 + KV_DIM
G0 = V0 + KV_DIM
W_ROWS = G0 + ATT_INNER
W_CHUNK = 1024
HALF_INNER = SSD_INNER // SSD_GROUPS
WO_CHUNK = HALF_INNER
LOG2E = 1.4426950408889634
Q_SCALE = HEAD_DIM ** -0.5 * LOG2E


def _dot(a, b):
    return jnp.dot(a, b, preferred_element_type=F32)


def _dot_nt(a, b):
    return lax.dot_general(a, b, (((1,), (1,)), ((), ())), preferred_element_type=F32)


def _split3(x):
    hi = x.astype(BF16)
    r1 = x - hi.astype(F32)
    mid = r1.astype(BF16)
    lo = (r1 - mid.astype(F32)).astype(BF16)
    return hi, mid, lo


def _dot_exact_rhs(x, m):
    hi, mid, lo = _split3(x)
    return _dot(hi, m) + _dot(mid, m) + _dot(lo, m)


def _dot_wide_rhs(x, m):
    hi = x.astype(BF16)
    lo = (x - hi.astype(F32)).astype(BF16)
    return _dot(hi, m) + _dot(lo, m)


def _dot_exact_lhs(m, x):
    hi, mid, lo = _split3(x)
    return _dot(m, hi) + _dot(m, mid) + _dot(m, lo)


def _silu(x):
    hx = 0.5 * x
    return hx * jnp.tanh(hx) + hx


def _multiplicity(d):
    w = jnp.zeros(d.shape, F32)
    for window, dil in DILATED_BRANCHES:
        hit = (d >= 0) & (d <= window) & (lax.rem(d, dil) == 0)
        w = w + jnp.where(hit, 1.0, 0.0)
    return w


def _norm_rope(y, nw, cos, sa, sb, scale):
    lane = lax.broadcasted_iota(jnp.int32, (1, LANES), 1)
    first = lane < HEAD_DIM
    y2 = y * y
    s_lo = jnp.sum(jnp.where(first, y2, 0.0), axis=-1, keepdims=True)
    s_hi = jnp.sum(jnp.where(first, 0.0, y2), axis=-1, keepdims=True)
    ms = jnp.where(first, s_lo, s_hi) * (1.0 / HEAD_DIM)
    yn = y * lax.rsqrt(ms + EPS) * nw
    half = ROT_DIM // 2
    rot = yn * cos + pltpu.roll(yn, half, 1) * sa + pltpu.roll(yn, LANES - half, 1) * sb
    return rot * scale


def _causal_conv_silu(load, cw_ref, cb_ref, store):
    assert CONV_W == 4
    for c0 in range(0, CONV_DIM, CONV_COLS):
        cs = slice(c0, c0 + CONV_COLS)
        xe = load(cs)
        x2 = pltpu.roll(xe, 2, 0)
        even = cw_ref[3:4, cs] * xe + cw_ref[1:2, cs] * x2
        odd = cw_ref[2:3, cs] * xe + cw_ref[0:1, cs] * x2
        store(cs, _silu((even + pltpu.roll(odd, 1, 0))[SUBLANES:, :] + cb_ref[:, cs]))


def _pre_norm(x_ref, nw_ref):
    x = x_ref[...]
    ms = jnp.mean(x * x, axis=-1, keepdims=True)
    return (x * lax.rsqrt(ms + EPS) * nw_ref[...]).astype(BF16)


def _inproj_kernel(x_ref, nw_ref, w_ref, *refs):
    hn = _pre_norm(x_ref, nw_ref)

    def proj(a, b):
        return _dot_nt(hn, w_ref[a:b, :])

    _emit_sections(proj, x_ref.shape[0], *refs)


def _emit_sections(proj, tm, qnw_ref, knw_ref, cos_ref, sa_ref, sb_ref,
                   zs_ref, xbc_ref, dt_ref, q_ref, kt_ref, vt_ref, gs_ref, ctail_ref):
    cos, sa, sb = cos_ref[...], sa_ref[...], sb_ref[...]
    q = proj(Q0, K0)
    for c in range(ATT_INNER // LANES):
        sl = slice(c * LANES, (c + 1) * LANES)
        qn = _norm_rope(q[:, sl], qnw_ref[...], cos, sa, sb, Q_SCALE)
        if len(q_ref.shape) == 3:
            q_ref[0, sl, :] = qn.T.astype(BF16)
        else:
            q_ref[:, sl] = qn.astype(BF16)
    k = proj(K0, V0)
    kn = [_norm_rope(k[:, c * LANES:(c + 1) * LANES], knw_ref[...], cos, sa, sb, 1.0)
          for c in range(KV_DIM // LANES)]
    kt_ref[0] = jnp.concatenate(kn, axis=1).T

    xbc = proj(X0, DT0)
    xbc_ref[...] = xbc
    ctail_ref[0] = xbc[tm - SUBLANES:tm, :]
    vt_ref[0] = proj(V0, G0).T
    zs_ref[...] = _silu(proj(Z0, X0))
    gs = _silu(proj(G0, W_ROWS))
    if len(gs_ref.shape) == 3:
        gs_ref[0] = gs.T
    else:
        gs_ref[...] = gs
    dt_ref[...] = proj(DT0, DT0 + LANES)


def _in_proj(x2d, nw, w_t, qnw, knw, tables, tm, seq):
    rows = x2d.shape[0]
    const = lambda i: (0, 0)
    row = lambda i: (i, 0)
    sec_in, out_specs, out_shape = _section_specs(rows, tm, seq, tables[0].shape[0] // tm, True)
    return pl.pallas_call(
        _inproj_kernel,
        grid=(rows // tm,),
        in_specs=[
            pl.BlockSpec((tm, D_MODEL), row),
            pl.BlockSpec((1, D_MODEL), const),
            pl.BlockSpec((W_ROWS, D_MODEL), const, pipeline_mode=pl.Buffered(1)),
        ] + sec_in,
        out_specs=out_specs,
        out_shape=out_shape,
        compiler_params=pltpu.CompilerParams(
            dimension_semantics=("arbitrary",), vmem_limit_bytes=VMEM_LIMIT),
        name="in_proj",
    )(x2d, nw, w_t, qnw, knw, *tables)


def _section_specs(rows, tm, seq, period, q_transposed):
    per_seq = seq // tm
    const = lambda i: (0, 0)
    row = lambda i: (i, 0)
    tab = lambda i: (i % period, 0)
    tr = lambda i: (i // per_seq, 0, i % per_seq)
    slab = lambda i: (i // per_seq, 0, 0)
    in_specs = [pl.BlockSpec((1, LANES), const)] * 2 + [pl.BlockSpec((tm, LANES), tab)] * 3
    row_outs = {0: (SSD_INNER, F32), 1: (CONV_DIM, F32), 2: (LANES, F32), 3: (ATT_INNER, BF16),
                6: (ATT_INNER, F32)}
    out_specs, out_shape = [], []
    for idx in range(7):
        if idx in (3, 6) and q_transposed:
            dt = row_outs[idx][1]
            out_specs.append(pl.BlockSpec((1, ATT_INNER, tm), tr))
            out_shape.append(jax.ShapeDtypeStruct((rows // seq, ATT_INNER, seq), dt))
        elif idx in row_outs:
            n, dt = row_outs[idx]
            out_specs.append(pl.BlockSpec((tm, n), row))
            out_shape.append(jax.ShapeDtypeStruct((rows, n), dt))
        else:
            out_specs.append(pl.BlockSpec((1, KV_DIM, tm), tr))
            out_shape.append(jax.ShapeDtypeStruct((rows // seq, KV_DIM, seq), F32))
    out_specs.append(pl.BlockSpec((1, SUBLANES, CONV_DIM), slab))
    out_shape.append(jax.ShapeDtypeStruct((rows // seq, SUBLANES, CONV_DIM), F32))
    return in_specs, out_specs, out_shape


def _wconv_in_kernel(x_ref, nw_ref, w_ref, wb_ref, p_ref, xpad_scr, hn_scr):
    @pl.when(pl.program_id(0) == 0)
    def _():
        bs, ts, _ = x_ref.shape
        xpad_scr[...] = jnp.zeros(xpad_scr.shape, F32)
        for b in range(bs):
            xpad_scr[b * SAMPLE_PAD:b * SAMPLE_PAD + ts, :] = x_ref[b]
        hn_scr[...] = _pre_norm(xpad_scr, nw_ref)

    row = pl.program_id(0) * W_CHUNK + lax.broadcasted_iota(jnp.int32, (W_CHUNK, 1), 0)
    wb = jnp.where(row < W_ROWS, w_ref[...], 0.0).astype(BF16)
    wb_ref[...] = wb
    p_ref[0] = _dot_nt(hn_scr[...], wb)


def _wconv_in(x_s, nw, w_f32):
    rows = x_s.shape[0] * SAMPLE_PAD
    n_chunks = pl.cdiv(W_ROWS, W_CHUNK)
    const = lambda j: (0, 0)
    return pl.pallas_call(
        _wconv_in_kernel,
        grid=(n_chunks,),
        in_specs=[
            pl.BlockSpec(x_s.shape, lambda j: (0, 0, 0)),
            pl.BlockSpec((1, D_MODEL), const),
            pl.BlockSpec((W_CHUNK, D_MODEL), lambda j: (j, 0)),
        ],
        out_specs=[
            pl.BlockSpec((W_CHUNK, D_MODEL), lambda j: (j, 0)),
            pl.BlockSpec((1, rows, W_CHUNK), lambda j: (j, 0, 0)),
        ],
        out_shape=[
            jax.ShapeDtypeStruct((W_ROWS, D_MODEL), BF16),
            jax.ShapeDtypeStruct((n_chunks, rows, W_CHUNK), F32),
        ],
        scratch_shapes=[pltpu.VMEM((rows, D_MODEL), F32), pltpu.VMEM((rows, D_MODEL), BF16)],
        compiler_params=pltpu.CompilerParams(
            dimension_semantics=("arbitrary",), vmem_limit_bytes=VMEM_LIMIT),
        name="wconv_in",
    )(x_s, nw, w_f32)


def _sections_kernel(p_ref, *refs):
    *refs, tail_scr = refs
    pfull = jnp.concatenate([p_ref[j] for j in range(p_ref.shape[0])], axis=1)
    tail_scr[...] = pfull[:, Q0:W_ROWS]

    def proj(a, b):
        if a >= Q0:
            return tail_scr[:, a - Q0:b - Q0]
        return pfull[:, a:b]

    _emit_sections(proj, p_ref.shape[1], *refs)


def _sections(p, qnw, knw, tables):
    rows = p.shape[1]
    sec_in, out_specs, out_shape = _section_specs(rows, rows, rows, 1, False)
    return pl.pallas_call(
        _sections_kernel,
        grid=(1,),
        in_specs=[pl.BlockSpec(p.shape, lambda i: (0, 0, 0))] + sec_in,
        out_specs=out_specs,
        out_shape=out_shape,
        scratch_shapes=[pltpu.VMEM((rows, W_ROWS - Q0), F32)],
        compiler_params=pltpu.CompilerParams(
            dimension_semantics=("arbitrary",), vmem_limit_bytes=VMEM_LIMIT),
        name="sections",
    )(p, qnw, knw, *tables)


def _ssd_chain(s, xbc_ref, dtr_ref, z_ref, cprev_ref, h0_ref, cw_ref, cb_ref, dtb_ref, alog_ref,
               dexp_ref, y_ref, hout_ref, xext, ht, xc_scr, lin, valid):
    L = SSD_CHUNK
    c = pl.program_id(1)
    last = pl.num_programs(1) - 1

    @pl.when(c == 0)
    def _():
        xext[s, 0:SUBLANES, :] = cprev_ref[s]
        for g in range(SSD_GROUPS):
            ht[s, g] = h0_ref[s, g * HALF_INNER:(g + 1) * HALF_INNER, :].T

    def pad_rows(v):
        if lin == L:
            return v
        return jnp.concatenate([v, jnp.zeros((L - lin, v.shape[1]), v.dtype)], axis=0)

    lane = lax.broadcasted_iota(jnp.int32, (1, LANES), 1)
    rowid = lax.broadcasted_iota(jnp.int32, (L, 1), 0)
    dt = jax.nn.softplus(pad_rows(dtr_ref[s]) + dtb_ref[...])
    dt = jnp.where((lane < SSD_HEADS) & (rowid < valid), dt, 0.0)
    dta = dt * (-jnp.exp(alog_ref[...]) * LOG2E)

    r2 = lax.broadcasted_iota(jnp.int32, (L, L), 0)
    c2 = lax.broadcasted_iota(jnp.int32, (L, L), 1)
    tri = r2 >= c2
    tri_b = jnp.where(tri, 1.0, 0.0).astype(BF16)
    cum = _dot_exact_lhs(tri_b, dta)
    cum_last = cum[L - 1:L, :]
    yield

    er = lax.broadcasted_iota(jnp.int32, (LANES, SSD_INNER), 0)
    ec = lax.broadcasted_iota(jnp.int32, (LANES, SSD_INNER), 1)
    expand = jnp.where(ec // SSD_HEADDIM == er, 1.0, 0.0).astype(BF16)
    ecum = _dot_wide_rhs(jnp.exp2(cum), expand)
    wexp = _dot_wide_rhs(jnp.exp2(cum_last - cum) * dt, expand)
    cd = _dot_exact_rhs(jnp.broadcast_to(jnp.exp2(cum_last), (SUBLANES, LANES)), expand)[0:1, :]
    cum_t = cum.T
    dt_t = dt.T
    yield

    if lin < L:
        xext[s, SUBLANES + lin:SUBLANES + L, :] = jnp.zeros((L - lin, CONV_DIM), F32)
    xext[s, SUBLANES:SUBLANES + lin, :] = xbc_ref[s]

    def load(cs):
        return xext[s, 0:SUBLANES + L, cs]

    def store(cs, v):
        xc_scr[s, :, cs] = v

    _causal_conv_silu(load, cw_ref, cb_ref, store)
    xext[s, 0:SUBLANES, :] = xext[s, L:L + SUBLANES, :]
    xc = xc_scr.at[s]
    xs = xc[:, :SSD_INNER]
    bm = xc[:, SSD_INNER:SSD_INNER + SSD_GROUPS * SSD_STATE]
    cm = xc[:, SSD_INNER + SSD_GROUPS * SSD_STATE:]
    bm_b = bm.astype(BF16)
    cm_b = cm.astype(BF16)
    xs_b = xs.astype(BF16)

    y_off = jnp.concatenate(
        [_dot(cm_b[:, g * SSD_STATE:(g + 1) * SSD_STATE], ht[s, g].astype(BF16))
         for g in range(SSD_GROUPS)], axis=1) * ecum
    cbs = [_dot_nt(cm_b[:, g * SSD_STATE:(g + 1) * SSD_STATE], bm_b[:, g * SSD_STATE:(g + 1) * SSD_STATE])
           for g in range(SSD_GROUPS)]
    yield

    lane_half = lane // HEAD_DIM
    heads_per_group = SSD_HEADS // SSD_GROUPS
    y_parts = []
    for g in range(SSD_GROUPS):
        for pr in range(heads_per_group // 2):
            col0 = g * HALF_INNER + pr * LANES
            xp = xs_b[:, col0:col0 + LANES]
            yp = jnp.zeros((L, LANES), F32)
            for e in range(2):
                hh = g * heads_per_group + pr * 2 + e
                decay = jnp.exp2(cum[:, hh:hh + 1] - cum_t[hh:hh + 1, :])
                sc = jnp.where(tri, cbs[g] * decay, 0.0) * dt_t[hh:hh + 1, :]
                xm = jnp.where(lane_half == e, xp, jnp.zeros_like(xp))
                yp = yp + _dot(sc.astype(BF16), xm)
            y_parts.append(yp)
            if pr % 2 == 1:
                yield
    y_diag = jnp.concatenate(y_parts, axis=1)

    xw = (xs * wexp).astype(BF16)
    for g in range(SSD_GROUPS):
        sl = slice(g * HALF_INNER, (g + 1) * HALF_INNER)
        bm_t = bm[:, g * SSD_STATE:(g + 1) * SSD_STATE].T.astype(BF16)
        ht[s, g] = ht[s, g] * cd[:, sl] + _dot(bm_t, xw[:, sl])
    yield

    y = (y_diag + y_off + dexp_ref[...] * xs) * pad_rows(z_ref[s])
    y_ref[s] = y[:lin]

    @pl.when(c == last)
    def _():
        for g in range(SSD_GROUPS):
            hout_ref[s, g * HALF_INNER:(g + 1) * HALF_INNER, :] = ht[s, g].T


def _ssd_kernel(*refs, lin, valid):
    chains = [_ssd_chain(s, *refs, lin, valid) for s in range(SSD_SEQS)]
    while chains:
        alive = []
        for ch in chains:
            try:
                next(ch)
                alive.append(ch)
            except StopIteration:
                pass
        chains = alive


def _ssd(xbc, dtr, z, cprev, h0, cw, cb, dtb, alog, dexp, batch, lin, valid):
    rows = xbc.shape[0]
    seq = rows // batch
    nc = seq // lin
    g = SSD_SEQS
    assert batch % g == 0
    blk = lambda b, c: (b, c, 0)
    per_b = lambda b, c: (b, 0, 0)
    const = lambda b, c: (0, 0)
    y, hout = pl.pallas_call(
        functools.partial(_ssd_kernel, lin=lin, valid=valid),
        grid=(batch // g, nc),
        in_specs=[
            pl.BlockSpec((g, lin, CONV_DIM), blk),
            pl.BlockSpec((g, lin, LANES), blk),
            pl.BlockSpec((g, lin, SSD_INNER), blk),
            pl.BlockSpec((g, SUBLANES, CONV_DIM), per_b),
            pl.BlockSpec((g, SSD_INNER, SSD_STATE), per_b),
            pl.BlockSpec((SUBLANES, CONV_DIM), const),
            pl.BlockSpec((1, CONV_DIM), const),
            pl.BlockSpec((1, LANES), const),
            pl.BlockSpec((1, LANES), const),
            pl.BlockSpec((1, SSD_INNER), const),
        ],
        out_specs=[
            pl.BlockSpec((g, lin, SSD_INNER), blk),
            pl.BlockSpec((g, SSD_INNER, SSD_STATE), per_b),
        ],
        out_shape=[
            jax.ShapeDtypeStruct((batch, seq, SSD_INNER), F32),
            jax.ShapeDtypeStruct((batch, SSD_INNER, SSD_STATE), F32),
        ],
        scratch_shapes=[
            pltpu.VMEM((g, SUBLANES + SSD_CHUNK + SUBLANES, CONV_DIM), F32),
            pltpu.VMEM((g, SSD_GROUPS, SSD_STATE, HALF_INNER), F32),
            pltpu.VMEM((g, SSD_CHUNK, CONV_DIM), F32),
        ],
        compiler_params=pltpu.CompilerParams(
            dimension_semantics=("arbitrary", "arbitrary"), vmem_limit_bytes=VMEM_LIMIT),
        name="ssd",
    )(xbc.reshape(batch, seq, CONV_DIM), dtr.reshape(batch, seq, LANES),
      z.reshape(batch, seq, SSD_INNER), cprev, h0, cw, cb, dtb, alog, dexp)
    return y.reshape(rows, SSD_INNER), hout


def _attn_prompt_kernel(q_ref, kt_ref, vt_ref, g_ref, o_ref,
                        bias_scr, kh_scr, va_scr, qt_scr, m_scr, acc_scr, s_scr, s2_scr, *, n_tab, n_sb_max):
    b = pl.program_id(0)
    i = pl.program_id(1)
    seq = kt_ref.shape[2]
    pair_w = 2 * Q_BLOCK
    pairs_per_kv = ATT_GQ // 2

    @pl.when((b == 0) & (i == 0))
    def _():
        r = lax.broadcasted_iota(jnp.int32, (K_SUPER, Q_BLOCK), 0)
        c = lax.broadcasted_iota(jnp.int32, (K_SUPER, Q_BLOCK), 1)
        for tb in range(n_tab):
            w = _multiplicity(tb * Q_BLOCK + c - r)
            bias_scr[tb] = jnp.where(w > 0.0, jnp.log2(jnp.maximum(w, 1.0)), NEG)
        va_scr[:, HEAD_DIM:, :] = jnp.ones((ATT_KV_HEADS, ONES_ROWS, seq), BF16)

    @pl.when(i == 0)
    def _():
        for kvh in range(ATT_KV_HEADS):
            hs = slice(kvh * HEAD_DIM, (kvh + 1) * HEAD_DIM)
            va_scr[kvh, 0:HEAD_DIM, :] = vt_ref[0, hs, :].astype(BF16)
        for cidx in range(seq // K_SUPER):
            rows = slice(cidx * K_SUPER, (cidx + 1) * K_SUPER)
            kc = kt_ref[0, :, rows].T
            for kvh in range(ATT_KV_HEADS):
                hs = slice(kvh * HEAD_DIM, (kvh + 1) * HEAD_DIM)
                kh_scr[kvh, rows, :] = kc[:, hs].astype(BF16)

    for kvh in range(ATT_KV_HEADS):
        qt_scr[kvh] = jnp.concatenate(
            [q_ref[0, (kvh * ATT_GQ + g) * HEAD_DIM:(kvh * ATT_GQ + g + 1) * HEAD_DIM, :]
             for g in range(ATT_GQ)], axis=1)
    m_scr[...] = jnp.full(m_scr.shape, NEG, F32)
    acc_scr[...] = jnp.zeros(acc_scr.shape, F32)
    par = i % Q_PER_SUPER
    j_last = i // Q_PER_SUPER
    n_sb = jnp.minimum(j_last + 1, n_sb_max)

    def key_start(dl):
        return pl.multiple_of((j_last - dl) * K_SUPER, K_SUPER)

    def scores(dl, dst):
        for kvh in range(ATT_KV_HEADS):
            dst[kvh] = _dot(kh_scr[kvh, pl.ds(key_start(dl), K_SUPER), :], qt_scr[kvh])

    def softmax_pv(dl, src):
        start = key_start(dl)
        bias = bias_scr[par + Q_PER_SUPER * dl]
        bias2 = jnp.concatenate([bias, bias], axis=1)
        for kvh in range(ATT_KV_HEADS):
            vtb = va_scr[kvh, :, pl.ds(start, K_SUPER)]
            for pr in range(pairs_per_kv):
                u = kvh * pairs_per_kv + pr
                s = src[kvh, :, pr * pair_w:(pr + 1) * pair_w] + bias2
                m_old = m_scr[u, 0:1, :]
                m_new = jnp.maximum(m_old, jnp.max(s, axis=0, keepdims=True))
                p = jnp.exp2(s - m_new).astype(BF16)
                acc_scr[u] = jnp.exp2(m_old - m_new) * acc_scr[u] + _dot(vtb, p)
                m_scr[u, 0:1, :] = m_new

    def diagonal_block():
        half = K_SUPER // 2
        start = key_start(0)
        bias = bias_scr[0]
        for kvh in range(ATT_KV_HEADS):
            s_scr[kvh, 0:half, :] = _dot(kh_scr[kvh, pl.ds(start, half), :], qt_scr[kvh])
        for kvh in range(ATT_KV_HEADS):
            q_late = jnp.concatenate(
                [qt_scr[kvh, :, g * Q_BLOCK + half:(g + 1) * Q_BLOCK] for g in range(ATT_GQ)], axis=1)
            s2_scr[kvh, 0:half, 0:ATT_GQ * half] = _dot(
                kh_scr[kvh, pl.ds(start + half, half), :], q_late)
        bias_a = jnp.concatenate([bias[0:half, :]] * 2, axis=1)
        for kvh in range(ATT_KV_HEADS):
            vtb = va_scr[kvh, :, pl.ds(start, half)]
            for pr in range(pairs_per_kv):
                u = kvh * pairs_per_kv + pr
                s = s_scr[kvh, 0:half, pr * pair_w:(pr + 1) * pair_w] + bias_a
                m_old = m_scr[u, 0:1, :]
                m_new = jnp.maximum(m_old, jnp.max(s, axis=0, keepdims=True))
                p = jnp.exp2(s - m_new).astype(BF16)
                acc_scr[u] = jnp.exp2(m_old - m_new) * acc_scr[u] + _dot(vtb, p)
                m_scr[u, 0:1, :] = m_new
        bias_b = jnp.concatenate([bias[half:, half:]] * 2, axis=1)
        late = [slice(e * Q_BLOCK + half, (e + 1) * Q_BLOCK) for e in range(2)]
        for kvh in range(ATT_KV_HEADS):
            vtb = va_scr[kvh, :, pl.ds(start + half, half)]
            for pr in range(pairs_per_kv):
                u = kvh * pairs_per_kv + pr
                s = s2_scr[kvh, 0:half, pr * Q_BLOCK:(pr + 1) * Q_BLOCK] + bias_b
                m_old = jnp.concatenate([m_scr[u, 0:1, sl] for sl in late], axis=1)
                m_new = jnp.maximum(m_old, jnp.max(s, axis=0, keepdims=True))
                p = jnp.exp2(s - m_new).astype(BF16)
                alpha = jnp.exp2(m_old - m_new)
                pv = _dot(vtb, p)
                for e, sl in enumerate(late):
                    es = slice(e * half, (e + 1) * half)
                    acc_scr[u, :, sl] = alpha[:, es] * acc_scr[u, :, sl] + pv[:, es]
                    m_scr[u, 0:1, sl] = m_new[:, es]

    diagonal_block()
    n_rest = n_sb - 1

    def body(t, carry):
        scores(1 + 2 * t, s_scr)
        scores(2 + 2 * t, s2_scr)
        softmax_pv(1 + 2 * t, s_scr)
        softmax_pv(2 + 2 * t, s2_scr)
        return carry

    lax.fori_loop(0, n_rest // 2, body, 0)

    @pl.when(n_rest % 2 == 1)
    def _():
        scores(n_sb - 1, s_scr)
        softmax_pv(n_sb - 1, s_scr)

    for u in range(ATT_HEADS // 2):
        acc = acc_scr[u]
        o = acc[0:HEAD_DIM] * (1.0 / acc[HEAD_DIM:HEAD_DIM + 1])
        for e in range(2):
            hs = slice((2 * u + e) * HEAD_DIM, (2 * u + e + 1) * HEAD_DIM)
            o_ref[0, hs, :] = (o[:, e * Q_BLOCK:(e + 1) * Q_BLOCK] * g_ref[0, hs, :]).astype(BF16)


def _attn_prompt(qt, kt, vt, gate_t, batch, seq):
    nqb = seq // Q_BLOCK
    assert seq % K_SUPER == 0 and Q_BLOCK == K_SUPER
    n_sb_max = min(seq // K_SUPER, W_MAX // K_SUPER + 1)
    n_tab = Q_PER_SUPER * n_sb_max
    qblk = lambda b, i: (b, 0, i)
    per_b = lambda b, i: (b, 0, 0)
    return pl.pallas_call(
        functools.partial(_attn_prompt_kernel, n_tab=n_tab, n_sb_max=n_sb_max),
        grid=(batch, nqb),
        in_specs=[
            pl.BlockSpec((1, ATT_INNER, Q_BLOCK), qblk),
            pl.BlockSpec((1, KV_DIM, seq), per_b),
            pl.BlockSpec((1, KV_DIM, seq), per_b),
            pl.BlockSpec((1, ATT_INNER, Q_BLOCK), qblk),
        ],
        out_specs=pl.BlockSpec((1, ATT_INNER, Q_BLOCK), qblk),
        out_shape=jax.ShapeDtypeStruct((batch, ATT_INNER, seq), BF16),
        scratch_shapes=[
            pltpu.VMEM((n_tab, K_SUPER, Q_BLOCK), F32),
            pltpu.VMEM((ATT_KV_HEADS, seq, HEAD_DIM), BF16),
            pltpu.VMEM((ATT_KV_HEADS, ACC_ROWS, seq), BF16),
            pltpu.VMEM((ATT_KV_HEADS, HEAD_DIM, ATT_GQ * Q_BLOCK), BF16),
            pltpu.VMEM((ATT_HEADS // 2, SUBLANES, 2 * Q_BLOCK), F32),
            pltpu.VMEM((ATT_HEADS // 2, ACC_ROWS, 2 * Q_BLOCK), F32),
            pltpu.VMEM((ATT_KV_HEADS, K_SUPER, ATT_GQ * Q_BLOCK), F32),
            pltpu.VMEM((ATT_KV_HEADS, K_SUPER, ATT_GQ * Q_BLOCK), F32),
        ],
        compiler_params=pltpu.CompilerParams(
            dimension_semantics=("arbitrary", "arbitrary"), vmem_limit_bytes=VMEM_LIMIT),
        name="attn_prompt",
    )(qt, kt, vt, gate_t)


def _attn_sample_kernel(q_ref, knt_ref, vnt_ref, g_ref, ckt_ref, cvt_ref, o_ref, kot_ref, vot_ref,
                        bc_scr, bn_scr, *, ts):
    win = ckt_ref.shape[2]
    tp = SAMPLE_PAD
    rows = ATT_GQ * tp

    @pl.when(pl.program_id(0) == 0)
    def _():
        t_c = lax.broadcasted_iota(jnp.int32, (rows, win), 0) % tp
        j_c = lax.broadcasted_iota(jnp.int32, (rows, win), 1)
        w_c = _multiplicity(win + t_c - j_c)
        bc_scr[...] = jnp.where(w_c > 0.0, jnp.log2(jnp.maximum(w_c, 1.0)), NEG)
        t_n = lax.broadcasted_iota(jnp.int32, (rows, tp), 0) % tp
        j_n = lax.broadcasted_iota(jnp.int32, (rows, tp), 1)
        w_n = jnp.where(j_n < ts, _multiplicity(t_n - j_n), 0.0)
        bn_scr[...] = jnp.where(w_n > 0.0, jnp.log2(jnp.maximum(w_n, 1.0)), NEG)

    for kvh in range(ATT_KV_HEADS):
        c0 = kvh * ATT_GQ * HEAD_DIM
        qh = jnp.concatenate(
            [q_ref[:, c0 + g * HEAD_DIM:c0 + (g + 1) * HEAD_DIM] for g in range(ATT_GQ)], axis=0)
        hs = slice(kvh * HEAD_DIM, (kvh + 1) * HEAD_DIM)
        s_c = _dot(qh, ckt_ref[0, hs, :].astype(BF16)) + bc_scr[...]
        s_n = _dot(qh, knt_ref[0, hs, :].astype(BF16)) + bn_scr[...]
        m = jnp.maximum(jnp.max(s_c, axis=-1, keepdims=True), jnp.max(s_n, axis=-1, keepdims=True))
        p_c = jnp.exp2(s_c - m)
        p_n = jnp.exp2(s_n - m)
        den = jnp.sum(p_c, axis=-1, keepdims=True) + jnp.sum(p_n, axis=-1, keepdims=True)
        num = (_dot_nt(p_c.astype(BF16), cvt_ref[0, hs, :].astype(BF16))
               + _dot_nt(p_n.astype(BF16), vnt_ref[0, hs, :].astype(BF16)))
        o = num / den
        o = jnp.concatenate([o[g * tp:(g + 1) * tp] for g in range(ATT_GQ)], axis=1)
        csl = slice(c0, c0 + ATT_GQ * HEAD_DIM)
        o_ref[:, csl] = (o * g_ref[:, csl]).astype(BF16)
    kot_ref[0, :, 0:win - ts] = ckt_ref[0, :, ts:win]
    kot_ref[0, :, win - ts:win] = knt_ref[0, :, 0:ts]
    vot_ref[0, :, 0:win - ts] = cvt_ref[0, :, ts:win]
    vot_ref[0, :, win - ts:win] = vnt_ref[0, :, 0:ts]


def _attn_sample(q, knt, vnt, gate, ckt, cvt, ts):
    batch, _, win = ckt.shape
    tp = SAMPLE_PAD
    blk = lambda b: (b, 0)
    per_b = lambda b: (b, 0, 0)
    return pl.pallas_call(
        functools.partial(_attn_sample_kernel, ts=ts),
        grid=(batch,),
        in_specs=[
            pl.BlockSpec((tp, ATT_INNER), blk),
            pl.BlockSpec((1, KV_DIM, tp), per_b),
            pl.BlockSpec((1, KV_DIM, tp), per_b),
            pl.BlockSpec((tp, ATT_INNER), blk),
            pl.BlockSpec((1, KV_DIM, win), per_b),
            pl.BlockSpec((1, KV_DIM, win), per_b),
        ],
        out_specs=[
            pl.BlockSpec((tp, ATT_INNER), blk),
            pl.BlockSpec((1, KV_DIM, win), per_b),
            pl.BlockSpec((1, KV_DIM, win), per_b),
        ],
        out_shape=[
            jax.ShapeDtypeStruct((batch * tp, ATT_INNER), BF16),
            jax.ShapeDtypeStruct((batch, KV_DIM, win), F32),
            jax.ShapeDtypeStruct((batch, KV_DIM, win), F32),
        ],
        scratch_shapes=[pltpu.VMEM((ATT_GQ * tp, win), F32), pltpu.VMEM((ATT_GQ * tp, tp), F32)],
        compiler_params=pltpu.CompilerParams(
            dimension_semantics=("arbitrary",), vmem_limit_bytes=VMEM_LIMIT),
        name="attn_sample",
    )(q, knt, vnt, gate, ckt, cvt)


def _group_norm(yg, nw):
    ms = jnp.mean(yg * yg, axis=-1, keepdims=True)
    return (yg * lax.rsqrt(ms + EPS) * nw).astype(BF16)


def _outproj_kernel(x_ref, ys_ref, yat_ref, nw_ref, w_ref, o_ref):
    acc = x_ref[...]
    for g in range(SSD_GROUPS):
        gs = slice(g * HALF_INNER, (g + 1) * HALF_INNER)
        acc = acc + _dot(_group_norm(ys_ref[:, gs], nw_ref[:, gs]), w_ref[gs, :])
    ya = yat_ref[0].astype(F32).T.astype(BF16)
    o_ref[...] = acc + _dot(ya, w_ref[SSD_INNER:, :])


def _out_proj(x2d, y_ssd, y_att_t, nw, w_o, tm):
    rows = x2d.shape[0]
    per_seq = y_att_t.shape[2] // tm
    row = lambda i: (i, 0)
    const = lambda i: (0, 0)
    return pl.pallas_call(
        _outproj_kernel,
        grid=(rows // tm,),
        in_specs=[
            pl.BlockSpec((tm, D_MODEL), row),
            pl.BlockSpec((tm, SSD_INNER), row),
            pl.BlockSpec((1, ATT_INNER, tm), lambda i: (i // per_seq, 0, i % per_seq)),
            pl.BlockSpec((1, SSD_INNER), const),
            pl.BlockSpec((SSD_INNER + ATT_INNER, D_MODEL), const, pipeline_mode=pl.Buffered(1)),
        ],
        out_specs=pl.BlockSpec((tm, D_MODEL), row),
        out_shape=jax.ShapeDtypeStruct((rows, D_MODEL), F32),
        compiler_params=pltpu.CompilerParams(
            dimension_semantics=("arbitrary",), vmem_limit_bytes=VMEM_LIMIT),
        name="out_proj",
    )(x2d, y_ssd, y_att_t, nw, w_o)


def _wconv_out_kernel(x_ref, ys_ref, ya_ref, nw_ref, w_ref, wb_ref, o_ref):
    j = pl.program_id(0)
    wb = w_ref[...].astype(BF16)
    wb_ref[...] = wb
    y = jnp.where(j < SSD_GROUPS, _group_norm(ys_ref[...], nw_ref[...]), ya_ref[...])
    part = _dot(y, wb)
    bs, ts, _ = x_ref.shape

    @pl.when(j == 0)
    def _():
        for b in range(bs):
            o_ref[b] = x_ref[b] + part[b * SAMPLE_PAD:b * SAMPLE_PAD + ts]

    @pl.when(j > 0)
    def _():
        for b in range(bs):
            o_ref[b] += part[b * SAMPLE_PAD:b * SAMPLE_PAD + ts]


def _wconv_out(x_s, y_ssd, y_att, nw, w_f32):
    rows = x_s.shape[0] * SAMPLE_PAD
    n_ssd = SSD_GROUPS
    n_chunks = (SSD_INNER + ATT_INNER) // WO_CHUNK
    const = lambda j: (0, 0)
    ssd_chunk = lambda j: (0, jnp.minimum(j, n_ssd - 1))
    return pl.pallas_call(
        _wconv_out_kernel,
        grid=(n_chunks,),
        in_specs=[
            pl.BlockSpec(x_s.shape, lambda j: (0, 0, 0)),
            pl.BlockSpec((rows, WO_CHUNK), ssd_chunk),
            pl.BlockSpec((rows, WO_CHUNK), lambda j: (0, jnp.maximum(j - n_ssd, 0))),
            pl.BlockSpec((1, WO_CHUNK), ssd_chunk),
            pl.BlockSpec((WO_CHUNK, D_MODEL), lambda j: (j, 0)),
        ],
        out_specs=[
            pl.BlockSpec((WO_CHUNK, D_MODEL), lambda j: (j, 0)),
            pl.BlockSpec(x_s.shape, lambda j: (0, 0, 0)),
        ],
        out_shape=[
            jax.ShapeDtypeStruct((SSD_INNER + ATT_INNER, D_MODEL), BF16),
            jax.ShapeDtypeStruct(x_s.shape, F32),
        ],
        compiler_params=pltpu.CompilerParams(
            dimension_semantics=("arbitrary",), vmem_limit_bytes=VMEM_LIMIT),
        name="wconv_out",
    )(x_s, y_ssd, y_att, nw, w_f32)


def _rope_tables(pos):
    n = pos.shape[0]
    half = ROT_DIM // 2
    inv = ROPE_THETA ** (-np.arange(0, ROT_DIM, 2, dtype=np.float64) / ROT_DIM)
    ang = pos.astype(np.float64)[:, None] * inv[None, :]
    cos, sin = np.cos(ang), np.sin(ang)
    rest = HEAD_DIM - ROT_DIM
    zh = np.zeros((n, half))
    cos_h = np.concatenate([cos, cos, np.ones((n, rest))], axis=1)
    sa_h = np.concatenate([zh, sin, np.zeros((n, rest))], axis=1)
    sb_h = np.concatenate([-sin, zh, np.zeros((n, rest))], axis=1)
    rep = LANES // HEAD_DIM
    return tuple(jnp.asarray(np.tile(t, (1, rep)), dtype=F32) for t in (cos_h, sa_h, sb_h))


def _lane_pad(v, n=LANES):
    return jnp.pad(v, (0, n - v.shape[0])).reshape(1, n)


def _to_cache(xt, batch, seq):
    return xt.reshape(1, batch, ATT_KV_HEADS, HEAD_DIM, seq).transpose(0, 1, 4, 2, 3)


def kernel(x_prompt, x_sample, cache_k, cache_v, state_conv, state_ssm, norm_w, w_in, conv_w,
           conv_b, dt_bias, a_log, d_skip, ssd_norm_w, q_norm_w, k_norm_w, w_out):
    bp, tp_, _ = x_prompt.shape
    bs, ts, _ = x_sample.shape
    depth = w_in.shape[0]
    assert depth == 1 and tp_ % SSD_CHUNK == 0 and ts <= SAMPLE_PAD and ts >= CONV_W - 1
    l = 0
    win = cache_k.shape[2]

    nw = norm_w[l].reshape(1, D_MODEL)
    rep = LANES // HEAD_DIM
    qnw = jnp.tile(q_norm_w[l], rep).reshape(1, LANES)
    knw = jnp.tile(k_norm_w[l], rep).reshape(1, LANES)
    cw = jnp.pad(conv_w[l], ((0, SUBLANES - CONV_W), (0, 0)))
    cb = conv_b[l].reshape(1, CONV_DIM)
    dtb = _lane_pad(dt_bias[l])
    alog = _lane_pad(a_log[l])
    dexp = jnp.repeat(d_skip[l], SSD_HEADDIM).reshape(1, SSD_INNER)
    snw = ssd_norm_w[l].reshape(1, SSD_INNER)

    pad = SAMPLE_PAD
    tabs_s = _rope_tables(PAST_LEN + np.arange(bs * pad) % pad)
    w_t, proj_s = _wconv_in(x_sample, nw, jnp.swapaxes(w_in[l], 0, 1))
    zs, xbc, dtr, q, kt, vt, gs, _ = _sections(proj_s, qnw, knw, tabs_s)
    cprev = jnp.pad(state_conv[l], ((0, 0), (SUBLANES - (CONV_W - 1), 0), (0, 0)))
    y_ssd, h_s = _ssd(xbc, dtr, zs, cprev, state_ssm[l].reshape(bs, SSD_INNER, SSD_STATE),
                      cw, cb, dtb, alog, dexp, bs, pad, ts)
    knt = kt.reshape(KV_DIM, bs, pad).transpose(1, 0, 2)
    vnt = vt.reshape(KV_DIM, bs, pad).transpose(1, 0, 2)
    ckt = cache_k[l].transpose(0, 2, 3, 1).reshape(bs, KV_DIM, win)
    cvt = cache_v[l].transpose(0, 2, 3, 1).reshape(bs, KV_DIM, win)
    y_att, kot, vot = _attn_sample(q, knt, vnt, gs, ckt, cvt, ts)
    w_o, y_s = _wconv_out(x_sample, y_ssd, y_att, snw, w_out[l])
    k_s = _to_cache(kot, bs, win)
    v_s = _to_cache(vot, bs, win)
    c_s = xbc.reshape(bs, pad, CONV_DIM)[:, ts - (CONV_W - 1):ts][None]
    h_s = h_s.reshape(1, bs, SSD_HEADS, SSD_HEADDIM, SSD_STATE)

    tm = TM_IN
    xp2 = x_prompt.reshape(bp * tp_, D_MODEL)
    tabs = _rope_tables(np.arange(tp_))
    zs, xbc, dtr, q, kt, vt, gs, ctail = _in_proj(xp2, nw, w_t, qnw, knw, tabs, tm, tp_)
    y_ssd, h_p = _ssd(xbc, dtr, zs, jnp.zeros((bp, SUBLANES, CONV_DIM), F32),
                      jnp.zeros((bp, SSD_INNER, SSD_STATE), F32),
                      cw, cb, dtb, alog, dexp, bp, SSD_CHUNK, SSD_CHUNK)
    y_att = _attn_prompt(q, kt, vt, gs, bp, tp_)
    y_p = _out_proj(xp2, y_ssd, y_att, snw, w_o, TM_OUT).reshape(bp, tp_, D_MODEL)
    keep = min(W_MAX, tp_)
    k_p = _to_cache(kt, bp, tp_)[:, :, tp_ - keep:]
    v_p = _to_cache(vt, bp, tp_)[:, :, tp_ - keep:]
    c_p = ctail[:, SUBLANES - (CONV_W - 1):][None]
    h_p = h_p.reshape(1, bp, SSD_HEADS, SSD_HEADDIM, SSD_STATE)

    return (y_p, y_s, k_p, v_p, c_p, h_p, k_s, v_s, c_s, h_s)
```

```python
import functools

import jax
import jax.numpy as jnp
import numpy as np
from jax import lax
from jax.experimental import pallas as pl
from jax.experimental.pallas import tpu as pltpu

F32 = jnp.float32
BF16 = jnp.bfloat16

D_MODEL = 2048
SSD_HEADS = 16
SSD_HEADDIM = 64
SSD_INNER = SSD_HEADS * SSD_HEADDIM
SSD_GROUPS = 2
SSD_STATE = 128
CONV_W = 4
CONV_DIM = SSD_INNER + 2 * SSD_GROUPS * SSD_STATE
SSD_CHUNK = 128
ATT_HEADS = 16
ATT_KV_HEADS = 4
HEAD_DIM = 64
ATT_GQ = ATT_HEADS // ATT_KV_HEADS
ATT_INNER = ATT_HEADS * HEAD_DIM
KV_DIM = ATT_KV_HEADS * HEAD_DIM
ROT_DIM = HEAD_DIM // 4
ROPE_THETA = 500000.0
DILATED_BRANCHES = ((128, 1), (512, 4), (2048, 16))
W_MAX = 2048
PAST_LEN = 16384
EPS = 1e-6

LANES = 128
SUBLANES = 8
Q_BLOCK = 256
K_SUPER = 256
Q_PER_SUPER = K_SUPER // Q_BLOCK
ONES_ROWS = 16
ACC_ROWS = HEAD_DIM + ONES_ROWS
CONV_COLS = 256
TM_IN = 512
TM_OUT = 512
SSD_SEQS = 4
SAMPLE_PAD = 16
NEG = -1e30
VMEM_LIMIT = 58 * 1024 * 1024

Z0 = 0
X0 = Z0 + SSD_INNER
DT0 = X0 + CONV_DIM
Q0 = DT0 + SSD_HEADS
K0 = Q0 + ATT_INNER
V0 = K0 + KV_DIM
G0 = V0 + KV_DIM
W_ROWS = G0 + ATT_INNER
W_CHUNK = 1024
HALF_INNER = SSD_INNER // SSD_GROUPS
WO_CHUNK = HALF_INNER
LOG2E = 1.4426950408889634
Q_SCALE = HEAD_DIM ** -0.5 * LOG2E


def _dot(a, b):
    return jnp.dot(a, b, preferred_element_type=F32)


def _dot_nt(a, b):
    return lax.dot_general(a, b, (((1,), (1,)), ((), ())), preferred_element_type=F32)


def _split3(x):
    hi = x.astype(BF16)
    r1 = x - hi.astype(F32)
    mid = r1.astype(BF16)
    lo = (r1 - mid.astype(F32)).astype(BF16)
    return hi, mid, lo


def _dot_exact_rhs(x, m):
    hi, mid, lo = _split3(x)
    return _dot(hi, m) + _dot(mid, m) + _dot(lo, m)


def _dot_wide_rhs(x, m):
    hi = x.astype(BF16)
    lo = (x - hi.astype(F32)).astype(BF16)
    return _dot(hi, m) + _dot(lo, m)


def _dot_exact_lhs(m, x):
    hi, mid, lo = _split3(x)
    return _dot(m, hi) + _dot(m, mid) + _dot(m, lo)


def _silu(x):
    hx = 0.5 * x
    return hx * jnp.tanh(hx) + hx


def _multiplicity(d):
    w = jnp.zeros(d.shape, F32)
    for window, dil in DILATED_BRANCHES:
        hit = (d >= 0) & (d <= window) & (lax.rem(d, dil) == 0)
        w = w + jnp.where(hit, 1.0, 0.0)
    return w


def _norm_rope(y, nw, cos, sa, sb, scale):
    lane = lax.broadcasted_iota(jnp.int32, (1, LANES), 1)
    first = lane < HEAD_DIM
    y2 = y * y
    s_lo = jnp.sum(jnp.where(first, y2, 0.0), axis=-1, keepdims=True)
    s_hi = jnp.sum(jnp.where(first, 0.0, y2), axis=-1, keepdims=True)
    ms = jnp.where(first, s_lo, s_hi) * (1.0 / HEAD_DIM)
    yn = y * lax.rsqrt(ms + EPS) * nw
    half = ROT_DIM // 2
    rot = yn * cos + pltpu.roll(yn, half, 1) * sa + pltpu.roll(yn, LANES - half, 1) * sb
    return rot * scale


def _causal_conv_silu(load, cw_ref, cb_ref, store):
    assert CONV_W == 4
    for c0 in range(0, CONV_DIM, CONV_COLS):
        cs = slice(c0, c0 + CONV_COLS)
        xe = load(cs)
        x2 = pltpu.roll(xe, 2, 0)
        even = cw_ref[3:4, cs] * xe + cw_ref[1:2, cs] * x2
        odd = cw_ref[2:3, cs] * xe + cw_ref[0:1, cs] * x2
        store(cs, _silu((even + pltpu.roll(odd, 1, 0))[SUBLANES:, :] + cb_ref[:, cs]))


def _pre_norm(x_ref, nw_ref):
    x = x_ref[...]
    ms = jnp.mean(x * x, axis=-1, keepdims=True)
    return (x * lax.rsqrt(ms + EPS) * nw_ref[...]).astype(BF16)


def _inproj_kernel(x_ref, nw_ref, w_ref, *refs):
    hn = _pre_norm(x_ref, nw_ref)

    def proj(a, b):
        return _dot_nt(hn, w_ref[a:b, :])

    _emit_sections(proj, x_ref.shape[0], *refs)


def _emit_sections(proj, tm, qnw_ref, knw_ref, cos_ref, sa_ref, sb_ref,
                   zs_ref, xbc_ref, dt_ref, q_ref, kt_ref, vt_ref, gs_ref, ctail_ref):
    cos, sa, sb = cos_ref[...], sa_ref[...], sb_ref[...]
    q = proj(Q0, K0)
    for c in range(ATT_INNER // LANES):
        sl = slice(c * LANES, (c + 1) * LANES)
        qn = _norm_rope(q[:, sl], qnw_ref[...], cos, sa, sb, Q_SCALE)
        if len(q_ref.shape) == 3:
            q_ref[0, sl, :] = qn.T.astype(BF16)
        else:
            q_ref[:, sl] = qn.astype(BF16)
    k = proj(K0, V0)
    kn = [_norm_rope(k[:, c * LANES:(c + 1) * LANES], knw_ref[...], cos, sa, sb, 1.0)
          for c in range(KV_DIM // LANES)]
    kt_ref[0] = jnp.concatenate(kn, axis=1).T

    xbc = proj(X0, DT0)
    xbc_ref[...] = xbc
    ctail_ref[0] = xbc[tm - SUBLANES:tm, :]
    vt_ref[0] = proj(V0, G0).T
    zs_ref[...] = _silu(proj(Z0, X0))
    gs = _silu(proj(G0, W_ROWS))
    if len(gs_ref.shape) == 3:
        gs_ref[0] = gs.T
    else:
        gs_ref[...] = gs
    dt_ref[...] = proj(DT0, DT0 + LANES)


def _in_proj(x2d, nw, w_t, qnw, knw, tables, tm, seq):
    rows = x2d.shape[0]
    const = lambda i: (0, 0)
    row = lambda i: (i, 0)
    sec_in, out_specs, out_shape = _section_specs(rows, tm, seq, tables[0].shape[0] // tm, True)
    return pl.pallas_call(
        _inproj_kernel,
        grid=(rows // tm,),
        in_specs=[
            pl.BlockSpec((tm, D_MODEL), row),
            pl.BlockSpec((1, D_MODEL), const),
            pl.BlockSpec((W_ROWS, D_MODEL), const, pipeline_mode=pl.Buffered(1)),
        ] + sec_in,
        out_specs=out_specs,
        out_shape=out_shape,
        compiler_params=pltpu.CompilerParams(
            dimension_semantics=("arbitrary",), vmem_limit_bytes=VMEM_LIMIT),
        name="in_proj",
    )(x2d, nw, w_t, qnw, knw, *tables)


def _section_specs(rows, tm, seq, period, q_transposed):
    per_seq = seq // tm
    const = lambda i: (0, 0)
    row = lambda i: (i, 0)
    tab = lambda i: (i % period, 0)
    tr = lambda i: (i // per_seq, 0, i % per_seq)
    slab = lambda i: (i // per_seq, 0, 0)
    in_specs = [pl.BlockSpec((1, LANES), const)] * 2 + [pl.BlockSpec((tm, LANES), tab)] * 3
    row_outs = {0: (SSD_INNER, F32), 1: (CONV_DIM, F32), 2: (LANES, F32), 3: (ATT_INNER, BF16),
                6: (ATT_INNER, F32)}
    out_specs, out_shape = [], []
    for idx in range(7):
        if idx in (3, 6) and q_transposed:
            dt = row_outs[idx][1]
            out_specs.append(pl.BlockSpec((1, ATT_INNER, tm), tr))
            out_shape.append(jax.ShapeDtypeStruct((rows // seq, ATT_INNER, seq), dt))
        elif idx in row_outs:
            n, dt = row_outs[idx]
            out_specs.append(pl.BlockSpec((tm, n), row))
            out_shape.append(jax.ShapeDtypeStruct((rows, n), dt))
        else:
            out_specs.append(pl.BlockSpec((1, KV_DIM, tm), tr))
            out_shape.append(jax.ShapeDtypeStruct((rows // seq, KV_DIM, seq), F32))
    out_specs.append(pl.BlockSpec((1, SUBLANES, CONV_DIM), slab))
    out_shape.append(jax.ShapeDtypeStruct((rows // seq, SUBLANES, CONV_DIM), F32))
    return in_specs, out_specs, out_shape


def _wconv_in_kernel(x_ref, nw_ref, w_ref, wb_ref, p_ref, xpad_scr, hn_scr):
    @pl.when(pl.program_id(0) == 0)
    def _():
        bs, ts, _ = x_ref.shape
        xpad_scr[...] = jnp.zeros(xpad_scr.shape, F32)
        for b in range(bs):
            xpad_scr[b * SAMPLE_PAD:b * SAMPLE_PAD + ts, :] = x_ref[b]
        hn_scr[...] = _pre_norm(xpad_scr, nw_ref)

    row = pl.program_id(0) * W_CHUNK + lax.broadcasted_iota(jnp.int32, (W_CHUNK, 1), 0)
    wb = jnp.where(row < W_ROWS, w_ref[...], 0.0).astype(BF16)
    wb_ref[...] = wb
    p_ref[0] = _dot_nt(hn_scr[...], wb)


def _wconv_in(x_s, nw, w_f32):
    rows = x_s.shape[0] * SAMPLE_PAD
    n_chunks = pl.cdiv(W_ROWS, W_CHUNK)
    const = lambda j: (0, 0)
    return pl.pallas_call(
        _wconv_in_kernel,
        grid=(n_chunks,),
        in_specs=[
            pl.BlockSpec(x_s.shape, lambda j: (0, 0, 0)),
            pl.BlockSpec((1, D_MODEL), const),
            pl.BlockSpec((W_CHUNK, D_MODEL), lambda j: (j, 0)),
        ],
        out_specs=[
            pl.BlockSpec((W_CHUNK, D_MODEL), lambda j: (j, 0)),
            pl.BlockSpec((1, rows, W_CHUNK), lambda j: (j, 0, 0)),
        ],
        out_shape=[
            jax.ShapeDtypeStruct((W_ROWS, D_MODEL), BF16),
            jax.ShapeDtypeStruct((n_chunks, rows, W_CHUNK), F32),
        ],
        scratch_shapes=[pltpu.VMEM((rows, D_MODEL), F32), pltpu.VMEM((rows, D_MODEL), BF16)],
        compiler_params=pltpu.CompilerParams(
            dimension_semantics=("arbitrary",), vmem_limit_bytes=VMEM_LIMIT),
        name="wconv_in",
    )(x_s, nw, w_f32)


def _sections_kernel(p_ref, *refs):
    *refs, tail_scr = refs
    pfull = jnp.concatenate([p_ref[j] for j in range(p_ref.shape[0])], axis=1)
    tail_scr[...] = pfull[:, Q0:W_ROWS]

    def proj(a, b):
        if a >= Q0:
            return tail_scr[:, a - Q0:b - Q0]
        return pfull[:, a:b]

    _emit_sections(proj, p_ref.shape[1], *refs)


def _sections(p, qnw, knw, tables):
    rows = p.shape[1]
    sec_in, out_specs, out_shape = _section_specs(rows, rows, rows, 1, False)
    return pl.pallas_call(
        _sections_kernel,
        grid=(1,),
        in_specs=[pl.BlockSpec(p.shape, lambda i: (0, 0, 0))] + sec_in,
        out_specs=out_specs,
        out_shape=out_shape,
        scratch_shapes=[pltpu.VMEM((rows, W_ROWS - Q0), F32)],
        compiler_params=pltpu.CompilerParams(
            dimension_semantics=("arbitrary",), vmem_limit_bytes=VMEM_LIMIT),
        name="sections",
    )(p, qnw, knw, *tables)


def _ssd_chain(s, xbc_ref, dtr_ref, z_ref, cprev_ref, h0_ref, cw_ref, cb_ref, dtb_ref, alog_ref,
               dexp_ref, y_ref, hout_ref, xext, ht, xc_scr, lin, valid):
    L = SSD_CHUNK
    c = pl.program_id(1)
    last = pl.num_programs(1) - 1

    @pl.when(c == 0)
    def _():
        xext[s, 0:SUBLANES, :] = cprev_ref[s]
        for g in range(SSD_GROUPS):
            ht[s, g] = h0_ref[s, g * HALF_INNER:(g + 1) * HALF_INNER, :].T

    def pad_rows(v):
        if lin == L:
            return v
        return jnp.concatenate([v, jnp.zeros((L - lin, v.shape[1]), v.dtype)], axis=0)

    lane = lax.broadcasted_iota(jnp.int32, (1, LANES), 1)
    rowid = lax.broadcasted_iota(jnp.int32, (L, 1), 0)
    dt = jax.nn.softplus(pad_rows(dtr_ref[s]) + dtb_ref[...])
    dt = jnp.where((lane < SSD_HEADS) & (rowid < valid), dt, 0.0)
    dta = dt * (-jnp.exp(alog_ref[...]) * LOG2E)

    r2 = lax.broadcasted_iota(jnp.int32, (L, L), 0)
    c2 = lax.broadcasted_iota(jnp.int32, (L, L), 1)
    tri = r2 >= c2
    tri_b = jnp.where(tri, 1.0, 0.0).astype(BF16)
    cum = _dot_exact_lhs(tri_b, dta)
    cum_last = cum[L - 1:L, :]
    yield

    er = lax.broadcasted_iota(jnp.int32, (LANES, SSD_INNER), 0)
    ec = lax.broadcasted_iota(jnp.int32, (LANES, SSD_INNER), 1)
    expand = jnp.where(ec // SSD_HEADDIM == er, 1.0, 0.0).astype(BF16)
    ecum = _dot_wide_rhs(jnp.exp2(cum), expand)
    wexp = _dot_wide_rhs(jnp.exp2(cum_last - cum) * dt, expand)
    cd = _dot_exact_rhs(jnp.broadcast_to(jnp.exp2(cum_last), (SUBLANES, LANES)), expand)[0:1, :]
    cum_t = cum.T
    dt_t = dt.T
    yield

    if lin < L:
        xext[s, SUBLANES + lin:SUBLANES + L, :] = jnp.zeros((L - lin, CONV_DIM), F32)
    xext[s, SUBLANES:SUBLANES + lin, :] = xbc_ref[s]

    def load(cs):
        return xext[s, 0:SUBLANES + L, cs]

    def store(cs, v):
        xc_scr[s, :, cs] = v

    _causal_conv_silu(load, cw_ref, cb_ref, store)
    xext[s, 0:SUBLANES, :] = xext[s, L:L + SUBLANES, :]
    xc = xc_scr.at[s]
    xs = xc[:, :SSD_INNER]
    bm = xc[:, SSD_INNER:SSD_INNER + SSD_GROUPS * SSD_STATE]
    cm = xc[:, SSD_INNER + SSD_GROUPS * SSD_STATE:]
    bm_b = bm.astype(BF16)
    cm_b = cm.astype(BF16)
    xs_b = xs.astype(BF16)

    y_off = jnp.concatenate(
        [_dot(cm_b[:, g * SSD_STATE:(g + 1) * SSD_STATE], ht[s, g].astype(BF16))
         for g in range(SSD_GROUPS)], axis=1) * ecum
    cbs = [_dot_nt(cm_b[:, g * SSD_STATE:(g + 1) * SSD_STATE], bm_b[:, g * SSD_STATE:(g + 1) * SSD_STATE])
           for g in range(SSD_GROUPS)]
    yield

    lane_half = lane // HEAD_DIM
    heads_per_group = SSD_HEADS // SSD_GROUPS
    y_parts = []
    for g in range(SSD_GROUPS):
        for pr in range(heads_per_group // 2):
            col0 = g * HALF_INNER + pr * LANES
            xp = xs_b[:, col0:col0 + LANES]
            yp = jnp.zeros((L, LANES), F32)
            for e in range(2):
                hh = g * heads_per_group + pr * 2 + e
                decay = jnp.exp2(cum[:, hh:hh + 1] - cum_t[hh:hh + 1, :])
                sc = jnp.where(tri, cbs[g] * decay, 0.0) * dt_t[hh:hh + 1, :]
                xm = jnp.where(lane_half == e, xp, jnp.zeros_like(xp))
                yp = yp + _dot(sc.astype(BF16), xm)
            y_parts.append(yp)
            if pr % 2 == 1:
                yield
    y_diag = jnp.concatenate(y_parts, axis=1)

    xw = (xs * wexp).astype(BF16)
    for g in range(SSD_GROUPS):
        sl = slice(g * HALF_INNER, (g + 1) * HALF_INNER)
        bm_t = bm[:, g * SSD_STATE:(g + 1) * SSD_STATE].T.astype(BF16)
        ht[s, g] = ht[s, g] * cd[:, sl] + _dot(bm_t, xw[:, sl])
    yield

    y = (y_diag + y_off + dexp_ref[...] * xs) * pad_rows(z_ref[s])
    y_ref[s] = y[:lin]

    @pl.when(c == last)
    def _():
        for g in range(SSD_GROUPS):
            hout_ref[s, g * HALF_INNER:(g + 1) * HALF_INNER, :] = ht[s, g].T


def _ssd_kernel(*refs, lin, valid):
    chains = [_ssd_chain(s, *refs, lin, valid) for s in range(SSD_SEQS)]
    while chains:
        alive = []
        for ch in chains:
            try:
                next(ch)
                alive.append(ch)
            except StopIteration:
                pass
        chains = alive


def _ssd(xbc, dtr, z, cprev, h0, cw, cb, dtb, alog, dexp, batch, lin, valid):
    rows = xbc.shape[0]
    seq = rows // batch
    nc = seq // lin
    g = SSD_SEQS
    assert batch % g == 0
    blk = lambda b, c: (b, c, 0)
    per_b = lambda b, c: (b, 0, 0)
    const = lambda b, c: (0, 0)
    y, hout = pl.pallas_call(
        functools.partial(_ssd_kernel, lin=lin, valid=valid),
        grid=(batch // g, nc),
        in_specs=[
            pl.BlockSpec((g, lin, CONV_DIM), blk),
            pl.BlockSpec((g, lin, LANES), blk),
            pl.BlockSpec((g, lin, SSD_INNER), blk),
            pl.BlockSpec((g, SUBLANES, CONV_DIM), per_b),
            pl.BlockSpec((g, SSD_INNER, SSD_STATE), per_b),
            pl.BlockSpec((SUBLANES, CONV_DIM), const),
            pl.BlockSpec((1, CONV_DIM), const),
            pl.BlockSpec((1, LANES), const),
            pl.BlockSpec((1, LANES), const),
            pl.BlockSpec((1, SSD_INNER), const),
        ],
        out_specs=[
            pl.BlockSpec((g, lin, SSD_INNER), blk),
            pl.BlockSpec((g, SSD_INNER, SSD_STATE), per_b),
        ],
        out_shape=[
            jax.ShapeDtypeStruct((batch, seq, SSD_INNER), F32),
            jax.ShapeDtypeStruct((batch, SSD_INNER, SSD_STATE), F32),
        ],
        scratch_shapes=[
            pltpu.VMEM((g, SUBLANES + SSD_CHUNK + SUBLANES, CONV_DIM), F32),
            pltpu.VMEM((g, SSD_GROUPS, SSD_STATE, HALF_INNER), F32),
            pltpu.VMEM((g, SSD_CHUNK, CONV_DIM), F32),
        ],
        compiler_params=pltpu.CompilerParams(
            dimension_semantics=("arbitrary", "arbitrary"), vmem_limit_bytes=VMEM_LIMIT),
        name="ssd",
    )(xbc.reshape(batch, seq, CONV_DIM), dtr.reshape(batch, seq, LANES),
      z.reshape(batch, seq, SSD_INNER), cprev, h0, cw, cb, dtb, alog, dexp)
    return y.reshape(rows, SSD_INNER), hout


def _attn_prompt_kernel(q_ref, kt_ref, vt_ref, g_ref, o_ref,
                        bias_scr, kh_scr, va_scr, qt_scr, m_scr, acc_scr, s_scr, s2_scr, *, n_tab, n_sb_max):
    b = pl.program_id(0)
    i = pl.program_id(1)
    seq = kt_ref.shape[2]
    pair_w = 2 * Q_BLOCK
    pairs_per_kv = ATT_GQ // 2

    @pl.when((b == 0) & (i == 0))
    def _():
        r = lax.broadcasted_iota(jnp.int32, (K_SUPER, Q_BLOCK), 0)
        c = lax.broadcasted_iota(jnp.int32, (K_SUPER, Q_BLOCK), 1)
        for tb in range(n_tab):
            w = _multiplicity(tb * Q_BLOCK + c - r)
            bias_scr[tb] = jnp.where(w > 0.0, jnp.log2(jnp.maximum(w, 1.0)), NEG)
        va_scr[:, HEAD_DIM:, :] = jnp.ones((ATT_KV_HEADS, ONES_ROWS, seq), BF16)

    @pl.when(i == 0)
    def _():
        for kvh in range(ATT_KV_HEADS):
            hs = slice(kvh * HEAD_DIM, (kvh + 1) * HEAD_DIM)
            va_scr[kvh, 0:HEAD_DIM, :] = vt_ref[0, hs, :].astype(BF16)
        for cidx in range(seq // K_SUPER):
            rows = slice(cidx * K_SUPER, (cidx + 1) * K_SUPER)
            kc = kt_ref[0, :, rows].T
            for kvh in range(ATT_KV_HEADS):
                hs = slice(kvh * HEAD_DIM, (kvh + 1) * HEAD_DIM)
                kh_scr[kvh, rows, :] = kc[:, hs].astype(BF16)

    for kvh in range(ATT_KV_HEADS):
        qt_scr[kvh] = jnp.concatenate(
            [q_ref[0, (kvh * ATT_GQ + g) * HEAD_DIM:(kvh * ATT_GQ + g + 1) * HEAD_DIM, :]
             for g in range(ATT_GQ)], axis=1)
    m_scr[...] = jnp.full(m_scr.shape, NEG, F32)
    acc_scr[...] = jnp.zeros(acc_scr.shape, F32)
    par = i % Q_PER_SUPER
    j_last = i // Q_PER_SUPER
    n_sb = jnp.minimum(j_last + 1, n_sb_max)

    def key_start(dl):
        return pl.multiple_of((j_last - dl) * K_SUPER, K_SUPER)

    def scores(dl, dst):
        for kvh in range(ATT_KV_HEADS):
            dst[kvh] = _dot(kh_scr[kvh, pl.ds(key_start(dl), K_SUPER), :], qt_scr[kvh])

    def softmax_pv(dl, src):
        start = key_start(dl)
        bias = bias_scr[par + Q_PER_SUPER * dl]
        bias2 = jnp.concatenate([bias, bias], axis=1)
        for kvh in range(ATT_KV_HEADS):
            vtb = va_scr[kvh, :, pl.ds(start, K_SUPER)]
            for pr in range(pairs_per_kv):
                u = kvh * pairs_per_kv + pr
                s = src[kvh, :, pr * pair_w:(pr + 1) * pair_w] + bias2
                m_old = m_scr[u, 0:1, :]
                m_new = jnp.maximum(m_old, jnp.max(s, axis=0, keepdims=True))
                p = jnp.exp2(s - m_new).astype(BF16)
                acc_scr[u] = jnp.exp2(m_old - m_new) * acc_scr[u] + _dot(vtb, p)
                m_scr[u, 0:1, :] = m_new

    def diagonal_block():
        half = K_SUPER // 2
        start = key_start(0)
        bias = bias_scr[0]
        for kvh in range(ATT_KV_HEADS):
            s_scr[kvh, 0:half, :] = _dot(kh_scr[kvh, pl.ds(start, half), :], qt_scr[kvh])
        for kvh in range(ATT_KV_HEADS):
            q_late = jnp.concatenate(
                [qt_scr[kvh, :, g * Q_BLOCK + half:(g + 1) * Q_BLOCK] for g in range(ATT_GQ)], axis=1)
            s2_scr[kvh, 0:half, 0:ATT_GQ * half] = _dot(
                kh_scr[kvh, pl.ds(start + half, half), :], q_late)
        bias_a = jnp.concatenate([bias[0:half, :]] * 2, axis=1)
        for kvh in range(ATT_KV_HEADS):
            vtb = va_scr[kvh, :, pl.ds(start, half)]
            for pr in range(pairs_per_kv):
                u = kvh * pairs_per_kv + pr
                s = s_scr[kvh, 0:half, pr * pair_w:(pr + 1) * pair_w] + bias_a
                m_old = m_scr[u, 0:1, :]
                m_new = jnp.maximum(m_old, jnp.max(s, axis=0, keepdims=True))
                p = jnp.exp2(s - m_new).astype(BF16)
                acc_scr[u] = jnp.exp2(m_old - m_new) * acc_scr[u] + _dot(vtb, p)
                m_scr[u, 0:1, :] = m_new
        bias_b = jnp.concatenate([bias[half:, half:]] * 2, axis=1)
        late = [slice(e * Q_BLOCK + half, (e + 1) * Q_BLOCK) for e in range(2)]
        for kvh in range(ATT_KV_HEADS):
            vtb = va_scr[kvh, :, pl.ds(start + half, half)]
            for pr in range(pairs_per_kv):
                u = kvh * pairs_per_kv + pr
                s = s2_scr[kvh, 0:half, pr * Q_BLOCK:(pr + 1) * Q_BLOCK] + bias_b
                m_old = jnp.concatenate([m_scr[u, 0:1, sl] for sl in late], axis=1)
                m_new = jnp.maximum(m_old, jnp.max(s, axis=0, keepdims=True))
                p = jnp.exp2(s - m_new).astype(BF16)
                alpha = jnp.exp2(m_old - m_new)
                pv = _dot(vtb, p)
                for e, sl in enumerate(late):
                    es = slice(e * half, (e + 1) * half)
                    acc_scr[u, :, sl] = alpha[:, es] * acc_scr[u, :, sl] + pv[:, es]
                    m_scr[u, 0:1, sl] = m_new[:, es]

    diagonal_block()
    n_rest = n_sb - 1

    def body(t, carry):
        scores(1 + 2 * t, s_scr)
        scores(2 + 2 * t, s2_scr)
        softmax_pv(1 + 2 * t, s_scr)
        softmax_pv(2 + 2 * t, s2_scr)
        return carry

    lax.fori_loop(0, n_rest // 2, body, 0)

    @pl.when(n_rest % 2 == 1)
    def _():
        scores(n_sb - 1, s_scr)
        softmax_pv(n_sb - 1, s_scr)

    for u in range(ATT_HEADS // 2):
        acc = acc_scr[u]
        o = acc[0:HEAD_DIM] * (1.0 / acc[HEAD_DIM:HEAD_DIM + 1])
        for e in range(2):
            hs = slice((2 * u + e) * HEAD_DIM, (2 * u + e + 1) * HEAD_DIM)
            o_ref[0, hs, :] = (o[:, e * Q_BLOCK:(e + 1) * Q_BLOCK] * g_ref[0, hs, :]).astype(BF16)


def _attn_prompt(qt, kt, vt, gate_t, batch, seq):
    nqb = seq // Q_BLOCK
    assert seq % K_SUPER == 0 and Q_BLOCK == K_SUPER
    n_sb_max = min(seq // K_SUPER, W_MAX // K_SUPER + 1)
    n_tab = Q_PER_SUPER * n_sb_max
    qblk = lambda b, i: (b, 0, i)
    per_b = lambda b, i: (b, 0, 0)
    return pl.pallas_call(
        functools.partial(_attn_prompt_kernel, n_tab=n_tab, n_sb_max=n_sb_max),
        grid=(batch, nqb),
        in_specs=[
            pl.BlockSpec((1, ATT_INNER, Q_BLOCK), qblk),
            pl.BlockSpec((1, KV_DIM, seq), per_b),
            pl.BlockSpec((1, KV_DIM, seq), per_b),
            pl.BlockSpec((1, ATT_INNER, Q_BLOCK), qblk),
        ],
        out_specs=pl.BlockSpec((1, ATT_INNER, Q_BLOCK), qblk),
        out_shape=jax.ShapeDtypeStruct((batch, ATT_INNER, seq), BF16),
        scratch_shapes=[
            pltpu.VMEM((n_tab, K_SUPER, Q_BLOCK), F32),
            pltpu.VMEM((ATT_KV_HEADS, seq, HEAD_DIM), BF16),
            pltpu.VMEM((ATT_KV_HEADS, ACC_ROWS, seq), BF16),
            pltpu.VMEM((ATT_KV_HEADS, HEAD_DIM, ATT_GQ * Q_BLOCK), BF16),
            pltpu.VMEM((ATT_HEADS // 2, SUBLANES, 2 * Q_BLOCK), F32),
            pltpu.VMEM((ATT_HEADS // 2, ACC_ROWS, 2 * Q_BLOCK), F32),
            pltpu.VMEM((ATT_KV_HEADS, K_SUPER, ATT_GQ * Q_BLOCK), F32),
            pltpu.VMEM((ATT_KV_HEADS, K_SUPER, ATT_GQ * Q_BLOCK), F32),
        ],
        compiler_params=pltpu.CompilerParams(
            dimension_semantics=("arbitrary", "arbitrary"), vmem_limit_bytes=VMEM_LIMIT),
        name="attn_prompt",
    )(qt, kt, vt, gate_t)


def _attn_sample_kernel(q_ref, knt_ref, vnt_ref, g_ref, ckt_ref, cvt_ref, o_ref, kot_ref, vot_ref,
                        bc_scr, bn_scr, *, ts):
    win = ckt_ref.shape[2]
    tp = SAMPLE_PAD
    rows = ATT_GQ * tp

    @pl.when(pl.program_id(0) == 0)
    def _():
        t_c = lax.broadcasted_iota(jnp.int32, (rows, win), 0) % tp
        j_c = lax.broadcasted_iota(jnp.int32, (rows, win), 1)
        w_c = _multiplicity(win + t_c - j_c)
        bc_scr[...] = jnp.where(w_c > 0.0, jnp.log2(jnp.maximum(w_c, 1.0)), NEG)
        t_n = lax.broadcasted_iota(jnp.int32, (rows, tp), 0) % tp
        j_n = lax.broadcasted_iota(jnp.int32, (rows, tp), 1)
        w_n = jnp.where(j_n < ts, _multiplicity(t_n - j_n), 0.0)
        bn_scr[...] = jnp.where(w_n > 0.0, jnp.log2(jnp.maximum(w_n, 1.0)), NEG)

    for kvh in range(ATT_KV_HEADS):
        c0 = kvh * ATT_GQ * HEAD_DIM
        qh = jnp.concatenate(
            [q_ref[:, c0 + g * HEAD_DIM:c0 + (g + 1) * HEAD_DIM] for g in range(ATT_GQ)], axis=0)
        hs = slice(kvh * HEAD_DIM, (kvh + 1) * HEAD_DIM)
        s_c = _dot(qh, ckt_ref[0, hs, :].astype(BF16)) + bc_scr[...]
        s_n = _dot(qh, knt_ref[0, hs, :].astype(BF16)) + bn_scr[...]
        m = jnp.maximum(jnp.max(s_c, axis=-1, keepdims=True), jnp.max(s_n, axis=-1, keepdims=True))
        p_c = jnp.exp2(s_c - m)
        p_n = jnp.exp2(s_n - m)
        den = jnp.sum(p_c, axis=-1, keepdims=True) + jnp.sum(p_n, axis=-1, keepdims=True)
        num = (_dot_nt(p_c.astype(BF16), cvt_ref[0, hs, :].astype(BF16))
               + _dot_nt(p_n.astype(BF16), vnt_ref[0, hs, :].astype(BF16)))
        o = num / den
        o = jnp.concatenate([o[g * tp:(g + 1) * tp] for g in range(ATT_GQ)], axis=1)
        csl = slice(c0, c0 + ATT_GQ * HEAD_DIM)
        o_ref[:, csl] = (o * g_ref[:, csl]).astype(BF16)
    kot_ref[0, :, 0:win - ts] = ckt_ref[0, :, ts:win]
    kot_ref[0, :, win - ts:win] = knt_ref[0, :, 0:ts]
    vot_ref[0, :, 0:win - ts] = cvt_ref[0, :, ts:win]
    vot_ref[0, :, win - ts:win] = vnt_ref[0, :, 0:ts]


def _attn_sample(q, knt, vnt, gate, ckt, cvt, ts):
    batch, _, win = ckt.shape
    tp = SAMPLE_PAD
    blk = lambda b: (b, 0)
    per_b = lambda b: (b, 0, 0)
    return pl.pallas_call(
        functools.partial(_attn_sample_kernel, ts=ts),
        grid=(batch,),
        in_specs=[
            pl.BlockSpec((tp, ATT_INNER), blk),
            pl.BlockSpec((1, KV_DIM, tp), per_b),
            pl.BlockSpec((1, KV_DIM, tp), per_b),
            pl.BlockSpec((tp, ATT_INNER), blk),
            pl.BlockSpec((1, KV_DIM, win), per_b),
            pl.BlockSpec((1, KV_DIM, win), per_b),
        ],
        out_specs=[
            pl.BlockSpec((tp, ATT_INNER), blk),
            pl.BlockSpec((1, KV_DIM, win), per_b),
            pl.BlockSpec((1, KV_DIM, win), per_b),
        ],
        out_shape=[
            jax.ShapeDtypeStruct((batch * tp, ATT_INNER), BF16),
            jax.ShapeDtypeStruct((batch, KV_DIM, win), F32),
            jax.ShapeDtypeStruct((batch, KV_DIM, win), F32),
        ],
        scratch_shapes=[pltpu.VMEM((ATT_GQ * tp, win), F32), pltpu.VMEM((ATT_GQ * tp, tp), F32)],
        compiler_params=pltpu.CompilerParams(
            dimension_semantics=("arbitrary",), vmem_limit_bytes=VMEM_LIMIT),
        name="attn_sample",
    )(q, knt, vnt, gate, ckt, cvt)


def _group_norm(yg, nw):
    ms = jnp.mean(yg * yg, axis=-1, keepdims=True)
    return (yg * lax.rsqrt(ms + EPS) * nw).astype(BF16)


def _outproj_kernel(x_ref, ys_ref, yat_ref, nw_ref, wt_ref, o_ref):
    yn = jnp.concatenate(
        [_group_norm(ys_ref[:, g * HALF_INNER:(g + 1) * HALF_INNER],
                     nw_ref[:, g * HALF_INNER:(g + 1) * HALF_INNER]) for g in range(SSD_GROUPS)],
        axis=1)
    yt = jnp.concatenate([yn.astype(F32).T.astype(BF16), yat_ref[0]], axis=0)
    o_ref[...] = x_ref[...] + _dot(wt_ref[...], yt).T


def _out_proj(x2d, y_ssd, y_att_t, nw, w_o, tm):
    rows = x2d.shape[0]
    per_seq = y_att_t.shape[2] // tm
    row = lambda i: (i, 0)
    const = lambda i: (0, 0)
    return pl.pallas_call(
        _outproj_kernel,
        grid=(rows // tm,),
        in_specs=[
            pl.BlockSpec((tm, D_MODEL), row),
            pl.BlockSpec((tm, SSD_INNER), row),
            pl.BlockSpec((1, ATT_INNER, tm), lambda i: (i // per_seq, 0, i % per_seq)),
            pl.BlockSpec((1, SSD_INNER), const),
            pl.BlockSpec((D_MODEL, SSD_INNER + ATT_INNER), const, pipeline_mode=pl.Buffered(1)),
        ],
        out_specs=pl.BlockSpec((tm, D_MODEL), row),
        out_shape=jax.ShapeDtypeStruct((rows, D_MODEL), F32),
        compiler_params=pltpu.CompilerParams(
            dimension_semantics=("arbitrary",), vmem_limit_bytes=VMEM_LIMIT),
        name="out_proj",
    )(x2d, y_ssd, y_att_t, nw, w_o)


def _wconv_out_kernel(x_ref, ys_ref, ya_ref, nw_ref, w_ref, wb_ref, o_ref):
    j = pl.program_id(0)
    wb = w_ref[...].astype(BF16)
    wb_ref[...] = w_ref[...].T.astype(BF16)
    y = jnp.where(j < SSD_GROUPS, _group_norm(ys_ref[...], nw_ref[...]), ya_ref[...])
    part = _dot(y, wb)
    bs, ts, _ = x_ref.shape

    @pl.when(j == 0)
    def _():
        for b in range(bs):
            o_ref[b] = x_ref[b] + part[b * SAMPLE_PAD:b * SAMPLE_PAD + ts]

    @pl.when(j > 0)
    def _():
        for b in range(bs):
            o_ref[b] += part[b * SAMPLE_PAD:b * SAMPLE_PAD + ts]


def _wconv_out(x_s, y_ssd, y_att, nw, w_f32):
    rows = x_s.shape[0] * SAMPLE_PAD
    n_ssd = SSD_GROUPS
    n_chunks = (SSD_INNER + ATT_INNER) // WO_CHUNK
    const = lambda j: (0, 0)
    ssd_chunk = lambda j: (0, jnp.minimum(j, n_ssd - 1))
    return pl.pallas_call(
        _wconv_out_kernel,
        grid=(n_chunks,),
        in_specs=[
            pl.BlockSpec(x_s.shape, lambda j: (0, 0, 0)),
            pl.BlockSpec((rows, WO_CHUNK), ssd_chunk),
            pl.BlockSpec((rows, WO_CHUNK), lambda j: (0, jnp.maximum(j - n_ssd, 0))),
            pl.BlockSpec((1, WO_CHUNK), ssd_chunk),
            pl.BlockSpec((WO_CHUNK, D_MODEL), lambda j: (j, 0)),
        ],
        out_specs=[
            pl.BlockSpec((D_MODEL, WO_CHUNK), lambda j: (0, j)),
            pl.BlockSpec(x_s.shape, lambda j: (0, 0, 0)),
        ],
        out_shape=[
            jax.ShapeDtypeStruct((D_MODEL, SSD_INNER + ATT_INNER), BF16),
            jax.ShapeDtypeStruct(x_s.shape, F32),
        ],
        compiler_params=pltpu.CompilerParams(
            dimension_semantics=("arbitrary",), vmem_limit_bytes=VMEM_LIMIT),
        name="wconv_out",
    )(x_s, y_ssd, y_att, nw, w_f32)


def _rope_tables(pos):
    n = pos.shape[0]
    half = ROT_DIM // 2
    inv = ROPE_THETA ** (-np.arange(0, ROT_DIM, 2, dtype=np.float64) / ROT_DIM)
    ang = pos.astype(np.float64)[:, None] * inv[None, :]
    cos, sin = np.cos(ang), np.sin(ang)
    rest = HEAD_DIM - ROT_DIM
    zh = np.zeros((n, half))
    cos_h = np.concatenate([cos, cos, np.ones((n, rest))], axis=1)
    sa_h = np.concatenate([zh, sin, np.zeros((n, rest))], axis=1)
    sb_h = np.concatenate([-sin, zh, np.zeros((n, rest))], axis=1)
    rep = LANES // HEAD_DIM
    return tuple(jnp.asarray(np.tile(t, (1, rep)), dtype=F32) for t in (cos_h, sa_h, sb_h))


def _lane_pad(v, n=LANES):
    return jnp.pad(v, (0, n - v.shape[0])).reshape(1, n)


def _to_cache(xt, batch, seq):
    return xt.reshape(1, batch, ATT_KV_HEADS, HEAD_DIM, seq).transpose(0, 1, 4, 2, 3)


def kernel(x_prompt, x_sample, cache_k, cache_v, state_conv, state_ssm, norm_w, w_in, conv_w,
           conv_b, dt_bias, a_log, d_skip, ssd_norm_w, q_norm_w, k_norm_w, w_out):
    bp, tp_, _ = x_prompt.shape
    bs, ts, _ = x_sample.shape
    depth = w_in.shape[0]
    assert depth == 1 and tp_ % SSD_CHUNK == 0 and ts <= SAMPLE_PAD and ts >= CONV_W - 1
    l = 0
    win = cache_k.shape[2]

    nw = norm_w[l].reshape(1, D_MODEL)
    rep = LANES // HEAD_DIM
    qnw = jnp.tile(q_norm_w[l], rep).reshape(1, LANES)
    knw = jnp.tile(k_norm_w[l], rep).reshape(1, LANES)
    cw = jnp.pad(conv_w[l], ((0, SUBLANES - CONV_W), (0, 0)))
    cb = conv_b[l].reshape(1, CONV_DIM)
    dtb = _lane_pad(dt_bias[l])
    alog = _lane_pad(a_log[l])
    dexp = jnp.repeat(d_skip[l], SSD_HEADDIM).reshape(1, SSD_INNER)
    snw = ssd_norm_w[l].reshape(1, SSD_INNER)

    pad = SAMPLE_PAD
    tabs_s = _rope_tables(PAST_LEN + np.arange(bs * pad) % pad)
    w_t, proj_s = _wconv_in(x_sample, nw, jnp.swapaxes(w_in[l], 0, 1))
    zs, xbc, dtr, q, kt, vt, gs, _ = _sections(proj_s, qnw, knw, tabs_s)
    cprev = jnp.pad(state_conv[l], ((0, 0), (SUBLANES - (CONV_W - 1), 0), (0, 0)))
    y_ssd, h_s = _ssd(xbc, dtr, zs, cprev, state_ssm[l].reshape(bs, SSD_INNER, SSD_STATE),
                      cw, cb, dtb, alog, dexp, bs, pad, ts)
    knt = kt.reshape(KV_DIM, bs, pad).transpose(1, 0, 2)
    vnt = vt.reshape(KV_DIM, bs, pad).transpose(1, 0, 2)
    ckt = cache_k[l].transpose(0, 2, 3, 1).reshape(bs, KV_DIM, win)
    cvt = cache_v[l].transpose(0, 2, 3, 1).reshape(bs, KV_DIM, win)
    y_att, kot, vot = _attn_sample(q, knt, vnt, gs, ckt, cvt, ts)
    w_o, y_s = _wconv_out(x_sample, y_ssd, y_att, snw, w_out[l])
    k_s = _to_cache(kot, bs, win)
    v_s = _to_cache(vot, bs, win)
    c_s = xbc.reshape(bs, pad, CONV_DIM)[:, ts - (CONV_W - 1):ts][None]
    h_s = h_s.reshape(1, bs, SSD_HEADS, SSD_HEADDIM, SSD_STATE)

    tm = TM_IN
    xp2 = x_prompt.reshape(bp * tp_, D_MODEL)
    tabs = _rope_tables(np.arange(tp_))
    zs, xbc, dtr, q, kt, vt, gs, ctail = _in_proj(xp2, nw, w_t, qnw, knw, tabs, tm, tp_)
    y_ssd, h_p = _ssd(xbc, dtr, zs, jnp.zeros((bp, SUBLANES, CONV_DIM), F32),
                      jnp.zeros((bp, SSD_INNER, SSD_STATE), F32),
                      cw, cb, dtb, alog, dexp, bp, SSD_CHUNK, SSD_CHUNK)
    y_att = _attn_prompt(q, kt, vt, gs, bp, tp_)
    y_p = _out_proj(xp2, y_ssd, y_att, snw, w_o, TM_OUT).reshape(bp, tp_, D_MODEL)
    keep = min(W_MAX, tp_)
    k_p = _to_cache(kt, bp, tp_)[:, :, tp_ - keep:]
    v_p = _to_cache(vt, bp, tp_)[:, :, tp_ - keep:]
    c_p = ctail[:, SUBLANES - (CONV_W - 1):][None]
    h_p = h_p.reshape(1, bp, SSD_HEADS, SSD_HEADDIM, SSD_STATE)

    return (y_p, y_s, k_p, v_p, c_p, h_p, k_s, v_s, c_s, h_s)
```

```python
import functools

import jax
import jax.numpy as jnp
import numpy as np
from jax import lax
from jax.experimental import pallas as pl
from jax.experimental.pallas import tpu as pltpu

F32 = jnp.float32
BF16 = jnp.bfloat16

D_MODEL = 2048
SSD_HEADS = 16
SSD_HEADDIM = 64
SSD_INNER = SSD_HEADS * SSD_HEADDIM
SSD_GROUPS = 2
SSD_STATE = 128
CONV_W = 4
CONV_DIM = SSD_INNER + 2 * SSD_GROUPS * SSD_STATE
SSD_CHUNK = 128
ATT_HEADS = 16
ATT_KV_HEADS = 4
HEAD_DIM = 64
ATT_GQ = ATT_HEADS // ATT_KV_HEADS
ATT_INNER = ATT_HEADS * HEAD_DIM
KV_DIM = ATT_KV_HEADS * HEAD_DIM
ROT_DIM = HEAD_DIM // 4
ROPE_THETA = 500000.0
DILATED_BRANCHES = ((128, 1), (512, 4), (2048, 16))
W_MAX = 2048
PAST_LEN = 16384
EPS = 1e-6

LANES = 128
SUBLANES = 8
Q_BLOCK = 256
K_SUPER = 256
Q_PER_SUPER = K_SUPER // Q_BLOCK
ONES_ROWS = 16
ACC_ROWS = HEAD_DIM + ONES_ROWS
CONV_COLS = 256
TM_IN = 512
TM_OUT = 512
SSD_SEQS = 4
SAMPLE_PAD = 16
NEG = -1e30
VMEM_LIMIT = 58 * 1024 * 1024

Z0 = 0
X0 = Z0 + SSD_INNER
DT0 = X0 + CONV_DIM
Q0 = DT0 + SSD_HEADS
K0 = Q0 + ATT_INNER
V0 = K0 + KV_DIM
G0 = V0 + KV_DIM
W_ROWS = G0 + ATT_INNER
W_CHUNK = 1024
HALF_INNER = SSD_INNER // SSD_GROUPS
WO_CHUNK = HALF_INNER
LOG2E = 1.4426950408889634
Q_SCALE = HEAD_DIM ** -0.5 * LOG2E


def _dot(a, b):
    return jnp.dot(a, b, preferred_element_type=F32)


def _dot_nt(a, b):
    return lax.dot_general(a, b, (((1,), (1,)), ((), ())), preferred_element_type=F32)


def _split3(x):
    hi = x.astype(BF16)
    r1 = x - hi.astype(F32)
    mid = r1.astype(BF16)
    lo = (r1 - mid.astype(F32)).astype(BF16)
    return hi, mid, lo


def _dot_exact_rhs(x, m):
    hi, mid, lo = _split3(x)
    return _dot(hi, m) + _dot(mid, m) + _dot(lo, m)


def _dot_wide_rhs(x, m):
    hi = x.astype(BF16)
    lo = (x - hi.astype(F32)).astype(BF16)
    return _dot(hi, m) + _dot(lo, m)


def _dot_exact_lhs(m, x):
    hi, mid, lo = _split3(x)
    return _dot(m, hi) + _dot(m, mid) + _dot(m, lo)


def _silu(x):
    hx = 0.5 * x
    return hx * jnp.tanh(hx) + hx


def _multiplicity(d):
    w = jnp.zeros(d.shape, F32)
    for window, dil in DILATED_BRANCHES:
        hit = (d >= 0) & (d <= window) & (lax.rem(d, dil) == 0)
        w = w + jnp.where(hit, 1.0, 0.0)
    return w


def _norm_rope(y, nw, cos, sa, sb, scale):
    lane = lax.broadcasted_iota(jnp.int32, (1, LANES), 1)
    first = lane < HEAD_DIM
    y2 = y * y
    s_lo = jnp.sum(jnp.where(first, y2, 0.0), axis=-1, keepdims=True)
    s_hi = jnp.sum(jnp.where(first, 0.0, y2), axis=-1, keepdims=True)
    ms = jnp.where(first, s_lo, s_hi) * (1.0 / HEAD_DIM)
    yn = y * lax.rsqrt(ms + EPS) * nw
    half = ROT_DIM // 2
    rot = yn * cos + pltpu.roll(yn, half, 1) * sa + pltpu.roll(yn, LANES - half, 1) * sb
    return rot * scale


def _causal_conv_silu(load, cw_ref, cb_ref, store):
    assert CONV_W == 4
    for c0 in range(0, CONV_DIM, CONV_COLS):
        cs = slice(c0, c0 + CONV_COLS)
        xe = load(cs)
        x2 = pltpu.roll(xe, 2, 0)
        even = cw_ref[3:4, cs] * xe + cw_ref[1:2, cs] * x2
        odd = cw_ref[2:3, cs] * xe + cw_ref[0:1, cs] * x2
        store(cs, _silu((even + pltpu.roll(odd, 1, 0))[SUBLANES:, :] + cb_ref[:, cs]))


def _pre_norm(x_ref, nw_ref):
    x = x_ref[...]
    ms = jnp.mean(x * x, axis=-1, keepdims=True)
    return (x * lax.rsqrt(ms + EPS) * nw_ref[...]).astype(BF16)


def _inproj_kernel(x_ref, nw_ref, w_ref, *refs):
    hn = _pre_norm(x_ref, nw_ref)

    def proj(a, b):
        return _dot_nt(hn, w_ref[a:b, :])

    _emit_sections(proj, x_ref.shape[0], *refs)


def _emit_sections(proj, tm, qnw_ref, knw_ref, cos_ref, sa_ref, sb_ref,
                   zs_ref, xbc_ref, dt_ref, q_ref, kt_ref, vt_ref, gs_ref, ctail_ref):
    cos, sa, sb = cos_ref[...], sa_ref[...], sb_ref[...]
    q = proj(Q0, K0)
    for c in range(ATT_INNER // LANES):
        sl = slice(c * LANES, (c + 1) * LANES)
        qn = _norm_rope(q[:, sl], qnw_ref[...], cos, sa, sb, Q_SCALE)
        if len(q_ref.shape) == 3:
            q_ref[0, sl, :] = qn.T.astype(BF16)
        else:
            q_ref[:, sl] = qn.astype(BF16)
    k = proj(K0, V0)
    kn = [_norm_rope(k[:, c * LANES:(c + 1) * LANES], knw_ref[...], cos, sa, sb, 1.0)
          for c in range(KV_DIM // LANES)]
    kt_ref[0] = jnp.concatenate(kn, axis=1).T

    xbc = proj(X0, DT0)
    xbc_ref[...] = xbc
    ctail_ref[0] = xbc[tm - SUBLANES:tm, :]
    vt_ref[0] = proj(V0, G0).T
    zs_ref[...] = _silu(proj(Z0, X0))
    gs = _silu(proj(G0, W_ROWS))
    if len(gs_ref.shape) == 3:
        gs_ref[0] = gs.T
    else:
        gs_ref[...] = gs
    dt_ref[...] = proj(DT0, DT0 + LANES)


def _in_proj(x2d, nw, w_t, qnw, knw, tables, tm, seq):
    rows = x2d.shape[0]
    const = lambda i: (0, 0)
    row = lambda i: (i, 0)
    sec_in, out_specs, out_shape = _section_specs(rows, tm, seq, tables[0].shape[0] // tm, True)
    return pl.pallas_call(
        _inproj_kernel,
        grid=(rows // tm,),
        in_specs=[
            pl.BlockSpec((tm, D_MODEL), row),
            pl.BlockSpec((1, D_MODEL), const),
            pl.BlockSpec((W_ROWS, D_MODEL), const, pipeline_mode=pl.Buffered(1)),
        ] + sec_in,
        out_specs=out_specs,
        out_shape=out_shape,
        compiler_params=pltpu.CompilerParams(
            dimension_semantics=("arbitrary",), vmem_limit_bytes=VMEM_LIMIT),
        name="in_proj",
    )(x2d, nw, w_t, qnw, knw, *tables)


def _section_specs(rows, tm, seq, period, q_transposed):
    per_seq = seq // tm
    const = lambda i: (0, 0)
    row = lambda i: (i, 0)
    tab = lambda i: (i % period, 0)
    tr = lambda i: (i // per_seq, 0, i % per_seq)
    slab = lambda i: (i // per_seq, 0, 0)
    in_specs = [pl.BlockSpec((1, LANES), const)] * 2 + [pl.BlockSpec((tm, LANES), tab)] * 3
    row_outs = {0: (SSD_INNER, F32), 1: (CONV_DIM, F32), 2: (LANES, F32), 3: (ATT_INNER, BF16),
                6: (ATT_INNER, F32)}
    out_specs, out_shape = [], []
    for idx in range(7):
        if idx in (3, 6) and q_transposed:
            dt = row_outs[idx][1]
            out_specs.append(pl.BlockSpec((1, ATT_INNER, tm), tr))
            out_shape.append(jax.ShapeDtypeStruct((rows // seq, ATT_INNER, seq), dt))
        elif idx in (0, 1, 2) and q_transposed:
            n, dt = row_outs[idx]
            out_specs.append(pl.BlockSpec((tm, n), lambda i: (i % per_seq, i // per_seq)))
            out_shape.append(jax.ShapeDtypeStruct((seq, (rows // seq) * n), dt))
        elif idx in row_outs:
            n, dt = row_outs[idx]
            out_specs.append(pl.BlockSpec((tm, n), row))
            out_shape.append(jax.ShapeDtypeStruct((rows, n), dt))
        else:
            out_specs.append(pl.BlockSpec((1, KV_DIM, tm), tr))
            out_shape.append(jax.ShapeDtypeStruct((rows // seq, KV_DIM, seq), F32))
    out_specs.append(pl.BlockSpec((1, SUBLANES, CONV_DIM), slab))
    out_shape.append(jax.ShapeDtypeStruct((rows // seq, SUBLANES, CONV_DIM), F32))
    return in_specs, out_specs, out_shape


def _wconv_in_kernel(x_ref, nw_ref, w_ref, wb_ref, p_ref, xpad_scr, hn_scr):
    @pl.when(pl.program_id(0) == 0)
    def _():
        bs, ts, _ = x_ref.shape
        xpad_scr[...] = jnp.zeros(xpad_scr.shape, F32)
        for b in range(bs):
            xpad_scr[b * SAMPLE_PAD:b * SAMPLE_PAD + ts, :] = x_ref[b]
        hn_scr[...] = _pre_norm(xpad_scr, nw_ref)

    row = pl.program_id(0) * W_CHUNK + lax.broadcasted_iota(jnp.int32, (W_CHUNK, 1), 0)
    wb = jnp.where(row < W_ROWS, w_ref[...], 0.0).astype(BF16)
    wb_ref[...] = wb
    p_ref[0] = _dot_nt(hn_scr[...], wb)


def _wconv_in(x_s, nw, w_f32):
    rows = x_s.shape[0] * SAMPLE_PAD
    n_chunks = pl.cdiv(W_ROWS, W_CHUNK)
    const = lambda j: (0, 0)
    return pl.pallas_call(
        _wconv_in_kernel,
        grid=(n_chunks,),
        in_specs=[
            pl.BlockSpec(x_s.shape, lambda j: (0, 0, 0)),
            pl.BlockSpec((1, D_MODEL), const),
            pl.BlockSpec((W_CHUNK, D_MODEL), lambda j: (j, 0)),
        ],
        out_specs=[
            pl.BlockSpec((W_CHUNK, D_MODEL), lambda j: (j, 0)),
            pl.BlockSpec((1, rows, W_CHUNK), lambda j: (j, 0, 0)),
        ],
        out_shape=[
            jax.ShapeDtypeStruct((W_ROWS, D_MODEL), BF16),
            jax.ShapeDtypeStruct((n_chunks, rows, W_CHUNK), F32),
        ],
        scratch_shapes=[pltpu.VMEM((rows, D_MODEL), F32), pltpu.VMEM((rows, D_MODEL), BF16)],
        compiler_params=pltpu.CompilerParams(
            dimension_semantics=("arbitrary",), vmem_limit_bytes=VMEM_LIMIT),
        name="wconv_in",
    )(x_s, nw, w_f32)


def _sections_kernel(p_ref, *refs):
    *refs, tail_scr = refs
    pfull = jnp.concatenate([p_ref[j] for j in range(p_ref.shape[0])], axis=1)
    tail_scr[...] = pfull[:, Q0:W_ROWS]

    def proj(a, b):
        if a >= Q0:
            return tail_scr[:, a - Q0:b - Q0]
        return pfull[:, a:b]

    _emit_sections(proj, p_ref.shape[1], *refs)


def _sections(p, qnw, knw, tables):
    rows = p.shape[1]
    sec_in, out_specs, out_shape = _section_specs(rows, rows, rows, 1, False)
    return pl.pallas_call(
        _sections_kernel,
        grid=(1,),
        in_specs=[pl.BlockSpec(p.shape, lambda i: (0, 0, 0))] + sec_in,
        out_specs=out_specs,
        out_shape=out_shape,
        scratch_shapes=[pltpu.VMEM((rows, W_ROWS - Q0), F32)],
        compiler_params=pltpu.CompilerParams(
            dimension_semantics=("arbitrary",), vmem_limit_bytes=VMEM_LIMIT),
        name="sections",
    )(p, qnw, knw, *tables)


def _ssd_chain(s, xbc_ref, dtr_ref, z_ref, cprev_ref, h0_ref, cw_ref, cb_ref, dtb_ref, alog_ref,
               dexp_ref, y_ref, hout_ref, xext, ht, xc_scr, lin, valid):
    L = SSD_CHUNK
    c = pl.program_id(1)
    last = pl.num_programs(1) - 1

    @pl.when(c == 0)
    def _():
        xext[s, 0:SUBLANES, :] = cprev_ref[s]
        for g in range(SSD_GROUPS):
            ht[s, g] = h0_ref[s, g * HALF_INNER:(g + 1) * HALF_INNER, :].T

    def pad_rows(v):
        if lin == L:
            return v
        return jnp.concatenate([v, jnp.zeros((L - lin, v.shape[1]), v.dtype)], axis=0)

    lane = lax.broadcasted_iota(jnp.int32, (1, LANES), 1)
    rowid = lax.broadcasted_iota(jnp.int32, (L, 1), 0)
    dt = jax.nn.softplus(pad_rows(_seq_slab(dtr_ref, s)[...]) + dtb_ref[...])
    dt = jnp.where((lane < SSD_HEADS) & (rowid < valid), dt, 0.0)
    dta = dt * (-jnp.exp(alog_ref[...]) * LOG2E)

    r2 = lax.broadcasted_iota(jnp.int32, (L, L), 0)
    c2 = lax.broadcasted_iota(jnp.int32, (L, L), 1)
    tri = r2 >= c2
    tri_b = jnp.where(tri, 1.0, 0.0).astype(BF16)
    cum = _dot_exact_lhs(tri_b, dta)
    cum_last = cum[L - 1:L, :]
    yield

    er = lax.broadcasted_iota(jnp.int32, (LANES, SSD_INNER), 0)
    ec = lax.broadcasted_iota(jnp.int32, (LANES, SSD_INNER), 1)
    expand = jnp.where(ec // SSD_HEADDIM == er, 1.0, 0.0).astype(BF16)
    ecum = _dot_wide_rhs(jnp.exp2(cum), expand)
    wexp = _dot_wide_rhs(jnp.exp2(cum_last - cum) * dt, expand)
    cd = _dot_exact_rhs(jnp.broadcast_to(jnp.exp2(cum_last), (SUBLANES, LANES)), expand)[0:1, :]
    cum_t = cum.T
    dt_t = dt.T
    yield

    if lin < L:
        xext[s, SUBLANES + lin:SUBLANES + L, :] = jnp.zeros((L - lin, CONV_DIM), F32)
    xext[s, SUBLANES:SUBLANES + lin, :] = _seq_slab(xbc_ref, s)[...]

    def load(cs):
        return xext[s, 0:SUBLANES + L, cs]

    def store(cs, v):
        xc_scr[s, :, cs] = v

    _causal_conv_silu(load, cw_ref, cb_ref, store)
    xext[s, 0:SUBLANES, :] = xext[s, L:L + SUBLANES, :]
    xc = xc_scr.at[s]
    xs = xc[:, :SSD_INNER]
    bm = xc[:, SSD_INNER:SSD_INNER + SSD_GROUPS * SSD_STATE]
    cm = xc[:, SSD_INNER + SSD_GROUPS * SSD_STATE:]
    bm_b = bm.astype(BF16)
    cm_b = cm.astype(BF16)
    xs_b = xs.astype(BF16)

    y_off = jnp.concatenate(
        [_dot(cm_b[:, g * SSD_STATE:(g + 1) * SSD_STATE], ht[s, g].astype(BF16))
         for g in range(SSD_GROUPS)], axis=1) * ecum
    cbs = [_dot_nt(cm_b[:, g * SSD_STATE:(g + 1) * SSD_STATE], bm_b[:, g * SSD_STATE:(g + 1) * SSD_STATE])
           for g in range(SSD_GROUPS)]
    yield

    lane_half = lane // HEAD_DIM
    heads_per_group = SSD_HEADS // SSD_GROUPS
    y_parts = []
    for g in range(SSD_GROUPS):
        for pr in range(heads_per_group // 2):
            col0 = g * HALF_INNER + pr * LANES
            xp = xs_b[:, col0:col0 + LANES]
            yp = jnp.zeros((L, LANES), F32)
            for e in range(2):
                hh = g * heads_per_group + pr * 2 + e
                decay = jnp.exp2(cum[:, hh:hh + 1] - cum_t[hh:hh + 1, :])
                sc = jnp.where(tri, cbs[g] * decay, 0.0) * dt_t[hh:hh + 1, :]
                xm = jnp.where(lane_half == e, xp, jnp.zeros_like(xp))
                yp = yp + _dot(sc.astype(BF16), xm)
            y_parts.append(yp)
            if pr % 2 == 1:
                yield
    y_diag = jnp.concatenate(y_parts, axis=1)

    xw = (xs * wexp).astype(BF16)
    for g in range(SSD_GROUPS):
        sl = slice(g * HALF_INNER, (g + 1) * HALF_INNER)
        bm_t = bm[:, g * SSD_STATE:(g + 1) * SSD_STATE].T.astype(BF16)
        ht[s, g] = ht[s, g] * cd[:, sl] + _dot(bm_t, xw[:, sl])
    yield

    y = (y_diag + y_off + dexp_ref[...] * xs) * pad_rows(_seq_slab(z_ref, s)[...])
    _seq_slab(y_ref, s)[...] = y[:lin]

    @pl.when(c == last)
    def _():
        for g in range(SSD_GROUPS):
            hout_ref[s, g * HALF_INNER:(g + 1) * HALF_INNER, :] = ht[s, g].T


def _seq_slab(ref, s):
    if len(ref.shape) == 3:
        return ref.at[s]
    width = ref.shape[1] // SSD_SEQS
    return ref.at[:, s * width:(s + 1) * width]


def _ssd_kernel(*refs, lin, valid):
    chains = [_ssd_chain(s, *refs, lin, valid) for s in range(SSD_SEQS)]
    while chains:
        alive = []
        for ch in chains:
            try:
                next(ch)
                alive.append(ch)
            except StopIteration:
                pass
        chains = alive


def _ssd(xbc, dtr, z, cprev, h0, cw, cb, dtb, alog, dexp, batch, lin, valid):
    wide = xbc.shape[1] != CONV_DIM
    seq = xbc.shape[0] if wide else xbc.shape[0] // batch
    nc = seq // lin
    g = SSD_SEQS
    assert batch % g == 0
    per_b = lambda b, c: (b, 0, 0)
    const = lambda b, c: (0, 0)
    if wide:
        spec = lambda n: pl.BlockSpec((lin, g * n), lambda b, c: (c, b))
        arr = lambda a, n: a
        y_shape = (seq, batch * SSD_INNER)
    else:
        spec = lambda n: pl.BlockSpec((g, lin, n), lambda b, c: (b, c, 0))
        arr = lambda a, n: a.reshape(batch, seq, n)
        y_shape = (batch, seq, SSD_INNER)
    y, hout = pl.pallas_call(
        functools.partial(_ssd_kernel, lin=lin, valid=valid),
        grid=(batch // g, nc),
        in_specs=[
            spec(CONV_DIM),
            spec(LANES),
            spec(SSD_INNER),
            pl.BlockSpec((g, SUBLANES, CONV_DIM), per_b),
            pl.BlockSpec((g, SSD_INNER, SSD_STATE), per_b),
            pl.BlockSpec((SUBLANES, CONV_DIM), const),
            pl.BlockSpec((1, CONV_DIM), const),
            pl.BlockSpec((1, LANES), const),
            pl.BlockSpec((1, LANES), const),
            pl.BlockSpec((1, SSD_INNER), const),
        ],
        out_specs=[
            spec(SSD_INNER),
            pl.BlockSpec((g, SSD_INNER, SSD_STATE), per_b),
        ],
        out_shape=[
            jax.ShapeDtypeStruct(y_shape, F32),
            jax.ShapeDtypeStruct((batch, SSD_INNER, SSD_STATE), F32),
        ],
        scratch_shapes=[
            pltpu.VMEM((g, SUBLANES + SSD_CHUNK + SUBLANES, CONV_DIM), F32),
            pltpu.VMEM((g, SSD_GROUPS, SSD_STATE, HALF_INNER), F32),
            pltpu.VMEM((g, SSD_CHUNK, CONV_DIM), F32),
        ],
        compiler_params=pltpu.CompilerParams(
            dimension_semantics=("arbitrary", "arbitrary"), vmem_limit_bytes=VMEM_LIMIT),
        name="ssd",
    )(arr(xbc, CONV_DIM), arr(dtr, LANES), arr(z, SSD_INNER), cprev, h0, cw, cb, dtb, alog, dexp)
    return (y if wide else y.reshape(batch * seq, SSD_INNER)), hout


def _attn_prompt_kernel(q_ref, kt_ref, vt_ref, g_ref, o_ref,
                        bias_scr, kh_scr, va_scr, qt_scr, m_scr, acc_scr, s_scr, s2_scr, *, n_tab, n_sb_max):
    b = pl.program_id(0)
    i = pl.program_id(1)
    seq = kt_ref.shape[2]
    pair_w = 2 * Q_BLOCK
    pairs_per_kv = ATT_GQ // 2

    @pl.when((b == 0) & (i == 0))
    def _():
        r = lax.broadcasted_iota(jnp.int32, (K_SUPER, Q_BLOCK), 0)
        c = lax.broadcasted_iota(jnp.int32, (K_SUPER, Q_BLOCK), 1)
        for tb in range(n_tab):
            w = _multiplicity(tb * Q_BLOCK + c - r)
            bias_scr[tb] = jnp.where(w > 0.0, jnp.log2(jnp.maximum(w, 1.0)), NEG)
        va_scr[:, HEAD_DIM:, :] = jnp.ones((ATT_KV_HEADS, ONES_ROWS, seq), BF16)

    @pl.when(i == 0)
    def _():
        for kvh in range(ATT_KV_HEADS):
            hs = slice(kvh * HEAD_DIM, (kvh + 1) * HEAD_DIM)
            va_scr[kvh, 0:HEAD_DIM, :] = vt_ref[0, hs, :].astype(BF16)
        for cidx in range(seq // K_SUPER):
            rows = slice(cidx * K_SUPER, (cidx + 1) * K_SUPER)
            kc = kt_ref[0, :, rows].T
            for kvh in range(ATT_KV_HEADS):
                hs = slice(kvh * HEAD_DIM, (kvh + 1) * HEAD_DIM)
                kh_scr[kvh, rows, :] = kc[:, hs].astype(BF16)

    for kvh in range(ATT_KV_HEADS):
        qt_scr[kvh] = jnp.concatenate(
            [q_ref[0, (kvh * ATT_GQ + g) * HEAD_DIM:(kvh * ATT_GQ + g + 1) * HEAD_DIM, :]
             for g in range(ATT_GQ)], axis=1)
    m_scr[...] = jnp.full(m_scr.shape, NEG, F32)
    acc_scr[...] = jnp.zeros(acc_scr.shape, F32)
    par = i % Q_PER_SUPER
    j_last = i // Q_PER_SUPER
    n_sb = jnp.minimum(j_last + 1, n_sb_max)

    def key_start(dl):
        return pl.multiple_of((j_last - dl) * K_SUPER, K_SUPER)

    def scores(dl, dst):
        for kvh in range(ATT_KV_HEADS):
            dst[kvh] = _dot(kh_scr[kvh, pl.ds(key_start(dl), K_SUPER), :], qt_scr[kvh])

    def softmax_pv(dl, src):
        start = key_start(dl)
        bias = bias_scr[par + Q_PER_SUPER * dl]
        bias2 = jnp.concatenate([bias, bias], axis=1)
        for kvh in range(ATT_KV_HEADS):
            vtb = va_scr[kvh, :, pl.ds(start, K_SUPER)]
            for pr in range(pairs_per_kv):
                u = kvh * pairs_per_kv + pr
                s = src[kvh, :, pr * pair_w:(pr + 1) * pair_w] + bias2
                m_old = m_scr[u, 0:1, :]
                m_new = jnp.maximum(m_old, jnp.max(s, axis=0, keepdims=True))
                p = jnp.exp2(s - m_new).astype(BF16)
                acc_scr[u] = jnp.exp2(m_old - m_new) * acc_scr[u] + _dot(vtb, p)
                m_scr[u, 0:1, :] = m_new

    def diagonal_block():
        half = K_SUPER // 2
        start = key_start(0)
        bias = bias_scr[0]
        for kvh in range(ATT_KV_HEADS):
            s_scr[kvh, 0:half, :] = _dot(kh_scr[kvh, pl.ds(start, half), :], qt_scr[kvh])
        for kvh in range(ATT_KV_HEADS):
            q_late = jnp.concatenate(
                [qt_scr[kvh, :, g * Q_BLOCK + half:(g + 1) * Q_BLOCK] for g in range(ATT_GQ)], axis=1)
            s2_scr[kvh, 0:half, 0:ATT_GQ * half] = _dot(
                kh_scr[kvh, pl.ds(start + half, half), :], q_late)
        bias_a = jnp.concatenate([bias[0:half, :]] * 2, axis=1)
        for kvh in range(ATT_KV_HEADS):
            vtb = va_scr[kvh, :, pl.ds(start, half)]
            for pr in range(pairs_per_kv):
                u = kvh * pairs_per_kv + pr
                s = s_scr[kvh, 0:half, pr * pair_w:(pr + 1) * pair_w] + bias_a
                m_old = m_scr[u, 0:1, :]
                m_new = jnp.maximum(m_old, jnp.max(s, axis=0, keepdims=True))
                p = jnp.exp2(s - m_new).astype(BF16)
                acc_scr[u] = jnp.exp2(m_old - m_new) * acc_scr[u] + _dot(vtb, p)
                m_scr[u, 0:1, :] = m_new
        bias_b = jnp.concatenate([bias[half:, half:]] * 2, axis=1)
        late = [slice(e * Q_BLOCK + half, (e + 1) * Q_BLOCK) for e in range(2)]
        for kvh in range(ATT_KV_HEADS):
            vtb = va_scr[kvh, :, pl.ds(start + half, half)]
            for pr in range(pairs_per_kv):
                u = kvh * pairs_per_kv + pr
                s = s2_scr[kvh, 0:half, pr * Q_BLOCK:(pr + 1) * Q_BLOCK] + bias_b
                m_old = jnp.concatenate([m_scr[u, 0:1, sl] for sl in late], axis=1)
                m_new = jnp.maximum(m_old, jnp.max(s, axis=0, keepdims=True))
                p = jnp.exp2(s - m_new).astype(BF16)
                alpha = jnp.exp2(m_old - m_new)
                pv = _dot(vtb, p)
                for e, sl in enumerate(late):
                    es = slice(e * half, (e + 1) * half)
                    acc_scr[u, :, sl] = alpha[:, es] * acc_scr[u, :, sl] + pv[:, es]
                    m_scr[u, 0:1, sl] = m_new[:, es]

    diagonal_block()
    n_rest = n_sb - 1

    def body(t, carry):
        scores(1 + 2 * t, s_scr)
        scores(2 + 2 * t, s2_scr)
        softmax_pv(1 + 2 * t, s_scr)
        softmax_pv(2 + 2 * t, s2_scr)
        return carry

    lax.fori_loop(0, n_rest // 2, body, 0)

    @pl.when(n_rest % 2 == 1)
    def _():
        scores(n_sb - 1, s_scr)
        softmax_pv(n_sb - 1, s_scr)

    for u in range(ATT_HEADS // 2):
        acc = acc_scr[u]
        o = acc[0:HEAD_DIM] * (1.0 / acc[HEAD_DIM:HEAD_DIM + 1])
        for e in range(2):
            hs = slice((2 * u + e) * HEAD_DIM, (2 * u + e + 1) * HEAD_DIM)
            o_ref[0, hs, :] = (o[:, e * Q_BLOCK:(e + 1) * Q_BLOCK] * g_ref[0, hs, :]).astype(BF16)


def _attn_prompt(qt, kt, vt, gate_t, batch, seq):
    nqb = seq // Q_BLOCK
    assert seq % K_SUPER == 0 and Q_BLOCK == K_SUPER
    n_sb_max = min(seq // K_SUPER, W_MAX // K_SUPER + 1)
    n_tab = Q_PER_SUPER * n_sb_max
    qblk = lambda b, i: (b, 0, i)
    per_b = lambda b, i: (b, 0, 0)
    return pl.pallas_call(
        functools.partial(_attn_prompt_kernel, n_tab=n_tab, n_sb_max=n_sb_max),
        grid=(batch, nqb),
        in_specs=[
            pl.BlockSpec((1, ATT_INNER, Q_BLOCK), qblk),
            pl.BlockSpec((1, KV_DIM, seq), per_b),
            pl.BlockSpec((1, KV_DIM, seq), per_b),
            pl.BlockSpec((1, ATT_INNER, Q_BLOCK), qblk),
        ],
        out_specs=pl.BlockSpec((1, ATT_INNER, Q_BLOCK), qblk),
        out_shape=jax.ShapeDtypeStruct((batch, ATT_INNER, seq), BF16),
        scratch_shapes=[
            pltpu.VMEM((n_tab, K_SUPER, Q_BLOCK), F32),
            pltpu.VMEM((ATT_KV_HEADS, seq, HEAD_DIM), BF16),
            pltpu.VMEM((ATT_KV_HEADS, ACC_ROWS, seq), BF16),
            pltpu.VMEM((ATT_KV_HEADS, HEAD_DIM, ATT_GQ * Q_BLOCK), BF16),
            pltpu.VMEM((ATT_HEADS // 2, SUBLANES, 2 * Q_BLOCK), F32),
            pltpu.VMEM((ATT_HEADS // 2, ACC_ROWS, 2 * Q_BLOCK), F32),
            pltpu.VMEM((ATT_KV_HEADS, K_SUPER, ATT_GQ * Q_BLOCK), F32),
            pltpu.VMEM((ATT_KV_HEADS, K_SUPER, ATT_GQ * Q_BLOCK), F32),
        ],
        compiler_params=pltpu.CompilerParams(
            dimension_semantics=("arbitrary", "arbitrary"), vmem_limit_bytes=VMEM_LIMIT),
        name="attn_prompt",
    )(qt, kt, vt, gate_t)


def _attn_sample_kernel(q_ref, knt_ref, vnt_ref, g_ref, ckt_ref, cvt_ref, o_ref, kot_ref, vot_ref,
                        bc_scr, bn_scr, *, ts):
    win = ckt_ref.shape[2]
    tp = SAMPLE_PAD
    rows = ATT_GQ * tp

    @pl.when(pl.program_id(0) == 0)
    def _():
        t_c = lax.broadcasted_iota(jnp.int32, (rows, win), 0) % tp
        j_c = lax.broadcasted_iota(jnp.int32, (rows, win), 1)
        w_c = _multiplicity(win + t_c - j_c)
        bc_scr[...] = jnp.where(w_c > 0.0, jnp.log2(jnp.maximum(w_c, 1.0)), NEG)
        t_n = lax.broadcasted_iota(jnp.int32, (rows, tp), 0) % tp
        j_n = lax.broadcasted_iota(jnp.int32, (rows, tp), 1)
        w_n = jnp.where(j_n < ts, _multiplicity(t_n - j_n), 0.0)
        bn_scr[...] = jnp.where(w_n > 0.0, jnp.log2(jnp.maximum(w_n, 1.0)), NEG)

    for kvh in range(ATT_KV_HEADS):
        c0 = kvh * ATT_GQ * HEAD_DIM
        qh = jnp.concatenate(
            [q_ref[:, c0 + g * HEAD_DIM:c0 + (g + 1) * HEAD_DIM] for g in range(ATT_GQ)], axis=0)
        hs = slice(kvh * HEAD_DIM, (kvh + 1) * HEAD_DIM)
        s_c = _dot(qh, ckt_ref[0, hs, :].astype(BF16)) + bc_scr[...]
        s_n = _dot(qh, knt_ref[0, hs, :].astype(BF16)) + bn_scr[...]
        m = jnp.maximum(jnp.max(s_c, axis=-1, keepdims=True), jnp.max(s_n, axis=-1, keepdims=True))
        p_c = jnp.exp2(s_c - m)
        p_n = jnp.exp2(s_n - m)
        den = jnp.sum(p_c, axis=-1, keepdims=True) + jnp.sum(p_n, axis=-1, keepdims=True)
        num = (_dot_nt(p_c.astype(BF16), cvt_ref[0, hs, :].astype(BF16))
               + _dot_nt(p_n.astype(BF16), vnt_ref[0, hs, :].astype(BF16)))
        o = num / den
        o = jnp.concatenate([o[g * tp:(g + 1) * tp] for g in range(ATT_GQ)], axis=1)
        csl = slice(c0, c0 + ATT_GQ * HEAD_DIM)
        o_ref[:, csl] = (o * g_ref[:, csl]).astype(BF16)
    kot_ref[0, :, 0:win - ts] = ckt_ref[0, :, ts:win]
    kot_ref[0, :, win - ts:win] = knt_ref[0, :, 0:ts]
    vot_ref[0, :, 0:win - ts] = cvt_ref[0, :, ts:win]
    vot_ref[0, :, win - ts:win] = vnt_ref[0, :, 0:ts]


def _attn_sample(q, knt, vnt, gate, ckt, cvt, ts):
    batch, _, win = ckt.shape
    tp = SAMPLE_PAD
    blk = lambda b: (b, 0)
    per_b = lambda b: (b, 0, 0)
    return pl.pallas_call(
        functools.partial(_attn_sample_kernel, ts=ts),
        grid=(batch,),
        in_specs=[
            pl.BlockSpec((tp, ATT_INNER), blk),
            pl.BlockSpec((1, KV_DIM, tp), per_b),
            pl.BlockSpec((1, KV_DIM, tp), per_b),
            pl.BlockSpec((tp, ATT_INNER), blk),
            pl.BlockSpec((1, KV_DIM, win), per_b),
            pl.BlockSpec((1, KV_DIM, win), per_b),
        ],
        out_specs=[
            pl.BlockSpec((tp, ATT_INNER), blk),
            pl.BlockSpec((1, KV_DIM, win), per_b),
            pl.BlockSpec((1, KV_DIM, win), per_b),
        ],
        out_shape=[
            jax.ShapeDtypeStruct((batch * tp, ATT_INNER), BF16),
            jax.ShapeDtypeStruct((batch, KV_DIM, win), F32),
            jax.ShapeDtypeStruct((batch, KV_DIM, win), F32),
        ],
        scratch_shapes=[pltpu.VMEM((ATT_GQ * tp, win), F32), pltpu.VMEM((ATT_GQ * tp, tp), F32)],
        compiler_params=pltpu.CompilerParams(
            dimension_semantics=("arbitrary",), vmem_limit_bytes=VMEM_LIMIT),
        name="attn_sample",
    )(q, knt, vnt, gate, ckt, cvt)


def _group_norm(yg, nw):
    ms = jnp.mean(yg * yg, axis=-1, keepdims=True)
    return (yg * lax.rsqrt(ms + EPS) * nw).astype(BF16)


def _outproj_kernel(x_ref, ys_ref, yat_ref, nw_ref, w_ref, o_ref):
    acc = x_ref[...]
    for g in range(SSD_GROUPS):
        gs = slice(g * HALF_INNER, (g + 1) * HALF_INNER)
        acc = acc + _dot(_group_norm(ys_ref[:, gs], nw_ref[:, gs]), w_ref[gs, :])
    ya = yat_ref[0].astype(F32).T.astype(BF16)
    o_ref[...] = acc + _dot(ya, w_ref[SSD_INNER:, :])


def _out_proj(x2d, y_ssd, y_att_t, nw, w_o, tm):
    rows = x2d.shape[0]
    per_seq = y_att_t.shape[2] // tm
    row = lambda i: (i, 0)
    const = lambda i: (0, 0)
    return pl.pallas_call(
        _outproj_kernel,
        grid=(rows // tm,),
        in_specs=[
            pl.BlockSpec((tm, D_MODEL), row),
            pl.BlockSpec((tm, SSD_INNER), lambda i: (i % per_seq, i // per_seq)),
            pl.BlockSpec((1, ATT_INNER, tm), lambda i: (i // per_seq, 0, i % per_seq)),
            pl.BlockSpec((1, SSD_INNER), const),
            pl.BlockSpec((SSD_INNER + ATT_INNER, D_MODEL), const, pipeline_mode=pl.Buffered(1)),
        ],
        out_specs=pl.BlockSpec((tm, D_MODEL), row),
        out_shape=jax.ShapeDtypeStruct((rows, D_MODEL), F32),
        compiler_params=pltpu.CompilerParams(
            dimension_semantics=("arbitrary",), vmem_limit_bytes=VMEM_LIMIT),
        name="out_proj",
    )(x2d, y_ssd, y_att_t, nw, w_o)


def _wconv_out_kernel(x_ref, ys_ref, ya_ref, nw_ref, w_ref, wb_ref, o_ref):
    j = pl.program_id(0)
    wb = w_ref[...].astype(BF16)
    wb_ref[...] = wb
    y = jnp.where(j < SSD_GROUPS, _group_norm(ys_ref[...], nw_ref[...]), ya_ref[...])
    part = _dot(y, wb)
    bs, ts, _ = x_ref.shape

    @pl.when(j == 0)
    def _():
        for b in range(bs):
            o_ref[b] = x_ref[b] + part[b * SAMPLE_PAD:b * SAMPLE_PAD + ts]

    @pl.when(j > 0)
    def _():
        for b in range(bs):
            o_ref[b] += part[b * SAMPLE_PAD:b * SAMPLE_PAD + ts]


def _wconv_out(x_s, y_ssd, y_att, nw, w_f32):
    rows = x_s.shape[0] * SAMPLE_PAD
    n_ssd = SSD_GROUPS
    n_chunks = (SSD_INNER + ATT_INNER) // WO_CHUNK
    const = lambda j: (0, 0)
    ssd_chunk = lambda j: (0, jnp.minimum(j, n_ssd - 1))
    return pl.pallas_call(
        _wconv_out_kernel,
        grid=(n_chunks,),
        in_specs=[
            pl.BlockSpec(x_s.shape, lambda j: (0, 0, 0)),
            pl.BlockSpec((rows, WO_CHUNK), ssd_chunk),
            pl.BlockSpec((rows, WO_CHUNK), lambda j: (0, jnp.maximum(j - n_ssd, 0))),
            pl.BlockSpec((1, WO_CHUNK), ssd_chunk),
            pl.BlockSpec((WO_CHUNK, D_MODEL), lambda j: (j, 0)),
        ],
        out_specs=[
            pl.BlockSpec((WO_CHUNK, D_MODEL), lambda j: (j, 0)),
            pl.BlockSpec(x_s.shape, lambda j: (0, 0, 0)),
        ],
        out_shape=[
            jax.ShapeDtypeStruct((SSD_INNER + ATT_INNER, D_MODEL), BF16),
            jax.ShapeDtypeStruct(x_s.shape, F32),
        ],
        compiler_params=pltpu.CompilerParams(
            dimension_semantics=("arbitrary",), vmem_limit_bytes=VMEM_LIMIT),
        name="wconv_out",
    )(x_s, y_ssd, y_att, nw, w_f32)


def _rope_tables(pos):
    n = pos.shape[0]
    half = ROT_DIM // 2
    inv = ROPE_THETA ** (-np.arange(0, ROT_DIM, 2, dtype=np.float64) / ROT_DIM)
    ang = pos.astype(np.float64)[:, None] * inv[None, :]
    cos, sin = np.cos(ang), np.sin(ang)
    rest = HEAD_DIM - ROT_DIM
    zh = np.zeros((n, half))
    cos_h = np.concatenate([cos, cos, np.ones((n, rest))], axis=1)
    sa_h = np.concatenate([zh, sin, np.zeros((n, rest))], axis=1)
    sb_h = np.concatenate([-sin, zh, np.zeros((n, rest))], axis=1)
    rep = LANES // HEAD_DIM
    return tuple(jnp.asarray(np.tile(t, (1, rep)), dtype=F32) for t in (cos_h, sa_h, sb_h))


def _lane_pad(v, n=LANES):
    return jnp.pad(v, (0, n - v.shape[0])).reshape(1, n)


def _to_cache(xt, batch, seq):
    return xt.reshape(1, batch, ATT_KV_HEADS, HEAD_DIM, seq).transpose(0, 1, 4, 2, 3)


def kernel(x_prompt, x_sample, cache_k, cache_v, state_conv, state_ssm, norm_w, w_in, conv_w,
           conv_b, dt_bias, a_log, d_skip, ssd_norm_w, q_norm_w, k_norm_w, w_out):
    bp, tp_, _ = x_prompt.shape
    bs, ts, _ = x_sample.shape
    depth = w_in.shape[0]
    assert depth == 1 and tp_ % SSD_CHUNK == 0 and ts <= SAMPLE_PAD and ts >= CONV_W - 1
    l = 0
    win = cache_k.shape[2]

    nw = norm_w[l].reshape(1, D_MODEL)
    rep = LANES // HEAD_DIM
    qnw = jnp.tile(q_norm_w[l], rep).reshape(1, LANES)
    knw = jnp.tile(k_norm_w[l], rep).reshape(1, LANES)
    cw = jnp.pad(conv_w[l], ((0, SUBLANES - CONV_W), (0, 0)))
    cb = conv_b[l].reshape(1, CONV_DIM)
    dtb = _lane_pad(dt_bias[l])
    alog = _lane_pad(a_log[l])
    dexp = jnp.repeat(d_skip[l], SSD_HEADDIM).reshape(1, SSD_INNER)
    snw = ssd_norm_w[l].reshape(1, SSD_INNER)

    pad = SAMPLE_PAD
    tabs_s = _rope_tables(PAST_LEN + np.arange(bs * pad) % pad)
    w_t, proj_s = _wconv_in(x_sample, nw, jnp.swapaxes(w_in[l], 0, 1))
    zs, xbc, dtr, q, kt, vt, gs, _ = _sections(proj_s, qnw, knw, tabs_s)
    cprev = jnp.pad(state_conv[l], ((0, 0), (SUBLANES - (CONV_W - 1), 0), (0, 0)))
    y_ssd, h_s = _ssd(xbc, dtr, zs, cprev, state_ssm[l].reshape(bs, SSD_INNER, SSD_STATE),
                      cw, cb, dtb, alog, dexp, bs, pad, ts)
    knt = kt.reshape(KV_DIM, bs, pad).transpose(1, 0, 2)
    vnt = vt.reshape(KV_DIM, bs, pad).transpose(1, 0, 2)
    ckt = cache_k[l].transpose(0, 2, 3, 1).reshape(bs, KV_DIM, win)
    cvt = cache_v[l].transpose(0, 2, 3, 1).reshape(bs, KV_DIM, win)
    y_att, kot, vot = _attn_sample(q, knt, vnt, gs, ckt, cvt, ts)
    w_o, y_s = _wconv_out(x_sample, y_ssd, y_att, snw, w_out[l])
    k_s = _to_cache(kot, bs, win)
    v_s = _to_cache(vot, bs, win)
    c_s = xbc.reshape(bs, pad, CONV_DIM)[:, ts - (CONV_W - 1):ts][None]
    h_s = h_s.reshape(1, bs, SSD_HEADS, SSD_HEADDIM, SSD_STATE)

    tm = TM_IN
    xp2 = x_prompt.reshape(bp * tp_, D_MODEL)
    tabs = _rope_tables(np.arange(tp_))
    zs, xbc, dtr, q, kt, vt, gs, ctail = _in_proj(xp2, nw, w_t, qnw, knw, tabs, tm, tp_)
    y_ssd, h_p = _ssd(xbc, dtr, zs, jnp.zeros((bp, SUBLANES, CONV_DIM), F32),
                      jnp.zeros((bp, SSD_INNER, SSD_STATE), F32),
                      cw, cb, dtb, alog, dexp, bp, SSD_CHUNK, SSD_CHUNK)
    y_att = _attn_prompt(q, kt, vt, gs, bp, tp_)
    y_p = _out_proj(xp2, y_ssd, y_att, snw, w_o, TM_OUT).reshape(bp, tp_, D_MODEL)
    keep = min(W_MAX, tp_)
    k_p = _to_cache(kt, bp, tp_)[:, :, tp_ - keep:]
    v_p = _to_cache(vt, bp, tp_)[:, :, tp_ - keep:]
    c_p = ctail[:, SUBLANES - (CONV_W - 1):][None]
    h_p = h_p.reshape(1, bp, SSD_HEADS, SSD_HEADDIM, SSD_STATE)

    return (y_p, y_s, k_p, v_p, c_p, h_p, k_s, v_s, c_s, h_s)
```

```python
import functools

import jax
import jax.numpy as jnp
import numpy as np
from jax import lax
from jax.experimental import pallas as pl
from jax.experimental.pallas import tpu as pltpu

F32 = jnp.float32
BF16 = jnp.bfloat16

D_MODEL = 2048
SSD_HEADS = 16
SSD_HEADDIM = 64
SSD_INNER = SSD_HEADS * SSD_HEADDIM
SSD_GROUPS = 2
SSD_STATE = 128
CONV_W = 4
CONV_DIM = SSD_INNER + 2 * SSD_GROUPS * SSD_STATE
SSD_CHUNK = 128
ATT_HEADS = 16
ATT_KV_HEADS = 4
HEAD_DIM = 64
ATT_GQ = ATT_HEADS // ATT_KV_HEADS
ATT_INNER = ATT_HEADS * HEAD_DIM
KV_DIM = ATT_KV_HEADS * HEAD_DIM
ROT_DIM = HEAD_DIM // 4
ROPE_THETA = 500000.0
DILATED_BRANCHES = ((128, 1), (512, 4), (2048, 16))
W_MAX = 2048
PAST_LEN = 16384
EPS = 1e-6

LANES = 128
SUBLANES = 8
Q_BLOCK = 256
K_SUPER = 256
Q_PER_SUPER = K_SUPER // Q_BLOCK
ONES_ROWS = 16
ACC_ROWS = HEAD_DIM + ONES_ROWS
CONV_COLS = 256
TM_IN = 512
TM_OUT = 512
SSD_SEQS = 4
SAMPLE_PAD = 16
NEG = -1e30
VMEM_LIMIT = 58 * 1024 * 1024

Z0 = 0
X0 = Z0 + SSD_INNER
DT0 = X0 + CONV_DIM
Q0 = DT0 + SSD_HEADS
K0 = Q0 + ATT_INNER
V0 = K0 + KV_DIM
G0 = V0 + KV_DIM
W_ROWS = G0 + ATT_INNER
W_CHUNK = 1024
HALF_INNER = SSD_INNER // SSD_GROUPS
WO_CHUNK = HALF_INNER
LOG2E = 1.4426950408889634
Q_SCALE = HEAD_DIM ** -0.5 * LOG2E


def _dot(a, b):
    return jnp.dot(a, b, preferred_element_type=F32)


def _dot_nt(a, b):
    return lax.dot_general(a, b, (((1,), (1,)), ((), ())), preferred_element_type=F32)


def _split3(x):
    hi = x.astype(BF16)
    r1 = x - hi.astype(F32)
    mid = r1.astype(BF16)
    lo = (r1 - mid.astype(F32)).astype(BF16)
    return hi, mid, lo


def _dot_exact_rhs(x, m):
    hi, mid, lo = _split3(x)
    return _dot(hi, m) + _dot(mid, m) + _dot(lo, m)


def _dot_wide_rhs(x, m):
    hi = x.astype(BF16)
    lo = (x - hi.astype(F32)).astype(BF16)
    return _dot(hi, m) + _dot(lo, m)


def _dot_exact_lhs(m, x):
    hi, mid, lo = _split3(x)
    return _dot(m, hi) + _dot(m, mid) + _dot(m, lo)


def _silu(x):
    hx = 0.5 * x
    return hx * jnp.tanh(hx) + hx


def _multiplicity(d):
    w = jnp.zeros(d.shape, F32)
    for window, dil in DILATED_BRANCHES:
        hit = (d >= 0) & (d <= window) & (lax.rem(d, dil) == 0)
        w = w + jnp.where(hit, 1.0, 0.0)
    return w


def _norm_rope(y, nw, cos, sa, sb, scale):
    lane = lax.broadcasted_iota(jnp.int32, (1, LANES), 1)
    first = lane < HEAD_DIM
    y2 = y * y
    s_lo = jnp.sum(jnp.where(first, y2, 0.0), axis=-1, keepdims=True)
    s_hi = jnp.sum(jnp.where(first, 0.0, y2), axis=-1, keepdims=True)
    ms = jnp.where(first, s_lo, s_hi) * (1.0 / HEAD_DIM)
    yn = y * lax.rsqrt(ms + EPS) * nw
    half = ROT_DIM // 2
    rot = yn * cos + pltpu.roll(yn, half, 1) * sa + pltpu.roll(yn, LANES - half, 1) * sb
    return rot * scale


def _causal_conv_silu(load, cw_ref, cb_ref, store):
    for c0 in range(0, CONV_DIM, CONV_COLS):
        cs = slice(c0, c0 + CONV_COLS)
        xe = load(cs)
        x2 = pltpu.roll(xe, 2, 0)
        even = cw_ref[3:4, cs] * xe + cw_ref[1:2, cs] * x2
        odd = cw_ref[2:3, cs] * xe + cw_ref[0:1, cs] * x2
        store(cs, _silu((even + pltpu.roll(odd, 1, 0))[SUBLANES:, :] + cb_ref[:, cs]))


def _pre_norm(x_ref, nw_ref):
    x = x_ref[...]
    ms = jnp.mean(x * x, axis=-1, keepdims=True)
    return (x * lax.rsqrt(ms + EPS) * nw_ref[...]).astype(BF16)


def _inproj_kernel(x_ref, nw_ref, w_ref, *refs):
    hn = _pre_norm(x_ref, nw_ref)

    def proj(a, b):
        return _dot_nt(hn, w_ref[a:b, :])

    _emit_sections(proj, x_ref.shape[0], *refs)


def _emit_sections(proj, tm, qnw_ref, knw_ref, cos_ref, sa_ref, sb_ref,
                   zs_ref, xbc_ref, dt_ref, q_ref, kt_ref, vt_ref, gs_ref, ctail_ref):
    cos, sa, sb = cos_ref[...], sa_ref[...], sb_ref[...]
    q = proj(Q0, K0)
    for c in range(ATT_INNER // LANES):
        sl = slice(c * LANES, (c + 1) * LANES)
        qn = _norm_rope(q[:, sl], qnw_ref[...], cos, sa, sb, Q_SCALE)
        if len(q_ref.shape) == 3:
            q_ref[0, sl, :] = qn.T.astype(BF16)
        else:
            q_ref[:, sl] = qn.astype(BF16)
    k = proj(K0, V0)
    kn = [_norm_rope(k[:, c * LANES:(c + 1) * LANES], knw_ref[...], cos, sa, sb, 1.0)
          for c in range(KV_DIM // LANES)]
    kt_ref[0] = jnp.concatenate(kn, axis=1).T

    xbc = proj(X0, DT0)
    xbc_ref[...] = xbc
    ctail_ref[0] = xbc[tm - SUBLANES:tm, :]
    vt_ref[0] = proj(V0, G0).T
    zs_ref[...] = _silu(proj(Z0, X0))
    gs = _silu(proj(G0, W_ROWS))
    if len(gs_ref.shape) == 3:
        gs_ref[0] = gs.T
    else:
        gs_ref[...] = gs
    dt_ref[...] = proj(DT0, DT0 + LANES)


def _in_proj(x2d, nw, w_t, qnw, knw, tables, tm, seq):
    rows = x2d.shape[0]
    const = lambda i: (0, 0)
    row = lambda i: (i, 0)
    sec_in, out_specs, out_shape = _section_specs(rows, tm, seq, tables[0].shape[0] // tm, True)
    return pl.pallas_call(
        _inproj_kernel,
        grid=(rows // tm,),
        in_specs=[
            pl.BlockSpec((tm, D_MODEL), row),
            pl.BlockSpec((1, D_MODEL), const),
            pl.BlockSpec((W_ROWS, D_MODEL), const, pipeline_mode=pl.Buffered(1)),
        ] + sec_in,
        out_specs=out_specs,
        out_shape=out_shape,
        compiler_params=pltpu.CompilerParams(
            dimension_semantics=("arbitrary",), vmem_limit_bytes=VMEM_LIMIT),
        name="in_proj",
    )(x2d, nw, w_t, qnw, knw, *tables)


def _section_specs(rows, tm, seq, period, q_transposed):
    per_seq = seq // tm
    const = lambda i: (0, 0)
    row = lambda i: (i, 0)
    tab = lambda i: (i % period, 0)
    tr = lambda i: (i // per_seq, 0, i % per_seq)
    slab = lambda i: (i // per_seq, 0, 0)
    in_specs = [pl.BlockSpec((1, LANES), const)] * 2 + [pl.BlockSpec((tm, LANES), tab)] * 3
    row_outs = {0: (SSD_INNER, F32), 1: (CONV_DIM, F32), 2: (LANES, F32), 3: (ATT_INNER, BF16),
                6: (ATT_INNER, F32)}
    out_specs, out_shape = [], []
    for idx in range(7):
        if idx in (3, 6) and q_transposed:
            dt = row_outs[idx][1]
            out_specs.append(pl.BlockSpec((1, ATT_INNER, tm), tr))
            out_shape.append(jax.ShapeDtypeStruct((rows // seq, ATT_INNER, seq), dt))
        elif idx in row_outs:
            n, dt = row_outs[idx]
            out_specs.append(pl.BlockSpec((tm, n), row))
            out_shape.append(jax.ShapeDtypeStruct((rows, n), dt))
        else:
            out_specs.append(pl.BlockSpec((1, KV_DIM, tm), tr))
            out_shape.append(jax.ShapeDtypeStruct((rows // seq, KV_DIM, seq), F32))
    out_specs.append(pl.BlockSpec((1, SUBLANES, CONV_DIM), slab))
    out_shape.append(jax.ShapeDtypeStruct((rows // seq, SUBLANES, CONV_DIM), F32))
    return in_specs, out_specs, out_shape


def _wconv_in_kernel(x_ref, nw_ref, w_ref, wb_ref, p_ref, xpad_scr, hn_scr):
    @pl.when(pl.program_id(0) == 0)
    def _():
        bs, ts, _ = x_ref.shape
        xpad_scr[...] = jnp.zeros(xpad_scr.shape, F32)
        for b in range(bs):
            xpad_scr[b * SAMPLE_PAD:b * SAMPLE_PAD + ts, :] = x_ref[b]
        hn_scr[...] = _pre_norm(xpad_scr, nw_ref)

    row = pl.program_id(0) * W_CHUNK + lax.broadcasted_iota(jnp.int32, (W_CHUNK, 1), 0)
    wb = jnp.where(row < W_ROWS, w_ref[...], 0.0).astype(BF16)
    wb_ref[...] = wb
    p_ref[0] = _dot_nt(hn_scr[...], wb)


def _wconv_in(x_s, nw, w_f32):
    rows = x_s.shape[0] * SAMPLE_PAD
    n_chunks = pl.cdiv(W_ROWS, W_CHUNK)
    const = lambda j: (0, 0)
    return pl.pallas_call(
        _wconv_in_kernel,
        grid=(n_chunks,),
        in_specs=[
            pl.BlockSpec(x_s.shape, lambda j: (0, 0, 0)),
            pl.BlockSpec((1, D_MODEL), const),
            pl.BlockSpec((W_CHUNK, D_MODEL), lambda j: (j, 0)),
        ],
        out_specs=[
            pl.BlockSpec((W_CHUNK, D_MODEL), lambda j: (j, 0)),
            pl.BlockSpec((1, rows, W_CHUNK), lambda j: (j, 0, 0)),
        ],
        out_shape=[
            jax.ShapeDtypeStruct((W_ROWS, D_MODEL), BF16),
            jax.ShapeDtypeStruct((n_chunks, rows, W_CHUNK), F32),
        ],
        scratch_shapes=[pltpu.VMEM((rows, D_MODEL), F32), pltpu.VMEM((rows, D_MODEL), BF16)],
        compiler_params=pltpu.CompilerParams(
            dimension_semantics=("arbitrary",), vmem_limit_bytes=VMEM_LIMIT),
        name="wconv_in",
    )(x_s, nw, w_f32)


def _sections_kernel(p_ref, *refs):
    *refs, tail_scr = refs
    pfull = jnp.concatenate([p_ref[j] for j in range(p_ref.shape[0])], axis=1)
    tail_scr[...] = pfull[:, Q0:W_ROWS]

    def proj(a, b):
        if a >= Q0:
            return tail_scr[:, a - Q0:b - Q0]
        return pfull[:, a:b]

    _emit_sections(proj, p_ref.shape[1], *refs)


def _sections(p, qnw, knw, tables):
    rows = p.shape[1]
    sec_in, out_specs, out_shape = _section_specs(rows, rows, rows, 1, False)
    return pl.pallas_call(
        _sections_kernel,
        grid=(1,),
        in_specs=[pl.BlockSpec(p.shape, lambda i: (0, 0, 0))] + sec_in,
        out_specs=out_specs,
        out_shape=out_shape,
        scratch_shapes=[pltpu.VMEM((rows, W_ROWS - Q0), F32)],
        compiler_params=pltpu.CompilerParams(
            dimension_semantics=("arbitrary",), vmem_limit_bytes=VMEM_LIMIT),
        name="sections",
    )(p, qnw, knw, *tables)


def _ssd_chain(s, xbc_ref, dtr_ref, z_ref, cprev_ref, h0_ref, cw_ref, cb_ref, dtb_ref, alog_ref,
               dexp_ref, y_ref, hout_ref, xext, ht, xc_scr, lin, valid):
    L = SSD_CHUNK
    c = pl.program_id(1)
    last = pl.num_programs(1) - 1

    @pl.when(c == 0)
    def _():
        xext[s, 0:SUBLANES, :] = cprev_ref[s]
        for g in range(SSD_GROUPS):
            ht[s, g] = h0_ref[s, g * HALF_INNER:(g + 1) * HALF_INNER, :].T

    def pad_rows(v):
        if lin == L:
            return v
        return jnp.concatenate([v, jnp.zeros((L - lin, v.shape[1]), v.dtype)], axis=0)

    lane = lax.broadcasted_iota(jnp.int32, (1, LANES), 1)
    rowid = lax.broadcasted_iota(jnp.int32, (L, 1), 0)
    dt = jax.nn.softplus(pad_rows(dtr_ref[s]) + dtb_ref[...])
    dt = jnp.where((lane < SSD_HEADS) & (rowid < valid), dt, 0.0)
    dta = dt * (-jnp.exp(alog_ref[...]) * LOG2E)

    r2 = lax.broadcasted_iota(jnp.int32, (L, L), 0)
    c2 = lax.broadcasted_iota(jnp.int32, (L, L), 1)
    tri = r2 >= c2
    tri_b = jnp.where(tri, 1.0, 0.0).astype(BF16)
    cum = _dot_exact_lhs(tri_b, dta)
    cum_last = cum[L - 1:L, :]
    yield

    er = lax.broadcasted_iota(jnp.int32, (LANES, SSD_INNER), 0)
    ec = lax.broadcasted_iota(jnp.int32, (LANES, SSD_INNER), 1)
    expand = jnp.where(ec // SSD_HEADDIM == er, 1.0, 0.0).astype(BF16)
    ecum = _dot_wide_rhs(jnp.exp2(cum), expand)
    wexp = _dot_wide_rhs(jnp.exp2(cum_last - cum) * dt, expand)
    cd = _dot_exact_rhs(jnp.broadcast_to(jnp.exp2(cum_last), (SUBLANES, LANES)), expand)[0:1, :]
    cum_t = cum.T
    dt_t = dt.T
    yield

    if lin < L:
        xext[s, SUBLANES + lin:SUBLANES + L, :] = jnp.zeros((L - lin, CONV_DIM), F32)
    xext[s, SUBLANES:SUBLANES + lin, :] = xbc_ref[s]

    def load(cs):
        return xext[s, 0:SUBLANES + L, cs]

    def store(cs, v):
        xc_scr[s, :, cs] = v

    _causal_conv_silu(load, cw_ref, cb_ref, store)
    xext[s, 0:SUBLANES, :] = xext[s, L:L + SUBLANES, :]
    xc = xc_scr.at[s]
    xs = xc[:, :SSD_INNER]
    bm = xc[:, SSD_INNER:SSD_INNER + SSD_GROUPS * SSD_STATE]
    cm = xc[:, SSD_INNER + SSD_GROUPS * SSD_STATE:]
    bm_b = bm.astype(BF16)
    cm_b = cm.astype(BF16)
    xs_b = xs.astype(BF16)

    y_off = jnp.concatenate(
        [_dot(cm_b[:, g * SSD_STATE:(g + 1) * SSD_STATE], ht[s, g].astype(BF16))
         for g in range(SSD_GROUPS)], axis=1) * ecum
    cbs = [_dot_nt(cm_b[:, g * SSD_STATE:(g + 1) * SSD_STATE], bm_b[:, g * SSD_STATE:(g + 1) * SSD_STATE])
           for g in range(SSD_GROUPS)]
    yield

    lane_half = lane // HEAD_DIM
    heads_per_group = SSD_HEADS // SSD_GROUPS
    y_parts = []
    for g in range(SSD_GROUPS):
        for pr in range(heads_per_group // 2):
            col0 = g * HALF_INNER + pr * LANES
            xp = xs_b[:, col0:col0 + LANES]
            yp = jnp.zeros((L, LANES), F32)
            for e in range(2):
                hh = g * heads_per_group + pr * 2 + e
                decay = jnp.exp2(cum[:, hh:hh + 1] - cum_t[hh:hh + 1, :])
                sc = jnp.where(tri, cbs[g] * decay, 0.0) * dt_t[hh:hh + 1, :]
                xm = jnp.where(lane_half == e, xp, jnp.zeros_like(xp))
                yp = yp + _dot(sc.astype(BF16), xm)
            y_parts.append(yp)
            if pr % 2 == 1:
                yield
    y_diag = jnp.concatenate(y_parts, axis=1)

    xw = (xs * wexp).astype(BF16)
    for g in range(SSD_GROUPS):
        sl = slice(g * HALF_INNER, (g + 1) * HALF_INNER)
        bm_t = bm[:, g * SSD_STATE:(g + 1) * SSD_STATE].T.astype(BF16)
        ht[s, g] = ht[s, g] * cd[:, sl] + _dot(bm_t, xw[:, sl])
    yield

    y = (y_diag + y_off + dexp_ref[...] * xs) * pad_rows(z_ref[s])
    y_ref[s] = y[:lin]

    @pl.when(c == last)
    def _():
        hout_ref[s] = ht[s]


def _ssd_kernel(*refs, lin, valid):
    chains = [_ssd_chain(s, *refs, lin, valid) for s in range(SSD_SEQS)]
    while chains:
        alive = []
        for ch in chains:
            try:
                next(ch)
                alive.append(ch)
            except StopIteration:
                pass
        chains = alive


def _ssd(xbc, dtr, z, cprev, h0, cw, cb, dtb, alog, dexp, batch, lin, valid):
    rows = xbc.shape[0]
    seq = rows // batch
    nc = seq // lin
    g = SSD_SEQS
    assert batch % g == 0
    blk = lambda b, c: (b, c, 0)
    per_b = lambda b, c: (b, 0, 0)
    const = lambda b, c: (0, 0)
    y, hout = pl.pallas_call(
        functools.partial(_ssd_kernel, lin=lin, valid=valid),
        grid=(batch // g, nc),
        in_specs=[
            pl.BlockSpec((g, lin, CONV_DIM), blk),
            pl.BlockSpec((g, lin, LANES), blk),
            pl.BlockSpec((g, lin, SSD_INNER), blk),
            pl.BlockSpec((g, SUBLANES, CONV_DIM), per_b),
            pl.BlockSpec((g, SSD_INNER, SSD_STATE), per_b),
            pl.BlockSpec((SUBLANES, CONV_DIM), const),
            pl.BlockSpec((1, CONV_DIM), const),
            pl.BlockSpec((1, LANES), const),
            pl.BlockSpec((1, LANES), const),
            pl.BlockSpec((1, SSD_INNER), const),
        ],
        out_specs=[
            pl.BlockSpec((g, lin, SSD_INNER), blk),
            pl.BlockSpec((g, SSD_GROUPS, SSD_STATE, HALF_INNER), lambda b, c: (b, 0, 0, 0)),
        ],
        out_shape=[
            jax.ShapeDtypeStruct((batch, seq, SSD_INNER), F32),
            jax.ShapeDtypeStruct((batch, SSD_GROUPS, SSD_STATE, HALF_INNER), F32),
        ],
        scratch_shapes=[
            pltpu.VMEM((g, SUBLANES + SSD_CHUNK + SUBLANES, CONV_DIM), F32),
            pltpu.VMEM((g, SSD_GROUPS, SSD_STATE, HALF_INNER), F32),
            pltpu.VMEM((g, SSD_CHUNK, CONV_DIM), F32),
        ],
        compiler_params=pltpu.CompilerParams(
            dimension_semantics=("arbitrary", "arbitrary"), vmem_limit_bytes=VMEM_LIMIT),
        name="ssd",
    )(xbc.reshape(batch, seq, CONV_DIM), dtr.reshape(batch, seq, LANES),
      z.reshape(batch, seq, SSD_INNER), cprev, h0, cw, cb, dtb, alog, dexp)
    hout = hout.reshape(batch, SSD_GROUPS, SSD_STATE, SSD_HEADS // SSD_GROUPS, SSD_HEADDIM)
    hout = hout.transpose(0, 1, 3, 4, 2).reshape(batch, SSD_HEADS, SSD_HEADDIM, SSD_STATE)
    return y.reshape(rows, SSD_INNER), hout


def _attn_prompt_kernel(q_ref, kt_ref, vt_ref, g_ref, o_ref,
                        bias_scr, kh_scr, va_scr, qt_scr, m_scr, acc_scr, s_scr, s2_scr, *, n_tab, n_sb_max):
    b = pl.program_id(0)
    i = pl.program_id(1)
    seq = kt_ref.shape[2]
    pair_w = 2 * Q_BLOCK
    pairs_per_kv = ATT_GQ // 2

    @pl.when((b == 0) & (i == 0))
    def _():
        r = lax.broadcasted_iota(jnp.int32, (K_SUPER, Q_BLOCK), 0)
        c = lax.broadcasted_iota(jnp.int32, (K_SUPER, Q_BLOCK), 1)
        for tb in range(n_tab):
            w = _multiplicity(tb * Q_BLOCK + c - r)
            bias_scr[tb] = jnp.where(w > 0.0, jnp.log2(jnp.maximum(w, 1.0)), NEG)
        va_scr[:, HEAD_DIM:, :] = jnp.ones((ATT_KV_HEADS, ONES_ROWS, seq), BF16)

    @pl.when(i == 0)
    def _():
        for kvh in range(ATT_KV_HEADS):
            hs = slice(kvh * HEAD_DIM, (kvh + 1) * HEAD_DIM)
            va_scr[kvh, 0:HEAD_DIM, :] = vt_ref[0, hs, :].astype(BF16)
        for cidx in range(seq // K_SUPER):
            rows = slice(cidx * K_SUPER, (cidx + 1) * K_SUPER)
            kc = kt_ref[0, :, rows].T
            for kvh in range(ATT_KV_HEADS):
                hs = slice(kvh * HEAD_DIM, (kvh + 1) * HEAD_DIM)
                kh_scr[kvh, rows, :] = kc[:, hs].astype(BF16)

    for kvh in range(ATT_KV_HEADS):
        qt_scr[kvh] = jnp.concatenate(
            [q_ref[0, (kvh * ATT_GQ + g) * HEAD_DIM:(kvh * ATT_GQ + g + 1) * HEAD_DIM, :]
             for g in range(ATT_GQ)], axis=1)
    m_scr[...] = jnp.full(m_scr.shape, NEG, F32)
    acc_scr[...] = jnp.zeros(acc_scr.shape, F32)
    par = i % Q_PER_SUPER
    j_last = i // Q_PER_SUPER
    n_sb = jnp.minimum(j_last + 1, n_sb_max)

    def key_start(dl):
        return pl.multiple_of((j_last - dl) * K_SUPER, K_SUPER)

    def scores(dl, dst):
        for kvh in range(ATT_KV_HEADS):
            dst[kvh] = _dot(kh_scr[kvh, pl.ds(key_start(dl), K_SUPER), :], qt_scr[kvh])

    def softmax_pv(dl, src):
        start = key_start(dl)
        bias = bias_scr[par + Q_PER_SUPER * dl]
        bias2 = jnp.concatenate([bias, bias], axis=1)
        for kvh in range(ATT_KV_HEADS):
            vtb = va_scr[kvh, :, pl.ds(start, K_SUPER)]
            for pr in range(pairs_per_kv):
                u = kvh * pairs_per_kv + pr
                s = src[kvh, :, pr * pair_w:(pr + 1) * pair_w] + bias2
                m_old = m_scr[u, 0:1, :]
                m_new = jnp.maximum(m_old, jnp.max(s, axis=0, keepdims=True))
                p = jnp.exp2(s - m_new).astype(BF16)
                acc_scr[u] = jnp.exp2(m_old - m_new) * acc_scr[u] + _dot(vtb, p)
                m_scr[u, 0:1, :] = m_new

    def diagonal_block():
        half = K_SUPER // 2
        start = key_start(0)
        bias = bias_scr[0]
        for kvh in range(ATT_KV_HEADS):
            s_scr[kvh, 0:half, :] = _dot(kh_scr[kvh, pl.ds(start, half), :], qt_scr[kvh])
        for kvh in range(ATT_KV_HEADS):
            q_late = jnp.concatenate(
                [qt_scr[kvh, :, g * Q_BLOCK + half:(g + 1) * Q_BLOCK] for g in range(ATT_GQ)], axis=1)
            s2_scr[kvh, 0:half, 0:ATT_GQ * half] = _dot(
                kh_scr[kvh, pl.ds(start + half, half), :], q_late)
        bias_a = jnp.concatenate([bias[0:half, :]] * 2, axis=1)
        for kvh in range(ATT_KV_HEADS):
            vtb = va_scr[kvh, :, pl.ds(start, half)]
            for pr in range(pairs_per_kv):
                u = kvh * pairs_per_kv + pr
                s = s_scr[kvh, 0:half, pr * pair_w:(pr + 1) * pair_w] + bias_a
                m_old = m_scr[u, 0:1, :]
                m_new = jnp.maximum(m_old, jnp.max(s, axis=0, keepdims=True))
                p = jnp.exp2(s - m_new).astype(BF16)
                acc_scr[u] = jnp.exp2(m_old - m_new) * acc_scr[u] + _dot(vtb, p)
                m_scr[u, 0:1, :] = m_new
        bias_b = jnp.concatenate([bias[half:, half:]] * 2, axis=1)
        late = [slice(e * Q_BLOCK + half, (e + 1) * Q_BLOCK) for e in range(2)]
        for kvh in range(ATT_KV_HEADS):
            vtb = va_scr[kvh, :, pl.ds(start + half, half)]
            for pr in range(pairs_per_kv):
                u = kvh * pairs_per_kv + pr
                s = s2_scr[kvh, 0:half, pr * Q_BLOCK:(pr + 1) * Q_BLOCK] + bias_b
                m_old = jnp.concatenate([m_scr[u, 0:1, sl] for sl in late], axis=1)
                m_new = jnp.maximum(m_old, jnp.max(s, axis=0, keepdims=True))
                p = jnp.exp2(s - m_new).astype(BF16)
                alpha = jnp.exp2(m_old - m_new)
                pv = _dot(vtb, p)
                for e, sl in enumerate(late):
                    es = slice(e * half, (e + 1) * half)
                    acc_scr[u, :, sl] = alpha[:, es] * acc_scr[u, :, sl] + pv[:, es]
                    m_scr[u, 0:1, sl] = m_new[:, es]

    diagonal_block()
    n_rest = n_sb - 1

    def body(t, carry):
        scores(1 + 2 * t, s_scr)
        scores(2 + 2 * t, s2_scr)
        softmax_pv(1 + 2 * t, s_scr)
        softmax_pv(2 + 2 * t, s2_scr)
        return carry

    lax.fori_loop(0, n_rest // 2, body, 0)

    @pl.when(n_rest % 2 == 1)
    def _():
        scores(n_sb - 1, s_scr)
        softmax_pv(n_sb - 1, s_scr)

    for u in range(ATT_HEADS // 2):
        acc = acc_scr[u]
        o = acc[0:HEAD_DIM] * (1.0 / acc[HEAD_DIM:HEAD_DIM + 1])
        for e in range(2):
            hs = slice((2 * u + e) * HEAD_DIM, (2 * u + e + 1) * HEAD_DIM)
            o_ref[0, hs, :] = (o[:, e * Q_BLOCK:(e + 1) * Q_BLOCK] * g_ref[0, hs, :]).astype(BF16)


def _attn_prompt(qt, kt, vt, gate_t, batch, seq):
    nqb = seq // Q_BLOCK
    assert seq % K_SUPER == 0 and Q_BLOCK == K_SUPER
    n_sb_max = min(seq // K_SUPER, W_MAX // K_SUPER + 1)
    n_tab = Q_PER_SUPER * n_sb_max
    qblk = lambda b, i: (b, 0, i)
    per_b = lambda b, i: (b, 0, 0)
    return pl.pallas_call(
        functools.partial(_attn_prompt_kernel, n_tab=n_tab, n_sb_max=n_sb_max),
        grid=(batch, nqb),
        in_specs=[
            pl.BlockSpec((1, ATT_INNER, Q_BLOCK), qblk),
            pl.BlockSpec((1, KV_DIM, seq), per_b),
            pl.BlockSpec((1, KV_DIM, seq), per_b),
            pl.BlockSpec((1, ATT_INNER, Q_BLOCK), qblk),
        ],
        out_specs=pl.BlockSpec((1, ATT_INNER, Q_BLOCK), qblk),
        out_shape=jax.ShapeDtypeStruct((batch, ATT_INNER, seq), BF16),
        scratch_shapes=[
            pltpu.VMEM((n_tab, K_SUPER, Q_BLOCK), F32),
            pltpu.VMEM((ATT_KV_HEADS, seq, HEAD_DIM), BF16),
            pltpu.VMEM((ATT_KV_HEADS, ACC_ROWS, seq), BF16),
            pltpu.VMEM((ATT_KV_HEADS, HEAD_DIM, ATT_GQ * Q_BLOCK), BF16),
            pltpu.VMEM((ATT_HEADS // 2, SUBLANES, 2 * Q_BLOCK), F32),
            pltpu.VMEM((ATT_HEADS // 2, ACC_ROWS, 2 * Q_BLOCK), F32),
            pltpu.VMEM((ATT_KV_HEADS, K_SUPER, ATT_GQ * Q_BLOCK), F32),
            pltpu.VMEM((ATT_KV_HEADS, K_SUPER, ATT_GQ * Q_BLOCK), F32),
        ],
        compiler_params=pltpu.CompilerParams(
            dimension_semantics=("arbitrary", "arbitrary"), vmem_limit_bytes=VMEM_LIMIT),
        name="attn_prompt",
    )(qt, kt, vt, gate_t)


def _attn_sample_kernel(q_ref, knt_ref, vnt_ref, g_ref, ckt_ref, cvt_ref, o_ref, kot_ref, vot_ref, *, ts):
    win = ckt_ref.shape[2]
    tp = SAMPLE_PAD
    rows = ATT_GQ * tp
    t_c = lax.broadcasted_iota(jnp.int32, (rows, win), 0) % tp
    j_c = lax.broadcasted_iota(jnp.int32, (rows, win), 1)
    w_c = _multiplicity(win + t_c - j_c)
    t_n = lax.broadcasted_iota(jnp.int32, (rows, tp), 0) % tp
    j_n = lax.broadcasted_iota(jnp.int32, (rows, tp), 1)
    w_n = jnp.where(j_n < ts, _multiplicity(t_n - j_n), 0.0)
    for kvh in range(ATT_KV_HEADS):
        c0 = kvh * ATT_GQ * HEAD_DIM
        qh = jnp.concatenate(
            [q_ref[:, c0 + g * HEAD_DIM:c0 + (g + 1) * HEAD_DIM] for g in range(ATT_GQ)], axis=0)
        hs = slice(kvh * HEAD_DIM, (kvh + 1) * HEAD_DIM)
        s_c = jnp.where(w_c > 0.0, _dot(qh, ckt_ref[0, hs, :].astype(BF16)), NEG)
        s_n = jnp.where(w_n > 0.0, _dot(qh, knt_ref[0, hs, :].astype(BF16)), NEG)
        m = jnp.maximum(jnp.max(s_c, axis=-1, keepdims=True), jnp.max(s_n, axis=-1, keepdims=True))
        p_c = jnp.exp2(s_c - m) * w_c
        p_n = jnp.exp2(s_n - m) * w_n
        den = jnp.sum(p_c, axis=-1, keepdims=True) + jnp.sum(p_n, axis=-1, keepdims=True)
        num = (_dot_nt(p_c.astype(BF16), cvt_ref[0, hs, :].astype(BF16))
               + _dot_nt(p_n.astype(BF16), vnt_ref[0, hs, :].astype(BF16)))
        o = num / den
        o = jnp.concatenate([o[g * tp:(g + 1) * tp] for g in range(ATT_GQ)], axis=1)
        csl = slice(c0, c0 + ATT_GQ * HEAD_DIM)
        o_ref[:, csl] = (o * g_ref[:, csl]).astype(BF16)
    kot_ref[0, :, 0:win - ts] = ckt_ref[0, :, ts:win]
    kot_ref[0, :, win - ts:win] = knt_ref[0, :, 0:ts]
    vot_ref[0, :, 0:win - ts] = cvt_ref[0, :, ts:win]
    vot_ref[0, :, win - ts:win] = vnt_ref[0, :, 0:ts]


def _attn_sample(q, knt, vnt, gate, ckt, cvt, ts):
    batch, _, win = ckt.shape
    tp = SAMPLE_PAD
    blk = lambda b: (b, 0)
    per_b = lambda b: (b, 0, 0)
    return pl.pallas_call(
        functools.partial(_attn_sample_kernel, ts=ts),
        grid=(batch,),
        in_specs=[
            pl.BlockSpec((tp, ATT_INNER), blk),
            pl.BlockSpec((1, KV_DIM, tp), per_b),
            pl.BlockSpec((1, KV_DIM, tp), per_b),
            pl.BlockSpec((tp, ATT_INNER), blk),
            pl.BlockSpec((1, KV_DIM, win), per_b),
            pl.BlockSpec((1, KV_DIM, win), per_b),
        ],
        out_specs=[
            pl.BlockSpec((tp, ATT_INNER), blk),
            pl.BlockSpec((1, KV_DIM, win), per_b),
            pl.BlockSpec((1, KV_DIM, win), per_b),
        ],
        out_shape=[
            jax.ShapeDtypeStruct((batch * tp, ATT_INNER), BF16),
            jax.ShapeDtypeStruct((batch, KV_DIM, win), F32),
            jax.ShapeDtypeStruct((batch, KV_DIM, win), F32),
        ],
        compiler_params=pltpu.CompilerParams(
            dimension_semantics=("arbitrary",), vmem_limit_bytes=VMEM_LIMIT),
        name="attn_sample",
    )(q, knt, vnt, gate, ckt, cvt)


def _group_norm(yg, nw):
    ms = jnp.mean(yg * yg, axis=-1, keepdims=True)
    return (yg * lax.rsqrt(ms + EPS) * nw).astype(BF16)


def _outproj_kernel(x_ref, ys_ref, yat_ref, nw_ref, w_ref, o_ref):
    acc = x_ref[...]
    for g in range(SSD_GROUPS):
        gs = slice(g * HALF_INNER, (g + 1) * HALF_INNER)
        acc = acc + _dot(_group_norm(ys_ref[:, gs], nw_ref[:, gs]), w_ref[gs, :])
    ya = yat_ref[0].astype(F32).T.astype(BF16)
    o_ref[...] = acc + _dot(ya, w_ref[SSD_INNER:, :])


def _out_proj(x2d, y_ssd, y_att_t, nw, w_o, tm):
    rows = x2d.shape[0]
    per_seq = y_att_t.shape[2] // tm
    row = lambda i: (i, 0)
    const = lambda i: (0, 0)
    return pl.pallas_call(
        _outproj_kernel,
        grid=(rows // tm,),
        in_specs=[
            pl.BlockSpec((tm, D_MODEL), row),
            pl.BlockSpec((tm, SSD_INNER), row),
            pl.BlockSpec((1, ATT_INNER, tm), lambda i: (i // per_seq, 0, i % per_seq)),
            pl.BlockSpec((1, SSD_INNER), const),
            pl.BlockSpec((SSD_INNER + ATT_INNER, D_MODEL), const, pipeline_mode=pl.Buffered(1)),
        ],
        out_specs=pl.BlockSpec((tm, D_MODEL), row),
        out_shape=jax.ShapeDtypeStruct((rows, D_MODEL), F32),
        compiler_params=pltpu.CompilerParams(
            dimension_semantics=("arbitrary",), vmem_limit_bytes=VMEM_LIMIT),
        name="out_proj",
    )(x2d, y_ssd, y_att_t, nw, w_o)


def _wconv_out_kernel(x_ref, ys_ref, ya_ref, nw_ref, w_ref, wb_ref, o_ref):
    j = pl.program_id(0)
    wb = w_ref[...].astype(BF16)
    wb_ref[...] = wb
    y = jnp.where(j < SSD_GROUPS, _group_norm(ys_ref[...], nw_ref[...]), ya_ref[...])
    part = _dot(y, wb)
    bs, ts, _ = x_ref.shape

    @pl.when(j == 0)
    def _():
        for b in range(bs):
            o_ref[b] = x_ref[b] + part[b * SAMPLE_PAD:b * SAMPLE_PAD + ts]

    @pl.when(j > 0)
    def _():
        for b in range(bs):
            o_ref[b] += part[b * SAMPLE_PAD:b * SAMPLE_PAD + ts]


def _wconv_out(x_s, y_ssd, y_att, nw, w_f32):
    rows = x_s.shape[0] * SAMPLE_PAD
    n_ssd = SSD_GROUPS
    n_chunks = (SSD_INNER + ATT_INNER) // WO_CHUNK
    const = lambda j: (0, 0)
    ssd_chunk = lambda j: (0, jnp.minimum(j, n_ssd - 1))
    return pl.pallas_call(
        _wconv_out_kernel,
        grid=(n_chunks,),
        in_specs=[
            pl.BlockSpec(x_s.shape, lambda j: (0, 0, 0)),
            pl.BlockSpec((rows, WO_CHUNK), ssd_chunk),
            pl.BlockSpec((rows, WO_CHUNK), lambda j: (0, jnp.maximum(j - n_ssd, 0))),
            pl.BlockSpec((1, WO_CHUNK), ssd_chunk),
            pl.BlockSpec((WO_CHUNK, D_MODEL), lambda j: (j, 0)),
        ],
        out_specs=[
            pl.BlockSpec((WO_CHUNK, D_MODEL), lambda j: (j, 0)),
            pl.BlockSpec(x_s.shape, lambda j: (0, 0, 0)),
        ],
        out_shape=[
            jax.ShapeDtypeStruct((SSD_INNER + ATT_INNER, D_MODEL), BF16),
            jax.ShapeDtypeStruct(x_s.shape, F32),
        ],
        compiler_params=pltpu.CompilerParams(
            dimension_semantics=("arbitrary",), vmem_limit_bytes=VMEM_LIMIT),
        name="wconv_out",
    )(x_s, y_ssd, y_att, nw, w_f32)


def _rope_tables(pos):
    n = pos.shape[0]
    half = ROT_DIM // 2
    inv = ROPE_THETA ** (-np.arange(0, ROT_DIM, 2, dtype=np.float64) / ROT_DIM)
    ang = pos.astype(np.float64)[:, None] * inv[None, :]
    cos, sin = np.cos(ang), np.sin(ang)
    rest = HEAD_DIM - ROT_DIM
    zh = np.zeros((n, half))
    cos_h = np.concatenate([cos, cos, np.ones((n, rest))], axis=1)
    sa_h = np.concatenate([zh, sin, np.zeros((n, rest))], axis=1)
    sb_h = np.concatenate([-sin, zh, np.zeros((n, rest))], axis=1)
    rep = LANES // HEAD_DIM
    return tuple(jnp.asarray(np.tile(t, (1, rep)), dtype=F32) for t in (cos_h, sa_h, sb_h))


def _lane_pad(v, n=LANES):
    return jnp.pad(v, (0, n - v.shape[0])).reshape(1, n)


def _to_cache(xt, batch, seq):
    return xt.reshape(1, batch, ATT_KV_HEADS, HEAD_DIM, seq).transpose(0, 1, 4, 2, 3)


def kernel(x_prompt, x_sample, cache_k, cache_v, state_conv, state_ssm, norm_w, w_in, conv_w,
           conv_b, dt_bias, a_log, d_skip, ssd_norm_w, q_norm_w, k_norm_w, w_out):
    bp, tp_, _ = x_prompt.shape
    bs, ts, _ = x_sample.shape
    depth = w_in.shape[0]
    assert depth == 1 and tp_ % SSD_CHUNK == 0 and ts <= SAMPLE_PAD and ts >= CONV_W - 1
    l = 0
    win = cache_k.shape[2]

    nw = norm_w[l].reshape(1, D_MODEL)
    rep = LANES // HEAD_DIM
    qnw = jnp.tile(q_norm_w[l], rep).reshape(1, LANES)
    knw = jnp.tile(k_norm_w[l], rep).reshape(1, LANES)
    cw = jnp.pad(conv_w[l], ((0, SUBLANES - CONV_W), (0, 0)))
    cb = conv_b[l].reshape(1, CONV_DIM)
    dtb = _lane_pad(dt_bias[l])
    alog = _lane_pad(a_log[l])
    dexp = jnp.repeat(d_skip[l], SSD_HEADDIM).reshape(1, SSD_INNER)
    snw = ssd_norm_w[l].reshape(1, SSD_INNER)

    pad = SAMPLE_PAD
    tabs_s = _rope_tables(PAST_LEN + np.arange(bs * pad) % pad)
    w_t, proj_s = _wconv_in(x_sample, nw, jnp.swapaxes(w_in[l], 0, 1))
    zs, xbc, dtr, q, kt, vt, gs, _ = _sections(proj_s, qnw, knw, tabs_s)
    cprev = jnp.pad(state_conv[l], ((0, 0), (SUBLANES - (CONV_W - 1), 0), (0, 0)))
    y_ssd, h_s = _ssd(xbc, dtr, zs, cprev, state_ssm[l].reshape(bs, SSD_INNER, SSD_STATE),
                      cw, cb, dtb, alog, dexp, bs, pad, ts)
    knt = kt.reshape(KV_DIM, bs, pad).transpose(1, 0, 2)
    vnt = vt.reshape(KV_DIM, bs, pad).transpose(1, 0, 2)
    ckt = cache_k[l].transpose(0, 2, 3, 1).reshape(bs, KV_DIM, win)
    cvt = cache_v[l].transpose(0, 2, 3, 1).reshape(bs, KV_DIM, win)
    y_att, kot, vot = _attn_sample(q, knt, vnt, gs, ckt, cvt, ts)
    w_o, y_s = _wconv_out(x_sample, y_ssd, y_att, snw, w_out[l])
    k_s = _to_cache(kot, bs, win)
    v_s = _to_cache(vot, bs, win)
    c_s = xbc.reshape(bs, pad, CONV_DIM)[:, ts - (CONV_W - 1):ts][None]
    h_s = h_s.reshape(1, bs, SSD_HEADS, SSD_HEADDIM, SSD_STATE)

    tm = TM_IN
    xp2 = x_prompt.reshape(bp * tp_, D_MODEL)
    tabs = _rope_tables(np.arange(tp_))
    zs, xbc, dtr, q, kt, vt, gs, ctail = _in_proj(xp2, nw, w_t, qnw, knw, tabs, tm, tp_)
    y_ssd, h_p = _ssd(xbc, dtr, zs, jnp.zeros((bp, SUBLANES, CONV_DIM), F32),
                      jnp.zeros((bp, SSD_INNER, SSD_STATE), F32),
                      cw, cb, dtb, alog, dexp, bp, SSD_CHUNK, SSD_CHUNK)
    y_att = _attn_prompt(q, kt, vt, gs, bp, tp_)
    y_p = _out_proj(xp2, y_ssd, y_att, snw, w_o, TM_OUT).reshape(bp, tp_, D_MODEL)
    keep = min(W_MAX, tp_)
    k_p = _to_cache(kt, bp, tp_)[:, :, tp_ - keep:]
    v_p = _to_cache(vt, bp, tp_)[:, :, tp_ - keep:]
    c_p = ctail[:, SUBLANES - (CONV_W - 1):][None]
    h_p = h_p.reshape(1, bp, SSD_HEADS, SSD_HEADDIM, SSD_STATE)

    return (y_p, y_s, k_p, v_p, c_p, h_p, k_s, v_s, c_s, h_s)
```

```python
import functools

import jax
import jax.numpy as jnp
import numpy as np
from jax import lax
from jax.experimental import pallas as pl
from jax.experimental.pallas import tpu as pltpu

F32 = jnp.float32
BF16 = jnp.bfloat16

D_MODEL = 2048
SSD_HEADS = 16
SSD_HEADDIM = 64
SSD_INNER = SSD_HEADS * SSD_HEADDIM
SSD_GROUPS = 2
SSD_STATE = 128
CONV_W = 4
CONV_DIM = SSD_INNER + 2 * SSD_GROUPS * SSD_STATE
SSD_CHUNK = 128
ATT_HEADS = 16
ATT_KV_HEADS = 4
HEAD_DIM = 64
ATT_GQ = ATT_HEADS // ATT_KV_HEADS
ATT_INNER = ATT_HEADS * HEAD_DIM
KV_DIM = ATT_KV_HEADS * HEAD_DIM
ROT_DIM = HEAD_DIM // 4
ROPE_THETA = 500000.0
DILATED_BRANCHES = ((128, 1), (512, 4), (2048, 16))
W_MAX = 2048
PAST_LEN = 16384
EPS = 1e-6

LANES = 128
SUBLANES = 8
Q_BLOCK = 256
K_SUPER = 256
Q_PER_SUPER = K_SUPER // Q_BLOCK
ONES_ROWS = 16
ACC_ROWS = HEAD_DIM + ONES_ROWS
CONV_COLS = 256
TM_IN = 512
IN_PASS_ROWS = 256
TM_OUT = 512
SSD_SEQS = 4
SAMPLE_PAD = 16
NEG = -1e30
VMEM_LIMIT = 58 * 1024 * 1024

Z0 = 0
X0 = Z0 + SSD_INNER
DT0 = X0 + CONV_DIM
Q0 = DT0 + SSD_HEADS
K0 = Q0 + ATT_INNER
V0 = K0 + KV_DIM
G0 = V0 + KV_DIM
W_ROWS = G0 + ATT_INNER
W_CHUNK = 1024
HALF_INNER = SSD_INNER // SSD_GROUPS
WO_CHUNK = HALF_INNER
LOG2E = 1.4426950408889634
Q_SCALE = HEAD_DIM ** -0.5 * LOG2E


def _dot(a, b):
    return jnp.dot(a, b, preferred_element_type=F32)


def _dot_nt(a, b):
    return lax.dot_general(a, b, (((1,), (1,)), ((), ())), preferred_element_type=F32)


def _split3(x):
    hi = x.astype(BF16)
    r1 = x - hi.astype(F32)
    mid = r1.astype(BF16)
    lo = (r1 - mid.astype(F32)).astype(BF16)
    return hi, mid, lo


def _dot_exact_rhs(x, m):
    hi, mid, lo = _split3(x)
    return _dot(hi, m) + _dot(mid, m) + _dot(lo, m)


def _dot_wide_rhs(x, m):
    hi = x.astype(BF16)
    lo = (x - hi.astype(F32)).astype(BF16)
    return _dot(hi, m) + _dot(lo, m)


def _dot_exact_lhs(m, x):
    hi, mid, lo = _split3(x)
    return _dot(m, hi) + _dot(m, mid) + _dot(m, lo)


def _silu(x):
    hx = 0.5 * x
    return hx * jnp.tanh(hx) + hx


def _multiplicity(d):
    w = jnp.zeros(d.shape, F32)
    for window, dil in DILATED_BRANCHES:
        hit = (d >= 0) & (d <= window) & (lax.rem(d, dil) == 0)
        w = w + jnp.where(hit, 1.0, 0.0)
    return w


def _norm_rope(y, nw, cos, sa, sb, scale):
    lane = lax.broadcasted_iota(jnp.int32, (1, LANES), 1)
    first = lane < HEAD_DIM
    y2 = y * y
    s_lo = jnp.sum(jnp.where(first, y2, 0.0), axis=-1, keepdims=True)
    s_hi = jnp.sum(jnp.where(first, 0.0, y2), axis=-1, keepdims=True)
    ms = jnp.where(first, s_lo, s_hi) * (1.0 / HEAD_DIM)
    yn = y * lax.rsqrt(ms + EPS) * nw
    half = ROT_DIM // 2
    rot = yn * cos + pltpu.roll(yn, half, 1) * sa + pltpu.roll(yn, LANES - half, 1) * sb
    return rot * scale


def _causal_conv_silu(load, cw_ref, cb_ref, store):
    for c0 in range(0, CONV_DIM, CONV_COLS):
        cs = slice(c0, c0 + CONV_COLS)
        xe = load(cs)
        x2 = pltpu.roll(xe, 2, 0)
        even = cw_ref[3:4, cs] * xe + cw_ref[1:2, cs] * x2
        odd = cw_ref[2:3, cs] * xe + cw_ref[0:1, cs] * x2
        store(cs, _silu((even + pltpu.roll(odd, 1, 0))[SUBLANES:, :] + cb_ref[:, cs]))


def _pre_norm(x, nw_ref):
    ms = jnp.mean(x * x, axis=-1, keepdims=True)
    return (x * lax.rsqrt(ms + EPS) * nw_ref[...]).astype(BF16)


def _inproj_kernel(x_ref, nw_ref, w_ref, *refs):
    def row_pass(h, carry):
        rs = pl.ds(pl.multiple_of(h * IN_PASS_ROWS, IN_PASS_ROWS), IN_PASS_ROWS)
        hn = _pre_norm(x_ref[rs, :], nw_ref)

        def proj(a, b):
            return _dot_nt(hn, w_ref[a:b, :])

        _emit_sections(proj, rs, *refs)
        return carry

    lax.fori_loop(0, x_ref.shape[0] // IN_PASS_ROWS, row_pass, 0)


def _emit_sections(proj, rs, qnw_ref, knw_ref, cos_ref, sa_ref, sb_ref,
                   zs_ref, xbc_ref, dt_ref, q_ref, kt_ref, vt_ref, gs_ref, ctail_ref):
    cos, sa, sb = cos_ref[rs, :], sa_ref[rs, :], sb_ref[rs, :]
    q = proj(Q0, K0)
    for c in range(ATT_INNER // LANES):
        sl = slice(c * LANES, (c + 1) * LANES)
        qn = _norm_rope(q[:, sl], qnw_ref[...], cos, sa, sb, Q_SCALE)
        if len(q_ref.shape) == 3:
            q_ref[0, sl, rs] = qn.T.astype(BF16)
        else:
            q_ref[rs, sl] = qn.astype(BF16)
    k = proj(K0, V0)
    kn = [_norm_rope(k[:, c * LANES:(c + 1) * LANES], knw_ref[...], cos, sa, sb, 1.0)
          for c in range(KV_DIM // LANES)]
    kt_ref[0, :, rs] = jnp.concatenate(kn, axis=1).T

    xbc = proj(X0, DT0)
    xbc_ref[rs, :] = xbc
    ctail_ref[0] = xbc[-SUBLANES:, :]
    vt_ref[0, :, rs] = proj(V0, G0).T
    zs_ref[rs, :] = _silu(proj(Z0, X0))
    gs = _silu(proj(G0, W_ROWS))
    if len(gs_ref.shape) == 3:
        gs_ref[0, :, rs] = gs.T
    else:
        gs_ref[rs, :] = gs
    dt_ref[rs, :] = proj(DT0, DT0 + LANES)


def _in_proj(x2d, nw, w_t, qnw, knw, tables, tm, seq):
    rows = x2d.shape[0]
    const = lambda i: (0, 0)
    row = lambda i: (i, 0)
    sec_in, out_specs, out_shape = _section_specs(rows, tm, seq, tables[0].shape[0] // tm, True)
    return pl.pallas_call(
        _inproj_kernel,
        grid=(rows // tm,),
        in_specs=[
            pl.BlockSpec((tm, D_MODEL), row),
            pl.BlockSpec((1, D_MODEL), const),
            pl.BlockSpec((W_ROWS, D_MODEL), const, pipeline_mode=pl.Buffered(1)),
        ] + sec_in,
        out_specs=out_specs,
        out_shape=out_shape,
        compiler_params=pltpu.CompilerParams(
            dimension_semantics=("arbitrary",), vmem_limit_bytes=VMEM_LIMIT),
        name="in_proj",
    )(x2d, nw, w_t, qnw, knw, *tables)


def _section_specs(rows, tm, seq, period, q_transposed):
    per_seq = seq // tm
    const = lambda i: (0, 0)
    row = lambda i: (i, 0)
    tab = lambda i: (i % period, 0)
    tr = lambda i: (i // per_seq, 0, i % per_seq)
    slab = lambda i: (i // per_seq, 0, 0)
    in_specs = [pl.BlockSpec((1, LANES), const)] * 2 + [pl.BlockSpec((tm, LANES), tab)] * 3
    row_outs = {0: (SSD_INNER, F32), 1: (CONV_DIM, F32), 2: (LANES, F32), 3: (ATT_INNER, BF16),
                6: (ATT_INNER, F32)}
    out_specs, out_shape = [], []
    for idx in range(7):
        if idx in (3, 6) and q_transposed:
            dt = row_outs[idx][1]
            out_specs.append(pl.BlockSpec((1, ATT_INNER, tm), tr))
            out_shape.append(jax.ShapeDtypeStruct((rows // seq, ATT_INNER, seq), dt))
        elif idx in row_outs:
            n, dt = row_outs[idx]
            out_specs.append(pl.BlockSpec((tm, n), row))
            out_shape.append(jax.ShapeDtypeStruct((rows, n), dt))
        else:
            out_specs.append(pl.BlockSpec((1, KV_DIM, tm), tr))
            out_shape.append(jax.ShapeDtypeStruct((rows // seq, KV_DIM, seq), F32))
    out_specs.append(pl.BlockSpec((1, SUBLANES, CONV_DIM), slab))
    out_shape.append(jax.ShapeDtypeStruct((rows // seq, SUBLANES, CONV_DIM), F32))
    return in_specs, out_specs, out_shape


def _wconv_in_kernel(x_ref, nw_ref, w_ref, wb_ref, p_ref, xpad_scr, hn_scr):
    @pl.when(pl.program_id(0) == 0)
    def _():
        bs, ts, _ = x_ref.shape
        xpad_scr[...] = jnp.zeros(xpad_scr.shape, F32)
        for b in range(bs):
            xpad_scr[b * SAMPLE_PAD:b * SAMPLE_PAD + ts, :] = x_ref[b]
        hn_scr[...] = _pre_norm(xpad_scr[...], nw_ref)

    row = pl.program_id(0) * W_CHUNK + lax.broadcasted_iota(jnp.int32, (W_CHUNK, 1), 0)
    wb = jnp.where(row < W_ROWS, w_ref[...], 0.0).astype(BF16)
    wb_ref[...] = wb
    p_ref[0] = _dot_nt(hn_scr[...], wb)


def _wconv_in(x_s, nw, w_f32):
    rows = x_s.shape[0] * SAMPLE_PAD
    n_chunks = pl.cdiv(W_ROWS, W_CHUNK)
    const = lambda j: (0, 0)
    return pl.pallas_call(
        _wconv_in_kernel,
        grid=(n_chunks,),
        in_specs=[
            pl.BlockSpec(x_s.shape, lambda j: (0, 0, 0)),
            pl.BlockSpec((1, D_MODEL), const),
            pl.BlockSpec((W_CHUNK, D_MODEL), lambda j: (j, 0)),
        ],
        out_specs=[
            pl.BlockSpec((W_CHUNK, D_MODEL), lambda j: (j, 0)),
            pl.BlockSpec((1, rows, W_CHUNK), lambda j: (j, 0, 0)),
        ],
        out_shape=[
            jax.ShapeDtypeStruct((W_ROWS, D_MODEL), BF16),
            jax.ShapeDtypeStruct((n_chunks, rows, W_CHUNK), F32),
        ],
        scratch_shapes=[pltpu.VMEM((rows, D_MODEL), F32), pltpu.VMEM((rows, D_MODEL), BF16)],
        compiler_params=pltpu.CompilerParams(
            dimension_semantics=("arbitrary",), vmem_limit_bytes=VMEM_LIMIT),
        name="wconv_in",
    )(x_s, nw, w_f32)


def _sections_kernel(p_ref, *refs):
    *refs, tail_scr = refs
    pfull = jnp.concatenate([p_ref[j] for j in range(p_ref.shape[0])], axis=1)
    tail_scr[...] = pfull[:, Q0:W_ROWS]

    def proj(a, b):
        if a >= Q0:
            return tail_scr[:, a - Q0:b - Q0]
        return pfull[:, a:b]

    _emit_sections(proj, slice(0, p_ref.shape[1]), *refs)


def _sections(p, qnw, knw, tables):
    rows = p.shape[1]
    sec_in, out_specs, out_shape = _section_specs(rows, rows, rows, 1, False)
    return pl.pallas_call(
        _sections_kernel,
        grid=(1,),
        in_specs=[pl.BlockSpec(p.shape, lambda i: (0, 0, 0))] + sec_in,
        out_specs=out_specs,
        out_shape=out_shape,
        scratch_shapes=[pltpu.VMEM((rows, W_ROWS - Q0), F32)],
        compiler_params=pltpu.CompilerParams(
            dimension_semantics=("arbitrary",), vmem_limit_bytes=VMEM_LIMIT),
        name="sections",
    )(p, qnw, knw, *tables)


def _ssd_chain(s, xbc_ref, dtr_ref, z_ref, cprev_ref, h0_ref, cw_ref, cb_ref, dtb_ref, alog_ref,
               dexp_ref, y_ref, hout_ref, xext, ht, xc_scr, lin, valid):
    L = SSD_CHUNK
    c = pl.program_id(1)
    last = pl.num_programs(1) - 1

    @pl.when(c == 0)
    def _():
        xext[s, 0:SUBLANES, :] = cprev_ref[s]
        for g in range(SSD_GROUPS):
            ht[s, g] = h0_ref[s, g * HALF_INNER:(g + 1) * HALF_INNER, :].T

    def pad_rows(v):
        if lin == L:
            return v
        return jnp.concatenate([v, jnp.zeros((L - lin, v.shape[1]), v.dtype)], axis=0)

    lane = lax.broadcasted_iota(jnp.int32, (1, LANES), 1)
    rowid = lax.broadcasted_iota(jnp.int32, (L, 1), 0)
    dt = jax.nn.softplus(pad_rows(dtr_ref[s]) + dtb_ref[...])
    dt = jnp.where((lane < SSD_HEADS) & (rowid < valid), dt, 0.0)
    dta = dt * (-jnp.exp(alog_ref[...]) * LOG2E)

    r2 = lax.broadcasted_iota(jnp.int32, (L, L), 0)
    c2 = lax.broadcasted_iota(jnp.int32, (L, L), 1)
    tri = r2 >= c2
    tri_b = jnp.where(tri, 1.0, 0.0).astype(BF16)
    cum = _dot_exact_lhs(tri_b, dta)
    cum_last = cum[L - 1:L, :]
    yield

    er = lax.broadcasted_iota(jnp.int32, (LANES, SSD_INNER), 0)
    ec = lax.broadcasted_iota(jnp.int32, (LANES, SSD_INNER), 1)
    expand = jnp.where(ec // SSD_HEADDIM == er, 1.0, 0.0).astype(BF16)
    ecum = _dot_wide_rhs(jnp.exp2(cum), expand)
    wexp = _dot_wide_rhs(jnp.exp2(cum_last - cum) * dt, expand)
    cd = _dot_exact_rhs(jnp.broadcast_to(jnp.exp2(cum_last), (SUBLANES, LANES)), expand)[0:1, :]
    cum_t = cum.T
    dt_t = dt.T
    yield

    if lin < L:
        xext[s, SUBLANES + lin:SUBLANES + L, :] = jnp.zeros((L - lin, CONV_DIM), F32)
    xext[s, SUBLANES:SUBLANES + lin, :] = xbc_ref[s]

    def load(cs):
        return xext[s, 0:SUBLANES + L, cs]

    def store(cs, v):
        xc_scr[s, :, cs] = v

    _causal_conv_silu(load, cw_ref, cb_ref, store)
    xext[s, 0:SUBLANES, :] = xext[s, L:L + SUBLANES, :]
    xc = xc_scr.at[s]
    xs = xc[:, :SSD_INNER]
    bm = xc[:, SSD_INNER:SSD_INNER + SSD_GROUPS * SSD_STATE]
    cm = xc[:, SSD_INNER + SSD_GROUPS * SSD_STATE:]
    bm_b = bm.astype(BF16)
    cm_b = cm.astype(BF16)
    xs_b = xs.astype(BF16)

    y_off = jnp.concatenate(
        [_dot(cm_b[:, g * SSD_STATE:(g + 1) * SSD_STATE], ht[s, g].astype(BF16))
         for g in range(SSD_GROUPS)], axis=1) * ecum
    cbs = [_dot_nt(cm_b[:, g * SSD_STATE:(g + 1) * SSD_STATE], bm_b[:, g * SSD_STATE:(g + 1) * SSD_STATE])
           for g in range(SSD_GROUPS)]
    yield

    lane_half = lane // HEAD_DIM
    heads_per_group = SSD_HEADS // SSD_GROUPS
    y_parts = []
    for g in range(SSD_GROUPS):
        for pr in range(heads_per_group // 2):
            col0 = g * HALF_INNER + pr * LANES
            xp = xs_b[:, col0:col0 + LANES]
            yp = jnp.zeros((L, LANES), F32)
            for e in range(2):
                hh = g * heads_per_group + pr * 2 + e
                decay = jnp.exp2(cum[:, hh:hh + 1] - cum_t[hh:hh + 1, :])
                sc = jnp.where(tri, cbs[g] * decay, 0.0) * dt_t[hh:hh + 1, :]
                xm = jnp.where(lane_half == e, xp, jnp.zeros_like(xp))
                yp = yp + _dot(sc.astype(BF16), xm)
            y_parts.append(yp)
            if pr % 2 == 1:
                yield
    y_diag = jnp.concatenate(y_parts, axis=1)

    xw = (xs * wexp).astype(BF16)
    for g in range(SSD_GROUPS):
        sl = slice(g * HALF_INNER, (g + 1) * HALF_INNER)
        bm_t = bm[:, g * SSD_STATE:(g + 1) * SSD_STATE].T.astype(BF16)
        ht[s, g] = ht[s, g] * cd[:, sl] + _dot(bm_t, xw[:, sl])
    yield

    y = (y_diag + y_off + dexp_ref[...] * xs) * pad_rows(z_ref[s])
    y_ref[s] = y[:lin]

    @pl.when(c == last)
    def _():
        for g in range(SSD_GROUPS):
            hout_ref[s, g * HALF_INNER:(g + 1) * HALF_INNER, :] = ht[s, g].T


def _ssd_kernel(*refs, lin, valid):
    chains = [_ssd_chain(s, *refs, lin, valid) for s in range(SSD_SEQS)]
    while chains:
        alive = []
        for ch in chains:
            try:
                next(ch)
                alive.append(ch)
            except StopIteration:
                pass
        chains = alive


def _ssd(xbc, dtr, z, cprev, h0, cw, cb, dtb, alog, dexp, batch, lin, valid):
    rows = xbc.shape[0]
    seq = rows // batch
    nc = seq // lin
    g = SSD_SEQS
    assert batch % g == 0
    blk = lambda b, c: (b, c, 0)
    per_b = lambda b, c: (b, 0, 0)
    const = lambda b, c: (0, 0)
    y, hout = pl.pallas_call(
        functools.partial(_ssd_kernel, lin=lin, valid=valid),
        grid=(batch // g, nc),
        in_specs=[
            pl.BlockSpec((g, lin, CONV_DIM), blk),
            pl.BlockSpec((g, lin, LANES), blk),
            pl.BlockSpec((g, lin, SSD_INNER), blk),
            pl.BlockSpec((g, SUBLANES, CONV_DIM), per_b),
            pl.BlockSpec((g, SSD_INNER, SSD_STATE), per_b),
            pl.BlockSpec((SUBLANES, CONV_DIM), const),
            pl.BlockSpec((1, CONV_DIM), const),
            pl.BlockSpec((1, LANES), const),
            pl.BlockSpec((1, LANES), const),
            pl.BlockSpec((1, SSD_INNER), const),
        ],
        out_specs=[
            pl.BlockSpec((g, lin, SSD_INNER), blk),
            pl.BlockSpec((g, SSD_INNER, SSD_STATE), per_b),
        ],
        out_shape=[
            jax.ShapeDtypeStruct((batch, seq, SSD_INNER), F32),
            jax.ShapeDtypeStruct((batch, SSD_INNER, SSD_STATE), F32),
        ],
        scratch_shapes=[
            pltpu.VMEM((g, SUBLANES + SSD_CHUNK + SUBLANES, CONV_DIM), F32),
            pltpu.VMEM((g, SSD_GROUPS, SSD_STATE, HALF_INNER), F32),
            pltpu.VMEM((g, SSD_CHUNK, CONV_DIM), F32),
        ],
        compiler_params=pltpu.CompilerParams(
            dimension_semantics=("arbitrary", "arbitrary"), vmem_limit_bytes=VMEM_LIMIT),
        name="ssd",
    )(xbc.reshape(batch, seq, CONV_DIM), dtr.reshape(batch, seq, LANES),
      z.reshape(batch, seq, SSD_INNER), cprev, h0, cw, cb, dtb, alog, dexp)
    return y.reshape(rows, SSD_INNER), hout


def _attn_prompt_kernel(q_ref, kt_ref, vt_ref, g_ref, o_ref,
                        bias_scr, kh_scr, va_scr, qt_scr, m_scr, acc_scr, s_scr, s2_scr, *, n_tab, n_sb_max):
    b = pl.program_id(0)
    i = pl.program_id(1)
    seq = kt_ref.shape[2]
    pair_w = 2 * Q_BLOCK
    pairs_per_kv = ATT_GQ // 2

    @pl.when((b == 0) & (i == 0))
    def _():
        r = lax.broadcasted_iota(jnp.int32, (K_SUPER, Q_BLOCK), 0)
        c = lax.broadcasted_iota(jnp.int32, (K_SUPER, Q_BLOCK), 1)
        for tb in range(n_tab):
            w = _multiplicity(tb * Q_BLOCK + c - r)
            bias_scr[tb] = jnp.where(w > 0.0, jnp.log2(jnp.maximum(w, 1.0)), NEG)
        va_scr[:, HEAD_DIM:, :] = jnp.ones((ATT_KV_HEADS, ONES_ROWS, seq), BF16)

    @pl.when(i == 0)
    def _():
        for kvh in range(ATT_KV_HEADS):
            hs = slice(kvh * HEAD_DIM, (kvh + 1) * HEAD_DIM)
            va_scr[kvh, 0:HEAD_DIM, :] = vt_ref[0, hs, :].astype(BF16)
        for cidx in range(seq // K_SUPER):
            rows = slice(cidx * K_SUPER, (cidx + 1) * K_SUPER)
            kc = kt_ref[0, :, rows].T
            for kvh in range(ATT_KV_HEADS):
                hs = slice(kvh * HEAD_DIM, (kvh + 1) * HEAD_DIM)
                kh_scr[kvh, rows, :] = kc[:, hs].astype(BF16)

    for kvh in range(ATT_KV_HEADS):
        qt_scr[kvh] = jnp.concatenate(
            [q_ref[0, (kvh * ATT_GQ + g) * HEAD_DIM:(kvh * ATT_GQ + g + 1) * HEAD_DIM, :]
             for g in range(ATT_GQ)], axis=1)
    m_scr[...] = jnp.full(m_scr.shape, NEG, F32)
    acc_scr[...] = jnp.zeros(acc_scr.shape, F32)
    par = i % Q_PER_SUPER
    j_last = i // Q_PER_SUPER
    n_sb = jnp.minimum(j_last + 1, n_sb_max)

    def key_start(dl):
        return pl.multiple_of((j_last - dl) * K_SUPER, K_SUPER)

    def scores(dl, dst):
        for kvh in range(ATT_KV_HEADS):
            dst[kvh] = _dot(kh_scr[kvh, pl.ds(key_start(dl), K_SUPER), :], qt_scr[kvh])

    def softmax_pv(dl, src):
        start = key_start(dl)
        bias = bias_scr[par + Q_PER_SUPER * dl]
        bias2 = jnp.concatenate([bias, bias], axis=1)
        for kvh in range(ATT_KV_HEADS):
            vtb = va_scr[kvh, :, pl.ds(start, K_SUPER)]
            for pr in range(pairs_per_kv):
                u = kvh * pairs_per_kv + pr
                s = src[kvh, :, pr * pair_w:(pr + 1) * pair_w] + bias2
                m_old = m_scr[u, 0:1, :]
                m_new = jnp.maximum(m_old, jnp.max(s, axis=0, keepdims=True))
                p = jnp.exp2(s - m_new).astype(BF16)
                acc_scr[u] = jnp.exp2(m_old - m_new) * acc_scr[u] + _dot(vtb, p)
                m_scr[u, 0:1, :] = m_new

    def diagonal_block():
        half = K_SUPER // 2
        start = key_start(0)
        bias = bias_scr[0]
        for kvh in range(ATT_KV_HEADS):
            s_scr[kvh, 0:half, :] = _dot(kh_scr[kvh, pl.ds(start, half), :], qt_scr[kvh])
        for kvh in range(ATT_KV_HEADS):
            q_late = jnp.concatenate(
                [qt_scr[kvh, :, g * Q_BLOCK + half:(g + 1) * Q_BLOCK] for g in range(ATT_GQ)], axis=1)
            s2_scr[kvh, 0:half, 0:ATT_GQ * half] = _dot(
                kh_scr[kvh, pl.ds(start + half, half), :], q_late)
        bias_a = jnp.concatenate([bias[0:half, :]] * 2, axis=1)
        for kvh in range(ATT_KV_HEADS):
            vtb = va_scr[kvh, :, pl.ds(start, half)]
            for pr in range(pairs_per_kv):
                u = kvh * pairs_per_kv + pr
                s = s_scr[kvh, 0:half, pr * pair_w:(pr + 1) * pair_w] + bias_a
                m_old = m_scr[u, 0:1, :]
                m_new = jnp.maximum(m_old, jnp.max(s, axis=0, keepdims=True))
                p = jnp.exp2(s - m_new).astype(BF16)
                acc_scr[u] = jnp.exp2(m_old - m_new) * acc_scr[u] + _dot(vtb, p)
                m_scr[u, 0:1, :] = m_new
        bias_b = jnp.concatenate([bias[half:, half:]] * 2, axis=1)
        late = [slice(e * Q_BLOCK + half, (e + 1) * Q_BLOCK) for e in range(2)]
        for kvh in range(ATT_KV_HEADS):
            vtb = va_scr[kvh, :, pl.ds(start + half, half)]
            for pr in range(pairs_per_kv):
                u = kvh * pairs_per_kv + pr
                s = s2_scr[kvh, 0:half, pr * Q_BLOCK:(pr + 1) * Q_BLOCK] + bias_b
                m_old = jnp.concatenate([m_scr[u, 0:1, sl] for sl in late], axis=1)
                m_new = jnp.maximum(m_old, jnp.max(s, axis=0, keepdims=True))
                p = jnp.exp2(s - m_new).astype(BF16)
                alpha = jnp.exp2(m_old - m_new)
                pv = _dot(vtb, p)
                for e, sl in enumerate(late):
                    es = slice(e * half, (e + 1) * half)
                    acc_scr[u, :, sl] = alpha[:, es] * acc_scr[u, :, sl] + pv[:, es]
                    m_scr[u, 0:1, sl] = m_new[:, es]

    diagonal_block()
    n_rest = n_sb - 1

    def body(t, carry):
        scores(1 + 2 * t, s_scr)
        scores(2 + 2 * t, s2_scr)
        softmax_pv(1 + 2 * t, s_scr)
        softmax_pv(2 + 2 * t, s2_scr)
        return carry

    lax.fori_loop(0, n_rest // 2, body, 0)

    @pl.when(n_rest % 2 == 1)
    def _():
        scores(n_sb - 1, s_scr)
        softmax_pv(n_sb - 1, s_scr)

    for u in range(ATT_HEADS // 2):
        acc = acc_scr[u]
        o = acc[0:HEAD_DIM] * (1.0 / acc[HEAD_DIM:HEAD_DIM + 1])
        for e in range(2):
            hs = slice((2 * u + e) * HEAD_DIM, (2 * u + e + 1) * HEAD_DIM)
            o_ref[0, hs, :] = (o[:, e * Q_BLOCK:(e + 1) * Q_BLOCK] * g_ref[0, hs, :]).astype(BF16)


def _attn_prompt(qt, kt, vt, gate_t, batch, seq):
    nqb = seq // Q_BLOCK
    assert seq % K_SUPER == 0 and Q_BLOCK == K_SUPER
    n_sb_max = min(seq // K_SUPER, W_MAX // K_SUPER + 1)
    n_tab = Q_PER_SUPER * n_sb_max
    qblk = lambda b, i: (b, 0, i)
    per_b = lambda b, i: (b, 0, 0)
    return pl.pallas_call(
        functools.partial(_attn_prompt_kernel, n_tab=n_tab, n_sb_max=n_sb_max),
        grid=(batch, nqb),
        in_specs=[
            pl.BlockSpec((1, ATT_INNER, Q_BLOCK), qblk),
            pl.BlockSpec((1, KV_DIM, seq), per_b),
            pl.BlockSpec((1, KV_DIM, seq), per_b),
            pl.BlockSpec((1, ATT_INNER, Q_BLOCK), qblk),
        ],
        out_specs=pl.BlockSpec((1, ATT_INNER, Q_BLOCK), qblk),
        out_shape=jax.ShapeDtypeStruct((batch, ATT_INNER, seq), BF16),
        scratch_shapes=[
            pltpu.VMEM((n_tab, K_SUPER, Q_BLOCK), F32),
            pltpu.VMEM((ATT_KV_HEADS, seq, HEAD_DIM), BF16),
            pltpu.VMEM((ATT_KV_HEADS, ACC_ROWS, seq), BF16),
            pltpu.VMEM((ATT_KV_HEADS, HEAD_DIM, ATT_GQ * Q_BLOCK), BF16),
            pltpu.VMEM((ATT_HEADS // 2, SUBLANES, 2 * Q_BLOCK), F32),
            pltpu.VMEM((ATT_HEADS // 2, ACC_ROWS, 2 * Q_BLOCK), F32),
            pltpu.VMEM((ATT_KV_HEADS, K_SUPER, ATT_GQ * Q_BLOCK), F32),
            pltpu.VMEM((ATT_KV_HEADS, K_SUPER, ATT_GQ * Q_BLOCK), F32),
        ],
        compiler_params=pltpu.CompilerParams(
            dimension_semantics=("arbitrary", "arbitrary"), vmem_limit_bytes=VMEM_LIMIT),
        name="attn_prompt",
    )(qt, kt, vt, gate_t)


def _attn_sample_kernel(q_ref, knt_ref, vnt_ref, g_ref, ckt_ref, cvt_ref, o_ref, kot_ref, vot_ref, *, ts):
    win = ckt_ref.shape[2]
    tp = SAMPLE_PAD
    rows = ATT_GQ * tp
    t_c = lax.broadcasted_iota(jnp.int32, (rows, win), 0) % tp
    j_c = lax.broadcasted_iota(jnp.int32, (rows, win), 1)
    w_c = _multiplicity(win + t_c - j_c)
    t_n = lax.broadcasted_iota(jnp.int32, (rows, tp), 0) % tp
    j_n = lax.broadcasted_iota(jnp.int32, (rows, tp), 1)
    w_n = jnp.where(j_n < ts, _multiplicity(t_n - j_n), 0.0)
    for kvh in range(ATT_KV_HEADS):
        c0 = kvh * ATT_GQ * HEAD_DIM
        qh = jnp.concatenate(
            [q_ref[:, c0 + g * HEAD_DIM:c0 + (g + 1) * HEAD_DIM] for g in range(ATT_GQ)], axis=0)
        hs = slice(kvh * HEAD_DIM, (kvh + 1) * HEAD_DIM)
        s_c = jnp.where(w_c > 0.0, _dot(qh, ckt_ref[0, hs, :].astype(BF16)), NEG)
        s_n = jnp.where(w_n > 0.0, _dot(qh, knt_ref[0, hs, :].astype(BF16)), NEG)
        m = jnp.maximum(jnp.max(s_c, axis=-1, keepdims=True), jnp.max(s_n, axis=-1, keepdims=True))
        p_c = jnp.exp2(s_c - m) * w_c
        p_n = jnp.exp2(s_n - m) * w_n
        den = jnp.sum(p_c, axis=-1, keepdims=True) + jnp.sum(p_n, axis=-1, keepdims=True)
        num = (_dot_nt(p_c.astype(BF16), cvt_ref[0, hs, :].astype(BF16))
               + _dot_nt(p_n.astype(BF16), vnt_ref[0, hs, :].astype(BF16)))
        o = num / den
        o = jnp.concatenate([o[g * tp:(g + 1) * tp] for g in range(ATT_GQ)], axis=1)
        csl = slice(c0, c0 + ATT_GQ * HEAD_DIM)
        o_ref[:, csl] = (o * g_ref[:, csl]).astype(BF16)
    kot_ref[0, :, 0:win - ts] = ckt_ref[0, :, ts:win]
    kot_ref[0, :, win - ts:win] = knt_ref[0, :, 0:ts]
    vot_ref[0, :, 0:win - ts] = cvt_ref[0, :, ts:win]
    vot_ref[0, :, win - ts:win] = vnt_ref[0, :, 0:ts]


def _attn_sample(q, knt, vnt, gate, ckt, cvt, ts):
    batch, _, win = ckt.shape
    tp = SAMPLE_PAD
    blk = lambda b: (b, 0)
    per_b = lambda b: (b, 0, 0)
    return pl.pallas_call(
        functools.partial(_attn_sample_kernel, ts=ts),
        grid=(batch,),
        in_specs=[
            pl.BlockSpec((tp, ATT_INNER), blk),
            pl.BlockSpec((1, KV_DIM, tp), per_b),
            pl.BlockSpec((1, KV_DIM, tp), per_b),
            pl.BlockSpec((tp, ATT_INNER), blk),
            pl.BlockSpec((1, KV_DIM, win), per_b),
            pl.BlockSpec((1, KV_DIM, win), per_b),
        ],
        out_specs=[
            pl.BlockSpec((tp, ATT_INNER), blk),
            pl.BlockSpec((1, KV_DIM, win), per_b),
            pl.BlockSpec((1, KV_DIM, win), per_b),
        ],
        out_shape=[
            jax.ShapeDtypeStruct((batch * tp, ATT_INNER), BF16),
            jax.ShapeDtypeStruct((batch, KV_DIM, win), F32),
            jax.ShapeDtypeStruct((batch, KV_DIM, win), F32),
        ],
        compiler_params=pltpu.CompilerParams(
            dimension_semantics=("arbitrary",), vmem_limit_bytes=VMEM_LIMIT),
        name="attn_sample",
    )(q, knt, vnt, gate, ckt, cvt)


def _group_norm(yg, nw):
    ms = jnp.mean(yg * yg, axis=-1, keepdims=True)
    return (yg * lax.rsqrt(ms + EPS) * nw).astype(BF16)


def _outproj_kernel(x_ref, ys_ref, yat_ref, nw_ref, w_ref, o_ref):
    acc = x_ref[...]
    for g in range(SSD_GROUPS):
        gs = slice(g * HALF_INNER, (g + 1) * HALF_INNER)
        acc = acc + _dot(_group_norm(ys_ref[:, gs], nw_ref[:, gs]), w_ref[gs, :])
    ya = yat_ref[0].astype(F32).T.astype(BF16)
    o_ref[...] = acc + _dot(ya, w_ref[SSD_INNER:, :])


def _out_proj(x2d, y_ssd, y_att_t, nw, w_o, tm):
    rows = x2d.shape[0]
    per_seq = y_att_t.shape[2] // tm
    row = lambda i: (i, 0)
    const = lambda i: (0, 0)
    return pl.pallas_call(
        _outproj_kernel,
        grid=(rows // tm,),
        in_specs=[
            pl.BlockSpec((tm, D_MODEL), row),
            pl.BlockSpec((tm, SSD_INNER), row),
            pl.BlockSpec((1, ATT_INNER, tm), lambda i: (i // per_seq, 0, i % per_seq)),
            pl.BlockSpec((1, SSD_INNER), const),
            pl.BlockSpec((SSD_INNER + ATT_INNER, D_MODEL), const, pipeline_mode=pl.Buffered(1)),
        ],
        out_specs=pl.BlockSpec((tm, D_MODEL), row),
        out_shape=jax.ShapeDtypeStruct((rows, D_MODEL), F32),
        compiler_params=pltpu.CompilerParams(
            dimension_semantics=("arbitrary",), vmem_limit_bytes=VMEM_LIMIT),
        name="out_proj",
    )(x2d, y_ssd, y_att_t, nw, w_o)


def _wconv_out_kernel(x_ref, ys_ref, ya_ref, nw_ref, w_ref, wb_ref, o_ref):
    j = pl.program_id(0)
    wb = w_ref[...].astype(BF16)
    wb_ref[...] = wb
    y = jnp.where(j < SSD_GROUPS, _group_norm(ys_ref[...], nw_ref[...]), ya_ref[...])
    part = _dot(y, wb)
    bs, ts, _ = x_ref.shape

    @pl.when(j == 0)
    def _():
        for b in range(bs):
            o_ref[b] = x_ref[b] + part[b * SAMPLE_PAD:b * SAMPLE_PAD + ts]

    @pl.when(j > 0)
    def _():
        for b in range(bs):
            o_ref[b] += part[b * SAMPLE_PAD:b * SAMPLE_PAD + ts]


def _wconv_out(x_s, y_ssd, y_att, nw, w_f32):
    rows = x_s.shape[0] * SAMPLE_PAD
    n_ssd = SSD_GROUPS
    n_chunks = (SSD_INNER + ATT_INNER) // WO_CHUNK
    const = lambda j: (0, 0)
    ssd_chunk = lambda j: (0, jnp.minimum(j, n_ssd - 1))
    return pl.pallas_call(
        _wconv_out_kernel,
        grid=(n_chunks,),
        in_specs=[
            pl.BlockSpec(x_s.shape, lambda j: (0, 0, 0)),
            pl.BlockSpec((rows, WO_CHUNK), ssd_chunk),
            pl.BlockSpec((rows, WO_CHUNK), lambda j: (0, jnp.maximum(j - n_ssd, 0))),
            pl.BlockSpec((1, WO_CHUNK), ssd_chunk),
            pl.BlockSpec((WO_CHUNK, D_MODEL), lambda j: (j, 0)),
        ],
        out_specs=[
            pl.BlockSpec((WO_CHUNK, D_MODEL), lambda j: (j, 0)),
            pl.BlockSpec(x_s.shape, lambda j: (0, 0, 0)),
        ],
        out_shape=[
            jax.ShapeDtypeStruct((SSD_INNER + ATT_INNER, D_MODEL), BF16),
            jax.ShapeDtypeStruct(x_s.shape, F32),
        ],
        compiler_params=pltpu.CompilerParams(
            dimension_semantics=("arbitrary",), vmem_limit_bytes=VMEM_LIMIT),
        name="wconv_out",
    )(x_s, y_ssd, y_att, nw, w_f32)


def _rope_tables(pos):
    n = pos.shape[0]
    half = ROT_DIM // 2
    inv = ROPE_THETA ** (-np.arange(0, ROT_DIM, 2, dtype=np.float64) / ROT_DIM)
    ang = pos.astype(np.float64)[:, None] * inv[None, :]
    cos, sin = np.cos(ang), np.sin(ang)
    rest = HEAD_DIM - ROT_DIM
    zh = np.zeros((n, half))
    cos_h = np.concatenate([cos, cos, np.ones((n, rest))], axis=1)
    sa_h = np.concatenate([zh, sin, np.zeros((n, rest))], axis=1)
    sb_h = np.concatenate([-sin, zh, np.zeros((n, rest))], axis=1)
    rep = LANES // HEAD_DIM
    return tuple(jnp.asarray(np.tile(t, (1, rep)), dtype=F32) for t in (cos_h, sa_h, sb_h))


def _lane_pad(v, n=LANES):
    return jnp.pad(v, (0, n - v.shape[0])).reshape(1, n)


def _to_cache(xt, batch, seq):
    return xt.reshape(1, batch, ATT_KV_HEADS, HEAD_DIM, seq).transpose(0, 1, 4, 2, 3)


def kernel(x_prompt, x_sample, cache_k, cache_v, state_conv, state_ssm, norm_w, w_in, conv_w,
           conv_b, dt_bias, a_log, d_skip, ssd_norm_w, q_norm_w, k_norm_w, w_out):
    bp, tp_, _ = x_prompt.shape
    bs, ts, _ = x_sample.shape
    depth = w_in.shape[0]
    assert depth == 1 and tp_ % SSD_CHUNK == 0 and ts <= SAMPLE_PAD and ts >= CONV_W - 1
    l = 0
    win = cache_k.shape[2]

    nw = norm_w[l].reshape(1, D_MODEL)
    rep = LANES // HEAD_DIM
    qnw = jnp.tile(q_norm_w[l], rep).reshape(1, LANES)
    knw = jnp.tile(k_norm_w[l], rep).reshape(1, LANES)
    cw = jnp.pad(conv_w[l], ((0, SUBLANES - CONV_W), (0, 0)))
    cb = conv_b[l].reshape(1, CONV_DIM)
    dtb = _lane_pad(dt_bias[l])
    alog = _lane_pad(a_log[l])
    dexp = jnp.repeat(d_skip[l], SSD_HEADDIM).reshape(1, SSD_INNER)
    snw = ssd_norm_w[l].reshape(1, SSD_INNER)

    pad = SAMPLE_PAD
    tabs_s = _rope_tables(PAST_LEN + np.arange(bs * pad) % pad)
    w_t, proj_s = _wconv_in(x_sample, nw, jnp.swapaxes(w_in[l], 0, 1))
    zs, xbc, dtr, q, kt, vt, gs, _ = _sections(proj_s, qnw, knw, tabs_s)
    cprev = jnp.pad(state_conv[l], ((0, 0), (SUBLANES - (CONV_W - 1), 0), (0, 0)))
    y_ssd, h_s = _ssd(xbc, dtr, zs, cprev, state_ssm[l].reshape(bs, SSD_INNER, SSD_STATE),
                      cw, cb, dtb, alog, dexp, bs, pad, ts)
    knt = kt.reshape(KV_DIM, bs, pad).transpose(1, 0, 2)
    vnt = vt.reshape(KV_DIM, bs, pad).transpose(1, 0, 2)
    ckt = cache_k[l].transpose(0, 2, 3, 1).reshape(bs, KV_DIM, win)
    cvt = cache_v[l].transpose(0, 2, 3, 1).reshape(bs, KV_DIM, win)
    y_att, kot, vot = _attn_sample(q, knt, vnt, gs, ckt, cvt, ts)
    w_o, y_s = _wconv_out(x_sample, y_ssd, y_att, snw, w_out[l])
    k_s = _to_cache(kot, bs, win)
    v_s = _to_cache(vot, bs, win)
    c_s = xbc.reshape(bs, pad, CONV_DIM)[:, ts - (CONV_W - 1):ts][None]
    h_s = h_s.reshape(1, bs, SSD_HEADS, SSD_HEADDIM, SSD_STATE)

    tm = TM_IN
    xp2 = x_prompt.reshape(bp * tp_, D_MODEL)
    tabs = _rope_tables(np.arange(tp_))
    zs, xbc, dtr, q, kt, vt, gs, ctail = _in_proj(xp2, nw, w_t, qnw, knw, tabs, tm, tp_)
    y_ssd, h_p = _ssd(xbc, dtr, zs, jnp.zeros((bp, SUBLANES, CONV_DIM), F32),
                      jnp.zeros((bp, SSD_INNER, SSD_STATE), F32),
                      cw, cb, dtb, alog, dexp, bp, SSD_CHUNK, SSD_CHUNK)
    y_att = _attn_prompt(q, kt, vt, gs, bp, tp_)
    y_p = _out_proj(xp2, y_ssd, y_att, snw, w_o, TM_OUT).reshape(bp, tp_, D_MODEL)
    keep = min(W_MAX, tp_)
    k_p = _to_cache(kt, bp, tp_)[:, :, tp_ - keep:]
    v_p = _to_cache(vt, bp, tp_)[:, :, tp_ - keep:]
    c_p = ctail[:, SUBLANES - (CONV_W - 1):][None]
    h_p = h_p.reshape(1, bp, SSD_HEADS, SSD_HEADDIM, SSD_STATE)

    return (y_p, y_s, k_p, v_p, c_p, h_p, k_s, v_s, c_s, h_s)
```

```python
import functools

import jax
import jax.numpy as jnp
import numpy as np
from jax import lax
from jax.experimental import pallas as pl
from jax.experimental.pallas import tpu as pltpu

F32 = jnp.float32
BF16 = jnp.bfloat16

D_MODEL = 2048
SSD_HEADS = 16
SSD_HEADDIM = 64
SSD_INNER = SSD_HEADS * SSD_HEADDIM
SSD_GROUPS = 2
SSD_STATE = 128
CONV_W = 4
CONV_DIM = SSD_INNER + 2 * SSD_GROUPS * SSD_STATE
SSD_CHUNK = 128
ATT_HEADS = 16
ATT_KV_HEADS = 4
HEAD_DIM = 64
ATT_GQ = ATT_HEADS // ATT_KV_HEADS
ATT_INNER = ATT_HEADS * HEAD_DIM
KV_DIM = ATT_KV_HEADS * HEAD_DIM
ROT_DIM = HEAD_DIM // 4
ROPE_THETA = 500000.0
DILATED_BRANCHES = ((128, 1), (512, 4), (2048, 16))
W_MAX = 2048
PAST_LEN = 16384
EPS = 1e-6

LANES = 128
SUBLANES = 8
Q_BLOCK = 256
K_SUPER = 256
Q_PER_SUPER = K_SUPER // Q_BLOCK
ONES_ROWS = 16
ACC_ROWS = HEAD_DIM + ONES_ROWS
CONV_COLS = 256
TM_IN = 512
TM_OUT = 512
SSD_SEQS = 4
SAMPLE_PAD = 16
NEG = -1e30
VMEM_LIMIT = 58 * 1024 * 1024

Z0 = 0
X0 = Z0 + SSD_INNER
DT0 = X0 + CONV_DIM
Q0 = DT0 + SSD_HEADS
K0 = Q0 + ATT_INNER
V0 = K0 + KV_DIM
G0 = V0 + KV_DIM
W_ROWS = G0 + ATT_INNER
W_CHUNK = 1024
HALF_INNER = SSD_INNER // SSD_GROUPS
WO_CHUNK = HALF_INNER
LOG2E = 1.4426950408889634
Q_SCALE = HEAD_DIM ** -0.5 * LOG2E


def _dot(a, b):
    return jnp.dot(a, b, preferred_element_type=F32)


def _dot_nt(a, b):
    return lax.dot_general(a, b, (((1,), (1,)), ((), ())), preferred_element_type=F32)


def _split3(x):
    hi = x.astype(BF16)
    r1 = x - hi.astype(F32)
    mid = r1.astype(BF16)
    lo = (r1 - mid.astype(F32)).astype(BF16)
    return hi, mid, lo


def _dot_exact_rhs(x, m):
    hi, mid, lo = _split3(x)
    return _dot(hi, m) + _dot(mid, m) + _dot(lo, m)


def _dot_wide_rhs(x, m):
    hi = x.astype(BF16)
    lo = (x - hi.astype(F32)).astype(BF16)
    return _dot(hi, m) + _dot(lo, m)


def _dot_exact_lhs(m, x):
    hi, mid, lo = _split3(x)
    return _dot(m, hi) + _dot(m, mid) + _dot(m, lo)


def _silu(x):
    hx = 0.5 * x
    return hx * jnp.tanh(hx) + hx


def _multiplicity(d):
    w = jnp.zeros(d.shape, F32)
    for window, dil in DILATED_BRANCHES:
        hit = (d >= 0) & (d <= window) & (lax.rem(d, dil) == 0)
        w = w + jnp.where(hit, 1.0, 0.0)
    return w


def _norm_rope(y, nw, cos, sa, sb, scale):
    lane = lax.broadcasted_iota(jnp.int32, (1, LANES), 1)
    first = lane < HEAD_DIM
    y2 = y * y
    s_lo = jnp.sum(jnp.where(first, y2, 0.0), axis=-1, keepdims=True)
    s_hi = jnp.sum(jnp.where(first, 0.0, y2), axis=-1, keepdims=True)
    ms = jnp.where(first, s_lo, s_hi) * (1.0 / HEAD_DIM)
    yn = y * lax.rsqrt(ms + EPS) * nw
    half = ROT_DIM // 2
    rot = yn * cos + pltpu.roll(yn, half, 1) * sa + pltpu.roll(yn, LANES - half, 1) * sb
    return rot * scale


def _causal_conv_silu(load, cw_ref, cb_ref, store):
    for c0 in range(0, CONV_DIM, CONV_COLS):
        cs = slice(c0, c0 + CONV_COLS)
        xe = load(cs)
        x2 = pltpu.roll(xe, 2, 0)
        even = cw_ref[3:4, cs] * xe + cw_ref[1:2, cs] * x2
        odd = cw_ref[2:3, cs] * xe + cw_ref[0:1, cs] * x2
        store(cs, _silu((even + pltpu.roll(odd, 1, 0))[SUBLANES:, :] + cb_ref[:, cs]))


def _pre_norm(x_ref, nw_ref):
    x = x_ref[...]
    ms = jnp.mean(x * x, axis=-1, keepdims=True)
    return (x * lax.rsqrt(ms + EPS) * nw_ref[...]).astype(BF16)


def _inproj_kernel(x_ref, nw_ref, w_ref, *refs):
    hn = _pre_norm(x_ref, nw_ref)

    def proj(a, b):
        return _dot_nt(hn, w_ref[a:b, :])

    _emit_sections(proj, x_ref.shape[0], *refs)


def _emit_sections(proj, tm, qnw_ref, knw_ref, cos_ref, sa_ref, sb_ref,
                   zs_ref, xbc_ref, dt_ref, q_ref, kt_ref, vt_ref, gs_ref, ctail_ref):
    cos, sa, sb = cos_ref[...], sa_ref[...], sb_ref[...]
    q = proj(Q0, K0)
    for c in range(ATT_INNER // LANES):
        sl = slice(c * LANES, (c + 1) * LANES)
        qn = _norm_rope(q[:, sl], qnw_ref[...], cos, sa, sb, Q_SCALE)
        if len(q_ref.shape) == 3:
            q_ref[0, sl, :] = qn.T.astype(BF16)
        else:
            q_ref[:, sl] = qn.astype(BF16)
    k = proj(K0, V0)
    kn = [_norm_rope(k[:, c * LANES:(c + 1) * LANES], knw_ref[...], cos, sa, sb, 1.0)
          for c in range(KV_DIM // LANES)]
    kt_ref[0] = jnp.concatenate(kn, axis=1).T

    xbc = proj(X0, DT0)
    xbc_ref[...] = xbc
    ctail_ref[0] = xbc[tm - SUBLANES:tm, :]
    vt_ref[0] = proj(V0, G0).T
    zs_ref[...] = _silu(proj(Z0, X0))
    gs = _silu(proj(G0, W_ROWS))
    if len(gs_ref.shape) == 3:
        gs_ref[0] = gs.T
    else:
        gs_ref[...] = gs
    dt_ref[...] = proj(DT0, DT0 + LANES)


def _in_proj(x2d, nw, w_t, qnw, knw, tables, tm, seq):
    rows = x2d.shape[0]
    const = lambda i: (0, 0)
    row = lambda i: (i, 0)
    sec_in, out_specs, out_shape = _section_specs(rows, tm, seq, tables[0].shape[0] // tm, True)
    return pl.pallas_call(
        _inproj_kernel,
        grid=(rows // tm,),
        in_specs=[
            pl.BlockSpec((tm, D_MODEL), row),
            pl.BlockSpec((1, D_MODEL), const),
            pl.BlockSpec((W_ROWS, D_MODEL), const, pipeline_mode=pl.Buffered(1)),
        ] + sec_in,
        out_specs=out_specs,
        out_shape=out_shape,
        compiler_params=pltpu.CompilerParams(
            dimension_semantics=("arbitrary",), vmem_limit_bytes=VMEM_LIMIT),
        name="in_proj",
    )(x2d, nw, w_t, qnw, knw, *tables)


def _section_specs(rows, tm, seq, period, q_transposed):
    per_seq = seq // tm
    const = lambda i: (0, 0)
    row = lambda i: (i, 0)
    tab = lambda i: (i % period, 0)
    tr = lambda i: (i // per_seq, 0, i % per_seq)
    slab = lambda i: (i // per_seq, 0, 0)
    in_specs = [pl.BlockSpec((1, LANES), const)] * 2 + [pl.BlockSpec((tm, LANES), tab)] * 3
    row_outs = {0: (SSD_INNER, F32), 1: (CONV_DIM, F32), 2: (LANES, F32), 3: (ATT_INNER, BF16),
                6: (ATT_INNER, F32)}
    out_specs, out_shape = [], []
    for idx in range(7):
        if idx in (3, 6) and q_transposed:
            dt = row_outs[idx][1]
            out_specs.append(pl.BlockSpec((1, ATT_INNER, tm), tr))
            out_shape.append(jax.ShapeDtypeStruct((rows // seq, ATT_INNER, seq), dt))
        elif idx in row_outs:
            n, dt = row_outs[idx]
            out_specs.append(pl.BlockSpec((tm, n), row))
            out_shape.append(jax.ShapeDtypeStruct((rows, n), dt))
        else:
            out_specs.append(pl.BlockSpec((1, KV_DIM, tm), tr))
            out_shape.append(jax.ShapeDtypeStruct((rows // seq, KV_DIM, seq), F32))
    out_specs.append(pl.BlockSpec((1, SUBLANES, CONV_DIM), slab))
    out_shape.append(jax.ShapeDtypeStruct((rows // seq, SUBLANES, CONV_DIM), F32))
    return in_specs, out_specs, out_shape


def _wconv_in_kernel(x_ref, nw_ref, w_ref, wb_ref, p_ref, xpad_scr, hn_scr):
    @pl.when(pl.program_id(0) == 0)
    def _():
        bs, ts, _ = x_ref.shape
        xpad_scr[...] = jnp.zeros(xpad_scr.shape, F32)
        for b in range(bs):
            xpad_scr[b * SAMPLE_PAD:b * SAMPLE_PAD + ts, :] = x_ref[b]
        hn_scr[...] = _pre_norm(xpad_scr, nw_ref)

    row = pl.program_id(0) * W_CHUNK + lax.broadcasted_iota(jnp.int32, (W_CHUNK, 1), 0)
    wb = jnp.where(row < W_ROWS, w_ref[...], 0.0).astype(BF16)
    wb_ref[...] = wb
    p_ref[0] = _dot_nt(hn_scr[...], wb)


def _wconv_in(x_s, nw, w_f32):
    rows = x_s.shape[0] * SAMPLE_PAD
    n_chunks = pl.cdiv(W_ROWS, W_CHUNK)
    const = lambda j: (0, 0)
    return pl.pallas_call(
        _wconv_in_kernel,
        grid=(n_chunks,),
        in_specs=[
            pl.BlockSpec(x_s.shape, lambda j: (0, 0, 0)),
            pl.BlockSpec((1, D_MODEL), const),
            pl.BlockSpec((W_CHUNK, D_MODEL), lambda j: (j, 0)),
        ],
        out_specs=[
            pl.BlockSpec((W_CHUNK, D_MODEL), lambda j: (j, 0)),
            pl.BlockSpec((1, rows, W_CHUNK), lambda j: (j, 0, 0)),
        ],
        out_shape=[
            jax.ShapeDtypeStruct((W_ROWS, D_MODEL), BF16),
            jax.ShapeDtypeStruct((n_chunks, rows, W_CHUNK), F32),
        ],
        scratch_shapes=[pltpu.VMEM((rows, D_MODEL), F32), pltpu.VMEM((rows, D_MODEL), BF16)],
        compiler_params=pltpu.CompilerParams(
            dimension_semantics=("arbitrary",), vmem_limit_bytes=VMEM_LIMIT),
        name="wconv_in",
    )(x_s, nw, w_f32)


def _sections_kernel(p_ref, *refs):
    *refs, tail_scr = refs
    pfull = jnp.concatenate([p_ref[j] for j in range(p_ref.shape[0])], axis=1)
    tail_scr[...] = pfull[:, Q0:W_ROWS]

    def proj(a, b):
        if a >= Q0:
            return tail_scr[:, a - Q0:b - Q0]
        return pfull[:, a:b]

    _emit_sections(proj, p_ref.shape[1], *refs)


def _sections(p, qnw, knw, tables):
    rows = p.shape[1]
    sec_in, out_specs, out_shape = _section_specs(rows, rows, rows, 1, False)
    return pl.pallas_call(
        _sections_kernel,
        grid=(1,),
        in_specs=[pl.BlockSpec(p.shape, lambda i: (0, 0, 0))] + sec_in,
        out_specs=out_specs,
        out_shape=out_shape,
        scratch_shapes=[pltpu.VMEM((rows, W_ROWS - Q0), F32)],
        compiler_params=pltpu.CompilerParams(
            dimension_semantics=("arbitrary",), vmem_limit_bytes=VMEM_LIMIT),
        name="sections",
    )(p, qnw, knw, *tables)


def _ssd_chain(s, xbc_ref, dtr_ref, z_ref, cprev_ref, h0_ref, cw_ref, cb_ref, dtb_ref, alog_ref,
               dexp_ref, y_ref, hout_ref, xext, ht, xc_scr, lin, valid):
    L = SSD_CHUNK

    def pad_rows(v):
        if lin == L:
            return v
        return jnp.concatenate([v, jnp.zeros((L - lin, v.shape[1]), v.dtype)], axis=0)

    lane = lax.broadcasted_iota(jnp.int32, (1, LANES), 1)
    rowid = lax.broadcasted_iota(jnp.int32, (L, 1), 0)
    dt = jax.nn.softplus(pad_rows(dtr_ref[s]) + dtb_ref[...])
    dt = jnp.where((lane < SSD_HEADS) & (rowid < valid), dt, 0.0)
    dta = dt * (-jnp.exp(alog_ref[...]) * LOG2E)

    r2 = lax.broadcasted_iota(jnp.int32, (L, L), 0)
    c2 = lax.broadcasted_iota(jnp.int32, (L, L), 1)
    tri = r2 >= c2
    tri_b = jnp.where(tri, 1.0, 0.0).astype(BF16)
    cum = _dot_exact_lhs(tri_b, dta)
    cum_last = cum[L - 1:L, :]
    yield

    er = lax.broadcasted_iota(jnp.int32, (LANES, SSD_INNER), 0)
    ec = lax.broadcasted_iota(jnp.int32, (LANES, SSD_INNER), 1)
    expand = jnp.where(ec // SSD_HEADDIM == er, 1.0, 0.0).astype(BF16)
    ecum = _dot_wide_rhs(jnp.exp2(cum), expand)
    wexp = _dot_wide_rhs(jnp.exp2(cum_last - cum) * dt, expand)
    cd = _dot_exact_rhs(jnp.broadcast_to(jnp.exp2(cum_last), (SUBLANES, LANES)), expand)[0:1, :]
    cum_t = cum.T
    dt_t = dt.T
    yield

    if lin < L:
        xext[s, SUBLANES + lin:SUBLANES + L, :] = jnp.zeros((L - lin, CONV_DIM), F32)
    xext[s, SUBLANES:SUBLANES + lin, :] = xbc_ref[s]

    def load(cs):
        return xext[s, 0:SUBLANES + L, cs]

    def store(cs, v):
        xc_scr[s, :, cs] = v

    _causal_conv_silu(load, cw_ref, cb_ref, store)
    xext[s, 0:SUBLANES, :] = xext[s, L:L + SUBLANES, :]
    xc = xc_scr.at[s]
    xs = xc[:, :SSD_INNER]
    bm = xc[:, SSD_INNER:SSD_INNER + SSD_GROUPS * SSD_STATE]
    cm = xc[:, SSD_INNER + SSD_GROUPS * SSD_STATE:]
    bm_b = bm.astype(BF16)
    cm_b = cm.astype(BF16)
    xs_b = xs.astype(BF16)

    y_off = jnp.concatenate(
        [_dot(cm_b[:, g * SSD_STATE:(g + 1) * SSD_STATE], ht[s, g].astype(BF16))
         for g in range(SSD_GROUPS)], axis=1) * ecum
    cbs = [_dot_nt(cm_b[:, g * SSD_STATE:(g + 1) * SSD_STATE], bm_b[:, g * SSD_STATE:(g + 1) * SSD_STATE])
           for g in range(SSD_GROUPS)]
    yield

    lane_half = lane // HEAD_DIM
    heads_per_group = SSD_HEADS // SSD_GROUPS
    y_parts = []
    for g in range(SSD_GROUPS):
        for pr in range(heads_per_group // 2):
            col0 = g * HALF_INNER + pr * LANES
            xp = xs_b[:, col0:col0 + LANES]
            yp = jnp.zeros((L, LANES), F32)
            for e in range(2):
                hh = g * heads_per_group + pr * 2 + e
                decay = jnp.exp2(cum[:, hh:hh + 1] - cum_t[hh:hh + 1, :])
                sc = jnp.where(tri, cbs[g] * decay, 0.0) * dt_t[hh:hh + 1, :]
                xm = jnp.where(lane_half == e, xp, jnp.zeros_like(xp))
                yp = yp + _dot(sc.astype(BF16), xm)
            y_parts.append(yp)
            if pr % 2 == 1:
                yield
    y_diag = jnp.concatenate(y_parts, axis=1)

    xw = (xs * wexp).astype(BF16)
    for g in range(SSD_GROUPS):
        sl = slice(g * HALF_INNER, (g + 1) * HALF_INNER)
        bm_t = bm[:, g * SSD_STATE:(g + 1) * SSD_STATE].T.astype(BF16)
        ht[s, g] = ht[s, g] * cd[:, sl] + _dot(bm_t, xw[:, sl])
    yield

    y = (y_diag + y_off + dexp_ref[...] * xs) * pad_rows(z_ref[s])
    y_ref[s] = y[:lin]


def _ssd_kernel(*refs, lin, valid):
    cprev_ref, h0_ref = refs[3:5]
    hout_ref, xext, ht = refs[11:14]
    c = pl.program_id(1)

    def for_each_state(fn):
        def step(j, carry):
            g = j % SSD_GROUPS
            fn(j // SSD_GROUPS, g, pl.multiple_of(g * HALF_INNER, HALF_INNER))
            return carry

        lax.fori_loop(0, SSD_SEQS * SSD_GROUPS, step, 0)

    @pl.when(c == 0)
    def _():
        for s in range(SSD_SEQS):
            xext[s, 0:SUBLANES, :] = cprev_ref[s]

        def load_state(s, g, r0):
            ht[s, g] = h0_ref[s, pl.ds(r0, HALF_INNER), :].T

        for_each_state(load_state)

    chains = [_ssd_chain(s, *refs, lin, valid) for s in range(SSD_SEQS)]
    while chains:
        alive = []
        for ch in chains:
            try:
                next(ch)
                alive.append(ch)
            except StopIteration:
                pass
        chains = alive

    @pl.when(c == pl.num_programs(1) - 1)
    def _():
        def store_state(s, g, r0):
            hout_ref[s, pl.ds(r0, HALF_INNER), :] = ht[s, g].T

        for_each_state(store_state)


def _ssd(xbc, dtr, z, cprev, h0, cw, cb, dtb, alog, dexp, batch, lin, valid):
    rows = xbc.shape[0]
    seq = rows // batch
    nc = seq // lin
    g = SSD_SEQS
    assert batch % g == 0
    blk = lambda b, c: (b, c, 0)
    per_b = lambda b, c: (b, 0, 0)
    const = lambda b, c: (0, 0)
    y, hout = pl.pallas_call(
        functools.partial(_ssd_kernel, lin=lin, valid=valid),
        grid=(batch // g, nc),
        in_specs=[
            pl.BlockSpec((g, lin, CONV_DIM), blk),
            pl.BlockSpec((g, lin, LANES), blk),
            pl.BlockSpec((g, lin, SSD_INNER), blk),
            pl.BlockSpec((g, SUBLANES, CONV_DIM), per_b),
            pl.BlockSpec((g, SSD_INNER, SSD_STATE), per_b),
            pl.BlockSpec((SUBLANES, CONV_DIM), const),
            pl.BlockSpec((1, CONV_DIM), const),
            pl.BlockSpec((1, LANES), const),
            pl.BlockSpec((1, LANES), const),
            pl.BlockSpec((1, SSD_INNER), const),
        ],
        out_specs=[
            pl.BlockSpec((g, lin, SSD_INNER), blk),
            pl.BlockSpec((g, SSD_INNER, SSD_STATE), per_b),
        ],
        out_shape=[
            jax.ShapeDtypeStruct((batch, seq, SSD_INNER), F32),
            jax.ShapeDtypeStruct((batch, SSD_INNER, SSD_STATE), F32),
        ],
        scratch_shapes=[
            pltpu.VMEM((g, SUBLANES + SSD_CHUNK + SUBLANES, CONV_DIM), F32),
            pltpu.VMEM((g, SSD_GROUPS, SSD_STATE, HALF_INNER), F32),
            pltpu.VMEM((g, SSD_CHUNK, CONV_DIM), F32),
        ],
        compiler_params=pltpu.CompilerParams(
            dimension_semantics=("arbitrary", "arbitrary"), vmem_limit_bytes=VMEM_LIMIT),
        name="ssd",
    )(xbc.reshape(batch, seq, CONV_DIM), dtr.reshape(batch, seq, LANES),
      z.reshape(batch, seq, SSD_INNER), cprev, h0, cw, cb, dtb, alog, dexp)
    return y.reshape(rows, SSD_INNER), hout


def _attn_prompt_kernel(q_ref, kt_ref, vt_ref, g_ref, o_ref,
                        bias_scr, kh_scr, va_scr, qt_scr, m_scr, acc_scr, s_scr, s2_scr, *, n_tab, n_sb_max):
    b = pl.program_id(0)
    i = pl.program_id(1)
    seq = kt_ref.shape[2]
    pair_w = 2 * Q_BLOCK
    pairs_per_kv = ATT_GQ // 2

    @pl.when((b == 0) & (i == 0))
    def _():
        r = lax.broadcasted_iota(jnp.int32, (K_SUPER, Q_BLOCK), 0)
        c = lax.broadcasted_iota(jnp.int32, (K_SUPER, Q_BLOCK), 1)
        def table(tb, carry):
            w = _multiplicity(tb * Q_BLOCK + c - r)
            bias_scr[tb] = jnp.where(w > 0.0, jnp.log2(jnp.maximum(w, 1.0)), NEG)
            return carry

        lax.fori_loop(0, n_tab, table, 0)
        va_scr[:, HEAD_DIM:, :] = jnp.ones((ATT_KV_HEADS, ONES_ROWS, seq), BF16)

    @pl.when(i == 0)
    def _():
        for kvh in range(ATT_KV_HEADS):
            hs = slice(kvh * HEAD_DIM, (kvh + 1) * HEAD_DIM)
            va_scr[kvh, 0:HEAD_DIM, :] = vt_ref[0, hs, :].astype(BF16)
        for cidx in range(seq // K_SUPER):
            rows = slice(cidx * K_SUPER, (cidx + 1) * K_SUPER)
            kc = kt_ref[0, :, rows].T
            for kvh in range(ATT_KV_HEADS):
                hs = slice(kvh * HEAD_DIM, (kvh + 1) * HEAD_DIM)
                kh_scr[kvh, rows, :] = kc[:, hs].astype(BF16)

    for kvh in range(ATT_KV_HEADS):
        qt_scr[kvh] = jnp.concatenate(
            [q_ref[0, (kvh * ATT_GQ + g) * HEAD_DIM:(kvh * ATT_GQ + g + 1) * HEAD_DIM, :]
             for g in range(ATT_GQ)], axis=1)
    m_scr[...] = jnp.full(m_scr.shape, NEG, F32)
    acc_scr[...] = jnp.zeros(acc_scr.shape, F32)
    par = i % Q_PER_SUPER
    j_last = i // Q_PER_SUPER
    n_sb = jnp.minimum(j_last + 1, n_sb_max)

    def key_start(dl):
        return pl.multiple_of((j_last - dl) * K_SUPER, K_SUPER)

    def scores(dl, dst):
        for kvh in range(ATT_KV_HEADS):
            dst[kvh] = _dot(kh_scr[kvh, pl.ds(key_start(dl), K_SUPER), :], qt_scr[kvh])

    def softmax_pv(dl, src):
        start = key_start(dl)
        bias = bias_scr[par + Q_PER_SUPER * dl]
        bias2 = jnp.concatenate([bias, bias], axis=1)
        for kvh in range(ATT_KV_HEADS):
            vtb = va_scr[kvh, :, pl.ds(start, K_SUPER)]
            for pr in range(pairs_per_kv):
                u = kvh * pairs_per_kv + pr
                s = src[kvh, :, pr * pair_w:(pr + 1) * pair_w] + bias2
                m_old = m_scr[u, 0:1, :]
                m_new = jnp.maximum(m_old, jnp.max(s, axis=0, keepdims=True))
                p = jnp.exp2(s - m_new).astype(BF16)
                acc_scr[u] = jnp.exp2(m_old - m_new) * acc_scr[u] + _dot(vtb, p)
                m_scr[u, 0:1, :] = m_new

    def diagonal_block():
        half = K_SUPER // 2
        start = key_start(0)
        bias = bias_scr[0]
        for kvh in range(ATT_KV_HEADS):
            s_scr[kvh, 0:half, :] = _dot(kh_scr[kvh, pl.ds(start, half), :], qt_scr[kvh])
        for kvh in range(ATT_KV_HEADS):
            q_late = jnp.concatenate(
                [qt_scr[kvh, :, g * Q_BLOCK + half:(g + 1) * Q_BLOCK] for g in range(ATT_GQ)], axis=1)
            s2_scr[kvh, 0:half, 0:ATT_GQ * half] = _dot(
                kh_scr[kvh, pl.ds(start + half, half), :], q_late)
        bias_a = jnp.concatenate([bias[0:half, :]] * 2, axis=1)
        for kvh in range(ATT_KV_HEADS):
            vtb = va_scr[kvh, :, pl.ds(start, half)]
            for pr in range(pairs_per_kv):
                u = kvh * pairs_per_kv + pr
                s = s_scr[kvh, 0:half, pr * pair_w:(pr + 1) * pair_w] + bias_a
                m_old = m_scr[u, 0:1, :]
                m_new = jnp.maximum(m_old, jnp.max(s, axis=0, keepdims=True))
                p = jnp.exp2(s - m_new).astype(BF16)
                acc_scr[u] = jnp.exp2(m_old - m_new) * acc_scr[u] + _dot(vtb, p)
                m_scr[u, 0:1, :] = m_new
        bias_b = jnp.concatenate([bias[half:, half:]] * 2, axis=1)
        late = [slice(e * Q_BLOCK + half, (e + 1) * Q_BLOCK) for e in range(2)]
        for kvh in range(ATT_KV_HEADS):
            vtb = va_scr[kvh, :, pl.ds(start + half, half)]
            for pr in range(pairs_per_kv):
                u = kvh * pairs_per_kv + pr
                s = s2_scr[kvh, 0:half, pr * Q_BLOCK:(pr + 1) * Q_BLOCK] + bias_b
                m_old = jnp.concatenate([m_scr[u, 0:1, sl] for sl in late], axis=1)
                m_new = jnp.maximum(m_old, jnp.max(s, axis=0, keepdims=True))
                p = jnp.exp2(s - m_new).astype(BF16)
                alpha = jnp.exp2(m_old - m_new)
                pv = _dot(vtb, p)
                for e, sl in enumerate(late):
                    es = slice(e * half, (e + 1) * half)
                    acc_scr[u, :, sl] = alpha[:, es] * acc_scr[u, :, sl] + pv[:, es]
                    m_scr[u, 0:1, sl] = m_new[:, es]

    diagonal_block()
    n_rest = n_sb - 1

    def body(t, carry):
        scores(1 + 2 * t, s_scr)
        scores(2 + 2 * t, s2_scr)
        softmax_pv(1 + 2 * t, s_scr)
        softmax_pv(2 + 2 * t, s2_scr)
        return carry

    lax.fori_loop(0, n_rest // 2, body, 0)

    @pl.when(n_rest % 2 == 1)
    def _():
        scores(n_sb - 1, s_scr)
        softmax_pv(n_sb - 1, s_scr)

    for u in range(ATT_HEADS // 2):
        acc = acc_scr[u]
        o = acc[0:HEAD_DIM] * (1.0 / acc[HEAD_DIM:HEAD_DIM + 1])
        for e in range(2):
            hs = slice((2 * u + e) * HEAD_DIM, (2 * u + e + 1) * HEAD_DIM)
            o_ref[0, hs, :] = (o[:, e * Q_BLOCK:(e + 1) * Q_BLOCK] * g_ref[0, hs, :]).astype(BF16)


def _attn_prompt(qt, kt, vt, gate_t, batch, seq):
    nqb = seq // Q_BLOCK
    assert seq % K_SUPER == 0 and Q_BLOCK == K_SUPER
    n_sb_max = min(seq // K_SUPER, W_MAX // K_SUPER + 1)
    n_tab = Q_PER_SUPER * n_sb_max
    qblk = lambda b, i: (b, 0, i)
    per_b = lambda b, i: (b, 0, 0)
    return pl.pallas_call(
        functools.partial(_attn_prompt_kernel, n_tab=n_tab, n_sb_max=n_sb_max),
        grid=(batch, nqb),
        in_specs=[
            pl.BlockSpec((1, ATT_INNER, Q_BLOCK), qblk),
            pl.BlockSpec((1, KV_DIM, seq), per_b),
            pl.BlockSpec((1, KV_DIM, seq), per_b),
            pl.BlockSpec((1, ATT_INNER, Q_BLOCK), qblk),
        ],
        out_specs=pl.BlockSpec((1, ATT_INNER, Q_BLOCK), qblk),
        out_shape=jax.ShapeDtypeStruct((batch, ATT_INNER, seq), BF16),
        scratch_shapes=[
            pltpu.VMEM((n_tab, K_SUPER, Q_BLOCK), F32),
            pltpu.VMEM((ATT_KV_HEADS, seq, HEAD_DIM), BF16),
            pltpu.VMEM((ATT_KV_HEADS, ACC_ROWS, seq), BF16),
            pltpu.VMEM((ATT_KV_HEADS, HEAD_DIM, ATT_GQ * Q_BLOCK), BF16),
            pltpu.VMEM((ATT_HEADS // 2, SUBLANES, 2 * Q_BLOCK), F32),
            pltpu.VMEM((ATT_HEADS // 2, ACC_ROWS, 2 * Q_BLOCK), F32),
            pltpu.VMEM((ATT_KV_HEADS, K_SUPER, ATT_GQ * Q_BLOCK), F32),
            pltpu.VMEM((ATT_KV_HEADS, K_SUPER, ATT_GQ * Q_BLOCK), F32),
        ],
        compiler_params=pltpu.CompilerParams(
            dimension_semantics=("arbitrary", "arbitrary"), vmem_limit_bytes=VMEM_LIMIT),
        name="attn_prompt",
    )(qt, kt, vt, gate_t)


def _attn_sample_kernel(q_ref, knt_ref, vnt_ref, g_ref, ckt_ref, cvt_ref, o_ref, kot_ref, vot_ref, *, ts):
    win = ckt_ref.shape[2]
    tp = SAMPLE_PAD
    rows = ATT_GQ * tp
    t_c = lax.broadcasted_iota(jnp.int32, (rows, win), 0) % tp
    j_c = lax.broadcasted_iota(jnp.int32, (rows, win), 1)
    w_c = _multiplicity(win + t_c - j_c)
    t_n = lax.broadcasted_iota(jnp.int32, (rows, tp), 0) % tp
    j_n = lax.broadcasted_iota(jnp.int32, (rows, tp), 1)
    w_n = jnp.where(j_n < ts, _multiplicity(t_n - j_n), 0.0)
    for kvh in range(ATT_KV_HEADS):
        c0 = kvh * ATT_GQ * HEAD_DIM
        qh = jnp.concatenate(
            [q_ref[:, c0 + g * HEAD_DIM:c0 + (g + 1) * HEAD_DIM] for g in range(ATT_GQ)], axis=0)
        hs = slice(kvh * HEAD_DIM, (kvh + 1) * HEAD_DIM)
        s_c = jnp.where(w_c > 0.0, _dot(qh, ckt_ref[0, hs, :].astype(BF16)), NEG)
        s_n = jnp.where(w_n > 0.0, _dot(qh, knt_ref[0, hs, :].astype(BF16)), NEG)
        m = jnp.maximum(jnp.max(s_c, axis=-1, keepdims=True), jnp.max(s_n, axis=-1, keepdims=True))
        p_c = jnp.exp2(s_c - m) * w_c
        p_n = jnp.exp2(s_n - m) * w_n
        den = jnp.sum(p_c, axis=-1, keepdims=True) + jnp.sum(p_n, axis=-1, keepdims=True)
        num = (_dot_nt(p_c.astype(BF16), cvt_ref[0, hs, :].astype(BF16))
               + _dot_nt(p_n.astype(BF16), vnt_ref[0, hs, :].astype(BF16)))
        o = num / den
        o = jnp.concatenate([o[g * tp:(g + 1) * tp] for g in range(ATT_GQ)], axis=1)
        csl = slice(c0, c0 + ATT_GQ * HEAD_DIM)
        o_ref[:, csl] = (o * g_ref[:, csl]).astype(BF16)
    kot_ref[0, :, 0:win - ts] = ckt_ref[0, :, ts:win]
    kot_ref[0, :, win - ts:win] = knt_ref[0, :, 0:ts]
    vot_ref[0, :, 0:win - ts] = cvt_ref[0, :, ts:win]
    vot_ref[0, :, win - ts:win] = vnt_ref[0, :, 0:ts]


def _attn_sample(q, knt, vnt, gate, ckt, cvt, ts):
    batch, _, win = ckt.shape
    tp = SAMPLE_PAD
    blk = lambda b: (b, 0)
    per_b = lambda b: (b, 0, 0)
    return pl.pallas_call(
        functools.partial(_attn_sample_kernel, ts=ts),
        grid=(batch,),
        in_specs=[
            pl.BlockSpec((tp, ATT_INNER), blk),
            pl.BlockSpec((1, KV_DIM, tp), per_b),
            pl.BlockSpec((1, KV_DIM, tp), per_b),
            pl.BlockSpec((tp, ATT_INNER), blk),
            pl.BlockSpec((1, KV_DIM, win), per_b),
            pl.BlockSpec((1, KV_DIM, win), per_b),
        ],
        out_specs=[
            pl.BlockSpec((tp, ATT_INNER), blk),
            pl.BlockSpec((1, KV_DIM, win), per_b),
            pl.BlockSpec((1, KV_DIM, win), per_b),
        ],
        out_shape=[
            jax.ShapeDtypeStruct((batch * tp, ATT_INNER), BF16),
            jax.ShapeDtypeStruct((batch, KV_DIM, win), F32),
            jax.ShapeDtypeStruct((batch, KV_DIM, win), F32),
        ],
        compiler_params=pltpu.CompilerParams(
            dimension_semantics=("arbitrary",), vmem_limit_bytes=VMEM_LIMIT),
        name="attn_sample",
    )(q, knt, vnt, gate, ckt, cvt)


def _group_norm(yg, nw):
    ms = jnp.mean(yg * yg, axis=-1, keepdims=True)
    return (yg * lax.rsqrt(ms + EPS) * nw).astype(BF16)


def _outproj_kernel(x_ref, ys_ref, yat_ref, nw_ref, w_ref, o_ref):
    acc = x_ref[...]
    for g in range(SSD_GROUPS):
        gs = slice(g * HALF_INNER, (g + 1) * HALF_INNER)
        acc = acc + _dot(_group_norm(ys_ref[:, gs], nw_ref[:, gs]), w_ref[gs, :])
    ya = yat_ref[0].astype(F32).T.astype(BF16)
    o_ref[...] = acc + _dot(ya, w_ref[SSD_INNER:, :])


def _out_proj(x2d, y_ssd, y_att_t, nw, w_o, tm):
    rows = x2d.shape[0]
    per_seq = y_att_t.shape[2] // tm
    row = lambda i: (i, 0)
    const = lambda i: (0, 0)
    return pl.pallas_call(
        _outproj_kernel,
        grid=(rows // tm,),
        in_specs=[
            pl.BlockSpec((tm, D_MODEL), row),
            pl.BlockSpec((tm, SSD_INNER), row),
            pl.BlockSpec((1, ATT_INNER, tm), lambda i: (i // per_seq, 0, i % per_seq)),
            pl.BlockSpec((1, SSD_INNER), const),
            pl.BlockSpec((SSD_INNER + ATT_INNER, D_MODEL), const, pipeline_mode=pl.Buffered(1)),
        ],
        out_specs=pl.BlockSpec((tm, D_MODEL), row),
        out_shape=jax.ShapeDtypeStruct((rows, D_MODEL), F32),
        compiler_params=pltpu.CompilerParams(
            dimension_semantics=("arbitrary",), vmem_limit_bytes=VMEM_LIMIT),
        name="out_proj",
    )(x2d, y_ssd, y_att_t, nw, w_o)


def _wconv_out_kernel(x_ref, ys_ref, ya_ref, nw_ref, w_ref, wb_ref, o_ref):
    j = pl.program_id(0)
    wb = w_ref[...].astype(BF16)
    wb_ref[...] = wb
    y = jnp.where(j < SSD_GROUPS, _group_norm(ys_ref[...], nw_ref[...]), ya_ref[...])
    part = _dot(y, wb)
    bs, ts, _ = x_ref.shape

    @pl.when(j == 0)
    def _():
        for b in range(bs):
            o_ref[b] = x_ref[b] + part[b * SAMPLE_PAD:b * SAMPLE_PAD + ts]

    @pl.when(j > 0)
    def _():
        for b in range(bs):
            o_ref[b] += part[b * SAMPLE_PAD:b * SAMPLE_PAD + ts]


def _wconv_out(x_s, y_ssd, y_att, nw, w_f32):
    rows = x_s.shape[0] * SAMPLE_PAD
    n_ssd = SSD_GROUPS
    n_chunks = (SSD_INNER + ATT_INNER) // WO_CHUNK
    const = lambda j: (0, 0)
    ssd_chunk = lambda j: (0, jnp.minimum(j, n_ssd - 1))
    return pl.pallas_call(
        _wconv_out_kernel,
        grid=(n_chunks,),
        in_specs=[
            pl.BlockSpec(x_s.shape, lambda j: (0, 0, 0)),
            pl.BlockSpec((rows, WO_CHUNK), ssd_chunk),
            pl.BlockSpec((rows, WO_CHUNK), lambda j: (0, jnp.maximum(j - n_ssd, 0))),
            pl.BlockSpec((1, WO_CHUNK), ssd_chunk),
            pl.BlockSpec((WO_CHUNK, D_MODEL), lambda j: (j, 0)),
        ],
        out_specs=[
            pl.BlockSpec((WO_CHUNK, D_MODEL), lambda j: (j, 0)),
            pl.BlockSpec(x_s.shape, lambda j: (0, 0, 0)),
        ],
        out_shape=[
            jax.ShapeDtypeStruct((SSD_INNER + ATT_INNER, D_MODEL), BF16),
            jax.ShapeDtypeStruct(x_s.shape, F32),
        ],
        compiler_params=pltpu.CompilerParams(
            dimension_semantics=("arbitrary",), vmem_limit_bytes=VMEM_LIMIT),
        name="wconv_out",
    )(x_s, y_ssd, y_att, nw, w_f32)


def _rope_tables(pos):
    n = pos.shape[0]
    half = ROT_DIM // 2
    inv = ROPE_THETA ** (-np.arange(0, ROT_DIM, 2, dtype=np.float64) / ROT_DIM)
    ang = pos.astype(np.float64)[:, None] * inv[None, :]
    cos, sin = np.cos(ang), np.sin(ang)
    rest = HEAD_DIM - ROT_DIM
    zh = np.zeros((n, half))
    cos_h = np.concatenate([cos, cos, np.ones((n, rest))], axis=1)
    sa_h = np.concatenate([zh, sin, np.zeros((n, rest))], axis=1)
    sb_h = np.concatenate([-sin, zh, np.zeros((n, rest))], axis=1)
    rep = LANES // HEAD_DIM
    return tuple(jnp.asarray(np.tile(t, (1, rep)), dtype=F32) for t in (cos_h, sa_h, sb_h))


def _lane_pad(v, n=LANES):
    return jnp.pad(v, (0, n - v.shape[0])).reshape(1, n)


def _to_cache(xt, batch, seq):
    return xt.reshape(1, batch, ATT_KV_HEADS, HEAD_DIM, seq).transpose(0, 1, 4, 2, 3)


def kernel(x_prompt, x_sample, cache_k, cache_v, state_conv, state_ssm, norm_w, w_in, conv_w,
           conv_b, dt_bias, a_log, d_skip, ssd_norm_w, q_norm_w, k_norm_w, w_out):
    bp, tp_, _ = x_prompt.shape
    bs, ts, _ = x_sample.shape
    depth = w_in.shape[0]
    assert depth == 1 and tp_ % SSD_CHUNK == 0 and ts <= SAMPLE_PAD and ts >= CONV_W - 1
    l = 0
    win = cache_k.shape[2]

    nw = norm_w[l].reshape(1, D_MODEL)
    rep = LANES // HEAD_DIM
    qnw = jnp.tile(q_norm_w[l], rep).reshape(1, LANES)
    knw = jnp.tile(k_norm_w[l], rep).reshape(1, LANES)
    cw = jnp.pad(conv_w[l], ((0, SUBLANES - CONV_W), (0, 0)))
    cb = conv_b[l].reshape(1, CONV_DIM)
    dtb = _lane_pad(dt_bias[l])
    alog = _lane_pad(a_log[l])
    dexp = jnp.repeat(d_skip[l], SSD_HEADDIM).reshape(1, SSD_INNER)
    snw = ssd_norm_w[l].reshape(1, SSD_INNER)

    pad = SAMPLE_PAD
    tabs_s = _rope_tables(PAST_LEN + np.arange(bs * pad) % pad)
    w_t, proj_s = _wconv_in(x_sample, nw, jnp.swapaxes(w_in[l], 0, 1))
    zs, xbc, dtr, q, kt, vt, gs, _ = _sections(proj_s, qnw, knw, tabs_s)
    cprev = jnp.pad(state_conv[l], ((0, 0), (SUBLANES - (CONV_W - 1), 0), (0, 0)))
    y_ssd, h_s = _ssd(xbc, dtr, zs, cprev, state_ssm[l].reshape(bs, SSD_INNER, SSD_STATE),
                      cw, cb, dtb, alog, dexp, bs, pad, ts)
    knt = kt.reshape(KV_DIM, bs, pad).transpose(1, 0, 2)
    vnt = vt.reshape(KV_DIM, bs, pad).transpose(1, 0, 2)
    ckt = cache_k[l].transpose(0, 2, 3, 1).reshape(bs, KV_DIM, win)
    cvt = cache_v[l].transpose(0, 2, 3, 1).reshape(bs, KV_DIM, win)
    y_att, kot, vot = _attn_sample(q, knt, vnt, gs, ckt, cvt, ts)
    w_o, y_s = _wconv_out(x_sample, y_ssd, y_att, snw, w_out[l])
    k_s = _to_cache(kot, bs, win)
    v_s = _to_cache(vot, bs, win)
    c_s = xbc.reshape(bs, pad, CONV_DIM)[:, ts - (CONV_W - 1):ts][None]
    h_s = h_s.reshape(1, bs, SSD_HEADS, SSD_HEADDIM, SSD_STATE)

    tm = TM_IN
    xp2 = x_prompt.reshape(bp * tp_, D_MODEL)
    tabs = _rope_tables(np.arange(tp_))
    zs, xbc, dtr, q, kt, vt, gs, ctail = _in_proj(xp2, nw, w_t, qnw, knw, tabs, tm, tp_)
    y_ssd, h_p = _ssd(xbc, dtr, zs, jnp.zeros((bp, SUBLANES, CONV_DIM), F32),
                      jnp.zeros((bp, SSD_INNER, SSD_STATE), F32),
                      cw, cb, dtb, alog, dexp, bp, SSD_CHUNK, SSD_CHUNK)
    y_att = _attn_prompt(q, kt, vt, gs, bp, tp_)
    y_p = _out_proj(xp2, y_ssd, y_att, snw, w_o, TM_OUT).reshape(bp, tp_, D_MODEL)
    keep = min(W_MAX, tp_)
    k_p = _to_cache(kt, bp, tp_)[:, :, tp_ - keep:]
    v_p = _to_cache(vt, bp, tp_)[:, :, tp_ - keep:]
    c_p = ctail[:, SUBLANES - (CONV_W - 1):][None]
    h_p = h_p.reshape(1, bp, SSD_HEADS, SSD_HEADDIM, SSD_STATE)

    return (y_p, y_s, k_p, v_p, c_p, h_p, k_s, v_s, c_s, h_s)
```

```python
import functools

import jax
import jax.numpy as jnp
import numpy as np
from jax import lax
from jax.experimental import pallas as pl
from jax.experimental.pallas import tpu as pltpu

F32 = jnp.float32
BF16 = jnp.bfloat16

D_MODEL = 2048
SSD_HEADS = 16
SSD_HEADDIM = 64
SSD_INNER = SSD_HEADS * SSD_HEADDIM
SSD_GROUPS = 2
SSD_STATE = 128
CONV_W = 4
CONV_DIM = SSD_INNER + 2 * SSD_GROUPS * SSD_STATE
SSD_CHUNK = 128
ATT_HEADS = 16
ATT_KV_HEADS = 4
HEAD_DIM = 64
ATT_GQ = ATT_HEADS // ATT_KV_HEADS
ATT_INNER = ATT_HEADS * HEAD_DIM
KV_DIM = ATT_KV_HEADS * HEAD_DIM
ROT_DIM = HEAD_DIM // 4
ROPE_THETA = 500000.0
DILATED_BRANCHES = ((128, 1), (512, 4), (2048, 16))
W_MAX = 2048
PAST_LEN = 16384
EPS = 1e-6

LANES = 128
SUBLANES = 8
Q_BLOCK = 256
K_SUPER = 256
Q_PER_SUPER = K_SUPER // Q_BLOCK
ONES_ROWS = 16
ACC_ROWS = HEAD_DIM + ONES_ROWS
CONV_COLS = 256
TM_IN = 512
TM_OUT = 512
SSD_SEQS = 4
SAMPLE_PAD = 16
NEG = -1e30
VMEM_LIMIT = 58 * 1024 * 1024

Z0 = 0
X0 = Z0 + SSD_INNER
DT0 = X0 + CONV_DIM
Q0 = DT0 + SSD_HEADS
K0 = Q0 + ATT_INNER
V0 = K0 + KV_DIM
G0 = V0 + KV_DIM
W_ROWS = G0 + ATT_INNER
W_CHUNK = 1024
HALF_INNER = SSD_INNER // SSD_GROUPS
WO_CHUNK = HALF_INNER
LOG2E = 1.4426950408889634
Q_SCALE = HEAD_DIM ** -0.5 * LOG2E


def _dot(a, b):
    return jnp.dot(a, b, preferred_element_type=F32)


def _dot_nt(a, b):
    return lax.dot_general(a, b, (((1,), (1,)), ((), ())), preferred_element_type=F32)


def _split3(x):
    hi = x.astype(BF16)
    r1 = x - hi.astype(F32)
    mid = r1.astype(BF16)
    lo = (r1 - mid.astype(F32)).astype(BF16)
    return hi, mid, lo


def _dot_exact_rhs(x, m):
    hi, mid, lo = _split3(x)
    return _dot(hi, m) + _dot(mid, m) + _dot(lo, m)


def _dot_wide_rhs(x, m):
    hi = x.astype(BF16)
    lo = (x - hi.astype(F32)).astype(BF16)
    return _dot(hi, m) + _dot(lo, m)


def _dot_exact_lhs(m, x):
    hi, mid, lo = _split3(x)
    return _dot(m, hi) + _dot(m, mid) + _dot(m, lo)


def _silu(x):
    hx = 0.5 * x
    return hx * jnp.tanh(hx) + hx


def _multiplicity(d):
    w = jnp.zeros(d.shape, F32)
    for window, dil in DILATED_BRANCHES:
        hit = (d >= 0) & (d <= window) & (lax.rem(d, dil) == 0)
        w = w + jnp.where(hit, 1.0, 0.0)
    return w


def _norm_rope(y, nw, cos, sa, sb, scale):
    lane = lax.broadcasted_iota(jnp.int32, (1, LANES), 1)
    first = lane < HEAD_DIM
    y2 = y * y
    s_lo = jnp.sum(jnp.where(first, y2, 0.0), axis=-1, keepdims=True)
    s_hi = jnp.sum(jnp.where(first, 0.0, y2), axis=-1, keepdims=True)
    ms = jnp.where(first, s_lo, s_hi) * (1.0 / HEAD_DIM)
    yn = y * lax.rsqrt(ms + EPS) * nw
    half = ROT_DIM // 2
    rot = yn * cos + pltpu.roll(yn, half, 1) * sa + pltpu.roll(yn, LANES - half, 1) * sb
    return rot * scale


def _causal_conv_silu(load, cw_ref, cb_ref, store):
    for c0 in range(0, CONV_DIM, CONV_COLS):
        cs = slice(c0, c0 + CONV_COLS)
        xe = load(cs)
        x2 = pltpu.roll(xe, 2, 0)
        even = cw_ref[3:4, cs] * xe + cw_ref[1:2, cs] * x2
        odd = cw_ref[2:3, cs] * xe + cw_ref[0:1, cs] * x2
        store(cs, _silu((even + pltpu.roll(odd, 1, 0))[SUBLANES:, :] + cb_ref[:, cs]))


def _pre_norm(x_ref, nw_ref):
    x = x_ref[...]
    ms = jnp.mean(x * x, axis=-1, keepdims=True)
    return (x * lax.rsqrt(ms + EPS) * nw_ref[...]).astype(BF16)


def _inproj_kernel(x_ref, nw_ref, w_ref, *refs):
    hn = _pre_norm(x_ref, nw_ref)

    def proj(a, b):
        return _dot_nt(hn, w_ref[a:b, :])

    _emit_sections(proj, x_ref.shape[0], *refs)


def _emit_sections(proj, tm, qnw_ref, knw_ref, cos_ref, sa_ref, sb_ref,
                   zs_ref, xbc_ref, dt_ref, q_ref, kt_ref, vt_ref, gs_ref, ctail_ref):
    cos, sa, sb = cos_ref[...], sa_ref[...], sb_ref[...]
    q = proj(Q0, K0)
    for c in range(ATT_INNER // LANES):
        sl = slice(c * LANES, (c + 1) * LANES)
        qn = _norm_rope(q[:, sl], qnw_ref[...], cos, sa, sb, Q_SCALE)
        if len(q_ref.shape) == 3:
            q_ref[0, sl, :] = qn.T.astype(BF16)
        else:
            q_ref[:, sl] = qn.astype(BF16)
    k = proj(K0, V0)
    kn = [_norm_rope(k[:, c * LANES:(c + 1) * LANES], knw_ref[...], cos, sa, sb, 1.0)
          for c in range(KV_DIM // LANES)]
    kt_ref[0] = jnp.concatenate(kn, axis=1).T

    xbc = proj(X0, DT0)
    xbc_ref[...] = xbc
    ctail_ref[0] = xbc[tm - SUBLANES:tm, :]
    vt_ref[0] = proj(V0, G0).T
    zs_ref[...] = _silu(proj(Z0, X0))
    gs = _silu(proj(G0, W_ROWS))
    if len(gs_ref.shape) == 3:
        gs_ref[0] = gs.T
    else:
        gs_ref[...] = gs
    dt_ref[...] = proj(DT0, DT0 + LANES)


def _in_proj(x2d, nw, w_t, qnw, knw, tables, tm, seq):
    rows = x2d.shape[0]
    const = lambda i: (0, 0)
    row = lambda i: (i, 0)
    sec_in, out_specs, out_shape = _section_specs(rows, tm, seq, tables[0].shape[0] // tm, True)
    return pl.pallas_call(
        _inproj_kernel,
        grid=(rows // tm,),
        in_specs=[
            pl.BlockSpec((tm, D_MODEL), row),
            pl.BlockSpec((1, D_MODEL), const),
            pl.BlockSpec((W_ROWS, D_MODEL), const, pipeline_mode=pl.Buffered(1)),
        ] + sec_in,
        out_specs=out_specs,
        out_shape=out_shape,
        compiler_params=pltpu.CompilerParams(
            dimension_semantics=("arbitrary",), vmem_limit_bytes=VMEM_LIMIT),
        name="in_proj",
    )(x2d, nw, w_t, qnw, knw, *tables)


def _section_specs(rows, tm, seq, period, q_transposed):
    per_seq = seq // tm
    const = lambda i: (0, 0)
    row = lambda i: (i, 0)
    tab = lambda i: (i % period, 0)
    tr = lambda i: (i // per_seq, 0, i % per_seq)
    slab = lambda i: (i // per_seq, 0, 0)
    in_specs = [pl.BlockSpec((1, LANES), const)] * 2 + [pl.BlockSpec((tm, LANES), tab)] * 3
    row_outs = {0: (SSD_INNER, F32), 1: (CONV_DIM, F32), 2: (LANES, F32), 3: (ATT_INNER, BF16),
                6: (ATT_INNER, F32)}
    out_specs, out_shape = [], []
    for idx in range(7):
        if idx in (3, 6) and q_transposed:
            dt = row_outs[idx][1]
            out_specs.append(pl.BlockSpec((1, ATT_INNER, tm), tr))
            out_shape.append(jax.ShapeDtypeStruct((rows // seq, ATT_INNER, seq), dt))
        elif idx in row_outs:
            n, dt = row_outs[idx]
            out_specs.append(pl.BlockSpec((tm, n), row))
            out_shape.append(jax.ShapeDtypeStruct((rows, n), dt))
        else:
            out_specs.append(pl.BlockSpec((1, KV_DIM, tm), tr))
            out_shape.append(jax.ShapeDtypeStruct((rows // seq, KV_DIM, seq), F32))
    out_specs.append(pl.BlockSpec((1, SUBLANES, CONV_DIM), slab))
    out_shape.append(jax.ShapeDtypeStruct((rows // seq, SUBLANES, CONV_DIM), F32))
    return in_specs, out_specs, out_shape


def _wconv_in_kernel(x_ref, nw_ref, w_ref, wb_ref, p_ref, xpad_scr, hn_scr):
    @pl.when(pl.program_id(0) == 0)
    def _():
        bs, ts, _ = x_ref.shape
        xpad_scr[...] = jnp.zeros(xpad_scr.shape, F32)
        for b in range(bs):
            xpad_scr[b * SAMPLE_PAD:b * SAMPLE_PAD + ts, :] = x_ref[b]
        hn_scr[...] = _pre_norm(xpad_scr, nw_ref)

    row = pl.program_id(0) * W_CHUNK + lax.broadcasted_iota(jnp.int32, (W_CHUNK, 1), 0)
    wb = jnp.where(row < W_ROWS, w_ref[...], 0.0).astype(BF16)
    wb_ref[...] = wb
    p_ref[0] = _dot_nt(hn_scr[...], wb)


def _wconv_in(x_s, nw, w_f32):
    rows = x_s.shape[0] * SAMPLE_PAD
    n_chunks = pl.cdiv(W_ROWS, W_CHUNK)
    const = lambda j: (0, 0)
    return pl.pallas_call(
        _wconv_in_kernel,
        grid=(n_chunks,),
        in_specs=[
            pl.BlockSpec(x_s.shape, lambda j: (0, 0, 0)),
            pl.BlockSpec((1, D_MODEL), const),
            pl.BlockSpec((W_CHUNK, D_MODEL), lambda j: (j, 0)),
        ],
        out_specs=[
            pl.BlockSpec((W_CHUNK, D_MODEL), lambda j: (j, 0)),
            pl.BlockSpec((1, rows, W_CHUNK), lambda j: (j, 0, 0)),
        ],
        out_shape=[
            jax.ShapeDtypeStruct((W_ROWS, D_MODEL), BF16),
            jax.ShapeDtypeStruct((n_chunks, rows, W_CHUNK), F32),
        ],
        scratch_shapes=[pltpu.VMEM((rows, D_MODEL), F32), pltpu.VMEM((rows, D_MODEL), BF16)],
        compiler_params=pltpu.CompilerParams(
            dimension_semantics=("arbitrary",), vmem_limit_bytes=VMEM_LIMIT),
        name="wconv_in",
    )(x_s, nw, w_f32)


def _sections_kernel(p_ref, *refs):
    *refs, tail_scr = refs
    pfull = jnp.concatenate([p_ref[j] for j in range(p_ref.shape[0])], axis=1)
    tail_scr[...] = pfull[:, Q0:W_ROWS]

    def proj(a, b):
        if a >= Q0:
            return tail_scr[:, a - Q0:b - Q0]
        return pfull[:, a:b]

    _emit_sections(proj, p_ref.shape[1], *refs)


def _sections(p, qnw, knw, tables):
    rows = p.shape[1]
    sec_in, out_specs, out_shape = _section_specs(rows, rows, rows, 1, False)
    return pl.pallas_call(
        _sections_kernel,
        grid=(1,),
        in_specs=[pl.BlockSpec(p.shape, lambda i: (0, 0, 0))] + sec_in,
        out_specs=out_specs,
        out_shape=out_shape,
        scratch_shapes=[pltpu.VMEM((rows, W_ROWS - Q0), F32)],
        compiler_params=pltpu.CompilerParams(
            dimension_semantics=("arbitrary",), vmem_limit_bytes=VMEM_LIMIT),
        name="sections",
    )(p, qnw, knw, *tables)


def _ssd_chain(s, xbc_ref, dtr_ref, z_ref, cprev_ref, h0_ref, cw_ref, cb_ref, dtb_ref, alog_ref,
               dexp_ref, y_ref, hout_ref, xext, ht, xc_scr, lin, valid):
    L = SSD_CHUNK

    def pad_rows(v):
        if lin == L:
            return v
        return jnp.concatenate([v, jnp.zeros((L - lin, v.shape[1]), v.dtype)], axis=0)

    lane = lax.broadcasted_iota(jnp.int32, (1, LANES), 1)
    rowid = lax.broadcasted_iota(jnp.int32, (L, 1), 0)
    dt = jax.nn.softplus(pad_rows(dtr_ref[s]) + dtb_ref[...])
    dt = jnp.where((lane < SSD_HEADS) & (rowid < valid), dt, 0.0)
    dta = dt * (-jnp.exp(alog_ref[...]) * LOG2E)

    r2 = lax.broadcasted_iota(jnp.int32, (L, L), 0)
    c2 = lax.broadcasted_iota(jnp.int32, (L, L), 1)
    tri = r2 >= c2
    tri_b = jnp.where(tri, 1.0, 0.0).astype(BF16)
    cum = _dot_exact_lhs(tri_b, dta)
    cum_last = cum[L - 1:L, :]
    yield

    er = lax.broadcasted_iota(jnp.int32, (LANES, SSD_INNER), 0)
    ec = lax.broadcasted_iota(jnp.int32, (LANES, SSD_INNER), 1)
    expand = jnp.where(ec // SSD_HEADDIM == er, 1.0, 0.0).astype(BF16)
    ecum = _dot_wide_rhs(jnp.exp2(cum), expand)
    wexp = _dot_wide_rhs(jnp.exp2(cum_last - cum) * dt, expand)
    cd = _dot_exact_rhs(jnp.broadcast_to(jnp.exp2(cum_last), (SUBLANES, LANES)), expand)[0:1, :]
    cum_t = cum.T
    dt_t = dt.T
    yield

    if lin < L:
        xext[s, SUBLANES + lin:SUBLANES + L, :] = jnp.zeros((L - lin, CONV_DIM), F32)
    xext[s, SUBLANES:SUBLANES + lin, :] = xbc_ref[s]

    def load(cs):
        return xext[s, 0:SUBLANES + L, cs]

    def store(cs, v):
        xc_scr[s, :, cs] = v

    _causal_conv_silu(load, cw_ref, cb_ref, store)
    xext[s, 0:SUBLANES, :] = xext[s, L:L + SUBLANES, :]
    xc = xc_scr.at[s]
    xs = xc[:, :SSD_INNER]
    bm = xc[:, SSD_INNER:SSD_INNER + SSD_GROUPS * SSD_STATE]
    cm = xc[:, SSD_INNER + SSD_GROUPS * SSD_STATE:]
    bm_b = bm.astype(BF16)
    cm_b = cm.astype(BF16)
    xs_b = xs.astype(BF16)

    y_off = jnp.concatenate(
        [_dot(cm_b[:, g * SSD_STATE:(g + 1) * SSD_STATE], ht[s, g].astype(BF16))
         for g in range(SSD_GROUPS)], axis=1) * ecum
    cbs = [_dot_nt(cm_b[:, g * SSD_STATE:(g + 1) * SSD_STATE], bm_b[:, g * SSD_STATE:(g + 1) * SSD_STATE])
           for g in range(SSD_GROUPS)]
    yield

    lane_half = lane // HEAD_DIM
    heads_per_group = SSD_HEADS // SSD_GROUPS
    y_parts = []
    for g in range(SSD_GROUPS):
        for pr in range(heads_per_group // 2):
            col0 = g * HALF_INNER + pr * LANES
            xp = xs_b[:, col0:col0 + LANES]
            yp = jnp.zeros((L, LANES), F32)
            for e in range(2):
                hh = g * heads_per_group + pr * 2 + e
                decay = jnp.exp2(cum[:, hh:hh + 1] - cum_t[hh:hh + 1, :])
                sc = jnp.where(tri, cbs[g] * decay, 0.0) * dt_t[hh:hh + 1, :]
                xm = jnp.where(lane_half == e, xp, jnp.zeros_like(xp))
                yp = yp + _dot(sc.astype(BF16), xm)
            y_parts.append(yp)
            if pr % 2 == 1:
                yield
    y_diag = jnp.concatenate(y_parts, axis=1)

    xw = (xs * wexp).astype(BF16)
    for g in range(SSD_GROUPS):
        sl = slice(g * HALF_INNER, (g + 1) * HALF_INNER)
        bm_t = bm[:, g * SSD_STATE:(g + 1) * SSD_STATE].T.astype(BF16)
        ht[s, g] = ht[s, g] * cd[:, sl] + _dot(bm_t, xw[:, sl])
    yield

    y = (y_diag + y_off + dexp_ref[...] * xs) * pad_rows(z_ref[s])
    y_ref[s] = y[:lin]


def _ssd_kernel(*refs, lin, valid):
    cprev_ref, h0_ref = refs[3:5]
    hout_ref, xext, ht = refs[11:14]
    c = pl.program_id(1)

    def for_each_state(fn):
        def step(j, carry):
            g = j % SSD_GROUPS
            fn(j // SSD_GROUPS, g, pl.multiple_of(g * HALF_INNER, HALF_INNER))
            return carry

        lax.fori_loop(0, SSD_SEQS * SSD_GROUPS, step, 0, unroll=2)

    @pl.when(c == 0)
    def _():
        for s in range(SSD_SEQS):
            xext[s, 0:SUBLANES, :] = cprev_ref[s]

        def load_state(s, g, r0):
            ht[s, g] = h0_ref[s, pl.ds(r0, HALF_INNER), :].T

        for_each_state(load_state)

    chains = [_ssd_chain(s, *refs, lin, valid) for s in range(SSD_SEQS)]
    while chains:
        alive = []
        for ch in chains:
            try:
                next(ch)
                alive.append(ch)
            except StopIteration:
                pass
        chains = alive

    @pl.when(c == pl.num_programs(1) - 1)
    def _():
        def store_state(s, g, r0):
            hout_ref[s, pl.ds(r0, HALF_INNER), :] = ht[s, g].T

        for_each_state(store_state)


def _ssd(xbc, dtr, z, cprev, h0, cw, cb, dtb, alog, dexp, batch, lin, valid):
    rows = xbc.shape[0]
    seq = rows // batch
    nc = seq // lin
    g = SSD_SEQS
    assert batch % g == 0
    blk = lambda b, c: (b, c, 0)
    per_b = lambda b, c: (b, 0, 0)
    const = lambda b, c: (0, 0)
    y, hout = pl.pallas_call(
        functools.partial(_ssd_kernel, lin=lin, valid=valid),
        grid=(batch // g, nc),
        in_specs=[
            pl.BlockSpec((g, lin, CONV_DIM), blk),
            pl.BlockSpec((g, lin, LANES), blk),
            pl.BlockSpec((g, lin, SSD_INNER), blk),
            pl.BlockSpec((g, SUBLANES, CONV_DIM), per_b),
            pl.BlockSpec((g, SSD_INNER, SSD_STATE), per_b),
            pl.BlockSpec((SUBLANES, CONV_DIM), const),
            pl.BlockSpec((1, CONV_DIM), const),
            pl.BlockSpec((1, LANES), const),
            pl.BlockSpec((1, LANES), const),
            pl.BlockSpec((1, SSD_INNER), const),
        ],
        out_specs=[
            pl.BlockSpec((g, lin, SSD_INNER), blk),
            pl.BlockSpec((g, SSD_INNER, SSD_STATE), per_b),
        ],
        out_shape=[
            jax.ShapeDtypeStruct((batch, seq, SSD_INNER), F32),
            jax.ShapeDtypeStruct((batch, SSD_INNER, SSD_STATE), F32),
        ],
        scratch_shapes=[
            pltpu.VMEM((g, SUBLANES + SSD_CHUNK + SUBLANES, CONV_DIM), F32),
            pltpu.VMEM((g, SSD_GROUPS, SSD_STATE, HALF_INNER), F32),
            pltpu.VMEM((g, SSD_CHUNK, CONV_DIM), F32),
        ],
        compiler_params=pltpu.CompilerParams(
            dimension_semantics=("arbitrary", "arbitrary"), vmem_limit_bytes=VMEM_LIMIT),
        name="ssd",
    )(xbc.reshape(batch, seq, CONV_DIM), dtr.reshape(batch, seq, LANES),
      z.reshape(batch, seq, SSD_INNER), cprev, h0, cw, cb, dtb, alog, dexp)
    return y.reshape(rows, SSD_INNER), hout


def _attn_prompt_kernel(q_ref, kt_ref, vt_ref, g_ref, o_ref,
                        bias_scr, kh_scr, va_scr, qt_scr, m_scr, acc_scr, s_scr, s2_scr, *, n_tab, n_sb_max):
    b = pl.program_id(0)
    i = pl.program_id(1)
    seq = kt_ref.shape[2]
    pair_w = 2 * Q_BLOCK
    pairs_per_kv = ATT_GQ // 2

    @pl.when((b == 0) & (i == 0))
    def _():
        r = lax.broadcasted_iota(jnp.int32, (K_SUPER, Q_BLOCK), 0)
        c = lax.broadcasted_iota(jnp.int32, (K_SUPER, Q_BLOCK), 1)
        def table(tb, carry):
            w = _multiplicity(tb * Q_BLOCK + c - r)
            bias_scr[tb] = jnp.where(w > 0.0, jnp.log2(jnp.maximum(w, 1.0)), NEG)
            return carry

        lax.fori_loop(0, n_tab, table, 0)
        va_scr[:, HEAD_DIM:, :] = jnp.ones((ATT_KV_HEADS, ONES_ROWS, seq), BF16)

    @pl.when(i == 0)
    def _():
        for kvh in range(ATT_KV_HEADS):
            hs = slice(kvh * HEAD_DIM, (kvh + 1) * HEAD_DIM)
            va_scr[kvh, 0:HEAD_DIM, :] = vt_ref[0, hs, :].astype(BF16)
        for cidx in range(seq // K_SUPER):
            rows = slice(cidx * K_SUPER, (cidx + 1) * K_SUPER)
            kc = kt_ref[0, :, rows].T
            for kvh in range(ATT_KV_HEADS):
                hs = slice(kvh * HEAD_DIM, (kvh + 1) * HEAD_DIM)
                kh_scr[kvh, rows, :] = kc[:, hs].astype(BF16)

    for kvh in range(ATT_KV_HEADS):
        qt_scr[kvh] = jnp.concatenate(
            [q_ref[0, (kvh * ATT_GQ + g) * HEAD_DIM:(kvh * ATT_GQ + g + 1) * HEAD_DIM, :]
             for g in range(ATT_GQ)], axis=1)
    m_scr[...] = jnp.full(m_scr.shape, NEG, F32)
    acc_scr[...] = jnp.zeros(acc_scr.shape, F32)
    par = i % Q_PER_SUPER
    j_last = i // Q_PER_SUPER
    n_sb = jnp.minimum(j_last + 1, n_sb_max)

    def key_start(dl):
        return pl.multiple_of((j_last - dl) * K_SUPER, K_SUPER)

    def scores(dl, dst):
        for kvh in range(ATT_KV_HEADS):
            dst[kvh] = _dot(kh_scr[kvh, pl.ds(key_start(dl), K_SUPER), :], qt_scr[kvh])

    def softmax_pv(dl, src):
        start = key_start(dl)
        bias = bias_scr[par + Q_PER_SUPER * dl]
        bias2 = jnp.concatenate([bias, bias], axis=1)
        for kvh in range(ATT_KV_HEADS):
            vtb = va_scr[kvh, :, pl.ds(start, K_SUPER)]
            for pr in range(pairs_per_kv):
                u = kvh * pairs_per_kv + pr
                s = src[kvh, :, pr * pair_w:(pr + 1) * pair_w] + bias2
                m_old = m_scr[u, 0:1, :]
                m_new = jnp.maximum(m_old, jnp.max(s, axis=0, keepdims=True))
                p = jnp.exp2(s - m_new).astype(BF16)
                acc_scr[u] = jnp.exp2(m_old - m_new) * acc_scr[u] + _dot(vtb, p)
                m_scr[u, 0:1, :] = m_new

    def diagonal_block():
        half = K_SUPER // 2
        start = key_start(0)
        bias = bias_scr[0]
        for kvh in range(ATT_KV_HEADS):
            s_scr[kvh, 0:half, :] = _dot(kh_scr[kvh, pl.ds(start, half), :], qt_scr[kvh])
        for kvh in range(ATT_KV_HEADS):
            q_late = jnp.concatenate(
                [qt_scr[kvh, :, g * Q_BLOCK + half:(g + 1) * Q_BLOCK] for g in range(ATT_GQ)], axis=1)
            s2_scr[kvh, 0:half, 0:ATT_GQ * half] = _dot(
                kh_scr[kvh, pl.ds(start + half, half), :], q_late)
        bias_a = jnp.concatenate([bias[0:half, :]] * 2, axis=1)
        for kvh in range(ATT_KV_HEADS):
            vtb = va_scr[kvh, :, pl.ds(start, half)]
            for pr in range(pairs_per_kv):
                u = kvh * pairs_per_kv + pr
                s = s_scr[kvh, 0:half, pr * pair_w:(pr + 1) * pair_w] + bias_a
                m_old = m_scr[u, 0:1, :]
                m_new = jnp.maximum(m_old, jnp.max(s, axis=0, keepdims=True))
                p = jnp.exp2(s - m_new).astype(BF16)
                acc_scr[u] = jnp.exp2(m_old - m_new) * acc_scr[u] + _dot(vtb, p)
                m_scr[u, 0:1, :] = m_new
        bias_b = jnp.concatenate([bias[half:, half:]] * 2, axis=1)
        late = [slice(e * Q_BLOCK + half, (e + 1) * Q_BLOCK) for e in range(2)]
        for kvh in range(ATT_KV_HEADS):
            vtb = va_scr[kvh, :, pl.ds(start + half, half)]
            for pr in range(pairs_per_kv):
                u = kvh * pairs_per_kv + pr
                s = s2_scr[kvh, 0:half, pr * Q_BLOCK:(pr + 1) * Q_BLOCK] + bias_b
                m_old = jnp.concatenate([m_scr[u, 0:1, sl] for sl in late], axis=1)
                m_new = jnp.maximum(m_old, jnp.max(s, axis=0, keepdims=True))
                p = jnp.exp2(s - m_new).astype(BF16)
                alpha = jnp.exp2(m_old - m_new)
                pv = _dot(vtb, p)
                for e, sl in enumerate(late):
                    es = slice(e * half, (e + 1) * half)
                    acc_scr[u, :, sl] = alpha[:, es] * acc_scr[u, :, sl] + pv[:, es]
                    m_scr[u, 0:1, sl] = m_new[:, es]

    diagonal_block()
    n_rest = n_sb - 1

    def body(t, carry):
        scores(1 + 2 * t, s_scr)
        scores(2 + 2 * t, s2_scr)
        softmax_pv(1 + 2 * t, s_scr)
        softmax_pv(2 + 2 * t, s2_scr)
        return carry

    lax.fori_loop(0, n_rest // 2, body, 0)

    @pl.when(n_rest % 2 == 1)
    def _():
        scores(n_sb - 1, s_scr)
        softmax_pv(n_sb - 1, s_scr)

    for u in range(ATT_HEADS // 2):
        acc = acc_scr[u]
        o = acc[0:HEAD_DIM] * (1.0 / acc[HEAD_DIM:HEAD_DIM + 1])
        for e in range(2):
            hs = slice((2 * u + e) * HEAD_DIM, (2 * u + e + 1) * HEAD_DIM)
            o_ref[0, hs, :] = (o[:, e * Q_BLOCK:(e + 1) * Q_BLOCK] * g_ref[0, hs, :]).astype(BF16)


def _attn_prompt(qt, kt, vt, gate_t, batch, seq):
    nqb = seq // Q_BLOCK
    assert seq % K_SUPER == 0 and Q_BLOCK == K_SUPER
    n_sb_max = min(seq // K_SUPER, W_MAX // K_SUPER + 1)
    n_tab = Q_PER_SUPER * n_sb_max
    qblk = lambda b, i: (b, 0, i)
    per_b = lambda b, i: (b, 0, 0)
    return pl.pallas_call(
        functools.partial(_attn_prompt_kernel, n_tab=n_tab, n_sb_max=n_sb_max),
        grid=(batch, nqb),
        in_specs=[
            pl.BlockSpec((1, ATT_INNER, Q_BLOCK), qblk),
            pl.BlockSpec((1, KV_DIM, seq), per_b),
            pl.BlockSpec((1, KV_DIM, seq), per_b),
            pl.BlockSpec((1, ATT_INNER, Q_BLOCK), qblk),
        ],
        out_specs=pl.BlockSpec((1, ATT_INNER, Q_BLOCK), qblk),
        out_shape=jax.ShapeDtypeStruct((batch, ATT_INNER, seq), BF16),
        scratch_shapes=[
            pltpu.VMEM((n_tab, K_SUPER, Q_BLOCK), F32),
            pltpu.VMEM((ATT_KV_HEADS, seq, HEAD_DIM), BF16),
            pltpu.VMEM((ATT_KV_HEADS, ACC_ROWS, seq), BF16),
            pltpu.VMEM((ATT_KV_HEADS, HEAD_DIM, ATT_GQ * Q_BLOCK), BF16),
            pltpu.VMEM((ATT_HEADS // 2, SUBLANES, 2 * Q_BLOCK), F32),
            pltpu.VMEM((ATT_HEADS // 2, ACC_ROWS, 2 * Q_BLOCK), F32),
            pltpu.VMEM((ATT_KV_HEADS, K_SUPER, ATT_GQ * Q_BLOCK), F32),
            pltpu.VMEM((ATT_KV_HEADS, K_SUPER, ATT_GQ * Q_BLOCK), F32),
        ],
        compiler_params=pltpu.CompilerParams(
            dimension_semantics=("arbitrary", "arbitrary"), vmem_limit_bytes=VMEM_LIMIT),
        name="attn_prompt",
    )(qt, kt, vt, gate_t)


def _attn_sample_kernel(q_ref, knt_ref, vnt_ref, g_ref, ckt_ref, cvt_ref, o_ref, kot_ref, vot_ref, *, ts):
    win = ckt_ref.shape[2]
    tp = SAMPLE_PAD
    rows = ATT_GQ * tp
    t_c = lax.broadcasted_iota(jnp.int32, (rows, win), 0) % tp
    j_c = lax.broadcasted_iota(jnp.int32, (rows, win), 1)
    w_c = _multiplicity(win + t_c - j_c)
    t_n = lax.broadcasted_iota(jnp.int32, (rows, tp), 0) % tp
    j_n = lax.broadcasted_iota(jnp.int32, (rows, tp), 1)
    w_n = jnp.where(j_n < ts, _multiplicity(t_n - j_n), 0.0)
    for kvh in range(ATT_KV_HEADS):
        c0 = kvh * ATT_GQ * HEAD_DIM
        qh = jnp.concatenate(
            [q_ref[:, c0 + g * HEAD_DIM:c0 + (g + 1) * HEAD_DIM] for g in range(ATT_GQ)], axis=0)
        hs = slice(kvh * HEAD_DIM, (kvh + 1) * HEAD_DIM)
        s_c = jnp.where(w_c > 0.0, _dot(qh, ckt_ref[0, hs, :].astype(BF16)), NEG)
        s_n = jnp.where(w_n > 0.0, _dot(qh, knt_ref[0, hs, :].astype(BF16)), NEG)
        m = jnp.maximum(jnp.max(s_c, axis=-1, keepdims=True), jnp.max(s_n, axis=-1, keepdims=True))
        p_c = jnp.exp2(s_c - m) * w_c
        p_n = jnp.exp2(s_n - m) * w_n
        den = jnp.sum(p_c, axis=-1, keepdims=True) + jnp.sum(p_n, axis=-1, keepdims=True)
        num = (_dot_nt(p_c.astype(BF16), cvt_ref[0, hs, :].astype(BF16))
               + _dot_nt(p_n.astype(BF16), vnt_ref[0, hs, :].astype(BF16)))
        o = num / den
        o = jnp.concatenate([o[g * tp:(g + 1) * tp] for g in range(ATT_GQ)], axis=1)
        csl = slice(c0, c0 + ATT_GQ * HEAD_DIM)
        o_ref[:, csl] = (o * g_ref[:, csl]).astype(BF16)
    kot_ref[0, :, 0:win - ts] = ckt_ref[0, :, ts:win]
    kot_ref[0, :, win - ts:win] = knt_ref[0, :, 0:ts]
    vot_ref[0, :, 0:win - ts] = cvt_ref[0, :, ts:win]
    vot_ref[0, :, win - ts:win] = vnt_ref[0, :, 0:ts]


def _attn_sample(q, knt, vnt, gate, ckt, cvt, ts):
    batch, _, win = ckt.shape
    tp = SAMPLE_PAD
    blk = lambda b: (b, 0)
    per_b = lambda b: (b, 0, 0)
    return pl.pallas_call(
        functools.partial(_attn_sample_kernel, ts=ts),
        grid=(batch,),
        in_specs=[
            pl.BlockSpec((tp, ATT_INNER), blk),
            pl.BlockSpec((1, KV_DIM, tp), per_b),
            pl.BlockSpec((1, KV_DIM, tp), per_b),
            pl.BlockSpec((tp, ATT_INNER), blk),
            pl.BlockSpec((1, KV_DIM, win), per_b),
            pl.BlockSpec((1, KV_DIM, win), per_b),
        ],
        out_specs=[
            pl.BlockSpec((tp, ATT_INNER), blk),
            pl.BlockSpec((1, KV_DIM, win), per_b),
            pl.BlockSpec((1, KV_DIM, win), per_b),
        ],
        out_shape=[
            jax.ShapeDtypeStruct((batch * tp, ATT_INNER), BF16),
            jax.ShapeDtypeStruct((batch, KV_DIM, win), F32),
            jax.ShapeDtypeStruct((batch, KV_DIM, win), F32),
        ],
        compiler_params=pltpu.CompilerParams(
            dimension_semantics=("arbitrary",), vmem_limit_bytes=VMEM_LIMIT),
        name="attn_sample",
    )(q, knt, vnt, gate, ckt, cvt)


def _group_norm(yg, nw):
    ms = jnp.mean(yg * yg, axis=-1, keepdims=True)
    return (yg * lax.rsqrt(ms + EPS) * nw).astype(BF16)


def _outproj_kernel(x_ref, ys_ref, yat_ref, nw_ref, w_ref, o_ref):
    acc = x_ref[...]
    for g in range(SSD_GROUPS):
        gs = slice(g * HALF_INNER, (g + 1) * HALF_INNER)
        acc = acc + _dot(_group_norm(ys_ref[:, gs], nw_ref[:, gs]), w_ref[gs, :])
    ya = yat_ref[0].astype(F32).T.astype(BF16)
    o_ref[...] = acc + _dot(ya, w_ref[SSD_INNER:, :])


def _out_proj(x2d, y_ssd, y_att_t, nw, w_o, tm):
    rows = x2d.shape[0]
    per_seq = y_att_t.shape[2] // tm
    row = lambda i: (i, 0)
    const = lambda i: (0, 0)
    return pl.pallas_call(
        _outproj_kernel,
        grid=(rows // tm,),
        in_specs=[
            pl.BlockSpec((tm, D_MODEL), row),
            pl.BlockSpec((tm, SSD_INNER), row),
            pl.BlockSpec((1, ATT_INNER, tm), lambda i: (i // per_seq, 0, i % per_seq)),
            pl.BlockSpec((1, SSD_INNER), const),
            pl.BlockSpec((SSD_INNER + ATT_INNER, D_MODEL), const, pipeline_mode=pl.Buffered(1)),
        ],
        out_specs=pl.BlockSpec((tm, D_MODEL), row),
        out_shape=jax.ShapeDtypeStruct((rows, D_MODEL), F32),
        compiler_params=pltpu.CompilerParams(
            dimension_semantics=("arbitrary",), vmem_limit_bytes=VMEM_LIMIT),
        name="out_proj",
    )(x2d, y_ssd, y_att_t, nw, w_o)


def _wconv_out_kernel(x_ref, ys_ref, ya_ref, nw_ref, w_ref, wb_ref, o_ref):
    j = pl.program_id(0)
    wb = w_ref[...].astype(BF16)
    wb_ref[...] = wb
    y = jnp.where(j < SSD_GROUPS, _group_norm(ys_ref[...], nw_ref[...]), ya_ref[...])
    part = _dot(y, wb)
    bs, ts, _ = x_ref.shape

    @pl.when(j == 0)
    def _():
        for b in range(bs):
            o_ref[b] = x_ref[b] + part[b * SAMPLE_PAD:b * SAMPLE_PAD + ts]

    @pl.when(j > 0)
    def _():
        for b in range(bs):
            o_ref[b] += part[b * SAMPLE_PAD:b * SAMPLE_PAD + ts]


def _wconv_out(x_s, y_ssd, y_att, nw, w_f32):
    rows = x_s.shape[0] * SAMPLE_PAD
    n_ssd = SSD_GROUPS
    n_chunks = (SSD_INNER + ATT_INNER) // WO_CHUNK
    const = lambda j: (0, 0)
    ssd_chunk = lambda j: (0, jnp.minimum(j, n_ssd - 1))
    return pl.pallas_call(
        _wconv_out_kernel,
        grid=(n_chunks,),
        in_specs=[
            pl.BlockSpec(x_s.shape, lambda j: (0, 0, 0)),
            pl.BlockSpec((rows, WO_CHUNK), ssd_chunk),
            pl.BlockSpec((rows, WO_CHUNK), lambda j: (0, jnp.maximum(j - n_ssd, 0))),
            pl.BlockSpec((1, WO_CHUNK), ssd_chunk),
            pl.BlockSpec((WO_CHUNK, D_MODEL), lambda j: (j, 0)),
        ],
        out_specs=[
            pl.BlockSpec((WO_CHUNK, D_MODEL), lambda j: (j, 0)),
            pl.BlockSpec(x_s.shape, lambda j: (0, 0, 0)),
        ],
        out_shape=[
            jax.ShapeDtypeStruct((SSD_INNER + ATT_INNER, D_MODEL), BF16),
            jax.ShapeDtypeStruct(x_s.shape, F32),
        ],
        compiler_params=pltpu.CompilerParams(
            dimension_semantics=("arbitrary",), vmem_limit_bytes=VMEM_LIMIT),
        name="wconv_out",
    )(x_s, y_ssd, y_att, nw, w_f32)


def _rope_tables(pos):
    n = pos.shape[0]
    half = ROT_DIM // 2
    inv = ROPE_THETA ** (-np.arange(0, ROT_DIM, 2, dtype=np.float64) / ROT_DIM)
    ang = pos.astype(np.float64)[:, None] * inv[None, :]
    cos, sin = np.cos(ang), np.sin(ang)
    rest = HEAD_DIM - ROT_DIM
    zh = np.zeros((n, half))
    cos_h = np.concatenate([cos, cos, np.ones((n, rest))], axis=1)
    sa_h = np.concatenate([zh, sin, np.zeros((n, rest))], axis=1)
    sb_h = np.concatenate([-sin, zh, np.zeros((n, rest))], axis=1)
    rep = LANES // HEAD_DIM
    return tuple(jnp.asarray(np.tile(t, (1, rep)), dtype=F32) for t in (cos_h, sa_h, sb_h))


def _lane_pad(v, n=LANES):
    return jnp.pad(v, (0, n - v.shape[0])).reshape(1, n)


def _to_cache(xt, batch, seq):
    return xt.reshape(1, batch, ATT_KV_HEADS, HEAD_DIM, seq).transpose(0, 1, 4, 2, 3)


def kernel(x_prompt, x_sample, cache_k, cache_v, state_conv, state_ssm, norm_w, w_in, conv_w,
           conv_b, dt_bias, a_log, d_skip, ssd_norm_w, q_norm_w, k_norm_w, w_out):
    bp, tp_, _ = x_prompt.shape
    bs, ts, _ = x_sample.shape
    depth = w_in.shape[0]
    assert depth == 1 and tp_ % SSD_CHUNK == 0 and ts <= SAMPLE_PAD and ts >= CONV_W - 1
    l = 0
    win = cache_k.shape[2]

    nw = norm_w[l].reshape(1, D_MODEL)
    rep = LANES // HEAD_DIM
    qnw = jnp.tile(q_norm_w[l], rep).reshape(1, LANES)
    knw = jnp.tile(k_norm_w[l], rep).reshape(1, LANES)
    cw = jnp.pad(conv_w[l], ((0, SUBLANES - CONV_W), (0, 0)))
    cb = conv_b[l].reshape(1, CONV_DIM)
    dtb = _lane_pad(dt_bias[l])
    alog = _lane_pad(a_log[l])
    dexp = jnp.repeat(d_skip[l], SSD_HEADDIM).reshape(1, SSD_INNER)
    snw = ssd_norm_w[l].reshape(1, SSD_INNER)

    pad = SAMPLE_PAD
    tabs_s = _rope_tables(PAST_LEN + np.arange(bs * pad) % pad)
    w_t, proj_s = _wconv_in(x_sample, nw, jnp.swapaxes(w_in[l], 0, 1))
    zs, xbc, dtr, q, kt, vt, gs, _ = _sections(proj_s, qnw, knw, tabs_s)
    cprev = jnp.pad(state_conv[l], ((0, 0), (SUBLANES - (CONV_W - 1), 0), (0, 0)))
    y_ssd, h_s = _ssd(xbc, dtr, zs, cprev, state_ssm[l].reshape(bs, SSD_INNER, SSD_STATE),
                      cw, cb, dtb, alog, dexp, bs, pad, ts)
    knt = kt.reshape(KV_DIM, bs, pad).transpose(1, 0, 2)
    vnt = vt.reshape(KV_DIM, bs, pad).transpose(1, 0, 2)
    ckt = cache_k[l].transpose(0, 2, 3, 1).reshape(bs, KV_DIM, win)
    cvt = cache_v[l].transpose(0, 2, 3, 1).reshape(bs, KV_DIM, win)
    y_att, kot, vot = _attn_sample(q, knt, vnt, gs, ckt, cvt, ts)
    w_o, y_s = _wconv_out(x_sample, y_ssd, y_att, snw, w_out[l])
    k_s = _to_cache(kot, bs, win)
    v_s = _to_cache(vot, bs, win)
    c_s = xbc.reshape(bs, pad, CONV_DIM)[:, ts - (CONV_W - 1):ts][None]
    h_s = h_s.reshape(1, bs, SSD_HEADS, SSD_HEADDIM, SSD_STATE)

    tm = TM_IN
    xp2 = x_prompt.reshape(bp * tp_, D_MODEL)
    tabs = _rope_tables(np.arange(tp_))
    zs, xbc, dtr, q, kt, vt, gs, ctail = _in_proj(xp2, nw, w_t, qnw, knw, tabs, tm, tp_)
    y_ssd, h_p = _ssd(xbc, dtr, zs, jnp.zeros((bp, SUBLANES, CONV_DIM), F32),
                      jnp.zeros((bp, SSD_INNER, SSD_STATE), F32),
                      cw, cb, dtb, alog, dexp, bp, SSD_CHUNK, SSD_CHUNK)
    y_att = _attn_prompt(q, kt, vt, gs, bp, tp_)
    y_p = _out_proj(xp2, y_ssd, y_att, snw, w_o, TM_OUT).reshape(bp, tp_, D_MODEL)
    keep = min(W_MAX, tp_)
    k_p = _to_cache(kt, bp, tp_)[:, :, tp_ - keep:]
    v_p = _to_cache(vt, bp, tp_)[:, :, tp_ - keep:]
    c_p = ctail[:, SUBLANES - (CONV_W - 1):][None]
    h_p = h_p.reshape(1, bp, SSD_HEADS, SSD_HEADDIM, SSD_STATE)

    return (y_p, y_s, k_p, v_p, c_p, h_p, k_s, v_s, c_s, h_s)
```

```python
import functools

import jax
import jax.numpy as jnp
import numpy as np
from jax import lax
from jax.experimental import pallas as pl
from jax.experimental.pallas import tpu as pltpu

F32 = jnp.float32
BF16 = jnp.bfloat16

D_MODEL = 2048
SSD_HEADS = 16
SSD_HEADDIM = 64
SSD_INNER = SSD_HEADS * SSD_HEADDIM
SSD_GROUPS = 2
SSD_STATE = 128
CONV_W = 4
CONV_DIM = SSD_INNER + 2 * SSD_GROUPS * SSD_STATE
SSD_CHUNK = 128
ATT_HEADS = 16
ATT_KV_HEADS = 4
HEAD_DIM = 64
ATT_GQ = ATT_HEADS // ATT_KV_HEADS
ATT_INNER = ATT_HEADS * HEAD_DIM
KV_DIM = ATT_KV_HEADS * HEAD_DIM
ROT_DIM = HEAD_DIM // 4
ROPE_THETA = 500000.0
DILATED_BRANCHES = ((128, 1), (512, 4), (2048, 16))
W_MAX = 2048
PAST_LEN = 16384
EPS = 1e-6

LANES = 128
SUBLANES = 8
Q_BLOCK = 256
K_SUPER = 256
Q_PER_SUPER = K_SUPER // Q_BLOCK
ONES_ROWS = 16
ACC_ROWS = HEAD_DIM + ONES_ROWS
CONV_COLS = 256
TM_IN = 512
TM_OUT = 512
SSD_SEQS = 4
SAMPLE_PAD = 16
NEG = -1e30
VMEM_LIMIT = 58 * 1024 * 1024

Z0 = 0
X0 = Z0 + SSD_INNER
DT0 = X0 + CONV_DIM
Q0 = DT0 + SSD_HEADS
K0 = Q0 + ATT_INNER
V0 = K0 + KV_DIM
G0 = V0 + KV_DIM
W_ROWS = G0 + ATT_INNER
W_CHUNK = 1024
HALF_INNER = SSD_INNER // SSD_GROUPS
WO_CHUNK = HALF_INNER
LOG2E = 1.4426950408889634
Q_SCALE = HEAD_DIM ** -0.5 * LOG2E


def _dot(a, b):
    return jnp.dot(a, b, preferred_element_type=F32)


def _dot_nt(a, b):
    return lax.dot_general(a, b, (((1,), (1,)), ((), ())), preferred_element_type=F32)


def _split3(x):
    hi = x.astype(BF16)
    r1 = x - hi.astype(F32)
    mid = r1.astype(BF16)
    lo = (r1 - mid.astype(F32)).astype(BF16)
    return hi, mid, lo


def _dot_exact_rhs(x, m):
    hi, mid, lo = _split3(x)
    return _dot(hi, m) + _dot(mid, m) + _dot(lo, m)


def _dot_wide_rhs(x, m):
    hi = x.astype(BF16)
    lo = (x - hi.astype(F32)).astype(BF16)
    return _dot(hi, m) + _dot(lo, m)


def _dot_exact_lhs(m, x):
    hi, mid, lo = _split3(x)
    return _dot(m, hi) + _dot(m, mid) + _dot(m, lo)


def _silu(x):
    hx = 0.5 * x
    return hx * jnp.tanh(hx) + hx


def _multiplicity(d):
    w = jnp.zeros(d.shape, F32)
    for window, dil in DILATED_BRANCHES:
        hit = (d >= 0) & (d <= window) & (lax.rem(d, dil) == 0)
        w = w + jnp.where(hit, 1.0, 0.0)
    return w


def _norm_rope(y, nw, cos, sa, sb, scale):
    lane = lax.broadcasted_iota(jnp.int32, (1, LANES), 1)
    first = lane < HEAD_DIM
    y2 = y * y
    s_lo = jnp.sum(jnp.where(first, y2, 0.0), axis=-1, keepdims=True)
    s_hi = jnp.sum(jnp.where(first, 0.0, y2), axis=-1, keepdims=True)
    ms = jnp.where(first, s_lo, s_hi) * (1.0 / HEAD_DIM)
    yn = y * lax.rsqrt(ms + EPS) * nw
    half = ROT_DIM // 2
    rot = yn * cos + pltpu.roll(yn, half, 1) * sa + pltpu.roll(yn, LANES - half, 1) * sb
    return rot * scale


def _causal_conv_silu(load, cw_ref, cb_ref, store):
    for c0 in range(0, CONV_DIM, CONV_COLS):
        cs = slice(c0, c0 + CONV_COLS)
        xe = load(cs)
        x2 = pltpu.roll(xe, 2, 0)
        even = cw_ref[3:4, cs] * xe + cw_ref[1:2, cs] * x2
        odd = cw_ref[2:3, cs] * xe + cw_ref[0:1, cs] * x2
        store(cs, _silu((even + pltpu.roll(odd, 1, 0))[SUBLANES:, :] + cb_ref[:, cs]))


def _pre_norm(x_ref, nw_ref):
    x = x_ref[...]
    ms = jnp.mean(x * x, axis=-1, keepdims=True)
    return (x * lax.rsqrt(ms + EPS) * nw_ref[...]).astype(BF16)


def _inproj_kernel(x_ref, nw_ref, w_ref, *refs):
    hn = _pre_norm(x_ref, nw_ref)

    def proj(a, b):
        return _dot_nt(hn, w_ref[a:b, :])

    _emit_sections(proj, x_ref.shape[0], *refs)


def _emit_sections(proj, tm, qnw_ref, knw_ref, cos_ref, sa_ref, sb_ref,
                   zs_ref, xbc_ref, dt_ref, q_ref, kt_ref, vt_ref, gs_ref, ctail_ref):
    cos, sa, sb = cos_ref[...], sa_ref[...], sb_ref[...]
    q = proj(Q0, K0)
    for c in range(ATT_INNER // LANES):
        sl = slice(c * LANES, (c + 1) * LANES)
        qn = _norm_rope(q[:, sl], qnw_ref[...], cos, sa, sb, Q_SCALE)
        if len(q_ref.shape) == 3:
            q_ref[0, sl, :] = qn.T.astype(BF16)
        else:
            q_ref[:, sl] = qn.astype(BF16)
    k = proj(K0, V0)
    kn = [_norm_rope(k[:, c * LANES:(c + 1) * LANES], knw_ref[...], cos, sa, sb, 1.0)
          for c in range(KV_DIM // LANES)]
    kt_ref[0] = jnp.concatenate(kn, axis=1).T

    xbc = proj(X0, DT0)
    xbc_ref[...] = xbc
    ctail_ref[0] = xbc[tm - SUBLANES:tm, :]
    vt_ref[0] = proj(V0, G0).T
    zs_ref[...] = _silu(proj(Z0, X0))
    gs = _silu(proj(G0, W_ROWS))
    if len(gs_ref.shape) == 3:
        gs_ref[0] = gs.T
    else:
        gs_ref[...] = gs
    dt_ref[...] = proj(DT0, DT0 + LANES)


def _in_proj(x2d, nw, w_t, qnw, knw, tables, tm, seq):
    rows = x2d.shape[0]
    const = lambda i: (0, 0)
    row = lambda i: (i, 0)
    sec_in, out_specs, out_shape = _section_specs(rows, tm, seq, tables[0].shape[0] // tm, True)
    return pl.pallas_call(
        _inproj_kernel,
        grid=(rows // tm,),
        in_specs=[
            pl.BlockSpec((tm, D_MODEL), row),
            pl.BlockSpec((1, D_MODEL), const),
            pl.BlockSpec((W_ROWS, D_MODEL), const, pipeline_mode=pl.Buffered(1)),
        ] + sec_in,
        out_specs=out_specs,
        out_shape=out_shape,
        compiler_params=pltpu.CompilerParams(
            dimension_semantics=("arbitrary",), vmem_limit_bytes=VMEM_LIMIT),
        name="in_proj",
    )(x2d, nw, w_t, qnw, knw, *tables)


def _section_specs(rows, tm, seq, period, q_transposed):
    per_seq = seq // tm
    const = lambda i: (0, 0)
    row = lambda i: (i, 0)
    tab = lambda i: (i % period, 0)
    tr = lambda i: (i // per_seq, 0, i % per_seq)
    slab = lambda i: (i // per_seq, 0, 0)
    in_specs = [pl.BlockSpec((1, LANES), const)] * 2 + [pl.BlockSpec((tm, LANES), tab)] * 3
    row_outs = {0: (SSD_INNER, F32), 1: (CONV_DIM, F32), 2: (LANES, F32), 3: (ATT_INNER, BF16),
                6: (ATT_INNER, F32)}
    out_specs, out_shape = [], []
    for idx in range(7):
        if idx in (3, 6) and q_transposed:
            dt = row_outs[idx][1]
            out_specs.append(pl.BlockSpec((1, ATT_INNER, tm), tr))
            out_shape.append(jax.ShapeDtypeStruct((rows // seq, ATT_INNER, seq), dt))
        elif idx in row_outs:
            n, dt = row_outs[idx]
            out_specs.append(pl.BlockSpec((tm, n), row))
            out_shape.append(jax.ShapeDtypeStruct((rows, n), dt))
        else:
            out_specs.append(pl.BlockSpec((1, KV_DIM, tm), tr))
            out_shape.append(jax.ShapeDtypeStruct((rows // seq, KV_DIM, seq), F32))
    out_specs.append(pl.BlockSpec((1, SUBLANES, CONV_DIM), slab))
    out_shape.append(jax.ShapeDtypeStruct((rows // seq, SUBLANES, CONV_DIM), F32))
    return in_specs, out_specs, out_shape


def _wconv_in_kernel(x_ref, nw_ref, w_ref, wb_ref, p_ref, xpad_scr, hn_scr):
    @pl.when(pl.program_id(0) == 0)
    def _():
        bs, ts, _ = x_ref.shape
        xpad_scr[...] = jnp.zeros(xpad_scr.shape, F32)
        for b in range(bs):
            xpad_scr[b * SAMPLE_PAD:b * SAMPLE_PAD + ts, :] = x_ref[b]
        hn_scr[...] = _pre_norm(xpad_scr, nw_ref)

    row = pl.program_id(0) * W_CHUNK + lax.broadcasted_iota(jnp.int32, (W_CHUNK, 1), 0)
    wb = jnp.where(row < W_ROWS, w_ref[...], 0.0).astype(BF16)
    wb_ref[...] = wb
    p_ref[0] = _dot_nt(hn_scr[...], wb)


def _wconv_in(x_s, nw, w_f32):
    rows = x_s.shape[0] * SAMPLE_PAD
    n_chunks = pl.cdiv(W_ROWS, W_CHUNK)
    const = lambda j: (0, 0)
    return pl.pallas_call(
        _wconv_in_kernel,
        grid=(n_chunks,),
        in_specs=[
            pl.BlockSpec(x_s.shape, lambda j: (0, 0, 0)),
            pl.BlockSpec((1, D_MODEL), const),
            pl.BlockSpec((W_CHUNK, D_MODEL), lambda j: (j, 0)),
        ],
        out_specs=[
            pl.BlockSpec((W_CHUNK, D_MODEL), lambda j: (j, 0)),
            pl.BlockSpec((1, rows, W_CHUNK), lambda j: (j, 0, 0)),
        ],
        out_shape=[
            jax.ShapeDtypeStruct((W_ROWS, D_MODEL), BF16),
            jax.ShapeDtypeStruct((n_chunks, rows, W_CHUNK), F32),
        ],
        scratch_shapes=[pltpu.VMEM((rows, D_MODEL), F32), pltpu.VMEM((rows, D_MODEL), BF16)],
        compiler_params=pltpu.CompilerParams(
            dimension_semantics=("arbitrary",), vmem_limit_bytes=VMEM_LIMIT),
        name="wconv_in",
    )(x_s, nw, w_f32)


def _sections_kernel(p_ref, *refs):
    *refs, tail_scr = refs
    pfull = jnp.concatenate([p_ref[j] for j in range(p_ref.shape[0])], axis=1)
    tail_scr[...] = pfull[:, Q0:W_ROWS]

    def proj(a, b):
        if a >= Q0:
            return tail_scr[:, a - Q0:b - Q0]
        return pfull[:, a:b]

    _emit_sections(proj, p_ref.shape[1], *refs)


def _sections(p, qnw, knw, tables):
    rows = p.shape[1]
    sec_in, out_specs, out_shape = _section_specs(rows, rows, rows, 1, False)
    return pl.pallas_call(
        _sections_kernel,
        grid=(1,),
        in_specs=[pl.BlockSpec(p.shape, lambda i: (0, 0, 0))] + sec_in,
        out_specs=out_specs,
        out_shape=out_shape,
        scratch_shapes=[pltpu.VMEM((rows, W_ROWS - Q0), F32)],
        compiler_params=pltpu.CompilerParams(
            dimension_semantics=("arbitrary",), vmem_limit_bytes=VMEM_LIMIT),
        name="sections",
    )(p, qnw, knw, *tables)


def _ssd_chain(s, xbc_ref, dtr_ref, z_ref, cw_ref, cb_ref, dtb_ref, alog_ref, dexp_ref, y_ref,
               xext, ht, xc_scr, lin, valid):
    L = SSD_CHUNK

    def pad_rows(v):
        if lin == L:
            return v
        return jnp.concatenate([v, jnp.zeros((L - lin, v.shape[1]), v.dtype)], axis=0)

    lane = lax.broadcasted_iota(jnp.int32, (1, LANES), 1)
    rowid = lax.broadcasted_iota(jnp.int32, (L, 1), 0)
    dt = jax.nn.softplus(pad_rows(dtr_ref[s]) + dtb_ref[...])
    dt = jnp.where((lane < SSD_HEADS) & (rowid < valid), dt, 0.0)
    dta = dt * (-jnp.exp(alog_ref[...]) * LOG2E)

    r2 = lax.broadcasted_iota(jnp.int32, (L, L), 0)
    c2 = lax.broadcasted_iota(jnp.int32, (L, L), 1)
    tri = r2 >= c2
    tri_b = jnp.where(tri, 1.0, 0.0).astype(BF16)
    cum = _dot_exact_lhs(tri_b, dta)
    cum_last = cum[L - 1:L, :]
    yield

    er = lax.broadcasted_iota(jnp.int32, (LANES, SSD_INNER), 0)
    ec = lax.broadcasted_iota(jnp.int32, (LANES, SSD_INNER), 1)
    expand = jnp.where(ec // SSD_HEADDIM == er, 1.0, 0.0).astype(BF16)
    ecum = _dot_wide_rhs(jnp.exp2(cum), expand)
    wexp = _dot_wide_rhs(jnp.exp2(cum_last - cum) * dt, expand)
    cd = _dot_exact_rhs(jnp.broadcast_to(jnp.exp2(cum_last), (SUBLANES, LANES)), expand)[0:1, :]
    cum_t = cum.T
    dt_t = dt.T
    yield

    if lin < L:
        xext[s, SUBLANES + lin:SUBLANES + L, :] = jnp.zeros((L - lin, CONV_DIM), F32)
    xext[s, SUBLANES:SUBLANES + lin, :] = xbc_ref[s]

    def load(cs):
        return xext[s, 0:SUBLANES + L, cs]

    def store(cs, v):
        xc_scr[s, :, cs] = v

    _causal_conv_silu(load, cw_ref, cb_ref, store)
    xext[s, 0:SUBLANES, :] = xext[s, L:L + SUBLANES, :]
    xc = xc_scr.at[s]
    xs = xc[:, :SSD_INNER]
    bm = xc[:, SSD_INNER:SSD_INNER + SSD_GROUPS * SSD_STATE]
    cm = xc[:, SSD_INNER + SSD_GROUPS * SSD_STATE:]
    bm_b = bm.astype(BF16)
    cm_b = cm.astype(BF16)
    xs_b = xs.astype(BF16)

    y_off = jnp.concatenate(
        [_dot(cm_b[:, g * SSD_STATE:(g + 1) * SSD_STATE], ht[s, g].astype(BF16))
         for g in range(SSD_GROUPS)], axis=1) * ecum
    cbs = [_dot_nt(cm_b[:, g * SSD_STATE:(g + 1) * SSD_STATE], bm_b[:, g * SSD_STATE:(g + 1) * SSD_STATE])
           for g in range(SSD_GROUPS)]
    yield

    lane_half = lane // HEAD_DIM
    heads_per_group = SSD_HEADS // SSD_GROUPS
    y_parts = []
    for g in range(SSD_GROUPS):
        for pr in range(heads_per_group // 2):
            col0 = g * HALF_INNER + pr * LANES
            xp = xs_b[:, col0:col0 + LANES]
            yp = jnp.zeros((L, LANES), F32)
            for e in range(2):
                hh = g * heads_per_group + pr * 2 + e
                decay = jnp.exp2(cum[:, hh:hh + 1] - cum_t[hh:hh + 1, :])
                sc = jnp.where(tri, cbs[g] * decay, 0.0) * dt_t[hh:hh + 1, :]
                xm = jnp.where(lane_half == e, xp, jnp.zeros_like(xp))
                yp = yp + _dot(sc.astype(BF16), xm)
            y_parts.append(yp)
            if pr % 2 == 1:
                yield
    y_diag = jnp.concatenate(y_parts, axis=1)

    xw = (xs * wexp).astype(BF16)
    for g in range(SSD_GROUPS):
        sl = slice(g * HALF_INNER, (g + 1) * HALF_INNER)
        bm_t = bm[:, g * SSD_STATE:(g + 1) * SSD_STATE].T.astype(BF16)
        ht[s, g] = ht[s, g] * cd[:, sl] + _dot(bm_t, xw[:, sl])
    yield

    y = (y_diag + y_off + dexp_ref[...] * xs) * pad_rows(z_ref[s])
    y_ref[s] = y[:lin]


def _ssd_kernel(xbc_ref, dtr_ref, z_ref, cprev_ref, h0_ref, cw_ref, cb_ref, dtb_ref, alog_ref, dexp_ref,
                y_ref, hout_ref, xext, ht, xc_scr, *, lin, valid):
    c = pl.program_id(1)

    def for_each_state(fn):
        def step(j, carry):
            g = j % SSD_GROUPS
            fn(j // SSD_GROUPS, g, pl.multiple_of(g * HALF_INNER, HALF_INNER))
            return carry

        lax.fori_loop(0, SSD_SEQS * SSD_GROUPS, step, 0, unroll=2)

    @pl.when(c == 0)
    def _():
        for s in range(SSD_SEQS):
            xext[s, 0:SUBLANES, :] = cprev_ref[s]

        def load_state(s, g, r0):
            ht[s, g] = h0_ref[s, pl.ds(r0, HALF_INNER), :].T

        for_each_state(load_state)

    chains = [_ssd_chain(s, xbc_ref, dtr_ref, z_ref, cw_ref, cb_ref, dtb_ref, alog_ref, dexp_ref, y_ref,
                         xext, ht, xc_scr, lin, valid) for s in range(SSD_SEQS)]
    while chains:
        alive = []
        for ch in chains:
            try:
                next(ch)
                alive.append(ch)
            except StopIteration:
                pass
        chains = alive

    @pl.when(c == pl.num_programs(1) - 1)
    def _():
        def store_state(s, g, r0):
            hout_ref[s, pl.ds(r0, HALF_INNER), :] = ht[s, g].T

        for_each_state(store_state)


def _ssd(xbc, dtr, z, cprev, h0, cw, cb, dtb, alog, dexp, batch, lin, valid):
    rows = xbc.shape[0]
    seq = rows // batch
    nc = seq // lin
    g = SSD_SEQS
    assert batch % g == 0
    blk = lambda b, c: (b, c, 0)
    per_b = lambda b, c: (b, 0, 0)
    const = lambda b, c: (0, 0)
    y, hout = pl.pallas_call(
        functools.partial(_ssd_kernel, lin=lin, valid=valid),
        grid=(batch // g, nc),
        in_specs=[
            pl.BlockSpec((g, lin, CONV_DIM), blk),
            pl.BlockSpec((g, lin, LANES), blk),
            pl.BlockSpec((g, lin, SSD_INNER), blk),
            pl.BlockSpec((g, SUBLANES, CONV_DIM), per_b),
            pl.BlockSpec((g, SSD_INNER, SSD_STATE), per_b),
            pl.BlockSpec((SUBLANES, CONV_DIM), const),
            pl.BlockSpec((1, CONV_DIM), const),
            pl.BlockSpec((1, LANES), const),
            pl.BlockSpec((1, LANES), const),
            pl.BlockSpec((1, SSD_INNER), const),
        ],
        out_specs=[
            pl.BlockSpec((g, lin, SSD_INNER), blk),
            pl.BlockSpec((g, SSD_INNER, SSD_STATE), per_b),
        ],
        out_shape=[
            jax.ShapeDtypeStruct((batch, seq, SSD_INNER), F32),
            jax.ShapeDtypeStruct((batch, SSD_INNER, SSD_STATE), F32),
        ],
        scratch_shapes=[
            pltpu.VMEM((g, SUBLANES + SSD_CHUNK + SUBLANES, CONV_DIM), F32),
            pltpu.VMEM((g, SSD_GROUPS, SSD_STATE, HALF_INNER), F32),
            pltpu.VMEM((g, SSD_CHUNK, CONV_DIM), F32),
        ],
        compiler_params=pltpu.CompilerParams(
            dimension_semantics=("arbitrary", "arbitrary"), vmem_limit_bytes=VMEM_LIMIT),
        name="ssd",
    )(xbc.reshape(batch, seq, CONV_DIM), dtr.reshape(batch, seq, LANES),
      z.reshape(batch, seq, SSD_INNER), cprev, h0, cw, cb, dtb, alog, dexp)
    return y.reshape(rows, SSD_INNER), hout


def _attn_prompt_kernel(q_ref, kt_ref, vt_ref, g_ref, o_ref,
                        bias_scr, kh_scr, va_scr, qt_scr, m_scr, acc_scr, s_scr, s2_scr, *, n_tab, n_sb_max):
    b = pl.program_id(0)
    i = pl.program_id(1)
    seq = kt_ref.shape[2]
    pair_w = 2 * Q_BLOCK
    pairs_per_kv = ATT_GQ // 2

    @pl.when((b == 0) & (i == 0))
    def _():
        r = lax.broadcasted_iota(jnp.int32, (K_SUPER, Q_BLOCK), 0)
        c = lax.broadcasted_iota(jnp.int32, (K_SUPER, Q_BLOCK), 1)
        def table(tb, carry):
            w = _multiplicity(tb * Q_BLOCK + c - r)
            bias_scr[tb] = jnp.where(w > 0.0, jnp.log2(jnp.maximum(w, 1.0)), NEG)
            return carry

        lax.fori_loop(0, n_tab, table, 0)
        va_scr[:, HEAD_DIM:, :] = jnp.ones((ATT_KV_HEADS, ONES_ROWS, seq), BF16)

    @pl.when(i == 0)
    def _():
        for kvh in range(ATT_KV_HEADS):
            hs = slice(kvh * HEAD_DIM, (kvh + 1) * HEAD_DIM)
            va_scr[kvh, 0:HEAD_DIM, :] = vt_ref[0, hs, :].astype(BF16)
        for cidx in range(seq // K_SUPER):
            rows = slice(cidx * K_SUPER, (cidx + 1) * K_SUPER)
            kc = kt_ref[0, :, rows].T
            for kvh in range(ATT_KV_HEADS):
                hs = slice(kvh * HEAD_DIM, (kvh + 1) * HEAD_DIM)
                kh_scr[kvh, rows, :] = kc[:, hs].astype(BF16)

    for kvh in range(ATT_KV_HEADS):
        qt_scr[kvh] = jnp.concatenate(
            [q_ref[0, (kvh * ATT_GQ + g) * HEAD_DIM:(kvh * ATT_GQ + g + 1) * HEAD_DIM, :]
             for g in range(ATT_GQ)], axis=1)
    m_scr[...] = jnp.full(m_scr.shape, NEG, F32)
    acc_scr[...] = jnp.zeros(acc_scr.shape, F32)
    par = i % Q_PER_SUPER
    j_last = i // Q_PER_SUPER
    n_sb = jnp.minimum(j_last + 1, n_sb_max)

    def key_start(dl):
        return pl.multiple_of((j_last - dl) * K_SUPER, K_SUPER)

    def scores(dl, dst):
        for kvh in range(ATT_KV_HEADS):
            dst[kvh] = _dot(kh_scr[kvh, pl.ds(key_start(dl), K_SUPER), :], qt_scr[kvh])

    def softmax_pv(dl, src):
        start = key_start(dl)
        bias = bias_scr[par + Q_PER_SUPER * dl]
        bias2 = jnp.concatenate([bias, bias], axis=1)
        for kvh in range(ATT_KV_HEADS):
            vtb = va_scr[kvh, :, pl.ds(start, K_SUPER)]
            for pr in range(pairs_per_kv):
                u = kvh * pairs_per_kv + pr
                s = src[kvh, :, pr * pair_w:(pr + 1) * pair_w] + bias2
                m_old = m_scr[u, 0:1, :]
                m_new = jnp.maximum(m_old, jnp.max(s, axis=0, keepdims=True))
                p = jnp.exp2(s - m_new).astype(BF16)
                acc_scr[u] = jnp.exp2(m_old - m_new) * acc_scr[u] + _dot(vtb, p)
                m_scr[u, 0:1, :] = m_new

    def diagonal_block():
        half = K_SUPER // 2
        start = key_start(0)
        bias = bias_scr[0]
        for kvh in range(ATT_KV_HEADS):
            s_scr[kvh, 0:half, :] = _dot(kh_scr[kvh, pl.ds(start, half), :], qt_scr[kvh])
        for kvh in range(ATT_KV_HEADS):
            q_late = jnp.concatenate(
                [qt_scr[kvh, :, g * Q_BLOCK + half:(g + 1) * Q_BLOCK] for g in range(ATT_GQ)], axis=1)
            s2_scr[kvh, 0:half, 0:ATT_GQ * half] = _dot(
                kh_scr[kvh, pl.ds(start + half, half), :], q_late)
        bias_a = jnp.concatenate([bias[0:half, :]] * 2, axis=1)
        for kvh in range(ATT_KV_HEADS):
            vtb = va_scr[kvh, :, pl.ds(start, half)]
            for pr in range(pairs_per_kv):
                u = kvh * pairs_per_kv + pr
                s = s_scr[kvh, 0:half, pr * pair_w:(pr + 1) * pair_w] + bias_a
                m_old = m_scr[u, 0:1, :]
                m_new = jnp.maximum(m_old, jnp.max(s, axis=0, keepdims=True))
                p = jnp.exp2(s - m_new).astype(BF16)
                acc_scr[u] = jnp.exp2(m_old - m_new) * acc_scr[u] + _dot(vtb, p)
                m_scr[u, 0:1, :] = m_new
        bias_b = jnp.concatenate([bias[half:, half:]] * 2, axis=1)
        late = [slice(e * Q_BLOCK + half, (e + 1) * Q_BLOCK) for e in range(2)]
        for kvh in range(ATT_KV_HEADS):
            vtb = va_scr[kvh, :, pl.ds(start + half, half)]
            for pr in range(pairs_per_kv):
                u = kvh * pairs_per_kv + pr
                s = s2_scr[kvh, 0:half, pr * Q_BLOCK:(pr + 1) * Q_BLOCK] + bias_b
                m_old = jnp.concatenate([m_scr[u, 0:1, sl] for sl in late], axis=1)
                m_new = jnp.maximum(m_old, jnp.max(s, axis=0, keepdims=True))
                p = jnp.exp2(s - m_new).astype(BF16)
                alpha = jnp.exp2(m_old - m_new)
                pv = _dot(vtb, p)
                for e, sl in enumerate(late):
                    es = slice(e * half, (e + 1) * half)
                    acc_scr[u, :, sl] = alpha[:, es] * acc_scr[u, :, sl] + pv[:, es]
                    m_scr[u, 0:1, sl] = m_new[:, es]

    diagonal_block()
    n_rest = n_sb - 1

    def body(t, carry):
        scores(1 + 2 * t, s_scr)
        scores(2 + 2 * t, s2_scr)
        softmax_pv(1 + 2 * t, s_scr)
        softmax_pv(2 + 2 * t, s2_scr)
        return carry

    lax.fori_loop(0, n_rest // 2, body, 0)

    @pl.when(n_rest % 2 == 1)
    def _():
        scores(n_sb - 1, s_scr)
        softmax_pv(n_sb - 1, s_scr)

    for u in range(ATT_HEADS // 2):
        acc = acc_scr[u]
        o = acc[0:HEAD_DIM] * (1.0 / acc[HEAD_DIM:HEAD_DIM + 1])
        for e in range(2):
            hs = slice((2 * u + e) * HEAD_DIM, (2 * u + e + 1) * HEAD_DIM)
            o_ref[0, hs, :] = (o[:, e * Q_BLOCK:(e + 1) * Q_BLOCK] * g_ref[0, hs, :]).astype(BF16)


def _attn_prompt(qt, kt, vt, gate_t, batch, seq):
    nqb = seq // Q_BLOCK
    assert seq % K_SUPER == 0 and Q_BLOCK == K_SUPER
    n_sb_max = min(seq // K_SUPER, W_MAX // K_SUPER + 1)
    n_tab = Q_PER_SUPER * n_sb_max
    qblk = lambda b, i: (b, 0, i)
    per_b = lambda b, i: (b, 0, 0)
    return pl.pallas_call(
        functools.partial(_attn_prompt_kernel, n_tab=n_tab, n_sb_max=n_sb_max),
        grid=(batch, nqb),
        in_specs=[
            pl.BlockSpec((1, ATT_INNER, Q_BLOCK), qblk),
            pl.BlockSpec((1, KV_DIM, seq), per_b),
            pl.BlockSpec((1, KV_DIM, seq), per_b),
            pl.BlockSpec((1, ATT_INNER, Q_BLOCK), qblk),
        ],
        out_specs=pl.BlockSpec((1, ATT_INNER, Q_BLOCK), qblk),
        out_shape=jax.ShapeDtypeStruct((batch, ATT_INNER, seq), BF16),
        scratch_shapes=[
            pltpu.VMEM((n_tab, K_SUPER, Q_BLOCK), F32),
            pltpu.VMEM((ATT_KV_HEADS, seq, HEAD_DIM), BF16),
            pltpu.VMEM((ATT_KV_HEADS, ACC_ROWS, seq), BF16),
            pltpu.VMEM((ATT_KV_HEADS, HEAD_DIM, ATT_GQ * Q_BLOCK), BF16),
            pltpu.VMEM((ATT_HEADS // 2, SUBLANES, 2 * Q_BLOCK), F32),
            pltpu.VMEM((ATT_HEADS // 2, ACC_ROWS, 2 * Q_BLOCK), F32),
            pltpu.VMEM((ATT_KV_HEADS, K_SUPER, ATT_GQ * Q_BLOCK), F32),
            pltpu.VMEM((ATT_KV_HEADS, K_SUPER, ATT_GQ * Q_BLOCK), F32),
        ],
        compiler_params=pltpu.CompilerParams(
            dimension_semantics=("arbitrary", "arbitrary"), vmem_limit_bytes=VMEM_LIMIT),
        name="attn_prompt",
    )(qt, kt, vt, gate_t)


def _attn_sample_kernel(q_ref, knt_ref, vnt_ref, g_ref, ckt_ref, cvt_ref, o_ref, kot_ref, vot_ref, *, ts):
    win = ckt_ref.shape[2]
    tp = SAMPLE_PAD
    rows = ATT_GQ * tp
    t_c = lax.broadcasted_iota(jnp.int32, (rows, win), 0) % tp
    j_c = lax.broadcasted_iota(jnp.int32, (rows, win), 1)
    w_c = _multiplicity(win + t_c - j_c)
    t_n = lax.broadcasted_iota(jnp.int32, (rows, tp), 0) % tp
    j_n = lax.broadcasted_iota(jnp.int32, (rows, tp), 1)
    w_n = jnp.where(j_n < ts, _multiplicity(t_n - j_n), 0.0)
    for kvh in range(ATT_KV_HEADS):
        c0 = kvh * ATT_GQ * HEAD_DIM
        qh = jnp.concatenate(
            [q_ref[:, c0 + g * HEAD_DIM:c0 + (g + 1) * HEAD_DIM] for g in range(ATT_GQ)], axis=0)
        hs = slice(kvh * HEAD_DIM, (kvh + 1) * HEAD_DIM)
        s_c = jnp.where(w_c > 0.0, _dot(qh, ckt_ref[0, hs, :].astype(BF16)), NEG)
        s_n = jnp.where(w_n > 0.0, _dot(qh, knt_ref[0, hs, :].astype(BF16)), NEG)
        m = jnp.maximum(jnp.max(s_c, axis=-1, keepdims=True), jnp.max(s_n, axis=-1, keepdims=True))
        p_c = jnp.exp2(s_c - m) * w_c
        p_n = jnp.exp2(s_n - m) * w_n
        den = jnp.sum(p_c, axis=-1, keepdims=True) + jnp.sum(p_n, axis=-1, keepdims=True)
        num = (_dot_nt(p_c.astype(BF16), cvt_ref[0, hs, :].astype(BF16))
               + _dot_nt(p_n.astype(BF16), vnt_ref[0, hs, :].astype(BF16)))
        o = num / den
        o = jnp.concatenate([o[g * tp:(g + 1) * tp] for g in range(ATT_GQ)], axis=1)
        csl = slice(c0, c0 + ATT_GQ * HEAD_DIM)
        o_ref[:, csl] = (o * g_ref[:, csl]).astype(BF16)
    kot_ref[0, :, 0:win - ts] = ckt_ref[0, :, ts:win]
    kot_ref[0, :, win - ts:win] = knt_ref[0, :, 0:ts]
    vot_ref[0, :, 0:win - ts] = cvt_ref[0, :, ts:win]
    vot_ref[0, :, win - ts:win] = vnt_ref[0, :, 0:ts]


def _attn_sample(q, knt, vnt, gate, ckt, cvt, ts):
    batch, _, win = ckt.shape
    tp = SAMPLE_PAD
    blk = lambda b: (b, 0)
    per_b = lambda b: (b, 0, 0)
    return pl.pallas_call(
        functools.partial(_attn_sample_kernel, ts=ts),
        grid=(batch,),
        in_specs=[
            pl.BlockSpec((tp, ATT_INNER), blk),
            pl.BlockSpec((1, KV_DIM, tp), per_b),
            pl.BlockSpec((1, KV_DIM, tp), per_b),
            pl.BlockSpec((tp, ATT_INNER), blk),
            pl.BlockSpec((1, KV_DIM, win), per_b),
            pl.BlockSpec((1, KV_DIM, win), per_b),
        ],
        out_specs=[
            pl.BlockSpec((tp, ATT_INNER), blk),
            pl.BlockSpec((1, KV_DIM, win), per_b),
            pl.BlockSpec((1, KV_DIM, win), per_b),
        ],
        out_shape=[
            jax.ShapeDtypeStruct((batch * tp, ATT_INNER), BF16),
            jax.ShapeDtypeStruct((batch, KV_DIM, win), F32),
            jax.ShapeDtypeStruct((batch, KV_DIM, win), F32),
        ],
        compiler_params=pltpu.CompilerParams(
            dimension_semantics=("arbitrary",), vmem_limit_bytes=VMEM_LIMIT),
        name="attn_sample",
    )(q, knt, vnt, gate, ckt, cvt)


def _group_norm(yg, nw):
    ms = jnp.mean(yg * yg, axis=-1, keepdims=True)
    return (yg * lax.rsqrt(ms + EPS) * nw).astype(BF16)


def _outproj_kernel(x_ref, ys_ref, yat_ref, nw_ref, w_ref, o_ref):
    acc = x_ref[...]
    for g in range(SSD_GROUPS):
        gs = slice(g * HALF_INNER, (g + 1) * HALF_INNER)
        acc = acc + _dot(_group_norm(ys_ref[:, gs], nw_ref[:, gs]), w_ref[gs, :])
    ya = yat_ref[0].astype(F32).T.astype(BF16)
    o_ref[...] = acc + _dot(ya, w_ref[SSD_INNER:, :])


def _out_proj(x2d, y_ssd, y_att_t, nw, w_o, tm):
    rows = x2d.shape[0]
    per_seq = y_att_t.shape[2] // tm
    row = lambda i: (i, 0)
    const = lambda i: (0, 0)
    return pl.pallas_call(
        _outproj_kernel,
        grid=(rows // tm,),
        in_specs=[
            pl.BlockSpec((tm, D_MODEL), row),
            pl.BlockSpec((tm, SSD_INNER), row),
            pl.BlockSpec((1, ATT_INNER, tm), lambda i: (i // per_seq, 0, i % per_seq)),
            pl.BlockSpec((1, SSD_INNER), const),
            pl.BlockSpec((SSD_INNER + ATT_INNER, D_MODEL), const, pipeline_mode=pl.Buffered(1)),
        ],
        out_specs=pl.BlockSpec((tm, D_MODEL), row),
        out_shape=jax.ShapeDtypeStruct((rows, D_MODEL), F32),
        compiler_params=pltpu.CompilerParams(
            dimension_semantics=("arbitrary",), vmem_limit_bytes=VMEM_LIMIT),
        name="out_proj",
    )(x2d, y_ssd, y_att_t, nw, w_o)


def _wconv_out_kernel(x_ref, ys_ref, ya_ref, nw_ref, w_ref, wb_ref, o_ref):
    j = pl.program_id(0)
    wb = w_ref[...].astype(BF16)
    wb_ref[...] = wb
    y = jnp.where(j < SSD_GROUPS, _group_norm(ys_ref[...], nw_ref[...]), ya_ref[...])
    part = _dot(y, wb)
    bs, ts, _ = x_ref.shape

    @pl.when(j == 0)
    def _():
        for b in range(bs):
            o_ref[b] = x_ref[b] + part[b * SAMPLE_PAD:b * SAMPLE_PAD + ts]

    @pl.when(j > 0)
    def _():
        for b in range(bs):
            o_ref[b] += part[b * SAMPLE_PAD:b * SAMPLE_PAD + ts]


def _wconv_out(x_s, y_ssd, y_att, nw, w_f32):
    rows = x_s.shape[0] * SAMPLE_PAD
    n_ssd = SSD_GROUPS
    n_chunks = (SSD_INNER + ATT_INNER) // WO_CHUNK
    const = lambda j: (0, 0)
    ssd_chunk = lambda j: (0, jnp.minimum(j, n_ssd - 1))
    return pl.pallas_call(
        _wconv_out_kernel,
        grid=(n_chunks,),
        in_specs=[
            pl.BlockSpec(x_s.shape, lambda j: (0, 0, 0)),
            pl.BlockSpec((rows, WO_CHUNK), ssd_chunk),
            pl.BlockSpec((rows, WO_CHUNK), lambda j: (0, jnp.maximum(j - n_ssd, 0))),
            pl.BlockSpec((1, WO_CHUNK), ssd_chunk),
            pl.BlockSpec((WO_CHUNK, D_MODEL), lambda j: (j, 0)),
        ],
        out_specs=[
            pl.BlockSpec((WO_CHUNK, D_MODEL), lambda j: (j, 0)),
            pl.BlockSpec(x_s.shape, lambda j: (0, 0, 0)),
        ],
        out_shape=[
            jax.ShapeDtypeStruct((SSD_INNER + ATT_INNER, D_MODEL), BF16),
            jax.ShapeDtypeStruct(x_s.shape, F32),
        ],
        compiler_params=pltpu.CompilerParams(
            dimension_semantics=("arbitrary",), vmem_limit_bytes=VMEM_LIMIT),
        name="wconv_out",
    )(x_s, y_ssd, y_att, nw, w_f32)


def _rope_tables(pos):
    n = pos.shape[0]
    half = ROT_DIM // 2
    inv = ROPE_THETA ** (-np.arange(0, ROT_DIM, 2, dtype=np.float64) / ROT_DIM)
    ang = pos.astype(np.float64)[:, None] * inv[None, :]
    cos, sin = np.cos(ang), np.sin(ang)
    rest = HEAD_DIM - ROT_DIM
    zh = np.zeros((n, half))
    cos_h = np.concatenate([cos, cos, np.ones((n, rest))], axis=1)
    sa_h = np.concatenate([zh, sin, np.zeros((n, rest))], axis=1)
    sb_h = np.concatenate([-sin, zh, np.zeros((n, rest))], axis=1)
    rep = LANES // HEAD_DIM
    return tuple(jnp.asarray(np.tile(t, (1, rep)), dtype=F32) for t in (cos_h, sa_h, sb_h))


def _lane_pad(v, n=LANES):
    return jnp.pad(v, (0, n - v.shape[0])).reshape(1, n)


def _to_cache(xt, batch, seq):
    return xt.reshape(1, batch, ATT_KV_HEADS, HEAD_DIM, seq).transpose(0, 1, 4, 2, 3)


def kernel(x_prompt, x_sample, cache_k, cache_v, state_conv, state_ssm, norm_w, w_in, conv_w,
           conv_b, dt_bias, a_log, d_skip, ssd_norm_w, q_norm_w, k_norm_w, w_out):
    bp, tp_, _ = x_prompt.shape
    bs, ts, _ = x_sample.shape
    depth = w_in.shape[0]
    assert depth == 1 and tp_ % SSD_CHUNK == 0 and ts <= SAMPLE_PAD and ts >= CONV_W - 1
    l = 0
    win = cache_k.shape[2]

    nw = norm_w[l].reshape(1, D_MODEL)
    rep = LANES // HEAD_DIM
    qnw = jnp.tile(q_norm_w[l], rep).reshape(1, LANES)
    knw = jnp.tile(k_norm_w[l], rep).reshape(1, LANES)
    cw = jnp.pad(conv_w[l], ((0, SUBLANES - CONV_W), (0, 0)))
    cb = conv_b[l].reshape(1, CONV_DIM)
    dtb = _lane_pad(dt_bias[l])
    alog = _lane_pad(a_log[l])
    dexp = jnp.repeat(d_skip[l], SSD_HEADDIM).reshape(1, SSD_INNER)
    snw = ssd_norm_w[l].reshape(1, SSD_INNER)

    pad = SAMPLE_PAD
    tabs_s = _rope_tables(PAST_LEN + np.arange(bs * pad) % pad)
    w_t, proj_s = _wconv_in(x_sample, nw, jnp.swapaxes(w_in[l], 0, 1))
    zs, xbc, dtr, q, kt, vt, gs, _ = _sections(proj_s, qnw, knw, tabs_s)
    cprev = jnp.pad(state_conv[l], ((0, 0), (SUBLANES - (CONV_W - 1), 0), (0, 0)))
    y_ssd, h_s = _ssd(xbc, dtr, zs, cprev, state_ssm[l].reshape(bs, SSD_INNER, SSD_STATE),
                      cw, cb, dtb, alog, dexp, bs, pad, ts)
    knt = kt.reshape(KV_DIM, bs, pad).transpose(1, 0, 2)
    vnt = vt.reshape(KV_DIM, bs, pad).transpose(1, 0, 2)
    ckt = cache_k[l].transpose(0, 2, 3, 1).reshape(bs, KV_DIM, win)
    cvt = cache_v[l].transpose(0, 2, 3, 1).reshape(bs, KV_DIM, win)
    y_att, kot, vot = _attn_sample(q, knt, vnt, gs, ckt, cvt, ts)
    w_o, y_s = _wconv_out(x_sample, y_ssd, y_att, snw, w_out[l])
    k_s = _to_cache(kot, bs, win)
    v_s = _to_cache(vot, bs, win)
    c_s = xbc.reshape(bs, pad, CONV_DIM)[:, ts - (CONV_W - 1):ts][None]
    h_s = h_s.reshape(1, bs, SSD_HEADS, SSD_HEADDIM, SSD_STATE)

    tm = TM_IN
    xp2 = x_prompt.reshape(bp * tp_, D_MODEL)
    tabs = _rope_tables(np.arange(tp_))
    zs, xbc, dtr, q, kt, vt, gs, ctail = _in_proj(xp2, nw, w_t, qnw, knw, tabs, tm, tp_)
    y_ssd, h_p = _ssd(xbc, dtr, zs, jnp.zeros((bp, SUBLANES, CONV_DIM), F32),
                      jnp.zeros((bp, SSD_INNER, SSD_STATE), F32),
                      cw, cb, dtb, alog, dexp, bp, SSD_CHUNK, SSD_CHUNK)
    y_att = _attn_prompt(q, kt, vt, gs, bp, tp_)
    y_p = _out_proj(xp2, y_ssd, y_att, snw, w_o, TM_OUT).reshape(bp, tp_, D_MODEL)
    keep = min(W_MAX, tp_)
    k_p = _to_cache(kt, bp, tp_)[:, :, tp_ - keep:]
    v_p = _to_cache(vt, bp, tp_)[:, :, tp_ - keep:]
    c_p = ctail[:, SUBLANES - (CONV_W - 1):][None]
    h_p = h_p.reshape(1, bp, SSD_HEADS, SSD_HEADDIM, SSD_STATE)

    return (y_p, y_s, k_p, v_p, c_p, h_p, k_s, v_s, c_s, h_s)
```

```python
import functools

import jax
import jax.numpy as jnp
import numpy as np
from jax import lax
from jax.experimental import pallas as pl
from jax.experimental.pallas import tpu as pltpu

F32 = jnp.float32
BF16 = jnp.bfloat16

D_MODEL = 2048
SSD_HEADS = 16
SSD_HEADDIM = 64
SSD_INNER = SSD_HEADS * SSD_HEADDIM
SSD_GROUPS = 2
SSD_STATE = 128
CONV_W = 4
CONV_DIM = SSD_INNER + 2 * SSD_GROUPS * SSD_STATE
SSD_CHUNK = 128
ATT_HEADS = 16
ATT_KV_HEADS = 4
HEAD_DIM = 64
ATT_GQ = ATT_HEADS // ATT_KV_HEADS
ATT_INNER = ATT_HEADS * HEAD_DIM
KV_DIM = ATT_KV_HEADS * HEAD_DIM
ROT_DIM = HEAD_DIM // 4
ROPE_THETA = 500000.0
DILATED_BRANCHES = ((128, 1), (512, 4), (2048, 16))
W_MAX = 2048
PAST_LEN = 16384
EPS = 1e-6

LANES = 128
SUBLANES = 8
Q_BLOCK = 256
K_SUPER = 256
Q_PER_SUPER = K_SUPER // Q_BLOCK
ONES_ROWS = 16
ACC_ROWS = HEAD_DIM + ONES_ROWS
CONV_COLS = 256
TM_IN = 512
TM_OUT = 512
SSD_SEQS = 4
SAMPLE_PAD = 16
NEG = -1e30
VMEM_LIMIT = 58 * 1024 * 1024

Z0 = 0
X0 = Z0 + SSD_INNER
DT0 = X0 + CONV_DIM
Q0 = DT0 + SSD_HEADS
K0 = Q0 + ATT_INNER
V0 = K0 + KV_DIM
G0 = V0 + KV_DIM
W_ROWS = G0 + ATT_INNER
W_CHUNK = 1024
HALF_INNER = SSD_INNER // SSD_GROUPS
WO_CHUNK = HALF_INNER
LOG2E = 1.4426950408889634
Q_SCALE = HEAD_DIM ** -0.5 * LOG2E


def _dot(a, b):
    return jnp.dot(a, b, preferred_element_type=F32)


def _dot_nt(a, b):
    return lax.dot_general(a, b, (((1,), (1,)), ((), ())), preferred_element_type=F32)


def _split3(x):
    hi = x.astype(BF16)
    r1 = x - hi.astype(F32)
    mid = r1.astype(BF16)
    lo = (r1 - mid.astype(F32)).astype(BF16)
    return hi, mid, lo


def _dot_exact_rhs(x, m):
    hi, mid, lo = _split3(x)
    return _dot(hi, m) + _dot(mid, m) + _dot(lo, m)


def _dot_wide_rhs(x, m):
    hi = x.astype(BF16)
    lo = (x - hi.astype(F32)).astype(BF16)
    return _dot(hi, m) + _dot(lo, m)


def _dot_exact_lhs(m, x):
    hi, mid, lo = _split3(x)
    return _dot(m, hi) + _dot(m, mid) + _dot(m, lo)


def _silu(x):
    hx = 0.5 * x
    return hx * jnp.tanh(hx) + hx


def _multiplicity(d):
    w = jnp.zeros(d.shape, F32)
    for window, dil in DILATED_BRANCHES:
        hit = (d >= 0) & (d <= window) & (lax.rem(d, dil) == 0)
        w = w + jnp.where(hit, 1.0, 0.0)
    return w


def _norm_rope(y, nw, cos, sa, sb, scale):
    lane = lax.broadcasted_iota(jnp.int32, (1, LANES), 1)
    first = lane < HEAD_DIM
    y2 = y * y
    s_lo = jnp.sum(jnp.where(first, y2, 0.0), axis=-1, keepdims=True)
    s_hi = jnp.sum(jnp.where(first, 0.0, y2), axis=-1, keepdims=True)
    ms = jnp.where(first, s_lo, s_hi) * (1.0 / HEAD_DIM)
    yn = y * lax.rsqrt(ms + EPS) * nw
    half = ROT_DIM // 2
    rot = yn * cos + pltpu.roll(yn, half, 1) * sa + pltpu.roll(yn, LANES - half, 1) * sb
    return rot * scale


def _causal_conv_silu(load, cw_ref, cb_ref, store):
    for c0 in range(0, CONV_DIM, CONV_COLS):
        cs = slice(c0, c0 + CONV_COLS)
        xe = load(cs)
        x2 = pltpu.roll(xe, 2, 0)
        even = cw_ref[3:4, cs] * xe + cw_ref[1:2, cs] * x2
        odd = cw_ref[2:3, cs] * xe + cw_ref[0:1, cs] * x2
        store(cs, _silu((even + pltpu.roll(odd, 1, 0))[SUBLANES:, :] + cb_ref[:, cs]))


def _pre_norm(x_ref, nw_ref):
    x = x_ref[...]
    ms = jnp.mean(x * x, axis=-1, keepdims=True)
    return (x * lax.rsqrt(ms + EPS) * nw_ref[...]).astype(BF16)


def _inproj_kernel(x_ref, nw_ref, w_ref, *refs):
    hn = _pre_norm(x_ref, nw_ref)

    def proj(a, b):
        return _dot_nt(hn, w_ref[a:b, :])

    _emit_sections(proj, x_ref.shape[0], *refs)


def _emit_sections(proj, tm, qnw_ref, knw_ref, cos_ref, sa_ref, sb_ref,
                   zs_ref, xbc_ref, dt_ref, q_ref, kt_ref, vt_ref, gs_ref, ctail_ref):
    cos, sa, sb = cos_ref[...], sa_ref[...], sb_ref[...]
    q = proj(Q0, K0)
    for c in range(ATT_INNER // LANES):
        sl = slice(c * LANES, (c + 1) * LANES)
        qn = _norm_rope(q[:, sl], qnw_ref[...], cos, sa, sb, Q_SCALE)
        if len(q_ref.shape) == 3:
            q_ref[0, sl, :] = qn.T.astype(BF16)
        else:
            q_ref[:, sl] = qn.astype(BF16)
    k = proj(K0, V0)
    kn = [_norm_rope(k[:, c * LANES:(c + 1) * LANES], knw_ref[...], cos, sa, sb, 1.0)
          for c in range(KV_DIM // LANES)]
    kt_ref[0] = jnp.concatenate(kn, axis=1).T

    xbc = proj(X0, DT0)
    xbc_ref[...] = xbc
    ctail_ref[0] = xbc[tm - SUBLANES:tm, :]
    vt_ref[0] = proj(V0, G0).T
    zs_ref[...] = _silu(proj(Z0, X0))
    gs = _silu(proj(G0, W_ROWS))
    if len(gs_ref.shape) == 3:
        gs_ref[0] = gs.T
    else:
        gs_ref[...] = gs
    dt_ref[...] = proj(DT0, DT0 + LANES)


def _in_proj(x2d, nw, w_t, qnw, knw, tables, tm, seq):
    rows = x2d.shape[0]
    const = lambda i: (0, 0)
    row = lambda i: (i, 0)
    sec_in, out_specs, out_shape = _section_specs(rows, tm, seq, tables[0].shape[0] // tm, True)
    return pl.pallas_call(
        _inproj_kernel,
        grid=(rows // tm,),
        in_specs=[
            pl.BlockSpec((tm, D_MODEL), row),
            pl.BlockSpec((1, D_MODEL), const),
            pl.BlockSpec((W_ROWS, D_MODEL), const, pipeline_mode=pl.Buffered(1)),
        ] + sec_in,
        out_specs=out_specs,
        out_shape=out_shape,
        compiler_params=pltpu.CompilerParams(
            dimension_semantics=("arbitrary",), vmem_limit_bytes=VMEM_LIMIT),
        name="in_proj",
    )(x2d, nw, w_t, qnw, knw, *tables)


def _section_specs(rows, tm, seq, period, q_transposed):
    per_seq = seq // tm
    const = lambda i: (0, 0)
    row = lambda i: (i, 0)
    tab = lambda i: (i % period, 0)
    tr = lambda i: (i // per_seq, 0, i % per_seq)
    slab = lambda i: (i // per_seq, 0, 0)
    in_specs = [pl.BlockSpec((1, LANES), const)] * 2 + [pl.BlockSpec((tm, LANES), tab)] * 3
    row_outs = {0: (SSD_INNER, F32), 1: (CONV_DIM, F32), 2: (LANES, F32), 3: (ATT_INNER, BF16),
                6: (ATT_INNER, F32)}
    out_specs, out_shape = [], []
    for idx in range(7):
        if idx in (3, 6) and q_transposed:
            dt = row_outs[idx][1]
            out_specs.append(pl.BlockSpec((1, ATT_INNER, tm), tr))
            out_shape.append(jax.ShapeDtypeStruct((rows // seq, ATT_INNER, seq), dt))
        elif idx in row_outs:
            n, dt = row_outs[idx]
            out_specs.append(pl.BlockSpec((tm, n), row))
            out_shape.append(jax.ShapeDtypeStruct((rows, n), dt))
        else:
            out_specs.append(pl.BlockSpec((1, KV_DIM, tm), tr))
            out_shape.append(jax.ShapeDtypeStruct((rows // seq, KV_DIM, seq), F32))
    out_specs.append(pl.BlockSpec((1, SUBLANES, CONV_DIM), slab))
    out_shape.append(jax.ShapeDtypeStruct((rows // seq, SUBLANES, CONV_DIM), F32))
    return in_specs, out_specs, out_shape


def _wconv_in_kernel(x_ref, nw_ref, w_ref, wb_ref, p_ref, xpad_scr, hn_scr):
    @pl.when(pl.program_id(0) == 0)
    def _():
        bs, ts, _ = x_ref.shape
        xpad_scr[...] = jnp.zeros(xpad_scr.shape, F32)
        for b in range(bs):
            xpad_scr[b * SAMPLE_PAD:b * SAMPLE_PAD + ts, :] = x_ref[b]
        hn_scr[...] = _pre_norm(xpad_scr, nw_ref)

    row = pl.program_id(0) * W_CHUNK + lax.broadcasted_iota(jnp.int32, (W_CHUNK, 1), 0)
    wb = jnp.where(row < W_ROWS, w_ref[...], 0.0).astype(BF16)
    wb_ref[...] = wb
    p_ref[0] = _dot_nt(hn_scr[...], wb)


def _wconv_in(x_s, nw, w_f32):
    rows = x_s.shape[0] * SAMPLE_PAD
    n_chunks = pl.cdiv(W_ROWS, W_CHUNK)
    const = lambda j: (0, 0)
    return pl.pallas_call(
        _wconv_in_kernel,
        grid=(n_chunks,),
        in_specs=[
            pl.BlockSpec(x_s.shape, lambda j: (0, 0, 0)),
            pl.BlockSpec((1, D_MODEL), const),
            pl.BlockSpec((W_CHUNK, D_MODEL), lambda j: (j, 0)),
        ],
        out_specs=[
            pl.BlockSpec((W_CHUNK, D_MODEL), lambda j: (j, 0)),
            pl.BlockSpec((1, rows, W_CHUNK), lambda j: (j, 0, 0)),
        ],
        out_shape=[
            jax.ShapeDtypeStruct((W_ROWS, D_MODEL), BF16),
            jax.ShapeDtypeStruct((n_chunks, rows, W_CHUNK), F32),
        ],
        scratch_shapes=[pltpu.VMEM((rows, D_MODEL), F32), pltpu.VMEM((rows, D_MODEL), BF16)],
        compiler_params=pltpu.CompilerParams(
            dimension_semantics=("arbitrary",), vmem_limit_bytes=VMEM_LIMIT),
        name="wconv_in",
    )(x_s, nw, w_f32)


def _sections_kernel(p_ref, *refs):
    *refs, tail_scr = refs
    pfull = jnp.concatenate([p_ref[j] for j in range(p_ref.shape[0])], axis=1)
    tail_scr[...] = pfull[:, Q0:W_ROWS]

    def proj(a, b):
        if a >= Q0:
            return tail_scr[:, a - Q0:b - Q0]
        return pfull[:, a:b]

    _emit_sections(proj, p_ref.shape[1], *refs)


def _sections(p, qnw, knw, tables):
    rows = p.shape[1]
    sec_in, out_specs, out_shape = _section_specs(rows, rows, rows, 1, False)
    return pl.pallas_call(
        _sections_kernel,
        grid=(1,),
        in_specs=[pl.BlockSpec(p.shape, lambda i: (0, 0, 0))] + sec_in,
        out_specs=out_specs,
        out_shape=out_shape,
        scratch_shapes=[pltpu.VMEM((rows, W_ROWS - Q0), F32)],
        compiler_params=pltpu.CompilerParams(
            dimension_semantics=("arbitrary",), vmem_limit_bytes=VMEM_LIMIT),
        name="sections",
    )(p, qnw, knw, *tables)


def _ssd_chain(s, xbc_ref, dtr_ref, z_ref, cw_ref, cb_ref, dtb_ref, alog_ref, dexp_ref, y_ref,
               xext, ht, xc_scr, lin, valid):
    L = lin

    lane = lax.broadcasted_iota(jnp.int32, (1, LANES), 1)
    rowid = lax.broadcasted_iota(jnp.int32, (L, 1), 0)
    dt = jax.nn.softplus(dtr_ref[s] + dtb_ref[...])
    dt = jnp.where((lane < SSD_HEADS) & (rowid < valid), dt, 0.0)
    dta = dt * (-jnp.exp(alog_ref[...]) * LOG2E)

    r2 = lax.broadcasted_iota(jnp.int32, (L, L), 0)
    c2 = lax.broadcasted_iota(jnp.int32, (L, L), 1)
    tri = r2 >= c2
    tri_b = jnp.where(tri, 1.0, 0.0).astype(BF16)
    cum = _dot_exact_lhs(tri_b, dta)
    cum_last = cum[L - 1:L, :]
    yield

    er = lax.broadcasted_iota(jnp.int32, (LANES, SSD_INNER), 0)
    ec = lax.broadcasted_iota(jnp.int32, (LANES, SSD_INNER), 1)
    expand = jnp.where(ec // SSD_HEADDIM == er, 1.0, 0.0).astype(BF16)
    ecum = _dot_wide_rhs(jnp.exp2(cum), expand)
    wexp = _dot_wide_rhs(jnp.exp2(cum_last - cum) * dt, expand)
    cd = _dot_exact_rhs(jnp.broadcast_to(jnp.exp2(cum_last), (SUBLANES, LANES)), expand)[0:1, :]
    cum_t = cum.T
    dt_t = dt.T
    yield

    xext[s, SUBLANES:SUBLANES + L, :] = xbc_ref[s]

    def load(cs):
        return xext[s, 0:SUBLANES + L, cs]

    def store(cs, v):
        xc_scr[s, :, cs] = v

    _causal_conv_silu(load, cw_ref, cb_ref, store)
    xext[s, 0:SUBLANES, :] = xext[s, L:L + SUBLANES, :]
    xc = xc_scr.at[s]
    xs = xc[:, :SSD_INNER]
    bm = xc[:, SSD_INNER:SSD_INNER + SSD_GROUPS * SSD_STATE]
    cm = xc[:, SSD_INNER + SSD_GROUPS * SSD_STATE:]
    bm_b = bm.astype(BF16)
    cm_b = cm.astype(BF16)
    xs_b = xs.astype(BF16)

    y_off = jnp.concatenate(
        [_dot(cm_b[:, g * SSD_STATE:(g + 1) * SSD_STATE], ht[s, g].astype(BF16))
         for g in range(SSD_GROUPS)], axis=1) * ecum
    cbs = [_dot_nt(cm_b[:, g * SSD_STATE:(g + 1) * SSD_STATE], bm_b[:, g * SSD_STATE:(g + 1) * SSD_STATE])
           for g in range(SSD_GROUPS)]
    yield

    lane_half = lane // HEAD_DIM
    heads_per_group = SSD_HEADS // SSD_GROUPS
    y_parts = []
    for g in range(SSD_GROUPS):
        for pr in range(heads_per_group // 2):
            col0 = g * HALF_INNER + pr * LANES
            xp = xs_b[:, col0:col0 + LANES]
            yp = jnp.zeros((L, LANES), F32)
            for e in range(2):
                hh = g * heads_per_group + pr * 2 + e
                decay = jnp.exp2(cum[:, hh:hh + 1] - cum_t[hh:hh + 1, :])
                sc = jnp.where(tri, cbs[g] * decay, 0.0) * dt_t[hh:hh + 1, :]
                xm = jnp.where(lane_half == e, xp, jnp.zeros_like(xp))
                yp = yp + _dot(sc.astype(BF16), xm)
            y_parts.append(yp)
            if pr % 2 == 1:
                yield
    y_diag = jnp.concatenate(y_parts, axis=1)

    xw = (xs * wexp).astype(BF16)
    for g in range(SSD_GROUPS):
        sl = slice(g * HALF_INNER, (g + 1) * HALF_INNER)
        bm_t = bm[:, g * SSD_STATE:(g + 1) * SSD_STATE].T.astype(BF16)
        ht[s, g] = ht[s, g] * cd[:, sl] + _dot(bm_t, xw[:, sl])
    yield

    y_ref[s] = (y_diag + y_off + dexp_ref[...] * xs) * z_ref[s]


def _ssd_kernel(xbc_ref, dtr_ref, z_ref, cprev_ref, h0_ref, cw_ref, cb_ref, dtb_ref, alog_ref, dexp_ref,
                y_ref, hout_ref, xext, ht, xc_scr, *, lin, valid):
    c = pl.program_id(1)
    n_seq = xbc_ref.shape[0]

    def for_each_state(fn):
        def step(j, carry):
            g = j % SSD_GROUPS
            fn(j // SSD_GROUPS, g, pl.multiple_of(g * HALF_INNER, HALF_INNER))
            return carry

        lax.fori_loop(0, n_seq * SSD_GROUPS, step, 0, unroll=2)

    @pl.when(c == 0)
    def _():
        for s in range(n_seq):
            xext[s, 0:SUBLANES, :] = cprev_ref[s]

        def load_state(s, g, r0):
            ht[s, g] = h0_ref[s, pl.ds(r0, HALF_INNER), :].T

        for_each_state(load_state)

    chains = [_ssd_chain(s, xbc_ref, dtr_ref, z_ref, cw_ref, cb_ref, dtb_ref, alog_ref, dexp_ref, y_ref,
                         xext, ht, xc_scr, lin, valid) for s in range(n_seq)]
    while chains:
        alive = []
        for ch in chains:
            try:
                next(ch)
                alive.append(ch)
            except StopIteration:
                pass
        chains = alive

    @pl.when(c == pl.num_programs(1) - 1)
    def _():
        def store_state(s, g, r0):
            hout_ref[s, pl.ds(r0, HALF_INNER), :] = ht[s, g].T

        for_each_state(store_state)


def _ssd(xbc, dtr, z, cprev, h0, cw, cb, dtb, alog, dexp, batch, lin, valid, g):
    rows = xbc.shape[0]
    seq = rows // batch
    nc = seq // lin
    assert batch % g == 0
    blk = lambda b, c: (b, c, 0)
    per_b = lambda b, c: (b, 0, 0)
    const = lambda b, c: (0, 0)
    y, hout = pl.pallas_call(
        functools.partial(_ssd_kernel, lin=lin, valid=valid),
        grid=(batch // g, nc),
        in_specs=[
            pl.BlockSpec((g, lin, CONV_DIM), blk),
            pl.BlockSpec((g, lin, LANES), blk),
            pl.BlockSpec((g, lin, SSD_INNER), blk),
            pl.BlockSpec((g, SUBLANES, CONV_DIM), per_b),
            pl.BlockSpec((g, SSD_INNER, SSD_STATE), per_b),
            pl.BlockSpec((SUBLANES, CONV_DIM), const),
            pl.BlockSpec((1, CONV_DIM), const),
            pl.BlockSpec((1, LANES), const),
            pl.BlockSpec((1, LANES), const),
            pl.BlockSpec((1, SSD_INNER), const),
        ],
        out_specs=[
            pl.BlockSpec((g, lin, SSD_INNER), blk),
            pl.BlockSpec((g, SSD_INNER, SSD_STATE), per_b),
        ],
        out_shape=[
            jax.ShapeDtypeStruct((batch, seq, SSD_INNER), F32),
            jax.ShapeDtypeStruct((batch, SSD_INNER, SSD_STATE), F32),
        ],
        scratch_shapes=[
            pltpu.VMEM((g, SUBLANES + lin + SUBLANES, CONV_DIM), F32),
            pltpu.VMEM((g, SSD_GROUPS, SSD_STATE, HALF_INNER), F32),
            pltpu.VMEM((g, lin, CONV_DIM), F32),
        ],
        compiler_params=pltpu.CompilerParams(
            dimension_semantics=("arbitrary", "arbitrary"), vmem_limit_bytes=VMEM_LIMIT),
        name="ssd",
    )(xbc.reshape(batch, seq, CONV_DIM), dtr.reshape(batch, seq, LANES),
      z.reshape(batch, seq, SSD_INNER), cprev, h0, cw, cb, dtb, alog, dexp)
    return y.reshape(rows, SSD_INNER), hout


def _attn_prompt_kernel(q_ref, kt_ref, vt_ref, g_ref, o_ref,
                        bias_scr, kh_scr, va_scr, qt_scr, m_scr, acc_scr, s_scr, s2_scr, *, n_tab, n_sb_max):
    b = pl.program_id(0)
    i = pl.program_id(1)
    seq = kt_ref.shape[2]
    pair_w = 2 * Q_BLOCK
    pairs_per_kv = ATT_GQ // 2

    @pl.when((b == 0) & (i == 0))
    def _():
        r = lax.broadcasted_iota(jnp.int32, (K_SUPER, Q_BLOCK), 0)
        c = lax.broadcasted_iota(jnp.int32, (K_SUPER, Q_BLOCK), 1)
        def table(tb, carry):
            w = _multiplicity(tb * Q_BLOCK + c - r)
            bias_scr[tb] = jnp.where(w > 0.0, jnp.log2(jnp.maximum(w, 1.0)), NEG)
            return carry

        lax.fori_loop(0, n_tab, table, 0)
        va_scr[:, HEAD_DIM:, :] = jnp.ones((ATT_KV_HEADS, ONES_ROWS, seq), BF16)

    @pl.when(i == 0)
    def _():
        for kvh in range(ATT_KV_HEADS):
            hs = slice(kvh * HEAD_DIM, (kvh + 1) * HEAD_DIM)
            va_scr[kvh, 0:HEAD_DIM, :] = vt_ref[0, hs, :].astype(BF16)
        for cidx in range(seq // K_SUPER):
            rows = slice(cidx * K_SUPER, (cidx + 1) * K_SUPER)
            kc = kt_ref[0, :, rows].T
            for kvh in range(ATT_KV_HEADS):
                hs = slice(kvh * HEAD_DIM, (kvh + 1) * HEAD_DIM)
                kh_scr[kvh, rows, :] = kc[:, hs].astype(BF16)

    for kvh in range(ATT_KV_HEADS):
        qt_scr[kvh] = jnp.concatenate(
            [q_ref[0, (kvh * ATT_GQ + g) * HEAD_DIM:(kvh * ATT_GQ + g + 1) * HEAD_DIM, :]
             for g in range(ATT_GQ)], axis=1)
    m_scr[...] = jnp.full(m_scr.shape, NEG, F32)
    acc_scr[...] = jnp.zeros(acc_scr.shape, F32)
    par = i % Q_PER_SUPER
    j_last = i // Q_PER_SUPER
    n_sb = jnp.minimum(j_last + 1, n_sb_max)

    def key_start(dl):
        return pl.multiple_of((j_last - dl) * K_SUPER, K_SUPER)

    def scores(dl, dst):
        for kvh in range(ATT_KV_HEADS):
            dst[kvh] = _dot(kh_scr[kvh, pl.ds(key_start(dl), K_SUPER), :], qt_scr[kvh])

    def softmax_pv(dl, src):
        start = key_start(dl)
        bias = bias_scr[par + Q_PER_SUPER * dl]
        bias2 = jnp.concatenate([bias, bias], axis=1)
        for kvh in range(ATT_KV_HEADS):
            vtb = va_scr[kvh, :, pl.ds(start, K_SUPER)]
            for pr in range(pairs_per_kv):
                u = kvh * pairs_per_kv + pr
                s = src[kvh, :, pr * pair_w:(pr + 1) * pair_w] + bias2
                m_old = m_scr[u, 0:1, :]
                m_new = jnp.maximum(m_old, jnp.max(s, axis=0, keepdims=True))
                p = jnp.exp2(s - m_new).astype(BF16)
                acc_scr[u] = jnp.exp2(m_old - m_new) * acc_scr[u] + _dot(vtb, p)
                m_scr[u, 0:1, :] = m_new

    def diagonal_block():
        half = K_SUPER // 2
        start = key_start(0)
        bias = bias_scr[0]
        for kvh in range(ATT_KV_HEADS):
            s_scr[kvh, 0:half, :] = _dot(kh_scr[kvh, pl.ds(start, half), :], qt_scr[kvh])
        for kvh in range(ATT_KV_HEADS):
            q_late = jnp.concatenate(
                [qt_scr[kvh, :, g * Q_BLOCK + half:(g + 1) * Q_BLOCK] for g in range(ATT_GQ)], axis=1)
            s2_scr[kvh, 0:half, 0:ATT_GQ * half] = _dot(
                kh_scr[kvh, pl.ds(start + half, half), :], q_late)
        bias_a = jnp.concatenate([bias[0:half, :]] * 2, axis=1)
        for kvh in range(ATT_KV_HEADS):
            vtb = va_scr[kvh, :, pl.ds(start, half)]
            for pr in range(pairs_per_kv):
                u = kvh * pairs_per_kv + pr
                s = s_scr[kvh, 0:half, pr * pair_w:(pr + 1) * pair_w] + bias_a
                m_old = m_scr[u, 0:1, :]
                m_new = jnp.maximum(m_old, jnp.max(s, axis=0, keepdims=True))
                p = jnp.exp2(s - m_new).astype(BF16)
                acc_scr[u] = jnp.exp2(m_old - m_new) * acc_scr[u] + _dot(vtb, p)
                m_scr[u, 0:1, :] = m_new
        bias_b = jnp.concatenate([bias[half:, half:]] * 2, axis=1)
        late = [slice(e * Q_BLOCK + half, (e + 1) * Q_BLOCK) for e in range(2)]
        for kvh in range(ATT_KV_HEADS):
            vtb = va_scr[kvh, :, pl.ds(start + half, half)]
            for pr in range(pairs_per_kv):
                u = kvh * pairs_per_kv + pr
                s = s2_scr[kvh, 0:half, pr * Q_BLOCK:(pr + 1) * Q_BLOCK] + bias_b
                m_old = jnp.concatenate([m_scr[u, 0:1, sl] for sl in late], axis=1)
                m_new = jnp.maximum(m_old, jnp.max(s, axis=0, keepdims=True))
                p = jnp.exp2(s - m_new).astype(BF16)
                alpha = jnp.exp2(m_old - m_new)
                pv = _dot(vtb, p)
                for e, sl in enumerate(late):
                    es = slice(e * half, (e + 1) * half)
                    acc_scr[u, :, sl] = alpha[:, es] * acc_scr[u, :, sl] + pv[:, es]
                    m_scr[u, 0:1, sl] = m_new[:, es]

    diagonal_block()
    n_rest = n_sb - 1

    def body(t, carry):
        scores(1 + 2 * t, s_scr)
        scores(2 + 2 * t, s2_scr)
        softmax_pv(1 + 2 * t, s_scr)
        softmax_pv(2 + 2 * t, s2_scr)
        return carry

    lax.fori_loop(0, n_rest // 2, body, 0)

    @pl.when(n_rest % 2 == 1)
    def _():
        scores(n_sb - 1, s_scr)
        softmax_pv(n_sb - 1, s_scr)

    for u in range(ATT_HEADS // 2):
        acc = acc_scr[u]
        o = acc[0:HEAD_DIM] * (1.0 / acc[HEAD_DIM:HEAD_DIM + 1])
        for e in range(2):
            hs = slice((2 * u + e) * HEAD_DIM, (2 * u + e + 1) * HEAD_DIM)
            o_ref[0, hs, :] = (o[:, e * Q_BLOCK:(e + 1) * Q_BLOCK] * g_ref[0, hs, :]).astype(BF16)


def _attn_prompt(qt, kt, vt, gate_t, batch, seq):
    nqb = seq // Q_BLOCK
    assert seq % K_SUPER == 0 and Q_BLOCK == K_SUPER
    n_sb_max = min(seq // K_SUPER, W_MAX // K_SUPER + 1)
    n_tab = Q_PER_SUPER * n_sb_max
    qblk = lambda b, i: (b, 0, i)
    per_b = lambda b, i: (b, 0, 0)
    return pl.pallas_call(
        functools.partial(_attn_prompt_kernel, n_tab=n_tab, n_sb_max=n_sb_max),
        grid=(batch, nqb),
        in_specs=[
            pl.BlockSpec((1, ATT_INNER, Q_BLOCK), qblk),
            pl.BlockSpec((1, KV_DIM, seq), per_b),
            pl.BlockSpec((1, KV_DIM, seq), per_b),
            pl.BlockSpec((1, ATT_INNER, Q_BLOCK), qblk),
        ],
        out_specs=pl.BlockSpec((1, ATT_INNER, Q_BLOCK), qblk),
        out_shape=jax.ShapeDtypeStruct((batch, ATT_INNER, seq), BF16),
        scratch_shapes=[
            pltpu.VMEM((n_tab, K_SUPER, Q_BLOCK), F32),
            pltpu.VMEM((ATT_KV_HEADS, seq, HEAD_DIM), BF16),
            pltpu.VMEM((ATT_KV_HEADS, ACC_ROWS, seq), BF16),
            pltpu.VMEM((ATT_KV_HEADS, HEAD_DIM, ATT_GQ * Q_BLOCK), BF16),
            pltpu.VMEM((ATT_HEADS // 2, SUBLANES, 2 * Q_BLOCK), F32),
            pltpu.VMEM((ATT_HEADS // 2, ACC_ROWS, 2 * Q_BLOCK), F32),
            pltpu.VMEM((ATT_KV_HEADS, K_SUPER, ATT_GQ * Q_BLOCK), F32),
            pltpu.VMEM((ATT_KV_HEADS, K_SUPER, ATT_GQ * Q_BLOCK), F32),
        ],
        compiler_params=pltpu.CompilerParams(
            dimension_semantics=("arbitrary", "arbitrary"), vmem_limit_bytes=VMEM_LIMIT),
        name="attn_prompt",
    )(qt, kt, vt, gate_t)


def _attn_sample_kernel(q_ref, knt_ref, vnt_ref, g_ref, ckt_ref, cvt_ref, o_ref, kot_ref, vot_ref, *, ts):
    win = ckt_ref.shape[2]
    tp = SAMPLE_PAD
    rows = ATT_GQ * tp
    t_c = lax.broadcasted_iota(jnp.int32, (rows, win), 0) % tp
    j_c = lax.broadcasted_iota(jnp.int32, (rows, win), 1)
    w_c = _multiplicity(win + t_c - j_c)
    t_n = lax.broadcasted_iota(jnp.int32, (rows, tp), 0) % tp
    j_n = lax.broadcasted_iota(jnp.int32, (rows, tp), 1)
    w_n = jnp.where(j_n < ts, _multiplicity(t_n - j_n), 0.0)
    for kvh in range(ATT_KV_HEADS):
        c0 = kvh * ATT_GQ * HEAD_DIM
        qh = jnp.concatenate(
            [q_ref[:, c0 + g * HEAD_DIM:c0 + (g + 1) * HEAD_DIM] for g in range(ATT_GQ)], axis=0)
        hs = slice(kvh * HEAD_DIM, (kvh + 1) * HEAD_DIM)
        s_c = jnp.where(w_c > 0.0, _dot(qh, ckt_ref[0, hs, :].astype(BF16)), NEG)
        s_n = jnp.where(w_n > 0.0, _dot(qh, knt_ref[0, hs, :].astype(BF16)), NEG)
        m = jnp.maximum(jnp.max(s_c, axis=-1, keepdims=True), jnp.max(s_n, axis=-1, keepdims=True))
        p_c = jnp.exp2(s_c - m) * w_c
        p_n = jnp.exp2(s_n - m) * w_n
        den = jnp.sum(p_c, axis=-1, keepdims=True) + jnp.sum(p_n, axis=-1, keepdims=True)
        num = (_dot_nt(p_c.astype(BF16), cvt_ref[0, hs, :].astype(BF16))
               + _dot_nt(p_n.astype(BF16), vnt_ref[0, hs, :].astype(BF16)))
        o = num / den
        o = jnp.concatenate([o[g * tp:(g + 1) * tp] for g in range(ATT_GQ)], axis=1)
        csl = slice(c0, c0 + ATT_GQ * HEAD_DIM)
        o_ref[:, csl] = (o * g_ref[:, csl]).astype(BF16)
    kot_ref[0, :, 0:win - ts] = ckt_ref[0, :, ts:win]
    kot_ref[0, :, win - ts:win] = knt_ref[0, :, 0:ts]
    vot_ref[0, :, 0:win - ts] = cvt_ref[0, :, ts:win]
    vot_ref[0, :, win - ts:win] = vnt_ref[0, :, 0:ts]


def _attn_sample(q, knt, vnt, gate, ckt, cvt, ts):
    batch, _, win = ckt.shape
    tp = SAMPLE_PAD
    blk = lambda b: (b, 0)
    per_b = lambda b: (b, 0, 0)
    return pl.pallas_call(
        functools.partial(_attn_sample_kernel, ts=ts),
        grid=(batch,),
        in_specs=[
            pl.BlockSpec((tp, ATT_INNER), blk),
            pl.BlockSpec((1, KV_DIM, tp), per_b),
            pl.BlockSpec((1, KV_DIM, tp), per_b),
            pl.BlockSpec((tp, ATT_INNER), blk),
            pl.BlockSpec((1, KV_DIM, win), per_b),
            pl.BlockSpec((1, KV_DIM, win), per_b),
        ],
        out_specs=[
            pl.BlockSpec((tp, ATT_INNER), blk),
            pl.BlockSpec((1, KV_DIM, win), per_b),
            pl.BlockSpec((1, KV_DIM, win), per_b),
        ],
        out_shape=[
            jax.ShapeDtypeStruct((batch * tp, ATT_INNER), BF16),
            jax.ShapeDtypeStruct((batch, KV_DIM, win), F32),
            jax.ShapeDtypeStruct((batch, KV_DIM, win), F32),
        ],
        compiler_params=pltpu.CompilerParams(
            dimension_semantics=("arbitrary",), vmem_limit_bytes=VMEM_LIMIT),
        name="attn_sample",
    )(q, knt, vnt, gate, ckt, cvt)


def _group_norm(yg, nw):
    ms = jnp.mean(yg * yg, axis=-1, keepdims=True)
    return (yg * lax.rsqrt(ms + EPS) * nw).astype(BF16)


def _outproj_kernel(x_ref, ys_ref, yat_ref, nw_ref, w_ref, o_ref):
    acc = x_ref[...]
    for g in range(SSD_GROUPS):
        gs = slice(g * HALF_INNER, (g + 1) * HALF_INNER)
        acc = acc + _dot(_group_norm(ys_ref[:, gs], nw_ref[:, gs]), w_ref[gs, :])
    ya = yat_ref[0].astype(F32).T.astype(BF16)
    o_ref[...] = acc + _dot(ya, w_ref[SSD_INNER:, :])


def _out_proj(x2d, y_ssd, y_att_t, nw, w_o, tm):
    rows = x2d.shape[0]
    per_seq = y_att_t.shape[2] // tm
    row = lambda i: (i, 0)
    const = lambda i: (0, 0)
    return pl.pallas_call(
        _outproj_kernel,
        grid=(rows // tm,),
        in_specs=[
            pl.BlockSpec((tm, D_MODEL), row),
            pl.BlockSpec((tm, SSD_INNER), row),
            pl.BlockSpec((1, ATT_INNER, tm), lambda i: (i // per_seq, 0, i % per_seq)),
            pl.BlockSpec((1, SSD_INNER), const),
            pl.BlockSpec((SSD_INNER + ATT_INNER, D_MODEL), const, pipeline_mode=pl.Buffered(1)),
        ],
        out_specs=pl.BlockSpec((tm, D_MODEL), row),
        out_shape=jax.ShapeDtypeStruct((rows, D_MODEL), F32),
        compiler_params=pltpu.CompilerParams(
            dimension_semantics=("arbitrary",), vmem_limit_bytes=VMEM_LIMIT),
        name="out_proj",
    )(x2d, y_ssd, y_att_t, nw, w_o)


def _wconv_out_kernel(x_ref, ys_ref, ya_ref, nw_ref, w_ref, wb_ref, o_ref):
    j = pl.program_id(0)
    wb = w_ref[...].astype(BF16)
    wb_ref[...] = wb
    y = jnp.where(j < SSD_GROUPS, _group_norm(ys_ref[...], nw_ref[...]), ya_ref[...])
    part = _dot(y, wb)
    bs, ts, _ = x_ref.shape

    @pl.when(j == 0)
    def _():
        for b in range(bs):
            o_ref[b] = x_ref[b] + part[b * SAMPLE_PAD:b * SAMPLE_PAD + ts]

    @pl.when(j > 0)
    def _():
        for b in range(bs):
            o_ref[b] += part[b * SAMPLE_PAD:b * SAMPLE_PAD + ts]


def _wconv_out(x_s, y_ssd, y_att, nw, w_f32):
    rows = x_s.shape[0] * SAMPLE_PAD
    n_ssd = SSD_GROUPS
    n_chunks = (SSD_INNER + ATT_INNER) // WO_CHUNK
    const = lambda j: (0, 0)
    ssd_chunk = lambda j: (0, jnp.minimum(j, n_ssd - 1))
    return pl.pallas_call(
        _wconv_out_kernel,
        grid=(n_chunks,),
        in_specs=[
            pl.BlockSpec(x_s.shape, lambda j: (0, 0, 0)),
            pl.BlockSpec((rows, WO_CHUNK), ssd_chunk),
            pl.BlockSpec((rows, WO_CHUNK), lambda j: (0, jnp.maximum(j - n_ssd, 0))),
            pl.BlockSpec((1, WO_CHUNK), ssd_chunk),
            pl.BlockSpec((WO_CHUNK, D_MODEL), lambda j: (j, 0)),
        ],
        out_specs=[
            pl.BlockSpec((WO_CHUNK, D_MODEL), lambda j: (j, 0)),
            pl.BlockSpec(x_s.shape, lambda j: (0, 0, 0)),
        ],
        out_shape=[
            jax.ShapeDtypeStruct((SSD_INNER + ATT_INNER, D_MODEL), BF16),
            jax.ShapeDtypeStruct(x_s.shape, F32),
        ],
        compiler_params=pltpu.CompilerParams(
            dimension_semantics=("arbitrary",), vmem_limit_bytes=VMEM_LIMIT),
        name="wconv_out",
    )(x_s, y_ssd, y_att, nw, w_f32)


def _rope_tables(pos):
    n = pos.shape[0]
    half = ROT_DIM // 2
    inv = ROPE_THETA ** (-np.arange(0, ROT_DIM, 2, dtype=np.float64) / ROT_DIM)
    ang = pos.astype(np.float64)[:, None] * inv[None, :]
    cos, sin = np.cos(ang), np.sin(ang)
    rest = HEAD_DIM - ROT_DIM
    zh = np.zeros((n, half))
    cos_h = np.concatenate([cos, cos, np.ones((n, rest))], axis=1)
    sa_h = np.concatenate([zh, sin, np.zeros((n, rest))], axis=1)
    sb_h = np.concatenate([-sin, zh, np.zeros((n, rest))], axis=1)
    rep = LANES // HEAD_DIM
    return tuple(jnp.asarray(np.tile(t, (1, rep)), dtype=F32) for t in (cos_h, sa_h, sb_h))


def _lane_pad(v, n=LANES):
    return jnp.pad(v, (0, n - v.shape[0])).reshape(1, n)


def _to_cache(xt, batch, seq):
    return xt.reshape(1, batch, ATT_KV_HEADS, HEAD_DIM, seq).transpose(0, 1, 4, 2, 3)


def kernel(x_prompt, x_sample, cache_k, cache_v, state_conv, state_ssm, norm_w, w_in, conv_w,
           conv_b, dt_bias, a_log, d_skip, ssd_norm_w, q_norm_w, k_norm_w, w_out):
    bp, tp_, _ = x_prompt.shape
    bs, ts, _ = x_sample.shape
    depth = w_in.shape[0]
    assert depth == 1 and tp_ % SSD_CHUNK == 0 and ts <= SAMPLE_PAD and ts >= CONV_W - 1
    l = 0
    win = cache_k.shape[2]

    nw = norm_w[l].reshape(1, D_MODEL)
    rep = LANES // HEAD_DIM
    qnw = jnp.tile(q_norm_w[l], rep).reshape(1, LANES)
    knw = jnp.tile(k_norm_w[l], rep).reshape(1, LANES)
    cw = jnp.pad(conv_w[l], ((0, SUBLANES - CONV_W), (0, 0)))
    cb = conv_b[l].reshape(1, CONV_DIM)
    dtb = _lane_pad(dt_bias[l])
    alog = _lane_pad(a_log[l])
    dexp = jnp.repeat(d_skip[l], SSD_HEADDIM).reshape(1, SSD_INNER)
    snw = ssd_norm_w[l].reshape(1, SSD_INNER)

    pad = SAMPLE_PAD
    tabs_s = _rope_tables(PAST_LEN + np.arange(bs * pad) % pad)
    w_t, proj_s = _wconv_in(x_sample, nw, jnp.swapaxes(w_in[l], 0, 1))
    zs, xbc, dtr, q, kt, vt, gs, _ = _sections(proj_s, qnw, knw, tabs_s)
    cprev = jnp.pad(state_conv[l], ((0, 0), (SUBLANES - (CONV_W - 1), 0), (0, 0)))
    y_ssd, h_s = _ssd(xbc, dtr, zs, cprev, state_ssm[l].reshape(bs, SSD_INNER, SSD_STATE),
                      cw, cb, dtb, alog, dexp, bs, pad, ts, bs)
    knt = kt.reshape(KV_DIM, bs, pad).transpose(1, 0, 2)
    vnt = vt.reshape(KV_DIM, bs, pad).transpose(1, 0, 2)
    ckt = cache_k[l].transpose(0, 2, 3, 1).reshape(bs, KV_DIM, win)
    cvt = cache_v[l].transpose(0, 2, 3, 1).reshape(bs, KV_DIM, win)
    y_att, kot, vot = _attn_sample(q, knt, vnt, gs, ckt, cvt, ts)
    w_o, y_s = _wconv_out(x_sample, y_ssd, y_att, snw, w_out[l])
    k_s = _to_cache(kot, bs, win)
    v_s = _to_cache(vot, bs, win)
    c_s = xbc.reshape(bs, pad, CONV_DIM)[:, ts - (CONV_W - 1):ts][None]
    h_s = h_s.reshape(1, bs, SSD_HEADS, SSD_HEADDIM, SSD_STATE)

    tm = TM_IN
    xp2 = x_prompt.reshape(bp * tp_, D_MODEL)
    tabs = _rope_tables(np.arange(tp_))
    zs, xbc, dtr, q, kt, vt, gs, ctail = _in_proj(xp2, nw, w_t, qnw, knw, tabs, tm, tp_)
    y_ssd, h_p = _ssd(xbc, dtr, zs, jnp.zeros((bp, SUBLANES, CONV_DIM), F32),
                      jnp.zeros((bp, SSD_INNER, SSD_STATE), F32),
                      cw, cb, dtb, alog, dexp, bp, SSD_CHUNK, SSD_CHUNK, SSD_SEQS)
    y_att = _attn_prompt(q, kt, vt, gs, bp, tp_)
    y_p = _out_proj(xp2, y_ssd, y_att, snw, w_o, TM_OUT).reshape(bp, tp_, D_MODEL)
    keep = min(W_MAX, tp_)
    k_p = _to_cache(kt, bp, tp_)[:, :, tp_ - keep:]
    v_p = _to_cache(vt, bp, tp_)[:, :, tp_ - keep:]
    c_p = ctail[:, SUBLANES - (CONV_W - 1):][None]
    h_p = h_p.reshape(1, bp, SSD_HEADS, SSD_HEADDIM, SSD_STATE)

    return (y_p, y_s, k_p, v_p, c_p, h_p, k_s, v_s, c_s, h_s)
```

```python
import functools

import jax
import jax.numpy as jnp
import numpy as np
from jax import lax
from jax.experimental import pallas as pl
from jax.experimental.pallas import tpu as pltpu

F32 = jnp.float32
BF16 = jnp.bfloat16

D_MODEL = 2048
SSD_HEADS = 16
SSD_HEADDIM = 64
SSD_INNER = SSD_HEADS * SSD_HEADDIM
SSD_GROUPS = 2
SSD_STATE = 128
CONV_W = 4
CONV_DIM = SSD_INNER + 2 * SSD_GROUPS * SSD_STATE
SSD_CHUNK = 128
ATT_HEADS = 16
ATT_KV_HEADS = 4
HEAD_DIM = 64
ATT_GQ = ATT_HEADS // ATT_KV_HEADS
ATT_INNER = ATT_HEADS * HEAD_DIM
KV_DIM = ATT_KV_HEADS * HEAD_DIM
ROT_DIM = HEAD_DIM // 4
ROPE_THETA = 500000.0
DILATED_BRANCHES = ((128, 1), (512, 4), (2048, 16))
W_MAX = 2048
PAST_LEN = 16384
EPS = 1e-6

LANES = 128
SUBLANES = 8
Q_BLOCK = 256
K_SUPER = 256
Q_PER_SUPER = K_SUPER // Q_BLOCK
ONES_ROWS = 16
ACC_ROWS = HEAD_DIM + ONES_ROWS
CONV_COLS = 256
TM_IN = 512
TM_OUT = 512
SSD_SEQS = 4
SAMPLE_PAD = 16
NEG = -1e30
VMEM_LIMIT = 58 * 1024 * 1024

Z0 = 0
X0 = Z0 + SSD_INNER
DT0 = X0 + CONV_DIM
Q0 = DT0 + SSD_HEADS
K0 = Q0 + ATT_INNER
V0 = K0 + KV_DIM
G0 = V0 + KV_DIM
W_ROWS = G0 + ATT_INNER
W_CHUNK = 1024
HALF_INNER = SSD_INNER // SSD_GROUPS
WO_CHUNK = HALF_INNER
LOG2E = 1.4426950408889634
Q_SCALE = HEAD_DIM ** -0.5 * LOG2E


def _dot(a, b):
    return jnp.dot(a, b, preferred_element_type=F32)


def _dot_nt(a, b):
    return lax.dot_general(a, b, (((1,), (1,)), ((), ())), preferred_element_type=F32)


def _split3(x):
    hi = x.astype(BF16)
    r1 = x - hi.astype(F32)
    mid = r1.astype(BF16)
    lo = (r1 - mid.astype(F32)).astype(BF16)
    return hi, mid, lo


def _dot_exact_rhs(x, m):
    hi, mid, lo = _split3(x)
    return _dot(hi, m) + _dot(mid, m) + _dot(lo, m)


def _dot_wide_rhs(x, m):
    hi = x.astype(BF16)
    lo = (x - hi.astype(F32)).astype(BF16)
    return _dot(hi, m) + _dot(lo, m)


def _dot_exact_lhs(m, x):
    hi, mid, lo = _split3(x)
    return _dot(m, hi) + _dot(m, mid) + _dot(m, lo)


def _silu(x):
    hx = 0.5 * x
    return hx * jnp.tanh(hx) + hx


def _multiplicity(d):
    w = jnp.zeros(d.shape, F32)
    for window, dil in DILATED_BRANCHES:
        hit = (d >= 0) & (d <= window) & (lax.rem(d, dil) == 0)
        w = w + jnp.where(hit, 1.0, 0.0)
    return w


def _norm_rope(y, nw, cos, sa, sb, scale):
    lane = lax.broadcasted_iota(jnp.int32, (1, LANES), 1)
    first = lane < HEAD_DIM
    y2 = y * y
    s_lo = jnp.sum(jnp.where(first, y2, 0.0), axis=-1, keepdims=True)
    s_hi = jnp.sum(jnp.where(first, 0.0, y2), axis=-1, keepdims=True)
    ms = jnp.where(first, s_lo, s_hi) * (1.0 / HEAD_DIM)
    yn = y * lax.rsqrt(ms + EPS) * nw
    half = ROT_DIM // 2
    rot = yn * cos + pltpu.roll(yn, half, 1) * sa + pltpu.roll(yn, LANES - half, 1) * sb
    return rot * scale


def _causal_conv_silu(load, cw_ref, cb_ref, store):
    for c0 in range(0, CONV_DIM, CONV_COLS):
        cs = slice(c0, c0 + CONV_COLS)
        xe = load(cs)
        x2 = pltpu.roll(xe, 2, 0)
        even = cw_ref[3:4, cs] * xe + cw_ref[1:2, cs] * x2
        odd = cw_ref[2:3, cs] * xe + cw_ref[0:1, cs] * x2
        store(cs, _silu((even + pltpu.roll(odd, 1, 0))[SUBLANES:, :] + cb_ref[:, cs]))


def _pre_norm(x_ref, nw_ref):
    x = x_ref[...]
    ms = jnp.mean(x * x, axis=-1, keepdims=True)
    return (x * lax.rsqrt(ms + EPS) * nw_ref[...]).astype(BF16)


def _inproj_kernel(x_ref, nw_ref, w_ref, *refs):
    hn = _pre_norm(x_ref, nw_ref)

    def proj(a, b):
        return _dot_nt(hn, w_ref[a:b, :])

    _emit_sections(proj, x_ref.shape[0], *refs)


def _emit_sections(proj, tm, qnw_ref, knw_ref, cos_ref, sa_ref, sb_ref,
                   zs_ref, xbc_ref, dt_ref, q_ref, kt_ref, vt_ref, gs_ref, ctail_ref):
    cos, sa, sb = cos_ref[...], sa_ref[...], sb_ref[...]
    q = proj(Q0, K0)
    for c in range(ATT_INNER // LANES):
        sl = slice(c * LANES, (c + 1) * LANES)
        qn = _norm_rope(q[:, sl], qnw_ref[...], cos, sa, sb, Q_SCALE)
        if len(q_ref.shape) == 3:
            q_ref[0, sl, :] = qn.T.astype(BF16)
        else:
            q_ref[:, sl] = qn.astype(BF16)
    k = proj(K0, V0)
    kn = [_norm_rope(k[:, c * LANES:(c + 1) * LANES], knw_ref[...], cos, sa, sb, 1.0)
          for c in range(KV_DIM // LANES)]
    kt_ref[0] = jnp.concatenate(kn, axis=1).T

    xbc = proj(X0, DT0)
    xbc_ref[...] = xbc
    ctail_ref[0] = xbc[tm - SUBLANES:tm, :]
    vt_ref[0] = proj(V0, G0).T
    zs_ref[...] = _silu(proj(Z0, X0))
    gs = _silu(proj(G0, W_ROWS))
    if len(gs_ref.shape) == 3:
        gs_ref[0] = gs.T
    else:
        gs_ref[...] = gs
    dt_ref[...] = proj(DT0, DT0 + LANES)


def _in_proj(x2d, nw, w_t, qnw, knw, tables, tm, seq):
    rows = x2d.shape[0]
    const = lambda i: (0, 0)
    row = lambda i: (i, 0)
    sec_in, out_specs, out_shape = _section_specs(rows, tm, seq, tables[0].shape[0] // tm, True)
    return pl.pallas_call(
        _inproj_kernel,
        grid=(rows // tm,),
        in_specs=[
            pl.BlockSpec((tm, D_MODEL), row),
            pl.BlockSpec((1, D_MODEL), const),
            pl.BlockSpec((W_ROWS, D_MODEL), const, pipeline_mode=pl.Buffered(1)),
        ] + sec_in,
        out_specs=out_specs,
        out_shape=out_shape,
        compiler_params=pltpu.CompilerParams(
            dimension_semantics=("arbitrary",), vmem_limit_bytes=VMEM_LIMIT),
        name="in_proj",
    )(x2d, nw, w_t, qnw, knw, *tables)


def _section_specs(rows, tm, seq, period, q_transposed):
    per_seq = seq // tm
    const = lambda i: (0, 0)
    row = lambda i: (i, 0)
    tab = lambda i: (i % period, 0)
    tr = lambda i: (i // per_seq, 0, i % per_seq)
    slab = lambda i: (i // per_seq, 0, 0)
    in_specs = [pl.BlockSpec((1, LANES), const)] * 2 + [pl.BlockSpec((tm, LANES), tab)] * 3
    row_outs = {0: (SSD_INNER, F32), 1: (CONV_DIM, F32), 2: (LANES, F32), 3: (ATT_INNER, BF16),
                6: (ATT_INNER, F32)}
    out_specs, out_shape = [], []
    for idx in range(7):
        if idx in (3, 6) and q_transposed:
            dt = row_outs[idx][1]
            out_specs.append(pl.BlockSpec((1, ATT_INNER, tm), tr))
            out_shape.append(jax.ShapeDtypeStruct((rows // seq, ATT_INNER, seq), dt))
        elif idx in row_outs:
            n, dt = row_outs[idx]
            out_specs.append(pl.BlockSpec((tm, n), row))
            out_shape.append(jax.ShapeDtypeStruct((rows, n), dt))
        else:
            out_specs.append(pl.BlockSpec((1, KV_DIM, tm), tr))
            out_shape.append(jax.ShapeDtypeStruct((rows // seq, KV_DIM, seq), F32))
    out_specs.append(pl.BlockSpec((1, SUBLANES, CONV_DIM), slab))
    out_shape.append(jax.ShapeDtypeStruct((rows // seq, SUBLANES, CONV_DIM), F32))
    return in_specs, out_specs, out_shape


def _wconv_in_kernel(x_ref, nw_ref, w_ref, wb_ref, p_ref, xpad_scr, hn_scr):
    @pl.when(pl.program_id(0) == 0)
    def _():
        bs, ts, _ = x_ref.shape
        xpad_scr[...] = jnp.zeros(xpad_scr.shape, F32)
        for b in range(bs):
            xpad_scr[b * SAMPLE_PAD:b * SAMPLE_PAD + ts, :] = x_ref[b]
        hn_scr[...] = _pre_norm(xpad_scr, nw_ref)

    row = pl.program_id(0) * W_CHUNK + lax.broadcasted_iota(jnp.int32, (W_CHUNK, 1), 0)
    wb = jnp.where(row < W_ROWS, w_ref[...], 0.0).astype(BF16)
    wb_ref[...] = wb
    p_ref[0] = _dot_nt(hn_scr[...], wb)


def _wconv_in(x_s, nw, w_f32):
    rows = x_s.shape[0] * SAMPLE_PAD
    n_chunks = pl.cdiv(W_ROWS, W_CHUNK)
    const = lambda j: (0, 0)
    return pl.pallas_call(
        _wconv_in_kernel,
        grid=(n_chunks,),
        in_specs=[
            pl.BlockSpec(x_s.shape, lambda j: (0, 0, 0)),
            pl.BlockSpec((1, D_MODEL), const),
            pl.BlockSpec((W_CHUNK, D_MODEL), lambda j: (j, 0)),
        ],
        out_specs=[
            pl.BlockSpec((W_CHUNK, D_MODEL), lambda j: (j, 0)),
            pl.BlockSpec((1, rows, W_CHUNK), lambda j: (j, 0, 0)),
        ],
        out_shape=[
            jax.ShapeDtypeStruct((W_ROWS, D_MODEL), BF16),
            jax.ShapeDtypeStruct((n_chunks, rows, W_CHUNK), F32),
        ],
        scratch_shapes=[pltpu.VMEM((rows, D_MODEL), F32), pltpu.VMEM((rows, D_MODEL), BF16)],
        compiler_params=pltpu.CompilerParams(
            dimension_semantics=("arbitrary",), vmem_limit_bytes=VMEM_LIMIT),
        name="wconv_in",
    )(x_s, nw, w_f32)


def _sections_kernel(p_ref, *refs):
    *refs, tail_scr = refs
    pfull = jnp.concatenate([p_ref[j] for j in range(p_ref.shape[0])], axis=1)
    tail_scr[...] = pfull[:, Q0:W_ROWS]

    def proj(a, b):
        if a >= Q0:
            return tail_scr[:, a - Q0:b - Q0]
        return pfull[:, a:b]

    _emit_sections(proj, p_ref.shape[1], *refs)


def _sections(p, qnw, knw, tables):
    rows = p.shape[1]
    sec_in, out_specs, out_shape = _section_specs(rows, rows, rows, 1, False)
    return pl.pallas_call(
        _sections_kernel,
        grid=(1,),
        in_specs=[pl.BlockSpec(p.shape, lambda i: (0, 0, 0))] + sec_in,
        out_specs=out_specs,
        out_shape=out_shape,
        scratch_shapes=[pltpu.VMEM((rows, W_ROWS - Q0), F32)],
        compiler_params=pltpu.CompilerParams(
            dimension_semantics=("arbitrary",), vmem_limit_bytes=VMEM_LIMIT),
        name="sections",
    )(p, qnw, knw, *tables)


def _ssd_chain(s, xbc_ref, dtr_ref, z_ref, cw_ref, cb_ref, dtb_ref, alog_ref, dexp_ref, y_ref,
               xext, ht, xc_scr, lin, valid):
    L = lin

    lane = lax.broadcasted_iota(jnp.int32, (1, LANES), 1)
    rowid = lax.broadcasted_iota(jnp.int32, (L, 1), 0)
    dt = jax.nn.softplus(dtr_ref[s] + dtb_ref[...])
    dt = jnp.where((lane < SSD_HEADS) & (rowid < valid), dt, 0.0)
    dta = dt * (-jnp.exp(alog_ref[...]) * LOG2E)

    r2 = lax.broadcasted_iota(jnp.int32, (L, L), 0)
    c2 = lax.broadcasted_iota(jnp.int32, (L, L), 1)
    tri = r2 >= c2
    tri_b = jnp.where(tri, 1.0, 0.0).astype(BF16)
    cum = _dot_exact_lhs(tri_b, dta)
    cum_last = cum[L - 1:L, :]
    yield

    er = lax.broadcasted_iota(jnp.int32, (LANES, SSD_INNER), 0)
    ec = lax.broadcasted_iota(jnp.int32, (LANES, SSD_INNER), 1)
    expand = jnp.where(ec // SSD_HEADDIM == er, 1.0, 0.0).astype(BF16)
    ecum = _dot_wide_rhs(jnp.exp2(cum), expand)
    wexp = _dot_wide_rhs(jnp.exp2(cum_last - cum) * dt, expand)
    cd = _dot_exact_rhs(jnp.broadcast_to(jnp.exp2(cum_last), (SUBLANES, LANES)), expand)[0:1, :]
    cum_t = cum.T
    dt_t = dt.T
    yield

    xext[s, SUBLANES:SUBLANES + L, :] = xbc_ref[s]

    def load(cs):
        return xext[s, 0:SUBLANES + L, cs]

    def store(cs, v):
        xc_scr[s, :, cs] = v

    _causal_conv_silu(load, cw_ref, cb_ref, store)
    xext[s, 0:SUBLANES, :] = xext[s, L:L + SUBLANES, :]
    xc = xc_scr.at[s]
    xs = xc[:, :SSD_INNER]
    bm = xc[:, SSD_INNER:SSD_INNER + SSD_GROUPS * SSD_STATE]
    cm = xc[:, SSD_INNER + SSD_GROUPS * SSD_STATE:]
    bm_b = bm.astype(BF16)
    cm_b = cm.astype(BF16)
    xs_b = xs.astype(BF16)

    y_off = jnp.concatenate(
        [_dot(cm_b[:, g * SSD_STATE:(g + 1) * SSD_STATE], ht[s, g].astype(BF16))
         for g in range(SSD_GROUPS)], axis=1) * ecum
    cbs = [_dot_nt(cm_b[:, g * SSD_STATE:(g + 1) * SSD_STATE], bm_b[:, g * SSD_STATE:(g + 1) * SSD_STATE])
           for g in range(SSD_GROUPS)]
    yield

    lane_half = lane // HEAD_DIM
    heads_per_group = SSD_HEADS // SSD_GROUPS
    y_parts = []
    for g in range(SSD_GROUPS):
        for pr in range(heads_per_group // 2):
            col0 = g * HALF_INNER + pr * LANES
            xp = xs_b[:, col0:col0 + LANES]
            yp = jnp.zeros((L, LANES), F32)
            for e in range(2):
                hh = g * heads_per_group + pr * 2 + e
                decay = jnp.exp2(cum[:, hh:hh + 1] - cum_t[hh:hh + 1, :])
                sc = jnp.where(tri, cbs[g] * decay, 0.0) * dt_t[hh:hh + 1, :]
                xm = jnp.where(lane_half == e, xp, jnp.zeros_like(xp))
                yp = yp + _dot(sc.astype(BF16), xm)
            y_parts.append(yp)
            if pr % 2 == 1:
                yield
    y_diag = jnp.concatenate(y_parts, axis=1)

    xw = (xs * wexp).astype(BF16)
    for g in range(SSD_GROUPS):
        sl = slice(g * HALF_INNER, (g + 1) * HALF_INNER)
        bm_t = bm[:, g * SSD_STATE:(g + 1) * SSD_STATE].T.astype(BF16)
        ht[s, g] = ht[s, g] * cd[:, sl] + _dot(bm_t, xw[:, sl])
    yield

    y_ref[s] = (y_diag + y_off + dexp_ref[...] * xs) * z_ref[s]


def _ssd_kernel(xbc_ref, dtr_ref, z_ref, *refs, lin, valid):
    *start_refs, cw_ref, cb_ref, dtb_ref, alog_ref, dexp_ref, y_ref, hout_ref, xext, ht, xc_scr = refs
    c = pl.program_id(1)

    def for_each_state(fn):
        def step(j, carry):
            g = j % SSD_GROUPS
            fn(j // SSD_GROUPS, g, pl.multiple_of(g * HALF_INNER, HALF_INNER))
            return carry

        lax.fori_loop(0, SSD_SEQS * SSD_GROUPS, step, 0, unroll=2)

    @pl.when(c == 0)
    def _():
        if not start_refs:
            xext[:, 0:SUBLANES, :] = jnp.zeros((SSD_SEQS, SUBLANES, CONV_DIM), F32)
            ht[...] = jnp.zeros(ht.shape, F32)
            return
        cprev_ref, h0_ref = start_refs
        for s in range(SSD_SEQS):
            xext[s, 0:SUBLANES, :] = cprev_ref[s]

        def load_state(s, g, r0):
            ht[s, g] = h0_ref[s, pl.ds(r0, HALF_INNER), :].T

        for_each_state(load_state)

    chains = [_ssd_chain(s, xbc_ref, dtr_ref, z_ref, cw_ref, cb_ref, dtb_ref, alog_ref, dexp_ref, y_ref,
                         xext, ht, xc_scr, lin, valid) for s in range(SSD_SEQS)]
    while chains:
        alive = []
        for ch in chains:
            try:
                next(ch)
                alive.append(ch)
            except StopIteration:
                pass
        chains = alive

    @pl.when(c == pl.num_programs(1) - 1)
    def _():
        def store_state(s, g, r0):
            hout_ref[s, pl.ds(r0, HALF_INNER), :] = ht[s, g].T

        for_each_state(store_state)


def _ssd(xbc, dtr, z, start, cw, cb, dtb, alog, dexp, batch, lin, valid):
    rows = xbc.shape[0]
    seq = rows // batch
    nc = seq // lin
    g = SSD_SEQS
    assert batch % g == 0
    blk = lambda b, c: (b, c, 0)
    per_b = lambda b, c: (b, 0, 0)
    const = lambda b, c: (0, 0)
    start_specs = [pl.BlockSpec((g, SUBLANES, CONV_DIM), per_b),
                   pl.BlockSpec((g, SSD_INNER, SSD_STATE), per_b)] if start else []
    y, hout = pl.pallas_call(
        functools.partial(_ssd_kernel, lin=lin, valid=valid),
        grid=(batch // g, nc),
        in_specs=[
            pl.BlockSpec((g, lin, CONV_DIM), blk),
            pl.BlockSpec((g, lin, LANES), blk),
            pl.BlockSpec((g, lin, SSD_INNER), blk),
        ] + start_specs + [
            pl.BlockSpec((SUBLANES, CONV_DIM), const),
            pl.BlockSpec((1, CONV_DIM), const),
            pl.BlockSpec((1, LANES), const),
            pl.BlockSpec((1, LANES), const),
            pl.BlockSpec((1, SSD_INNER), const),
        ],
        out_specs=[
            pl.BlockSpec((g, lin, SSD_INNER), blk),
            pl.BlockSpec((g, SSD_INNER, SSD_STATE), per_b),
        ],
        out_shape=[
            jax.ShapeDtypeStruct((batch, seq, SSD_INNER), F32),
            jax.ShapeDtypeStruct((batch, SSD_INNER, SSD_STATE), F32),
        ],
        scratch_shapes=[
            pltpu.VMEM((g, SUBLANES + lin + SUBLANES, CONV_DIM), F32),
            pltpu.VMEM((g, SSD_GROUPS, SSD_STATE, HALF_INNER), F32),
            pltpu.VMEM((g, lin, CONV_DIM), F32),
        ],
        compiler_params=pltpu.CompilerParams(
            dimension_semantics=("arbitrary", "arbitrary"), vmem_limit_bytes=VMEM_LIMIT),
        name="ssd",
    )(xbc.reshape(batch, seq, CONV_DIM), dtr.reshape(batch, seq, LANES),
      z.reshape(batch, seq, SSD_INNER), *start, cw, cb, dtb, alog, dexp)
    return y.reshape(rows, SSD_INNER), hout


def _attn_prompt_kernel(q_ref, kt_ref, vt_ref, g_ref, o_ref,
                        bias_scr, kh_scr, va_scr, qt_scr, m_scr, acc_scr, s_scr, s2_scr, *, n_tab, n_sb_max):
    b = pl.program_id(0)
    i = pl.program_id(1)
    seq = kt_ref.shape[2]
    pair_w = 2 * Q_BLOCK
    pairs_per_kv = ATT_GQ // 2

    @pl.when((b == 0) & (i == 0))
    def _():
        r = lax.broadcasted_iota(jnp.int32, (K_SUPER, Q_BLOCK), 0)
        c = lax.broadcasted_iota(jnp.int32, (K_SUPER, Q_BLOCK), 1)
        def table(tb, carry):
            w = _multiplicity(tb * Q_BLOCK + c - r)
            bias_scr[tb] = jnp.where(w > 0.0, jnp.log2(jnp.maximum(w, 1.0)), NEG)
            return carry

        lax.fori_loop(0, n_tab, table, 0)
        va_scr[:, HEAD_DIM:, :] = jnp.ones((ATT_KV_HEADS, ONES_ROWS, seq), BF16)

    @pl.when(i == 0)
    def _():
        for kvh in range(ATT_KV_HEADS):
            hs = slice(kvh * HEAD_DIM, (kvh + 1) * HEAD_DIM)
            va_scr[kvh, 0:HEAD_DIM, :] = vt_ref[0, hs, :].astype(BF16)
        for cidx in range(seq // K_SUPER):
            rows = slice(cidx * K_SUPER, (cidx + 1) * K_SUPER)
            kc = kt_ref[0, :, rows].T
            for kvh in range(ATT_KV_HEADS):
                hs = slice(kvh * HEAD_DIM, (kvh + 1) * HEAD_DIM)
                kh_scr[kvh, rows, :] = kc[:, hs].astype(BF16)

    for kvh in range(ATT_KV_HEADS):
        qt_scr[kvh] = jnp.concatenate(
            [q_ref[0, (kvh * ATT_GQ + g) * HEAD_DIM:(kvh * ATT_GQ + g + 1) * HEAD_DIM, :]
             for g in range(ATT_GQ)], axis=1)
    m_scr[...] = jnp.full(m_scr.shape, NEG, F32)
    acc_scr[...] = jnp.zeros(acc_scr.shape, F32)
    par = i % Q_PER_SUPER
    j_last = i // Q_PER_SUPER
    n_sb = jnp.minimum(j_last + 1, n_sb_max)

    def key_start(dl):
        return pl.multiple_of((j_last - dl) * K_SUPER, K_SUPER)

    def scores(dl, dst):
        for kvh in range(ATT_KV_HEADS):
            dst[kvh] = _dot(kh_scr[kvh, pl.ds(key_start(dl), K_SUPER), :], qt_scr[kvh])

    def softmax_pv(dl, src):
        start = key_start(dl)
        bias = bias_scr[par + Q_PER_SUPER * dl]
        bias2 = jnp.concatenate([bias, bias], axis=1)
        for kvh in range(ATT_KV_HEADS):
            vtb = va_scr[kvh, :, pl.ds(start, K_SUPER)]
            for pr in range(pairs_per_kv):
                u = kvh * pairs_per_kv + pr
                s = src[kvh, :, pr * pair_w:(pr + 1) * pair_w] + bias2
                m_old = m_scr[u, 0:1, :]
                m_new = jnp.maximum(m_old, jnp.max(s, axis=0, keepdims=True))
                p = jnp.exp2(s - m_new).astype(BF16)
                acc_scr[u] = jnp.exp2(m_old - m_new) * acc_scr[u] + _dot(vtb, p)
                m_scr[u, 0:1, :] = m_new

    def diagonal_block():
        half = K_SUPER // 2
        start = key_start(0)
        bias = bias_scr[0]
        for kvh in range(ATT_KV_HEADS):
            s_scr[kvh, 0:half, :] = _dot(kh_scr[kvh, pl.ds(start, half), :], qt_scr[kvh])
        for kvh in range(ATT_KV_HEADS):
            q_late = jnp.concatenate(
                [qt_scr[kvh, :, g * Q_BLOCK + half:(g + 1) * Q_BLOCK] for g in range(ATT_GQ)], axis=1)
            s2_scr[kvh, 0:half, 0:ATT_GQ * half] = _dot(
                kh_scr[kvh, pl.ds(start + half, half), :], q_late)
        bias_a = jnp.concatenate([bias[0:half, :]] * 2, axis=1)
        for kvh in range(ATT_KV_HEADS):
            vtb = va_scr[kvh, :, pl.ds(start, half)]
            for pr in range(pairs_per_kv):
                u = kvh * pairs_per_kv + pr
                s = s_scr[kvh, 0:half, pr * pair_w:(pr + 1) * pair_w] + bias_a
                m_old = m_scr[u, 0:1, :]
                m_new = jnp.maximum(m_old, jnp.max(s, axis=0, keepdims=True))
                p = jnp.exp2(s - m_new).astype(BF16)
                acc_scr[u] = jnp.exp2(m_old - m_new) * acc_scr[u] + _dot(vtb, p)
                m_scr[u, 0:1, :] = m_new
        bias_b = jnp.concatenate([bias[half:, half:]] * 2, axis=1)
        late = [slice(e * Q_BLOCK + half, (e + 1) * Q_BLOCK) for e in range(2)]
        for kvh in range(ATT_KV_HEADS):
            vtb = va_scr[kvh, :, pl.ds(start + half, half)]
            for pr in range(pairs_per_kv):
                u = kvh * pairs_per_kv + pr
                s = s2_scr[kvh, 0:half, pr * Q_BLOCK:(pr + 1) * Q_BLOCK] + bias_b
                m_old = jnp.concatenate([m_scr[u, 0:1, sl] for sl in late], axis=1)
                m_new = jnp.maximum(m_old, jnp.max(s, axis=0, keepdims=True))
                p = jnp.exp2(s - m_new).astype(BF16)
                alpha = jnp.exp2(m_old - m_new)
                pv = _dot(vtb, p)
                for e, sl in enumerate(late):
                    es = slice(e * half, (e + 1) * half)
                    acc_scr[u, :, sl] = alpha[:, es] * acc_scr[u, :, sl] + pv[:, es]
                    m_scr[u, 0:1, sl] = m_new[:, es]

    diagonal_block()
    n_rest = n_sb - 1

    def body(t, carry):
        scores(1 + 2 * t, s_scr)
        scores(2 + 2 * t, s2_scr)
        softmax_pv(1 + 2 * t, s_scr)
        softmax_pv(2 + 2 * t, s2_scr)
        return carry

    lax.fori_loop(0, n_rest // 2, body, 0)

    @pl.when(n_rest % 2 == 1)
    def _():
        scores(n_sb - 1, s_scr)
        softmax_pv(n_sb - 1, s_scr)

    for u in range(ATT_HEADS // 2):
        acc = acc_scr[u]
        o = acc[0:HEAD_DIM] * (1.0 / acc[HEAD_DIM:HEAD_DIM + 1])
        for e in range(2):
            hs = slice((2 * u + e) * HEAD_DIM, (2 * u + e + 1) * HEAD_DIM)
            o_ref[0, hs, :] = (o[:, e * Q_BLOCK:(e + 1) * Q_BLOCK] * g_ref[0, hs, :]).astype(BF16)


def _attn_prompt(qt, kt, vt, gate_t, batch, seq):
    nqb = seq // Q_BLOCK
    assert seq % K_SUPER == 0 and Q_BLOCK == K_SUPER
    n_sb_max = min(seq // K_SUPER, W_MAX // K_SUPER + 1)
    n_tab = Q_PER_SUPER * n_sb_max
    qblk = lambda b, i: (b, 0, i)
    per_b = lambda b, i: (b, 0, 0)
    return pl.pallas_call(
        functools.partial(_attn_prompt_kernel, n_tab=n_tab, n_sb_max=n_sb_max),
        grid=(batch, nqb),
        in_specs=[
            pl.BlockSpec((1, ATT_INNER, Q_BLOCK), qblk),
            pl.BlockSpec((1, KV_DIM, seq), per_b),
            pl.BlockSpec((1, KV_DIM, seq), per_b),
            pl.BlockSpec((1, ATT_INNER, Q_BLOCK), qblk),
        ],
        out_specs=pl.BlockSpec((1, ATT_INNER, Q_BLOCK), qblk),
        out_shape=jax.ShapeDtypeStruct((batch, ATT_INNER, seq), BF16),
        scratch_shapes=[
            pltpu.VMEM((n_tab, K_SUPER, Q_BLOCK), F32),
            pltpu.VMEM((ATT_KV_HEADS, seq, HEAD_DIM), BF16),
            pltpu.VMEM((ATT_KV_HEADS, ACC_ROWS, seq), BF16),
            pltpu.VMEM((ATT_KV_HEADS, HEAD_DIM, ATT_GQ * Q_BLOCK), BF16),
            pltpu.VMEM((ATT_HEADS // 2, SUBLANES, 2 * Q_BLOCK), F32),
            pltpu.VMEM((ATT_HEADS // 2, ACC_ROWS, 2 * Q_BLOCK), F32),
            pltpu.VMEM((ATT_KV_HEADS, K_SUPER, ATT_GQ * Q_BLOCK), F32),
            pltpu.VMEM((ATT_KV_HEADS, K_SUPER, ATT_GQ * Q_BLOCK), F32),
        ],
        compiler_params=pltpu.CompilerParams(
            dimension_semantics=("arbitrary", "arbitrary"), vmem_limit_bytes=VMEM_LIMIT),
        name="attn_prompt",
    )(qt, kt, vt, gate_t)


def _attn_sample_kernel(q_ref, knt_ref, vnt_ref, g_ref, ckt_ref, cvt_ref, o_ref, kot_ref, vot_ref, *, ts):
    win = ckt_ref.shape[2]
    tp = SAMPLE_PAD
    rows = ATT_GQ * tp
    t_c = lax.broadcasted_iota(jnp.int32, (rows, win), 0) % tp
    j_c = lax.broadcasted_iota(jnp.int32, (rows, win), 1)
    w_c = _multiplicity(win + t_c - j_c)
    t_n = lax.broadcasted_iota(jnp.int32, (rows, tp), 0) % tp
    j_n = lax.broadcasted_iota(jnp.int32, (rows, tp), 1)
    w_n = jnp.where(j_n < ts, _multiplicity(t_n - j_n), 0.0)
    for kvh in range(ATT_KV_HEADS):
        c0 = kvh * ATT_GQ * HEAD_DIM
        qh = jnp.concatenate(
            [q_ref[:, c0 + g * HEAD_DIM:c0 + (g + 1) * HEAD_DIM] for g in range(ATT_GQ)], axis=0)
        hs = slice(kvh * HEAD_DIM, (kvh + 1) * HEAD_DIM)
        s_c = jnp.where(w_c > 0.0, _dot(qh, ckt_ref[0, hs, :].astype(BF16)), NEG)
        s_n = jnp.where(w_n > 0.0, _dot(qh, knt_ref[0, hs, :].astype(BF16)), NEG)
        m = jnp.maximum(jnp.max(s_c, axis=-1, keepdims=True), jnp.max(s_n, axis=-1, keepdims=True))
        p_c = jnp.exp2(s_c - m) * w_c
        p_n = jnp.exp2(s_n - m) * w_n
        den = jnp.sum(p_c, axis=-1, keepdims=True) + jnp.sum(p_n, axis=-1, keepdims=True)
        num = (_dot_nt(p_c.astype(BF16), cvt_ref[0, hs, :].astype(BF16))
               + _dot_nt(p_n.astype(BF16), vnt_ref[0, hs, :].astype(BF16)))
        o = num / den
        o = jnp.concatenate([o[g * tp:(g + 1) * tp] for g in range(ATT_GQ)], axis=1)
        csl = slice(c0, c0 + ATT_GQ * HEAD_DIM)
        o_ref[:, csl] = (o * g_ref[:, csl]).astype(BF16)
    kot_ref[0, :, 0:win - ts] = ckt_ref[0, :, ts:win]
    kot_ref[0, :, win - ts:win] = knt_ref[0, :, 0:ts]
    vot_ref[0, :, 0:win - ts] = cvt_ref[0, :, ts:win]
    vot_ref[0, :, win - ts:win] = vnt_ref[0, :, 0:ts]


def _attn_sample(q, knt, vnt, gate, ckt, cvt, ts):
    batch, _, win = ckt.shape
    tp = SAMPLE_PAD
    blk = lambda b: (b, 0)
    per_b = lambda b: (b, 0, 0)
    return pl.pallas_call(
        functools.partial(_attn_sample_kernel, ts=ts),
        grid=(batch,),
        in_specs=[
            pl.BlockSpec((tp, ATT_INNER), blk),
            pl.BlockSpec((1, KV_DIM, tp), per_b),
            pl.BlockSpec((1, KV_DIM, tp), per_b),
            pl.BlockSpec((tp, ATT_INNER), blk),
            pl.BlockSpec((1, KV_DIM, win), per_b),
            pl.BlockSpec((1, KV_DIM, win), per_b),
        ],
        out_specs=[
            pl.BlockSpec((tp, ATT_INNER), blk),
            pl.BlockSpec((1, KV_DIM, win), per_b),
            pl.BlockSpec((1, KV_DIM, win), per_b),
        ],
        out_shape=[
            jax.ShapeDtypeStruct((batch * tp, ATT_INNER), BF16),
            jax.ShapeDtypeStruct((batch, KV_DIM, win), F32),
            jax.ShapeDtypeStruct((batch, KV_DIM, win), F32),
        ],
        compiler_params=pltpu.CompilerParams(
            dimension_semantics=("arbitrary",), vmem_limit_bytes=VMEM_LIMIT),
        name="attn_sample",
    )(q, knt, vnt, gate, ckt, cvt)


def _group_norm(yg, nw):
    ms = jnp.mean(yg * yg, axis=-1, keepdims=True)
    return (yg * lax.rsqrt(ms + EPS) * nw).astype(BF16)


def _outproj_kernel(x_ref, ys_ref, yat_ref, nw_ref, w_ref, o_ref):
    acc = x_ref[...]
    for g in range(SSD_GROUPS):
        gs = slice(g * HALF_INNER, (g + 1) * HALF_INNER)
        acc = acc + _dot(_group_norm(ys_ref[:, gs], nw_ref[:, gs]), w_ref[gs, :])
    ya = yat_ref[0].astype(F32).T.astype(BF16)
    o_ref[...] = acc + _dot(ya, w_ref[SSD_INNER:, :])


def _out_proj(x2d, y_ssd, y_att_t, nw, w_o, tm):
    rows = x2d.shape[0]
    per_seq = y_att_t.shape[2] // tm
    row = lambda i: (i, 0)
    const = lambda i: (0, 0)
    return pl.pallas_call(
        _outproj_kernel,
        grid=(rows // tm,),
        in_specs=[
            pl.BlockSpec((tm, D_MODEL), row),
            pl.BlockSpec((tm, SSD_INNER), row),
            pl.BlockSpec((1, ATT_INNER, tm), lambda i: (i // per_seq, 0, i % per_seq)),
            pl.BlockSpec((1, SSD_INNER), const),
            pl.BlockSpec((SSD_INNER + ATT_INNER, D_MODEL), const, pipeline_mode=pl.Buffered(1)),
        ],
        out_specs=pl.BlockSpec((tm, D_MODEL), row),
        out_shape=jax.ShapeDtypeStruct((rows, D_MODEL), F32),
        compiler_params=pltpu.CompilerParams(
            dimension_semantics=("arbitrary",), vmem_limit_bytes=VMEM_LIMIT),
        name="out_proj",
    )(x2d, y_ssd, y_att_t, nw, w_o)


def _wconv_out_kernel(x_ref, ys_ref, ya_ref, nw_ref, w_ref, wb_ref, o_ref):
    j = pl.program_id(0)
    wb = w_ref[...].astype(BF16)
    wb_ref[...] = wb
    y = jnp.where(j < SSD_GROUPS, _group_norm(ys_ref[...], nw_ref[...]), ya_ref[...])
    part = _dot(y, wb)
    bs, ts, _ = x_ref.shape

    @pl.when(j == 0)
    def _():
        for b in range(bs):
            o_ref[b] = x_ref[b] + part[b * SAMPLE_PAD:b * SAMPLE_PAD + ts]

    @pl.when(j > 0)
    def _():
        for b in range(bs):
            o_ref[b] += part[b * SAMPLE_PAD:b * SAMPLE_PAD + ts]


def _wconv_out(x_s, y_ssd, y_att, nw, w_f32):
    rows = x_s.shape[0] * SAMPLE_PAD
    n_ssd = SSD_GROUPS
    n_chunks = (SSD_INNER + ATT_INNER) // WO_CHUNK
    const = lambda j: (0, 0)
    ssd_chunk = lambda j: (0, jnp.minimum(j, n_ssd - 1))
    return pl.pallas_call(
        _wconv_out_kernel,
        grid=(n_chunks,),
        in_specs=[
            pl.BlockSpec(x_s.shape, lambda j: (0, 0, 0)),
            pl.BlockSpec((rows, WO_CHUNK), ssd_chunk),
            pl.BlockSpec((rows, WO_CHUNK), lambda j: (0, jnp.maximum(j - n_ssd, 0))),
            pl.BlockSpec((1, WO_CHUNK), ssd_chunk),
            pl.BlockSpec((WO_CHUNK, D_MODEL), lambda j: (j, 0)),
        ],
        out_specs=[
            pl.BlockSpec((WO_CHUNK, D_MODEL), lambda j: (j, 0)),
            pl.BlockSpec(x_s.shape, lambda j: (0, 0, 0)),
        ],
        out_shape=[
            jax.ShapeDtypeStruct((SSD_INNER + ATT_INNER, D_MODEL), BF16),
            jax.ShapeDtypeStruct(x_s.shape, F32),
        ],
        compiler_params=pltpu.CompilerParams(
            dimension_semantics=("arbitrary",), vmem_limit_bytes=VMEM_LIMIT),
        name="wconv_out",
    )(x_s, y_ssd, y_att, nw, w_f32)


def _rope_tables(pos):
    n = pos.shape[0]
    half = ROT_DIM // 2
    inv = ROPE_THETA ** (-np.arange(0, ROT_DIM, 2, dtype=np.float64) / ROT_DIM)
    ang = pos.astype(np.float64)[:, None] * inv[None, :]
    cos, sin = np.cos(ang), np.sin(ang)
    rest = HEAD_DIM - ROT_DIM
    zh = np.zeros((n, half))
    cos_h = np.concatenate([cos, cos, np.ones((n, rest))], axis=1)
    sa_h = np.concatenate([zh, sin, np.zeros((n, rest))], axis=1)
    sb_h = np.concatenate([-sin, zh, np.zeros((n, rest))], axis=1)
    rep = LANES // HEAD_DIM
    return tuple(jnp.asarray(np.tile(t, (1, rep)), dtype=F32) for t in (cos_h, sa_h, sb_h))


def _lane_pad(v, n=LANES):
    return jnp.pad(v, (0, n - v.shape[0])).reshape(1, n)


def _to_cache(xt, batch, seq):
    return xt.reshape(1, batch, ATT_KV_HEADS, HEAD_DIM, seq).transpose(0, 1, 4, 2, 3)


def kernel(x_prompt, x_sample, cache_k, cache_v, state_conv, state_ssm, norm_w, w_in, conv_w,
           conv_b, dt_bias, a_log, d_skip, ssd_norm_w, q_norm_w, k_norm_w, w_out):
    bp, tp_, _ = x_prompt.shape
    bs, ts, _ = x_sample.shape
    depth = w_in.shape[0]
    assert depth == 1 and tp_ % SSD_CHUNK == 0 and ts <= SAMPLE_PAD and ts >= CONV_W - 1
    l = 0
    win = cache_k.shape[2]

    nw = norm_w[l].reshape(1, D_MODEL)
    rep = LANES // HEAD_DIM
    qnw = jnp.tile(q_norm_w[l], rep).reshape(1, LANES)
    knw = jnp.tile(k_norm_w[l], rep).reshape(1, LANES)
    cw = jnp.pad(conv_w[l], ((0, SUBLANES - CONV_W), (0, 0)))
    cb = conv_b[l].reshape(1, CONV_DIM)
    dtb = _lane_pad(dt_bias[l])
    alog = _lane_pad(a_log[l])
    dexp = jnp.repeat(d_skip[l], SSD_HEADDIM).reshape(1, SSD_INNER)
    snw = ssd_norm_w[l].reshape(1, SSD_INNER)

    pad = SAMPLE_PAD
    tabs_s = _rope_tables(PAST_LEN + np.arange(bs * pad) % pad)
    w_t, proj_s = _wconv_in(x_sample, nw, jnp.swapaxes(w_in[l], 0, 1))
    zs, xbc, dtr, q, kt, vt, gs, _ = _sections(proj_s, qnw, knw, tabs_s)
    cprev = jnp.pad(state_conv[l], ((0, 0), (SUBLANES - (CONV_W - 1), 0), (0, 0)))
    y_ssd, h_s = _ssd(xbc, dtr, zs, (cprev, state_ssm[l].reshape(bs, SSD_INNER, SSD_STATE)),
                      cw, cb, dtb, alog, dexp, bs, pad, ts)
    knt = kt.reshape(KV_DIM, bs, pad).transpose(1, 0, 2)
    vnt = vt.reshape(KV_DIM, bs, pad).transpose(1, 0, 2)
    ckt = cache_k[l].transpose(0, 2, 3, 1).reshape(bs, KV_DIM, win)
    cvt = cache_v[l].transpose(0, 2, 3, 1).reshape(bs, KV_DIM, win)
    y_att, kot, vot = _attn_sample(q, knt, vnt, gs, ckt, cvt, ts)
    w_o, y_s = _wconv_out(x_sample, y_ssd, y_att, snw, w_out[l])
    k_s = _to_cache(kot, bs, win)
    v_s = _to_cache(vot, bs, win)
    c_s = xbc.reshape(bs, pad, CONV_DIM)[:, ts - (CONV_W - 1):ts][None]
    h_s = h_s.reshape(1, bs, SSD_HEADS, SSD_HEADDIM, SSD_STATE)

    tm = TM_IN
    xp2 = x_prompt.reshape(bp * tp_, D_MODEL)
    tabs = _rope_tables(np.arange(tp_))
    zs, xbc, dtr, q, kt, vt, gs, ctail = _in_proj(xp2, nw, w_t, qnw, knw, tabs, tm, tp_)
    y_ssd, h_p = _ssd(xbc, dtr, zs, (), cw, cb, dtb, alog, dexp, bp, SSD_CHUNK, SSD_CHUNK)
    y_att = _attn_prompt(q, kt, vt, gs, bp, tp_)
    y_p = _out_proj(xp2, y_ssd, y_att, snw, w_o, TM_OUT).reshape(bp, tp_, D_MODEL)
    keep = min(W_MAX, tp_)
    k_p = _to_cache(kt, bp, tp_)[:, :, tp_ - keep:]
    v_p = _to_cache(vt, bp, tp_)[:, :, tp_ - keep:]
    c_p = ctail[:, SUBLANES - (CONV_W - 1):][None]
    h_p = h_p.reshape(1, bp, SSD_HEADS, SSD_HEADDIM, SSD_STATE)

    return (y_p, y_s, k_p, v_p, c_p, h_p, k_s, v_s, c_s, h_s)
```

```python
import functools

import jax
import jax.numpy as jnp
import numpy as np
from jax import lax
from jax.experimental import pallas as pl
from jax.experimental.pallas import tpu as pltpu

F32 = jnp.float32
BF16 = jnp.bfloat16

D_MODEL = 2048
SSD_HEADS = 16
SSD_HEADDIM = 64
SSD_INNER = SSD_HEADS * SSD_HEADDIM
SSD_GROUPS = 2
SSD_STATE = 128
CONV_W = 4
CONV_DIM = SSD_INNER + 2 * SSD_GROUPS * SSD_STATE
SSD_CHUNK = 128
ATT_HEADS = 16
ATT_KV_HEADS = 4
HEAD_DIM = 64
ATT_GQ = ATT_HEADS // ATT_KV_HEADS
ATT_INNER = ATT_HEADS * HEAD_DIM
KV_DIM = ATT_KV_HEADS * HEAD_DIM
ROT_DIM = HEAD_DIM // 4
ROPE_THETA = 500000.0
DILATED_BRANCHES = ((128, 1), (512, 4), (2048, 16))
W_MAX = 2048
PAST_LEN = 16384
EPS = 1e-6

LANES = 128
SUBLANES = 8
Q_BLOCK = 256
K_SUPER = 256
Q_PER_SUPER = K_SUPER // Q_BLOCK
ONES_ROWS = 16
ACC_ROWS = HEAD_DIM + ONES_ROWS
CONV_COLS = 256
TM_IN = 512
TM_OUT = 512
SSD_SEQS = 4
SAMPLE_PAD = 16
NEG = -1e30
VMEM_LIMIT = 58 * 1024 * 1024

Z0 = 0
X0 = Z0 + SSD_INNER
DT0 = X0 + CONV_DIM
Q0 = DT0 + SSD_HEADS
K0 = Q0 + ATT_INNER
V0 = K0 + KV_DIM
G0 = V0 + KV_DIM
W_ROWS = G0 + ATT_INNER
W_CHUNK = 1024
HALF_INNER = SSD_INNER // SSD_GROUPS
WO_CHUNK = HALF_INNER
LOG2E = 1.4426950408889634
Q_SCALE = HEAD_DIM ** -0.5 * LOG2E


def _dot(a, b):
    return jnp.dot(a, b, preferred_element_type=F32)


def _dot_nt(a, b):
    return lax.dot_general(a, b, (((1,), (1,)), ((), ())), preferred_element_type=F32)


def _split3(x):
    hi = x.astype(BF16)
    r1 = x - hi.astype(F32)
    mid = r1.astype(BF16)
    lo = (r1 - mid.astype(F32)).astype(BF16)
    return hi, mid, lo


def _dot_exact_rhs(x, m):
    hi, mid, lo = _split3(x)
    return _dot(hi, m) + _dot(mid, m) + _dot(lo, m)


def _dot_wide_rhs(x, m):
    hi = x.astype(BF16)
    lo = (x - hi.astype(F32)).astype(BF16)
    return _dot(hi, m) + _dot(lo, m)


def _dot_exact_lhs(m, x):
    hi, mid, lo = _split3(x)
    return _dot(m, hi) + _dot(m, mid) + _dot(m, lo)


def _silu(x):
    hx = 0.5 * x
    return hx * jnp.tanh(hx) + hx


def _multiplicity(d):
    w = jnp.zeros(d.shape, F32)
    for window, dil in DILATED_BRANCHES:
        hit = (d >= 0) & (d <= window) & (lax.rem(d, dil) == 0)
        w = w + jnp.where(hit, 1.0, 0.0)
    return w


def _norm_rope(y, nw, cos, sa, sb, scale):
    lane = lax.broadcasted_iota(jnp.int32, (1, LANES), 1)
    first = lane < HEAD_DIM
    y2 = y * y
    s_lo = jnp.sum(jnp.where(first, y2, 0.0), axis=-1, keepdims=True)
    s_hi = jnp.sum(jnp.where(first, 0.0, y2), axis=-1, keepdims=True)
    ms = jnp.where(first, s_lo, s_hi) * (1.0 / HEAD_DIM)
    yn = y * lax.rsqrt(ms + EPS) * nw
    half = ROT_DIM // 2
    rot = yn * cos + pltpu.roll(yn, half, 1) * sa + pltpu.roll(yn, LANES - half, 1) * sb
    return rot * scale


def _causal_conv_silu(load, cw_ref, cb_ref, store):
    for c0 in range(0, CONV_DIM, CONV_COLS):
        cs = slice(c0, c0 + CONV_COLS)
        xe = load(cs)
        x2 = pltpu.roll(xe, 2, 0)
        even = cw_ref[3:4, cs] * xe + cw_ref[1:2, cs] * x2
        odd = cw_ref[2:3, cs] * xe + cw_ref[0:1, cs] * x2
        store(cs, _silu((even + pltpu.roll(odd, 1, 0))[SUBLANES:, :] + cb_ref[:, cs]))


def _pre_norm(x_ref, nw_ref):
    x = x_ref[...]
    ms = jnp.mean(x * x, axis=-1, keepdims=True)
    return (x * lax.rsqrt(ms + EPS) * nw_ref[...]).astype(BF16)


def _inproj_kernel(x_ref, nw_ref, w_ref, *refs):
    hn = _pre_norm(x_ref, nw_ref)

    def proj(a, b):
        return _dot_nt(hn, w_ref[a:b, :])

    _emit_sections(proj, x_ref.shape[0], *refs)


def _emit_sections(proj, tm, qnw_ref, knw_ref, cos_ref, sa_ref, sb_ref,
                   zs_ref, xbc_ref, dt_ref, q_ref, kt_ref, vt_ref, gs_ref, ctail_ref):
    cos, sa, sb = cos_ref[...], sa_ref[...], sb_ref[...]
    q = proj(Q0, K0)
    for c in range(ATT_INNER // LANES):
        sl = slice(c * LANES, (c + 1) * LANES)
        qn = _norm_rope(q[:, sl], qnw_ref[...], cos, sa, sb, Q_SCALE)
        if len(q_ref.shape) == 3:
            q_ref[0, sl, :] = qn.T.astype(BF16)
        else:
            q_ref[:, sl] = qn.astype(BF16)
    k = proj(K0, V0)
    kn = [_norm_rope(k[:, c * LANES:(c + 1) * LANES], knw_ref[...], cos, sa, sb, 1.0)
          for c in range(KV_DIM // LANES)]
    kt_ref[0] = jnp.concatenate(kn, axis=1).T

    xbc = proj(X0, DT0)
    xbc_ref[...] = xbc
    ctail_ref[0] = xbc[tm - SUBLANES:tm, :]
    vt_ref[0] = proj(V0, G0).T
    zs_ref[...] = _silu(proj(Z0, X0))
    gs = _silu(proj(G0, W_ROWS))
    if len(gs_ref.shape) == 3:
        gs_ref[0] = gs.T
    else:
        gs_ref[...] = gs
    dt_ref[...] = proj(DT0, DT0 + LANES)


def _in_proj(x2d, nw, w_t, qnw, knw, tables, tm, seq):
    rows = x2d.shape[0]
    const = lambda i: (0, 0)
    row = lambda i: (i, 0)
    sec_in, out_specs, out_shape = _section_specs(rows, tm, seq, tables[0].shape[0] // tm, True)
    return pl.pallas_call(
        _inproj_kernel,
        grid=(rows // tm,),
        in_specs=[
            pl.BlockSpec((tm, D_MODEL), row),
            pl.BlockSpec((1, D_MODEL), const),
            pl.BlockSpec((W_ROWS, D_MODEL), const, pipeline_mode=pl.Buffered(1)),
        ] + sec_in,
        out_specs=out_specs,
        out_shape=out_shape,
        compiler_params=pltpu.CompilerParams(
            dimension_semantics=("arbitrary",), vmem_limit_bytes=VMEM_LIMIT),
        name="in_proj",
    )(x2d, nw, w_t, qnw, knw, *tables)


def _section_specs(rows, tm, seq, period, q_transposed):
    per_seq = seq // tm
    const = lambda i: (0, 0)
    row = lambda i: (i, 0)
    tab = lambda i: (i % period, 0)
    tr = lambda i: (i // per_seq, 0, i % per_seq)
    slab = lambda i: (i // per_seq, 0, 0)
    in_specs = [pl.BlockSpec((1, LANES), const)] * 2 + [pl.BlockSpec((tm, LANES), tab)] * 3
    row_outs = {0: (SSD_INNER, F32), 1: (CONV_DIM, F32), 2: (LANES, F32), 3: (ATT_INNER, BF16),
                6: (ATT_INNER, F32)}
    out_specs, out_shape = [], []
    for idx in range(7):
        if idx in (3, 6) and q_transposed:
            dt = row_outs[idx][1]
            out_specs.append(pl.BlockSpec((1, ATT_INNER, tm), tr))
            out_shape.append(jax.ShapeDtypeStruct((rows // seq, ATT_INNER, seq), dt))
        elif idx in row_outs:
            n, dt = row_outs[idx]
            out_specs.append(pl.BlockSpec((tm, n), row))
            out_shape.append(jax.ShapeDtypeStruct((rows, n), dt))
        else:
            out_specs.append(pl.BlockSpec((1, KV_DIM, tm), tr))
            out_shape.append(jax.ShapeDtypeStruct((rows // seq, KV_DIM, seq), F32))
    out_specs.append(pl.BlockSpec((1, SUBLANES, CONV_DIM), slab))
    out_shape.append(jax.ShapeDtypeStruct((rows // seq, SUBLANES, CONV_DIM), F32))
    return in_specs, out_specs, out_shape


def _wconv_in_kernel(x_ref, nw_ref, w_ref, *refs):
    *sec_refs, wb_ref, xpad_scr, hn_scr, p_scr, tail_scr = refs
    j = pl.program_id(0)

    @pl.when(j == 0)
    def _():
        bs, ts, _ = x_ref.shape
        xpad_scr[...] = jnp.zeros(xpad_scr.shape, F32)
        for b in range(bs):
            xpad_scr[b * SAMPLE_PAD:b * SAMPLE_PAD + ts, :] = x_ref[b]
        hn_scr[...] = _pre_norm(xpad_scr, nw_ref)

    row = j * W_CHUNK + lax.broadcasted_iota(jnp.int32, (W_CHUNK, 1), 0)
    wb = jnp.where(row < W_ROWS, w_ref[...], 0.0).astype(BF16)
    wb_ref[...] = wb
    p_scr[j] = _dot_nt(hn_scr[...], wb)

    @pl.when(j == pl.num_programs(0) - 1)
    def _():
        pfull = jnp.concatenate([p_scr[c] for c in range(p_scr.shape[0])], axis=1)
        tail_scr[...] = pfull[:, Q0:W_ROWS]

        def proj(a, b):
            if a >= Q0:
                return tail_scr[:, a - Q0:b - Q0]
            return pfull[:, a:b]

        _emit_sections(proj, p_scr.shape[1], *sec_refs)


def _wconv_in(x_s, nw, w_f32, qnw, knw, tables):
    rows = x_s.shape[0] * SAMPLE_PAD
    n_chunks = pl.cdiv(W_ROWS, W_CHUNK)
    const = lambda j: (0, 0)
    sec_in, sec_out, sec_shape = _section_specs(rows, rows, rows, 1, False)
    resident = lambda spec: pl.BlockSpec(
        spec.block_shape, lambda j, n=len(spec.block_shape): (0,) * n)
    outs = pl.pallas_call(
        _wconv_in_kernel,
        grid=(n_chunks,),
        in_specs=[
            pl.BlockSpec(x_s.shape, lambda j: (0, 0, 0)),
            pl.BlockSpec((1, D_MODEL), const),
            pl.BlockSpec((W_CHUNK, D_MODEL), lambda j: (j, 0)),
        ] + [resident(s) for s in sec_in],
        out_specs=[resident(s) for s in sec_out] + [pl.BlockSpec((W_CHUNK, D_MODEL), lambda j: (j, 0))],
        out_shape=sec_shape + [jax.ShapeDtypeStruct((W_ROWS, D_MODEL), BF16)],
        scratch_shapes=[
            pltpu.VMEM((rows, D_MODEL), F32),
            pltpu.VMEM((rows, D_MODEL), BF16),
            pltpu.VMEM((n_chunks, rows, W_CHUNK), F32),
            pltpu.VMEM((rows, W_ROWS - Q0), F32),
        ],
        compiler_params=pltpu.CompilerParams(
            dimension_semantics=("arbitrary",), vmem_limit_bytes=VMEM_LIMIT),
        name="wconv_in",
    )(x_s, nw, w_f32, qnw, knw, *tables)
    return outs[-1], outs[:-1]


def _ssd_chain(s, xbc_ref, dtr_ref, z_ref, cw_ref, cb_ref, dtb_ref, alog_ref, dexp_ref, y_ref,
               xext, ht, xc_scr, lin, valid):
    L = lin

    lane = lax.broadcasted_iota(jnp.int32, (1, LANES), 1)
    rowid = lax.broadcasted_iota(jnp.int32, (L, 1), 0)
    dt = jax.nn.softplus(dtr_ref[s] + dtb_ref[...])
    dt = jnp.where((lane < SSD_HEADS) & (rowid < valid), dt, 0.0)
    dta = dt * (-jnp.exp(alog_ref[...]) * LOG2E)

    r2 = lax.broadcasted_iota(jnp.int32, (L, L), 0)
    c2 = lax.broadcasted_iota(jnp.int32, (L, L), 1)
    tri = r2 >= c2
    tri_b = jnp.where(tri, 1.0, 0.0).astype(BF16)
    cum = _dot_exact_lhs(tri_b, dta)
    cum_last = cum[L - 1:L, :]
    yield

    er = lax.broadcasted_iota(jnp.int32, (LANES, SSD_INNER), 0)
    ec = lax.broadcasted_iota(jnp.int32, (LANES, SSD_INNER), 1)
    expand = jnp.where(ec // SSD_HEADDIM == er, 1.0, 0.0).astype(BF16)
    ecum = _dot_wide_rhs(jnp.exp2(cum), expand)
    wexp = _dot_wide_rhs(jnp.exp2(cum_last - cum) * dt, expand)
    cd = _dot_exact_rhs(jnp.broadcast_to(jnp.exp2(cum_last), (SUBLANES, LANES)), expand)[0:1, :]
    cum_t = cum.T
    dt_t = dt.T
    yield

    xext[s, SUBLANES:SUBLANES + L, :] = xbc_ref[s]

    def load(cs):
        return xext[s, 0:SUBLANES + L, cs]

    def store(cs, v):
        xc_scr[s, :, cs] = v

    _causal_conv_silu(load, cw_ref, cb_ref, store)
    xext[s, 0:SUBLANES, :] = xext[s, L:L + SUBLANES, :]
    xc = xc_scr.at[s]
    xs = xc[:, :SSD_INNER]
    bm = xc[:, SSD_INNER:SSD_INNER + SSD_GROUPS * SSD_STATE]
    cm = xc[:, SSD_INNER + SSD_GROUPS * SSD_STATE:]
    bm_b = bm.astype(BF16)
    cm_b = cm.astype(BF16)
    xs_b = xs.astype(BF16)

    y_off = jnp.concatenate(
        [_dot(cm_b[:, g * SSD_STATE:(g + 1) * SSD_STATE], ht[s, g].astype(BF16))
         for g in range(SSD_GROUPS)], axis=1) * ecum
    cbs = [_dot_nt(cm_b[:, g * SSD_STATE:(g + 1) * SSD_STATE], bm_b[:, g * SSD_STATE:(g + 1) * SSD_STATE])
           for g in range(SSD_GROUPS)]
    yield

    lane_half = lane // HEAD_DIM
    heads_per_group = SSD_HEADS // SSD_GROUPS
    y_parts = []
    for g in range(SSD_GROUPS):
        for pr in range(heads_per_group // 2):
            col0 = g * HALF_INNER + pr * LANES
            xp = xs_b[:, col0:col0 + LANES]
            yp = jnp.zeros((L, LANES), F32)
            for e in range(2):
                hh = g * heads_per_group + pr * 2 + e
                decay = jnp.exp2(cum[:, hh:hh + 1] - cum_t[hh:hh + 1, :])
                sc = jnp.where(tri, cbs[g] * decay, 0.0) * dt_t[hh:hh + 1, :]
                xm = jnp.where(lane_half == e, xp, jnp.zeros_like(xp))
                yp = yp + _dot(sc.astype(BF16), xm)
            y_parts.append(yp)
            if pr % 2 == 1:
                yield
    y_diag = jnp.concatenate(y_parts, axis=1)

    xw = (xs * wexp).astype(BF16)
    for g in range(SSD_GROUPS):
        sl = slice(g * HALF_INNER, (g + 1) * HALF_INNER)
        bm_t = bm[:, g * SSD_STATE:(g + 1) * SSD_STATE].T.astype(BF16)
        ht[s, g] = ht[s, g] * cd[:, sl] + _dot(bm_t, xw[:, sl])
    yield

    y_ref[s] = (y_diag + y_off + dexp_ref[...] * xs) * z_ref[s]


def _ssd_kernel(xbc_ref, dtr_ref, z_ref, *refs, lin, valid):
    *start_refs, cw_ref, cb_ref, dtb_ref, alog_ref, dexp_ref, y_ref, hout_ref, xext, ht, xc_scr = refs
    c = pl.program_id(1)

    def for_each_state(fn):
        def step(j, carry):
            g = j % SSD_GROUPS
            fn(j // SSD_GROUPS, g, pl.multiple_of(g * HALF_INNER, HALF_INNER))
            return carry

        lax.fori_loop(0, SSD_SEQS * SSD_GROUPS, step, 0, unroll=2)

    @pl.when(c == 0)
    def _():
        if not start_refs:
            xext[:, 0:SUBLANES, :] = jnp.zeros((SSD_SEQS, SUBLANES, CONV_DIM), F32)
            ht[...] = jnp.zeros(ht.shape, F32)
            return
        cprev_ref, h0_ref = start_refs
        for s in range(SSD_SEQS):
            xext[s, 0:SUBLANES, :] = cprev_ref[s]

        def load_state(s, g, r0):
            ht[s, g] = h0_ref[s, pl.ds(r0, HALF_INNER), :].T

        for_each_state(load_state)

    chains = [_ssd_chain(s, xbc_ref, dtr_ref, z_ref, cw_ref, cb_ref, dtb_ref, alog_ref, dexp_ref, y_ref,
                         xext, ht, xc_scr, lin, valid) for s in range(SSD_SEQS)]
    while chains:
        alive = []
        for ch in chains:
            try:
                next(ch)
                alive.append(ch)
            except StopIteration:
                pass
        chains = alive

    @pl.when(c == pl.num_programs(1) - 1)
    def _():
        def store_state(s, g, r0):
            hout_ref[s, pl.ds(r0, HALF_INNER), :] = ht[s, g].T

        for_each_state(store_state)


def _ssd(xbc, dtr, z, start, cw, cb, dtb, alog, dexp, batch, lin, valid):
    rows = xbc.shape[0]
    seq = rows // batch
    nc = seq // lin
    g = SSD_SEQS
    assert batch % g == 0
    blk = lambda b, c: (b, c, 0)
    per_b = lambda b, c: (b, 0, 0)
    const = lambda b, c: (0, 0)
    start_specs = [pl.BlockSpec((g, SUBLANES, CONV_DIM), per_b),
                   pl.BlockSpec((g, SSD_INNER, SSD_STATE), per_b)] if start else []
    y, hout = pl.pallas_call(
        functools.partial(_ssd_kernel, lin=lin, valid=valid),
        grid=(batch // g, nc),
        in_specs=[
            pl.BlockSpec((g, lin, CONV_DIM), blk),
            pl.BlockSpec((g, lin, LANES), blk),
            pl.BlockSpec((g, lin, SSD_INNER), blk),
        ] + start_specs + [
            pl.BlockSpec((SUBLANES, CONV_DIM), const),
            pl.BlockSpec((1, CONV_DIM), const),
            pl.BlockSpec((1, LANES), const),
            pl.BlockSpec((1, LANES), const),
            pl.BlockSpec((1, SSD_INNER), const),
        ],
        out_specs=[
            pl.BlockSpec((g, lin, SSD_INNER), blk),
            pl.BlockSpec((g, SSD_INNER, SSD_STATE), per_b),
        ],
        out_shape=[
            jax.ShapeDtypeStruct((batch, seq, SSD_INNER), F32),
            jax.ShapeDtypeStruct((batch, SSD_INNER, SSD_STATE), F32),
        ],
        scratch_shapes=[
            pltpu.VMEM((g, SUBLANES + lin + SUBLANES, CONV_DIM), F32),
            pltpu.VMEM((g, SSD_GROUPS, SSD_STATE, HALF_INNER), F32),
            pltpu.VMEM((g, lin, CONV_DIM), F32),
        ],
        compiler_params=pltpu.CompilerParams(
            dimension_semantics=("arbitrary", "arbitrary"), vmem_limit_bytes=VMEM_LIMIT),
        name="ssd",
    )(xbc.reshape(batch, seq, CONV_DIM), dtr.reshape(batch, seq, LANES),
      z.reshape(batch, seq, SSD_INNER), *start, cw, cb, dtb, alog, dexp)
    return y.reshape(rows, SSD_INNER), hout


def _attn_prompt_kernel(q_ref, kt_ref, vt_ref, g_ref, o_ref,
                        bias_scr, kh_scr, va_scr, qt_scr, m_scr, acc_scr, s_scr, s2_scr, *, n_tab, n_sb_max):
    b = pl.program_id(0)
    i = pl.program_id(1)
    seq = kt_ref.shape[2]
    pair_w = 2 * Q_BLOCK
    pairs_per_kv = ATT_GQ // 2

    @pl.when((b == 0) & (i == 0))
    def _():
        r = lax.broadcasted_iota(jnp.int32, (K_SUPER, Q_BLOCK), 0)
        c = lax.broadcasted_iota(jnp.int32, (K_SUPER, Q_BLOCK), 1)
        def table(tb, carry):
            w = _multiplicity(tb * Q_BLOCK + c - r)
            bias_scr[tb] = jnp.where(w > 0.0, jnp.log2(jnp.maximum(w, 1.0)), NEG)
            return carry

        lax.fori_loop(0, n_tab, table, 0)
        va_scr[:, HEAD_DIM:, :] = jnp.ones((ATT_KV_HEADS, ONES_ROWS, seq), BF16)

    @pl.when(i == 0)
    def _():
        for kvh in range(ATT_KV_HEADS):
            hs = slice(kvh * HEAD_DIM, (kvh + 1) * HEAD_DIM)
            va_scr[kvh, 0:HEAD_DIM, :] = vt_ref[0, hs, :].astype(BF16)
        for cidx in range(seq // K_SUPER):
            rows = slice(cidx * K_SUPER, (cidx + 1) * K_SUPER)
            kc = kt_ref[0, :, rows].T
            for kvh in range(ATT_KV_HEADS):
                hs = slice(kvh * HEAD_DIM, (kvh + 1) * HEAD_DIM)
                kh_scr[kvh, rows, :] = kc[:, hs].astype(BF16)

    for kvh in range(ATT_KV_HEADS):
        qt_scr[kvh] = jnp.concatenate(
            [q_ref[0, (kvh * ATT_GQ + g) * HEAD_DIM:(kvh * ATT_GQ + g + 1) * HEAD_DIM, :]
             for g in range(ATT_GQ)], axis=1)
    m_scr[...] = jnp.full(m_scr.shape, NEG, F32)
    acc_scr[...] = jnp.zeros(acc_scr.shape, F32)
    par = i % Q_PER_SUPER
    j_last = i // Q_PER_SUPER
    n_sb = jnp.minimum(j_last + 1, n_sb_max)

    def key_start(dl):
        return pl.multiple_of((j_last - dl) * K_SUPER, K_SUPER)

    def scores(dl, dst):
        for kvh in range(ATT_KV_HEADS):
            dst[kvh] = _dot(kh_scr[kvh, pl.ds(key_start(dl), K_SUPER), :], qt_scr[kvh])

    def softmax_pv(dl, src):
        start = key_start(dl)
        bias = bias_scr[par + Q_PER_SUPER * dl]
        bias2 = jnp.concatenate([bias, bias], axis=1)
        for kvh in range(ATT_KV_HEADS):
            vtb = va_scr[kvh, :, pl.ds(start, K_SUPER)]
            for pr in range(pairs_per_kv):
                u = kvh * pairs_per_kv + pr
                s = src[kvh, :, pr * pair_w:(pr + 1) * pair_w] + bias2
                m_old = m_scr[u, 0:1, :]
                m_new = jnp.maximum(m_old, jnp.max(s, axis=0, keepdims=True))
                p = jnp.exp2(s - m_new).astype(BF16)
                acc_scr[u] = jnp.exp2(m_old - m_new) * acc_scr[u] + _dot(vtb, p)
                m_scr[u, 0:1, :] = m_new

    def diagonal_block():
        half = K_SUPER // 2
        start = key_start(0)
        bias = bias_scr[0]
        for kvh in range(ATT_KV_HEADS):
            s_scr[kvh, 0:half, :] = _dot(kh_scr[kvh, pl.ds(start, half), :], qt_scr[kvh])
        for kvh in range(ATT_KV_HEADS):
            q_late = jnp.concatenate(
                [qt_scr[kvh, :, g * Q_BLOCK + half:(g + 1) * Q_BLOCK] for g in range(ATT_GQ)], axis=1)
            s2_scr[kvh, 0:half, 0:ATT_GQ * half] = _dot(
                kh_scr[kvh, pl.ds(start + half, half), :], q_late)
        bias_a = jnp.concatenate([bias[0:half, :]] * 2, axis=1)
        for kvh in range(ATT_KV_HEADS):
            vtb = va_scr[kvh, :, pl.ds(start, half)]
            for pr in range(pairs_per_kv):
                u = kvh * pairs_per_kv + pr
                s = s_scr[kvh, 0:half, pr * pair_w:(pr + 1) * pair_w] + bias_a
                m_old = m_scr[u, 0:1, :]
                m_new = jnp.maximum(m_old, jnp.max(s, axis=0, keepdims=True))
                p = jnp.exp2(s - m_new).astype(BF16)
                acc_scr[u] = jnp.exp2(m_old - m_new) * acc_scr[u] + _dot(vtb, p)
                m_scr[u, 0:1, :] = m_new
        bias_b = jnp.concatenate([bias[half:, half:]] * 2, axis=1)
        late = [slice(e * Q_BLOCK + half, (e + 1) * Q_BLOCK) for e in range(2)]
        for kvh in range(ATT_KV_HEADS):
            vtb = va_scr[kvh, :, pl.ds(start + half, half)]
            for pr in range(pairs_per_kv):
                u = kvh * pairs_per_kv + pr
                s = s2_scr[kvh, 0:half, pr * Q_BLOCK:(pr + 1) * Q_BLOCK] + bias_b
                m_old = jnp.concatenate([m_scr[u, 0:1, sl] for sl in late], axis=1)
                m_new = jnp.maximum(m_old, jnp.max(s, axis=0, keepdims=True))
                p = jnp.exp2(s - m_new).astype(BF16)
                alpha = jnp.exp2(m_old - m_new)
                pv = _dot(vtb, p)
                for e, sl in enumerate(late):
                    es = slice(e * half, (e + 1) * half)
                    acc_scr[u, :, sl] = alpha[:, es] * acc_scr[u, :, sl] + pv[:, es]
                    m_scr[u, 0:1, sl] = m_new[:, es]

    diagonal_block()
    n_rest = n_sb - 1

    def body(t, carry):
        scores(1 + 2 * t, s_scr)
        scores(2 + 2 * t, s2_scr)
        softmax_pv(1 + 2 * t, s_scr)
        softmax_pv(2 + 2 * t, s2_scr)
        return carry

    lax.fori_loop(0, n_rest // 2, body, 0)

    @pl.when(n_rest % 2 == 1)
    def _():
        scores(n_sb - 1, s_scr)
        softmax_pv(n_sb - 1, s_scr)

    for u in range(ATT_HEADS // 2):
        acc = acc_scr[u]
        o = acc[0:HEAD_DIM] * (1.0 / acc[HEAD_DIM:HEAD_DIM + 1])
        for e in range(2):
            hs = slice((2 * u + e) * HEAD_DIM, (2 * u + e + 1) * HEAD_DIM)
            o_ref[0, hs, :] = (o[:, e * Q_BLOCK:(e + 1) * Q_BLOCK] * g_ref[0, hs, :]).astype(BF16)


def _attn_prompt(qt, kt, vt, gate_t, batch, seq):
    nqb = seq // Q_BLOCK
    assert seq % K_SUPER == 0 and Q_BLOCK == K_SUPER
    n_sb_max = min(seq // K_SUPER, W_MAX // K_SUPER + 1)
    n_tab = Q_PER_SUPER * n_sb_max
    qblk = lambda b, i: (b, 0, i)
    per_b = lambda b, i: (b, 0, 0)
    return pl.pallas_call(
        functools.partial(_attn_prompt_kernel, n_tab=n_tab, n_sb_max=n_sb_max),
        grid=(batch, nqb),
        in_specs=[
            pl.BlockSpec((1, ATT_INNER, Q_BLOCK), qblk),
            pl.BlockSpec((1, KV_DIM, seq), per_b),
            pl.BlockSpec((1, KV_DIM, seq), per_b),
            pl.BlockSpec((1, ATT_INNER, Q_BLOCK), qblk),
        ],
        out_specs=pl.BlockSpec((1, ATT_INNER, Q_BLOCK), qblk),
        out_shape=jax.ShapeDtypeStruct((batch, ATT_INNER, seq), BF16),
        scratch_shapes=[
            pltpu.VMEM((n_tab, K_SUPER, Q_BLOCK), F32),
            pltpu.VMEM((ATT_KV_HEADS, seq, HEAD_DIM), BF16),
            pltpu.VMEM((ATT_KV_HEADS, ACC_ROWS, seq), BF16),
            pltpu.VMEM((ATT_KV_HEADS, HEAD_DIM, ATT_GQ * Q_BLOCK), BF16),
            pltpu.VMEM((ATT_HEADS // 2, SUBLANES, 2 * Q_BLOCK), F32),
            pltpu.VMEM((ATT_HEADS // 2, ACC_ROWS, 2 * Q_BLOCK), F32),
            pltpu.VMEM((ATT_KV_HEADS, K_SUPER, ATT_GQ * Q_BLOCK), F32),
            pltpu.VMEM((ATT_KV_HEADS, K_SUPER, ATT_GQ * Q_BLOCK), F32),
        ],
        compiler_params=pltpu.CompilerParams(
            dimension_semantics=("arbitrary", "arbitrary"), vmem_limit_bytes=VMEM_LIMIT),
        name="attn_prompt",
    )(qt, kt, vt, gate_t)


def _attn_sample_kernel(q_ref, knt_ref, vnt_ref, g_ref, ckt_ref, cvt_ref, o_ref, kot_ref, vot_ref, *, ts):
    win = ckt_ref.shape[2]
    tp = SAMPLE_PAD
    rows = ATT_GQ * tp
    t_c = lax.broadcasted_iota(jnp.int32, (rows, win), 0) % tp
    j_c = lax.broadcasted_iota(jnp.int32, (rows, win), 1)
    w_c = _multiplicity(win + t_c - j_c)
    t_n = lax.broadcasted_iota(jnp.int32, (rows, tp), 0) % tp
    j_n = lax.broadcasted_iota(jnp.int32, (rows, tp), 1)
    w_n = jnp.where(j_n < ts, _multiplicity(t_n - j_n), 0.0)
    for kvh in range(ATT_KV_HEADS):
        c0 = kvh * ATT_GQ * HEAD_DIM
        qh = jnp.concatenate(
            [q_ref[:, c0 + g * HEAD_DIM:c0 + (g + 1) * HEAD_DIM] for g in range(ATT_GQ)], axis=0)
        hs = slice(kvh * HEAD_DIM, (kvh + 1) * HEAD_DIM)
        s_c = jnp.where(w_c > 0.0, _dot(qh, ckt_ref[0, hs, :].astype(BF16)), NEG)
        s_n = jnp.where(w_n > 0.0, _dot(qh, knt_ref[0, hs, :].astype(BF16)), NEG)
        m = jnp.maximum(jnp.max(s_c, axis=-1, keepdims=True), jnp.max(s_n, axis=-1, keepdims=True))
        p_c = jnp.exp2(s_c - m) * w_c
        p_n = jnp.exp2(s_n - m) * w_n
        den = jnp.sum(p_c, axis=-1, keepdims=True) + jnp.sum(p_n, axis=-1, keepdims=True)
        num = (_dot_nt(p_c.astype(BF16), cvt_ref[0, hs, :].astype(BF16))
               + _dot_nt(p_n.astype(BF16), vnt_ref[0, hs, :].astype(BF16)))
        o = num / den
        o = jnp.concatenate([o[g * tp:(g + 1) * tp] for g in range(ATT_GQ)], axis=1)
        csl = slice(c0, c0 + ATT_GQ * HEAD_DIM)
        o_ref[:, csl] = (o * g_ref[:, csl]).astype(BF16)
    kot_ref[0, :, 0:win - ts] = ckt_ref[0, :, ts:win]
    kot_ref[0, :, win - ts:win] = knt_ref[0, :, 0:ts]
    vot_ref[0, :, 0:win - ts] = cvt_ref[0, :, ts:win]
    vot_ref[0, :, win - ts:win] = vnt_ref[0, :, 0:ts]


def _attn_sample(q, knt, vnt, gate, ckt, cvt, ts):
    batch, _, win = ckt.shape
    tp = SAMPLE_PAD
    blk = lambda b: (b, 0)
    per_b = lambda b: (b, 0, 0)
    return pl.pallas_call(
        functools.partial(_attn_sample_kernel, ts=ts),
        grid=(batch,),
        in_specs=[
            pl.BlockSpec((tp, ATT_INNER), blk),
            pl.BlockSpec((1, KV_DIM, tp), per_b),
            pl.BlockSpec((1, KV_DIM, tp), per_b),
            pl.BlockSpec((tp, ATT_INNER), blk),
            pl.BlockSpec((1, KV_DIM, win), per_b),
            pl.BlockSpec((1, KV_DIM, win), per_b),
        ],
        out_specs=[
            pl.BlockSpec((tp, ATT_INNER), blk),
            pl.BlockSpec((1, KV_DIM, win), per_b),
            pl.BlockSpec((1, KV_DIM, win), per_b),
        ],
        out_shape=[
            jax.ShapeDtypeStruct((batch * tp, ATT_INNER), BF16),
            jax.ShapeDtypeStruct((batch, KV_DIM, win), F32),
            jax.ShapeDtypeStruct((batch, KV_DIM, win), F32),
        ],
        compiler_params=pltpu.CompilerParams(
            dimension_semantics=("arbitrary",), vmem_limit_bytes=VMEM_LIMIT),
        name="attn_sample",
    )(q, knt, vnt, gate, ckt, cvt)


def _group_norm(yg, nw):
    ms = jnp.mean(yg * yg, axis=-1, keepdims=True)
    return (yg * lax.rsqrt(ms + EPS) * nw).astype(BF16)


def _outproj_kernel(x_ref, ys_ref, yat_ref, nw_ref, w_ref, o_ref):
    acc = x_ref[...]
    for g in range(SSD_GROUPS):
        gs = slice(g * HALF_INNER, (g + 1) * HALF_INNER)
        acc = acc + _dot(_group_norm(ys_ref[:, gs], nw_ref[:, gs]), w_ref[gs, :])
    ya = yat_ref[0].astype(F32).T.astype(BF16)
    o_ref[...] = acc + _dot(ya, w_ref[SSD_INNER:, :])


def _out_proj(x2d, y_ssd, y_att_t, nw, w_o, tm):
    rows = x2d.shape[0]
    per_seq = y_att_t.shape[2] // tm
    row = lambda i: (i, 0)
    const = lambda i: (0, 0)
    return pl.pallas_call(
        _outproj_kernel,
        grid=(rows // tm,),
        in_specs=[
            pl.BlockSpec((tm, D_MODEL), row),
            pl.BlockSpec((tm, SSD_INNER), row),
            pl.BlockSpec((1, ATT_INNER, tm), lambda i: (i // per_seq, 0, i % per_seq)),
            pl.BlockSpec((1, SSD_INNER), const),
            pl.BlockSpec((SSD_INNER + ATT_INNER, D_MODEL), const, pipeline_mode=pl.Buffered(1)),
        ],
        out_specs=pl.BlockSpec((tm, D_MODEL), row),
        out_shape=jax.ShapeDtypeStruct((rows, D_MODEL), F32),
        compiler_params=pltpu.CompilerParams(
            dimension_semantics=("arbitrary",), vmem_limit_bytes=VMEM_LIMIT),
        name="out_proj",
    )(x2d, y_ssd, y_att_t, nw, w_o)


def _wconv_out_kernel(x_ref, ys_ref, ya_ref, nw_ref, w_ref, wb_ref, o_ref):
    j = pl.program_id(0)
    wb = w_ref[...].astype(BF16)
    wb_ref[...] = wb
    y = jnp.where(j < SSD_GROUPS, _group_norm(ys_ref[...], nw_ref[...]), ya_ref[...])
    part = _dot(y, wb)
    bs, ts, _ = x_ref.shape

    @pl.when(j == 0)
    def _():
        for b in range(bs):
            o_ref[b] = x_ref[b] + part[b * SAMPLE_PAD:b * SAMPLE_PAD + ts]

    @pl.when(j > 0)
    def _():
        for b in range(bs):
            o_ref[b] += part[b * SAMPLE_PAD:b * SAMPLE_PAD + ts]


def _wconv_out(x_s, y_ssd, y_att, nw, w_f32):
    rows = x_s.shape[0] * SAMPLE_PAD
    n_ssd = SSD_GROUPS
    n_chunks = (SSD_INNER + ATT_INNER) // WO_CHUNK
    const = lambda j: (0, 0)
    ssd_chunk = lambda j: (0, jnp.minimum(j, n_ssd - 1))
    return pl.pallas_call(
        _wconv_out_kernel,
        grid=(n_chunks,),
        in_specs=[
            pl.BlockSpec(x_s.shape, lambda j: (0, 0, 0)),
            pl.BlockSpec((rows, WO_CHUNK), ssd_chunk),
            pl.BlockSpec((rows, WO_CHUNK), lambda j: (0, jnp.maximum(j - n_ssd, 0))),
            pl.BlockSpec((1, WO_CHUNK), ssd_chunk),
            pl.BlockSpec((WO_CHUNK, D_MODEL), lambda j: (j, 0)),
        ],
        out_specs=[
            pl.BlockSpec((WO_CHUNK, D_MODEL), lambda j: (j, 0)),
            pl.BlockSpec(x_s.shape, lambda j: (0, 0, 0)),
        ],
        out_shape=[
            jax.ShapeDtypeStruct((SSD_INNER + ATT_INNER, D_MODEL), BF16),
            jax.ShapeDtypeStruct(x_s.shape, F32),
        ],
        compiler_params=pltpu.CompilerParams(
            dimension_semantics=("arbitrary",), vmem_limit_bytes=VMEM_LIMIT),
        name="wconv_out",
    )(x_s, y_ssd, y_att, nw, w_f32)


def _rope_tables(pos):
    n = pos.shape[0]
    half = ROT_DIM // 2
    inv = ROPE_THETA ** (-np.arange(0, ROT_DIM, 2, dtype=np.float64) / ROT_DIM)
    ang = pos.astype(np.float64)[:, None] * inv[None, :]
    cos, sin = np.cos(ang), np.sin(ang)
    rest = HEAD_DIM - ROT_DIM
    zh = np.zeros((n, half))
    cos_h = np.concatenate([cos, cos, np.ones((n, rest))], axis=1)
    sa_h = np.concatenate([zh, sin, np.zeros((n, rest))], axis=1)
    sb_h = np.concatenate([-sin, zh, np.zeros((n, rest))], axis=1)
    rep = LANES // HEAD_DIM
    return tuple(jnp.asarray(np.tile(t, (1, rep)), dtype=F32) for t in (cos_h, sa_h, sb_h))


def _lane_pad(v, n=LANES):
    return jnp.pad(v, (0, n - v.shape[0])).reshape(1, n)


def _to_cache(xt, batch, seq):
    return xt.reshape(1, batch, ATT_KV_HEADS, HEAD_DIM, seq).transpose(0, 1, 4, 2, 3)


def kernel(x_prompt, x_sample, cache_k, cache_v, state_conv, state_ssm, norm_w, w_in, conv_w,
           conv_b, dt_bias, a_log, d_skip, ssd_norm_w, q_norm_w, k_norm_w, w_out):
    bp, tp_, _ = x_prompt.shape
    bs, ts, _ = x_sample.shape
    depth = w_in.shape[0]
    assert depth == 1 and tp_ % SSD_CHUNK == 0 and ts <= SAMPLE_PAD and ts >= CONV_W - 1
    l = 0
    win = cache_k.shape[2]

    nw = norm_w[l].reshape(1, D_MODEL)
    rep = LANES // HEAD_DIM
    qnw = jnp.tile(q_norm_w[l], rep).reshape(1, LANES)
    knw = jnp.tile(k_norm_w[l], rep).reshape(1, LANES)
    cw = jnp.pad(conv_w[l], ((0, SUBLANES - CONV_W), (0, 0)))
    cb = conv_b[l].reshape(1, CONV_DIM)
    dtb = _lane_pad(dt_bias[l])
    alog = _lane_pad(a_log[l])
    dexp = jnp.repeat(d_skip[l], SSD_HEADDIM).reshape(1, SSD_INNER)
    snw = ssd_norm_w[l].reshape(1, SSD_INNER)

    pad = SAMPLE_PAD
    tabs_s = _rope_tables(PAST_LEN + np.arange(bs * pad) % pad)
    w_t, (zs, xbc, dtr, q, kt, vt, gs, _) = _wconv_in(
        x_sample, nw, jnp.swapaxes(w_in[l], 0, 1), qnw, knw, tabs_s)
    cprev = jnp.pad(state_conv[l], ((0, 0), (SUBLANES - (CONV_W - 1), 0), (0, 0)))
    y_ssd, h_s = _ssd(xbc, dtr, zs, (cprev, state_ssm[l].reshape(bs, SSD_INNER, SSD_STATE)),
                      cw, cb, dtb, alog, dexp, bs, pad, ts)
    knt = kt.reshape(KV_DIM, bs, pad).transpose(1, 0, 2)
    vnt = vt.reshape(KV_DIM, bs, pad).transpose(1, 0, 2)
    ckt = cache_k[l].transpose(0, 2, 3, 1).reshape(bs, KV_DIM, win)
    cvt = cache_v[l].transpose(0, 2, 3, 1).reshape(bs, KV_DIM, win)
    y_att, kot, vot = _attn_sample(q, knt, vnt, gs, ckt, cvt, ts)
    w_o, y_s = _wconv_out(x_sample, y_ssd, y_att, snw, w_out[l])
    k_s = _to_cache(kot, bs, win)
    v_s = _to_cache(vot, bs, win)
    c_s = xbc.reshape(bs, pad, CONV_DIM)[:, ts - (CONV_W - 1):ts][None]
    h_s = h_s.reshape(1, bs, SSD_HEADS, SSD_HEADDIM, SSD_STATE)

    tm = TM_IN
    xp2 = x_prompt.reshape(bp * tp_, D_MODEL)
    tabs = _rope_tables(np.arange(tp_))
    zs, xbc, dtr, q, kt, vt, gs, ctail = _in_proj(xp2, nw, w_t, qnw, knw, tabs, tm, tp_)
    y_ssd, h_p = _ssd(xbc, dtr, zs, (), cw, cb, dtb, alog, dexp, bp, SSD_CHUNK, SSD_CHUNK)
    y_att = _attn_prompt(q, kt, vt, gs, bp, tp_)
    y_p = _out_proj(xp2, y_ssd, y_att, snw, w_o, TM_OUT).reshape(bp, tp_, D_MODEL)
    keep = min(W_MAX, tp_)
    k_p = _to_cache(kt, bp, tp_)[:, :, tp_ - keep:]
    v_p = _to_cache(vt, bp, tp_)[:, :, tp_ - keep:]
    c_p = ctail[:, SUBLANES - (CONV_W - 1):][None]
    h_p = h_p.reshape(1, bp, SSD_HEADS, SSD_HEADDIM, SSD_STATE)

    return (y_p, y_s, k_p, v_p, c_p, h_p, k_s, v_s, c_s, h_s)
```
